```python
import math
import jax, jax.numpy as jnp
from jax import lax
import numpy as np

D_MODEL = 1024
BATCH = 8
SEQ = 16384
DEPTH = 1

D_MIX = D_MODEL
ATTN_WIDTH = D_MIX // 2
HEAD_DIM = 64
N_HEADS = ATTN_WIDTH // HEAD_DIM
WINDOWS = (128, 512, 2048)
DILATIONS = (1, 4, 16)
BLOCK = 128
PAD_UNIT = max(DILATIONS) * BLOCK
SSM_WIDTH = D_MIX - ATTN_WIDTH
SSM_GROUP = 16
SSM_GROUPS = SSM_WIDTH // SSM_GROUP
SSM_STATE = 64
D_FF = 4 * D_MODEL
PROJ_WIDTH = 3 * ATTN_WIDTH + SSM_WIDTH
EPS = 1e-6
NEG_INF = -1e30
DT_MIN, DT_MAX = 1e-3, 1e-1

kernel_name = "hymba_longnet_s5_hybrid"


def _rmsnorm(x, g):
    xf = x.astype(jnp.float32)
    y = xf * lax.rsqrt(jnp.mean(xf * xf, axis=-1, keepdims=True) + EPS)
    return (y * g.astype(jnp.float32)).astype(x.dtype)


def _dilated_window(q, k, v, dilation, steps):
    b, sp, h, e = q.shape
    n = sp // dilation
    nb = n // BLOCK

    def to_blocks(t):
        t = t.reshape(b, n, dilation, h, e).transpose(0, 2, 3, 1, 4)
        return t.reshape(b, dilation, h, nb, BLOCK, e)

    qb, kb, vb = to_blocks(q), to_blocks(k), to_blocks(v)

    def with_prev(t):
        prev = jnp.pad(t[:, :, :, :-1], ((0, 0), (0, 0), (0, 0), (1, 0), (0, 0), (0, 0)))
        return jnp.concatenate([prev, t], axis=4)

    kw, vw = with_prev(kb), with_prev(vb)
    s = jnp.einsum('brhnqe,brhnke->brhnqk', qb, kw) * (HEAD_DIM ** -0.5)
    qi = jnp.arange(BLOCK)[:, None] + BLOCK
    ki = jnp.arange(2 * BLOCK)[None, :]
    dist = qi - ki
    band = (dist >= 0) & (dist <= steps)
    valid = (jnp.arange(nb)[:, None, None] * BLOCK - BLOCK + ki[None]) >= 0
    mask = band[None] & valid
    s = jnp.where(mask, s, NEG_INF)
    m = jnp.max(s, axis=-1)
    p = jnp.exp(s - m[..., None])
    l = jnp.sum(p, axis=-1)
    acc = jnp.einsum('brhnqk,brhnke->brhnqe', p, vw)

    def to_seq(t):
        t = t.reshape((b, dilation, h, n) + t.shape[5:])
        t = jnp.moveaxis(t, 3, 1)
        return t.reshape((b, sp, h) + t.shape[4:])

    return to_seq(acc), to_seq(m), to_seq(l)


def _dilated_attention(q, k, v):
    b, s, h, e = q.shape
    pad = (-s) % PAD_UNIT
    qf, kf, vf = [jnp.pad(t.astype(jnp.float32), ((0, 0), (0, pad), (0, 0), (0, 0))) for t in (q, k, v)]
    accs, ms, ls = [], [], []
    for w, d in zip(WINDOWS, DILATIONS):
        acc, m, l = _dilated_window(qf, kf, vf, d, w // d)
        accs.append(acc)
        ms.append(m)
        ls.append(l)
    m_all = jnp.stack(ms)
    wts = jnp.exp(m_all - jnp.max(m_all, axis=0, keepdims=True))
    num = jnp.sum(jnp.stack(accs) * wts[..., None], axis=0)
    den = jnp.sum(jnp.stack(ls) * wts, axis=0)
    out = num / den[..., None]
    return out[:, :s].astype(v.dtype)


def _complex_linear_combine(e1, e2):
    a1r, a1i, b1r, b1i = e1
    a2r, a2i, b2r, b2i = e2
    return (a2r * a1r - a2i * a1i,
            a2r * a1i + a2i * a1r,
            a2r * b1r - a2i * b1i + b2r,
            a2r * b1i + a2i * b1r + b2i)


def _s5_mixer(u, a_re, a_im, log_dt, b_re, b_im, c_re, c_im, d_skip, glu_w, glu_b):
    b, s, _ = u.shape
    f32 = jnp.float32
    uf = u.astype(f32).reshape(b, s, SSM_GROUPS, SSM_GROUP)
    lr, li = a_re.astype(f32), a_im.astype(f32)
    dt = jnp.exp(log_dt.astype(f32))[:, None]
    mag = jnp.exp(lr * dt)
    ab_r, ab_i = mag * jnp.cos(li * dt), mag * jnp.sin(li * dt)
    den = lr * lr + li * li
    nr, ni = ab_r - 1.0, ab_i
    cr = (nr * lr + ni * li) / den
    ci = (ni * lr - nr * li) / den
    br, bi = b_re.astype(f32), b_im.astype(f32)
    bb_r = cr[..., None] * br - ci[..., None] * bi
    bb_i = cr[..., None] * bi + ci[..., None] * br
    bu_r = jnp.einsum('bsgc,gpc->bsgp', uf, bb_r)
    bu_i = jnp.einsum('bsgc,gpc->bsgp', uf, bb_i)
    a_r = jnp.broadcast_to(ab_r, bu_r.shape)
    a_i = jnp.broadcast_to(ab_i, bu_i.shape)
    _, _, xr, xi = lax.associative_scan(_complex_linear_combine, (a_r, a_i, bu_r, bu_i), axis=1)
    y = (jnp.einsum('bsgp,gcp->bsgc', xr, c_re.astype(f32))
         - jnp.einsum('bsgp,gcp->bsgc', xi, c_im.astype(f32))
         + d_skip.astype(f32) * uf)
    y = y.reshape(b, s, SSM_WIDTH)
    z = jax.nn.gelu(y)
    out = z * jax.nn.sigmoid(z @ glu_w.astype(f32) + glu_b.astype(f32))
    return out.astype(u.dtype)


def _hybrid_layer(x, norm1_g, w_in, q_norm_g, k_norm_g, ssm_a_re, ssm_a_im, ssm_log_dt,
                  ssm_b_re, ssm_b_im, ssm_c_re, ssm_c_im, ssm_d, glu_w, glu_b,
                  attn_out_norm_g, ssm_out_norm_g, w_out, norm2_g, w_mlp_up, w_mlp_down):
    b, s, _ = x.shape
    xn = _rmsnorm(x, norm1_g)
    proj = xn @ w_in
    q = proj[..., :ATTN_WIDTH].reshape(b, s, N_HEADS, HEAD_DIM)
    k = proj[..., ATTN_WIDTH:2 * ATTN_WIDTH].reshape(b, s, N_HEADS, HEAD_DIM)
    v = proj[..., 2 * ATTN_WIDTH:3 * ATTN_WIDTH].reshape(b, s, N_HEADS, HEAD_DIM)
    u = proj[..., 3 * ATTN_WIDTH:]
    q = _rmsnorm(q, q_norm_g)
    k = _rmsnorm(k, k_norm_g)
    attn = _dilated_attention(q, k, v).reshape(b, s, ATTN_WIDTH)
    ssm = _s5_mixer(u, ssm_a_re, ssm_a_im, ssm_log_dt, ssm_b_re, ssm_b_im,
                    ssm_c_re, ssm_c_im, ssm_d, glu_w, glu_b)
    mix = jnp.concatenate([_rmsnorm(attn, attn_out_norm_g), _rmsnorm(ssm, ssm_out_norm_g)], axis=-1)
    x = x + mix @ w_out
    hdn = jnp.square(jax.nn.relu(_rmsnorm(x, norm2_g) @ w_mlp_up))
    return x + hdn @ w_mlp_down


def _fwd_setup_inputs(seed: int = 0) -> dict:
    key = jax.random.key(seed)
    ks = jax.random.split(key, 20)
    L, G, P, C = DEPTH, SSM_GROUPS, SSM_STATE, SSM_GROUP
    nrm = lambda k, shape, scale: jax.random.normal(k, shape, jnp.float32) * scale
    x = nrm(ks[0], (BATCH, SEQ, D_MODEL), 1.0)
    norm1_g = 1.0 + nrm(ks[1], (L, D_MODEL), 0.02)
    w_in = nrm(ks[2], (L, D_MODEL, PROJ_WIDTH), D_MODEL ** -0.5)
    q_norm_g = 1.0 + nrm(ks[3], (L, HEAD_DIM), 0.02)
    k_norm_g = 1.0 + nrm(ks[4], (L, HEAD_DIM), 0.02)
    ssm_a_re = -0.5 + nrm(ks[5], (L, G, P), 0.01)
    ssm_a_im = math.pi * jnp.arange(P, dtype=jnp.float32)[None, None, :] + nrm(ks[6], (L, G, P), 0.01)
    ssm_log_dt = jax.random.uniform(ks[7], (L, G), jnp.float32, math.log(DT_MIN), math.log(DT_MAX))
    ssm_b_re = nrm(ks[8], (L, G, P, C), (2 * C) ** -0.5)
    ssm_b_im = nrm(ks[9], (L, G, P, C), (2 * C) ** -0.5)
    ssm_c_re = nrm(ks[10], (L, G, C, P), (2 * P) ** -0.5)
    ssm_c_im = nrm(ks[11], (L, G, C, P), (2 * P) ** -0.5)
    ssm_d = nrm(ks[12], (L, G, C), 1.0)
    glu_w = nrm(ks[13], (L, SSM_WIDTH, SSM_WIDTH), SSM_WIDTH ** -0.5)
    glu_b = nrm(ks[14], (L, SSM_WIDTH), 0.01)
    attn_out_norm_g = 1.0 + nrm(ks[15], (L, ATTN_WIDTH), 0.02)
    ssm_out_norm_g = 1.0 + nrm(ks[16], (L, SSM_WIDTH), 0.02)
    w_out = nrm(ks[17], (L, D_MIX, D_MODEL), D_MIX ** -0.5)
    norm2_g = 1.0 + nrm(ks[18], (L, D_MODEL), 0.02)
    k_up, k_down = jax.random.split(ks[19])
    w_mlp_up = nrm(k_up, (L, D_MODEL, D_FF), D_MODEL ** -0.5)
    w_mlp_down = nrm(k_down, (L, D_FF, D_MODEL), D_FF ** -0.5)
    return {"x": x, "norm1_g": norm1_g, "w_in": w_in, "q_norm_g": q_norm_g, "k_norm_g": k_norm_g,
            "ssm_a_re": ssm_a_re, "ssm_a_im": ssm_a_im, "ssm_log_dt": ssm_log_dt,
            "ssm_b_re": ssm_b_re, "ssm_b_im": ssm_b_im, "ssm_c_re": ssm_c_re, "ssm_c_im": ssm_c_im,
            "ssm_d": ssm_d, "glu_w": glu_w, "glu_b": glu_b,
            "attn_out_norm_g": attn_out_norm_g, "ssm_out_norm_g": ssm_out_norm_g,
            "w_out": w_out, "norm2_g": norm2_g, "w_mlp_up": w_mlp_up, "w_mlp_down": w_mlp_down}


def _fwd_reference(x, norm1_g, w_in, q_norm_g, k_norm_g, ssm_a_re, ssm_a_im, ssm_log_dt,
              ssm_b_re, ssm_b_im, ssm_c_re, ssm_c_im, ssm_d, glu_w, glu_b,
              attn_out_norm_g, ssm_out_norm_g, w_out, norm2_g, w_mlp_up, w_mlp_down):
    h = x
    for l in range(DEPTH):
        h = _hybrid_layer(h, norm1_g[l], w_in[l], q_norm_g[l], k_norm_g[l], ssm_a_re[l], ssm_a_im[l],
                          ssm_log_dt[l], ssm_b_re[l], ssm_b_im[l], ssm_c_re[l], ssm_c_im[l], ssm_d[l],
                          glu_w[l], glu_b[l], attn_out_norm_g[l], ssm_out_norm_g[l], w_out[l],
                          norm2_g[l], w_mlp_up[l], w_mlp_down[l])
    return h


import jax as _jax
import jax.numpy as _jnp

TWIN_FORMAT = 'train_step'
FWD_PARAMS = ['x', 'norm1_g', 'w_in', 'q_norm_g', 'k_norm_g', 'ssm_a_re', 'ssm_a_im', 'ssm_log_dt', 'ssm_b_re', 'ssm_b_im', 'ssm_c_re', 'ssm_c_im', 'ssm_d', 'glu_w', 'glu_b', 'attn_out_norm_g', 'ssm_out_norm_g', 'w_out', 'norm2_g', 'w_mlp_up', 'w_mlp_down']
TWIN_WEIGHTS = ['norm1_g', 'w_in', 'q_norm_g', 'k_norm_g', 'ssm_a_re', 'ssm_a_im', 'ssm_log_dt', 'ssm_b_re', 'ssm_b_im', 'ssm_c_re', 'ssm_c_im', 'ssm_d', 'glu_w', 'glu_b', 'attn_out_norm_g', 'ssm_out_norm_g', 'w_out', 'norm2_g', 'w_mlp_up', 'w_mlp_down']
TWIN_DIFF_INPUT = 'x'
TWIN_INPUTS = ['x', 'norm1_g', 'w_in', 'q_norm_g', 'k_norm_g', 'ssm_a_re', 'ssm_a_im', 'ssm_log_dt', 'ssm_b_re', 'ssm_b_im', 'ssm_c_re', 'ssm_c_im', 'ssm_d', 'glu_w', 'glu_b', 'attn_out_norm_g', 'ssm_out_norm_g', 'w_out', 'norm2_g', 'w_mlp_up', 'w_mlp_down', 'loss_target', 'm_norm1_g', 'm_w_in', 'm_q_norm_g', 'm_k_norm_g', 'm_ssm_a_re', 'm_ssm_a_im', 'm_ssm_log_dt', 'm_ssm_b_re', 'm_ssm_b_im', 'm_ssm_c_re', 'm_ssm_c_im', 'm_ssm_d', 'm_glu_w', 'm_glu_b', 'm_attn_out_norm_g', 'm_ssm_out_norm_g', 'm_w_out', 'm_norm2_g', 'm_w_mlp_up', 'm_w_mlp_down', 'v_norm1_g', 'v_w_in', 'v_q_norm_g', 'v_k_norm_g', 'v_ssm_a_re', 'v_ssm_a_im', 'v_ssm_log_dt', 'v_ssm_b_re', 'v_ssm_b_im', 'v_ssm_c_re', 'v_ssm_c_im', 'v_ssm_d', 'v_glu_w', 'v_glu_b', 'v_attn_out_norm_g', 'v_ssm_out_norm_g', 'v_w_out', 'v_norm2_g', 'v_w_mlp_up', 'v_w_mlp_down']
TWIN_OUTPUTS = ['loss', 'grad_x', 'grad_norm1_g', 'grad_w_in', 'grad_q_norm_g', 'grad_k_norm_g', 'grad_ssm_a_re', 'grad_ssm_a_im', 'grad_ssm_log_dt', 'grad_ssm_b_re', 'grad_ssm_b_im', 'grad_ssm_c_re', 'grad_ssm_c_im', 'grad_ssm_d', 'grad_glu_w', 'grad_glu_b', 'grad_attn_out_norm_g', 'grad_ssm_out_norm_g', 'grad_w_out', 'grad_norm2_g', 'grad_w_mlp_up', 'grad_w_mlp_down', 'delta_norm1_g', 'delta_w_in', 'delta_q_norm_g', 'delta_k_norm_g', 'delta_ssm_a_re', 'delta_ssm_a_im', 'delta_ssm_log_dt', 'delta_ssm_b_re', 'delta_ssm_b_im', 'delta_ssm_c_re', 'delta_ssm_c_im', 'delta_ssm_d', 'delta_glu_w', 'delta_glu_b', 'delta_attn_out_norm_g', 'delta_ssm_out_norm_g', 'delta_w_out', 'delta_norm2_g', 'delta_w_mlp_up', 'delta_w_mlp_down', 'new_m_norm1_g', 'new_m_w_in', 'new_m_q_norm_g', 'new_m_k_norm_g', 'new_m_ssm_a_re', 'new_m_ssm_a_im', 'new_m_ssm_log_dt', 'new_m_ssm_b_re', 'new_m_ssm_b_im', 'new_m_ssm_c_re', 'new_m_ssm_c_im', 'new_m_ssm_d', 'new_m_glu_w', 'new_m_glu_b', 'new_m_attn_out_norm_g', 'new_m_ssm_out_norm_g', 'new_m_w_out', 'new_m_norm2_g', 'new_m_w_mlp_up', 'new_m_w_mlp_down', 'new_v_norm1_g', 'new_v_w_in', 'new_v_q_norm_g', 'new_v_k_norm_g', 'new_v_ssm_a_re', 'new_v_ssm_a_im', 'new_v_ssm_log_dt', 'new_v_ssm_b_re', 'new_v_ssm_b_im', 'new_v_ssm_c_re', 'new_v_ssm_c_im', 'new_v_ssm_d', 'new_v_glu_w', 'new_v_glu_b', 'new_v_attn_out_norm_g', 'new_v_ssm_out_norm_g', 'new_v_w_out', 'new_v_norm2_g', 'new_v_w_mlp_up', 'new_v_w_mlp_down']
TWIN_LEAF_KINDS = {'loss': 'loss', 'grad_x': 'grad_x', 'grad_norm1_g': 'grad_w', 'grad_w_in': 'grad_w', 'grad_q_norm_g': 'grad_w', 'grad_k_norm_g': 'grad_w', 'grad_ssm_a_re': 'grad_w', 'grad_ssm_a_im': 'grad_w', 'grad_ssm_log_dt': 'grad_w', 'grad_ssm_b_re': 'grad_w', 'grad_ssm_b_im': 'grad_w', 'grad_ssm_c_re': 'grad_w', 'grad_ssm_c_im': 'grad_w', 'grad_ssm_d': 'grad_w', 'grad_glu_w': 'grad_w', 'grad_glu_b': 'grad_w', 'grad_attn_out_norm_g': 'grad_w', 'grad_ssm_out_norm_g': 'grad_w', 'grad_w_out': 'grad_w', 'grad_norm2_g': 'grad_w', 'grad_w_mlp_up': 'grad_w', 'grad_w_mlp_down': 'grad_w', 'delta_norm1_g': 'delta_w', 'delta_w_in': 'delta_w', 'delta_q_norm_g': 'delta_w', 'delta_k_norm_g': 'delta_w', 'delta_ssm_a_re': 'delta_w', 'delta_ssm_a_im': 'delta_w', 'delta_ssm_log_dt': 'delta_w', 'delta_ssm_b_re': 'delta_w', 'delta_ssm_b_im': 'delta_w', 'delta_ssm_c_re': 'delta_w', 'delta_ssm_c_im': 'delta_w', 'delta_ssm_d': 'delta_w', 'delta_glu_w': 'delta_w', 'delta_glu_b': 'delta_w', 'delta_attn_out_norm_g': 'delta_w', 'delta_ssm_out_norm_g': 'delta_w', 'delta_w_out': 'delta_w', 'delta_norm2_g': 'delta_w', 'delta_w_mlp_up': 'delta_w', 'delta_w_mlp_down': 'delta_w', 'new_m_norm1_g': 'new_m', 'new_m_w_in': 'new_m', 'new_m_q_norm_g': 'new_m', 'new_m_k_norm_g': 'new_m', 'new_m_ssm_a_re': 'new_m', 'new_m_ssm_a_im': 'new_m', 'new_m_ssm_log_dt': 'new_m', 'new_m_ssm_b_re': 'new_m', 'new_m_ssm_b_im': 'new_m', 'new_m_ssm_c_re': 'new_m', 'new_m_ssm_c_im': 'new_m', 'new_m_ssm_d': 'new_m', 'new_m_glu_w': 'new_m', 'new_m_glu_b': 'new_m', 'new_m_attn_out_norm_g': 'new_m', 'new_m_ssm_out_norm_g': 'new_m', 'new_m_w_out': 'new_m', 'new_m_norm2_g': 'new_m', 'new_m_w_mlp_up': 'new_m', 'new_m_w_mlp_down': 'new_m', 'new_v_norm1_g': 'new_v', 'new_v_w_in': 'new_v', 'new_v_q_norm_g': 'new_v', 'new_v_k_norm_g': 'new_v', 'new_v_ssm_a_re': 'new_v', 'new_v_ssm_a_im': 'new_v', 'new_v_ssm_log_dt': 'new_v', 'new_v_ssm_b_re': 'new_v', 'new_v_ssm_b_im': 'new_v', 'new_v_ssm_c_re': 'new_v', 'new_v_ssm_c_im': 'new_v', 'new_v_ssm_d': 'new_v', 'new_v_glu_w': 'new_v', 'new_v_glu_b': 'new_v', 'new_v_attn_out_norm_g': 'new_v', 'new_v_ssm_out_norm_g': 'new_v', 'new_v_w_out': 'new_v', 'new_v_norm2_g': 'new_v', 'new_v_w_mlp_up': 'new_v', 'new_v_w_mlp_down': 'new_v'}


def _forward(args):
    return _fwd_reference(*[args[k] for k in FWD_PARAMS])


def _output_shape():
    def fwd():
        inp = _fwd_setup_inputs(0)
        return _fwd_reference(*[inp[k] for k in FWD_PARAMS])
    out = _jax.eval_shape(fwd)
    return out.shape, out.dtype

N_MICROBATCH = 1
ADAM_LR = 0.001
ADAM_B1 = 0.9
ADAM_B2 = 0.999
ADAM_EPS = 1e-08
ADAM_WD = 0.01
ADAM_STEP = 10
PER_EXAMPLE_BATCH_AXIS = {'x': 0, 'loss_target': 0}
SHARED_INPUTS = []
_WEIGHT_DTYPES = {'norm1_g': _jnp.float32, 'w_in': _jnp.float32, 'q_norm_g': _jnp.float32, 'k_norm_g': _jnp.float32, 'ssm_a_re': _jnp.float32, 'ssm_a_im': _jnp.float32, 'ssm_log_dt': _jnp.float32, 'ssm_b_re': _jnp.float32, 'ssm_b_im': _jnp.float32, 'ssm_c_re': _jnp.float32, 'ssm_c_im': _jnp.float32, 'ssm_d': _jnp.float32, 'glu_w': _jnp.float32, 'glu_b': _jnp.float32, 'attn_out_norm_g': _jnp.float32, 'ssm_out_norm_g': _jnp.float32, 'w_out': _jnp.float32, 'norm2_g': _jnp.float32, 'w_mlp_up': _jnp.float32, 'w_mlp_down': _jnp.float32}
MOMENT_SCALE = {'norm1_g': 4.873172e+00, 'w_in': 3.092047e+00, 'q_norm_g': 7.632295e+00, 'k_norm_g': 7.707588e+00, 'ssm_a_re': 8.418100e-02, 'ssm_a_im': 1.123326e-01, 'ssm_log_dt': 8.376046e+01, 'ssm_b_re': 7.765704e-02, 'ssm_b_im': 7.375249e-02, 'ssm_c_re': 1.423951e-01, 'ssm_c_im': 1.528595e-01, 'ssm_d': 6.489429e+01, 'glu_w': 8.383977e+00, 'glu_b': 2.643101e+01, 'attn_out_norm_g': 1.278973e+02, 'ssm_out_norm_g': 2.411436e+02, 'w_out': 3.843636e+01, 'norm2_g': 3.939635e+02, 'w_mlp_up': 1.459128e+01, 'w_mlp_down': 4.392843e+01}


def _to_microbatches(a, axis):
    t = _jnp.moveaxis(a, axis, 0)
    t = t.reshape((N_MICROBATCH, t.shape[0] // N_MICROBATCH) + t.shape[1:])
    return _jnp.moveaxis(t, 1, axis + 1)


def setup_inputs(seed: int = 0) -> dict:
    inp = _fwd_setup_inputs(seed)
    key = _jax.random.fold_in(_jax.random.key(seed), 7919)
    shape, _ = _output_shape()
    out = dict(inp)
    out["loss_target"] = _jax.random.normal(_jax.random.fold_in(key, 0), shape, _jnp.float32)
    for i, name in enumerate(TWIN_WEIGHTS):
        w = inp[name].astype(_jnp.float32)
        if MOMENT_SCALE is None:
            s = _jnp.sqrt(_jnp.mean(_jnp.square(w)) + 1e-30)
        else:
            s = MOMENT_SCALE[name]
        km, kv = _jax.random.split(_jax.random.fold_in(key, i + 1))
        out[name] = w
        out["m_" + name] = s * _jax.random.normal(km, w.shape, _jnp.float32)
        out["v_" + name] = (s * s) * _jax.random.uniform(kv, w.shape, _jnp.float32, 0.5, 1.5)
    if N_MICROBATCH > 1:
        for name, axis in PER_EXAMPLE_BATCH_AXIS.items():
            out[name] = _to_microbatches(out[name], axis)
    return {'x': out['x'], 'norm1_g': out['norm1_g'], 'w_in': out['w_in'], 'q_norm_g': out['q_norm_g'], 'k_norm_g': out['k_norm_g'], 'ssm_a_re': out['ssm_a_re'], 'ssm_a_im': out['ssm_a_im'], 'ssm_log_dt': out['ssm_log_dt'], 'ssm_b_re': out['ssm_b_re'], 'ssm_b_im': out['ssm_b_im'], 'ssm_c_re': out['ssm_c_re'], 'ssm_c_im': out['ssm_c_im'], 'ssm_d': out['ssm_d'], 'glu_w': out['glu_w'], 'glu_b': out['glu_b'], 'attn_out_norm_g': out['attn_out_norm_g'], 'ssm_out_norm_g': out['ssm_out_norm_g'], 'w_out': out['w_out'], 'norm2_g': out['norm2_g'], 'w_mlp_up': out['w_mlp_up'], 'w_mlp_down': out['w_mlp_down'], 'loss_target': out['loss_target'], 'm_norm1_g': out['m_norm1_g'], 'm_w_in': out['m_w_in'], 'm_q_norm_g': out['m_q_norm_g'], 'm_k_norm_g': out['m_k_norm_g'], 'm_ssm_a_re': out['m_ssm_a_re'], 'm_ssm_a_im': out['m_ssm_a_im'], 'm_ssm_log_dt': out['m_ssm_log_dt'], 'm_ssm_b_re': out['m_ssm_b_re'], 'm_ssm_b_im': out['m_ssm_b_im'], 'm_ssm_c_re': out['m_ssm_c_re'], 'm_ssm_c_im': out['m_ssm_c_im'], 'm_ssm_d': out['m_ssm_d'], 'm_glu_w': out['m_glu_w'], 'm_glu_b': out['m_glu_b'], 'm_attn_out_norm_g': out['m_attn_out_norm_g'], 'm_ssm_out_norm_g': out['m_ssm_out_norm_g'], 'm_w_out': out['m_w_out'], 'm_norm2_g': out['m_norm2_g'], 'm_w_mlp_up': out['m_w_mlp_up'], 'm_w_mlp_down': out['m_w_mlp_down'], 'v_norm1_g': out['v_norm1_g'], 'v_w_in': out['v_w_in'], 'v_q_norm_g': out['v_q_norm_g'], 'v_k_norm_g': out['v_k_norm_g'], 'v_ssm_a_re': out['v_ssm_a_re'], 'v_ssm_a_im': out['v_ssm_a_im'], 'v_ssm_log_dt': out['v_ssm_log_dt'], 'v_ssm_b_re': out['v_ssm_b_re'], 'v_ssm_b_im': out['v_ssm_b_im'], 'v_ssm_c_re': out['v_ssm_c_re'], 'v_ssm_c_im': out['v_ssm_c_im'], 'v_ssm_d': out['v_ssm_d'], 'v_glu_w': out['v_glu_w'], 'v_glu_b': out['v_glu_b'], 'v_attn_out_norm_g': out['v_attn_out_norm_g'], 'v_ssm_out_norm_g': out['v_ssm_out_norm_g'], 'v_w_out': out['v_w_out'], 'v_norm2_g': out['v_norm2_g'], 'v_w_mlp_up': out['v_w_mlp_up'], 'v_w_mlp_down': out['v_w_mlp_down']}


def _loss(weights, diff, rest, loss_target):
    with _jax.named_scope("forward"):
        args = {**rest, TWIN_DIFF_INPUT: diff, **{k: w.astype(_WEIGHT_DTYPES[k]) for k, w in weights.items()}}
        y = _forward(args)
    with _jax.named_scope("loss_head"):
        err = _jnp.square(y.astype(_jnp.float32) - loss_target)
        return 0.5 * _jnp.sum(_jnp.mean(err, axis=-1)) if err.ndim else 0.5 * err


def _adamw(w, g, m, v):
    m = ADAM_B1 * m + (1.0 - ADAM_B1) * g
    v = ADAM_B2 * v + (1.0 - ADAM_B2) * _jnp.square(g)
    m_hat = m / (1.0 - ADAM_B1 ** ADAM_STEP)
    v_hat = v / (1.0 - ADAM_B2 ** ADAM_STEP)
    delta = -ADAM_LR * (m_hat / (_jnp.sqrt(v_hat) + ADAM_EPS) + ADAM_WD * w)
    return delta, m, v


def reference(x, norm1_g, w_in, q_norm_g, k_norm_g, ssm_a_re, ssm_a_im, ssm_log_dt, ssm_b_re, ssm_b_im, ssm_c_re, ssm_c_im, ssm_d, glu_w, glu_b, attn_out_norm_g, ssm_out_norm_g, w_out, norm2_g, w_mlp_up, w_mlp_down, loss_target, m_norm1_g, m_w_in, m_q_norm_g, m_k_norm_g, m_ssm_a_re, m_ssm_a_im, m_ssm_log_dt, m_ssm_b_re, m_ssm_b_im, m_ssm_c_re, m_ssm_c_im, m_ssm_d, m_glu_w, m_glu_b, m_attn_out_norm_g, m_ssm_out_norm_g, m_w_out, m_norm2_g, m_w_mlp_up, m_w_mlp_down, v_norm1_g, v_w_in, v_q_norm_g, v_k_norm_g, v_ssm_a_re, v_ssm_a_im, v_ssm_log_dt, v_ssm_b_re, v_ssm_b_im, v_ssm_c_re, v_ssm_c_im, v_ssm_d, v_glu_w, v_glu_b, v_attn_out_norm_g, v_ssm_out_norm_g, v_w_out, v_norm2_g, v_w_mlp_up, v_w_mlp_down):
    given = dict(x=x, norm1_g=norm1_g, w_in=w_in, q_norm_g=q_norm_g, k_norm_g=k_norm_g, ssm_a_re=ssm_a_re, ssm_a_im=ssm_a_im, ssm_log_dt=ssm_log_dt, ssm_b_re=ssm_b_re, ssm_b_im=ssm_b_im, ssm_c_re=ssm_c_re, ssm_c_im=ssm_c_im, ssm_d=ssm_d, glu_w=glu_w, glu_b=glu_b, attn_out_norm_g=attn_out_norm_g, ssm_out_norm_g=ssm_out_norm_g, w_out=w_out, norm2_g=norm2_g, w_mlp_up=w_mlp_up, w_mlp_down=w_mlp_down, loss_target=loss_target, m_norm1_g=m_norm1_g, m_w_in=m_w_in, m_q_norm_g=m_q_norm_g, m_k_norm_g=m_k_norm_g, m_ssm_a_re=m_ssm_a_re, m_ssm_a_im=m_ssm_a_im, m_ssm_log_dt=m_ssm_log_dt, m_ssm_b_re=m_ssm_b_re, m_ssm_b_im=m_ssm_b_im, m_ssm_c_re=m_ssm_c_re, m_ssm_c_im=m_ssm_c_im, m_ssm_d=m_ssm_d, m_glu_w=m_glu_w, m_glu_b=m_glu_b, m_attn_out_norm_g=m_attn_out_norm_g, m_ssm_out_norm_g=m_ssm_out_norm_g, m_w_out=m_w_out, m_norm2_g=m_norm2_g, m_w_mlp_up=m_w_mlp_up, m_w_mlp_down=m_w_mlp_down, v_norm1_g=v_norm1_g, v_w_in=v_w_in, v_q_norm_g=v_q_norm_g, v_k_norm_g=v_k_norm_g, v_ssm_a_re=v_ssm_a_re, v_ssm_a_im=v_ssm_a_im, v_ssm_log_dt=v_ssm_log_dt, v_ssm_b_re=v_ssm_b_re, v_ssm_b_im=v_ssm_b_im, v_ssm_c_re=v_ssm_c_re, v_ssm_c_im=v_ssm_c_im, v_ssm_d=v_ssm_d, v_glu_w=v_glu_w, v_glu_b=v_glu_b, v_attn_out_norm_g=v_attn_out_norm_g, v_ssm_out_norm_g=v_ssm_out_norm_g, v_w_out=v_w_out, v_norm2_g=v_norm2_g, v_w_mlp_up=v_w_mlp_up, v_w_mlp_down=v_w_mlp_down)
    weights = {n: given[n] for n in TWIN_WEIGHTS}
    shared = {n: given[n] for n in SHARED_INPUTS}
    per_example = {n: given[n] for n in ['x']}
    grad_fn = _jax.value_and_grad(_loss, argnums=(0, 1))

    def one_microbatch(ex, loss_target):
        ex = dict(ex)
        diff = ex.pop(TWIN_DIFF_INPUT)
        return grad_fn(weights, diff, {**shared, **ex}, loss_target)

    if N_MICROBATCH == 1:
        loss, (grad_w, grad_x) = one_microbatch(per_example, given["loss_target"])
    else:
        def body(carry, xs):
            loss_sum, grad_sum = carry
            l_k, (gw_k, gx_k) = one_microbatch(xs[0], xs[1])
            with _jax.named_scope("update"):
                return (loss_sum + l_k, _jax.tree.map(_jnp.add, grad_sum, gw_k)), gx_k

        init = (_jnp.zeros((), _jnp.float32), _jax.tree.map(_jnp.zeros_like, weights))
        (loss, grad_w), grad_x = _jax.lax.scan(body, init, (per_example, given["loss_target"]))
    with _jax.named_scope("update"):
        delta_w, new_m, new_v = {}, {}, {}
        for n in TWIN_WEIGHTS:
            delta_w[n], new_m[n], new_v[n] = _adamw(weights[n], grad_w[n], given["m_" + n], given["v_" + n])
    return (loss, grad_x, *[grad_w[n] for n in TWIN_WEIGHTS], *[delta_w[n] for n in TWIN_WEIGHTS],
            *[new_m[n] for n in TWIN_WEIGHTS], *[new_v[n] for n in TWIN_WEIGHTS])
```

```python
import functools
import math

import jax
import jax.numpy as jnp
from jax import lax
from jax.experimental import pallas as pl
from jax.experimental.pallas import tpu as pltpu

F32 = jnp.float32
BF16 = jnp.bfloat16
MESH = pl.DeviceIdType.MESH

D_MODEL = 1024
ATTN_W = 512
SSM_W = 512
HEAD = 64
D_FF = 4096
PROJ_W = 2048
N_GROUPS = 32
N_STATE = 64
GROUP_W = 16
EPS = 1e-6
NEG = -1e30
DILATIONS = (1, 4, 16)
BLK = 128
TILE = 2048
LANES = 128
N_LB = SSM_W // LANES
N_SLAB = 2 * N_LB * N_STATE * 8 // LANES // N_LB
VMEM_LIMIT = 56 * 1024 * 1024

ADAM_LR, ADAM_B1, ADAM_B2, ADAM_EPS, ADAM_WD, ADAM_STEP = 0.001, 0.9, 0.999, 1e-08, 0.01, 10


def _params(*sem):
    return pltpu.CompilerParams(dimension_semantics=sem, vmem_limit_bytes=VMEM_LIMIT)


def _nt(a, b):
    return lax.dot_general(a, b, (((1,), (1,)), ((), ())), preferred_element_type=F32)


def _tn(a, b):
    return lax.dot_general(a, b, (((0,), (0,)), ((), ())), preferred_element_type=F32)


def _mm(a, b):
    return jnp.dot(a, b, preferred_element_type=F32)


def _group_mean(t, ones_bd, width):
    hi = t.astype(BF16)
    lo = (t - hi.astype(F32)).astype(BF16)
    return (_mm(hi, ones_bd) + _mm(lo, ones_bd)) * (1.0 / width)


def _rms(x):
    return lax.rsqrt(jnp.mean(x * x, axis=-1, keepdims=True) + EPS)


def _rms_bwd(dy, x, r, g):
    xh = x * r
    dxh = dy * g
    dx = r * (dxh - xh * jnp.mean(dxh * xh, axis=-1, keepdims=True))
    return dx, dy * xh


def _colsum(x):
    return jnp.sum(x, axis=0, keepdims=True)


def _row_block(rows, cap):
    for b in range(min(rows, cap) // 8 * 8, 0, -8):
        if rows % b == 0:
            return b
    raise ValueError(f"no row block for {rows} rows")


def _inproj_fwd(x, g1, wi, gq, gk, ones64):
    t_len = x.shape[0]
    tm = 512
    n_hp = ATTN_W // LANES

    def body(x_ref, g1_ref, wi_ref, gq_ref, gk_ref, bd_ref, xn_ref, q_ref, k_ref, v_ref, u_ref, qr_ref, kr_ref):
        xv = x_ref[...]
        xn = (xv * _rms(xv) * g1_ref[...]).astype(BF16)
        xn_ref[...] = xn
        proj = _mm(xn, wi_ref[...])
        q = proj[:, 0:ATTN_W]
        k = proj[:, ATTN_W:2 * ATTN_W]
        v = proj[:, 2 * ATTN_W:3 * ATTN_W]
        u_ref[...] = proj[:, 3 * ATTN_W:]
        qr_ref[...] = q
        kr_ref[...] = k
        bd = bd_ref[...]
        qn = q * lax.rsqrt(_group_mean(q * q, bd, HEAD) + EPS) * gq_ref[...] * (HEAD ** -0.5)
        kn = k * lax.rsqrt(_group_mean(k * k, bd, HEAD) + EPS) * gk_ref[...]
        for hp in range(n_hp):
            sl = slice(hp * LANES, (hp + 1) * LANES)
            q_ref[hp] = qn[:, sl]
            k_ref[hp] = kn[:, sl]
            v_ref[hp] = v[:, sl]

    row = lambda i: (i, 0)
    const = lambda i: (0, 0)
    hp_spec = pl.BlockSpec((n_hp, tm, LANES), lambda i: (0, i, 0))
    hp_shape = jax.ShapeDtypeStruct((n_hp, t_len, LANES), F32)
    return pl.pallas_call(
        body, name="inproj_fwd", grid=(t_len // tm,),
        in_specs=[pl.BlockSpec((tm, D_MODEL), row), pl.BlockSpec((1, D_MODEL), const),
                  pl.BlockSpec((D_MODEL, PROJ_W), const), pl.BlockSpec((1, ATTN_W), const),
                  pl.BlockSpec((1, ATTN_W), const), pl.BlockSpec((ATTN_W, ATTN_W), const)],
        out_specs=[pl.BlockSpec((tm, D_MODEL), row), hp_spec, hp_spec, hp_spec,
                   pl.BlockSpec((tm, SSM_W), row), pl.BlockSpec((tm, ATTN_W), row), pl.BlockSpec((tm, ATTN_W), row)],
        out_shape=[jax.ShapeDtypeStruct((t_len, D_MODEL), BF16), hp_shape, hp_shape, hp_shape,
                   jax.ShapeDtypeStruct((t_len, SSM_W), F32), jax.ShapeDtypeStruct((t_len, ATTN_W), F32),
                   jax.ShapeDtypeStruct((t_len, ATTN_W), F32)],
        compiler_params=_params("arbitrary"),
    )(x, g1, wi, gq, gk, ones64)


def _attn_masks():
    lane = lax.broadcasted_iota(jnp.int32, (BLK, BLK), 1)
    row = lax.broadcasted_iota(jnp.int32, (BLK, BLK), 0)
    return lane < HEAD, lane >= row, lane <= row


def _unit_rows(uidx, d):
    nb = TILE // (BLK * d)
    r = lax.div(uidx, nb)
    b = lax.rem(uidx, nb)
    start = r + d * BLK * b
    if d == 1:
        start = pl.multiple_of(start, BLK)
        mk = lambda s: pl.ds(pl.multiple_of(s, BLK), BLK)
    else:
        mk = lambda s: pl.ds(s, BLK, stride=d)
    return b, mk(start), mk(TILE + start), mk(TILE + start - d * BLK)


def _attn_fwd(q, k, v):
    n_hp, t_len, _ = q.shape
    nt = t_len // TILE

    def body(q_ref, kp_ref, kc_ref, vp_ref, vc_ref, o_ref, lse_ref, kk, vv, m_s, l_s, acc_s):
        t = pl.program_id(1)
        kk[0:TILE] = kp_ref[0]
        kk[TILE:] = kc_ref[0]
        vv[0:TILE] = vp_ref[0]
        vv[TILE:] = vc_ref[0]
        head0, band_prev, band_cur = _attn_masks()

        for d in DILATIONS:
            def unit(uidx, carry, d=d):
                b, rows_q, rows_c, rows_p = _unit_rows(uidx, d)
                mask_p = band_prev & ((t > 0) | (b > 0))
                qv = q_ref.at[0][rows_q, :]
                kc = kk[rows_c, :].astype(BF16)
                kp = kk[rows_p, :].astype(BF16)
                vc = vv[rows_c, :].astype(BF16)
                vp = vv[rows_p, :].astype(BF16)
                if d != 1:
                    m_old_f = m_s[rows_q, :]
                    l_old_f = l_s[rows_q, :]
                m_h, l_h, c_h, a_h = [], [], [], []
                for h in range(2):
                    hm = head0 if h == 0 else ~head0
                    qh = jnp.where(hm, qv, 0.0).astype(BF16)
                    sp = jnp.where(mask_p, _nt(qh, kp), NEG)
                    sc = jnp.where(band_cur, _nt(qh, kc), NEG)
                    mb = jnp.maximum(jnp.max(sp, axis=1, keepdims=True), jnp.max(sc, axis=1, keepdims=True))
                    if d == 1:
                        m_new = mb
                    else:
                        m_old = m_old_f[:, h * HEAD:h * HEAD + 1]
                        m_new = jnp.maximum(m_old, mb)
                        alpha = jnp.exp(m_old - m_new)
                        a_h.append(alpha)
                    pp = jnp.exp(sp - m_new)
                    pc = jnp.exp(sc - m_new)
                    ls = jnp.sum(pp, axis=1, keepdims=True) + jnp.sum(pc, axis=1, keepdims=True)
                    if d != 1:
                        ls = ls + alpha * l_old_f[:, h * HEAD:h * HEAD + 1]
                    m_h.append(m_new)
                    l_h.append(ls)
                    c_h.append(_mm(pp.astype(BF16), vp) + _mm(pc.astype(BF16), vc))
                m_s[rows_q, :] = jnp.where(head0, m_h[0], m_h[1])
                l_s[rows_q, :] = jnp.where(head0, l_h[0], l_h[1])
                contrib = jnp.where(head0, c_h[0], c_h[1])
                if d == 1:
                    acc_s[rows_q, :] = contrib
                else:
                    acc_s[rows_q, :] = acc_s[rows_q, :] * jnp.where(head0, a_h[0], a_h[1]) + contrib
                return carry

            lax.fori_loop(0, TILE // BLK, unit, 0)

        lv = l_s[...]
        o_ref[...] = acc_s[...] / lv
        lse_ref[0] = m_s[...] + jnp.log(lv)

    cur = lambda hp, t: (hp, t, 0)
    prev = lambda hp, t: (hp, jnp.maximum(t - 1, 0), 0)
    blk = (1, TILE, LANES)
    return pl.pallas_call(
        body, name="attn_fwd", grid=(n_hp, nt),
        in_specs=[pl.BlockSpec(blk, cur), pl.BlockSpec(blk, prev), pl.BlockSpec(blk, cur),
                  pl.BlockSpec(blk, prev), pl.BlockSpec(blk, cur)],
        out_specs=[pl.BlockSpec((TILE, LANES), lambda hp, t: (t, hp)), pl.BlockSpec(blk, cur)],
        out_shape=[jax.ShapeDtypeStruct((t_len, ATTN_W), F32), jax.ShapeDtypeStruct((n_hp, t_len, LANES), F32)],
        scratch_shapes=[pltpu.VMEM((2 * TILE, LANES), F32), pltpu.VMEM((2 * TILE, LANES), F32),
                        pltpu.VMEM((TILE, LANES), F32), pltpu.VMEM((TILE, LANES), F32), pltpu.VMEM((TILE, LANES), F32)],
        compiler_params=_params("arbitrary", "arbitrary"),
    )(q, k, k, v, v)


def _attn_bwd(q, k, v, o, do, lse, ones_hp):
    n_hp, t_len, _ = q.shape
    nt = t_len // TILE

    def body(q_ref, kp_ref, kc_ref, vp_ref, vc_ref, o_ref, do_ref, lse_ref, bd_ref,
             dq_ref, dk_ref, dv_ref, kk, vv, dkk, dvv, dq_s, dl_s):
        t = pl.program_id(1)

        @pl.when(t == 0)
        def _():
            dkk[...] = jnp.zeros_like(dkk)
            dvv[...] = jnp.zeros_like(dvv)

        @pl.when(t > 0)
        def _():
            dkk[0:TILE] = dkk[TILE:]
            dvv[0:TILE] = dvv[TILE:]
            dkk[TILE:] = jnp.zeros((TILE, LANES), F32)
            dvv[TILE:] = jnp.zeros((TILE, LANES), F32)

        @pl.when(t < nt)
        def _():
            kk[0:TILE] = kp_ref[0]
            kk[TILE:] = kc_ref[0]
            vv[0:TILE] = vp_ref[0]
            vv[TILE:] = vc_ref[0]
            dl_s[...] = _group_mean(do_ref[...] * o_ref[...], bd_ref[...], 1.0)
            dq_s[...] = jnp.zeros_like(dq_s)
            head0, band_prev, band_cur = _attn_masks()

            for d in DILATIONS:
                def unit(uidx, carry, d=d):
                    b, rows_q, rows_c, rows_p = _unit_rows(uidx, d)
                    mask_p = band_prev & ((t > 0) | (b > 0))
                    qv = q_ref.at[0][rows_q, :]
                    dov = do_ref[rows_q, :]
                    lse_f = lse_ref.at[0][rows_q, :]
                    dl_f = dl_s[rows_q, :]
                    kc = kk[rows_c, :].astype(BF16)
                    kp = kk[rows_p, :].astype(BF16)
                    vc = vv[rows_c, :].astype(BF16)
                    vp = vv[rows_p, :].astype(BF16)
                    dq_h, q_h, do_h, dsp_h, dsc_h, pp_h, pc_h = [], [], [], [], [], [], []
                    for h in range(2):
                        hm = head0 if h == 0 else ~head0
                        qh = jnp.where(hm, qv, 0.0).astype(BF16)
                        doh = jnp.where(hm, dov, 0.0).astype(BF16)
                        lse_c = lse_f[:, h * HEAD:h * HEAD + 1]
                        dl_c = dl_f[:, h * HEAD:h * HEAD + 1]
                        pp = jnp.where(mask_p, jnp.exp(_nt(qh, kp) - lse_c), 0.0)
                        pc = jnp.where(band_cur, jnp.exp(_nt(qh, kc) - lse_c), 0.0)
                        dsp = (pp * (_nt(doh, vp) - dl_c)).astype(BF16)
                        dsc = (pc * (_nt(doh, vc) - dl_c)).astype(BF16)
                        dq_h.append(_mm(dsp, kp) + _mm(dsc, kc))
                        q_h.append(qh)
                        do_h.append(doh)
                        dsp_h.append(dsp)
                        dsc_h.append(dsc)
                        pp_h.append(pp.astype(BF16))
                        pc_h.append(pc.astype(BF16))
                    dq_s[rows_q, :] = dq_s[rows_q, :] + jnp.where(head0, dq_h[0], dq_h[1])
                    q2 = jnp.concatenate(q_h, axis=0)
                    do2 = jnp.concatenate(do_h, axis=0)
                    dkk[rows_p, :] = dkk[rows_p, :] + _tn(jnp.concatenate(dsp_h, axis=0), q2)
                    dkk[rows_c, :] = dkk[rows_c, :] + _tn(jnp.concatenate(dsc_h, axis=0), q2)
                    dvv[rows_p, :] = dvv[rows_p, :] + _tn(jnp.concatenate(pp_h, axis=0), do2)
                    dvv[rows_c, :] = dvv[rows_c, :] + _tn(jnp.concatenate(pc_h, axis=0), do2)
                    return carry

                lax.fori_loop(0, TILE // BLK, unit, 0)

            dq_ref[...] = dq_s[...]

        @pl.when(t > 0)
        def _():
            dk_ref[...] = dkk[0:TILE]
            dv_ref[...] = dvv[0:TILE]

    last = nt - 1
    cur = lambda hp, t: (hp, jnp.minimum(t, last), 0)
    prev = lambda hp, t: (hp, jnp.clip(t - 1, 0, last), 0)
    cur2 = lambda hp, t: (jnp.minimum(t, last), hp)
    prev2 = lambda hp, t: (jnp.maximum(t - 1, 0), hp)
    blk = (1, TILE, LANES)
    blk2 = (TILE, LANES)
    out = jax.ShapeDtypeStruct((t_len, ATTN_W), F32)
    return pl.pallas_call(
        body, name="attn_bwd", grid=(n_hp, nt + 1),
        in_specs=[pl.BlockSpec(blk, cur), pl.BlockSpec(blk, prev), pl.BlockSpec(blk, cur),
                  pl.BlockSpec(blk, prev), pl.BlockSpec(blk, cur), pl.BlockSpec(blk2, cur2),
                  pl.BlockSpec(blk2, cur2), pl.BlockSpec(blk, cur), pl.BlockSpec((LANES, LANES), lambda hp, t: (0, 0))],
        out_specs=[pl.BlockSpec(blk2, cur2), pl.BlockSpec(blk2, prev2), pl.BlockSpec(blk2, prev2)],
        out_shape=[out, out, out],
        scratch_shapes=[pltpu.VMEM((2 * TILE, LANES), F32), pltpu.VMEM((2 * TILE, LANES), F32),
                        pltpu.VMEM((2 * TILE, LANES), F32), pltpu.VMEM((2 * TILE, LANES), F32),
                        pltpu.VMEM((TILE, LANES), F32), pltpu.VMEM((TILE, LANES), F32)],
        compiler_params=_params("arbitrary", "arbitrary"),
    )(q, k, k, v, v, o, do, lse, ones_hp)


def _discretise(lr, li, ldt, br, bi):
    dt = jnp.exp(ldt)
    mag = jnp.exp(lr * dt)
    ab_r, ab_i = mag * jnp.cos(li * dt), mag * jnp.sin(li * dt)
    den = lr * lr + li * li
    nr, ni = ab_r - 1.0, ab_i
    cr = (nr * lr + ni * li) / den
    ci = (ni * lr - nr * li) / den
    return ab_r, ab_i, cr * br - ci * bi, cr * bi + ci * br


def _disc_fwd(lr, li, ldt, br, bi):
    def body(lr_ref, li_ref, ldt_ref, br_ref, bi_ref, ar_o, ai_o, bbr_o, bbi_o):
        outs = _discretise(lr_ref[...], li_ref[...], ldt_ref[...], br_ref[...], bi_ref[...])
        for o_ref, val in zip((ar_o, ai_o, bbr_o, bbi_o), outs):
            o_ref[...] = val

    col = jax.ShapeDtypeStruct(lr.shape, F32)
    mat = jax.ShapeDtypeStruct(br.shape, F32)
    return pl.pallas_call(body, name="s5_disc_fwd", out_shape=[col, col, mat, mat])(lr, li, ldt, br, bi)


def _disc_bwd(lr, li, ldt, br, bi, d_ar, d_ai, d_bbr, d_bbi, group_sum):
    def body(lr_ref, li_ref, ldt_ref, br_ref, bi_ref, c1, c2, c3, c4, gs_ref, dlr_o, dli_o, dldt_o, dbr_o, dbi_o):
        _, vjp = jax.vjp(_discretise, lr_ref[...], li_ref[...], ldt_ref[...], br_ref[...], bi_ref[...])
        dlr, dli, dldt, dbr, dbi = vjp((c1[...], c2[...], c3[...], c4[...]))
        dlr_o[...] = dlr
        dli_o[...] = dli
        dbr_o[...] = dbr
        dbi_o[...] = dbi
        wide = jnp.broadcast_to(dldt, (dldt.shape[0], LANES))
        dldt_o[...] = jnp.dot(gs_ref[...], wide, precision=lax.Precision.HIGHEST, preferred_element_type=F32)

    col = jax.ShapeDtypeStruct(lr.shape, F32)
    mat = jax.ShapeDtypeStruct(br.shape, F32)
    return pl.pallas_call(
        body, name="s5_disc_bwd", out_shape=[col, col, jax.ShapeDtypeStruct((N_GROUPS, LANES), F32), mat, mat],
    )(lr, li, ldt, br, bi, d_ar, d_ai, d_bbr, d_bbi, group_sum)


N_CHUNK = TILE // BLK
HALF = 4


def _cmul(ar, ai, xr, xi):
    return ar * xr - ai * xi, ar * xi + ai * xr


def _power_table(a_ref, tab, sign, reverse):
    ar = [a_ref[0, j:j + 1, :] for j in range(HALF)]
    ai = [sign * a_ref[0, HALF + j:HALF + j + 1, :] for j in range(HALF)]

    def step(s, cur):
        row = pl.ds((BLK - 1 - s) if reverse else s, 1)
        nxt = []
        for j in range(HALF):
            tab.at[j][row, :] = cur[j]
            tab.at[HALF + j][row, :] = cur[HALF + j]
            nxt.append(_cmul(ar[j], ai[j], cur[j], cur[HALF + j]))
        return tuple(p[0] for p in nxt) + tuple(p[1] for p in nxt)

    lax.fori_loop(0, BLK, step, tuple(ar) + tuple(ai))


def _chunk_scan(buf, a_ref, sign, reverse):
    ar = [jnp.broadcast_to(a_ref[0, j:j + 1, :], (N_CHUNK, LANES)) for j in range(HALF)]
    ai = [sign * jnp.broadcast_to(a_ref[0, HALF + j:HALF + j + 1, :], (N_CHUNK, LANES)) for j in range(HALF)]

    def step(i, carry):
        s = (BLK - 1 - i) if reverse else i
        rows = pl.ds(s, N_CHUNK, stride=BLK)
        out = []
        for j in range(HALF):
            pr, pi = _cmul(ar[j], ai[j], carry[j], carry[HALF + j])
            xr = buf.at[j][rows, :] + pr
            xi = buf.at[HALF + j][rows, :] + pi
            buf.at[j][rows, :] = xr
            buf.at[HALF + j][rows, :] = xi
            out.append((xr, xi))
        return tuple(p[0] for p in out) + tuple(p[1] for p in out)

    zero = jnp.zeros((N_CHUNK, LANES), F32)
    lax.fori_loop(0, BLK, step, (zero,) * (2 * HALF))


def _chunk_states(buf, carry_s, xin_s, tab, reverse):
    edge = 0 if reverse else BLK - 1
    top = 0 if reverse else BLK - 1
    pw = [tab[j, top:top + 1, :] for j in range(2 * HALF)]
    cur = [carry_s[j:j + 1, :] for j in range(2 * HALF)]
    order = range(N_CHUNK - 1, -1, -1) if reverse else range(N_CHUNK)
    for c in order:
        for j in range(2 * HALF):
            xin_s[j, c:c + 1, :] = cur[j]
        nxt = []
        for j in range(HALF):
            pr, pi = _cmul(pw[j], pw[HALF + j], cur[j], cur[HALF + j])
            row = c * BLK + edge
            nxt.append((pr + buf[j, row:row + 1, :], pi + buf[HALF + j, row:row + 1, :]))
        cur = [p[0] for p in nxt] + [p[1] for p in nxt]
    for j in range(2 * HALF):
        carry_s[j:j + 1, :] = cur[j]


def _s5_fwd(u, a_cat, b_mat, c_mat, d_skip):
    t_len = u.shape[0]
    nt = t_len // TILE

    def body(u_ref, a_ref, b_ref, c_ref, d_ref, y_ref, st_ref, xs, tab, carry_s, xin_s):
        sb = pl.program_id(1)

        @pl.when(sb == 0)
        def _():
            _power_table(a_ref, tab, 1.0, False)
            carry_s[...] = jnp.zeros_like(carry_s)

        st_ref[0, 0] = carry_s[...]
        uv = u_ref[...]
        bu = _mm(uv.astype(BF16), b_ref[0])
        for j in range(2 * HALF):
            xs[j] = bu[:, j * LANES:(j + 1) * LANES]
        _chunk_scan(xs, a_ref, 1.0, False)
        _chunk_states(xs, carry_s, xin_s, tab, False)

        def fix(c, acc):
            rows = pl.ds(pl.multiple_of(c * BLK, BLK), BLK)
            for j in range(HALF):
                xr_in = xin_s.at[j][pl.ds(c, 1), :]
                xi_in = xin_s.at[HALF + j][pl.ds(c, 1), :]
                pr, pi = _cmul(tab[j], tab[HALF + j], xr_in, xi_in)
                xs.at[j][rows, :] = xs.at[j][rows, :] + pr
                xs.at[HALF + j][rows, :] = xs.at[HALF + j][rows, :] + pi
            return acc

        lax.fori_loop(0, N_CHUNK, fix, 0)
        y = d_ref[0] * uv
        for j in range(2 * HALF):
            y = y + _mm(xs[j].astype(BF16), c_ref[0, j * LANES:(j + 1) * LANES, :])
        y_ref[...] = y

    return pl.pallas_call(
        body, name="s5_fwd", grid=(N_LB, nt),
        in_specs=[pl.BlockSpec((TILE, LANES), lambda lb, sb: (sb, lb)),
                  pl.BlockSpec((1, 2 * HALF, LANES), lambda lb, sb: (lb, 0, 0)),
                  pl.BlockSpec((1, LANES, 2 * HALF * LANES), lambda lb, sb: (lb, 0, 0)),
                  pl.BlockSpec((1, 2 * HALF * LANES, LANES), lambda lb, sb: (lb, 0, 0)),
                  pl.BlockSpec((1, 1, LANES), lambda lb, sb: (lb, 0, 0))],
        out_specs=[pl.BlockSpec((TILE, LANES), lambda lb, sb: (sb, lb)),
                   pl.BlockSpec((1, 1, 2 * HALF, LANES), lambda lb, sb: (lb, sb, 0, 0))],
        out_shape=[jax.ShapeDtypeStruct((t_len, SSM_W), F32), jax.ShapeDtypeStruct((N_LB, nt, 2 * HALF, LANES), F32)],
        scratch_shapes=[pltpu.VMEM((2 * HALF, TILE, LANES), F32), pltpu.VMEM((2 * HALF, BLK, LANES), F32),
                        pltpu.VMEM((2 * HALF, LANES), F32), pltpu.VMEM((2 * HALF, N_CHUNK, LANES), F32)],
        compiler_params=_params("arbitrary", "arbitrary"),
    )(u, a_cat, b_mat, c_mat, d_skip)


def _s5_bwd(u, dy, states, a_cat, b_mat, c_mat, d_skip):
    t_len = u.shape[0]
    nt = t_len // TILE
    last = nt - 1

    def body(u_ref, dy_ref, st_ref, a_ref, b_ref, c_ref, d_ref, du_ref, db_ref, dc_ref, da_ref, dd_ref,
             xs, gs, tab, tabc, carry_s, lam_s, xin_s, lin_s):
        sb = pl.program_id(1)

        @pl.when(sb == 0)
        def _():
            _power_table(a_ref, tab, 1.0, False)
            _power_table(a_ref, tabc, -1.0, True)
            lam_s[...] = jnp.zeros_like(lam_s)
            db_ref[...] = jnp.zeros_like(db_ref)
            dc_ref[...] = jnp.zeros_like(dc_ref)
            da_ref[...] = jnp.zeros_like(da_ref)
            dd_ref[...] = jnp.zeros_like(dd_ref)

        uv = u_ref[...]
        dyv = dy_ref[...]
        ub = uv.astype(BF16)
        dyb = dyv.astype(BF16)
        carry_s[...] = st_ref[0, 0]
        bu = _mm(ub, b_ref[0])
        gy = _nt(dyb, c_ref[0])
        for j in range(2 * HALF):
            xs[j] = bu[:, j * LANES:(j + 1) * LANES]
            gs[j] = gy[:, j * LANES:(j + 1) * LANES]
        _chunk_scan(xs, a_ref, 1.0, False)
        _chunk_states(xs, carry_s, xin_s, tab, False)
        _chunk_scan(gs, a_ref, -1.0, True)
        _chunk_states(gs, lam_s, lin_s, tabc, True)
        last_row = lax.broadcasted_iota(jnp.int32, (BLK, LANES), 0) == BLK - 1

        def fix(c, acc):
            rows = pl.ds(pl.multiple_of(c * BLK, BLK), BLK)
            out = []
            for j in range(HALF):
                xr_in = xin_s.at[j][pl.ds(c, 1), :]
                xi_in = xin_s.at[HALF + j][pl.ds(c, 1), :]
                pr, pi = _cmul(tab[j], tab[HALF + j], xr_in, xi_in)
                xr = xs.at[j][rows, :] + pr
                xi = xs.at[HALF + j][rows, :] + pi
                xs.at[j][rows, :] = xr
                xs.at[HALF + j][rows, :] = xi
                lr_in = lin_s.at[j][pl.ds(c, 1), :]
                li_in = lin_s.at[HALF + j][pl.ds(c, 1), :]
                qr, qi = _cmul(tabc[j], tabc[HALF + j], lr_in, li_in)
                lr_ = gs.at[j][rows, :] + qr
                li_ = gs.at[HALF + j][rows, :] + qi
                gs.at[j][rows, :] = lr_
                gs.at[HALF + j][rows, :] = li_
                nr = jnp.where(last_row, lr_in, pltpu.roll(lr_, BLK - 1, 0))
                ni = jnp.where(last_row, li_in, pltpu.roll(li_, BLK - 1, 0))
                out.append(acc[j] + _colsum(xr * nr + xi * ni))
                out.append(acc[HALF + j] + _colsum(xr * ni - xi * nr))
            return tuple(out[0::2]) + tuple(out[1::2])

        zero = jnp.zeros((1, LANES), F32)
        da = lax.fori_loop(0, N_CHUNK, fix, (zero,) * (2 * HALF))
        for j in range(2 * HALF):
            da_ref[0, j:j + 1, :] = da_ref[0, j:j + 1, :] + da[j]
        lam = jnp.concatenate([gs[j].astype(BF16) for j in range(2 * HALF)], axis=1)
        xcat = jnp.concatenate([xs[j].astype(BF16) for j in range(2 * HALF)], axis=1)
        du_ref[...] = _nt(lam, b_ref[0]) + d_ref[0] * dyv
        db_ref[0] = db_ref[0] + _tn(ub, lam)
        dc_ref[0] = dc_ref[0] + _tn(dyb, xcat)
        dd_ref[0] = dd_ref[0] + _colsum(dyv * uv)

    rev = lambda lb, sb: (last - sb, lb)
    per_lb = lambda lb, sb: (lb, 0, 0)
    wide = 2 * HALF * LANES
    return pl.pallas_call(
        body, name="s5_bwd", grid=(N_LB, nt),
        in_specs=[pl.BlockSpec((TILE, LANES), rev), pl.BlockSpec((TILE, LANES), rev),
                  pl.BlockSpec((1, 1, 2 * HALF, LANES), lambda lb, sb: (lb, last - sb, 0, 0)),
                  pl.BlockSpec((1, 2 * HALF, LANES), per_lb), pl.BlockSpec((1, LANES, wide), per_lb),
                  pl.BlockSpec((1, wide, LANES), per_lb), pl.BlockSpec((1, 1, LANES), per_lb)],
        out_specs=[pl.BlockSpec((TILE, LANES), rev), pl.BlockSpec((1, LANES, wide), per_lb),
                   pl.BlockSpec((1, LANES, wide), per_lb), pl.BlockSpec((1, 2 * HALF, LANES), per_lb),
                   pl.BlockSpec((1, 1, LANES), per_lb)],
        out_shape=[jax.ShapeDtypeStruct((t_len, SSM_W), F32), jax.ShapeDtypeStruct((N_LB, LANES, wide), F32),
                   jax.ShapeDtypeStruct((N_LB, LANES, wide), F32), jax.ShapeDtypeStruct((N_LB, 2 * HALF, LANES), F32),
                   jax.ShapeDtypeStruct((N_LB, 1, LANES), F32)],
        scratch_shapes=[pltpu.VMEM((2 * HALF, TILE, LANES), F32), pltpu.VMEM((2 * HALF, TILE, LANES), F32),
                        pltpu.VMEM((2 * HALF, BLK, LANES), F32), pltpu.VMEM((2 * HALF, BLK, LANES), F32),
                        pltpu.VMEM((2 * HALF, LANES), F32), pltpu.VMEM((2 * HALF, LANES), F32),
                        pltpu.VMEM((2 * HALF, N_CHUNK, LANES), F32), pltpu.VMEM((2 * HALF, N_CHUNK, LANES), F32)],
        compiler_params=_params("arbitrary", "arbitrary"),
    )(u, dy, states, a_cat, b_mat, c_mat, d_skip)


_GELU_C = math.sqrt(2.0 / math.pi)
_GELU_K = 0.044715


def _gelu(y):
    t = jnp.tanh(_GELU_C * (y + _GELU_K * (y * y * y)))
    return y * (0.5 * (1.0 + t)), t


def _gelu_grad(y, t):
    return 0.5 * (1.0 + t) + 0.5 * y * (1.0 - t * t) * (_GELU_C * (1.0 + 3.0 * _GELU_K * y * y))


def _glu(y, wg, bias):
    z, t = _gelu(y)
    sg = jax.nn.sigmoid(_mm(z.astype(BF16), wg) + bias)
    return z, t, sg


def _mix_fwd(attn, y, x, wg, glu_b, ga, gs, wo):
    t_len = x.shape[0]
    tm = 512

    def body(attn_ref, y_ref, x_ref, wg_ref, b_ref, ga_ref, gs_ref, wo_ref, x2_ref, mix_ref, z_ref):
        z, _, sg = _glu(y_ref[...], wg_ref[...], b_ref[...])
        z_ref[...] = z.astype(BF16)
        s = z * sg
        av = attn_ref[...]
        an = (av * _rms(av) * ga_ref[...]).astype(BF16)
        sn = (s * _rms(s) * gs_ref[...]).astype(BF16)
        mix_ref[:, 0:ATTN_W] = an
        mix_ref[:, ATTN_W:] = sn
        x2_ref[...] = x_ref[...] + _mm(an, wo_ref[0:ATTN_W, :]) + _mm(sn, wo_ref[ATTN_W:, :])

    row = lambda i: (i, 0)
    const = lambda i: (0, 0)
    return pl.pallas_call(
        body, name="mix_fwd", grid=(t_len // tm,),
        in_specs=[pl.BlockSpec((tm, ATTN_W), row), pl.BlockSpec((tm, SSM_W), row), pl.BlockSpec((tm, D_MODEL), row),
                  pl.BlockSpec((SSM_W, SSM_W), const), pl.BlockSpec((1, SSM_W), const), pl.BlockSpec((1, ATTN_W), const),
                  pl.BlockSpec((1, SSM_W), const), pl.BlockSpec((D_MODEL, D_MODEL), const)],
        out_specs=[pl.BlockSpec((tm, D_MODEL), row), pl.BlockSpec((tm, D_MODEL), row), pl.BlockSpec((tm, SSM_W), row)],
        out_shape=[jax.ShapeDtypeStruct((t_len, D_MODEL), F32), jax.ShapeDtypeStruct((t_len, D_MODEL), BF16),
                   jax.ShapeDtypeStruct((t_len, SSM_W), BF16)],
        compiler_params=_params("arbitrary"),
    )(attn, y, x, wg, glu_b, ga, gs, wo)


def _mlp(x2, target, g2, wu, wd):
    t_len = x2.shape[0]
    tm = 256
    fc = 1024
    n_fc = D_FF // fc

    def body(x2_ref, tg_ref, g2_ref, wu_hbm, wd_hbm, dx2_ref, hdn_ref, dup_ref, h_ref, dyb_ref, dg2_ref, loss_ref,
             wu_s, wd_s, relu_s, sem):
        @pl.when(pl.program_id(0) == 0)
        def _():
            cu = pltpu.make_async_copy(wu_hbm, wu_s, sem.at[0])
            cd = pltpu.make_async_copy(wd_hbm, wd_s, sem.at[1])
            cu.start()
            cd.start()
            cu.wait()
            cd.wait()
            dg2_ref[...] = jnp.zeros_like(dg2_ref)
            loss_ref[...] = jnp.zeros_like(loss_ref)

        x2v = x2_ref[...]
        r = _rms(x2v)
        g2v = g2_ref[...]
        h = (x2v * r * g2v).astype(BF16)
        h_ref[...] = h
        yout = x2v
        for c in range(n_fc):
            cols = slice(c * fc, (c + 1) * fc)
            ru = jnp.maximum(_mm(h, wu_s[:, cols]), 0.0)
            relu_s[:, cols] = ru
            hd = (ru * ru).astype(BF16)
            hdn_ref[:, cols] = hd
            yout = yout + _mm(hd, wd_s[cols, :])
        err = yout - tg_ref[...]
        loss_ref[...] = loss_ref[...] + 0.5 * jnp.sum(err * err) * (1.0 / D_MODEL)
        dy = err * (1.0 / D_MODEL)
        dyb = dy.astype(BF16)
        dyb_ref[...] = dyb
        dh = jnp.zeros((tm, D_MODEL), F32)
        for c in range(n_fc):
            cols = slice(c * fc, (c + 1) * fc)
            dup = (_nt(dyb, wd_s[cols, :]) * (2.0 * relu_s[:, cols])).astype(BF16)
            dup_ref[:, cols] = dup
            dh = dh + _nt(dup, wu_s[:, cols])
        dxn, gterm = _rms_bwd(dh, x2v, r, g2v)
        dx2_ref[...] = dy + dxn
        dg2_ref[...] = dg2_ref[...] + _colsum(gterm)

    row = lambda i: (i, 0)
    const = lambda i: (0, 0)
    any_spec = pl.BlockSpec(memory_space=pl.ANY)
    return pl.pallas_call(
        body, name="mlp", grid=(t_len // tm,),
        in_specs=[pl.BlockSpec((tm, D_MODEL), row), pl.BlockSpec((tm, D_MODEL), row), pl.BlockSpec((1, D_MODEL), const),
                  any_spec, any_spec],
        out_specs=[pl.BlockSpec((tm, D_MODEL), row), pl.BlockSpec((tm, D_FF), row), pl.BlockSpec((tm, D_FF), row),
                   pl.BlockSpec((tm, D_MODEL), row), pl.BlockSpec((tm, D_MODEL), row), pl.BlockSpec((1, D_MODEL), const),
                   pl.BlockSpec((1, LANES), const)],
        out_shape=[jax.ShapeDtypeStruct((t_len, D_MODEL), F32), jax.ShapeDtypeStruct((t_len, D_FF), BF16),
                   jax.ShapeDtypeStruct((t_len, D_FF), BF16), jax.ShapeDtypeStruct((t_len, D_MODEL), BF16),
                   jax.ShapeDtypeStruct((t_len, D_MODEL), BF16), jax.ShapeDtypeStruct((1, D_MODEL), F32),
                   jax.ShapeDtypeStruct((1, LANES), F32)],
        scratch_shapes=[pltpu.VMEM((D_MODEL, D_FF), BF16), pltpu.VMEM((D_FF, D_MODEL), BF16),
                        pltpu.VMEM((tm, D_FF), F32), pltpu.SemaphoreType.DMA((2,))],
        compiler_params=_params("arbitrary"),
    )(x2, target, g2, wu, wd)


def _mix_bwd(dx2, attn, y, wg, glu_b, ga, gs, wo):
    t_len = dx2.shape[0]
    tm = 512

    def body(dx2_ref, attn_ref, y_ref, wg_ref, b_ref, ga_ref, gs_ref, wo_ref,
             dattn_ref, dy_ref, dx2b_ref, dgp_ref, dga_ref, dgs_ref, db_ref):
        @pl.when(pl.program_id(0) == 0)
        def _():
            dga_ref[...] = jnp.zeros_like(dga_ref)
            dgs_ref[...] = jnp.zeros_like(dgs_ref)
            db_ref[...] = jnp.zeros_like(db_ref)

        dx2b = dx2_ref[...].astype(BF16)
        dx2b_ref[...] = dx2b
        d_an = _nt(dx2b, wo_ref[0:ATTN_W, :])
        d_sn = _nt(dx2b, wo_ref[ATTN_W:, :])
        yv = y_ref[...]
        wg = wg_ref[...]
        z, t, sg = _glu(yv, wg, b_ref[...])
        s = z * sg
        av = attn_ref[...]
        d_attn, ga_term = _rms_bwd(d_an, av, _rms(av), ga_ref[...])
        d_s, gs_term = _rms_bwd(d_sn, s, _rms(s), gs_ref[...])
        dattn_ref[...] = d_attn
        dgp = d_s * z * sg * (1.0 - sg)
        dgpb = dgp.astype(BF16)
        dgp_ref[...] = dgpb
        dz = d_s * sg + _nt(dgpb, wg)
        dy_ref[...] = dz * _gelu_grad(yv, t)
        dga_ref[...] = dga_ref[...] + _colsum(ga_term)
        dgs_ref[...] = dgs_ref[...] + _colsum(gs_term)
        db_ref[...] = db_ref[...] + _colsum(dgp)

    row = lambda i: (i, 0)
    const = lambda i: (0, 0)
    vec = jax.ShapeDtypeStruct((1, SSM_W), F32)
    return pl.pallas_call(
        body, name="mix_bwd", grid=(t_len // tm,),
        in_specs=[pl.BlockSpec((tm, D_MODEL), row), pl.BlockSpec((tm, ATTN_W), row), pl.BlockSpec((tm, SSM_W), row),
                  pl.BlockSpec((SSM_W, SSM_W), const), pl.BlockSpec((1, SSM_W), const), pl.BlockSpec((1, ATTN_W), const),
                  pl.BlockSpec((1, SSM_W), const), pl.BlockSpec((D_MODEL, D_MODEL), const)],
        out_specs=[pl.BlockSpec((tm, ATTN_W), row), pl.BlockSpec((tm, SSM_W), row), pl.BlockSpec((tm, D_MODEL), row),
                   pl.BlockSpec((tm, SSM_W), row), pl.BlockSpec((1, ATTN_W), const), pl.BlockSpec((1, SSM_W), const),
                   pl.BlockSpec((1, SSM_W), const)],
        out_shape=[jax.ShapeDtypeStruct((t_len, ATTN_W), F32), jax.ShapeDtypeStruct((t_len, SSM_W), F32),
                   jax.ShapeDtypeStruct((t_len, D_MODEL), BF16), jax.ShapeDtypeStruct((t_len, SSM_W), BF16), vec, vec, vec],
        compiler_params=_params("arbitrary"),
    )(dx2, attn, y, wg, glu_b, ga, gs, wo)


def _inproj_bwd(dqs, dkn, dv, du, q_raw, k_raw, x, dx2, wi, g1, gq, gk, ones64):
    t_len = x.shape[0]
    tm = 512
    n_heads = ATTN_W // HEAD

    def body(dqs_ref, dkn_ref, dv_ref, du_ref, q_ref, k_ref, x_ref, dx2_ref, wi_ref, g1_ref, gq_ref, gk_ref, bd_ref,
             gx_ref, dproj_ref, dg1_ref, dgq_ref, dgk_ref, accq, acck):
        i = pl.program_id(0)

        @pl.when(i == 0)
        def _():
            dg1_ref[...] = jnp.zeros_like(dg1_ref)
            accq[...] = jnp.zeros_like(accq)
            acck[...] = jnp.zeros_like(acck)

        bd = bd_ref[...]

        def head_norm_bwd(dy, raw, gain, acc):
            r = lax.rsqrt(_group_mean(raw * raw, bd, HEAD) + EPS)
            xh = raw * r
            dxh = dy * gain
            acc[...] = acc[...] + _colsum(dy * xh)
            return r * (dxh - xh * _group_mean(dxh * xh, bd, HEAD))

        dq = head_norm_bwd(dqs_ref[...] * (HEAD ** -0.5), q_ref[...], gq_ref[...], accq)
        dk = head_norm_bwd(dkn_ref[...], k_ref[...], gk_ref[...], acck)
        dproj_ref[:, 0:ATTN_W] = dq.astype(BF16)
        dproj_ref[:, ATTN_W:2 * ATTN_W] = dk.astype(BF16)
        dproj_ref[:, 2 * ATTN_W:3 * ATTN_W] = dv_ref[...].astype(BF16)
        dproj_ref[:, 3 * ATTN_W:] = du_ref[...].astype(BF16)
        dxn = _nt(dproj_ref[...], wi_ref[...])
        xv = x_ref[...]
        g1v = g1_ref[...]
        dx, g1_term = _rms_bwd(dxn, xv, _rms(xv), g1v)
        gx_ref[...] = dx2_ref[...] + dx
        dg1_ref[...] = dg1_ref[...] + _colsum(g1_term)

        @pl.when(i == pl.num_programs(0) - 1)
        def _():
            for acc, out in ((accq, dgq_ref), (acck, dgk_ref)):
                tot = acc[:, 0:HEAD]
                for h in range(1, n_heads):
                    tot = tot + acc[:, h * HEAD:(h + 1) * HEAD]
                out[...] = tot

    row = lambda i: (i, 0)
    const = lambda i: (0, 0)
    aw = pl.BlockSpec((tm, ATTN_W), row)
    dm = pl.BlockSpec((tm, D_MODEL), row)
    return pl.pallas_call(
        body, name="inproj_bwd", grid=(t_len // tm,),
        in_specs=[aw, aw, aw, aw, aw, aw, dm, dm, pl.BlockSpec((D_MODEL, PROJ_W), const), pl.BlockSpec((1, D_MODEL), const),
                  pl.BlockSpec((1, ATTN_W), const), pl.BlockSpec((1, ATTN_W), const), pl.BlockSpec((ATTN_W, ATTN_W), const)],
        out_specs=[dm, pl.BlockSpec((tm, PROJ_W), row), pl.BlockSpec((1, D_MODEL), const),
                   pl.BlockSpec((1, HEAD), const), pl.BlockSpec((1, HEAD), const)],
        out_shape=[jax.ShapeDtypeStruct((t_len, D_MODEL), F32), jax.ShapeDtypeStruct((t_len, PROJ_W), BF16),
                   jax.ShapeDtypeStruct((1, D_MODEL), F32), jax.ShapeDtypeStruct((1, HEAD), F32),
                   jax.ShapeDtypeStruct((1, HEAD), F32)],
        scratch_shapes=[pltpu.VMEM((1, ATTN_W), F32), pltpu.VMEM((1, ATTN_W), F32)],
        compiler_params=_params("arbitrary"),
    )(dqs, dkn, dv, du, q_raw, k_raw, x, dx2, wi, g1, gq, gk, ones64)


def _grad_matmul(a, b, name):
    t_len, m = a.shape
    n = b.shape[1]
    bm, bn, bt = min(m, 1024), min(n, 1024), 1024

    def body(a_ref, b_ref, o_ref):
        @pl.when(pl.program_id(2) == 0)
        def _():
            o_ref[...] = jnp.zeros_like(o_ref)

        o_ref[...] = o_ref[...] + _tn(a_ref[...], b_ref[...])

    return pl.pallas_call(
        body, name=name, grid=(m // bm, n // bn, t_len // bt),
        in_specs=[pl.BlockSpec((bt, bm), lambda i, j, k: (k, i)), pl.BlockSpec((bt, bn), lambda i, j, k: (k, j))],
        out_specs=pl.BlockSpec((bm, bn), lambda i, j, k: (i, j)),
        out_shape=jax.ShapeDtypeStruct((m, n), F32),
        compiler_params=_params("arbitrary", "arbitrary", "arbitrary"),
    )(a, b)


def _adamw(w, g, m, v, name):
    rows, cols = w.shape
    br = _row_block(rows, 256)

    def body(w_ref, g_ref, m_ref, v_ref, d_o, m_o, v_o):
        gv = g_ref[...]
        mn = ADAM_B1 * m_ref[...] + (1.0 - ADAM_B1) * gv
        vn = ADAM_B2 * v_ref[...] + (1.0 - ADAM_B2) * jnp.square(gv)
        m_hat = mn / (1.0 - ADAM_B1 ** ADAM_STEP)
        v_hat = vn / (1.0 - ADAM_B2 ** ADAM_STEP)
        d_o[...] = -ADAM_LR * (m_hat / (jnp.sqrt(v_hat) + ADAM_EPS) + ADAM_WD * w_ref[...])
        m_o[...] = mn
        v_o[...] = vn

    spec = pl.BlockSpec((br, cols), lambda i: (i, 0))
    shape = jax.ShapeDtypeStruct((rows, cols), F32)
    return pl.pallas_call(
        body, name=name, grid=(rows // br,), in_specs=[spec] * 4, out_specs=[spec] * 3, out_shape=[shape] * 3,
        compiler_params=_params("arbitrary"),
    )(w, g, m, v)


def _sum_arrays(arrs, name):
    rows, cols = arrs[0].shape
    br = _row_block(rows, 512)
    n = len(arrs)

    def body(*refs):
        tot = refs[0][...]
        for r in refs[1:n]:
            tot = tot + r[...]
        refs[n][...] = tot

    spec = pl.BlockSpec((br, cols), lambda i: (i, 0))
    return pl.pallas_call(
        body, name=name, grid=(rows // br,), in_specs=[spec] * n, out_specs=spec,
        out_shape=jax.ShapeDtypeStruct((rows, cols), F32), compiler_params=_params("arbitrary"),
    )(*arrs)


GPL = N_GROUPS // N_LB
SW = GPL * N_STATE


def _eye_groups():
    return jnp.eye(GPL, dtype=F32)


def _s5_matrices(ab_r, ab_i, bb_r, bb_i, c_re, c_im, d_skip):
    eye = _eye_groups()
    a_cat = jnp.concatenate([ab_r.reshape(N_LB, HALF, LANES), ab_i.reshape(N_LB, HALF, LANES)], axis=1)

    def b_part(bb):
        b4 = jnp.transpose(bb.reshape(N_LB, GPL, N_STATE, GROUP_W), (0, 1, 3, 2))
        return (b4[:, :, :, None, :] * eye[None, :, None, :, None]).reshape(N_LB, LANES, SW)

    def c_part(cc):
        c4 = jnp.transpose(cc.reshape(N_LB, GPL, GROUP_W, N_STATE), (0, 1, 3, 2))
        return (c4[:, :, :, None, :] * eye[None, :, None, :, None]).reshape(N_LB, SW, LANES)

    b_mat = jnp.concatenate([b_part(bb_r), b_part(bb_i)], axis=2).astype(BF16)
    c_mat = jnp.concatenate([c_part(c_re), -c_part(c_im)], axis=1).astype(BF16)
    return a_cat, b_mat, c_mat, d_skip.reshape(N_LB, 1, LANES)


def _s5_unpack_grads(db, dc, da, dd):
    eye = _eye_groups()
    mask = eye[None, :, None, None, :, None]
    d6 = jnp.sum(db.reshape(N_LB, GPL, GROUP_W, 2, GPL, N_STATE) * mask, axis=4)
    dbb = jnp.transpose(d6, (3, 0, 1, 4, 2)).reshape(2, N_GROUPS * N_STATE, GROUP_W)
    c6 = jnp.sum(dc.reshape(N_LB, GPL, GROUP_W, 2, GPL, N_STATE) * mask, axis=4)
    dcc = jnp.transpose(c6, (3, 0, 1, 2, 4)).reshape(2, N_GROUPS, GROUP_W, N_STATE)
    dab_r = da[:, :HALF].reshape(N_GROUPS * N_STATE, 1)
    dab_i = da[:, HALF:].reshape(N_GROUPS * N_STATE, 1)
    return dab_r, dab_i, dbb[0], dbb[1], dcc[0], -dcc[1], dd.reshape(N_GROUPS, GROUP_W)


def _block_ones(n, width):
    i = lax.broadcasted_iota(jnp.int32, (n, n), 0) // width
    j = lax.broadcasted_iota(jnp.int32, (n, n), 1) // width
    return (i == j).astype(BF16)


def _tile_heads(g):
    return jnp.tile(g.reshape(1, HEAD), (1, ATTN_W // HEAD))


def _local_step(x, target, wi, wg, wo, wu, wd, p):
    ones64 = _block_ones(ATTN_W, HEAD)
    ones_hp = _block_ones(LANES, HEAD)
    g1 = p["norm1_g"].reshape(1, D_MODEL)
    g2 = p["norm2_g"].reshape(1, D_MODEL)
    gq = _tile_heads(p["q_norm_g"])
    gk = _tile_heads(p["k_norm_g"])
    ga = p["attn_out_norm_g"].reshape(1, ATTN_W)
    gs = p["ssm_out_norm_g"].reshape(1, SSM_W)
    glu_b = p["glu_b"].reshape(1, SSM_W)
    n_gp = N_GROUPS * N_STATE
    lr = p["ssm_a_re"].reshape(n_gp, 1)
    li = p["ssm_a_im"].reshape(n_gp, 1)
    ldt = jnp.repeat(p["ssm_log_dt"].reshape(N_GROUPS), N_STATE).reshape(n_gp, 1)
    br = p["ssm_b_re"].reshape(n_gp, GROUP_W)
    bi = p["ssm_b_im"].reshape(n_gp, GROUP_W)
    ab_r, ab_i, bb_r, bb_i = _disc_fwd(lr, li, ldt, br, bi)
    a_cat, b_mat, c_mat, d_mat = _s5_matrices(
        ab_r, ab_i, bb_r, bb_i, p["ssm_c_re"].reshape(N_GROUPS, GROUP_W, N_STATE),
        p["ssm_c_im"].reshape(N_GROUPS, GROUP_W, N_STATE), p["ssm_d"])

    xn, qn, kn, vv, u, q_raw, k_raw = _inproj_fwd(x, g1, wi, gq, gk, ones64)
    attn, lse = _attn_fwd(qn, kn, vv)
    y, states = _s5_fwd(u, a_cat, b_mat, c_mat, d_mat)
    x2, mix, z = _mix_fwd(attn, y, x, wg, glu_b, ga, gs, wo)
    dx2, hdn, dup, h, dyb, dg2, loss = _mlp(x2, target, g2, wu, wd)
    d_attn, dy_ssm, dx2b, dgp, dga, dgs, dglu_b = _mix_bwd(dx2, attn, y, wg, glu_b, ga, gs, wo)
    dqs, dkn, dvv = _attn_bwd(qn, kn, vv, attn, d_attn, lse, ones_hp)
    du, db, dc, da, dd = _s5_bwd(u, dy_ssm, states, a_cat, b_mat, c_mat, d_mat)
    grad_x, dproj, dg1, dgq, dgk = _inproj_bwd(dqs, dkn, dvv, du, q_raw, k_raw, x, dx2, wi, g1, gq, gk, ones64)

    big = {
        "w_in": _grad_matmul(xn, dproj, "grad_w_in"),
        "glu_w": _grad_matmul(z, dgp, "grad_glu_w"),
        "w_out": _grad_matmul(mix, dx2b, "grad_w_out"),
        "w_mlp_up": _grad_matmul(h, dup, "grad_w_mlp_up"),
        "w_mlp_down": _grad_matmul(hdn, dyb, "grad_w_mlp_down"),
    }
    dab_r, dab_i, dbb_r, dbb_i, dc_re, dc_im, dd_g = _s5_unpack_grads(db, dc, da, dd)
    cot = {"norm1_g": dg1, "q_norm_g": dgq, "k_norm_g": dgk, "ab_r": dab_r, "ab_i": dab_i, "bb_r": dbb_r, "bb_i": dbb_i,
           "ssm_c_re": dc_re, "ssm_c_im": dc_im, "ssm_d": dd_g, "glu_b": dglu_b, "attn_out_norm_g": dga,
           "ssm_out_norm_g": dgs, "norm2_g": dg2}
    return loss[0, 0], grad_x, big, cot, (lr, li, ldt, br, bi)


COT_NAMES = ("norm1_g", "q_norm_g", "k_norm_g", "ab_r", "ab_i", "bb_r", "bb_i", "ssm_c_re", "ssm_c_im", "ssm_d",
             "glu_b", "attn_out_norm_g", "ssm_out_norm_g", "norm2_g")
SMALL_NAMES = ("norm1_g", "q_norm_g", "k_norm_g", "ssm_a_re", "ssm_a_im", "ssm_log_dt", "ssm_b_re", "ssm_b_im",
               "ssm_c_re", "ssm_c_im", "ssm_d", "glu_b", "attn_out_norm_g", "ssm_out_norm_g", "norm2_g")
BIG_NAMES = ("w_in", "glu_w", "w_out", "w_mlp_up", "w_mlp_down")
PACK_ROWS = 1152


def _pack(arrs):
    flat = jnp.concatenate([a.reshape(-1) for a in arrs])
    return jnp.pad(flat, (0, PACK_ROWS * LANES - flat.shape[0])).reshape(PACK_ROWS, LANES)


def _unpack(packed, like):
    flat = packed.reshape(-1)
    out, pos = [], 0
    for a in like:
        out.append(flat[pos:pos + a.size].reshape(a.shape))
        pos += a.size
    return out


def _small_grads(cot, disc_in, p):
    lr, li, ldt, br, bi = disc_in
    group_sum = (lax.broadcasted_iota(jnp.int32, (N_GROUPS, N_GROUPS * N_STATE), 1) // N_STATE
                 == lax.broadcasted_iota(jnp.int32, (N_GROUPS, N_GROUPS * N_STATE), 0)).astype(F32)
    dlr, dli, dldt, dbr, dbi = _disc_bwd(lr, li, ldt, br, bi, cot["ab_r"], cot["ab_i"], cot["bb_r"], cot["bb_i"], group_sum)
    g = dict(cot)
    g.update(ssm_a_re=dlr, ssm_a_im=dli, ssm_log_dt=dldt[:, 0], ssm_b_re=dbr, ssm_b_im=dbi)
    return {n: g[n].reshape(p[n].shape) for n in SMALL_NAMES}


BIG = {
    "w_in": ((D_MODEL, PROJ_W), 1, PROJ_W // 4, 0, D_MODEL // 2),
    "glu_w": ((SSM_W, SSM_W), 0, SSM_W // 4, 1, SSM_W // 2),
    "w_out": ((D_MODEL, D_MODEL), 0, D_MODEL // 4, 1, D_MODEL // 2),
    "w_mlp_up": ((D_MODEL, D_FF), 1, D_FF // 4, 0, D_MODEL // 2),
    "w_mlp_down": ((D_FF, D_MODEL), 0, D_FF // 4, 1, D_MODEL // 2),
}
N_BIG = len(BIG_NAMES)
N_CHIPS = 4
ANY = pl.BlockSpec(memory_space=pl.ANY)


def _cut(name, shard=False, half=False):
    shape, s_ax, s_sz, h_ax, h_sz = BIG[name]
    shape = list(shape)
    if shard:
        shape[s_ax] = s_sz
    if half:
        shape[h_ax] = h_sz
    return tuple(shape)


def _window(name, base, shard=None, half=None):
    _, s_ax, s_sz, h_ax, h_sz = BIG[name]
    idx = [pl.ds(0, base[0]), pl.ds(0, base[1])]
    if shard is not None:
        idx[s_ax] = pl.ds(pl.multiple_of(shard * s_sz, s_sz), s_sz)
    if half is not None:
        idx[h_ax] = pl.ds(pl.multiple_of(half * h_sz, h_sz), h_sz)
    return tuple(idx)


def _mesh_pos():
    return lax.axis_index("x"), lax.axis_index("y"), lax.axis_index("c")


def _other_chips(x, y):
    return [(1 - x, y, 2 * (1 - x) + y), (x, 1 - y, 2 * x + 1 - y), (1 - x, 1 - y, 2 * (1 - x) + 1 - y)]


def _remote(src, dst, send_sem, recv_sem, dev):
    return pltpu.make_async_remote_copy(src_ref=src, dst_ref=dst, send_sem=send_sem, recv_sem=recv_sem,
                                        device_id=dev, device_id_type=MESH)


def _gather_weights(shards):
    def body(*refs):
        ins, outs, stage = refs[0:N_BIG], refs[N_BIG:2 * N_BIG], refs[2 * N_BIG:3 * N_BIG]
        send, recv, fsend, frecv, lsem = refs[3 * N_BIG:]
        x, y, c = _mesh_pos()
        me = 2 * x + y
        sib = (x, y, 1 - c)
        chips = _other_chips(x, y)
        pending = []
        for w, n in enumerate(BIG_NAMES):
            stage[w][...] = ins[w][...].astype(BF16)
            full = BIG[n][0]
            cp = pltpu.make_async_copy(stage[w], outs[w].at[_window(n, full, shard=me)], lsem.at[w])
            cp.start()
            pending.append(cp)
        sends = []
        for k, (px, py, _) in enumerate(chips):
            for w, n in enumerate(BIG_NAMES):
                s = k * N_BIG + w
                cp = _remote(stage[w].at[_window(n, _cut(n, shard=True), half=c)],
                             outs[w].at[_window(n, BIG[n][0], shard=me, half=c)], send.at[s], recv.at[s], (px, py, c))
                cp.start()
                sends.append(cp)
        for k, (px, py, pj) in enumerate(chips):
            for w, n in enumerate(BIG_NAMES):
                s = k * N_BIG + w
                got = outs[w].at[_window(n, BIG[n][0], shard=pj, half=c)]
                _remote(got, got, send.at[s], recv.at[s], (px, py, c)).wait_recv()
                cp = _remote(got, got, fsend.at[s], frecv.at[s], sib)
                cp.start()
                sends.append(cp)
        for k, (px, py, pj) in enumerate(chips):
            for w, n in enumerate(BIG_NAMES):
                s = k * N_BIG + w
                got = outs[w].at[_window(n, BIG[n][0], shard=pj, half=1 - c)]
                _remote(got, got, fsend.at[s], frecv.at[s], sib).wait_recv()
        for cp in sends:
            cp.wait_send()
        for cp in pending:
            cp.wait()

    n_sem = (N_CHIPS - 1) * N_BIG
    outs = pl.pallas_call(
        body, name="gather_weights",
        in_specs=[pl.BlockSpec(memory_space=pltpu.VMEM)] * N_BIG, out_specs=[ANY] * N_BIG,
        out_shape=[jax.ShapeDtypeStruct(BIG[n][0], BF16) for n in BIG_NAMES],
        scratch_shapes=[pltpu.VMEM(_cut(n, shard=True), BF16) for n in BIG_NAMES]
        + [pltpu.SemaphoreType.DMA((n_sem,))] * 4 + [pltpu.SemaphoreType.DMA((N_BIG,))],
        compiler_params=pltpu.CompilerParams(vmem_limit_bytes=VMEM_LIMIT),
    )(*[shards[n] for n in BIG_NAMES])
    return outs


def _pair_exchange(big, packed):
    def body(*refs):
        ins, small = refs[0:N_BIG], refs[N_BIG]
        own = refs[N_BIG + 1:2 * N_BIG + 1]
        got = refs[2 * N_BIG + 1:3 * N_BIG + 2]
        send, recv, lsem = refs[3 * N_BIG + 2:]
        x, y, c = _mesh_pos()
        sib = (x, y, 1 - c)
        copies = []
        for w, n in enumerate(BIG_NAMES):
            full = BIG[n][0]
            lc = pltpu.make_async_copy(ins[w].at[_window(n, full, half=c)], own[w], lsem.at[w])
            rc = _remote(ins[w].at[_window(n, full, half=1 - c)], got[w], send.at[w], recv.at[w], sib)
            lc.start()
            rc.start()
            copies += [lc, rc]
        rc = _remote(small, got[N_BIG], send.at[N_BIG], recv.at[N_BIG], sib)
        rc.start()
        copies.append(rc)
        for cp in copies:
            cp.wait()

    halves = [jax.ShapeDtypeStruct(_cut(n, half=True), F32) for n in BIG_NAMES]
    outs = pl.pallas_call(
        body, name="grad_pair_exchange", in_specs=[ANY] * (N_BIG + 1), out_specs=[ANY] * (2 * N_BIG + 1),
        out_shape=halves + halves + [jax.ShapeDtypeStruct(packed.shape, F32)],
        scratch_shapes=[pltpu.SemaphoreType.DMA((N_BIG + 1,)), pltpu.SemaphoreType.DMA((N_BIG + 1,)),
                        pltpu.SemaphoreType.DMA((N_BIG,))],
    )(*[big[n] for n in BIG_NAMES], packed)
    return outs[0:N_BIG], outs[N_BIG:2 * N_BIG], outs[2 * N_BIG]


def _chip_exchange(halves, packed):
    n_all = N_BIG + 1

    def body(*refs):
        ins, outs = refs[0:n_all], refs[n_all:2 * n_all]
        send, recv, lsem = refs[2 * n_all:]
        x, y, c = _mesh_pos()
        me = 2 * x + y
        chips = _other_chips(x, y)

        def piece(w, shard):
            if w == N_BIG:
                return ins[w]
            n = BIG_NAMES[w]
            return ins[w].at[_window(n, _cut(n, half=True), shard=shard)]

        copies = []
        for w in range(n_all):
            cp = pltpu.make_async_copy(piece(w, me), outs[w].at[me], lsem.at[w])
            cp.start()
            copies.append(cp)
        for k, (px, py, pj) in enumerate(chips):
            for w in range(n_all):
                s = k * n_all + w
                cp = _remote(piece(w, pj), outs[w].at[me], send.at[s], recv.at[s], (px, py, c))
                cp.start()
                copies.append(cp)
        for k, (px, py, pj) in enumerate(chips):
            for w in range(n_all):
                s = k * n_all + w
                _remote(piece(w, me), outs[w].at[pj], send.at[s], recv.at[s], (px, py, c)).wait_recv()
        for i, cp in enumerate(copies):
            if i < n_all:
                cp.wait()
            else:
                cp.wait_send()

    shapes = [jax.ShapeDtypeStruct((N_CHIPS,) + _cut(n, shard=True, half=True), F32) for n in BIG_NAMES]
    shapes.append(jax.ShapeDtypeStruct((N_CHIPS,) + packed.shape, F32))
    n_sem = (N_CHIPS - 1) * n_all
    return pl.pallas_call(
        body, name="grad_chip_exchange", in_specs=[ANY] * n_all, out_specs=[ANY] * n_all, out_shape=shapes,
        scratch_shapes=[pltpu.SemaphoreType.DMA((n_sem,)), pltpu.SemaphoreType.DMA((n_sem,)),
                        pltpu.SemaphoreType.DMA((n_all,))],
    )(*halves, packed)


def _half_exchange(pieces):
    def body(*refs):
        ins, outs = refs[0:N_BIG], refs[N_BIG:2 * N_BIG]
        send, recv, lsem = refs[2 * N_BIG:]
        x, y, c = _mesh_pos()
        sib = (x, y, 1 - c)
        copies = []
        for w, n in enumerate(BIG_NAMES):
            mine = outs[w].at[_window(n, _cut(n, shard=True), half=c)]
            lc = pltpu.make_async_copy(ins[w], mine, lsem.at[w])
            rc = _remote(ins[w], mine, send.at[w], recv.at[w], sib)
            lc.start()
            rc.start()
            copies += [lc, rc]
        for w, n in enumerate(BIG_NAMES):
            theirs = outs[w].at[_window(n, _cut(n, shard=True), half=1 - c)]
            _remote(ins[w], theirs, send.at[w], recv.at[w], sib).wait_recv()
        for i, cp in enumerate(copies):
            if i % 2 == 0:
                cp.wait()
            else:
                cp.wait_send()

    return pl.pallas_call(
        body, name="grad_half_exchange", in_specs=[ANY] * N_BIG, out_specs=[ANY] * N_BIG,
        out_shape=[jax.ShapeDtypeStruct(_cut(n, shard=True), F32) for n in BIG_NAMES],
        scratch_shapes=[pltpu.SemaphoreType.DMA((N_BIG,)), pltpu.SemaphoreType.DMA((N_BIG,)),
                        pltpu.SemaphoreType.DMA((N_BIG,))],
    )(*pieces)


def _sum_slots(q, name):
    slots, rows, cols = q.shape
    br = _row_block(rows, 512)

    def body(*refs):
        tot = refs[0][...]
        for r in refs[1:slots]:
            tot = tot + r[...]
        refs[slots][...] = tot

    specs = [pl.BlockSpec((None, br, cols), functools.partial(lambda i, s: (s, i, 0), s=s)) for s in range(slots)]
    return pl.pallas_call(
        body, name=name, grid=(rows // br,), in_specs=specs, out_specs=pl.BlockSpec((br, cols), lambda i: (i, 0)),
        out_shape=jax.ShapeDtypeStruct((rows, cols), F32), compiler_params=_params("arbitrary"),
    )(*([q] * slots))


WEIGHT_NAMES = ("norm1_g", "w_in", "q_norm_g", "k_norm_g", "ssm_a_re", "ssm_a_im", "ssm_log_dt", "ssm_b_re", "ssm_b_im",
                "ssm_c_re", "ssm_c_im", "ssm_d", "glu_w", "glu_b", "attn_out_norm_g", "ssm_out_norm_g", "w_out", "norm2_g",
                "w_mlp_up", "w_mlp_down")


def _train_step(a):
    x = a["x"][0]
    target = a["loss_target"][0]
    wi, wg, wo, wu, wd = _gather_weights({n: a[n][0] for n in BIG_NAMES})
    p = {n: a[n][0] for n in SMALL_NAMES}
    loss, grad_x, big, cot, disc_in = _local_step(x, target, wi, wg, wo, wu, wd, p)

    cot_list = [cot[n] for n in COT_NAMES]
    packed = _pack(cot_list)
    own, got, got_packed = _pair_exchange(big, packed)
    chip = [_sum_arrays([o, g], "pair_sum_" + n) for n, o, g in zip(BIG_NAMES, own, got)]
    chip_packed = _sum_arrays([packed, got_packed], "pair_sum_small")
    slots = _chip_exchange(chip, chip_packed)
    pieces = [_sum_slots(q, "chip_sum_" + n) for n, q in zip(BIG_NAMES, slots[:N_BIG])]
    small_sum = _sum_slots(slots[N_BIG], "chip_sum_small")
    shard_grads = dict(zip(BIG_NAMES, _half_exchange(pieces)))
    small_grads = _small_grads(dict(zip(COT_NAMES, _unpack(small_sum, cot_list))), disc_in, p)

    grads, delta, new_m, new_v = {}, {}, {}, {}
    for n in BIG_NAMES:
        grads[n] = shard_grads[n]
        delta[n], new_m[n], new_v[n] = _adamw(a[n][0], grads[n], a["m_" + n][0], a["v_" + n][0], "adamw_" + n)
    small_w = [p[n] for n in SMALL_NAMES]
    res = _adamw(_pack(small_w), _pack([small_grads[n] for n in SMALL_NAMES]), _pack([a["m_" + n][0] for n in SMALL_NAMES]),
                 _pack([a["v_" + n][0] for n in SMALL_NAMES]), "adamw_small")
    for store, packed_out in zip((delta, new_m, new_v), res):
        store.update(zip(SMALL_NAMES, _unpack(packed_out, small_w)))
    grads.update(small_grads)

    total = lax.psum(loss, ("x", "y", "c"))
    out = [total, grad_x[None]]
    for store in (grads, delta, new_m, new_v):
        out += [store[n].reshape(a[n].shape) for n in WEIGHT_NAMES]
    return tuple(out)


def kernel(x, norm1_g, w_in, q_norm_g, k_norm_g, ssm_a_re, ssm_a_im, ssm_log_dt, ssm_b_re, ssm_b_im, ssm_c_re, ssm_c_im, ssm_d, glu_w, glu_b, attn_out_norm_g, ssm_out_norm_g, w_out, norm2_g, w_mlp_up, w_mlp_down, loss_target, m_norm1_g, m_w_in, m_q_norm_g, m_k_norm_g, m_ssm_a_re, m_ssm_a_im, m_ssm_log_dt, m_ssm_b_re, m_ssm_b_im, m_ssm_c_re, m_ssm_c_im, m_ssm_d, m_glu_w, m_glu_b, m_attn_out_norm_g, m_ssm_out_norm_g, m_w_out, m_norm2_g, m_w_mlp_up, m_w_mlp_down, v_norm1_g, v_w_in, v_q_norm_g, v_k_norm_g, v_ssm_a_re, v_ssm_a_im, v_ssm_log_dt, v_ssm_b_re, v_ssm_b_im, v_ssm_c_re, v_ssm_c_im, v_ssm_d, v_glu_w, v_glu_b, v_attn_out_norm_g, v_ssm_out_norm_g, v_w_out, v_norm2_g, v_w_mlp_up, v_w_mlp_down):
    return _train_step(dict(locals()))
```

```python
import functools
import math

import jax
import jax.numpy as jnp
from jax import lax
from jax.experimental import pallas as pl
from jax.experimental.pallas import tpu as pltpu

F32 = jnp.float32
BF16 = jnp.bfloat16
MESH = pl.DeviceIdType.MESH

D_MODEL = 1024
ATTN_W = 512
SSM_W = 512
HEAD = 64
D_FF = 4096
PROJ_W = 2048
N_GROUPS = 32
N_STATE = 64
GROUP_W = 16
EPS = 1e-6
NEG = -1e30
DILATIONS = (1, 4, 16)
BLK = 128
TILE = 2048
LANES = 128
N_LB = SSM_W // LANES
N_SLAB = 2 * N_LB * N_STATE * 8 // LANES // N_LB
VMEM_LIMIT = 56 * 1024 * 1024

ADAM_LR, ADAM_B1, ADAM_B2, ADAM_EPS, ADAM_WD, ADAM_STEP = 0.001, 0.9, 0.999, 1e-08, 0.01, 10


def _params(*sem):
    return pltpu.CompilerParams(dimension_semantics=sem, vmem_limit_bytes=VMEM_LIMIT)


def _nt(a, b):
    return lax.dot_general(a, b, (((1,), (1,)), ((), ())), preferred_element_type=F32)


def _tn(a, b):
    return lax.dot_general(a, b, (((0,), (0,)), ((), ())), preferred_element_type=F32)


def _mm(a, b):
    return jnp.dot(a, b, preferred_element_type=F32)


def _group_mean(t, ones_bd, width):
    hi = t.astype(BF16)
    lo = (t - hi.astype(F32)).astype(BF16)
    return (_mm(hi, ones_bd) + _mm(lo, ones_bd)) * (1.0 / width)


def _rms(x):
    return lax.rsqrt(jnp.mean(x * x, axis=-1, keepdims=True) + EPS)


def _rms_bwd(dy, x, r, g):
    xh = x * r
    dxh = dy * g
    dx = r * (dxh - xh * jnp.mean(dxh * xh, axis=-1, keepdims=True))
    return dx, dy * xh


def _colsum(x):
    return jnp.sum(x, axis=0, keepdims=True)


def _row_block(rows, cap):
    for b in range(min(rows, cap) // 8 * 8, 0, -8):
        if rows % b == 0:
            return b
    raise ValueError(f"no row block for {rows} rows")


def _inproj_fwd(x, g1, wi, gq, gk, ones64):
    t_len = x.shape[0]
    tm = 512
    n_hp = ATTN_W // LANES

    def body(x_ref, g1_ref, wi_ref, gq_ref, gk_ref, bd_ref, xn_ref, q_ref, k_ref, v_ref, u_ref, qr_ref, kr_ref):
        xv = x_ref[...]
        xn = (xv * _rms(xv) * g1_ref[...]).astype(BF16)
        xn_ref[...] = xn
        proj = _mm(xn, wi_ref[...])
        q = proj[:, 0:ATTN_W]
        k = proj[:, ATTN_W:2 * ATTN_W]
        v = proj[:, 2 * ATTN_W:3 * ATTN_W]
        u_ref[...] = proj[:, 3 * ATTN_W:]
        qr_ref[...] = q
        kr_ref[...] = k
        bd = bd_ref[...]
        qn = q * lax.rsqrt(_group_mean(q * q, bd, HEAD) + EPS) * gq_ref[...] * (HEAD ** -0.5)
        kn = k * lax.rsqrt(_group_mean(k * k, bd, HEAD) + EPS) * gk_ref[...]
        for hp in range(n_hp):
            sl = slice(hp * LANES, (hp + 1) * LANES)
            q_ref[hp] = qn[:, sl]
            k_ref[hp] = kn[:, sl]
            v_ref[hp] = v[:, sl]

    row = lambda i: (i, 0)
    const = lambda i: (0, 0)
    hp_spec = pl.BlockSpec((n_hp, tm, LANES), lambda i: (0, i, 0))
    hp_shape = jax.ShapeDtypeStruct((n_hp, t_len, LANES), F32)
    return pl.pallas_call(
        body, name="inproj_fwd", grid=(t_len // tm,),
        in_specs=[pl.BlockSpec((tm, D_MODEL), row), pl.BlockSpec((1, D_MODEL), const),
                  pl.BlockSpec((D_MODEL, PROJ_W), const), pl.BlockSpec((1, ATTN_W), const),
                  pl.BlockSpec((1, ATTN_W), const), pl.BlockSpec((ATTN_W, ATTN_W), const)],
        out_specs=[pl.BlockSpec((tm, D_MODEL), row), hp_spec, hp_spec, hp_spec,
                   pl.BlockSpec((tm, SSM_W), row), pl.BlockSpec((tm, ATTN_W), row), pl.BlockSpec((tm, ATTN_W), row)],
        out_shape=[jax.ShapeDtypeStruct((t_len, D_MODEL), BF16), hp_shape, hp_shape, hp_shape,
                   jax.ShapeDtypeStruct((t_len, SSM_W), F32), jax.ShapeDtypeStruct((t_len, ATTN_W), F32),
                   jax.ShapeDtypeStruct((t_len, ATTN_W), F32)],
        compiler_params=_params("arbitrary"),
    )(x, g1, wi, gq, gk, ones64)


def _attn_masks():
    head0 = lax.broadcasted_iota(jnp.int32, (BLK, LANES), 1) < HEAD
    row = lax.broadcasted_iota(jnp.int32, (2 * BLK, 2 * BLK), 0) & (BLK - 1)
    col = lax.broadcasted_iota(jnp.int32, (2 * BLK, 2 * BLK), 1)
    return head0, (col < BLK) & (col >= row), (col >= BLK) & (col - BLK <= row)


def _stack_heads(x, head0):
    return jnp.concatenate([jnp.where(head0, x, 0.0), jnp.where(head0, 0.0, x)], axis=0).astype(BF16)


def _unit_rows(uidx, d):
    nb = TILE // (BLK * d)
    r = lax.div(uidx, nb)
    b = lax.rem(uidx, nb)
    start = r + d * BLK * b
    if d == 1:
        start = pl.multiple_of(start, BLK)
        mk = lambda s: pl.ds(pl.multiple_of(s, BLK), BLK)
    else:
        mk = lambda s: pl.ds(s, BLK, stride=d)
    return b, mk(start), mk(TILE + start), mk(TILE + start - d * BLK)


def _attn_fwd(q, k, v):
    n_hp, t_len, _ = q.shape
    nt = t_len // TILE

    def body(q_ref, kp_ref, kc_ref, vp_ref, vc_ref, o_ref, lse_ref, kk, vv, m_s, l_s, acc_s):
        t = pl.program_id(1)
        kk[0:TILE] = kp_ref[0]
        kk[TILE:] = kc_ref[0]
        vv[0:TILE] = vp_ref[0]
        vv[TILE:] = vc_ref[0]
        head0, band_prev, band_cur = _attn_masks()

        for pi, d in enumerate(DILATIONS):
            def unit(uidx, carry, d=d, pi=pi):
                b, rows_q, rows_c, rows_p = _unit_rows(uidx, d)
                mask = band_cur | (band_prev & ((t > 0) | (b > 0)))
                q2 = _stack_heads(q_ref.at[0][rows_q, :], head0)
                kcat = jnp.concatenate([kk[rows_p, :], kk[rows_c, :]], axis=0).astype(BF16)
                vcat = jnp.concatenate([vv[rows_p, :], vv[rows_c, :]], axis=0).astype(BF16)
                s = jnp.where(mask, _nt(q2, kcat), NEG)
                m = jnp.max(s, axis=1, keepdims=True)
                p = jnp.exp(s - m)
                ls = jnp.sum(p, axis=1, keepdims=True)
                pv = _mm(p.astype(BF16), vcat)
                m_s.at[pi][rows_q, :] = jnp.where(head0, m[0:BLK], m[BLK:])
                l_s.at[pi][rows_q, :] = jnp.where(head0, ls[0:BLK], ls[BLK:])
                acc_s.at[pi][rows_q, :] = jnp.where(head0, pv[0:BLK], pv[BLK:])
                return carry

            lax.fori_loop(0, TILE // BLK, unit, 0, unroll=8)

        m_all = jnp.maximum(jnp.maximum(m_s[0], m_s[1]), m_s[2])
        num = jnp.zeros((TILE, LANES), F32)
        den = jnp.zeros((TILE, LANES), F32)
        for pi in range(len(DILATIONS)):
            wgt = jnp.exp(m_s[pi] - m_all)
            num = num + acc_s[pi] * wgt
            den = den + l_s[pi] * wgt
        o_ref[...] = num / den
        lse_ref[0] = m_all + jnp.log(den)

    cur = lambda hp, t: (hp, t, 0)
    prev = lambda hp, t: (hp, jnp.maximum(t - 1, 0), 0)
    blk = (1, TILE, LANES)
    per_pattern = pltpu.VMEM((len(DILATIONS), TILE, LANES), F32)
    return pl.pallas_call(
        body, name="attn_fwd", grid=(n_hp, nt),
        in_specs=[pl.BlockSpec(blk, cur), pl.BlockSpec(blk, prev), pl.BlockSpec(blk, cur),
                  pl.BlockSpec(blk, prev), pl.BlockSpec(blk, cur)],
        out_specs=[pl.BlockSpec((TILE, LANES), lambda hp, t: (t, hp)), pl.BlockSpec(blk, cur)],
        out_shape=[jax.ShapeDtypeStruct((t_len, ATTN_W), F32), jax.ShapeDtypeStruct((n_hp, t_len, LANES), F32)],
        scratch_shapes=[pltpu.VMEM((2 * TILE, LANES), F32), pltpu.VMEM((2 * TILE, LANES), F32),
                        per_pattern, per_pattern, per_pattern],
        compiler_params=_params("arbitrary", "arbitrary"),
    )(q, k, k, v, v)


def _attn_bwd(q, k, v, o, do, lse, ones_hp):
    n_hp, t_len, _ = q.shape
    nt = t_len // TILE

    n_pat = len(DILATIONS)

    def body(q_ref, kp_ref, kc_ref, vp_ref, vc_ref, o_ref, do_ref, lse_ref, bd_ref,
             dq_ref, dk_ref, dv_ref, kk, vv, dq_s, dkc, dkp, dvc, dvp, hold_k, hold_v, dl_s):
        t = pl.program_id(1)

        @pl.when(t < nt)
        def _():
            kk[0:TILE] = kp_ref[0]
            kk[TILE:] = kc_ref[0]
            vv[0:TILE] = vp_ref[0]
            vv[TILE:] = vc_ref[0]
            dl_s[...] = _group_mean(do_ref[...] * o_ref[...], bd_ref[...], 1.0)
            head0, band_prev, band_cur = _attn_masks()

            for pi, d in enumerate(DILATIONS):
                def unit(uidx, carry, d=d, pi=pi):
                    b, rows_q, rows_c, rows_p = _unit_rows(uidx, d)
                    mask = band_cur | (band_prev & ((t > 0) | (b > 0)))
                    q2 = _stack_heads(q_ref.at[0][rows_q, :], head0)
                    do2 = _stack_heads(do_ref[rows_q, :], head0)
                    lse_f = lse_ref.at[0][rows_q, :]
                    dl_f = dl_s[rows_q, :]
                    lse2 = jnp.concatenate([lse_f[:, 0:1], lse_f[:, HEAD:HEAD + 1]], axis=0)
                    dl2 = jnp.concatenate([dl_f[:, 0:1], dl_f[:, HEAD:HEAD + 1]], axis=0)
                    kcat = jnp.concatenate([kk[rows_p, :], kk[rows_c, :]], axis=0).astype(BF16)
                    vcat = jnp.concatenate([vv[rows_p, :], vv[rows_c, :]], axis=0).astype(BF16)
                    p = jnp.where(mask, jnp.exp(_nt(q2, kcat) - lse2), 0.0)
                    ds = (p * (_nt(do2, vcat) - dl2)).astype(BF16)
                    dq2 = _mm(ds, kcat)
                    dq_s.at[pi][rows_q, :] = jnp.where(head0, dq2[0:BLK], dq2[BLK:])
                    dk2 = _tn(ds, q2)
                    dv2 = _tn(p.astype(BF16), do2)
                    dkp.at[pi][rows_q, :] = dk2[0:BLK]
                    dkc.at[pi][rows_q, :] = dk2[BLK:]
                    dvp.at[pi][rows_q, :] = dv2[0:BLK]
                    dvc.at[pi][rows_q, :] = dv2[BLK:]
                    return carry

                lax.fori_loop(0, TILE // BLK, unit, 0, unroll=4)

            dq_ref[...] = dq_s[0] + dq_s[1] + dq_s[2]

        @pl.when(t > 0)
        def _():
            dk_ref[...] = hold_k[...]
            dv_ref[...] = hold_v[...]

        @pl.when((t > 0) & (t < nt))
        def _():
            for pi, d in enumerate(DILATIONS):
                back = d * BLK
                dk_ref[TILE - back:, :] = dk_ref[TILE - back:, :] + dkp[pi, 0:back, :]
                dv_ref[TILE - back:, :] = dv_ref[TILE - back:, :] + dvp[pi, 0:back, :]

        @pl.when(t < nt)
        def _():
            hold_k[...] = dkc[0] + dkc[1] + dkc[2]
            hold_v[...] = dvc[0] + dvc[1] + dvc[2]
            for pi, d in enumerate(DILATIONS):
                back = d * BLK
                if back < TILE:
                    hold_k[0:TILE - back, :] = hold_k[0:TILE - back, :] + dkp[pi, back:, :]
                    hold_v[0:TILE - back, :] = hold_v[0:TILE - back, :] + dvp[pi, back:, :]

    last = nt - 1
    cur = lambda hp, t: (hp, jnp.minimum(t, last), 0)
    prev = lambda hp, t: (hp, jnp.clip(t - 1, 0, last), 0)
    cur2 = lambda hp, t: (jnp.minimum(t, last), hp)
    prev2 = lambda hp, t: (jnp.maximum(t - 1, 0), hp)
    blk = (1, TILE, LANES)
    blk2 = (TILE, LANES)
    out = jax.ShapeDtypeStruct((t_len, ATTN_W), F32)
    return pl.pallas_call(
        body, name="attn_bwd", grid=(n_hp, nt + 1),
        in_specs=[pl.BlockSpec(blk, cur), pl.BlockSpec(blk, prev), pl.BlockSpec(blk, cur),
                  pl.BlockSpec(blk, prev), pl.BlockSpec(blk, cur), pl.BlockSpec(blk2, cur2),
                  pl.BlockSpec(blk2, cur2), pl.BlockSpec(blk, cur), pl.BlockSpec((LANES, LANES), lambda hp, t: (0, 0))],
        out_specs=[pl.BlockSpec(blk2, cur2), pl.BlockSpec(blk2, prev2), pl.BlockSpec(blk2, prev2)],
        out_shape=[out, out, out],
        scratch_shapes=[pltpu.VMEM((2 * TILE, LANES), F32), pltpu.VMEM((2 * TILE, LANES), F32)]
        + [pltpu.VMEM((n_pat, TILE, LANES), F32)] * 5 + [pltpu.VMEM((TILE, LANES), F32)] * 3,
        compiler_params=_params("arbitrary", "arbitrary"),
    )(q, k, k, v, v, o, do, lse, ones_hp)


def _discretise(lr, li, ldt, br, bi):
    dt = jnp.exp(ldt)
    mag = jnp.exp(lr * dt)
    ab_r, ab_i = mag * jnp.cos(li * dt), mag * jnp.sin(li * dt)
    den = lr * lr + li * li
    nr, ni = ab_r - 1.0, ab_i
    cr = (nr * lr + ni * li) / den
    ci = (ni * lr - nr * li) / den
    return ab_r, ab_i, cr * br - ci * bi, cr * bi + ci * br


def _disc_fwd(lr, li, ldt, br, bi):
    def body(lr_ref, li_ref, ldt_ref, br_ref, bi_ref, ar_o, ai_o, bbr_o, bbi_o):
        outs = _discretise(lr_ref[...], li_ref[...], ldt_ref[...], br_ref[...], bi_ref[...])
        for o_ref, val in zip((ar_o, ai_o, bbr_o, bbi_o), outs):
            o_ref[...] = val

    col = jax.ShapeDtypeStruct(lr.shape, F32)
    mat = jax.ShapeDtypeStruct(br.shape, F32)
    return pl.pallas_call(body, name="s5_disc_fwd", out_shape=[col, col, mat, mat])(lr, li, ldt, br, bi)


def _disc_bwd(lr, li, ldt, br, bi, d_ar, d_ai, d_bbr, d_bbi, group_sum):
    def body(lr_ref, li_ref, ldt_ref, br_ref, bi_ref, c1, c2, c3, c4, gs_ref, dlr_o, dli_o, dldt_o, dbr_o, dbi_o):
        _, vjp = jax.vjp(_discretise, lr_ref[...], li_ref[...], ldt_ref[...], br_ref[...], bi_ref[...])
        dlr, dli, dldt, dbr, dbi = vjp((c1[...], c2[...], c3[...], c4[...]))
        dlr_o[...] = dlr
        dli_o[...] = dli
        dbr_o[...] = dbr
        dbi_o[...] = dbi
        wide = jnp.broadcast_to(dldt, (dldt.shape[0], LANES))
        dldt_o[...] = jnp.dot(gs_ref[...], wide, precision=lax.Precision.HIGHEST, preferred_element_type=F32)

    col = jax.ShapeDtypeStruct(lr.shape, F32)
    mat = jax.ShapeDtypeStruct(br.shape, F32)
    return pl.pallas_call(
        body, name="s5_disc_bwd", out_shape=[col, col, jax.ShapeDtypeStruct((N_GROUPS, LANES), F32), mat, mat],
    )(lr, li, ldt, br, bi, d_ar, d_ai, d_bbr, d_bbi, group_sum)


N_CHUNK = TILE // BLK
HALF = 4


def _cmul(ar, ai, xr, xi):
    return ar * xr - ai * xi, ar * xi + ai * xr


def _power_table(a_ref, tab, sign, reverse):
    ar = [a_ref[0, j:j + 1, :] for j in range(HALF)]
    ai = [sign * a_ref[0, HALF + j:HALF + j + 1, :] for j in range(HALF)]

    def step(s, cur):
        row = pl.ds((BLK - 1 - s) if reverse else s, 1)
        nxt = []
        for j in range(HALF):
            tab.at[j][row, :] = cur[j]
            tab.at[HALF + j][row, :] = cur[HALF + j]
            nxt.append(_cmul(ar[j], ai[j], cur[j], cur[HALF + j]))
        return tuple(p[0] for p in nxt) + tuple(p[1] for p in nxt)

    lax.fori_loop(0, BLK, step, tuple(ar) + tuple(ai))


def _chunk_scan(buf, a_ref, sign, reverse):
    ar = [jnp.broadcast_to(a_ref[0, j:j + 1, :], (N_CHUNK, LANES)) for j in range(HALF)]
    ai = [sign * jnp.broadcast_to(a_ref[0, HALF + j:HALF + j + 1, :], (N_CHUNK, LANES)) for j in range(HALF)]

    def step(i, carry):
        s = (BLK - 1 - i) if reverse else i
        rows = pl.ds(s, N_CHUNK, stride=BLK)
        out = []
        for j in range(HALF):
            pr, pi = _cmul(ar[j], ai[j], carry[j], carry[HALF + j])
            xr = buf.at[j][rows, :] + pr
            xi = buf.at[HALF + j][rows, :] + pi
            buf.at[j][rows, :] = xr
            buf.at[HALF + j][rows, :] = xi
            out.append((xr, xi))
        return tuple(p[0] for p in out) + tuple(p[1] for p in out)

    zero = jnp.zeros((N_CHUNK, LANES), F32)
    lax.fori_loop(0, BLK, step, (zero,) * (2 * HALF))


def _chunk_states(buf, carry_s, xin_s, tab, reverse):
    edge = 0 if reverse else BLK - 1
    top = 0 if reverse else BLK - 1
    pw = [tab[j, top:top + 1, :] for j in range(2 * HALF)]
    cur = [carry_s[j:j + 1, :] for j in range(2 * HALF)]
    order = range(N_CHUNK - 1, -1, -1) if reverse else range(N_CHUNK)
    for c in order:
        for j in range(2 * HALF):
            xin_s[j, c:c + 1, :] = cur[j]
        nxt = []
        for j in range(HALF):
            pr, pi = _cmul(pw[j], pw[HALF + j], cur[j], cur[HALF + j])
            row = c * BLK + edge
            nxt.append((pr + buf[j, row:row + 1, :], pi + buf[HALF + j, row:row + 1, :]))
        cur = [p[0] for p in nxt] + [p[1] for p in nxt]
    for j in range(2 * HALF):
        carry_s[j:j + 1, :] = cur[j]


def _s5_fwd(u, a_cat, b_mat, c_mat, d_skip):
    t_len = u.shape[0]
    nt = t_len // TILE

    def body(u_ref, a_ref, b_ref, c_ref, d_ref, y_ref, st_ref, xs, tab, carry_s, xin_s):
        sb = pl.program_id(1)

        @pl.when(sb == 0)
        def _():
            _power_table(a_ref, tab, 1.0, False)
            carry_s[...] = jnp.zeros_like(carry_s)

        st_ref[0, 0] = carry_s[...]
        uv = u_ref[...]
        bu = _mm(uv.astype(BF16), b_ref[0])
        for j in range(2 * HALF):
            xs[j] = bu[:, j * LANES:(j + 1) * LANES]
        _chunk_scan(xs, a_ref, 1.0, False)
        _chunk_states(xs, carry_s, xin_s, tab, False)

        def fix(c, acc):
            rows = pl.ds(pl.multiple_of(c * BLK, BLK), BLK)
            for j in range(HALF):
                xr_in = xin_s.at[j][pl.ds(c, 1), :]
                xi_in = xin_s.at[HALF + j][pl.ds(c, 1), :]
                pr, pi = _cmul(tab[j], tab[HALF + j], xr_in, xi_in)
                xs.at[j][rows, :] = xs.at[j][rows, :] + pr
                xs.at[HALF + j][rows, :] = xs.at[HALF + j][rows, :] + pi
            return acc

        lax.fori_loop(0, N_CHUNK, fix, 0)
        y = d_ref[0] * uv
        for j in range(2 * HALF):
            y = y + _mm(xs[j].astype(BF16), c_ref[0, j * LANES:(j + 1) * LANES, :])
        y_ref[...] = y

    return pl.pallas_call(
        body, name="s5_fwd", grid=(N_LB, nt),
        in_specs=[pl.BlockSpec((TILE, LANES), lambda lb, sb: (sb, lb)),
                  pl.BlockSpec((1, 2 * HALF, LANES), lambda lb, sb: (lb, 0, 0)),
                  pl.BlockSpec((1, LANES, 2 * HALF * LANES), lambda lb, sb: (lb, 0, 0)),
                  pl.BlockSpec((1, 2 * HALF * LANES, LANES), lambda lb, sb: (lb, 0, 0)),
                  pl.BlockSpec((1, 1, LANES), lambda lb, sb: (lb, 0, 0))],
        out_specs=[pl.BlockSpec((TILE, LANES), lambda lb, sb: (sb, lb)),
                   pl.BlockSpec((1, 1, 2 * HALF, LANES), lambda lb, sb: (lb, sb, 0, 0))],
        out_shape=[jax.ShapeDtypeStruct((t_len, SSM_W), F32), jax.ShapeDtypeStruct((N_LB, nt, 2 * HALF, LANES), F32)],
        scratch_shapes=[pltpu.VMEM((2 * HALF, TILE, LANES), F32), pltpu.VMEM((2 * HALF, BLK, LANES), F32),
                        pltpu.VMEM((2 * HALF, LANES), F32), pltpu.VMEM((2 * HALF, N_CHUNK, LANES), F32)],
        compiler_params=_params("arbitrary", "arbitrary"),
    )(u, a_cat, b_mat, c_mat, d_skip)


def _s5_bwd(u, dy, states, a_cat, b_mat, c_mat, d_skip):
    t_len = u.shape[0]
    nt = t_len // TILE
    last = nt - 1

    def body(u_ref, dy_ref, st_ref, a_ref, b_ref, c_ref, d_ref, du_ref, db_ref, dc_ref, da_ref, dd_ref,
             xs, gs, tab, tabc, carry_s, lam_s, xin_s, lin_s):
        sb = pl.program_id(1)

        @pl.when(sb == 0)
        def _():
            _power_table(a_ref, tab, 1.0, False)
            _power_table(a_ref, tabc, -1.0, True)
            lam_s[...] = jnp.zeros_like(lam_s)
            db_ref[...] = jnp.zeros_like(db_ref)
            dc_ref[...] = jnp.zeros_like(dc_ref)
            da_ref[...] = jnp.zeros_like(da_ref)
            dd_ref[...] = jnp.zeros_like(dd_ref)

        uv = u_ref[...]
        dyv = dy_ref[...]
        ub = uv.astype(BF16)
        dyb = dyv.astype(BF16)
        carry_s[...] = st_ref[0, 0]
        bu = _mm(ub, b_ref[0])
        gy = _nt(dyb, c_ref[0])
        for j in range(2 * HALF):
            xs[j] = bu[:, j * LANES:(j + 1) * LANES]
            gs[j] = gy[:, j * LANES:(j + 1) * LANES]
        _chunk_scan(xs, a_ref, 1.0, False)
        _chunk_states(xs, carry_s, xin_s, tab, False)
        _chunk_scan(gs, a_ref, -1.0, True)
        _chunk_states(gs, lam_s, lin_s, tabc, True)
        last_row = lax.broadcasted_iota(jnp.int32, (BLK, LANES), 0) == BLK - 1

        def fix(c, acc):
            rows = pl.ds(pl.multiple_of(c * BLK, BLK), BLK)
            out = []
            for j in range(HALF):
                xr_in = xin_s.at[j][pl.ds(c, 1), :]
                xi_in = xin_s.at[HALF + j][pl.ds(c, 1), :]
                pr, pi = _cmul(tab[j], tab[HALF + j], xr_in, xi_in)
                xr = xs.at[j][rows, :] + pr
                xi = xs.at[HALF + j][rows, :] + pi
                xs.at[j][rows, :] = xr
                xs.at[HALF + j][rows, :] = xi
                lr_in = lin_s.at[j][pl.ds(c, 1), :]
                li_in = lin_s.at[HALF + j][pl.ds(c, 1), :]
                qr, qi = _cmul(tabc[j], tabc[HALF + j], lr_in, li_in)
                lr_ = gs.at[j][rows, :] + qr
                li_ = gs.at[HALF + j][rows, :] + qi
                gs.at[j][rows, :] = lr_
                gs.at[HALF + j][rows, :] = li_
                nr = jnp.where(last_row, lr_in, pltpu.roll(lr_, BLK - 1, 0))
                ni = jnp.where(last_row, li_in, pltpu.roll(li_, BLK - 1, 0))
                out.append(acc[j] + _colsum(xr * nr + xi * ni))
                out.append(acc[HALF + j] + _colsum(xr * ni - xi * nr))
            return tuple(out[0::2]) + tuple(out[1::2])

        zero = jnp.zeros((1, LANES), F32)
        da = lax.fori_loop(0, N_CHUNK, fix, (zero,) * (2 * HALF))
        for j in range(2 * HALF):
            da_ref[0, j:j + 1, :] = da_ref[0, j:j + 1, :] + da[j]
        lam = jnp.concatenate([gs[j].astype(BF16) for j in range(2 * HALF)], axis=1)
        xcat = jnp.concatenate([xs[j].astype(BF16) for j in range(2 * HALF)], axis=1)
        du_ref[...] = _nt(lam, b_ref[0]) + d_ref[0] * dyv
        db_ref[0] = db_ref[0] + _tn(ub, lam)
        dc_ref[0] = dc_ref[0] + _tn(dyb, xcat)
        dd_ref[0] = dd_ref[0] + _colsum(dyv * uv)

    rev = lambda lb, sb: (last - sb, lb)
    per_lb = lambda lb, sb: (lb, 0, 0)
    wide = 2 * HALF * LANES
    return pl.pallas_call(
        body, name="s5_bwd", grid=(N_LB, nt),
        in_specs=[pl.BlockSpec((TILE, LANES), rev), pl.BlockSpec((TILE, LANES), rev),
                  pl.BlockSpec((1, 1, 2 * HALF, LANES), lambda lb, sb: (lb, last - sb, 0, 0)),
                  pl.BlockSpec((1, 2 * HALF, LANES), per_lb), pl.BlockSpec((1, LANES, wide), per_lb),
                  pl.BlockSpec((1, wide, LANES), per_lb), pl.BlockSpec((1, 1, LANES), per_lb)],
        out_specs=[pl.BlockSpec((TILE, LANES), rev), pl.BlockSpec((1, LANES, wide), per_lb),
                   pl.BlockSpec((1, LANES, wide), per_lb), pl.BlockSpec((1, 2 * HALF, LANES), per_lb),
                   pl.BlockSpec((1, 1, LANES), per_lb)],
        out_shape=[jax.ShapeDtypeStruct((t_len, SSM_W), F32), jax.ShapeDtypeStruct((N_LB, LANES, wide), F32),
                   jax.ShapeDtypeStruct((N_LB, LANES, wide), F32), jax.ShapeDtypeStruct((N_LB, 2 * HALF, LANES), F32),
                   jax.ShapeDtypeStruct((N_LB, 1, LANES), F32)],
        scratch_shapes=[pltpu.VMEM((2 * HALF, TILE, LANES), F32), pltpu.VMEM((2 * HALF, TILE, LANES), F32),
                        pltpu.VMEM((2 * HALF, BLK, LANES), F32), pltpu.VMEM((2 * HALF, BLK, LANES), F32),
                        pltpu.VMEM((2 * HALF, LANES), F32), pltpu.VMEM((2 * HALF, LANES), F32),
                        pltpu.VMEM((2 * HALF, N_CHUNK, LANES), F32), pltpu.VMEM((2 * HALF, N_CHUNK, LANES), F32)],
        compiler_params=_params("arbitrary", "arbitrary"),
    )(u, dy, states, a_cat, b_mat, c_mat, d_skip)


_GELU_C = math.sqrt(2.0 / math.pi)
_GELU_K = 0.044715


def _gelu(y):
    t = jnp.tanh(_GELU_C * (y + _GELU_K * (y * y * y)))
    return y * (0.5 * (1.0 + t)), t


def _gelu_grad(y, t):
    return 0.5 * (1.0 + t) + 0.5 * y * (1.0 - t * t) * (_GELU_C * (1.0 + 3.0 * _GELU_K * y * y))


def _glu(y, wg, bias):
    z, t = _gelu(y)
    sg = jax.nn.sigmoid(_mm(z.astype(BF16), wg) + bias)
    return z, t, sg


def _mix_fwd(attn, y, x, wg, glu_b, ga, gs, wo):
    t_len = x.shape[0]
    tm = 512

    def body(attn_ref, y_ref, x_ref, wg_ref, b_ref, ga_ref, gs_ref, wo_ref, x2_ref, mix_ref, z_ref):
        z, _, sg = _glu(y_ref[...], wg_ref[...], b_ref[...])
        z_ref[...] = z.astype(BF16)
        s = z * sg
        av = attn_ref[...]
        an = (av * _rms(av) * ga_ref[...]).astype(BF16)
        sn = (s * _rms(s) * gs_ref[...]).astype(BF16)
        mix_ref[:, 0:ATTN_W] = an
        mix_ref[:, ATTN_W:] = sn
        x2_ref[...] = x_ref[...] + _mm(an, wo_ref[0:ATTN_W, :]) + _mm(sn, wo_ref[ATTN_W:, :])

    row = lambda i: (i, 0)
    const = lambda i: (0, 0)
    return pl.pallas_call(
        body, name="mix_fwd", grid=(t_len // tm,),
        in_specs=[pl.BlockSpec((tm, ATTN_W), row), pl.BlockSpec((tm, SSM_W), row), pl.BlockSpec((tm, D_MODEL), row),
                  pl.BlockSpec((SSM_W, SSM_W), const), pl.BlockSpec((1, SSM_W), const), pl.BlockSpec((1, ATTN_W), const),
                  pl.BlockSpec((1, SSM_W), const), pl.BlockSpec((D_MODEL, D_MODEL), const)],
        out_specs=[pl.BlockSpec((tm, D_MODEL), row), pl.BlockSpec((tm, D_MODEL), row), pl.BlockSpec((tm, SSM_W), row)],
        out_shape=[jax.ShapeDtypeStruct((t_len, D_MODEL), F32), jax.ShapeDtypeStruct((t_len, D_MODEL), BF16),
                   jax.ShapeDtypeStruct((t_len, SSM_W), BF16)],
        compiler_params=_params("arbitrary"),
    )(attn, y, x, wg, glu_b, ga, gs, wo)


def _mlp(x2, target, g2, wu, wd):
    t_len = x2.shape[0]
    tm = 256
    fc = 1024
    n_fc = D_FF // fc

    def body(x2_ref, tg_ref, g2_ref, wu_hbm, wd_hbm, dx2_ref, hdn_ref, dup_ref, h_ref, dyb_ref, dg2_ref, loss_ref,
             wu_s, wd_s, relu_s, sem):
        @pl.when(pl.program_id(0) == 0)
        def _():
            cu = pltpu.make_async_copy(wu_hbm, wu_s, sem.at[0])
            cd = pltpu.make_async_copy(wd_hbm, wd_s, sem.at[1])
            cu.start()
            cd.start()
            cu.wait()
            cd.wait()
            dg2_ref[...] = jnp.zeros_like(dg2_ref)
            loss_ref[...] = jnp.zeros_like(loss_ref)

        x2v = x2_ref[...]
        r = _rms(x2v)
        g2v = g2_ref[...]
        h = (x2v * r * g2v).astype(BF16)
        h_ref[...] = h
        yout = x2v
        for c in range(n_fc):
            cols = slice(c * fc, (c + 1) * fc)
            ru = jnp.maximum(_mm(h, wu_s[:, cols]), 0.0)
            relu_s[:, cols] = ru
            hd = (ru * ru).astype(BF16)
            hdn_ref[:, cols] = hd
            yout = yout + _mm(hd, wd_s[cols, :])
        err = yout - tg_ref[...]
        loss_ref[...] = loss_ref[...] + 0.5 * jnp.sum(err * err) * (1.0 / D_MODEL)
        dy = err * (1.0 / D_MODEL)
        dyb = dy.astype(BF16)
        dyb_ref[...] = dyb
        dh = jnp.zeros((tm, D_MODEL), F32)
        for c in range(n_fc):
            cols = slice(c * fc, (c + 1) * fc)
            dup = (_nt(dyb, wd_s[cols, :]) * (2.0 * relu_s[:, cols])).astype(BF16)
            dup_ref[:, cols] = dup
            dh = dh + _nt(dup, wu_s[:, cols])
        dxn, gterm = _rms_bwd(dh, x2v, r, g2v)
        dx2_ref[...] = dy + dxn
        dg2_ref[...] = dg2_ref[...] + _colsum(gterm)

    row = lambda i: (i, 0)
    const = lambda i: (0, 0)
    any_spec = pl.BlockSpec(memory_space=pl.ANY)
    return pl.pallas_call(
        body, name="mlp", grid=(t_len // tm,),
        in_specs=[pl.BlockSpec((tm, D_MODEL), row), pl.BlockSpec((tm, D_MODEL), row), pl.BlockSpec((1, D_MODEL), const),
                  any_spec, any_spec],
        out_specs=[pl.BlockSpec((tm, D_MODEL), row), pl.BlockSpec((tm, D_FF), row), pl.BlockSpec((tm, D_FF), row),
                   pl.BlockSpec((tm, D_MODEL), row), pl.BlockSpec((tm, D_MODEL), row), pl.BlockSpec((1, D_MODEL), const),
                   pl.BlockSpec((1, LANES), const)],
        out_shape=[jax.ShapeDtypeStruct((t_len, D_MODEL), F32), jax.ShapeDtypeStruct((t_len, D_FF), BF16),
                   jax.ShapeDtypeStruct((t_len, D_FF), BF16), jax.ShapeDtypeStruct((t_len, D_MODEL), BF16),
                   jax.ShapeDtypeStruct((t_len, D_MODEL), BF16), jax.ShapeDtypeStruct((1, D_MODEL), F32),
                   jax.ShapeDtypeStruct((1, LANES), F32)],
        scratch_shapes=[pltpu.VMEM((D_MODEL, D_FF), BF16), pltpu.VMEM((D_FF, D_MODEL), BF16),
                        pltpu.VMEM((tm, D_FF), F32), pltpu.SemaphoreType.DMA((2,))],
        compiler_params=_params("arbitrary"),
    )(x2, target, g2, wu, wd)


def _mix_bwd(dx2, attn, y, wg, glu_b, ga, gs, wo):
    t_len = dx2.shape[0]
    tm = 512

    def body(dx2_ref, attn_ref, y_ref, wg_ref, b_ref, ga_ref, gs_ref, wo_ref,
             dattn_ref, dy_ref, dx2b_ref, dgp_ref, dga_ref, dgs_ref, db_ref):
        @pl.when(pl.program_id(0) == 0)
        def _():
            dga_ref[...] = jnp.zeros_like(dga_ref)
            dgs_ref[...] = jnp.zeros_like(dgs_ref)
            db_ref[...] = jnp.zeros_like(db_ref)

        dx2b = dx2_ref[...].astype(BF16)
        dx2b_ref[...] = dx2b
        d_an = _nt(dx2b, wo_ref[0:ATTN_W, :])
        d_sn = _nt(dx2b, wo_ref[ATTN_W:, :])
        yv = y_ref[...]
        wg = wg_ref[...]
        z, t, sg = _glu(yv, wg, b_ref[...])
        s = z * sg
        av = attn_ref[...]
        d_attn, ga_term = _rms_bwd(d_an, av, _rms(av), ga_ref[...])
        d_s, gs_term = _rms_bwd(d_sn, s, _rms(s), gs_ref[...])
        dattn_ref[...] = d_attn
        dgp = d_s * z * sg * (1.0 - sg)
        dgpb = dgp.astype(BF16)
        dgp_ref[...] = dgpb
        dz = d_s * sg + _nt(dgpb, wg)
        dy_ref[...] = dz * _gelu_grad(yv, t)
        dga_ref[...] = dga_ref[...] + _colsum(ga_term)
        dgs_ref[...] = dgs_ref[...] + _colsum(gs_term)
        db_ref[...] = db_ref[...] + _colsum(dgp)

    row = lambda i: (i, 0)
    const = lambda i: (0, 0)
    vec = jax.ShapeDtypeStruct((1, SSM_W), F32)
    return pl.pallas_call(
        body, name="mix_bwd", grid=(t_len // tm,),
        in_specs=[pl.BlockSpec((tm, D_MODEL), row), pl.BlockSpec((tm, ATTN_W), row), pl.BlockSpec((tm, SSM_W), row),
                  pl.BlockSpec((SSM_W, SSM_W), const), pl.BlockSpec((1, SSM_W), const), pl.BlockSpec((1, ATTN_W), const),
                  pl.BlockSpec((1, SSM_W), const), pl.BlockSpec((D_MODEL, D_MODEL), const)],
        out_specs=[pl.BlockSpec((tm, ATTN_W), row), pl.BlockSpec((tm, SSM_W), row), pl.BlockSpec((tm, D_MODEL), row),
                   pl.BlockSpec((tm, SSM_W), row), pl.BlockSpec((1, ATTN_W), const), pl.BlockSpec((1, SSM_W), const),
                   pl.BlockSpec((1, SSM_W), const)],
        out_shape=[jax.ShapeDtypeStruct((t_len, ATTN_W), F32), jax.ShapeDtypeStruct((t_len, SSM_W), F32),
                   jax.ShapeDtypeStruct((t_len, D_MODEL), BF16), jax.ShapeDtypeStruct((t_len, SSM_W), BF16), vec, vec, vec],
        compiler_params=_params("arbitrary"),
    )(dx2, attn, y, wg, glu_b, ga, gs, wo)


def _inproj_bwd(dqs, dkn, dv, du, q_raw, k_raw, x, dx2, wi, g1, gq, gk, ones64):
    t_len = x.shape[0]
    tm = 512
    n_heads = ATTN_W // HEAD

    def body(dqs_ref, dkn_ref, dv_ref, du_ref, q_ref, k_ref, x_ref, dx2_ref, wi_ref, g1_ref, gq_ref, gk_ref, bd_ref,
             gx_ref, dproj_ref, dg1_ref, dgq_ref, dgk_ref, accq, acck):
        i = pl.program_id(0)

        @pl.when(i == 0)
        def _():
            dg1_ref[...] = jnp.zeros_like(dg1_ref)
            accq[...] = jnp.zeros_like(accq)
            acck[...] = jnp.zeros_like(acck)

        bd = bd_ref[...]

        def head_norm_bwd(dy, raw, gain, acc):
            r = lax.rsqrt(_group_mean(raw * raw, bd, HEAD) + EPS)
            xh = raw * r
            dxh = dy * gain
            acc[...] = acc[...] + _colsum(dy * xh)
            return r * (dxh - xh * _group_mean(dxh * xh, bd, HEAD))

        dq = head_norm_bwd(dqs_ref[...] * (HEAD ** -0.5), q_ref[...], gq_ref[...], accq)
        dk = head_norm_bwd(dkn_ref[...], k_ref[...], gk_ref[...], acck)
        dproj_ref[:, 0:ATTN_W] = dq.astype(BF16)
        dproj_ref[:, ATTN_W:2 * ATTN_W] = dk.astype(BF16)
        dproj_ref[:, 2 * ATTN_W:3 * ATTN_W] = dv_ref[...].astype(BF16)
        dproj_ref[:, 3 * ATTN_W:] = du_ref[...].astype(BF16)
        dxn = _nt(dproj_ref[...], wi_ref[...])
        xv = x_ref[...]
        g1v = g1_ref[...]
        dx, g1_term = _rms_bwd(dxn, xv, _rms(xv), g1v)
        gx_ref[...] = dx2_ref[...] + dx
        dg1_ref[...] = dg1_ref[...] + _colsum(g1_term)

        @pl.when(i == pl.num_programs(0) - 1)
        def _():
            for acc, out in ((accq, dgq_ref), (acck, dgk_ref)):
                tot = acc[:, 0:HEAD]
                for h in range(1, n_heads):
                    tot = tot + acc[:, h * HEAD:(h + 1) * HEAD]
                out[...] = tot

    row = lambda i: (i, 0)
    const = lambda i: (0, 0)
    aw = pl.BlockSpec((tm, ATTN_W), row)
    dm = pl.BlockSpec((tm, D_MODEL), row)
    return pl.pallas_call(
        body, name="inproj_bwd", grid=(t_len // tm,),
        in_specs=[aw, aw, aw, aw, aw, aw, dm, dm, pl.BlockSpec((D_MODEL, PROJ_W), const), pl.BlockSpec((1, D_MODEL), const),
                  pl.BlockSpec((1, ATTN_W), const), pl.BlockSpec((1, ATTN_W), const), pl.BlockSpec((ATTN_W, ATTN_W), const)],
        out_specs=[dm, pl.BlockSpec((tm, PROJ_W), row), pl.BlockSpec((1, D_MODEL), const),
                   pl.BlockSpec((1, HEAD), const), pl.BlockSpec((1, HEAD), const)],
        out_shape=[jax.ShapeDtypeStruct((t_len, D_MODEL), F32), jax.ShapeDtypeStruct((t_len, PROJ_W), BF16),
                   jax.ShapeDtypeStruct((1, D_MODEL), F32), jax.ShapeDtypeStruct((1, HEAD), F32),
                   jax.ShapeDtypeStruct((1, HEAD), F32)],
        scratch_shapes=[pltpu.VMEM((1, ATTN_W), F32), pltpu.VMEM((1, ATTN_W), F32)],
        compiler_params=_params("arbitrary"),
    )(dqs, dkn, dv, du, q_raw, k_raw, x, dx2, wi, g1, gq, gk, ones64)


def _grad_matmul(a, b, name):
    t_len, m = a.shape
    n = b.shape[1]
    bm, bn, bt = min(m, 1024), min(n, 1024), 1024

    def body(a_ref, b_ref, o_ref):
        @pl.when(pl.program_id(2) == 0)
        def _():
            o_ref[...] = jnp.zeros_like(o_ref)

        o_ref[...] = o_ref[...] + _tn(a_ref[...], b_ref[...])

    return pl.pallas_call(
        body, name=name, grid=(m // bm, n // bn, t_len // bt),
        in_specs=[pl.BlockSpec((bt, bm), lambda i, j, k: (k, i)), pl.BlockSpec((bt, bn), lambda i, j, k: (k, j))],
        out_specs=pl.BlockSpec((bm, bn), lambda i, j, k: (i, j)),
        out_shape=jax.ShapeDtypeStruct((m, n), F32),
        compiler_params=_params("arbitrary", "arbitrary", "arbitrary"),
    )(a, b)


def _adamw(w, g, m, v, name):
    rows, cols = w.shape
    br = _row_block(rows, 256)

    def body(w_ref, g_ref, m_ref, v_ref, d_o, m_o, v_o):
        gv = g_ref[...]
        mn = ADAM_B1 * m_ref[...] + (1.0 - ADAM_B1) * gv
        vn = ADAM_B2 * v_ref[...] + (1.0 - ADAM_B2) * jnp.square(gv)
        m_hat = mn / (1.0 - ADAM_B1 ** ADAM_STEP)
        v_hat = vn / (1.0 - ADAM_B2 ** ADAM_STEP)
        d_o[...] = -ADAM_LR * (m_hat / (jnp.sqrt(v_hat) + ADAM_EPS) + ADAM_WD * w_ref[...])
        m_o[...] = mn
        v_o[...] = vn

    spec = pl.BlockSpec((br, cols), lambda i: (i, 0))
    shape = jax.ShapeDtypeStruct((rows, cols), F32)
    return pl.pallas_call(
        body, name=name, grid=(rows // br,), in_specs=[spec] * 4, out_specs=[spec] * 3, out_shape=[shape] * 3,
        compiler_params=_params("arbitrary"),
    )(w, g, m, v)


def _sum_arrays(arrs, name, out_dtype=F32):
    rows, cols = arrs[0].shape
    br = _row_block(rows, 512)
    n = len(arrs)

    def body(*refs):
        tot = refs[0][...]
        for r in refs[1:n]:
            tot = tot + r[...]
        refs[n][...] = tot.astype(out_dtype)

    spec = pl.BlockSpec((br, cols), lambda i: (i, 0))
    return pl.pallas_call(
        body, name=name, grid=(rows // br,), in_specs=[spec] * n, out_specs=spec,
        out_shape=jax.ShapeDtypeStruct((rows, cols), out_dtype), compiler_params=_params("arbitrary"),
    )(*arrs)


GPL = N_GROUPS // N_LB
SW = GPL * N_STATE


def _eye_groups():
    return jnp.eye(GPL, dtype=F32)


def _s5_matrices(ab_r, ab_i, bb_r, bb_i, c_re, c_im, d_skip):
    eye = _eye_groups()
    a_cat = jnp.concatenate([ab_r.reshape(N_LB, HALF, LANES), ab_i.reshape(N_LB, HALF, LANES)], axis=1)

    def b_part(bb):
        b4 = jnp.transpose(bb.reshape(N_LB, GPL, N_STATE, GROUP_W), (0, 1, 3, 2))
        return (b4[:, :, :, None, :] * eye[None, :, None, :, None]).reshape(N_LB, LANES, SW)

    def c_part(cc):
        c4 = jnp.transpose(cc.reshape(N_LB, GPL, GROUP_W, N_STATE), (0, 1, 3, 2))
        return (c4[:, :, :, None, :] * eye[None, :, None, :, None]).reshape(N_LB, SW, LANES)

    b_mat = jnp.concatenate([b_part(bb_r), b_part(bb_i)], axis=2).astype(BF16)
    c_mat = jnp.concatenate([c_part(c_re), -c_part(c_im)], axis=1).astype(BF16)
    return a_cat, b_mat, c_mat, d_skip.reshape(N_LB, 1, LANES)


def _s5_unpack_grads(db, dc, da, dd):
    eye = _eye_groups()
    mask = eye[None, :, None, None, :, None]
    d6 = jnp.sum(db.reshape(N_LB, GPL, GROUP_W, 2, GPL, N_STATE) * mask, axis=4)
    dbb = jnp.transpose(d6, (3, 0, 1, 4, 2)).reshape(2, N_GROUPS * N_STATE, GROUP_W)
    c6 = jnp.sum(dc.reshape(N_LB, GPL, GROUP_W, 2, GPL, N_STATE) * mask, axis=4)
    dcc = jnp.transpose(c6, (3, 0, 1, 2, 4)).reshape(2, N_GROUPS, GROUP_W, N_STATE)
    dab_r = da[:, :HALF].reshape(N_GROUPS * N_STATE, 1)
    dab_i = da[:, HALF:].reshape(N_GROUPS * N_STATE, 1)
    return dab_r, dab_i, dbb[0], dbb[1], dcc[0], -dcc[1], dd.reshape(N_GROUPS, GROUP_W)


def _block_ones(n, width):
    i = lax.broadcasted_iota(jnp.int32, (n, n), 0) // width
    j = lax.broadcasted_iota(jnp.int32, (n, n), 1) // width
    return (i == j).astype(BF16)


def _tile_heads(g):
    return jnp.tile(g.reshape(1, HEAD), (1, ATTN_W // HEAD))


def _local_step(x, target, wi, wg, wo, wu, wd, p):
    ones64 = _block_ones(ATTN_W, HEAD)
    ones_hp = _block_ones(LANES, HEAD)
    g1 = p["norm1_g"].reshape(1, D_MODEL)
    g2 = p["norm2_g"].reshape(1, D_MODEL)
    gq = _tile_heads(p["q_norm_g"])
    gk = _tile_heads(p["k_norm_g"])
    ga = p["attn_out_norm_g"].reshape(1, ATTN_W)
    gs = p["ssm_out_norm_g"].reshape(1, SSM_W)
    glu_b = p["glu_b"].reshape(1, SSM_W)
    n_gp = N_GROUPS * N_STATE
    lr = p["ssm_a_re"].reshape(n_gp, 1)
    li = p["ssm_a_im"].reshape(n_gp, 1)
    ldt = jnp.repeat(p["ssm_log_dt"].reshape(N_GROUPS), N_STATE).reshape(n_gp, 1)
    br = p["ssm_b_re"].reshape(n_gp, GROUP_W)
    bi = p["ssm_b_im"].reshape(n_gp, GROUP_W)
    ab_r, ab_i, bb_r, bb_i = _disc_fwd(lr, li, ldt, br, bi)
    a_cat, b_mat, c_mat, d_mat = _s5_matrices(
        ab_r, ab_i, bb_r, bb_i, p["ssm_c_re"].reshape(N_GROUPS, GROUP_W, N_STATE),
        p["ssm_c_im"].reshape(N_GROUPS, GROUP_W, N_STATE), p["ssm_d"])

    xn, qn, kn, vv, u, q_raw, k_raw = _inproj_fwd(x, g1, wi, gq, gk, ones64)
    attn, lse = _attn_fwd(qn, kn, vv)
    y, states = _s5_fwd(u, a_cat, b_mat, c_mat, d_mat)
    x2, mix, z = _mix_fwd(attn, y, x, wg, glu_b, ga, gs, wo)
    dx2, hdn, dup, h, dyb, dg2, loss = _mlp(x2, target, g2, wu, wd)
    d_attn, dy_ssm, dx2b, dgp, dga, dgs, dglu_b = _mix_bwd(dx2, attn, y, wg, glu_b, ga, gs, wo)
    dqs, dkn, dvv = _attn_bwd(qn, kn, vv, attn, d_attn, lse, ones_hp)
    du, db, dc, da, dd = _s5_bwd(u, dy_ssm, states, a_cat, b_mat, c_mat, d_mat)
    grad_x, dproj, dg1, dgq, dgk = _inproj_bwd(dqs, dkn, dvv, du, q_raw, k_raw, x, dx2, wi, g1, gq, gk, ones64)

    big = {
        "w_in": _grad_matmul(xn, dproj, "grad_w_in"),
        "glu_w": _grad_matmul(z, dgp, "grad_glu_w"),
        "w_out": _grad_matmul(mix, dx2b, "grad_w_out"),
        "w_mlp_up": _grad_matmul(h, dup, "grad_w_mlp_up"),
        "w_mlp_down": _grad_matmul(hdn, dyb, "grad_w_mlp_down"),
    }
    dab_r, dab_i, dbb_r, dbb_i, dc_re, dc_im, dd_g = _s5_unpack_grads(db, dc, da, dd)
    cot = {"norm1_g": dg1, "q_norm_g": dgq, "k_norm_g": dgk, "ab_r": dab_r, "ab_i": dab_i, "bb_r": dbb_r, "bb_i": dbb_i,
           "ssm_c_re": dc_re, "ssm_c_im": dc_im, "ssm_d": dd_g, "glu_b": dglu_b, "attn_out_norm_g": dga,
           "ssm_out_norm_g": dgs, "norm2_g": dg2}
    return loss[0, 0], grad_x, big, cot, (lr, li, ldt, br, bi)


COT_NAMES = ("norm1_g", "q_norm_g", "k_norm_g", "ab_r", "ab_i", "bb_r", "bb_i", "ssm_c_re", "ssm_c_im", "ssm_d",
             "glu_b", "attn_out_norm_g", "ssm_out_norm_g", "norm2_g")
SMALL_NAMES = ("norm1_g", "q_norm_g", "k_norm_g", "ssm_a_re", "ssm_a_im", "ssm_log_dt", "ssm_b_re", "ssm_b_im",
               "ssm_c_re", "ssm_c_im", "ssm_d", "glu_b", "attn_out_norm_g", "ssm_out_norm_g", "norm2_g")
BIG_NAMES = ("w_in", "glu_w", "w_out", "w_mlp_up", "w_mlp_down")
PACK_ROWS = 1152


def _pack(arrs):
    flat = jnp.concatenate([a.reshape(-1) for a in arrs])
    return jnp.pad(flat, (0, PACK_ROWS * LANES - flat.shape[0])).reshape(PACK_ROWS, LANES)


def _unpack(packed, like):
    flat = packed.reshape(-1)
    out, pos = [], 0
    for a in like:
        out.append(flat[pos:pos + a.size].reshape(a.shape))
        pos += a.size
    return out


def _small_grads(cot, disc_in, p):
    lr, li, ldt, br, bi = disc_in
    group_sum = (lax.broadcasted_iota(jnp.int32, (N_GROUPS, N_GROUPS * N_STATE), 1) // N_STATE
                 == lax.broadcasted_iota(jnp.int32, (N_GROUPS, N_GROUPS * N_STATE), 0)).astype(F32)
    dlr, dli, dldt, dbr, dbi = _disc_bwd(lr, li, ldt, br, bi, cot["ab_r"], cot["ab_i"], cot["bb_r"], cot["bb_i"], group_sum)
    g = dict(cot)
    g.update(ssm_a_re=dlr, ssm_a_im=dli, ssm_log_dt=dldt[:, 0], ssm_b_re=dbr, ssm_b_im=dbi)
    return {n: g[n].reshape(p[n].shape) for n in SMALL_NAMES}


BIG = {
    "w_in": ((D_MODEL, PROJ_W), 1, PROJ_W // 4, 0, D_MODEL // 2),
    "glu_w": ((SSM_W, SSM_W), 0, SSM_W // 4, 1, SSM_W // 2),
    "w_out": ((D_MODEL, D_MODEL), 0, D_MODEL // 4, 1, D_MODEL // 2),
    "w_mlp_up": ((D_MODEL, D_FF), 1, D_FF // 4, 0, D_MODEL // 2),
    "w_mlp_down": ((D_FF, D_MODEL), 0, D_FF // 4, 1, D_MODEL // 2),
}
N_BIG = len(BIG_NAMES)
N_CHIPS = 4
ANY = pl.BlockSpec(memory_space=pl.ANY)


def _cut(name, shard=False, half=False):
    shape, s_ax, s_sz, h_ax, h_sz = BIG[name]
    shape = list(shape)
    if shard:
        shape[s_ax] = s_sz
    if half:
        shape[h_ax] = h_sz
    return tuple(shape)


def _window(name, base, shard=None, half=None):
    _, s_ax, s_sz, h_ax, h_sz = BIG[name]
    idx = [pl.ds(0, base[0]), pl.ds(0, base[1])]
    if shard is not None:
        idx[s_ax] = pl.ds(pl.multiple_of(shard * s_sz, s_sz), s_sz)
    if half is not None:
        idx[h_ax] = pl.ds(pl.multiple_of(half * h_sz, h_sz), h_sz)
    return tuple(idx)


def _mesh_pos():
    return lax.axis_index("x"), lax.axis_index("y"), lax.axis_index("c")


def _other_chips(x, y):
    return [(1 - x, y, 2 * (1 - x) + y), (x, 1 - y, 2 * x + 1 - y), (1 - x, 1 - y, 2 * (1 - x) + 1 - y)]


def _remote(src, dst, send_sem, recv_sem, dev):
    return pltpu.make_async_remote_copy(src_ref=src, dst_ref=dst, send_sem=send_sem, recv_sem=recv_sem,
                                        device_id=dev, device_id_type=MESH)


CHUNK_BYTES = 256 * 1024


def _row_chunks(shape, dtype):
    rows, cols = shape
    n = 1
    while rows % (2 * n) == 0 and (rows // (2 * n)) % 16 == 0 and rows * cols * jnp.dtype(dtype).itemsize // n > CHUNK_BYTES:
        n *= 2
    return [(pl.ds(i * (rows // n), rows // n), slice(None)) for i in range(n)]


def _start_remote(src, dst, send_sem, recv_sem, dev, shape, dtype):
    for sl in _row_chunks(shape, dtype):
        _remote(src.at[sl], dst.at[sl], send_sem, recv_sem, dev).start()
    return _remote(src, dst, send_sem, recv_sem, dev)


def _start_local(src, dst, sem, shape, dtype):
    for sl in _row_chunks(shape, dtype):
        pltpu.make_async_copy(src.at[sl], dst.at[sl], sem).start()
    return pltpu.make_async_copy(src, dst, sem)


def _gather_weights(shards):
    def body(*refs):
        ins, outs, stage = refs[0:N_BIG], refs[N_BIG:2 * N_BIG], refs[2 * N_BIG:3 * N_BIG]
        send, recv, fsend, frecv, lsem = refs[3 * N_BIG:]
        x, y, c = _mesh_pos()
        me = 2 * x + y
        sib = (x, y, 1 - c)
        chips = _other_chips(x, y)
        pending = []
        for w, n in enumerate(BIG_NAMES):
            stage[w][...] = ins[w][...].astype(BF16)
            full = BIG[n][0]
            pending.append(_start_local(stage[w], outs[w].at[_window(n, full, shard=me)], lsem.at[w],
                                        _cut(n, shard=True), BF16))
        sends = []
        for k, (px, py, _) in enumerate(chips):
            for w, n in enumerate(BIG_NAMES):
                s = k * N_BIG + w
                sends.append(_start_remote(stage[w].at[_window(n, _cut(n, shard=True), half=c)],
                                           outs[w].at[_window(n, BIG[n][0], shard=me, half=c)], send.at[s], recv.at[s],
                                           (px, py, c), _cut(n, shard=True, half=True), BF16))
        for k, (px, py, pj) in enumerate(chips):
            for w, n in enumerate(BIG_NAMES):
                s = k * N_BIG + w
                got = outs[w].at[_window(n, BIG[n][0], shard=pj, half=c)]
                _remote(got, got, send.at[s], recv.at[s], (px, py, c)).wait_recv()
                sends.append(_start_remote(got, got, fsend.at[s], frecv.at[s], sib, _cut(n, shard=True, half=True), BF16))
        for k, (px, py, pj) in enumerate(chips):
            for w, n in enumerate(BIG_NAMES):
                s = k * N_BIG + w
                got = outs[w].at[_window(n, BIG[n][0], shard=pj, half=1 - c)]
                _remote(got, got, fsend.at[s], frecv.at[s], sib).wait_recv()
        for cp in sends:
            cp.wait_send()
        for cp in pending:
            cp.wait()

    n_sem = (N_CHIPS - 1) * N_BIG
    outs = pl.pallas_call(
        body, name="gather_weights",
        in_specs=[pl.BlockSpec(memory_space=pltpu.VMEM)] * N_BIG, out_specs=[ANY] * N_BIG,
        out_shape=[jax.ShapeDtypeStruct(BIG[n][0], BF16) for n in BIG_NAMES],
        scratch_shapes=[pltpu.VMEM(_cut(n, shard=True), BF16) for n in BIG_NAMES]
        + [pltpu.SemaphoreType.DMA((n_sem,))] * 4 + [pltpu.SemaphoreType.DMA((N_BIG,))],
        compiler_params=pltpu.CompilerParams(vmem_limit_bytes=VMEM_LIMIT),
    )(*[shards[n] for n in BIG_NAMES])
    return outs


def _pair_exchange(big, packed):
    def body(*refs):
        ins, small = refs[0:N_BIG], refs[N_BIG]
        own = refs[N_BIG + 1:2 * N_BIG + 1]
        got = refs[2 * N_BIG + 1:3 * N_BIG + 2]
        send, recv, lsem = refs[3 * N_BIG + 2:]
        x, y, c = _mesh_pos()
        sib = (x, y, 1 - c)
        copies = []
        for w, n in enumerate(BIG_NAMES):
            full = BIG[n][0]
            half = _cut(n, half=True)
            copies.append(_start_remote(ins[w].at[_window(n, full, half=1 - c)], got[w], send.at[w], recv.at[w], sib, half, F32))
            copies.append(_start_local(ins[w].at[_window(n, full, half=c)], own[w], lsem.at[w], half, F32))
        copies.append(_start_remote(small, got[N_BIG], send.at[N_BIG], recv.at[N_BIG], sib, small.shape, F32))
        for cp in copies:
            cp.wait()

    halves = [jax.ShapeDtypeStruct(_cut(n, half=True), F32) for n in BIG_NAMES]
    outs = pl.pallas_call(
        body, name="grad_pair_exchange", in_specs=[ANY] * (N_BIG + 1), out_specs=[ANY] * (2 * N_BIG + 1),
        out_shape=halves + halves + [jax.ShapeDtypeStruct(packed.shape, F32)],
        scratch_shapes=[pltpu.SemaphoreType.DMA((N_BIG + 1,)), pltpu.SemaphoreType.DMA((N_BIG + 1,)),
                        pltpu.SemaphoreType.DMA((N_BIG,))],
    )(*[big[n] for n in BIG_NAMES], packed)
    return outs[0:N_BIG], outs[N_BIG:2 * N_BIG], outs[2 * N_BIG]


def _chip_exchange(halves, packed):
    n_all = N_BIG + 1

    def body(*refs):
        ins, outs = refs[0:n_all], refs[n_all:2 * n_all]
        send, recv, lsem = refs[2 * n_all:]
        x, y, c = _mesh_pos()
        me = 2 * x + y
        chips = _other_chips(x, y)

        def piece(w, shard):
            if w == N_BIG:
                return ins[w]
            n = BIG_NAMES[w]
            return ins[w].at[_window(n, _cut(n, half=True), shard=shard)]

        def kind(w):
            if w == N_BIG:
                return packed.shape, F32
            return _cut(BIG_NAMES[w], shard=True, half=True), BF16

        copies = []
        for w in range(n_all):
            copies.append(_start_local(piece(w, me), outs[w].at[me], lsem.at[w], *kind(w)))
        for k, (px, py, pj) in enumerate(chips):
            for w in range(n_all):
                s = k * n_all + w
                copies.append(_start_remote(piece(w, pj), outs[w].at[me], send.at[s], recv.at[s], (px, py, c), *kind(w)))
        for k, (px, py, pj) in enumerate(chips):
            for w in range(n_all):
                s = k * n_all + w
                _remote(piece(w, me), outs[w].at[pj], send.at[s], recv.at[s], (px, py, c)).wait_recv()
        for i, cp in enumerate(copies):
            if i < n_all:
                cp.wait()
            else:
                cp.wait_send()

    shapes = [jax.ShapeDtypeStruct((N_CHIPS,) + _cut(n, shard=True, half=True), BF16) for n in BIG_NAMES]
    shapes.append(jax.ShapeDtypeStruct((N_CHIPS,) + packed.shape, F32))
    n_sem = (N_CHIPS - 1) * n_all
    return pl.pallas_call(
        body, name="grad_chip_exchange", in_specs=[ANY] * n_all, out_specs=[ANY] * n_all, out_shape=shapes,
        scratch_shapes=[pltpu.SemaphoreType.DMA((n_sem,)), pltpu.SemaphoreType.DMA((n_sem,)),
                        pltpu.SemaphoreType.DMA((n_all,))],
    )(*halves, packed)


def _half_exchange(pieces):
    def body(*refs):
        ins, outs = refs[0:N_BIG], refs[N_BIG:2 * N_BIG]
        send, recv, lsem = refs[2 * N_BIG:]
        x, y, c = _mesh_pos()
        sib = (x, y, 1 - c)
        copies = []
        for w, n in enumerate(BIG_NAMES):
            mine = outs[w].at[_window(n, _cut(n, shard=True), half=c)]
            piece = _cut(n, shard=True, half=True)
            copies.append(_start_local(ins[w], mine, lsem.at[w], piece, F32))
            copies.append(_start_remote(ins[w], mine, send.at[w], recv.at[w], sib, piece, F32))
        for w, n in enumerate(BIG_NAMES):
            theirs = outs[w].at[_window(n, _cut(n, shard=True), half=1 - c)]
            _remote(ins[w], theirs, send.at[w], recv.at[w], sib).wait_recv()
        for i, cp in enumerate(copies):
            if i % 2 == 0:
                cp.wait()
            else:
                cp.wait_send()

    return pl.pallas_call(
        body, name="grad_half_exchange", in_specs=[ANY] * N_BIG, out_specs=[ANY] * N_BIG,
        out_shape=[jax.ShapeDtypeStruct(_cut(n, shard=True), F32) for n in BIG_NAMES],
        scratch_shapes=[pltpu.SemaphoreType.DMA((N_BIG,)), pltpu.SemaphoreType.DMA((N_BIG,)),
                        pltpu.SemaphoreType.DMA((N_BIG,))],
    )(*pieces)


def _sum_slots(q, name):
    slots, rows, cols = q.shape
    br = _row_block(rows, 512)

    def body(*refs):
        tot = refs[0][...].astype(F32)
        for r in refs[1:slots]:
            tot = tot + r[...].astype(F32)
        refs[slots][...] = tot

    specs = [pl.BlockSpec((None, br, cols), functools.partial(lambda i, s: (s, i, 0), s=s)) for s in range(slots)]
    return pl.pallas_call(
        body, name=name, grid=(rows // br,), in_specs=specs, out_specs=pl.BlockSpec((br, cols), lambda i: (i, 0)),
        out_shape=jax.ShapeDtypeStruct((rows, cols), F32), compiler_params=_params("arbitrary"),
    )(*([q] * slots))


WEIGHT_NAMES = ("norm1_g", "w_in", "q_norm_g", "k_norm_g", "ssm_a_re", "ssm_a_im", "ssm_log_dt", "ssm_b_re", "ssm_b_im",
                "ssm_c_re", "ssm_c_im", "ssm_d", "glu_w", "glu_b", "attn_out_norm_g", "ssm_out_norm_g", "w_out", "norm2_g",
                "w_mlp_up", "w_mlp_down")


def _train_step(a):
    x = a["x"][0]
    target = a["loss_target"][0]
    wi, wg, wo, wu, wd = _gather_weights({n: a[n][0] for n in BIG_NAMES})
    p = {n: a[n][0] for n in SMALL_NAMES}
    loss, grad_x, big, cot, disc_in = _local_step(x, target, wi, wg, wo, wu, wd, p)

    cot_list = [cot[n] for n in COT_NAMES]
    packed = _pack(cot_list)
    own, got, got_packed = _pair_exchange(big, packed)
    chip = [_sum_arrays([o, g], "pair_sum_" + n, BF16) for n, o, g in zip(BIG_NAMES, own, got)]
    chip_packed = _sum_arrays([packed, got_packed], "pair_sum_small")
    slots = _chip_exchange(chip, chip_packed)
    pieces = [_sum_slots(q, "chip_sum_" + n) for n, q in zip(BIG_NAMES, slots[:N_BIG])]
    small_sum = _sum_slots(slots[N_BIG], "chip_sum_small")
    shard_grads = dict(zip(BIG_NAMES, _half_exchange(pieces)))
    small_grads = _small_grads(dict(zip(COT_NAMES, _unpack(small_sum, cot_list))), disc_in, p)

    grads, delta, new_m, new_v = {}, {}, {}, {}
    for n in BIG_NAMES:
        grads[n] = shard_grads[n]
        delta[n], new_m[n], new_v[n] = _adamw(a[n][0], grads[n], a["m_" + n][0], a["v_" + n][0], "adamw_" + n)
    small_w = [p[n] for n in SMALL_NAMES]
    res = _adamw(_pack(small_w), _pack([small_grads[n] for n in SMALL_NAMES]), _pack([a["m_" + n][0] for n in SMALL_NAMES]),
                 _pack([a["v_" + n][0] for n in SMALL_NAMES]), "adamw_small")
    for store, packed_out in zip((delta, new_m, new_v), res):
        store.update(zip(SMALL_NAMES, _unpack(packed_out, small_w)))
    grads.update(small_grads)

    total = lax.psum(loss, ("x", "y", "c"))
    out = [total, grad_x[None]]
    for store in (grads, delta, new_m, new_v):
        out += [store[n].reshape(a[n].shape) for n in WEIGHT_NAMES]
    return tuple(out)


def kernel(x, norm1_g, w_in, q_norm_g, k_norm_g, ssm_a_re, ssm_a_im, ssm_log_dt, ssm_b_re, ssm_b_im, ssm_c_re, ssm_c_im, ssm_d, glu_w, glu_b, attn_out_norm_g, ssm_out_norm_g, w_out, norm2_g, w_mlp_up, w_mlp_down, loss_target, m_norm1_g, m_w_in, m_q_norm_g, m_k_norm_g, m_ssm_a_re, m_ssm_a_im, m_ssm_log_dt, m_ssm_b_re, m_ssm_b_im, m_ssm_c_re, m_ssm_c_im, m_ssm_d, m_glu_w, m_glu_b, m_attn_out_norm_g, m_ssm_out_norm_g, m_w_out, m_norm2_g, m_w_mlp_up, m_w_mlp_down, v_norm1_g, v_w_in, v_q_norm_g, v_k_norm_g, v_ssm_a_re, v_ssm_a_im, v_ssm_log_dt, v_ssm_b_re, v_ssm_b_im, v_ssm_c_re, v_ssm_c_im, v_ssm_d, v_glu_w, v_glu_b, v_attn_out_norm_g, v_ssm_out_norm_g, v_w_out, v_norm2_g, v_w_mlp_up, v_w_mlp_down):
    return _train_step(dict(locals()))
```

```python
import functools
import math

import jax
import jax.numpy as jnp
from jax import lax
from jax.experimental import pallas as pl
from jax.experimental.pallas import tpu as pltpu

F32 = jnp.float32
BF16 = jnp.bfloat16
MESH = pl.DeviceIdType.MESH

D_MODEL = 1024
ATTN_W = 512
SSM_W = 512
HEAD = 64
D_FF = 4096
PROJ_W = 2048
N_GROUPS = 32
N_STATE = 64
GROUP_W = 16
EPS = 1e-6
NEG = -1e30
DILATIONS = (1, 4, 16)
BLK = 128
TILE = 2048
LANES = 128
N_LB = SSM_W // LANES
N_SLAB = 2 * N_LB * N_STATE * 8 // LANES // N_LB
VMEM_LIMIT = 56 * 1024 * 1024

ADAM_LR, ADAM_B1, ADAM_B2, ADAM_EPS, ADAM_WD, ADAM_STEP = 0.001, 0.9, 0.999, 1e-08, 0.01, 10


def _params(*sem):
    return pltpu.CompilerParams(dimension_semantics=sem, vmem_limit_bytes=VMEM_LIMIT)


def _nt(a, b):
    return lax.dot_general(a, b, (((1,), (1,)), ((), ())), preferred_element_type=F32)


def _tn(a, b):
    return lax.dot_general(a, b, (((0,), (0,)), ((), ())), preferred_element_type=F32)


def _mm(a, b):
    return jnp.dot(a, b, preferred_element_type=F32)


def _group_mean(t, ones_bd, width):
    hi = t.astype(BF16)
    lo = (t - hi.astype(F32)).astype(BF16)
    return (_mm(hi, ones_bd) + _mm(lo, ones_bd)) * (1.0 / width)


def _rms(x):
    return lax.rsqrt(jnp.mean(x * x, axis=-1, keepdims=True) + EPS)


def _rms_bwd(dy, x, r, g):
    xh = x * r
    dxh = dy * g
    dx = r * (dxh - xh * jnp.mean(dxh * xh, axis=-1, keepdims=True))
    return dx, dy * xh


def _colsum(x):
    return jnp.sum(x, axis=0, keepdims=True)


def _row_block(rows, cap):
    for b in range(min(rows, cap) // 8 * 8, 0, -8):
        if rows % b == 0:
            return b
    raise ValueError(f"no row block for {rows} rows")


def _inproj_fwd(x, g1, wi, gq, gk, ones64):
    t_len = x.shape[0]
    tm = 512
    n_hp = ATTN_W // LANES

    def body(x_ref, g1_ref, wi_ref, gq_ref, gk_ref, bd_ref, xn_ref, q_ref, k_ref, v_ref, u_ref, qr_ref, kr_ref):
        xv = x_ref[...]
        xn = (xv * _rms(xv) * g1_ref[...]).astype(BF16)
        xn_ref[...] = xn
        proj = _mm(xn, wi_ref[...])
        q = proj[:, 0:ATTN_W]
        k = proj[:, ATTN_W:2 * ATTN_W]
        v = proj[:, 2 * ATTN_W:3 * ATTN_W]
        u_ref[...] = proj[:, 3 * ATTN_W:]
        qr_ref[...] = q
        kr_ref[...] = k
        bd = bd_ref[...]
        qn = q * lax.rsqrt(_group_mean(q * q, bd, HEAD) + EPS) * gq_ref[...] * (HEAD ** -0.5)
        kn = k * lax.rsqrt(_group_mean(k * k, bd, HEAD) + EPS) * gk_ref[...]
        for hp in range(n_hp):
            sl = slice(hp * LANES, (hp + 1) * LANES)
            q_ref[hp] = qn[:, sl]
            k_ref[hp] = kn[:, sl]
            v_ref[hp] = v[:, sl]

    row = lambda i: (i, 0)
    const = lambda i: (0, 0)
    hp_spec = pl.BlockSpec((n_hp, tm, LANES), lambda i: (0, i, 0))
    hp_shape = jax.ShapeDtypeStruct((n_hp, t_len, LANES), F32)
    return pl.pallas_call(
        body, name="inproj_fwd", grid=(t_len // tm,),
        in_specs=[pl.BlockSpec((tm, D_MODEL), row), pl.BlockSpec((1, D_MODEL), const),
                  pl.BlockSpec((D_MODEL, PROJ_W), const), pl.BlockSpec((1, ATTN_W), const),
                  pl.BlockSpec((1, ATTN_W), const), pl.BlockSpec((ATTN_W, ATTN_W), const)],
        out_specs=[pl.BlockSpec((tm, D_MODEL), row), hp_spec, hp_spec, hp_spec,
                   pl.BlockSpec((tm, SSM_W), row), pl.BlockSpec((tm, ATTN_W), row), pl.BlockSpec((tm, ATTN_W), row)],
        out_shape=[jax.ShapeDtypeStruct((t_len, D_MODEL), BF16), hp_shape, hp_shape, hp_shape,
                   jax.ShapeDtypeStruct((t_len, SSM_W), F32), jax.ShapeDtypeStruct((t_len, ATTN_W), F32),
                   jax.ShapeDtypeStruct((t_len, ATTN_W), F32)],
        compiler_params=_params("arbitrary"),
    )(x, g1, wi, gq, gk, ones64)


def _attn_masks():
    head0 = lax.broadcasted_iota(jnp.int32, (BLK, LANES), 1) < HEAD
    row = lax.broadcasted_iota(jnp.int32, (2 * BLK, 2 * BLK), 0) & (BLK - 1)
    col = lax.broadcasted_iota(jnp.int32, (2 * BLK, 2 * BLK), 1)
    return head0, (col < BLK) & (col >= row), (col >= BLK) & (col - BLK <= row)


def _stack_heads(x, head0):
    return jnp.concatenate([jnp.where(head0, x, 0.0), jnp.where(head0, 0.0, x)], axis=0).astype(BF16)


def _unit_rows(uidx, d):
    nb = TILE // (BLK * d)
    r = lax.div(uidx, nb)
    b = lax.rem(uidx, nb)
    start = r + d * BLK * b
    if d == 1:
        start = pl.multiple_of(start, BLK)
        mk = lambda s: pl.ds(pl.multiple_of(s, BLK), BLK)
    else:
        mk = lambda s: pl.ds(s, BLK, stride=d)
    return b, mk(start), mk(TILE + start), mk(TILE + start - d * BLK)


def _attn_fwd(q, k, v):
    n_hp, t_len, _ = q.shape
    nt = t_len // TILE

    def body(q_ref, kp_ref, kc_ref, vp_ref, vc_ref, o_ref, lse_ref, kk, vv, m_s, l_s, acc_s):
        t = pl.program_id(1)
        kk[0:TILE] = kp_ref[0]
        kk[TILE:] = kc_ref[0]
        vv[0:TILE] = vp_ref[0]
        vv[TILE:] = vc_ref[0]
        head0, band_prev, band_cur = _attn_masks()

        for pi, d in enumerate(DILATIONS):
            def unit(uidx, carry, d=d, pi=pi):
                b, rows_q, rows_c, rows_p = _unit_rows(uidx, d)
                mask = band_cur | (band_prev & ((t > 0) | (b > 0)))
                q2 = _stack_heads(q_ref.at[0][rows_q, :], head0)
                kcat = jnp.concatenate([kk[rows_p, :], kk[rows_c, :]], axis=0).astype(BF16)
                vcat = jnp.concatenate([vv[rows_p, :], vv[rows_c, :]], axis=0).astype(BF16)
                s = jnp.where(mask, _nt(q2, kcat), NEG)
                m = jnp.max(s, axis=1, keepdims=True)
                p = jnp.exp(s - m)
                ls = jnp.sum(p, axis=1, keepdims=True)
                pv = _mm(p.astype(BF16), vcat)
                m_s.at[pi][rows_q, :] = jnp.where(head0, m[0:BLK], m[BLK:])
                l_s.at[pi][rows_q, :] = jnp.where(head0, ls[0:BLK], ls[BLK:])
                acc_s.at[pi][rows_q, :] = jnp.where(head0, pv[0:BLK], pv[BLK:])
                return carry

            lax.fori_loop(0, TILE // BLK, unit, 0, unroll=8)

        m_all = jnp.maximum(jnp.maximum(m_s[0], m_s[1]), m_s[2])
        num = jnp.zeros((TILE, LANES), F32)
        den = jnp.zeros((TILE, LANES), F32)
        for pi in range(len(DILATIONS)):
            wgt = jnp.exp(m_s[pi] - m_all)
            num = num + acc_s[pi] * wgt
            den = den + l_s[pi] * wgt
        o_ref[...] = num / den
        lse_ref[0] = m_all + jnp.log(den)

    cur = lambda hp, t: (hp, t, 0)
    prev = lambda hp, t: (hp, jnp.maximum(t - 1, 0), 0)
    blk = (1, TILE, LANES)
    per_pattern = pltpu.VMEM((len(DILATIONS), TILE, LANES), F32)
    return pl.pallas_call(
        body, name="attn_fwd", grid=(n_hp, nt),
        in_specs=[pl.BlockSpec(blk, cur), pl.BlockSpec(blk, prev), pl.BlockSpec(blk, cur),
                  pl.BlockSpec(blk, prev), pl.BlockSpec(blk, cur)],
        out_specs=[pl.BlockSpec((TILE, LANES), lambda hp, t: (t, hp)), pl.BlockSpec(blk, cur)],
        out_shape=[jax.ShapeDtypeStruct((t_len, ATTN_W), F32), jax.ShapeDtypeStruct((n_hp, t_len, LANES), F32)],
        scratch_shapes=[pltpu.VMEM((2 * TILE, LANES), F32), pltpu.VMEM((2 * TILE, LANES), F32),
                        per_pattern, per_pattern, per_pattern],
        compiler_params=_params("arbitrary", "arbitrary"),
    )(q, k, k, v, v)


def _attn_bwd(q, k, v, o, do, lse, ones_hp):
    n_hp, t_len, _ = q.shape
    nt = t_len // TILE

    n_pat = len(DILATIONS)

    def body(q_ref, kp_ref, kc_ref, vp_ref, vc_ref, o_ref, do_ref, lse_ref, bd_ref,
             dq_ref, dk_ref, dv_ref, kk, vv, dq_s, dkc, dkp, dvc, dvp, hold_k, hold_v, dl_s):
        t = pl.program_id(1)

        @pl.when(t < nt)
        def _():
            kk[0:TILE] = kp_ref[0]
            kk[TILE:] = kc_ref[0]
            vv[0:TILE] = vp_ref[0]
            vv[TILE:] = vc_ref[0]
            dl_s[...] = _group_mean(do_ref[...] * o_ref[...], bd_ref[...], 1.0)
            head0, band_prev, band_cur = _attn_masks()

            for pi, d in enumerate(DILATIONS):
                def unit(uidx, carry, d=d, pi=pi):
                    b, rows_q, rows_c, rows_p = _unit_rows(uidx, d)
                    mask = band_cur | (band_prev & ((t > 0) | (b > 0)))
                    q2 = _stack_heads(q_ref.at[0][rows_q, :], head0)
                    do2 = _stack_heads(do_ref[rows_q, :], head0)
                    lse_f = lse_ref.at[0][rows_q, :]
                    dl_f = dl_s[rows_q, :]
                    lse2 = jnp.concatenate([lse_f[:, 0:1], lse_f[:, HEAD:HEAD + 1]], axis=0)
                    dl2 = jnp.concatenate([dl_f[:, 0:1], dl_f[:, HEAD:HEAD + 1]], axis=0)
                    kcat = jnp.concatenate([kk[rows_p, :], kk[rows_c, :]], axis=0).astype(BF16)
                    vcat = jnp.concatenate([vv[rows_p, :], vv[rows_c, :]], axis=0).astype(BF16)
                    p = jnp.where(mask, jnp.exp(_nt(q2, kcat) - lse2), 0.0)
                    ds = (p * (_nt(do2, vcat) - dl2)).astype(BF16)
                    dq2 = _mm(ds, kcat)
                    dq_s.at[pi][rows_q, :] = jnp.where(head0, dq2[0:BLK], dq2[BLK:])
                    dk2 = _tn(ds, q2)
                    dv2 = _tn(p.astype(BF16), do2)
                    dkp.at[pi][rows_q, :] = dk2[0:BLK]
                    dkc.at[pi][rows_q, :] = dk2[BLK:]
                    dvp.at[pi][rows_q, :] = dv2[0:BLK]
                    dvc.at[pi][rows_q, :] = dv2[BLK:]
                    return carry

                lax.fori_loop(0, TILE // BLK, unit, 0, unroll=4)

            dq_ref[...] = dq_s[0] + dq_s[1] + dq_s[2]

        @pl.when(t > 0)
        def _():
            dk_ref[...] = hold_k[...]
            dv_ref[...] = hold_v[...]

        @pl.when((t > 0) & (t < nt))
        def _():
            for pi, d in enumerate(DILATIONS):
                back = d * BLK
                dk_ref[TILE - back:, :] = dk_ref[TILE - back:, :] + dkp[pi, 0:back, :]
                dv_ref[TILE - back:, :] = dv_ref[TILE - back:, :] + dvp[pi, 0:back, :]

        @pl.when(t < nt)
        def _():
            hold_k[...] = dkc[0] + dkc[1] + dkc[2]
            hold_v[...] = dvc[0] + dvc[1] + dvc[2]
            for pi, d in enumerate(DILATIONS):
                back = d * BLK
                if back < TILE:
                    hold_k[0:TILE - back, :] = hold_k[0:TILE - back, :] + dkp[pi, back:, :]
                    hold_v[0:TILE - back, :] = hold_v[0:TILE - back, :] + dvp[pi, back:, :]

    last = nt - 1
    cur = lambda hp, t: (hp, jnp.minimum(t, last), 0)
    prev = lambda hp, t: (hp, jnp.clip(t - 1, 0, last), 0)
    cur2 = lambda hp, t: (jnp.minimum(t, last), hp)
    prev2 = lambda hp, t: (jnp.maximum(t - 1, 0), hp)
    blk = (1, TILE, LANES)
    blk2 = (TILE, LANES)
    out = jax.ShapeDtypeStruct((t_len, ATTN_W), F32)
    return pl.pallas_call(
        body, name="attn_bwd", grid=(n_hp, nt + 1),
        in_specs=[pl.BlockSpec(blk, cur), pl.BlockSpec(blk, prev), pl.BlockSpec(blk, cur),
                  pl.BlockSpec(blk, prev), pl.BlockSpec(blk, cur), pl.BlockSpec(blk2, cur2),
                  pl.BlockSpec(blk2, cur2), pl.BlockSpec(blk, cur), pl.BlockSpec((LANES, LANES), lambda hp, t: (0, 0))],
        out_specs=[pl.BlockSpec(blk2, cur2), pl.BlockSpec(blk2, prev2), pl.BlockSpec(blk2, prev2)],
        out_shape=[out, out, out],
        scratch_shapes=[pltpu.VMEM((2 * TILE, LANES), F32), pltpu.VMEM((2 * TILE, LANES), F32)]
        + [pltpu.VMEM((n_pat, TILE, LANES), F32)] * 5 + [pltpu.VMEM((TILE, LANES), F32)] * 3,
        compiler_params=_params("arbitrary", "arbitrary"),
    )(q, k, k, v, v, o, do, lse, ones_hp)


def _discretise(lr, li, ldt, br, bi):
    dt = jnp.exp(ldt)
    mag = jnp.exp(lr * dt)
    ab_r, ab_i = mag * jnp.cos(li * dt), mag * jnp.sin(li * dt)
    den = lr * lr + li * li
    nr, ni = ab_r - 1.0, ab_i
    cr = (nr * lr + ni * li) / den
    ci = (ni * lr - nr * li) / den
    return ab_r, ab_i, cr * br - ci * bi, cr * bi + ci * br


def _disc_fwd(lr, li, ldt, br, bi):
    def body(lr_ref, li_ref, ldt_ref, br_ref, bi_ref, ar_o, ai_o, bbr_o, bbi_o):
        outs = _discretise(lr_ref[...], li_ref[...], ldt_ref[...], br_ref[...], bi_ref[...])
        for o_ref, val in zip((ar_o, ai_o, bbr_o, bbi_o), outs):
            o_ref[...] = val

    col = jax.ShapeDtypeStruct(lr.shape, F32)
    mat = jax.ShapeDtypeStruct(br.shape, F32)
    return pl.pallas_call(body, name="s5_disc_fwd", out_shape=[col, col, mat, mat])(lr, li, ldt, br, bi)


def _disc_bwd(lr, li, ldt, br, bi, d_ar, d_ai, d_bbr, d_bbi, group_sum):
    def body(lr_ref, li_ref, ldt_ref, br_ref, bi_ref, c1, c2, c3, c4, gs_ref, dlr_o, dli_o, dldt_o, dbr_o, dbi_o):
        _, vjp = jax.vjp(_discretise, lr_ref[...], li_ref[...], ldt_ref[...], br_ref[...], bi_ref[...])
        dlr, dli, dldt, dbr, dbi = vjp((c1[...], c2[...], c3[...], c4[...]))
        dlr_o[...] = dlr
        dli_o[...] = dli
        dbr_o[...] = dbr
        dbi_o[...] = dbi
        wide = jnp.broadcast_to(dldt, (dldt.shape[0], LANES))
        dldt_o[...] = jnp.dot(gs_ref[...], wide, precision=lax.Precision.HIGHEST, preferred_element_type=F32)

    col = jax.ShapeDtypeStruct(lr.shape, F32)
    mat = jax.ShapeDtypeStruct(br.shape, F32)
    return pl.pallas_call(
        body, name="s5_disc_bwd", out_shape=[col, col, jax.ShapeDtypeStruct((N_GROUPS, LANES), F32), mat, mat],
    )(lr, li, ldt, br, bi, d_ar, d_ai, d_bbr, d_bbi, group_sum)


N_CHUNK = TILE // BLK
HALF = 4


def _cmul(ar, ai, xr, xi):
    return ar * xr - ai * xi, ar * xi + ai * xr


def _power_table(a_ref, tab, sign, reverse):
    ar = [a_ref[0, j:j + 1, :] for j in range(HALF)]
    ai = [sign * a_ref[0, HALF + j:HALF + j + 1, :] for j in range(HALF)]

    def step(s, cur):
        row = pl.ds((BLK - 1 - s) if reverse else s, 1)
        nxt = []
        for j in range(HALF):
            tab.at[j][row, :] = cur[j]
            tab.at[HALF + j][row, :] = cur[HALF + j]
            nxt.append(_cmul(ar[j], ai[j], cur[j], cur[HALF + j]))
        return tuple(p[0] for p in nxt) + tuple(p[1] for p in nxt)

    lax.fori_loop(0, BLK, step, tuple(ar) + tuple(ai))


def _chunk_scan(buf, a_ref, sign, reverse):
    ar = [jnp.broadcast_to(a_ref[0, j:j + 1, :], (N_CHUNK, LANES)) for j in range(HALF)]
    ai = [sign * jnp.broadcast_to(a_ref[0, HALF + j:HALF + j + 1, :], (N_CHUNK, LANES)) for j in range(HALF)]

    def step(i, carry):
        s = (BLK - 1 - i) if reverse else i
        rows = pl.ds(s, N_CHUNK, stride=BLK)
        out = []
        for j in range(HALF):
            pr, pi = _cmul(ar[j], ai[j], carry[j], carry[HALF + j])
            xr = buf.at[j][rows, :] + pr
            xi = buf.at[HALF + j][rows, :] + pi
            buf.at[j][rows, :] = xr
            buf.at[HALF + j][rows, :] = xi
            out.append((xr, xi))
        return tuple(p[0] for p in out) + tuple(p[1] for p in out)

    zero = jnp.zeros((N_CHUNK, LANES), F32)
    lax.fori_loop(0, BLK, step, (zero,) * (2 * HALF))


def _chunk_states(buf, carry_s, xin_s, tab, reverse):
    edge = 0 if reverse else BLK - 1
    top = 0 if reverse else BLK - 1
    pw = [tab[j, top:top + 1, :] for j in range(2 * HALF)]
    cur = [carry_s[j:j + 1, :] for j in range(2 * HALF)]
    order = range(N_CHUNK - 1, -1, -1) if reverse else range(N_CHUNK)
    for c in order:
        for j in range(2 * HALF):
            xin_s[j, c:c + 1, :] = cur[j]
        nxt = []
        for j in range(HALF):
            pr, pi = _cmul(pw[j], pw[HALF + j], cur[j], cur[HALF + j])
            row = c * BLK + edge
            nxt.append((pr + buf[j, row:row + 1, :], pi + buf[HALF + j, row:row + 1, :]))
        cur = [p[0] for p in nxt] + [p[1] for p in nxt]
    for j in range(2 * HALF):
        carry_s[j:j + 1, :] = cur[j]


def _s5_fwd(u, a_cat, b_mat, c_mat, d_skip):
    t_len = u.shape[0]
    nt = t_len // TILE

    def body(u_ref, a_ref, b_ref, c_ref, d_ref, y_ref, st_ref, xs, tab, carry_s, xin_s):
        sb = pl.program_id(1)

        @pl.when(sb == 0)
        def _():
            _power_table(a_ref, tab, 1.0, False)
            carry_s[...] = jnp.zeros_like(carry_s)

        st_ref[0, 0] = carry_s[...]
        uv = u_ref[...]
        bu = _mm(uv.astype(BF16), b_ref[0])
        for j in range(2 * HALF):
            xs[j] = bu[:, j * LANES:(j + 1) * LANES]
        _chunk_scan(xs, a_ref, 1.0, False)
        _chunk_states(xs, carry_s, xin_s, tab, False)

        def fix(c, acc):
            rows = pl.ds(pl.multiple_of(c * BLK, BLK), BLK)
            for j in range(HALF):
                xr_in = xin_s.at[j][pl.ds(c, 1), :]
                xi_in = xin_s.at[HALF + j][pl.ds(c, 1), :]
                pr, pi = _cmul(tab[j], tab[HALF + j], xr_in, xi_in)
                xs.at[j][rows, :] = xs.at[j][rows, :] + pr
                xs.at[HALF + j][rows, :] = xs.at[HALF + j][rows, :] + pi
            return acc

        lax.fori_loop(0, N_CHUNK, fix, 0)
        y = d_ref[0] * uv
        for j in range(2 * HALF):
            y = y + _mm(xs[j].astype(BF16), c_ref[0, j * LANES:(j + 1) * LANES, :])
        y_ref[...] = y

    return pl.pallas_call(
        body, name="s5_fwd", grid=(N_LB, nt),
        in_specs=[pl.BlockSpec((TILE, LANES), lambda lb, sb: (sb, lb)),
                  pl.BlockSpec((1, 2 * HALF, LANES), lambda lb, sb: (lb, 0, 0)),
                  pl.BlockSpec((1, LANES, 2 * HALF * LANES), lambda lb, sb: (lb, 0, 0)),
                  pl.BlockSpec((1, 2 * HALF * LANES, LANES), lambda lb, sb: (lb, 0, 0)),
                  pl.BlockSpec((1, 1, LANES), lambda lb, sb: (lb, 0, 0))],
        out_specs=[pl.BlockSpec((TILE, LANES), lambda lb, sb: (sb, lb)),
                   pl.BlockSpec((1, 1, 2 * HALF, LANES), lambda lb, sb: (lb, sb, 0, 0))],
        out_shape=[jax.ShapeDtypeStruct((t_len, SSM_W), F32), jax.ShapeDtypeStruct((N_LB, nt, 2 * HALF, LANES), F32)],
        scratch_shapes=[pltpu.VMEM((2 * HALF, TILE, LANES), F32), pltpu.VMEM((2 * HALF, BLK, LANES), F32),
                        pltpu.VMEM((2 * HALF, LANES), F32), pltpu.VMEM((2 * HALF, N_CHUNK, LANES), F32)],
        compiler_params=_params("arbitrary", "arbitrary"),
    )(u, a_cat, b_mat, c_mat, d_skip)


def _s5_bwd(u, dy, states, a_cat, b_mat, c_mat, d_skip):
    t_len = u.shape[0]
    nt = t_len // TILE
    last = nt - 1

    def body(u_ref, dy_ref, st_ref, a_ref, b_ref, c_ref, d_ref, du_ref, db_ref, dc_ref, da_ref, dd_ref,
             xs, gs, tab, tabc, carry_s, lam_s, xin_s, lin_s):
        sb = pl.program_id(1)

        @pl.when(sb == 0)
        def _():
            _power_table(a_ref, tab, 1.0, False)
            _power_table(a_ref, tabc, -1.0, True)
            lam_s[...] = jnp.zeros_like(lam_s)
            db_ref[...] = jnp.zeros_like(db_ref)
            dc_ref[...] = jnp.zeros_like(dc_ref)
            da_ref[...] = jnp.zeros_like(da_ref)
            dd_ref[...] = jnp.zeros_like(dd_ref)

        uv = u_ref[...]
        dyv = dy_ref[...]
        ub = uv.astype(BF16)
        dyb = dyv.astype(BF16)
        carry_s[...] = st_ref[0, 0]
        bu = _mm(ub, b_ref[0])
        gy = _nt(dyb, c_ref[0])
        for j in range(2 * HALF):
            xs[j] = bu[:, j * LANES:(j + 1) * LANES]
            gs[j] = gy[:, j * LANES:(j + 1) * LANES]
        _chunk_scan(xs, a_ref, 1.0, False)
        _chunk_states(xs, carry_s, xin_s, tab, False)
        _chunk_scan(gs, a_ref, -1.0, True)
        _chunk_states(gs, lam_s, lin_s, tabc, True)
        last_row = lax.broadcasted_iota(jnp.int32, (BLK, LANES), 0) == BLK - 1

        def fix(c, acc):
            rows = pl.ds(pl.multiple_of(c * BLK, BLK), BLK)
            out = []
            for j in range(HALF):
                xr_in = xin_s.at[j][pl.ds(c, 1), :]
                xi_in = xin_s.at[HALF + j][pl.ds(c, 1), :]
                pr, pi = _cmul(tab[j], tab[HALF + j], xr_in, xi_in)
                xr = xs.at[j][rows, :] + pr
                xi = xs.at[HALF + j][rows, :] + pi
                xs.at[j][rows, :] = xr
                xs.at[HALF + j][rows, :] = xi
                lr_in = lin_s.at[j][pl.ds(c, 1), :]
                li_in = lin_s.at[HALF + j][pl.ds(c, 1), :]
                qr, qi = _cmul(tabc[j], tabc[HALF + j], lr_in, li_in)
                lr_ = gs.at[j][rows, :] + qr
                li_ = gs.at[HALF + j][rows, :] + qi
                gs.at[j][rows, :] = lr_
                gs.at[HALF + j][rows, :] = li_
                nr = jnp.where(last_row, lr_in, pltpu.roll(lr_, BLK - 1, 0))
                ni = jnp.where(last_row, li_in, pltpu.roll(li_, BLK - 1, 0))
                out.append(acc[j] + _colsum(xr * nr + xi * ni))
                out.append(acc[HALF + j] + _colsum(xr * ni - xi * nr))
            return tuple(out[0::2]) + tuple(out[1::2])

        zero = jnp.zeros((1, LANES), F32)
        da = lax.fori_loop(0, N_CHUNK, fix, (zero,) * (2 * HALF))
        for j in range(2 * HALF):
            da_ref[0, j:j + 1, :] = da_ref[0, j:j + 1, :] + da[j]
        lam = jnp.concatenate([gs[j].astype(BF16) for j in range(2 * HALF)], axis=1)
        xcat = jnp.concatenate([xs[j].astype(BF16) for j in range(2 * HALF)], axis=1)
        du_ref[...] = _nt(lam, b_ref[0]) + d_ref[0] * dyv
        db_ref[0] = db_ref[0] + _tn(ub, lam)
        dc_ref[0] = dc_ref[0] + _tn(dyb, xcat)
        dd_ref[0] = dd_ref[0] + _colsum(dyv * uv)

    rev = lambda lb, sb: (last - sb, lb)
    per_lb = lambda lb, sb: (lb, 0, 0)
    wide = 2 * HALF * LANES
    return pl.pallas_call(
        body, name="s5_bwd", grid=(N_LB, nt),
        in_specs=[pl.BlockSpec((TILE, LANES), rev), pl.BlockSpec((TILE, LANES), rev),
                  pl.BlockSpec((1, 1, 2 * HALF, LANES), lambda lb, sb: (lb, last - sb, 0, 0)),
                  pl.BlockSpec((1, 2 * HALF, LANES), per_lb), pl.BlockSpec((1, LANES, wide), per_lb),
                  pl.BlockSpec((1, wide, LANES), per_lb), pl.BlockSpec((1, 1, LANES), per_lb)],
        out_specs=[pl.BlockSpec((TILE, LANES), rev), pl.BlockSpec((1, LANES, wide), per_lb),
                   pl.BlockSpec((1, LANES, wide), per_lb), pl.BlockSpec((1, 2 * HALF, LANES), per_lb),
                   pl.BlockSpec((1, 1, LANES), per_lb)],
        out_shape=[jax.ShapeDtypeStruct((t_len, SSM_W), F32), jax.ShapeDtypeStruct((N_LB, LANES, wide), F32),
                   jax.ShapeDtypeStruct((N_LB, LANES, wide), F32), jax.ShapeDtypeStruct((N_LB, 2 * HALF, LANES), F32),
                   jax.ShapeDtypeStruct((N_LB, 1, LANES), F32)],
        scratch_shapes=[pltpu.VMEM((2 * HALF, TILE, LANES), F32), pltpu.VMEM((2 * HALF, TILE, LANES), F32),
                        pltpu.VMEM((2 * HALF, BLK, LANES), F32), pltpu.VMEM((2 * HALF, BLK, LANES), F32),
                        pltpu.VMEM((2 * HALF, LANES), F32), pltpu.VMEM((2 * HALF, LANES), F32),
                        pltpu.VMEM((2 * HALF, N_CHUNK, LANES), F32), pltpu.VMEM((2 * HALF, N_CHUNK, LANES), F32)],
        compiler_params=_params("arbitrary", "arbitrary"),
    )(u, dy, states, a_cat, b_mat, c_mat, d_skip)


_GELU_C = math.sqrt(2.0 / math.pi)
_GELU_K = 0.044715


def _gelu(y):
    t = jnp.tanh(_GELU_C * (y + _GELU_K * (y * y * y)))
    return y * (0.5 * (1.0 + t)), t


def _gelu_grad(y, t):
    return 0.5 * (1.0 + t) + 0.5 * y * (1.0 - t * t) * (_GELU_C * (1.0 + 3.0 * _GELU_K * y * y))


def _glu(y, wg, bias):
    z, t = _gelu(y)
    sg = jax.nn.sigmoid(_mm(z.astype(BF16), wg) + bias)
    return z, t, sg


def _mix_fwd(attn, y, x, wg, glu_b, ga, gs, wo):
    t_len = x.shape[0]
    tm = 512

    def body(attn_ref, y_ref, x_ref, wg_ref, b_ref, ga_ref, gs_ref, wo_ref, x2_ref, mix_ref, z_ref):
        z, _, sg = _glu(y_ref[...], wg_ref[...], b_ref[...])
        z_ref[...] = z.astype(BF16)
        s = z * sg
        av = attn_ref[...]
        an = (av * _rms(av) * ga_ref[...]).astype(BF16)
        sn = (s * _rms(s) * gs_ref[...]).astype(BF16)
        mix_ref[:, 0:ATTN_W] = an
        mix_ref[:, ATTN_W:] = sn
        x2_ref[...] = x_ref[...] + _mm(an, wo_ref[0:ATTN_W, :]) + _mm(sn, wo_ref[ATTN_W:, :])

    row = lambda i: (i, 0)
    const = lambda i: (0, 0)
    return pl.pallas_call(
        body, name="mix_fwd", grid=(t_len // tm,),
        in_specs=[pl.BlockSpec((tm, ATTN_W), row), pl.BlockSpec((tm, SSM_W), row), pl.BlockSpec((tm, D_MODEL), row),
                  pl.BlockSpec((SSM_W, SSM_W), const), pl.BlockSpec((1, SSM_W), const), pl.BlockSpec((1, ATTN_W), const),
                  pl.BlockSpec((1, SSM_W), const), pl.BlockSpec((D_MODEL, D_MODEL), const)],
        out_specs=[pl.BlockSpec((tm, D_MODEL), row), pl.BlockSpec((tm, D_MODEL), row), pl.BlockSpec((tm, SSM_W), row)],
        out_shape=[jax.ShapeDtypeStruct((t_len, D_MODEL), F32), jax.ShapeDtypeStruct((t_len, D_MODEL), BF16),
                   jax.ShapeDtypeStruct((t_len, SSM_W), BF16)],
        compiler_params=_params("arbitrary"),
    )(attn, y, x, wg, glu_b, ga, gs, wo)


def _mlp(x2, target, g2, wu, wd):
    t_len = x2.shape[0]
    tm = 256
    fc = 1024
    n_fc = D_FF // fc

    def body(x2_ref, tg_ref, g2_ref, wu_hbm, wd_hbm, dx2_ref, hdn_ref, dup_ref, h_ref, dyb_ref, dg2_ref, loss_ref,
             wu_s, wd_s, relu_s, sem):
        @pl.when(pl.program_id(0) == 0)
        def _():
            cu = pltpu.make_async_copy(wu_hbm, wu_s, sem.at[0])
            cd = pltpu.make_async_copy(wd_hbm, wd_s, sem.at[1])
            cu.start()
            cd.start()
            cu.wait()
            cd.wait()
            dg2_ref[...] = jnp.zeros_like(dg2_ref)
            loss_ref[...] = jnp.zeros_like(loss_ref)

        x2v = x2_ref[...]
        r = _rms(x2v)
        g2v = g2_ref[...]
        h = (x2v * r * g2v).astype(BF16)
        h_ref[...] = h
        yout = x2v
        for c in range(n_fc):
            cols = slice(c * fc, (c + 1) * fc)
            ru = jnp.maximum(_mm(h, wu_s[:, cols]), 0.0)
            relu_s[:, cols] = ru
            hd = (ru * ru).astype(BF16)
            hdn_ref[:, cols] = hd
            yout = yout + _mm(hd, wd_s[cols, :])
        err = yout - tg_ref[...]
        loss_ref[...] = loss_ref[...] + 0.5 * jnp.sum(err * err) * (1.0 / D_MODEL)
        dy = err * (1.0 / D_MODEL)
        dyb = dy.astype(BF16)
        dyb_ref[...] = dyb
        dh = jnp.zeros((tm, D_MODEL), F32)
        for c in range(n_fc):
            cols = slice(c * fc, (c + 1) * fc)
            dup = (_nt(dyb, wd_s[cols, :]) * (2.0 * relu_s[:, cols])).astype(BF16)
            dup_ref[:, cols] = dup
            dh = dh + _nt(dup, wu_s[:, cols])
        dxn, gterm = _rms_bwd(dh, x2v, r, g2v)
        dx2_ref[...] = dy + dxn
        dg2_ref[...] = dg2_ref[...] + _colsum(gterm)

    row = lambda i: (i, 0)
    const = lambda i: (0, 0)
    any_spec = pl.BlockSpec(memory_space=pl.ANY)
    return pl.pallas_call(
        body, name="mlp", grid=(t_len // tm,),
        in_specs=[pl.BlockSpec((tm, D_MODEL), row), pl.BlockSpec((tm, D_MODEL), row), pl.BlockSpec((1, D_MODEL), const),
                  any_spec, any_spec],
        out_specs=[pl.BlockSpec((tm, D_MODEL), row), pl.BlockSpec((tm, D_FF), row), pl.BlockSpec((tm, D_FF), row),
                   pl.BlockSpec((tm, D_MODEL), row), pl.BlockSpec((tm, D_MODEL), row), pl.BlockSpec((1, D_MODEL), const),
                   pl.BlockSpec((1, LANES), const)],
        out_shape=[jax.ShapeDtypeStruct((t_len, D_MODEL), F32), jax.ShapeDtypeStruct((t_len, D_FF), BF16),
                   jax.ShapeDtypeStruct((t_len, D_FF), BF16), jax.ShapeDtypeStruct((t_len, D_MODEL), BF16),
                   jax.ShapeDtypeStruct((t_len, D_MODEL), BF16), jax.ShapeDtypeStruct((1, D_MODEL), F32),
                   jax.ShapeDtypeStruct((1, LANES), F32)],
        scratch_shapes=[pltpu.VMEM((D_MODEL, D_FF), BF16), pltpu.VMEM((D_FF, D_MODEL), BF16),
                        pltpu.VMEM((tm, D_FF), F32), pltpu.SemaphoreType.DMA((2,))],
        compiler_params=_params("arbitrary"),
    )(x2, target, g2, wu, wd)


def _mix_bwd(dx2, attn, y, wg, glu_b, ga, gs, wo):
    t_len = dx2.shape[0]
    tm = 512

    def body(dx2_ref, attn_ref, y_ref, wg_ref, b_ref, ga_ref, gs_ref, wo_ref,
             dattn_ref, dy_ref, dx2b_ref, dgp_ref, dga_ref, dgs_ref, db_ref):
        @pl.when(pl.program_id(0) == 0)
        def _():
            dga_ref[...] = jnp.zeros_like(dga_ref)
            dgs_ref[...] = jnp.zeros_like(dgs_ref)
            db_ref[...] = jnp.zeros_like(db_ref)

        dx2b = dx2_ref[...].astype(BF16)
        dx2b_ref[...] = dx2b
        d_an = _nt(dx2b, wo_ref[0:ATTN_W, :])
        d_sn = _nt(dx2b, wo_ref[ATTN_W:, :])
        yv = y_ref[...]
        wg = wg_ref[...]
        z, t, sg = _glu(yv, wg, b_ref[...])
        s = z * sg
        av = attn_ref[...]
        d_attn, ga_term = _rms_bwd(d_an, av, _rms(av), ga_ref[...])
        d_s, gs_term = _rms_bwd(d_sn, s, _rms(s), gs_ref[...])
        dattn_ref[...] = d_attn
        dgp = d_s * z * sg * (1.0 - sg)
        dgpb = dgp.astype(BF16)
        dgp_ref[...] = dgpb
        dz = d_s * sg + _nt(dgpb, wg)
        dy_ref[...] = dz * _gelu_grad(yv, t)
        dga_ref[...] = dga_ref[...] + _colsum(ga_term)
        dgs_ref[...] = dgs_ref[...] + _colsum(gs_term)
        db_ref[...] = db_ref[...] + _colsum(dgp)

    row = lambda i: (i, 0)
    const = lambda i: (0, 0)
    vec = jax.ShapeDtypeStruct((1, SSM_W), F32)
    return pl.pallas_call(
        body, name="mix_bwd", grid=(t_len // tm,),
        in_specs=[pl.BlockSpec((tm, D_MODEL), row), pl.BlockSpec((tm, ATTN_W), row), pl.BlockSpec((tm, SSM_W), row),
                  pl.BlockSpec((SSM_W, SSM_W), const), pl.BlockSpec((1, SSM_W), const), pl.BlockSpec((1, ATTN_W), const),
                  pl.BlockSpec((1, SSM_W), const), pl.BlockSpec((D_MODEL, D_MODEL), const)],
        out_specs=[pl.BlockSpec((tm, ATTN_W), row), pl.BlockSpec((tm, SSM_W), row), pl.BlockSpec((tm, D_MODEL), row),
                   pl.BlockSpec((tm, SSM_W), row), pl.BlockSpec((1, ATTN_W), const), pl.BlockSpec((1, SSM_W), const),
                   pl.BlockSpec((1, SSM_W), const)],
        out_shape=[jax.ShapeDtypeStruct((t_len, ATTN_W), F32), jax.ShapeDtypeStruct((t_len, SSM_W), F32),
                   jax.ShapeDtypeStruct((t_len, D_MODEL), BF16), jax.ShapeDtypeStruct((t_len, SSM_W), BF16), vec, vec, vec],
        compiler_params=_params("arbitrary"),
    )(dx2, attn, y, wg, glu_b, ga, gs, wo)


def _inproj_bwd(dqs, dkn, dv, du, q_raw, k_raw, x, dx2, wi, g1, gq, gk, ones64):
    t_len = x.shape[0]
    tm = 512
    n_heads = ATTN_W // HEAD

    def body(dqs_ref, dkn_ref, dv_ref, du_ref, q_ref, k_ref, x_ref, dx2_ref, wi_ref, g1_ref, gq_ref, gk_ref, bd_ref,
             gx_ref, dproj_ref, dg1_ref, dgq_ref, dgk_ref, accq, acck):
        i = pl.program_id(0)

        @pl.when(i == 0)
        def _():
            dg1_ref[...] = jnp.zeros_like(dg1_ref)
            accq[...] = jnp.zeros_like(accq)
            acck[...] = jnp.zeros_like(acck)

        bd = bd_ref[...]

        def head_norm_bwd(dy, raw, gain, acc):
            r = lax.rsqrt(_group_mean(raw * raw, bd, HEAD) + EPS)
            xh = raw * r
            dxh = dy * gain
            acc[...] = acc[...] + _colsum(dy * xh)
            return r * (dxh - xh * _group_mean(dxh * xh, bd, HEAD))

        dq = head_norm_bwd(dqs_ref[...] * (HEAD ** -0.5), q_ref[...], gq_ref[...], accq)
        dk = head_norm_bwd(dkn_ref[...], k_ref[...], gk_ref[...], acck)
        dproj_ref[:, 0:ATTN_W] = dq.astype(BF16)
        dproj_ref[:, ATTN_W:2 * ATTN_W] = dk.astype(BF16)
        dproj_ref[:, 2 * ATTN_W:3 * ATTN_W] = dv_ref[...].astype(BF16)
        dproj_ref[:, 3 * ATTN_W:] = du_ref[...].astype(BF16)
        dxn = _nt(dproj_ref[...], wi_ref[...])
        xv = x_ref[...]
        g1v = g1_ref[...]
        dx, g1_term = _rms_bwd(dxn, xv, _rms(xv), g1v)
        gx_ref[...] = dx2_ref[...] + dx
        dg1_ref[...] = dg1_ref[...] + _colsum(g1_term)

        @pl.when(i == pl.num_programs(0) - 1)
        def _():
            for acc, out in ((accq, dgq_ref), (acck, dgk_ref)):
                tot = acc[:, 0:HEAD]
                for h in range(1, n_heads):
                    tot = tot + acc[:, h * HEAD:(h + 1) * HEAD]
                out[...] = tot

    row = lambda i: (i, 0)
    const = lambda i: (0, 0)
    aw = pl.BlockSpec((tm, ATTN_W), row)
    dm = pl.BlockSpec((tm, D_MODEL), row)
    return pl.pallas_call(
        body, name="inproj_bwd", grid=(t_len // tm,),
        in_specs=[aw, aw, aw, aw, aw, aw, dm, dm, pl.BlockSpec((D_MODEL, PROJ_W), const), pl.BlockSpec((1, D_MODEL), const),
                  pl.BlockSpec((1, ATTN_W), const), pl.BlockSpec((1, ATTN_W), const), pl.BlockSpec((ATTN_W, ATTN_W), const)],
        out_specs=[dm, pl.BlockSpec((tm, PROJ_W), row), pl.BlockSpec((1, D_MODEL), const),
                   pl.BlockSpec((1, HEAD), const), pl.BlockSpec((1, HEAD), const)],
        out_shape=[jax.ShapeDtypeStruct((t_len, D_MODEL), F32), jax.ShapeDtypeStruct((t_len, PROJ_W), BF16),
                   jax.ShapeDtypeStruct((1, D_MODEL), F32), jax.ShapeDtypeStruct((1, HEAD), F32),
                   jax.ShapeDtypeStruct((1, HEAD), F32)],
        scratch_shapes=[pltpu.VMEM((1, ATTN_W), F32), pltpu.VMEM((1, ATTN_W), F32)],
        compiler_params=_params("arbitrary"),
    )(dqs, dkn, dv, du, q_raw, k_raw, x, dx2, wi, g1, gq, gk, ones64)


def _grad_matmul(a, b, name):
    t_len, m = a.shape
    n = b.shape[1]
    bm, bn, bt = min(m, 1024), min(n, 1024), 1024

    def body(a_ref, b_ref, o_ref):
        @pl.when(pl.program_id(2) == 0)
        def _():
            o_ref[...] = jnp.zeros_like(o_ref)

        o_ref[...] = o_ref[...] + _tn(a_ref[...], b_ref[...])

    return pl.pallas_call(
        body, name=name, grid=(m // bm, n // bn, t_len // bt),
        in_specs=[pl.BlockSpec((bt, bm), lambda i, j, k: (k, i)), pl.BlockSpec((bt, bn), lambda i, j, k: (k, j))],
        out_specs=pl.BlockSpec((bm, bn), lambda i, j, k: (i, j)),
        out_shape=jax.ShapeDtypeStruct((m, n), F32),
        compiler_params=_params("arbitrary", "arbitrary", "arbitrary"),
    )(a, b)


def _adamw(w, g, m, v, name):
    rows, cols = w.shape
    br = _row_block(rows, 256)

    def body(w_ref, g_ref, m_ref, v_ref, d_o, m_o, v_o):
        gv = g_ref[...]
        mn = ADAM_B1 * m_ref[...] + (1.0 - ADAM_B1) * gv
        vn = ADAM_B2 * v_ref[...] + (1.0 - ADAM_B2) * jnp.square(gv)
        m_hat = mn / (1.0 - ADAM_B1 ** ADAM_STEP)
        v_hat = vn / (1.0 - ADAM_B2 ** ADAM_STEP)
        d_o[...] = -ADAM_LR * (m_hat / (jnp.sqrt(v_hat) + ADAM_EPS) + ADAM_WD * w_ref[...])
        m_o[...] = mn
        v_o[...] = vn

    spec = pl.BlockSpec((br, cols), lambda i: (i, 0))
    shape = jax.ShapeDtypeStruct((rows, cols), F32)
    return pl.pallas_call(
        body, name=name, grid=(rows // br,), in_specs=[spec] * 4, out_specs=[spec] * 3, out_shape=[shape] * 3,
        compiler_params=_params("arbitrary"),
    )(w, g, m, v)


def _sum_arrays(arrs, name, out_dtype=F32):
    rows, cols = arrs[0].shape
    br = _row_block(rows, 512)
    n = len(arrs)

    def body(*refs):
        tot = refs[0][...]
        for r in refs[1:n]:
            tot = tot + r[...]
        refs[n][...] = tot.astype(out_dtype)

    spec = pl.BlockSpec((br, cols), lambda i: (i, 0))
    return pl.pallas_call(
        body, name=name, grid=(rows // br,), in_specs=[spec] * n, out_specs=spec,
        out_shape=jax.ShapeDtypeStruct((rows, cols), out_dtype), compiler_params=_params("arbitrary"),
    )(*arrs)


GPL = N_GROUPS // N_LB
SW = GPL * N_STATE


def _eye_groups():
    return jnp.eye(GPL, dtype=F32)


def _s5_matrices(ab_r, ab_i, bb_r, bb_i, c_re, c_im, d_skip):
    eye = _eye_groups()
    a_cat = jnp.concatenate([ab_r.reshape(N_LB, HALF, LANES), ab_i.reshape(N_LB, HALF, LANES)], axis=1)

    def b_part(bb):
        b4 = jnp.transpose(bb.reshape(N_LB, GPL, N_STATE, GROUP_W), (0, 1, 3, 2))
        return (b4[:, :, :, None, :] * eye[None, :, None, :, None]).reshape(N_LB, LANES, SW)

    def c_part(cc):
        c4 = jnp.transpose(cc.reshape(N_LB, GPL, GROUP_W, N_STATE), (0, 1, 3, 2))
        return (c4[:, :, :, None, :] * eye[None, :, None, :, None]).reshape(N_LB, SW, LANES)

    b_mat = jnp.concatenate([b_part(bb_r), b_part(bb_i)], axis=2).astype(BF16)
    c_mat = jnp.concatenate([c_part(c_re), -c_part(c_im)], axis=1).astype(BF16)
    return a_cat, b_mat, c_mat, d_skip.reshape(N_LB, 1, LANES)


def _s5_unpack_grads(db, dc, da, dd):
    eye = _eye_groups()
    mask = eye[None, :, None, None, :, None]
    d6 = jnp.sum(db.reshape(N_LB, GPL, GROUP_W, 2, GPL, N_STATE) * mask, axis=4)
    dbb = jnp.transpose(d6, (3, 0, 1, 4, 2)).reshape(2, N_GROUPS * N_STATE, GROUP_W)
    c6 = jnp.sum(dc.reshape(N_LB, GPL, GROUP_W, 2, GPL, N_STATE) * mask, axis=4)
    dcc = jnp.transpose(c6, (3, 0, 1, 2, 4)).reshape(2, N_GROUPS, GROUP_W, N_STATE)
    dab_r = da[:, :HALF].reshape(N_GROUPS * N_STATE, 1)
    dab_i = da[:, HALF:].reshape(N_GROUPS * N_STATE, 1)
    return dab_r, dab_i, dbb[0], dbb[1], dcc[0], -dcc[1], dd.reshape(N_GROUPS, GROUP_W)


def _block_ones(n, width):
    i = lax.broadcasted_iota(jnp.int32, (n, n), 0) // width
    j = lax.broadcasted_iota(jnp.int32, (n, n), 1) // width
    return (i == j).astype(BF16)


def _tile_heads(g):
    return jnp.tile(g.reshape(1, HEAD), (1, ATTN_W // HEAD))


def _local_step(x, target, wi, wg, wo, wu, wd, p):
    ones64 = _block_ones(ATTN_W, HEAD)
    ones_hp = _block_ones(LANES, HEAD)
    g1 = p["norm1_g"].reshape(1, D_MODEL)
    g2 = p["norm2_g"].reshape(1, D_MODEL)
    gq = _tile_heads(p["q_norm_g"])
    gk = _tile_heads(p["k_norm_g"])
    ga = p["attn_out_norm_g"].reshape(1, ATTN_W)
    gs = p["ssm_out_norm_g"].reshape(1, SSM_W)
    glu_b = p["glu_b"].reshape(1, SSM_W)
    n_gp = N_GROUPS * N_STATE
    lr = p["ssm_a_re"].reshape(n_gp, 1)
    li = p["ssm_a_im"].reshape(n_gp, 1)
    ldt = jnp.repeat(p["ssm_log_dt"].reshape(N_GROUPS), N_STATE).reshape(n_gp, 1)
    br = p["ssm_b_re"].reshape(n_gp, GROUP_W)
    bi = p["ssm_b_im"].reshape(n_gp, GROUP_W)
    ab_r, ab_i, bb_r, bb_i = _disc_fwd(lr, li, ldt, br, bi)
    a_cat, b_mat, c_mat, d_mat = _s5_matrices(
        ab_r, ab_i, bb_r, bb_i, p["ssm_c_re"].reshape(N_GROUPS, GROUP_W, N_STATE),
        p["ssm_c_im"].reshape(N_GROUPS, GROUP_W, N_STATE), p["ssm_d"])

    xn, qn, kn, vv, u, q_raw, k_raw = _inproj_fwd(x, g1, wi, gq, gk, ones64)
    attn, lse = _attn_fwd(qn, kn, vv)
    y, states = _s5_fwd(u, a_cat, b_mat, c_mat, d_mat)
    x2, mix, z = _mix_fwd(attn, y, x, wg, glu_b, ga, gs, wo)
    dx2, hdn, dup, h, dyb, dg2, loss = _mlp(x2, target, g2, wu, wd)
    d_attn, dy_ssm, dx2b, dgp, dga, dgs, dglu_b = _mix_bwd(dx2, attn, y, wg, glu_b, ga, gs, wo)
    dqs, dkn, dvv = _attn_bwd(qn, kn, vv, attn, d_attn, lse, ones_hp)
    du, db, dc, da, dd = _s5_bwd(u, dy_ssm, states, a_cat, b_mat, c_mat, d_mat)
    grad_x, dproj, dg1, dgq, dgk = _inproj_bwd(dqs, dkn, dvv, du, q_raw, k_raw, x, dx2, wi, g1, gq, gk, ones64)

    big = {
        "w_in": _grad_matmul(xn, dproj, "grad_w_in"),
        "glu_w": _grad_matmul(z, dgp, "grad_glu_w"),
        "w_out": _grad_matmul(mix, dx2b, "grad_w_out"),
        "w_mlp_up": _grad_matmul(h, dup, "grad_w_mlp_up"),
        "w_mlp_down": _grad_matmul(hdn, dyb, "grad_w_mlp_down"),
    }
    dab_r, dab_i, dbb_r, dbb_i, dc_re, dc_im, dd_g = _s5_unpack_grads(db, dc, da, dd)
    cot = {"norm1_g": dg1, "q_norm_g": dgq, "k_norm_g": dgk, "ab_r": dab_r, "ab_i": dab_i, "bb_r": dbb_r, "bb_i": dbb_i,
           "ssm_c_re": dc_re, "ssm_c_im": dc_im, "ssm_d": dd_g, "glu_b": dglu_b, "attn_out_norm_g": dga,
           "ssm_out_norm_g": dgs, "norm2_g": dg2}
    return loss[0, 0], grad_x, big, cot, (lr, li, ldt, br, bi)


COT_NAMES = ("norm1_g", "q_norm_g", "k_norm_g", "ab_r", "ab_i", "bb_r", "bb_i", "ssm_c_re", "ssm_c_im", "ssm_d",
             "glu_b", "attn_out_norm_g", "ssm_out_norm_g", "norm2_g")
SMALL_NAMES = ("norm1_g", "q_norm_g", "k_norm_g", "ssm_a_re", "ssm_a_im", "ssm_log_dt", "ssm_b_re", "ssm_b_im",
               "ssm_c_re", "ssm_c_im", "ssm_d", "glu_b", "attn_out_norm_g", "ssm_out_norm_g", "norm2_g")
BIG_NAMES = ("w_in", "glu_w", "w_out", "w_mlp_up", "w_mlp_down")
PACK_ROWS = 1152


def _pack(arrs):
    flat = jnp.concatenate([a.reshape(-1) for a in arrs])
    return jnp.pad(flat, (0, PACK_ROWS * LANES - flat.shape[0])).reshape(PACK_ROWS, LANES)


def _unpack(packed, like):
    flat = packed.reshape(-1)
    out, pos = [], 0
    for a in like:
        out.append(flat[pos:pos + a.size].reshape(a.shape))
        pos += a.size
    return out


def _small_grads(cot, disc_in, p):
    lr, li, ldt, br, bi = disc_in
    group_sum = (lax.broadcasted_iota(jnp.int32, (N_GROUPS, N_GROUPS * N_STATE), 1) // N_STATE
                 == lax.broadcasted_iota(jnp.int32, (N_GROUPS, N_GROUPS * N_STATE), 0)).astype(F32)
    dlr, dli, dldt, dbr, dbi = _disc_bwd(lr, li, ldt, br, bi, cot["ab_r"], cot["ab_i"], cot["bb_r"], cot["bb_i"], group_sum)
    g = dict(cot)
    g.update(ssm_a_re=dlr, ssm_a_im=dli, ssm_log_dt=dldt[:, 0], ssm_b_re=dbr, ssm_b_im=dbi)
    return {n: g[n].reshape(p[n].shape) for n in SMALL_NAMES}


BIG = {
    "w_in": ((D_MODEL, PROJ_W), 1, PROJ_W // 4, 0, D_MODEL // 2),
    "glu_w": ((SSM_W, SSM_W), 0, SSM_W // 4, 1, SSM_W // 2),
    "w_out": ((D_MODEL, D_MODEL), 0, D_MODEL // 4, 1, D_MODEL // 2),
    "w_mlp_up": ((D_MODEL, D_FF), 1, D_FF // 4, 0, D_MODEL // 2),
    "w_mlp_down": ((D_FF, D_MODEL), 0, D_FF // 4, 1, D_MODEL // 2),
}
N_BIG = len(BIG_NAMES)
N_CHIPS = 4
ANY = pl.BlockSpec(memory_space=pl.ANY)


def _cut(name, shard=False, half=False):
    shape, s_ax, s_sz, h_ax, h_sz = BIG[name]
    shape = list(shape)
    if shard:
        shape[s_ax] = s_sz
    if half:
        shape[h_ax] = h_sz
    return tuple(shape)


def _window(name, base, shard=None, half=None):
    _, s_ax, s_sz, h_ax, h_sz = BIG[name]
    idx = [pl.ds(0, base[0]), pl.ds(0, base[1])]
    if shard is not None:
        idx[s_ax] = pl.ds(pl.multiple_of(shard * s_sz, s_sz), s_sz)
    if half is not None:
        idx[h_ax] = pl.ds(pl.multiple_of(half * h_sz, h_sz), h_sz)
    return tuple(idx)


def _mesh_pos():
    return lax.axis_index("x"), lax.axis_index("y"), lax.axis_index("c")


def _other_chips(x, y):
    return [(1 - x, y, 2 * (1 - x) + y), (x, 1 - y, 2 * x + 1 - y), (1 - x, 1 - y, 2 * (1 - x) + 1 - y)]


def _remote(src, dst, send_sem, recv_sem, dev):
    return pltpu.make_async_remote_copy(src_ref=src, dst_ref=dst, send_sem=send_sem, recv_sem=recv_sem,
                                        device_id=dev, device_id_type=MESH)


CHUNK_BYTES = 256 * 1024


def _row_chunks(shape, dtype):
    rows, cols = shape
    n = 1
    while rows % (2 * n) == 0 and (rows // (2 * n)) % 16 == 0 and rows * cols * jnp.dtype(dtype).itemsize // n > CHUNK_BYTES:
        n *= 2
    return [(pl.ds(i * (rows // n), rows // n), slice(None)) for i in range(n)]


def _start_remote(src, dst, send_sem, recv_sem, dev, shape, dtype):
    for sl in _row_chunks(shape, dtype):
        _remote(src.at[sl], dst.at[sl], send_sem, recv_sem, dev).start()
    return _remote(src, dst, send_sem, recv_sem, dev)


def _start_local(src, dst, sem, shape, dtype):
    for sl in _row_chunks(shape, dtype):
        pltpu.make_async_copy(src.at[sl], dst.at[sl], sem).start()
    return pltpu.make_async_copy(src, dst, sem)


def _gather_weights(shards):
    def body(*refs):
        ins, outs, stage = refs[0:N_BIG], refs[N_BIG:2 * N_BIG], refs[2 * N_BIG:3 * N_BIG]
        send, recv, fsend, frecv, lsem = refs[3 * N_BIG:]
        x, y, c = _mesh_pos()
        me = 2 * x + y
        sib = (x, y, 1 - c)
        chips = _other_chips(x, y)
        pending = []
        for w, n in enumerate(BIG_NAMES):
            stage[w][...] = ins[w][...].astype(BF16)
            full = BIG[n][0]
            pending.append(_start_local(stage[w], outs[w].at[_window(n, full, shard=me)], lsem.at[w],
                                        _cut(n, shard=True), BF16))
        sends = []
        for k, (px, py, _) in enumerate(chips):
            for w, n in enumerate(BIG_NAMES):
                s = k * N_BIG + w
                sends.append(_start_remote(stage[w].at[_window(n, _cut(n, shard=True), half=c)],
                                           outs[w].at[_window(n, BIG[n][0], shard=me, half=c)], send.at[s], recv.at[s],
                                           (px, py, c), _cut(n, shard=True, half=True), BF16))
        for k, (px, py, pj) in enumerate(chips):
            for w, n in enumerate(BIG_NAMES):
                s = k * N_BIG + w
                got = outs[w].at[_window(n, BIG[n][0], shard=pj, half=c)]
                _remote(got, got, send.at[s], recv.at[s], (px, py, c)).wait_recv()
                sends.append(_start_remote(got, got, fsend.at[s], frecv.at[s], sib, _cut(n, shard=True, half=True), BF16))
        for k, (px, py, pj) in enumerate(chips):
            for w, n in enumerate(BIG_NAMES):
                s = k * N_BIG + w
                got = outs[w].at[_window(n, BIG[n][0], shard=pj, half=1 - c)]
                _remote(got, got, fsend.at[s], frecv.at[s], sib).wait_recv()
        for cp in sends:
            cp.wait_send()
        for cp in pending:
            cp.wait()

    n_sem = (N_CHIPS - 1) * N_BIG
    outs = pl.pallas_call(
        body, name="gather_weights",
        in_specs=[pl.BlockSpec(memory_space=pltpu.VMEM)] * N_BIG, out_specs=[ANY] * N_BIG,
        out_shape=[jax.ShapeDtypeStruct(BIG[n][0], BF16) for n in BIG_NAMES],
        scratch_shapes=[pltpu.VMEM(_cut(n, shard=True), BF16) for n in BIG_NAMES]
        + [pltpu.SemaphoreType.DMA((n_sem,))] * 4 + [pltpu.SemaphoreType.DMA((N_BIG,))],
        compiler_params=pltpu.CompilerParams(vmem_limit_bytes=VMEM_LIMIT),
    )(*[shards[n] for n in BIG_NAMES])
    return outs


def _pair_exchange(big, packed):
    def body(*refs):
        ins, small = refs[0:N_BIG], refs[N_BIG]
        got = refs[N_BIG + 1:2 * N_BIG + 2]
        send, recv = refs[2 * N_BIG + 2:]
        x, y, c = _mesh_pos()
        sib = (x, y, 1 - c)
        copies = []
        for w, n in enumerate(BIG_NAMES):
            copies.append(_start_remote(ins[w].at[_window(n, BIG[n][0], half=1 - c)], got[w], send.at[w], recv.at[w], sib,
                                        _cut(n, half=True), F32))
        copies.append(_start_remote(small, got[N_BIG], send.at[N_BIG], recv.at[N_BIG], sib, small.shape, F32))
        for cp in copies:
            cp.wait()

    halves = [jax.ShapeDtypeStruct(_cut(n, half=True), F32) for n in BIG_NAMES]
    outs = pl.pallas_call(
        body, name="grad_pair_exchange", in_specs=[ANY] * (N_BIG + 1), out_specs=[ANY] * (N_BIG + 1),
        out_shape=halves + [jax.ShapeDtypeStruct(packed.shape, F32)],
        scratch_shapes=[pltpu.SemaphoreType.DMA((N_BIG + 1,)), pltpu.SemaphoreType.DMA((N_BIG + 1,))],
    )(*[big[n] for n in BIG_NAMES], packed)
    return outs[0:N_BIG], outs[N_BIG]


def _pair_sum(name, full, got, core):
    _, _, _, h_ax, _ = BIG[name]
    rows, cols = _cut(name, half=True)
    br = _row_block(rows, 512)
    nb = rows // br
    own_map = (lambda i, c: (i + c[0] * nb, 0)) if h_ax == 0 else (lambda i, c: (i, c[0]))

    def body(c_ref, own_ref, got_ref, o_ref):
        o_ref[...] = (own_ref[...] + got_ref[...]).astype(BF16)

    plain = pl.BlockSpec((br, cols), lambda i, c: (i, 0))
    return pl.pallas_call(
        body, name="pair_sum_" + name,
        grid_spec=pltpu.PrefetchScalarGridSpec(num_scalar_prefetch=1, grid=(nb,),
                                               in_specs=[pl.BlockSpec((br, cols), own_map), plain], out_specs=plain),
        out_shape=jax.ShapeDtypeStruct((rows, cols), BF16), compiler_params=_params("arbitrary"),
    )(core, full, got)


def _chip_exchange(halves, packed):
    n_all = N_BIG + 1

    def body(*refs):
        ins, outs = refs[0:n_all], refs[n_all:2 * n_all]
        send, recv = refs[2 * n_all:]
        x, y, c = _mesh_pos()
        me = 2 * x + y
        chips = _other_chips(x, y)

        def piece(w, shard):
            if w == N_BIG:
                return ins[w]
            n = BIG_NAMES[w]
            return ins[w].at[_window(n, _cut(n, half=True), shard=shard)]

        def kind(w):
            if w == N_BIG:
                return packed.shape, F32
            return _cut(BIG_NAMES[w], shard=True, half=True), BF16

        copies = []
        for k, (px, py, pj) in enumerate(chips):
            for w in range(n_all):
                s = k * n_all + w
                copies.append(_start_remote(piece(w, pj), outs[w].at[me], send.at[s], recv.at[s], (px, py, c), *kind(w)))
        for k, (px, py, pj) in enumerate(chips):
            for w in range(n_all):
                s = k * n_all + w
                _remote(piece(w, me), outs[w].at[pj], send.at[s], recv.at[s], (px, py, c)).wait_recv()
        for cp in copies:
            cp.wait_send()

    shapes = [jax.ShapeDtypeStruct((N_CHIPS,) + _cut(n, shard=True, half=True), BF16) for n in BIG_NAMES]
    shapes.append(jax.ShapeDtypeStruct((N_CHIPS,) + packed.shape, F32))
    n_sem = (N_CHIPS - 1) * n_all
    return pl.pallas_call(
        body, name="grad_chip_exchange", in_specs=[ANY] * n_all, out_specs=[ANY] * n_all, out_shape=shapes,
        scratch_shapes=[pltpu.SemaphoreType.DMA((n_sem,)), pltpu.SemaphoreType.DMA((n_sem,))],
    )(*halves, packed)


def _chip_sum(name, own, slots, chip):
    n_slot, rows, cols = slots.shape
    br = _row_block(rows, 512)
    nb = rows // br
    if name in BIG and BIG[name][1] == 1:
        own_map = lambda i, m: (i, m[0])
    elif name in BIG:
        own_map = lambda i, m: (i + m[0] * nb, 0)
    else:
        own_map = lambda i, m: (i, 0)

    def slot_map(j):
        return lambda i, m: (jnp.where(m[0] == j, (j + 1) % n_slot, j), i, 0)

    def body(m_ref, own_ref, *refs):
        own_blk = own_ref[...].astype(F32)
        tot = None
        for j in range(n_slot):
            term = jnp.where(m_ref[0] == j, own_blk, refs[j][...].astype(F32))
            tot = term if tot is None else tot + term
        refs[n_slot][...] = tot

    in_specs = [pl.BlockSpec((br, cols), own_map)] + [pl.BlockSpec((None, br, cols), slot_map(j)) for j in range(n_slot)]
    return pl.pallas_call(
        body, name="chip_sum_" + name,
        grid_spec=pltpu.PrefetchScalarGridSpec(num_scalar_prefetch=1, grid=(nb,), in_specs=in_specs,
                                               out_specs=pl.BlockSpec((br, cols), lambda i, m: (i, 0))),
        out_shape=jax.ShapeDtypeStruct((rows, cols), F32), compiler_params=_params("arbitrary"),
    )(chip, own, *([slots] * n_slot))


def _half_exchange(pieces):
    def body(*refs):
        ins, outs = refs[0:N_BIG], refs[N_BIG:2 * N_BIG]
        send, recv = refs[2 * N_BIG:]
        x, y, c = _mesh_pos()
        sib = (x, y, 1 - c)
        copies = []
        for w, n in enumerate(BIG_NAMES):
            copies.append(_start_remote(ins[w], outs[w], send.at[w], recv.at[w], sib, _cut(n, shard=True, half=True), F32))
        for cp in copies:
            cp.wait()

    return pl.pallas_call(
        body, name="grad_half_exchange", in_specs=[ANY] * N_BIG, out_specs=[ANY] * N_BIG,
        out_shape=[jax.ShapeDtypeStruct(_cut(n, shard=True, half=True), F32) for n in BIG_NAMES],
        scratch_shapes=[pltpu.SemaphoreType.DMA((N_BIG,)), pltpu.SemaphoreType.DMA((N_BIG,))],
    )(*pieces)


WEIGHT_NAMES = ("norm1_g", "w_in", "q_norm_g", "k_norm_g", "ssm_a_re", "ssm_a_im", "ssm_log_dt", "ssm_b_re", "ssm_b_im",
                "ssm_c_re", "ssm_c_im", "ssm_d", "glu_w", "glu_b", "attn_out_norm_g", "ssm_out_norm_g", "w_out", "norm2_g",
                "w_mlp_up", "w_mlp_down")


def _train_step(a):
    x = a["x"][0]
    target = a["loss_target"][0]
    wi, wg, wo, wu, wd = _gather_weights({n: a[n][0] for n in BIG_NAMES})
    p = {n: a[n][0] for n in SMALL_NAMES}
    loss, grad_x, big, cot, disc_in = _local_step(x, target, wi, wg, wo, wu, wd, p)

    cot_list = [cot[n] for n in COT_NAMES]
    packed = _pack(cot_list)
    core = lax.axis_index("c").astype(jnp.int32).reshape(1)
    chip_id = (2 * lax.axis_index("x") + lax.axis_index("y")).astype(jnp.int32).reshape(1)
    got, got_packed = _pair_exchange(big, packed)
    chip = [_pair_sum(n, big[n], g, core) for n, g in zip(BIG_NAMES, got)]
    chip_packed = _sum_arrays([packed, got_packed], "pair_sum_small")
    slots = _chip_exchange(chip, chip_packed)
    pieces = [_chip_sum(n, o, q, chip_id) for n, o, q in zip(BIG_NAMES, chip, slots[:N_BIG])]
    small_sum = _chip_sum("small", chip_packed, slots[N_BIG], chip_id)
    shard_grads = {}
    for n, mine, theirs in zip(BIG_NAMES, pieces, _half_exchange(pieces)):
        h_ax = BIG[n][3]
        shard_grads[n] = jnp.where(core[0] == 0, jnp.concatenate([mine, theirs], axis=h_ax),
                                   jnp.concatenate([theirs, mine], axis=h_ax))
    small_grads = _small_grads(dict(zip(COT_NAMES, _unpack(small_sum, cot_list))), disc_in, p)

    grads, delta, new_m, new_v = {}, {}, {}, {}
    for n in BIG_NAMES:
        grads[n] = shard_grads[n]
        delta[n], new_m[n], new_v[n] = _adamw(a[n][0], grads[n], a["m_" + n][0], a["v_" + n][0], "adamw_" + n)
    small_w = [p[n] for n in SMALL_NAMES]
    res = _adamw(_pack(small_w), _pack([small_grads[n] for n in SMALL_NAMES]), _pack([a["m_" + n][0] for n in SMALL_NAMES]),
                 _pack([a["v_" + n][0] for n in SMALL_NAMES]), "adamw_small")
    for store, packed_out in zip((delta, new_m, new_v), res):
        store.update(zip(SMALL_NAMES, _unpack(packed_out, small_w)))
    grads.update(small_grads)

    total = lax.psum(loss, ("x", "y", "c"))
    out = [total, grad_x[None]]
    for store in (grads, delta, new_m, new_v):
        out += [store[n].reshape(a[n].shape) for n in WEIGHT_NAMES]
    return tuple(out)


def kernel(x, norm1_g, w_in, q_norm_g, k_norm_g, ssm_a_re, ssm_a_im, ssm_log_dt, ssm_b_re, ssm_b_im, ssm_c_re, ssm_c_im, ssm_d, glu_w, glu_b, attn_out_norm_g, ssm_out_norm_g, w_out, norm2_g, w_mlp_up, w_mlp_down, loss_target, m_norm1_g, m_w_in, m_q_norm_g, m_k_norm_g, m_ssm_a_re, m_ssm_a_im, m_ssm_log_dt, m_ssm_b_re, m_ssm_b_im, m_ssm_c_re, m_ssm_c_im, m_ssm_d, m_glu_w, m_glu_b, m_attn_out_norm_g, m_ssm_out_norm_g, m_w_out, m_norm2_g, m_w_mlp_up, m_w_mlp_down, v_norm1_g, v_w_in, v_q_norm_g, v_k_norm_g, v_ssm_a_re, v_ssm_a_im, v_ssm_log_dt, v_ssm_b_re, v_ssm_b_im, v_ssm_c_re, v_ssm_c_im, v_ssm_d, v_glu_w, v_glu_b, v_attn_out_norm_g, v_ssm_out_norm_g, v_w_out, v_norm2_g, v_w_mlp_up, v_w_mlp_down):
    return _train_step(dict(locals()))
```

```python
import functools
import math

import jax
import jax.numpy as jnp
from jax import lax
from jax.experimental import pallas as pl
from jax.experimental.pallas import tpu as pltpu

F32 = jnp.float32
BF16 = jnp.bfloat16
MESH = pl.DeviceIdType.MESH

D_MODEL = 1024
ATTN_W = 512
SSM_W = 512
HEAD = 64
D_FF = 4096
PROJ_W = 2048
N_GROUPS = 32
N_STATE = 64
GROUP_W = 16
EPS = 1e-6
NEG = -1e30
DILATIONS = (1, 4, 16)
BLK = 128
TILE = 2048
LANES = 128
N_LB = SSM_W // LANES
N_SLAB = 2 * N_LB * N_STATE * 8 // LANES // N_LB
VMEM_LIMIT = 56 * 1024 * 1024

ADAM_LR, ADAM_B1, ADAM_B2, ADAM_EPS, ADAM_WD, ADAM_STEP = 0.001, 0.9, 0.999, 1e-08, 0.01, 10


def _params(*sem):
    return pltpu.CompilerParams(dimension_semantics=sem, vmem_limit_bytes=VMEM_LIMIT)


def _nt(a, b):
    return lax.dot_general(a, b, (((1,), (1,)), ((), ())), preferred_element_type=F32)


def _tn(a, b):
    return lax.dot_general(a, b, (((0,), (0,)), ((), ())), preferred_element_type=F32)


def _mm(a, b):
    return jnp.dot(a, b, preferred_element_type=F32)


def _group_mean(t, ones_bd, width):
    hi = t.astype(BF16)
    lo = (t - hi.astype(F32)).astype(BF16)
    return (_mm(hi, ones_bd) + _mm(lo, ones_bd)) * (1.0 / width)


def _rms(x):
    return lax.rsqrt(jnp.mean(x * x, axis=-1, keepdims=True) + EPS)


def _rms_bwd(dy, x, r, g):
    xh = x * r
    dxh = dy * g
    dx = r * (dxh - xh * jnp.mean(dxh * xh, axis=-1, keepdims=True))
    return dx, dy * xh


def _colsum(x):
    return jnp.sum(x, axis=0, keepdims=True)


def _row_block(rows, cap):
    for b in range(min(rows, cap) // 8 * 8, 0, -8):
        if rows % b == 0:
            return b
    raise ValueError(f"no row block for {rows} rows")


def _inproj_fwd(x, g1, wi, gq, gk, ones64):
    t_len = x.shape[0]
    tm = 512
    n_hp = ATTN_W // LANES

    def body(x_ref, g1_ref, wi_ref, gq_ref, gk_ref, bd_ref, xn_ref, q_ref, k_ref, v_ref, u_ref, qr_ref, kr_ref):
        xv = x_ref[...]
        xn = (xv * _rms(xv) * g1_ref[...]).astype(BF16)
        xn_ref[...] = xn
        proj = _mm(xn, wi_ref[...])
        q = proj[:, 0:ATTN_W]
        k = proj[:, ATTN_W:2 * ATTN_W]
        v = proj[:, 2 * ATTN_W:3 * ATTN_W]
        u_ref[...] = proj[:, 3 * ATTN_W:]
        qr_ref[...] = q
        kr_ref[...] = k
        bd = bd_ref[...]
        qn = q * lax.rsqrt(_group_mean(q * q, bd, HEAD) + EPS) * gq_ref[...] * (HEAD ** -0.5)
        kn = k * lax.rsqrt(_group_mean(k * k, bd, HEAD) + EPS) * gk_ref[...]
        for hp in range(n_hp):
            sl = slice(hp * LANES, (hp + 1) * LANES)
            q_ref[hp] = qn[:, sl]
            k_ref[hp] = kn[:, sl]
            v_ref[hp] = v[:, sl]

    row = lambda i: (i, 0)
    const = lambda i: (0, 0)
    hp_spec = pl.BlockSpec((n_hp, tm, LANES), lambda i: (0, i, 0))
    hp_shape = jax.ShapeDtypeStruct((n_hp, t_len, LANES), F32)
    return pl.pallas_call(
        body, name="inproj_fwd", grid=(t_len // tm,),
        in_specs=[pl.BlockSpec((tm, D_MODEL), row), pl.BlockSpec((1, D_MODEL), const),
                  pl.BlockSpec((D_MODEL, PROJ_W), const), pl.BlockSpec((1, ATTN_W), const),
                  pl.BlockSpec((1, ATTN_W), const), pl.BlockSpec((ATTN_W, ATTN_W), const)],
        out_specs=[pl.BlockSpec((tm, D_MODEL), row), hp_spec, hp_spec, hp_spec,
                   pl.BlockSpec((tm, SSM_W), row), pl.BlockSpec((tm, ATTN_W), row), pl.BlockSpec((tm, ATTN_W), row)],
        out_shape=[jax.ShapeDtypeStruct((t_len, D_MODEL), BF16), hp_shape, hp_shape, hp_shape,
                   jax.ShapeDtypeStruct((t_len, SSM_W), F32), jax.ShapeDtypeStruct((t_len, ATTN_W), F32),
                   jax.ShapeDtypeStruct((t_len, ATTN_W), F32)],
        compiler_params=_params("arbitrary"),
    )(x, g1, wi, gq, gk, ones64)


def _attn_masks():
    head0 = lax.broadcasted_iota(jnp.int32, (BLK, LANES), 1) < HEAD
    row = lax.broadcasted_iota(jnp.int32, (2 * BLK, 2 * BLK), 0) & (BLK - 1)
    col = lax.broadcasted_iota(jnp.int32, (2 * BLK, 2 * BLK), 1)
    return head0, (col < BLK) & (col >= row), (col >= BLK) & (col - BLK <= row)


def _stack_heads(x, head0):
    return jnp.concatenate([jnp.where(head0, x, 0.0), jnp.where(head0, 0.0, x)], axis=0).astype(BF16)


def _unit_rows(uidx, d):
    nb = TILE // (BLK * d)
    r = lax.div(uidx, nb)
    b = lax.rem(uidx, nb)
    start = r + d * BLK * b
    if d == 1:
        start = pl.multiple_of(start, BLK)
        mk = lambda s: pl.ds(pl.multiple_of(s, BLK), BLK)
    else:
        mk = lambda s: pl.ds(s, BLK, stride=d)
    return b, mk(start), mk(TILE + start), mk(TILE + start - d * BLK)


def _attn_fwd(q, k, v):
    n_hp, t_len, _ = q.shape
    nt = t_len // TILE

    def body(q_ref, kp_ref, kc_ref, vp_ref, vc_ref, o_ref, lse_ref, kk, vv, m_s, l_s, acc_s):
        t = pl.program_id(1)
        kk[0:TILE] = kp_ref[0]
        kk[TILE:] = kc_ref[0]
        vv[0:TILE] = vp_ref[0]
        vv[TILE:] = vc_ref[0]
        head0, band_prev, band_cur = _attn_masks()

        for pi, d in enumerate(DILATIONS):
            def unit(uidx, carry, d=d, pi=pi):
                b, rows_q, rows_c, rows_p = _unit_rows(uidx, d)
                mask = band_cur | (band_prev & ((t > 0) | (b > 0)))
                q2 = _stack_heads(q_ref.at[0][rows_q, :], head0)
                kcat = jnp.concatenate([kk[rows_p, :], kk[rows_c, :]], axis=0).astype(BF16)
                vcat = jnp.concatenate([vv[rows_p, :], vv[rows_c, :]], axis=0).astype(BF16)
                s = jnp.where(mask, _nt(q2, kcat), NEG)
                m = jnp.max(s, axis=1, keepdims=True)
                p = jnp.exp(s - m)
                ls = jnp.sum(p, axis=1, keepdims=True)
                pv = _mm(p.astype(BF16), vcat)
                m_s.at[pi][rows_q, :] = jnp.where(head0, m[0:BLK], m[BLK:])
                l_s.at[pi][rows_q, :] = jnp.where(head0, ls[0:BLK], ls[BLK:])
                acc_s.at[pi][rows_q, :] = jnp.where(head0, pv[0:BLK], pv[BLK:])
                return carry

            lax.fori_loop(0, TILE // BLK, unit, 0, unroll=8)

        m_all = jnp.maximum(jnp.maximum(m_s[0], m_s[1]), m_s[2])
        num = jnp.zeros((TILE, LANES), F32)
        den = jnp.zeros((TILE, LANES), F32)
        for pi in range(len(DILATIONS)):
            wgt = jnp.exp(m_s[pi] - m_all)
            num = num + acc_s[pi] * wgt
            den = den + l_s[pi] * wgt
        o_ref[...] = num / den
        lse_ref[0] = m_all + jnp.log(den)

    cur = lambda hp, t: (hp, t, 0)
    prev = lambda hp, t: (hp, jnp.maximum(t - 1, 0), 0)
    blk = (1, TILE, LANES)
    per_pattern = pltpu.VMEM((len(DILATIONS), TILE, LANES), F32)
    return pl.pallas_call(
        body, name="attn_fwd", grid=(n_hp, nt),
        in_specs=[pl.BlockSpec(blk, cur), pl.BlockSpec(blk, prev), pl.BlockSpec(blk, cur),
                  pl.BlockSpec(blk, prev), pl.BlockSpec(blk, cur)],
        out_specs=[pl.BlockSpec((TILE, LANES), lambda hp, t: (t, hp)), pl.BlockSpec(blk, cur)],
        out_shape=[jax.ShapeDtypeStruct((t_len, ATTN_W), F32), jax.ShapeDtypeStruct((n_hp, t_len, LANES), F32)],
        scratch_shapes=[pltpu.VMEM((2 * TILE, LANES), F32), pltpu.VMEM((2 * TILE, LANES), F32),
                        per_pattern, per_pattern, per_pattern],
        compiler_params=_params("arbitrary", "arbitrary"),
    )(q, k, k, v, v)


def _attn_bwd(q, k, v, o, do, lse, ones_hp):
    n_hp, t_len, _ = q.shape
    nt = t_len // TILE

    n_pat = len(DILATIONS)

    def body(q_ref, kp_ref, kc_ref, vp_ref, vc_ref, o_ref, do_ref, lse_ref, bd_ref,
             dq_ref, dk_ref, dv_ref, kk, vv, dq_s, dkc, dkp, dvc, dvp, hold_k, hold_v, dl_s):
        t = pl.program_id(1)

        @pl.when(t < nt)
        def _():
            kk[0:TILE] = kp_ref[0]
            kk[TILE:] = kc_ref[0]
            vv[0:TILE] = vp_ref[0]
            vv[TILE:] = vc_ref[0]
            dl_s[...] = _group_mean(do_ref[...] * o_ref[...], bd_ref[...], 1.0)
            head0, band_prev, band_cur = _attn_masks()

            for pi, d in enumerate(DILATIONS):
                def unit(uidx, carry, d=d, pi=pi):
                    b, rows_q, rows_c, rows_p = _unit_rows(uidx, d)
                    mask = band_cur | (band_prev & ((t > 0) | (b > 0)))
                    q2 = _stack_heads(q_ref.at[0][rows_q, :], head0)
                    do2 = _stack_heads(do_ref[rows_q, :], head0)
                    lse_f = lse_ref.at[0][rows_q, :]
                    dl_f = dl_s[rows_q, :]
                    lse2 = jnp.concatenate([lse_f[:, 0:1], lse_f[:, HEAD:HEAD + 1]], axis=0)
                    dl2 = jnp.concatenate([dl_f[:, 0:1], dl_f[:, HEAD:HEAD + 1]], axis=0)
                    kcat = jnp.concatenate([kk[rows_p, :], kk[rows_c, :]], axis=0).astype(BF16)
                    vcat = jnp.concatenate([vv[rows_p, :], vv[rows_c, :]], axis=0).astype(BF16)
                    p = jnp.where(mask, jnp.exp(_nt(q2, kcat) - lse2), 0.0)
                    ds = (p * (_nt(do2, vcat) - dl2)).astype(BF16)
                    dq2 = _mm(ds, kcat)
                    dq_s.at[pi][rows_q, :] = jnp.where(head0, dq2[0:BLK], dq2[BLK:])
                    dk2 = _tn(ds, q2)
                    dv2 = _tn(p.astype(BF16), do2)
                    dkp.at[pi][rows_q, :] = dk2[0:BLK]
                    dkc.at[pi][rows_q, :] = dk2[BLK:]
                    dvp.at[pi][rows_q, :] = dv2[0:BLK]
                    dvc.at[pi][rows_q, :] = dv2[BLK:]
                    return carry

                lax.fori_loop(0, TILE // BLK, unit, 0, unroll=4)

            dq_ref[...] = dq_s[0] + dq_s[1] + dq_s[2]

        @pl.when(t > 0)
        def _():
            dk_ref[...] = hold_k[...]
            dv_ref[...] = hold_v[...]

        @pl.when((t > 0) & (t < nt))
        def _():
            for pi, d in enumerate(DILATIONS):
                back = d * BLK
                dk_ref[TILE - back:, :] = dk_ref[TILE - back:, :] + dkp[pi, 0:back, :]
                dv_ref[TILE - back:, :] = dv_ref[TILE - back:, :] + dvp[pi, 0:back, :]

        @pl.when(t < nt)
        def _():
            hold_k[...] = dkc[0] + dkc[1] + dkc[2]
            hold_v[...] = dvc[0] + dvc[1] + dvc[2]
            for pi, d in enumerate(DILATIONS):
                back = d * BLK
                if back < TILE:
                    hold_k[0:TILE - back, :] = hold_k[0:TILE - back, :] + dkp[pi, back:, :]
                    hold_v[0:TILE - back, :] = hold_v[0:TILE - back, :] + dvp[pi, back:, :]

    last = nt - 1
    cur = lambda hp, t: (hp, jnp.minimum(t, last), 0)
    prev = lambda hp, t: (hp, jnp.clip(t - 1, 0, last), 0)
    cur2 = lambda hp, t: (jnp.minimum(t, last), hp)
    prev2 = lambda hp, t: (jnp.maximum(t - 1, 0), hp)
    blk = (1, TILE, LANES)
    blk2 = (TILE, LANES)
    out = jax.ShapeDtypeStruct((t_len, ATTN_W), F32)
    return pl.pallas_call(
        body, name="attn_bwd", grid=(n_hp, nt + 1),
        in_specs=[pl.BlockSpec(blk, cur), pl.BlockSpec(blk, prev), pl.BlockSpec(blk, cur),
                  pl.BlockSpec(blk, prev), pl.BlockSpec(blk, cur), pl.BlockSpec(blk2, cur2),
                  pl.BlockSpec(blk2, cur2), pl.BlockSpec(blk, cur), pl.BlockSpec((LANES, LANES), lambda hp, t: (0, 0))],
        out_specs=[pl.BlockSpec(blk2, cur2), pl.BlockSpec(blk2, prev2), pl.BlockSpec(blk2, prev2)],
        out_shape=[out, out, out],
        scratch_shapes=[pltpu.VMEM((2 * TILE, LANES), F32), pltpu.VMEM((2 * TILE, LANES), F32)]
        + [pltpu.VMEM((n_pat, TILE, LANES), F32)] * 5 + [pltpu.VMEM((TILE, LANES), F32)] * 3,
        compiler_params=_params("arbitrary", "arbitrary"),
    )(q, k, k, v, v, o, do, lse, ones_hp)


def _discretise(lr, li, ldt, br, bi):
    dt = jnp.exp(ldt)
    mag = jnp.exp(lr * dt)
    ab_r, ab_i = mag * jnp.cos(li * dt), mag * jnp.sin(li * dt)
    den = lr * lr + li * li
    nr, ni = ab_r - 1.0, ab_i
    cr = (nr * lr + ni * li) / den
    ci = (ni * lr - nr * li) / den
    return ab_r, ab_i, cr * br - ci * bi, cr * bi + ci * br


def _disc_fwd(lr, li, ldt, br, bi):
    def body(lr_ref, li_ref, ldt_ref, br_ref, bi_ref, ar_o, ai_o, bbr_o, bbi_o):
        outs = _discretise(lr_ref[...], li_ref[...], ldt_ref[...], br_ref[...], bi_ref[...])
        for o_ref, val in zip((ar_o, ai_o, bbr_o, bbi_o), outs):
            o_ref[...] = val

    col = jax.ShapeDtypeStruct(lr.shape, F32)
    mat = jax.ShapeDtypeStruct(br.shape, F32)
    return pl.pallas_call(body, name="s5_disc_fwd", out_shape=[col, col, mat, mat])(lr, li, ldt, br, bi)


def _disc_bwd(lr, li, ldt, br, bi, d_ar, d_ai, d_bbr, d_bbi, group_sum):
    def body(lr_ref, li_ref, ldt_ref, br_ref, bi_ref, c1, c2, c3, c4, gs_ref, dlr_o, dli_o, dldt_o, dbr_o, dbi_o):
        _, vjp = jax.vjp(_discretise, lr_ref[...], li_ref[...], ldt_ref[...], br_ref[...], bi_ref[...])
        dlr, dli, dldt, dbr, dbi = vjp((c1[...], c2[...], c3[...], c4[...]))
        dlr_o[...] = dlr
        dli_o[...] = dli
        dbr_o[...] = dbr
        dbi_o[...] = dbi
        wide = jnp.broadcast_to(dldt, (dldt.shape[0], LANES))
        dldt_o[...] = jnp.dot(gs_ref[...], wide, precision=lax.Precision.HIGHEST, preferred_element_type=F32)

    col = jax.ShapeDtypeStruct(lr.shape, F32)
    mat = jax.ShapeDtypeStruct(br.shape, F32)
    return pl.pallas_call(
        body, name="s5_disc_bwd", out_shape=[col, col, jax.ShapeDtypeStruct((N_GROUPS, LANES), F32), mat, mat],
    )(lr, li, ldt, br, bi, d_ar, d_ai, d_bbr, d_bbi, group_sum)


N_CHUNK = TILE // BLK
HALF = 4


def _cmul(ar, ai, xr, xi):
    return ar * xr - ai * xi, ar * xi + ai * xr


def _power_table(a_ref, tab, sign, reverse):
    ar = [a_ref[0, j:j + 1, :] for j in range(HALF)]
    ai = [sign * a_ref[0, HALF + j:HALF + j + 1, :] for j in range(HALF)]

    def step(s, cur):
        row = pl.ds((BLK - 1 - s) if reverse else s, 1)
        nxt = []
        for j in range(HALF):
            tab.at[j][row, :] = cur[j]
            tab.at[HALF + j][row, :] = cur[HALF + j]
            nxt.append(_cmul(ar[j], ai[j], cur[j], cur[HALF + j]))
        return tuple(p[0] for p in nxt) + tuple(p[1] for p in nxt)

    lax.fori_loop(0, BLK, step, tuple(ar) + tuple(ai))


def _interleave(src, dst):
    for c in range(N_CHUNK):
        dst[pl.ds(c, BLK, stride=N_CHUNK), :] = src[c * BLK:(c + 1) * BLK, :]


def _deinterleave(src, dst):
    for c in range(N_CHUNK):
        dst[c * BLK:(c + 1) * BLK, :] = src[pl.ds(c, BLK, stride=N_CHUNK), :]


def _step_rows(s):
    return pl.ds(pl.multiple_of(s * N_CHUNK, N_CHUNK), N_CHUNK)


def _chunk_scan(buf, a_ref, sign, reverse):
    ar = [jnp.broadcast_to(a_ref[0, j:j + 1, :], (N_CHUNK, LANES)) for j in range(HALF)]
    ai = [sign * jnp.broadcast_to(a_ref[0, HALF + j:HALF + j + 1, :], (N_CHUNK, LANES)) for j in range(HALF)]

    def step(i, carry):
        s = (BLK - 1 - i) if reverse else i
        rows = _step_rows(s)
        out = []
        for j in range(HALF):
            pr, pi = _cmul(ar[j], ai[j], carry[j], carry[HALF + j])
            xr = buf.at[j][rows, :] + pr
            xi = buf.at[HALF + j][rows, :] + pi
            buf.at[j][rows, :] = xr
            buf.at[HALF + j][rows, :] = xi
            out.append((xr, xi))
        return tuple(p[0] for p in out) + tuple(p[1] for p in out)

    zero = jnp.zeros((N_CHUNK, LANES), F32)
    lax.fori_loop(0, BLK, step, (zero,) * (2 * HALF), unroll=2)


def _chunk_states(buf, carry_s, xin_s, tab, reverse):
    edge = 0 if reverse else BLK - 1
    top = 0 if reverse else BLK - 1
    pw = [tab[j, top:top + 1, :] for j in range(2 * HALF)]
    cur = [carry_s[j:j + 1, :] for j in range(2 * HALF)]
    summary = [buf[j, edge * N_CHUNK:(edge + 1) * N_CHUNK, :] for j in range(2 * HALF)]
    order = range(N_CHUNK - 1, -1, -1) if reverse else range(N_CHUNK)
    for c in order:
        for j in range(2 * HALF):
            xin_s[j, c:c + 1, :] = cur[j]
        nxt = []
        for j in range(HALF):
            pr, pi = _cmul(pw[j], pw[HALF + j], cur[j], cur[HALF + j])
            nxt.append((pr + summary[j][c:c + 1, :], pi + summary[HALF + j][c:c + 1, :]))
        cur = [p[0] for p in nxt] + [p[1] for p in nxt]
    for j in range(2 * HALF):
        carry_s[j:j + 1, :] = cur[j]


def _s5_fwd(u, a_cat, b_mat, c_mat, d_skip):
    t_len = u.shape[0]
    nt = t_len // TILE

    def body(u_ref, a_ref, b_ref, c_ref, d_ref, y_ref, st_ref, xs, us, tab, carry_s, xin_s):
        sb = pl.program_id(1)

        @pl.when(sb == 0)
        def _():
            _power_table(a_ref, tab, 1.0, False)
            carry_s[...] = jnp.zeros_like(carry_s)

        st_ref[0, 0] = carry_s[...]
        _interleave(u_ref, us)
        uv = us[...]
        bu = _mm(uv.astype(BF16), b_ref[0])
        for j in range(2 * HALF):
            xs[j] = bu[:, j * LANES:(j + 1) * LANES]
        _chunk_scan(xs, a_ref, 1.0, False)
        _chunk_states(xs, carry_s, xin_s, tab, False)
        xin = [xin_s[j] for j in range(2 * HALF)]

        def fix(s, acc):
            rows = _step_rows(s)
            for j in range(HALF):
                pr, pi = _cmul(tab.at[j][pl.ds(s, 1), :], tab.at[HALF + j][pl.ds(s, 1), :], xin[j], xin[HALF + j])
                xs.at[j][rows, :] = xs.at[j][rows, :] + pr
                xs.at[HALF + j][rows, :] = xs.at[HALF + j][rows, :] + pi
            return acc

        lax.fori_loop(0, BLK, fix, 0, unroll=2)
        xcat = jnp.concatenate([xs[j].astype(BF16) for j in range(2 * HALF)], axis=1)
        us[...] = d_ref[0] * uv + _mm(xcat, c_ref[0])
        _deinterleave(us, y_ref)

    return pl.pallas_call(
        body, name="s5_fwd", grid=(N_LB, nt),
        in_specs=[pl.BlockSpec((TILE, LANES), lambda lb, sb: (sb, lb)),
                  pl.BlockSpec((1, 2 * HALF, LANES), lambda lb, sb: (lb, 0, 0)),
                  pl.BlockSpec((1, LANES, 2 * HALF * LANES), lambda lb, sb: (lb, 0, 0)),
                  pl.BlockSpec((1, 2 * HALF * LANES, LANES), lambda lb, sb: (lb, 0, 0)),
                  pl.BlockSpec((1, 1, LANES), lambda lb, sb: (lb, 0, 0))],
        out_specs=[pl.BlockSpec((TILE, LANES), lambda lb, sb: (sb, lb)),
                   pl.BlockSpec((1, 1, 2 * HALF, LANES), lambda lb, sb: (lb, sb, 0, 0))],
        out_shape=[jax.ShapeDtypeStruct((t_len, SSM_W), F32), jax.ShapeDtypeStruct((N_LB, nt, 2 * HALF, LANES), F32)],
        scratch_shapes=[pltpu.VMEM((2 * HALF, TILE, LANES), F32), pltpu.VMEM((TILE, LANES), F32),
                        pltpu.VMEM((2 * HALF, BLK, LANES), F32),
                        pltpu.VMEM((2 * HALF, LANES), F32), pltpu.VMEM((2 * HALF, N_CHUNK, LANES), F32)],
        compiler_params=_params("arbitrary", "arbitrary"),
    )(u, a_cat, b_mat, c_mat, d_skip)


def _s5_bwd(u, dy, states, a_cat, b_mat, c_mat, d_skip):
    t_len = u.shape[0]
    nt = t_len // TILE
    last = nt - 1

    def body(u_ref, dy_ref, st_ref, a_ref, b_ref, c_ref, d_ref, du_ref, db_ref, dc_ref, da_ref, dd_ref,
             xs, gs, us, dys, tab, tabc, carry_s, lam_s, xin_s, lin_s):
        sb = pl.program_id(1)

        @pl.when(sb == 0)
        def _():
            _power_table(a_ref, tab, 1.0, False)
            _power_table(a_ref, tabc, -1.0, True)
            lam_s[...] = jnp.zeros_like(lam_s)
            db_ref[...] = jnp.zeros_like(db_ref)
            dc_ref[...] = jnp.zeros_like(dc_ref)
            da_ref[...] = jnp.zeros_like(da_ref)
            dd_ref[...] = jnp.zeros_like(dd_ref)

        _interleave(u_ref, us)
        _interleave(dy_ref, dys)
        uv = us[...]
        dyv = dys[...]
        ub = uv.astype(BF16)
        dyb = dyv.astype(BF16)
        carry_s[...] = st_ref[0, 0]
        bu = _mm(ub, b_ref[0])
        gy = _nt(dyb, c_ref[0])
        for j in range(2 * HALF):
            xs[j] = bu[:, j * LANES:(j + 1) * LANES]
            gs[j] = gy[:, j * LANES:(j + 1) * LANES]
        _chunk_scan(xs, a_ref, 1.0, False)
        _chunk_states(xs, carry_s, xin_s, tab, False)
        _chunk_scan(gs, a_ref, -1.0, True)
        _chunk_states(gs, lam_s, lin_s, tabc, True)
        zero = jnp.zeros((N_CHUNK, LANES), F32)
        for grp in range(0, HALF, 2):
            slabs = (grp, grp + 1)
            xin = [(xin_s[j], xin_s[HALF + j]) for j in slabs]
            lin = [(lin_s[j], lin_s[HALF + j]) for j in slabs]

            def fix(i, carry, slabs=slabs, xin=xin, lin=lin):
                s = BLK - 1 - i
                rows = _step_rows(s)
                out = []
                for k, j in enumerate(slabs):
                    nr, ni, acc_r, acc_i = carry[4 * k:4 * k + 4]
                    pr, pi = _cmul(tab.at[j][pl.ds(s, 1), :], tab.at[HALF + j][pl.ds(s, 1), :], xin[k][0], xin[k][1])
                    xr = xs.at[j][rows, :] + pr
                    xi = xs.at[HALF + j][rows, :] + pi
                    xs.at[j][rows, :] = xr
                    xs.at[HALF + j][rows, :] = xi
                    qr, qi = _cmul(tabc.at[j][pl.ds(s, 1), :], tabc.at[HALF + j][pl.ds(s, 1), :], lin[k][0], lin[k][1])
                    lr_ = gs.at[j][rows, :] + qr
                    li_ = gs.at[HALF + j][rows, :] + qi
                    gs.at[j][rows, :] = lr_
                    gs.at[HALF + j][rows, :] = li_
                    out += [lr_, li_, acc_r + (xr * nr + xi * ni), acc_i + (xr * ni - xi * nr)]
                return tuple(out)

            init = []
            for k in range(len(slabs)):
                init += [lin[k][0], lin[k][1], zero, zero]
            res = lax.fori_loop(0, BLK, fix, tuple(init), unroll=2)
            for k, j in enumerate(slabs):
                da_ref[0, j:j + 1, :] = da_ref[0, j:j + 1, :] + _colsum(res[4 * k + 2])
                da_ref[0, HALF + j:HALF + j + 1, :] = da_ref[0, HALF + j:HALF + j + 1, :] + _colsum(res[4 * k + 3])
        lam = jnp.concatenate([gs[j].astype(BF16) for j in range(2 * HALF)], axis=1)
        xcat = jnp.concatenate([xs[j].astype(BF16) for j in range(2 * HALF)], axis=1)
        us[...] = _nt(lam, b_ref[0]) + d_ref[0] * dyv
        _deinterleave(us, du_ref)
        db_ref[0] = db_ref[0] + _tn(ub, lam)
        dc_ref[0] = dc_ref[0] + _tn(dyb, xcat)
        dd_ref[0] = dd_ref[0] + _colsum(dyv * uv)

    rev = lambda lb, sb: (last - sb, lb)
    per_lb = lambda lb, sb: (lb, 0, 0)
    wide = 2 * HALF * LANES
    return pl.pallas_call(
        body, name="s5_bwd", grid=(N_LB, nt),
        in_specs=[pl.BlockSpec((TILE, LANES), rev), pl.BlockSpec((TILE, LANES), rev),
                  pl.BlockSpec((1, 1, 2 * HALF, LANES), lambda lb, sb: (lb, last - sb, 0, 0)),
                  pl.BlockSpec((1, 2 * HALF, LANES), per_lb), pl.BlockSpec((1, LANES, wide), per_lb),
                  pl.BlockSpec((1, wide, LANES), per_lb), pl.BlockSpec((1, 1, LANES), per_lb)],
        out_specs=[pl.BlockSpec((TILE, LANES), rev), pl.BlockSpec((1, LANES, wide), per_lb),
                   pl.BlockSpec((1, LANES, wide), per_lb), pl.BlockSpec((1, 2 * HALF, LANES), per_lb),
                   pl.BlockSpec((1, 1, LANES), per_lb)],
        out_shape=[jax.ShapeDtypeStruct((t_len, SSM_W), F32), jax.ShapeDtypeStruct((N_LB, LANES, wide), F32),
                   jax.ShapeDtypeStruct((N_LB, LANES, wide), F32), jax.ShapeDtypeStruct((N_LB, 2 * HALF, LANES), F32),
                   jax.ShapeDtypeStruct((N_LB, 1, LANES), F32)],
        scratch_shapes=[pltpu.VMEM((2 * HALF, TILE, LANES), F32), pltpu.VMEM((2 * HALF, TILE, LANES), F32),
                        pltpu.VMEM((TILE, LANES), F32), pltpu.VMEM((TILE, LANES), F32),
                        pltpu.VMEM((2 * HALF, BLK, LANES), F32), pltpu.VMEM((2 * HALF, BLK, LANES), F32),
                        pltpu.VMEM((2 * HALF, LANES), F32), pltpu.VMEM((2 * HALF, LANES), F32),
                        pltpu.VMEM((2 * HALF, N_CHUNK, LANES), F32), pltpu.VMEM((2 * HALF, N_CHUNK, LANES), F32)],
        compiler_params=_params("arbitrary", "arbitrary"),
    )(u, dy, states, a_cat, b_mat, c_mat, d_skip)


_GELU_C = math.sqrt(2.0 / math.pi)
_GELU_K = 0.044715


def _gelu(y):
    t = jnp.tanh(_GELU_C * (y + _GELU_K * (y * y * y)))
    return y * (0.5 * (1.0 + t)), t


def _gelu_grad(y, t):
    return 0.5 * (1.0 + t) + 0.5 * y * (1.0 - t * t) * (_GELU_C * (1.0 + 3.0 * _GELU_K * y * y))


def _glu(y, wg, bias):
    z, t = _gelu(y)
    sg = jax.nn.sigmoid(_mm(z.astype(BF16), wg) + bias)
    return z, t, sg


def _mix_fwd(attn, y, x, wg, glu_b, ga, gs, wo):
    t_len = x.shape[0]
    tm = 512

    def body(attn_ref, y_ref, x_ref, wg_ref, b_ref, ga_ref, gs_ref, wo_ref, x2_ref, mix_ref, z_ref):
        z, _, sg = _glu(y_ref[...], wg_ref[...], b_ref[...])
        z_ref[...] = z.astype(BF16)
        s = z * sg
        av = attn_ref[...]
        an = (av * _rms(av) * ga_ref[...]).astype(BF16)
        sn = (s * _rms(s) * gs_ref[...]).astype(BF16)
        mix_ref[:, 0:ATTN_W] = an
        mix_ref[:, ATTN_W:] = sn
        x2_ref[...] = x_ref[...] + _mm(an, wo_ref[0:ATTN_W, :]) + _mm(sn, wo_ref[ATTN_W:, :])

    row = lambda i: (i, 0)
    const = lambda i: (0, 0)
    return pl.pallas_call(
        body, name="mix_fwd", grid=(t_len // tm,),
        in_specs=[pl.BlockSpec((tm, ATTN_W), row), pl.BlockSpec((tm, SSM_W), row), pl.BlockSpec((tm, D_MODEL), row),
                  pl.BlockSpec((SSM_W, SSM_W), const), pl.BlockSpec((1, SSM_W), const), pl.BlockSpec((1, ATTN_W), const),
                  pl.BlockSpec((1, SSM_W), const), pl.BlockSpec((D_MODEL, D_MODEL), const)],
        out_specs=[pl.BlockSpec((tm, D_MODEL), row), pl.BlockSpec((tm, D_MODEL), row), pl.BlockSpec((tm, SSM_W), row)],
        out_shape=[jax.ShapeDtypeStruct((t_len, D_MODEL), F32), jax.ShapeDtypeStruct((t_len, D_MODEL), BF16),
                   jax.ShapeDtypeStruct((t_len, SSM_W), BF16)],
        compiler_params=_params("arbitrary"),
    )(attn, y, x, wg, glu_b, ga, gs, wo)


def _mlp(x2, target, g2, wu, wd):
    t_len = x2.shape[0]
    tm = 256
    fc = 1024
    n_fc = D_FF // fc

    def body(x2_ref, tg_ref, g2_ref, wu_hbm, wd_hbm, dx2_ref, hdn_ref, dup_ref, h_ref, dyb_ref, dg2_ref, loss_ref,
             wu_s, wd_s, relu_s, sem):
        @pl.when(pl.program_id(0) == 0)
        def _():
            cu = pltpu.make_async_copy(wu_hbm, wu_s, sem.at[0])
            cd = pltpu.make_async_copy(wd_hbm, wd_s, sem.at[1])
            cu.start()
            cd.start()
            cu.wait()
            cd.wait()
            dg2_ref[...] = jnp.zeros_like(dg2_ref)
            loss_ref[...] = jnp.zeros_like(loss_ref)

        x2v = x2_ref[...]
        r = _rms(x2v)
        g2v = g2_ref[...]
        h = (x2v * r * g2v).astype(BF16)
        h_ref[...] = h
        yout = x2v
        for c in range(n_fc):
            cols = slice(c * fc, (c + 1) * fc)
            ru = jnp.maximum(_mm(h, wu_s[:, cols]), 0.0)
            relu_s[:, cols] = ru
            hd = (ru * ru).astype(BF16)
            hdn_ref[:, cols] = hd
            yout = yout + _mm(hd, wd_s[cols, :])
        err = yout - tg_ref[...]
        loss_ref[...] = loss_ref[...] + 0.5 * jnp.sum(err * err) * (1.0 / D_MODEL)
        dy = err * (1.0 / D_MODEL)
        dyb = dy.astype(BF16)
        dyb_ref[...] = dyb
        dh = jnp.zeros((tm, D_MODEL), F32)
        for c in range(n_fc):
            cols = slice(c * fc, (c + 1) * fc)
            dup = (_nt(dyb, wd_s[cols, :]) * (2.0 * relu_s[:, cols])).astype(BF16)
            dup_ref[:, cols] = dup
            dh = dh + _nt(dup, wu_s[:, cols])
        dxn, gterm = _rms_bwd(dh, x2v, r, g2v)
        dx2_ref[...] = dy + dxn
        dg2_ref[...] = dg2_ref[...] + _colsum(gterm)

    row = lambda i: (i, 0)
    const = lambda i: (0, 0)
    any_spec = pl.BlockSpec(memory_space=pl.ANY)
    return pl.pallas_call(
        body, name="mlp", grid=(t_len // tm,),
        in_specs=[pl.BlockSpec((tm, D_MODEL), row), pl.BlockSpec((tm, D_MODEL), row), pl.BlockSpec((1, D_MODEL), const),
                  any_spec, any_spec],
        out_specs=[pl.BlockSpec((tm, D_MODEL), row), pl.BlockSpec((tm, D_FF), row), pl.BlockSpec((tm, D_FF), row),
                   pl.BlockSpec((tm, D_MODEL), row), pl.BlockSpec((tm, D_MODEL), row), pl.BlockSpec((1, D_MODEL), const),
                   pl.BlockSpec((1, LANES), const)],
        out_shape=[jax.ShapeDtypeStruct((t_len, D_MODEL), F32), jax.ShapeDtypeStruct((t_len, D_FF), BF16),
                   jax.ShapeDtypeStruct((t_len, D_FF), BF16), jax.ShapeDtypeStruct((t_len, D_MODEL), BF16),
                   jax.ShapeDtypeStruct((t_len, D_MODEL), BF16), jax.ShapeDtypeStruct((1, D_MODEL), F32),
                   jax.ShapeDtypeStruct((1, LANES), F32)],
        scratch_shapes=[pltpu.VMEM((D_MODEL, D_FF), BF16), pltpu.VMEM((D_FF, D_MODEL), BF16),
                        pltpu.VMEM((tm, D_FF), F32), pltpu.SemaphoreType.DMA((2,))],
        compiler_params=_params("arbitrary"),
    )(x2, target, g2, wu, wd)


def _mix_bwd(dx2, attn, y, wg, glu_b, ga, gs, wo):
    t_len = dx2.shape[0]
    tm = 512

    def body(dx2_ref, attn_ref, y_ref, wg_ref, b_ref, ga_ref, gs_ref, wo_ref,
             dattn_ref, dy_ref, dx2b_ref, dgp_ref, dga_ref, dgs_ref, db_ref):
        @pl.when(pl.program_id(0) == 0)
        def _():
            dga_ref[...] = jnp.zeros_like(dga_ref)
            dgs_ref[...] = jnp.zeros_like(dgs_ref)
            db_ref[...] = jnp.zeros_like(db_ref)

        dx2b = dx2_ref[...].astype(BF16)
        dx2b_ref[...] = dx2b
        d_an = _nt(dx2b, wo_ref[0:ATTN_W, :])
        d_sn = _nt(dx2b, wo_ref[ATTN_W:, :])
        yv = y_ref[...]
        wg = wg_ref[...]
        z, t, sg = _glu(yv, wg, b_ref[...])
        s = z * sg
        av = attn_ref[...]
        d_attn, ga_term = _rms_bwd(d_an, av, _rms(av), ga_ref[...])
        d_s, gs_term = _rms_bwd(d_sn, s, _rms(s), gs_ref[...])
        dattn_ref[...] = d_attn
        dgp = d_s * z * sg * (1.0 - sg)
        dgpb = dgp.astype(BF16)
        dgp_ref[...] = dgpb
        dz = d_s * sg + _nt(dgpb, wg)
        dy_ref[...] = dz * _gelu_grad(yv, t)
        dga_ref[...] = dga_ref[...] + _colsum(ga_term)
        dgs_ref[...] = dgs_ref[...] + _colsum(gs_term)
        db_ref[...] = db_ref[...] + _colsum(dgp)

    row = lambda i: (i, 0)
    const = lambda i: (0, 0)
    vec = jax.ShapeDtypeStruct((1, SSM_W), F32)
    return pl.pallas_call(
        body, name="mix_bwd", grid=(t_len // tm,),
        in_specs=[pl.BlockSpec((tm, D_MODEL), row), pl.BlockSpec((tm, ATTN_W), row), pl.BlockSpec((tm, SSM_W), row),
                  pl.BlockSpec((SSM_W, SSM_W), const), pl.BlockSpec((1, SSM_W), const), pl.BlockSpec((1, ATTN_W), const),
                  pl.BlockSpec((1, SSM_W), const), pl.BlockSpec((D_MODEL, D_MODEL), const)],
        out_specs=[pl.BlockSpec((tm, ATTN_W), row), pl.BlockSpec((tm, SSM_W), row), pl.BlockSpec((tm, D_MODEL), row),
                   pl.BlockSpec((tm, SSM_W), row), pl.BlockSpec((1, ATTN_W), const), pl.BlockSpec((1, SSM_W), const),
                   pl.BlockSpec((1, SSM_W), const)],
        out_shape=[jax.ShapeDtypeStruct((t_len, ATTN_W), F32), jax.ShapeDtypeStruct((t_len, SSM_W), F32),
                   jax.ShapeDtypeStruct((t_len, D_MODEL), BF16), jax.ShapeDtypeStruct((t_len, SSM_W), BF16), vec, vec, vec],
        compiler_params=_params("arbitrary"),
    )(dx2, attn, y, wg, glu_b, ga, gs, wo)


def _inproj_bwd(dqs, dkn, dv, du, q_raw, k_raw, x, dx2, wi, g1, gq, gk, ones64):
    t_len = x.shape[0]
    tm = 512
    n_heads = ATTN_W // HEAD

    def body(dqs_ref, dkn_ref, dv_ref, du_ref, q_ref, k_ref, x_ref, dx2_ref, wi_ref, g1_ref, gq_ref, gk_ref, bd_ref,
             gx_ref, dproj_ref, dg1_ref, dgq_ref, dgk_ref, accq, acck):
        i = pl.program_id(0)

        @pl.when(i == 0)
        def _():
            dg1_ref[...] = jnp.zeros_like(dg1_ref)
            accq[...] = jnp.zeros_like(accq)
            acck[...] = jnp.zeros_like(acck)

        bd = bd_ref[...]

        def head_norm_bwd(dy, raw, gain, acc):
            r = lax.rsqrt(_group_mean(raw * raw, bd, HEAD) + EPS)
            xh = raw * r
            dxh = dy * gain
            acc[...] = acc[...] + _colsum(dy * xh)
            return r * (dxh - xh * _group_mean(dxh * xh, bd, HEAD))

        dq = head_norm_bwd(dqs_ref[...] * (HEAD ** -0.5), q_ref[...], gq_ref[...], accq)
        dk = head_norm_bwd(dkn_ref[...], k_ref[...], gk_ref[...], acck)
        dproj_ref[:, 0:ATTN_W] = dq.astype(BF16)
        dproj_ref[:, ATTN_W:2 * ATTN_W] = dk.astype(BF16)
        dproj_ref[:, 2 * ATTN_W:3 * ATTN_W] = dv_ref[...].astype(BF16)
        dproj_ref[:, 3 * ATTN_W:] = du_ref[...].astype(BF16)
        dxn = _nt(dproj_ref[...], wi_ref[...])
        xv = x_ref[...]
        g1v = g1_ref[...]
        dx, g1_term = _rms_bwd(dxn, xv, _rms(xv), g1v)
        gx_ref[...] = dx2_ref[...] + dx
        dg1_ref[...] = dg1_ref[...] + _colsum(g1_term)

        @pl.when(i == pl.num_programs(0) - 1)
        def _():
            for acc, out in ((accq, dgq_ref), (acck, dgk_ref)):
                tot = acc[:, 0:HEAD]
                for h in range(1, n_heads):
                    tot = tot + acc[:, h * HEAD:(h + 1) * HEAD]
                out[...] = tot

    row = lambda i: (i, 0)
    const = lambda i: (0, 0)
    aw = pl.BlockSpec((tm, ATTN_W), row)
    dm = pl.BlockSpec((tm, D_MODEL), row)
    return pl.pallas_call(
        body, name="inproj_bwd", grid=(t_len // tm,),
        in_specs=[aw, aw, aw, aw, aw, aw, dm, dm, pl.BlockSpec((D_MODEL, PROJ_W), const), pl.BlockSpec((1, D_MODEL), const),
                  pl.BlockSpec((1, ATTN_W), const), pl.BlockSpec((1, ATTN_W), const), pl.BlockSpec((ATTN_W, ATTN_W), const)],
        out_specs=[dm, pl.BlockSpec((tm, PROJ_W), row), pl.BlockSpec((1, D_MODEL), const),
                   pl.BlockSpec((1, HEAD), const), pl.BlockSpec((1, HEAD), const)],
        out_shape=[jax.ShapeDtypeStruct((t_len, D_MODEL), F32), jax.ShapeDtypeStruct((t_len, PROJ_W), BF16),
                   jax.ShapeDtypeStruct((1, D_MODEL), F32), jax.ShapeDtypeStruct((1, HEAD), F32),
                   jax.ShapeDtypeStruct((1, HEAD), F32)],
        scratch_shapes=[pltpu.VMEM((1, ATTN_W), F32), pltpu.VMEM((1, ATTN_W), F32)],
        compiler_params=_params("arbitrary"),
    )(dqs, dkn, dv, du, q_raw, k_raw, x, dx2, wi, g1, gq, gk, ones64)


def _grad_matmul(a, b, name):
    t_len, m = a.shape
    n = b.shape[1]
    bm, bn, bt = min(m, 1024), min(n, 1024), 1024

    def body(a_ref, b_ref, o_ref):
        @pl.when(pl.program_id(2) == 0)
        def _():
            o_ref[...] = jnp.zeros_like(o_ref)

        o_ref[...] = o_ref[...] + _tn(a_ref[...], b_ref[...])

    return pl.pallas_call(
        body, name=name, grid=(m // bm, n // bn, t_len // bt),
        in_specs=[pl.BlockSpec((bt, bm), lambda i, j, k: (k, i)), pl.BlockSpec((bt, bn), lambda i, j, k: (k, j))],
        out_specs=pl.BlockSpec((bm, bn), lambda i, j, k: (i, j)),
        out_shape=jax.ShapeDtypeStruct((m, n), F32),
        compiler_params=_params("arbitrary", "arbitrary", "arbitrary"),
    )(a, b)


def _adamw(w, g, m, v, name):
    rows, cols = w.shape
    br = _row_block(rows, 256)

    def body(w_ref, g_ref, m_ref, v_ref, d_o, m_o, v_o):
        gv = g_ref[...]
        mn = ADAM_B1 * m_ref[...] + (1.0 - ADAM_B1) * gv
        vn = ADAM_B2 * v_ref[...] + (1.0 - ADAM_B2) * jnp.square(gv)
        m_hat = mn / (1.0 - ADAM_B1 ** ADAM_STEP)
        v_hat = vn / (1.0 - ADAM_B2 ** ADAM_STEP)
        d_o[...] = -ADAM_LR * (m_hat / (jnp.sqrt(v_hat) + ADAM_EPS) + ADAM_WD * w_ref[...])
        m_o[...] = mn
        v_o[...] = vn

    spec = pl.BlockSpec((br, cols), lambda i: (i, 0))
    shape = jax.ShapeDtypeStruct((rows, cols), F32)
    return pl.pallas_call(
        body, name=name, grid=(rows // br,), in_specs=[spec] * 4, out_specs=[spec] * 3, out_shape=[shape] * 3,
        compiler_params=_params("arbitrary"),
    )(w, g, m, v)


def _sum_arrays(arrs, name, out_dtype=F32):
    rows, cols = arrs[0].shape
    br = _row_block(rows, 512)
    n = len(arrs)

    def body(*refs):
        tot = refs[0][...]
        for r in refs[1:n]:
            tot = tot + r[...]
        refs[n][...] = tot.astype(out_dtype)

    spec = pl.BlockSpec((br, cols), lambda i: (i, 0))
    return pl.pallas_call(
        body, name=name, grid=(rows // br,), in_specs=[spec] * n, out_specs=spec,
        out_shape=jax.ShapeDtypeStruct((rows, cols), out_dtype), compiler_params=_params("arbitrary"),
    )(*arrs)


GPL = N_GROUPS // N_LB
SW = GPL * N_STATE


def _eye_groups():
    return jnp.eye(GPL, dtype=F32)


def _s5_matrices(ab_r, ab_i, bb_r, bb_i, c_re, c_im, d_skip):
    eye = _eye_groups()
    a_cat = jnp.concatenate([ab_r.reshape(N_LB, HALF, LANES), ab_i.reshape(N_LB, HALF, LANES)], axis=1)

    def b_part(bb):
        b4 = jnp.transpose(bb.reshape(N_LB, GPL, N_STATE, GROUP_W), (0, 1, 3, 2))
        return (b4[:, :, :, None, :] * eye[None, :, None, :, None]).reshape(N_LB, LANES, SW)

    def c_part(cc):
        c4 = jnp.transpose(cc.reshape(N_LB, GPL, GROUP_W, N_STATE), (0, 1, 3, 2))
        return (c4[:, :, :, None, :] * eye[None, :, None, :, None]).reshape(N_LB, SW, LANES)

    b_mat = jnp.concatenate([b_part(bb_r), b_part(bb_i)], axis=2).astype(BF16)
    c_mat = jnp.concatenate([c_part(c_re), -c_part(c_im)], axis=1).astype(BF16)
    return a_cat, b_mat, c_mat, d_skip.reshape(N_LB, 1, LANES)


def _s5_unpack_grads(db, dc, da, dd):
    eye = _eye_groups()
    mask = eye[None, :, None, None, :, None]
    d6 = jnp.sum(db.reshape(N_LB, GPL, GROUP_W, 2, GPL, N_STATE) * mask, axis=4)
    dbb = jnp.transpose(d6, (3, 0, 1, 4, 2)).reshape(2, N_GROUPS * N_STATE, GROUP_W)
    c6 = jnp.sum(dc.reshape(N_LB, GPL, GROUP_W, 2, GPL, N_STATE) * mask, axis=4)
    dcc = jnp.transpose(c6, (3, 0, 1, 2, 4)).reshape(2, N_GROUPS, GROUP_W, N_STATE)
    dab_r = da[:, :HALF].reshape(N_GROUPS * N_STATE, 1)
    dab_i = da[:, HALF:].reshape(N_GROUPS * N_STATE, 1)
    return dab_r, dab_i, dbb[0], dbb[1], dcc[0], -dcc[1], dd.reshape(N_GROUPS, GROUP_W)


def _block_ones(n, width):
    i = lax.broadcasted_iota(jnp.int32, (n, n), 0) // width
    j = lax.broadcasted_iota(jnp.int32, (n, n), 1) // width
    return (i == j).astype(BF16)


def _tile_heads(g):
    return jnp.tile(g.reshape(1, HEAD), (1, ATTN_W // HEAD))


def _local_step(x, target, wi, wg, wo, wu, wd, p):
    ones64 = _block_ones(ATTN_W, HEAD)
    ones_hp = _block_ones(LANES, HEAD)
    g1 = p["norm1_g"].reshape(1, D_MODEL)
    g2 = p["norm2_g"].reshape(1, D_MODEL)
    gq = _tile_heads(p["q_norm_g"])
    gk = _tile_heads(p["k_norm_g"])
    ga = p["attn_out_norm_g"].reshape(1, ATTN_W)
    gs = p["ssm_out_norm_g"].reshape(1, SSM_W)
    glu_b = p["glu_b"].reshape(1, SSM_W)
    n_gp = N_GROUPS * N_STATE
    lr = p["ssm_a_re"].reshape(n_gp, 1)
    li = p["ssm_a_im"].reshape(n_gp, 1)
    ldt = jnp.repeat(p["ssm_log_dt"].reshape(N_GROUPS), N_STATE).reshape(n_gp, 1)
    br = p["ssm_b_re"].reshape(n_gp, GROUP_W)
    bi = p["ssm_b_im"].reshape(n_gp, GROUP_W)
    ab_r, ab_i, bb_r, bb_i = _disc_fwd(lr, li, ldt, br, bi)
    a_cat, b_mat, c_mat, d_mat = _s5_matrices(
        ab_r, ab_i, bb_r, bb_i, p["ssm_c_re"].reshape(N_GROUPS, GROUP_W, N_STATE),
        p["ssm_c_im"].reshape(N_GROUPS, GROUP_W, N_STATE), p["ssm_d"])

    xn, qn, kn, vv, u, q_raw, k_raw = _inproj_fwd(x, g1, wi, gq, gk, ones64)
    attn, lse = _attn_fwd(qn, kn, vv)
    y, states = _s5_fwd(u, a_cat, b_mat, c_mat, d_mat)
    x2, mix, z = _mix_fwd(attn, y, x, wg, glu_b, ga, gs, wo)
    dx2, hdn, dup, h, dyb, dg2, loss = _mlp(x2, target, g2, wu, wd)
    d_attn, dy_ssm, dx2b, dgp, dga, dgs, dglu_b = _mix_bwd(dx2, attn, y, wg, glu_b, ga, gs, wo)
    dqs, dkn, dvv = _attn_bwd(qn, kn, vv, attn, d_attn, lse, ones_hp)
    du, db, dc, da, dd = _s5_bwd(u, dy_ssm, states, a_cat, b_mat, c_mat, d_mat)
    grad_x, dproj, dg1, dgq, dgk = _inproj_bwd(dqs, dkn, dvv, du, q_raw, k_raw, x, dx2, wi, g1, gq, gk, ones64)

    big = {
        "w_in": _grad_matmul(xn, dproj, "grad_w_in"),
        "glu_w": _grad_matmul(z, dgp, "grad_glu_w"),
        "w_out": _grad_matmul(mix, dx2b, "grad_w_out"),
        "w_mlp_up": _grad_matmul(h, dup, "grad_w_mlp_up"),
        "w_mlp_down": _grad_matmul(hdn, dyb, "grad_w_mlp_down"),
    }
    dab_r, dab_i, dbb_r, dbb_i, dc_re, dc_im, dd_g = _s5_unpack_grads(db, dc, da, dd)
    cot = {"norm1_g": dg1, "q_norm_g": dgq, "k_norm_g": dgk, "ab_r": dab_r, "ab_i": dab_i, "bb_r": dbb_r, "bb_i": dbb_i,
           "ssm_c_re": dc_re, "ssm_c_im": dc_im, "ssm_d": dd_g, "glu_b": dglu_b, "attn_out_norm_g": dga,
           "ssm_out_norm_g": dgs, "norm2_g": dg2}
    return loss[0, 0], grad_x, big, cot, (lr, li, ldt, br, bi)


COT_NAMES = ("norm1_g", "q_norm_g", "k_norm_g", "ab_r", "ab_i", "bb_r", "bb_i", "ssm_c_re", "ssm_c_im", "ssm_d",
             "glu_b", "attn_out_norm_g", "ssm_out_norm_g", "norm2_g")
SMALL_NAMES = ("norm1_g", "q_norm_g", "k_norm_g", "ssm_a_re", "ssm_a_im", "ssm_log_dt", "ssm_b_re", "ssm_b_im",
               "ssm_c_re", "ssm_c_im", "ssm_d", "glu_b", "attn_out_norm_g", "ssm_out_norm_g", "norm2_g")
BIG_NAMES = ("w_in", "glu_w", "w_out", "w_mlp_up", "w_mlp_down")
PACK_ROWS = 1152


def _pack(arrs):
    flat = jnp.concatenate([a.reshape(-1) for a in arrs])
    return jnp.pad(flat, (0, PACK_ROWS * LANES - flat.shape[0])).reshape(PACK_ROWS, LANES)


def _unpack(packed, like):
    flat = packed.reshape(-1)
    out, pos = [], 0
    for a in like:
        out.append(flat[pos:pos + a.size].reshape(a.shape))
        pos += a.size
    return out


def _small_grads(cot, disc_in, p):
    lr, li, ldt, br, bi = disc_in
    group_sum = (lax.broadcasted_iota(jnp.int32, (N_GROUPS, N_GROUPS * N_STATE), 1) // N_STATE
                 == lax.broadcasted_iota(jnp.int32, (N_GROUPS, N_GROUPS * N_STATE), 0)).astype(F32)
    dlr, dli, dldt, dbr, dbi = _disc_bwd(lr, li, ldt, br, bi, cot["ab_r"], cot["ab_i"], cot["bb_r"], cot["bb_i"], group_sum)
    g = dict(cot)
    g.update(ssm_a_re=dlr, ssm_a_im=dli, ssm_log_dt=dldt[:, 0], ssm_b_re=dbr, ssm_b_im=dbi)
    return {n: g[n].reshape(p[n].shape) for n in SMALL_NAMES}


BIG = {
    "w_in": ((D_MODEL, PROJ_W), 1, PROJ_W // 4, 0, D_MODEL // 2),
    "glu_w": ((SSM_W, SSM_W), 0, SSM_W // 4, 1, SSM_W // 2),
    "w_out": ((D_MODEL, D_MODEL), 0, D_MODEL // 4, 1, D_MODEL // 2),
    "w_mlp_up": ((D_MODEL, D_FF), 1, D_FF // 4, 0, D_MODEL // 2),
    "w_mlp_down": ((D_FF, D_MODEL), 0, D_FF // 4, 1, D_MODEL // 2),
}
N_BIG = len(BIG_NAMES)
N_CHIPS = 4
ANY = pl.BlockSpec(memory_space=pl.ANY)


def _cut(name, shard=False, half=False):
    shape, s_ax, s_sz, h_ax, h_sz = BIG[name]
    shape = list(shape)
    if shard:
        shape[s_ax] = s_sz
    if half:
        shape[h_ax] = h_sz
    return tuple(shape)


def _window(name, base, shard=None, half=None):
    _, s_ax, s_sz, h_ax, h_sz = BIG[name]
    idx = [pl.ds(0, base[0]), pl.ds(0, base[1])]
    if shard is not None:
        idx[s_ax] = pl.ds(pl.multiple_of(shard * s_sz, s_sz), s_sz)
    if half is not None:
        idx[h_ax] = pl.ds(pl.multiple_of(half * h_sz, h_sz), h_sz)
    return tuple(idx)


def _mesh_pos():
    return lax.axis_index("x"), lax.axis_index("y"), lax.axis_index("c")


def _other_chips(x, y):
    return [(1 - x, y, 2 * (1 - x) + y), (x, 1 - y, 2 * x + 1 - y), (1 - x, 1 - y, 2 * (1 - x) + 1 - y)]


def _remote(src, dst, send_sem, recv_sem, dev):
    return pltpu.make_async_remote_copy(src_ref=src, dst_ref=dst, send_sem=send_sem, recv_sem=recv_sem,
                                        device_id=dev, device_id_type=MESH)


CHUNK_BYTES = 256 * 1024


def _row_chunks(shape, dtype):
    rows, cols = shape
    n = 1
    while rows % (2 * n) == 0 and (rows // (2 * n)) % 16 == 0 and rows * cols * jnp.dtype(dtype).itemsize // n > CHUNK_BYTES:
        n *= 2
    return [(pl.ds(i * (rows // n), rows // n), slice(None)) for i in range(n)]


def _start_remote(src, dst, send_sem, recv_sem, dev, shape, dtype):
    for sl in _row_chunks(shape, dtype):
        _remote(src.at[sl], dst.at[sl], send_sem, recv_sem, dev).start()
    return _remote(src, dst, send_sem, recv_sem, dev)


def _start_local(src, dst, sem, shape, dtype):
    for sl in _row_chunks(shape, dtype):
        pltpu.make_async_copy(src.at[sl], dst.at[sl], sem).start()
    return pltpu.make_async_copy(src, dst, sem)


def _gather_weights(shards):
    def body(*refs):
        ins, outs, stage = refs[0:N_BIG], refs[N_BIG:2 * N_BIG], refs[2 * N_BIG:3 * N_BIG]
        send, recv, fsend, frecv, lsem = refs[3 * N_BIG:]
        x, y, c = _mesh_pos()
        me = 2 * x + y
        sib = (x, y, 1 - c)
        chips = _other_chips(x, y)
        pending = []
        for w, n in enumerate(BIG_NAMES):
            stage[w][...] = ins[w][...].astype(BF16)
            full = BIG[n][0]
            pending.append(_start_local(stage[w], outs[w].at[_window(n, full, shard=me)], lsem.at[w],
                                        _cut(n, shard=True), BF16))
        sends = []
        for k, (px, py, _) in enumerate(chips):
            for w, n in enumerate(BIG_NAMES):
                s = k * N_BIG + w
                sends.append(_start_remote(stage[w].at[_window(n, _cut(n, shard=True), half=c)],
                                           outs[w].at[_window(n, BIG[n][0], shard=me, half=c)], send.at[s], recv.at[s],
                                           (px, py, c), _cut(n, shard=True, half=True), BF16))
        for k, (px, py, pj) in enumerate(chips):
            for w, n in enumerate(BIG_NAMES):
                s = k * N_BIG + w
                got = outs[w].at[_window(n, BIG[n][0], shard=pj, half=c)]
                _remote(got, got, send.at[s], recv.at[s], (px, py, c)).wait_recv()
                sends.append(_start_remote(got, got, fsend.at[s], frecv.at[s], sib, _cut(n, shard=True, half=True), BF16))
        for k, (px, py, pj) in enumerate(chips):
            for w, n in enumerate(BIG_NAMES):
                s = k * N_BIG + w
                got = outs[w].at[_window(n, BIG[n][0], shard=pj, half=1 - c)]
                _remote(got, got, fsend.at[s], frecv.at[s], sib).wait_recv()
        for cp in sends:
            cp.wait_send()
        for cp in pending:
            cp.wait()

    n_sem = (N_CHIPS - 1) * N_BIG
    outs = pl.pallas_call(
        body, name="gather_weights",
        in_specs=[pl.BlockSpec(memory_space=pltpu.VMEM)] * N_BIG, out_specs=[ANY] * N_BIG,
        out_shape=[jax.ShapeDtypeStruct(BIG[n][0], BF16) for n in BIG_NAMES],
        scratch_shapes=[pltpu.VMEM(_cut(n, shard=True), BF16) for n in BIG_NAMES]
        + [pltpu.SemaphoreType.DMA((n_sem,))] * 4 + [pltpu.SemaphoreType.DMA((N_BIG,))],
        compiler_params=pltpu.CompilerParams(vmem_limit_bytes=VMEM_LIMIT),
    )(*[shards[n] for n in BIG_NAMES])
    return outs


def _pair_exchange(big, packed):
    def body(*refs):
        ins, small = refs[0:N_BIG], refs[N_BIG]
        got = refs[N_BIG + 1:2 * N_BIG + 2]
        send, recv = refs[2 * N_BIG + 2:]
        x, y, c = _mesh_pos()
        sib = (x, y, 1 - c)
        copies = []
        for w, n in enumerate(BIG_NAMES):
            copies.append(_start_remote(ins[w].at[_window(n, BIG[n][0], half=1 - c)], got[w], send.at[w], recv.at[w], sib,
                                        _cut(n, half=True), F32))
        copies.append(_start_remote(small, got[N_BIG], send.at[N_BIG], recv.at[N_BIG], sib, small.shape, F32))
        for cp in copies:
            cp.wait()

    halves = [jax.ShapeDtypeStruct(_cut(n, half=True), F32) for n in BIG_NAMES]
    outs = pl.pallas_call(
        body, name="grad_pair_exchange", in_specs=[ANY] * (N_BIG + 1), out_specs=[ANY] * (N_BIG + 1),
        out_shape=halves + [jax.ShapeDtypeStruct(packed.shape, F32)],
        scratch_shapes=[pltpu.SemaphoreType.DMA((N_BIG + 1,)), pltpu.SemaphoreType.DMA((N_BIG + 1,))],
    )(*[big[n] for n in BIG_NAMES], packed)
    return outs[0:N_BIG], outs[N_BIG]


def _pair_sum(name, full, got, core):
    _, _, _, h_ax, _ = BIG[name]
    rows, cols = _cut(name, half=True)
    br = _row_block(rows, 512)
    nb = rows // br
    own_map = (lambda i, c: (i + c[0] * nb, 0)) if h_ax == 0 else (lambda i, c: (i, c[0]))

    def body(c_ref, own_ref, got_ref, o_ref):
        o_ref[...] = (own_ref[...] + got_ref[...]).astype(BF16)

    plain = pl.BlockSpec((br, cols), lambda i, c: (i, 0))
    return pl.pallas_call(
        body, name="pair_sum_" + name,
        grid_spec=pltpu.PrefetchScalarGridSpec(num_scalar_prefetch=1, grid=(nb,),
                                               in_specs=[pl.BlockSpec((br, cols), own_map), plain], out_specs=plain),
        out_shape=jax.ShapeDtypeStruct((rows, cols), BF16), compiler_params=_params("arbitrary"),
    )(core, full, got)


def _chip_exchange(halves, packed):
    n_all = N_BIG + 1

    def body(*refs):
        ins, outs = refs[0:n_all], refs[n_all:2 * n_all]
        send, recv = refs[2 * n_all:]
        x, y, c = _mesh_pos()
        me = 2 * x + y
        chips = _other_chips(x, y)

        def piece(w, shard):
            if w == N_BIG:
                return ins[w]
            n = BIG_NAMES[w]
            return ins[w].at[_window(n, _cut(n, half=True), shard=shard)]

        def kind(w):
            if w == N_BIG:
                return packed.shape, F32
            return _cut(BIG_NAMES[w], shard=True, half=True), BF16

        copies = []
        for k, (px, py, pj) in enumerate(chips):
            for w in range(n_all):
                s = k * n_all + w
                copies.append(_start_remote(piece(w, pj), outs[w].at[me], send.at[s], recv.at[s], (px, py, c), *kind(w)))
        for k, (px, py, pj) in enumerate(chips):
            for w in range(n_all):
                s = k * n_all + w
                _remote(piece(w, me), outs[w].at[pj], send.at[s], recv.at[s], (px, py, c)).wait_recv()
        for cp in copies:
            cp.wait_send()

    shapes = [jax.ShapeDtypeStruct((N_CHIPS,) + _cut(n, shard=True, half=True), BF16) for n in BIG_NAMES]
    shapes.append(jax.ShapeDtypeStruct((N_CHIPS,) + packed.shape, F32))
    n_sem = (N_CHIPS - 1) * n_all
    return pl.pallas_call(
        body, name="grad_chip_exchange", in_specs=[ANY] * n_all, out_specs=[ANY] * n_all, out_shape=shapes,
        scratch_shapes=[pltpu.SemaphoreType.DMA((n_sem,)), pltpu.SemaphoreType.DMA((n_sem,))],
    )(*halves, packed)


def _chip_sum(name, own, slots, chip):
    n_slot, rows, cols = slots.shape
    br = _row_block(rows, 512)
    nb = rows // br
    if name in BIG and BIG[name][1] == 1:
        own_map = lambda i, m: (i, m[0])
    elif name in BIG:
        own_map = lambda i, m: (i + m[0] * nb, 0)
    else:
        own_map = lambda i, m: (i, 0)

    def slot_map(j):
        return lambda i, m: (jnp.where(m[0] == j, (j + 1) % n_slot, j), i, 0)

    def body(m_ref, own_ref, *refs):
        own_blk = own_ref[...].astype(F32)
        tot = None
        for j in range(n_slot):
            term = jnp.where(m_ref[0] == j, own_blk, refs[j][...].astype(F32))
            tot = term if tot is None else tot + term
        refs[n_slot][...] = tot

    in_specs = [pl.BlockSpec((br, cols), own_map)] + [pl.BlockSpec((None, br, cols), slot_map(j)) for j in range(n_slot)]
    return pl.pallas_call(
        body, name="chip_sum_" + name,
        grid_spec=pltpu.PrefetchScalarGridSpec(num_scalar_prefetch=1, grid=(nb,), in_specs=in_specs,
                                               out_specs=pl.BlockSpec((br, cols), lambda i, m: (i, 0))),
        out_shape=jax.ShapeDtypeStruct((rows, cols), F32), compiler_params=_params("arbitrary"),
    )(chip, own, *([slots] * n_slot))


def _half_exchange(pieces):
    def body(*refs):
        ins, outs = refs[0:N_BIG], refs[N_BIG:2 * N_BIG]
        send, recv = refs[2 * N_BIG:]
        x, y, c = _mesh_pos()
        sib = (x, y, 1 - c)
        copies = []
        for w, n in enumerate(BIG_NAMES):
            copies.append(_start_remote(ins[w], outs[w], send.at[w], recv.at[w], sib, _cut(n, shard=True, half=True), F32))
        for cp in copies:
            cp.wait()

    return pl.pallas_call(
        body, name="grad_half_exchange", in_specs=[ANY] * N_BIG, out_specs=[ANY] * N_BIG,
        out_shape=[jax.ShapeDtypeStruct(_cut(n, shard=True, half=True), F32) for n in BIG_NAMES],
        scratch_shapes=[pltpu.SemaphoreType.DMA((N_BIG,)), pltpu.SemaphoreType.DMA((N_BIG,))],
    )(*pieces)


WEIGHT_NAMES = ("norm1_g", "w_in", "q_norm_g", "k_norm_g", "ssm_a_re", "ssm_a_im", "ssm_log_dt", "ssm_b_re", "ssm_b_im",
                "ssm_c_re", "ssm_c_im", "ssm_d", "glu_w", "glu_b", "attn_out_norm_g", "ssm_out_norm_g", "w_out", "norm2_g",
                "w_mlp_up", "w_mlp_down")


def _train_step(a):
    x = a["x"][0]
    target = a["loss_target"][0]
    wi, wg, wo, wu, wd = _gather_weights({n: a[n][0] for n in BIG_NAMES})
    p = {n: a[n][0] for n in SMALL_NAMES}
    loss, grad_x, big, cot, disc_in = _local_step(x, target, wi, wg, wo, wu, wd, p)

    cot_list = [cot[n] for n in COT_NAMES]
    packed = _pack(cot_list)
    core = lax.axis_index("c").astype(jnp.int32).reshape(1)
    chip_id = (2 * lax.axis_index("x") + lax.axis_index("y")).astype(jnp.int32).reshape(1)
    got, got_packed = _pair_exchange(big, packed)
    chip = [_pair_sum(n, big[n], g, core) for n, g in zip(BIG_NAMES, got)]
    chip_packed = _sum_arrays([packed, got_packed], "pair_sum_small")
    slots = _chip_exchange(chip, chip_packed)
    pieces = [_chip_sum(n, o, q, chip_id) for n, o, q in zip(BIG_NAMES, chip, slots[:N_BIG])]
    small_sum = _chip_sum("small", chip_packed, slots[N_BIG], chip_id)
    shard_grads = {}
    for n, mine, theirs in zip(BIG_NAMES, pieces, _half_exchange(pieces)):
        h_ax = BIG[n][3]
        shard_grads[n] = jnp.where(core[0] == 0, jnp.concatenate([mine, theirs], axis=h_ax),
                                   jnp.concatenate([theirs, mine], axis=h_ax))
    small_grads = _small_grads(dict(zip(COT_NAMES, _unpack(small_sum, cot_list))), disc_in, p)

    grads, delta, new_m, new_v = {}, {}, {}, {}
    for n in BIG_NAMES:
        grads[n] = shard_grads[n]
        delta[n], new_m[n], new_v[n] = _adamw(a[n][0], grads[n], a["m_" + n][0], a["v_" + n][0], "adamw_" + n)
    small_w = [p[n] for n in SMALL_NAMES]
    res = _adamw(_pack(small_w), _pack([small_grads[n] for n in SMALL_NAMES]), _pack([a["m_" + n][0] for n in SMALL_NAMES]),
                 _pack([a["v_" + n][0] for n in SMALL_NAMES]), "adamw_small")
    for store, packed_out in zip((delta, new_m, new_v), res):
        store.update(zip(SMALL_NAMES, _unpack(packed_out, small_w)))
    grads.update(small_grads)

    total = lax.psum(loss, ("x", "y", "c"))
    out = [total, grad_x[None]]
    for store in (grads, delta, new_m, new_v):
        out += [store[n].reshape(a[n].shape) for n in WEIGHT_NAMES]
    return tuple(out)


def kernel(x, norm1_g, w_in, q_norm_g, k_norm_g, ssm_a_re, ssm_a_im, ssm_log_dt, ssm_b_re, ssm_b_im, ssm_c_re, ssm_c_im, ssm_d, glu_w, glu_b, attn_out_norm_g, ssm_out_norm_g, w_out, norm2_g, w_mlp_up, w_mlp_down, loss_target, m_norm1_g, m_w_in, m_q_norm_g, m_k_norm_g, m_ssm_a_re, m_ssm_a_im, m_ssm_log_dt, m_ssm_b_re, m_ssm_b_im, m_ssm_c_re, m_ssm_c_im, m_ssm_d, m_glu_w, m_glu_b, m_attn_out_norm_g, m_ssm_out_norm_g, m_w_out, m_norm2_g, m_w_mlp_up, m_w_mlp_down, v_norm1_g, v_w_in, v_q_norm_g, v_k_norm_g, v_ssm_a_re, v_ssm_a_im, v_ssm_log_dt, v_ssm_b_re, v_ssm_b_im, v_ssm_c_re, v_ssm_c_im, v_ssm_d, v_glu_w, v_glu_b, v_attn_out_norm_g, v_ssm_out_norm_g, v_w_out, v_norm2_g, v_w_mlp_up, v_w_mlp_down):
    return _train_step(dict(locals()))
```

```python
import functools
import math

import jax
import jax.numpy as jnp
from jax import lax
from jax.experimental import pallas as pl
from jax.experimental.pallas import tpu as pltpu

F32 = jnp.float32
BF16 = jnp.bfloat16
MESH = pl.DeviceIdType.MESH

D_MODEL = 1024
ATTN_W = 512
SSM_W = 512
HEAD = 64
D_FF = 4096
PROJ_W = 2048
N_GROUPS = 32
N_STATE = 64
GROUP_W = 16
EPS = 1e-6
NEG = -1e30
DILATIONS = (1, 4, 16)
BLK = 128
TILE = 2048
LANES = 128
N_LB = SSM_W // LANES
N_SLAB = 2 * N_LB * N_STATE * 8 // LANES // N_LB
VMEM_LIMIT = 56 * 1024 * 1024

ADAM_LR, ADAM_B1, ADAM_B2, ADAM_EPS, ADAM_WD, ADAM_STEP = 0.001, 0.9, 0.999, 1e-08, 0.01, 10


def _params(*sem):
    return pltpu.CompilerParams(dimension_semantics=sem, vmem_limit_bytes=VMEM_LIMIT)


def _nt(a, b):
    return lax.dot_general(a, b, (((1,), (1,)), ((), ())), preferred_element_type=F32)


def _tn(a, b):
    return lax.dot_general(a, b, (((0,), (0,)), ((), ())), preferred_element_type=F32)


def _mm(a, b):
    return jnp.dot(a, b, preferred_element_type=F32)


def _group_mean(t, ones_bd, width):
    hi = t.astype(BF16)
    lo = (t - hi.astype(F32)).astype(BF16)
    return (_mm(hi, ones_bd) + _mm(lo, ones_bd)) * (1.0 / width)


def _rms(x):
    return lax.rsqrt(jnp.mean(x * x, axis=-1, keepdims=True) + EPS)


def _rms_bwd(dy, x, r, g):
    xh = x * r
    dxh = dy * g
    dx = r * (dxh - xh * jnp.mean(dxh * xh, axis=-1, keepdims=True))
    return dx, dy * xh


def _colsum(x):
    return jnp.sum(x, axis=0, keepdims=True)


def _row_block(rows, cap):
    for b in range(min(rows, cap) // 8 * 8, 0, -8):
        if rows % b == 0:
            return b
    raise ValueError(f"no row block for {rows} rows")


def _inproj_fwd(x, g1, wi, gq, gk, ones64):
    t_len = x.shape[0]
    tm = 512
    n_hp = ATTN_W // LANES

    def body(x_ref, g1_ref, wi_ref, gq_ref, gk_ref, bd_ref, xn_ref, q_ref, k_ref, v_ref, u_ref, qr_ref, kr_ref):
        xv = x_ref[...]
        xn = (xv * _rms(xv) * g1_ref[...]).astype(BF16)
        xn_ref[...] = xn
        proj = _mm(xn, wi_ref[...])
        q = proj[:, 0:ATTN_W]
        k = proj[:, ATTN_W:2 * ATTN_W]
        v = proj[:, 2 * ATTN_W:3 * ATTN_W]
        u_ref[...] = proj[:, 3 * ATTN_W:]
        qr_ref[...] = q
        kr_ref[...] = k
        bd = bd_ref[...]
        qn = q * lax.rsqrt(_group_mean(q * q, bd, HEAD) + EPS) * gq_ref[...] * (HEAD ** -0.5)
        kn = k * lax.rsqrt(_group_mean(k * k, bd, HEAD) + EPS) * gk_ref[...]
        for hp in range(n_hp):
            sl = slice(hp * LANES, (hp + 1) * LANES)
            q_ref[hp] = qn[:, sl]
            k_ref[hp] = kn[:, sl]
            v_ref[hp] = v[:, sl]

    row = lambda i: (i, 0)
    const = lambda i: (0, 0)
    hp_spec = pl.BlockSpec((n_hp, tm, LANES), lambda i: (0, i, 0))
    hp_shape = jax.ShapeDtypeStruct((n_hp, t_len, LANES), F32)
    return pl.pallas_call(
        body, name="inproj_fwd", grid=(t_len // tm,),
        in_specs=[pl.BlockSpec((tm, D_MODEL), row), pl.BlockSpec((1, D_MODEL), const),
                  pl.BlockSpec((D_MODEL, PROJ_W), const), pl.BlockSpec((1, ATTN_W), const),
                  pl.BlockSpec((1, ATTN_W), const), pl.BlockSpec((ATTN_W, ATTN_W), const)],
        out_specs=[pl.BlockSpec((tm, D_MODEL), row), hp_spec, hp_spec, hp_spec,
                   pl.BlockSpec((tm, SSM_W), row), pl.BlockSpec((tm, ATTN_W), row), pl.BlockSpec((tm, ATTN_W), row)],
        out_shape=[jax.ShapeDtypeStruct((t_len, D_MODEL), BF16), hp_shape, hp_shape, hp_shape,
                   jax.ShapeDtypeStruct((t_len, SSM_W), F32), jax.ShapeDtypeStruct((t_len, ATTN_W), F32),
                   jax.ShapeDtypeStruct((t_len, ATTN_W), F32)],
        compiler_params=_params("arbitrary"),
    )(x, g1, wi, gq, gk, ones64)


def _attn_masks():
    head0 = lax.broadcasted_iota(jnp.int32, (BLK, LANES), 1) < HEAD
    row = lax.broadcasted_iota(jnp.int32, (2 * BLK, 2 * BLK), 0) & (BLK - 1)
    col = lax.broadcasted_iota(jnp.int32, (2 * BLK, 2 * BLK), 1)
    return head0, (col < BLK) & (col >= row), (col >= BLK) & (col - BLK <= row)


def _stack_heads(x, head0):
    return jnp.concatenate([jnp.where(head0, x, 0.0), jnp.where(head0, 0.0, x)], axis=0).astype(BF16)


def _unit_rows(uidx, d):
    nb = TILE // (BLK * d)
    r = lax.div(uidx, nb)
    b = lax.rem(uidx, nb)
    start = r + d * BLK * b
    if d == 1:
        start = pl.multiple_of(start, BLK)
        mk = lambda s: pl.ds(pl.multiple_of(s, BLK), BLK)
    else:
        mk = lambda s: pl.ds(s, BLK, stride=d)
    return b, mk(start), mk(TILE + start), mk(TILE + start - d * BLK)


def _attn_fwd(q, k, v, side=None, side_args=()):
    n_hp, t_len, _ = q.shape
    nt = t_len // TILE
    ns = side.n if side is not None else 0
    n_steps = n_hp * nt

    def body(*refs):
        q_ref, kp_ref, kc_ref, vp_ref, vc_ref = refs[0:5]
        o_ref, lse_ref = refs[5 + ns:7 + ns]
        kk, vv, m_s, l_s, acc_s = refs[7 + 2 * ns:12 + 2 * ns]
        t = pl.program_id(1)
        step = pl.program_id(0) * nt + t
        if side is not None:
            side.bind(refs[5:5 + ns], refs[7 + ns:7 + 2 * ns], refs[12 + 2 * ns:])
            pl.when(step == 0)(side.start)
            pl.when(step == n_steps // 2)(side.forward)
        kk[0:TILE] = kp_ref[0]
        kk[TILE:] = kc_ref[0]
        vv[0:TILE] = vp_ref[0]
        vv[TILE:] = vc_ref[0]
        head0, band_prev, band_cur = _attn_masks()

        for pi, d in enumerate(DILATIONS):
            def unit(uidx, carry, d=d, pi=pi):
                b, rows_q, rows_c, rows_p = _unit_rows(uidx, d)
                mask = band_cur | (band_prev & ((t > 0) | (b > 0)))
                q2 = _stack_heads(q_ref.at[0][rows_q, :], head0)
                kcat = jnp.concatenate([kk[rows_p, :], kk[rows_c, :]], axis=0).astype(BF16)
                vcat = jnp.concatenate([vv[rows_p, :], vv[rows_c, :]], axis=0).astype(BF16)
                s = jnp.where(mask, _nt(q2, kcat), NEG)
                m = jnp.max(s, axis=1, keepdims=True)
                p = jnp.exp(s - m)
                ls = jnp.sum(p, axis=1, keepdims=True)
                pv = _mm(p.astype(BF16), vcat)
                m_s.at[pi][rows_q, :] = jnp.where(head0, m[0:BLK], m[BLK:])
                l_s.at[pi][rows_q, :] = jnp.where(head0, ls[0:BLK], ls[BLK:])
                acc_s.at[pi][rows_q, :] = jnp.where(head0, pv[0:BLK], pv[BLK:])
                return carry

            lax.fori_loop(0, TILE // BLK, unit, 0, unroll=8)

        m_all = jnp.maximum(jnp.maximum(m_s[0], m_s[1]), m_s[2])
        num = jnp.zeros((TILE, LANES), F32)
        den = jnp.zeros((TILE, LANES), F32)
        for pi in range(len(DILATIONS)):
            wgt = jnp.exp(m_s[pi] - m_all)
            num = num + acc_s[pi] * wgt
            den = den + l_s[pi] * wgt
        o_ref[...] = num / den
        lse_ref[0] = m_all + jnp.log(den)
        if side is not None:
            pl.when(step == n_steps - 1)(side.finish)

    cur = lambda hp, t: (hp, t, 0)
    prev = lambda hp, t: (hp, jnp.maximum(t - 1, 0), 0)
    blk = (1, TILE, LANES)
    per_pattern = pltpu.VMEM((len(DILATIONS), TILE, LANES), F32)
    extra = (side.in_specs(), side.out_specs(), side.out_shapes(), side.scratch_shapes()) if side is not None else ([], [], [], [])
    return pl.pallas_call(
        body, name="attn_fwd", grid=(n_hp, nt),
        in_specs=[pl.BlockSpec(blk, cur), pl.BlockSpec(blk, prev), pl.BlockSpec(blk, cur),
                  pl.BlockSpec(blk, prev), pl.BlockSpec(blk, cur)] + extra[0],
        out_specs=[pl.BlockSpec((TILE, LANES), lambda hp, t: (t, hp)), pl.BlockSpec(blk, cur)] + extra[1],
        out_shape=[jax.ShapeDtypeStruct((t_len, ATTN_W), F32), jax.ShapeDtypeStruct((n_hp, t_len, LANES), F32)] + extra[2],
        scratch_shapes=[pltpu.VMEM((2 * TILE, LANES), F32), pltpu.VMEM((2 * TILE, LANES), F32),
                        per_pattern, per_pattern, per_pattern] + extra[3],
        compiler_params=_params("arbitrary", "arbitrary"),
    )(q, k, k, v, v, *side_args)


def _attn_bwd(q, k, v, o, do, lse, ones_hp, side=None, side_args=()):
    n_hp, t_len, _ = q.shape
    nt = t_len // TILE
    ns = side.n if side is not None else 0
    n_pat = len(DILATIONS)

    def body(*refs):
        q_ref, kp_ref, kc_ref, vp_ref, vc_ref, o_ref, do_ref, lse_ref, bd_ref = refs[0:9]
        dq_ref, dk_ref, dv_ref = refs[9 + ns:12 + ns]
        kk, vv, dq_s, dkc, dkp, dvc, dvp, hold_k, hold_v, dl_s = refs[12 + 2 * ns:22 + 2 * ns]
        t = pl.program_id(1)
        if side is not None:
            side.bind(refs[9:9 + ns], refs[12 + ns:12 + 2 * ns], refs[22 + 2 * ns:])
            pl.when((pl.program_id(0) == 0) & (t == 0))(side.start)

        @pl.when(t < nt)
        def _():
            kk[0:TILE] = kp_ref[0]
            kk[TILE:] = kc_ref[0]
            vv[0:TILE] = vp_ref[0]
            vv[TILE:] = vc_ref[0]
            dl_s[...] = _group_mean(do_ref[...] * o_ref[...], bd_ref[...], 1.0)
            head0, band_prev, band_cur = _attn_masks()

            for pi, d in enumerate(DILATIONS):
                def unit(uidx, carry, d=d, pi=pi):
                    b, rows_q, rows_c, rows_p = _unit_rows(uidx, d)
                    mask = band_cur | (band_prev & ((t > 0) | (b > 0)))
                    q2 = _stack_heads(q_ref.at[0][rows_q, :], head0)
                    do2 = _stack_heads(do_ref[rows_q, :], head0)
                    lse_f = lse_ref.at[0][rows_q, :]
                    dl_f = dl_s[rows_q, :]
                    lse2 = jnp.concatenate([lse_f[:, 0:1], lse_f[:, HEAD:HEAD + 1]], axis=0)
                    dl2 = jnp.concatenate([dl_f[:, 0:1], dl_f[:, HEAD:HEAD + 1]], axis=0)
                    kcat = jnp.concatenate([kk[rows_p, :], kk[rows_c, :]], axis=0).astype(BF16)
                    vcat = jnp.concatenate([vv[rows_p, :], vv[rows_c, :]], axis=0).astype(BF16)
                    p = jnp.where(mask, jnp.exp(_nt(q2, kcat) - lse2), 0.0)
                    ds = (p * (_nt(do2, vcat) - dl2)).astype(BF16)
                    dq2 = _mm(ds, kcat)
                    dq_s.at[pi][rows_q, :] = jnp.where(head0, dq2[0:BLK], dq2[BLK:])
                    dk2 = _tn(ds, q2)
                    dv2 = _tn(p.astype(BF16), do2)
                    dkp.at[pi][rows_q, :] = dk2[0:BLK]
                    dkc.at[pi][rows_q, :] = dk2[BLK:]
                    dvp.at[pi][rows_q, :] = dv2[0:BLK]
                    dvc.at[pi][rows_q, :] = dv2[BLK:]
                    return carry

                lax.fori_loop(0, TILE // BLK, unit, 0, unroll=4)

            dq_ref[...] = dq_s[0] + dq_s[1] + dq_s[2]

        @pl.when(t > 0)
        def _():
            dk_ref[...] = hold_k[...]
            dv_ref[...] = hold_v[...]

        @pl.when((t > 0) & (t < nt))
        def _():
            for pi, d in enumerate(DILATIONS):
                back = d * BLK
                dk_ref[TILE - back:, :] = dk_ref[TILE - back:, :] + dkp[pi, 0:back, :]
                dv_ref[TILE - back:, :] = dv_ref[TILE - back:, :] + dvp[pi, 0:back, :]

        @pl.when(t < nt)
        def _():
            hold_k[...] = dkc[0] + dkc[1] + dkc[2]
            hold_v[...] = dvc[0] + dvc[1] + dvc[2]
            for pi, d in enumerate(DILATIONS):
                back = d * BLK
                if back < TILE:
                    hold_k[0:TILE - back, :] = hold_k[0:TILE - back, :] + dkp[pi, back:, :]
                    hold_v[0:TILE - back, :] = hold_v[0:TILE - back, :] + dvp[pi, back:, :]

        if side is not None:
            pl.when((pl.program_id(0) == n_hp - 1) & (t == nt))(side.finish)

    last = nt - 1
    extra = (side.in_specs(), side.out_specs(), side.out_shapes(), side.scratch_shapes()) if side is not None else ([], [], [], [])
    cur = lambda hp, t: (hp, jnp.minimum(t, last), 0)
    prev = lambda hp, t: (hp, jnp.clip(t - 1, 0, last), 0)
    cur2 = lambda hp, t: (jnp.minimum(t, last), hp)
    prev2 = lambda hp, t: (jnp.maximum(t - 1, 0), hp)
    blk = (1, TILE, LANES)
    blk2 = (TILE, LANES)
    out = jax.ShapeDtypeStruct((t_len, ATTN_W), F32)
    return pl.pallas_call(
        body, name="attn_bwd", grid=(n_hp, nt + 1),
        in_specs=[pl.BlockSpec(blk, cur), pl.BlockSpec(blk, prev), pl.BlockSpec(blk, cur),
                  pl.BlockSpec(blk, prev), pl.BlockSpec(blk, cur), pl.BlockSpec(blk2, cur2),
                  pl.BlockSpec(blk2, cur2), pl.BlockSpec(blk, cur), pl.BlockSpec((LANES, LANES), lambda hp, t: (0, 0))]
        + extra[0],
        out_specs=[pl.BlockSpec(blk2, cur2), pl.BlockSpec(blk2, prev2), pl.BlockSpec(blk2, prev2)] + extra[1],
        out_shape=[out, out, out] + extra[2],
        scratch_shapes=[pltpu.VMEM((2 * TILE, LANES), F32), pltpu.VMEM((2 * TILE, LANES), F32)]
        + [pltpu.VMEM((n_pat, TILE, LANES), F32)] * 5 + [pltpu.VMEM((TILE, LANES), F32)] * 3 + extra[3],
        compiler_params=_params("arbitrary", "arbitrary"),
    )(q, k, k, v, v, o, do, lse, ones_hp, *side_args)


def _discretise(lr, li, ldt, br, bi):
    dt = jnp.exp(ldt)
    mag = jnp.exp(lr * dt)
    ab_r, ab_i = mag * jnp.cos(li * dt), mag * jnp.sin(li * dt)
    den = lr * lr + li * li
    nr, ni = ab_r - 1.0, ab_i
    cr = (nr * lr + ni * li) / den
    ci = (ni * lr - nr * li) / den
    return ab_r, ab_i, cr * br - ci * bi, cr * bi + ci * br


def _disc_fwd(lr, li, ldt, br, bi):
    def body(lr_ref, li_ref, ldt_ref, br_ref, bi_ref, ar_o, ai_o, bbr_o, bbi_o):
        outs = _discretise(lr_ref[...], li_ref[...], ldt_ref[...], br_ref[...], bi_ref[...])
        for o_ref, val in zip((ar_o, ai_o, bbr_o, bbi_o), outs):
            o_ref[...] = val

    col = jax.ShapeDtypeStruct(lr.shape, F32)
    mat = jax.ShapeDtypeStruct(br.shape, F32)
    return pl.pallas_call(body, name="s5_disc_fwd", out_shape=[col, col, mat, mat])(lr, li, ldt, br, bi)


def _disc_bwd(lr, li, ldt, br, bi, d_ar, d_ai, d_bbr, d_bbi, group_sum):
    def body(lr_ref, li_ref, ldt_ref, br_ref, bi_ref, c1, c2, c3, c4, gs_ref, dlr_o, dli_o, dldt_o, dbr_o, dbi_o):
        _, vjp = jax.vjp(_discretise, lr_ref[...], li_ref[...], ldt_ref[...], br_ref[...], bi_ref[...])
        dlr, dli, dldt, dbr, dbi = vjp((c1[...], c2[...], c3[...], c4[...]))
        dlr_o[...] = dlr
        dli_o[...] = dli
        dbr_o[...] = dbr
        dbi_o[...] = dbi
        wide = jnp.broadcast_to(dldt, (dldt.shape[0], LANES))
        dldt_o[...] = jnp.dot(gs_ref[...], wide, precision=lax.Precision.HIGHEST, preferred_element_type=F32)

    col = jax.ShapeDtypeStruct(lr.shape, F32)
    mat = jax.ShapeDtypeStruct(br.shape, F32)
    return pl.pallas_call(
        body, name="s5_disc_bwd", out_shape=[col, col, jax.ShapeDtypeStruct((N_GROUPS, LANES), F32), mat, mat],
    )(lr, li, ldt, br, bi, d_ar, d_ai, d_bbr, d_bbi, group_sum)


N_CHUNK = TILE // BLK
HALF = 4


def _cmul(ar, ai, xr, xi):
    return ar * xr - ai * xi, ar * xi + ai * xr


def _power_table(a_ref, tab, sign, reverse):
    ar = [a_ref[0, j:j + 1, :] for j in range(HALF)]
    ai = [sign * a_ref[0, HALF + j:HALF + j + 1, :] for j in range(HALF)]

    def step(s, cur):
        row = pl.ds((BLK - 1 - s) if reverse else s, 1)
        nxt = []
        for j in range(HALF):
            tab.at[j][row, :] = cur[j]
            tab.at[HALF + j][row, :] = cur[HALF + j]
            nxt.append(_cmul(ar[j], ai[j], cur[j], cur[HALF + j]))
        return tuple(p[0] for p in nxt) + tuple(p[1] for p in nxt)

    lax.fori_loop(0, BLK, step, tuple(ar) + tuple(ai))


def _interleave(src, dst):
    for c in range(N_CHUNK):
        dst[pl.ds(c, BLK, stride=N_CHUNK), :] = src[c * BLK:(c + 1) * BLK, :]


def _deinterleave(src, dst):
    for c in range(N_CHUNK):
        dst[c * BLK:(c + 1) * BLK, :] = src[pl.ds(c, BLK, stride=N_CHUNK), :]


def _step_rows(s):
    return pl.ds(pl.multiple_of(s * N_CHUNK, N_CHUNK), N_CHUNK)


def _chunk_scan(buf, a_ref, sign, reverse):
    ar = [jnp.broadcast_to(a_ref[0, j:j + 1, :], (N_CHUNK, LANES)) for j in range(HALF)]
    ai = [sign * jnp.broadcast_to(a_ref[0, HALF + j:HALF + j + 1, :], (N_CHUNK, LANES)) for j in range(HALF)]

    def step(i, carry):
        s = (BLK - 1 - i) if reverse else i
        rows = _step_rows(s)
        out = []
        for j in range(HALF):
            pr, pi = _cmul(ar[j], ai[j], carry[j], carry[HALF + j])
            xr = buf.at[j][rows, :] + pr
            xi = buf.at[HALF + j][rows, :] + pi
            buf.at[j][rows, :] = xr
            buf.at[HALF + j][rows, :] = xi
            out.append((xr, xi))
        return tuple(p[0] for p in out) + tuple(p[1] for p in out)

    zero = jnp.zeros((N_CHUNK, LANES), F32)
    lax.fori_loop(0, BLK, step, (zero,) * (2 * HALF), unroll=2)


def _chunk_states(buf, carry_s, xin_s, tab, reverse):
    edge = 0 if reverse else BLK - 1
    top = 0 if reverse else BLK - 1
    pw = [tab[j, top:top + 1, :] for j in range(2 * HALF)]
    cur = [carry_s[j:j + 1, :] for j in range(2 * HALF)]
    summary = [buf[j, edge * N_CHUNK:(edge + 1) * N_CHUNK, :] for j in range(2 * HALF)]
    order = range(N_CHUNK - 1, -1, -1) if reverse else range(N_CHUNK)
    for c in order:
        for j in range(2 * HALF):
            xin_s[j, c:c + 1, :] = cur[j]
        nxt = []
        for j in range(HALF):
            pr, pi = _cmul(pw[j], pw[HALF + j], cur[j], cur[HALF + j])
            nxt.append((pr + summary[j][c:c + 1, :], pi + summary[HALF + j][c:c + 1, :]))
        cur = [p[0] for p in nxt] + [p[1] for p in nxt]
    for j in range(2 * HALF):
        carry_s[j:j + 1, :] = cur[j]


def _s5_fwd(u, a_cat, b_mat, c_mat, d_skip):
    t_len = u.shape[0]
    nt = t_len // TILE

    def body(u_ref, a_ref, b_ref, c_ref, d_ref, y_ref, st_ref, xs, us, tab, carry_s, xin_s):
        sb = pl.program_id(1)

        @pl.when(sb == 0)
        def _():
            _power_table(a_ref, tab, 1.0, False)
            carry_s[...] = jnp.zeros_like(carry_s)

        st_ref[0, 0] = carry_s[...]
        _interleave(u_ref, us)
        uv = us[...]
        bu = _mm(uv.astype(BF16), b_ref[0])
        for j in range(2 * HALF):
            xs[j] = bu[:, j * LANES:(j + 1) * LANES]
        _chunk_scan(xs, a_ref, 1.0, False)
        _chunk_states(xs, carry_s, xin_s, tab, False)
        xin = [xin_s[j] for j in range(2 * HALF)]

        def fix(s, acc):
            rows = _step_rows(s)
            for j in range(HALF):
                pr, pi = _cmul(tab.at[j][pl.ds(s, 1), :], tab.at[HALF + j][pl.ds(s, 1), :], xin[j], xin[HALF + j])
                xs.at[j][rows, :] = xs.at[j][rows, :] + pr
                xs.at[HALF + j][rows, :] = xs.at[HALF + j][rows, :] + pi
            return acc

        lax.fori_loop(0, BLK, fix, 0, unroll=2)
        xcat = jnp.concatenate([xs[j].astype(BF16) for j in range(2 * HALF)], axis=1)
        us[...] = d_ref[0] * uv + _mm(xcat, c_ref[0])
        _deinterleave(us, y_ref)

    return pl.pallas_call(
        body, name="s5_fwd", grid=(N_LB, nt),
        in_specs=[pl.BlockSpec((TILE, LANES), lambda lb, sb: (sb, lb)),
                  pl.BlockSpec((1, 2 * HALF, LANES), lambda lb, sb: (lb, 0, 0)),
                  pl.BlockSpec((1, LANES, 2 * HALF * LANES), lambda lb, sb: (lb, 0, 0)),
                  pl.BlockSpec((1, 2 * HALF * LANES, LANES), lambda lb, sb: (lb, 0, 0)),
                  pl.BlockSpec((1, 1, LANES), lambda lb, sb: (lb, 0, 0))],
        out_specs=[pl.BlockSpec((TILE, LANES), lambda lb, sb: (sb, lb)),
                   pl.BlockSpec((1, 1, 2 * HALF, LANES), lambda lb, sb: (lb, sb, 0, 0))],
        out_shape=[jax.ShapeDtypeStruct((t_len, SSM_W), F32), jax.ShapeDtypeStruct((N_LB, nt, 2 * HALF, LANES), F32)],
        scratch_shapes=[pltpu.VMEM((2 * HALF, TILE, LANES), F32), pltpu.VMEM((TILE, LANES), F32),
                        pltpu.VMEM((2 * HALF, BLK, LANES), F32),
                        pltpu.VMEM((2 * HALF, LANES), F32), pltpu.VMEM((2 * HALF, N_CHUNK, LANES), F32)],
        compiler_params=_params("arbitrary", "arbitrary"),
    )(u, a_cat, b_mat, c_mat, d_skip)


def _s5_bwd(u, dy, states, a_cat, b_mat, c_mat, d_skip):
    t_len = u.shape[0]
    nt = t_len // TILE
    last = nt - 1

    def body(u_ref, dy_ref, st_ref, a_ref, b_ref, c_ref, d_ref, du_ref, db_ref, dc_ref, da_ref, dd_ref,
             xs, gs, us, dys, tab, tabc, carry_s, lam_s, xin_s, lin_s):
        sb = pl.program_id(1)

        @pl.when(sb == 0)
        def _():
            _power_table(a_ref, tab, 1.0, False)
            _power_table(a_ref, tabc, -1.0, True)
            lam_s[...] = jnp.zeros_like(lam_s)
            db_ref[...] = jnp.zeros_like(db_ref)
            dc_ref[...] = jnp.zeros_like(dc_ref)
            da_ref[...] = jnp.zeros_like(da_ref)
            dd_ref[...] = jnp.zeros_like(dd_ref)

        _interleave(u_ref, us)
        _interleave(dy_ref, dys)
        uv = us[...]
        dyv = dys[...]
        ub = uv.astype(BF16)
        dyb = dyv.astype(BF16)
        carry_s[...] = st_ref[0, 0]
        bu = _mm(ub, b_ref[0])
        gy = _nt(dyb, c_ref[0])
        for j in range(2 * HALF):
            xs[j] = bu[:, j * LANES:(j + 1) * LANES]
            gs[j] = gy[:, j * LANES:(j + 1) * LANES]
        _chunk_scan(xs, a_ref, 1.0, False)
        _chunk_states(xs, carry_s, xin_s, tab, False)
        _chunk_scan(gs, a_ref, -1.0, True)
        _chunk_states(gs, lam_s, lin_s, tabc, True)
        zero = jnp.zeros((N_CHUNK, LANES), F32)
        for grp in range(0, HALF, 2):
            slabs = (grp, grp + 1)
            xin = [(xin_s[j], xin_s[HALF + j]) for j in slabs]
            lin = [(lin_s[j], lin_s[HALF + j]) for j in slabs]

            def fix(i, carry, slabs=slabs, xin=xin, lin=lin):
                s = BLK - 1 - i
                rows = _step_rows(s)
                out = []
                for k, j in enumerate(slabs):
                    nr, ni, acc_r, acc_i = carry[4 * k:4 * k + 4]
                    pr, pi = _cmul(tab.at[j][pl.ds(s, 1), :], tab.at[HALF + j][pl.ds(s, 1), :], xin[k][0], xin[k][1])
                    xr = xs.at[j][rows, :] + pr
                    xi = xs.at[HALF + j][rows, :] + pi
                    xs.at[j][rows, :] = xr
                    xs.at[HALF + j][rows, :] = xi
                    qr, qi = _cmul(tabc.at[j][pl.ds(s, 1), :], tabc.at[HALF + j][pl.ds(s, 1), :], lin[k][0], lin[k][1])
                    lr_ = gs.at[j][rows, :] + qr
                    li_ = gs.at[HALF + j][rows, :] + qi
                    gs.at[j][rows, :] = lr_
                    gs.at[HALF + j][rows, :] = li_
                    out += [lr_, li_, acc_r + (xr * nr + xi * ni), acc_i + (xr * ni - xi * nr)]
                return tuple(out)

            init = []
            for k in range(len(slabs)):
                init += [lin[k][0], lin[k][1], zero, zero]
            res = lax.fori_loop(0, BLK, fix, tuple(init), unroll=2)
            for k, j in enumerate(slabs):
                da_ref[0, j:j + 1, :] = da_ref[0, j:j + 1, :] + _colsum(res[4 * k + 2])
                da_ref[0, HALF + j:HALF + j + 1, :] = da_ref[0, HALF + j:HALF + j + 1, :] + _colsum(res[4 * k + 3])
        lam = jnp.concatenate([gs[j].astype(BF16) for j in range(2 * HALF)], axis=1)
        xcat = jnp.concatenate([xs[j].astype(BF16) for j in range(2 * HALF)], axis=1)
        us[...] = _nt(lam, b_ref[0]) + d_ref[0] * dyv
        _deinterleave(us, du_ref)
        db_ref[0] = db_ref[0] + _tn(ub, lam)
        dc_ref[0] = dc_ref[0] + _tn(dyb, xcat)
        dd_ref[0] = dd_ref[0] + _colsum(dyv * uv)

    rev = lambda lb, sb: (last - sb, lb)
    per_lb = lambda lb, sb: (lb, 0, 0)
    wide = 2 * HALF * LANES
    return pl.pallas_call(
        body, name="s5_bwd", grid=(N_LB, nt),
        in_specs=[pl.BlockSpec((TILE, LANES), rev), pl.BlockSpec((TILE, LANES), rev),
                  pl.BlockSpec((1, 1, 2 * HALF, LANES), lambda lb, sb: (lb, last - sb, 0, 0)),
                  pl.BlockSpec((1, 2 * HALF, LANES), per_lb), pl.BlockSpec((1, LANES, wide), per_lb),
                  pl.BlockSpec((1, wide, LANES), per_lb), pl.BlockSpec((1, 1, LANES), per_lb)],
        out_specs=[pl.BlockSpec((TILE, LANES), rev), pl.BlockSpec((1, LANES, wide), per_lb),
                   pl.BlockSpec((1, LANES, wide), per_lb), pl.BlockSpec((1, 2 * HALF, LANES), per_lb),
                   pl.BlockSpec((1, 1, LANES), per_lb)],
        out_shape=[jax.ShapeDtypeStruct((t_len, SSM_W), F32), jax.ShapeDtypeStruct((N_LB, LANES, wide), F32),
                   jax.ShapeDtypeStruct((N_LB, LANES, wide), F32), jax.ShapeDtypeStruct((N_LB, 2 * HALF, LANES), F32),
                   jax.ShapeDtypeStruct((N_LB, 1, LANES), F32)],
        scratch_shapes=[pltpu.VMEM((2 * HALF, TILE, LANES), F32), pltpu.VMEM((2 * HALF, TILE, LANES), F32),
                        pltpu.VMEM((TILE, LANES), F32), pltpu.VMEM((TILE, LANES), F32),
                        pltpu.VMEM((2 * HALF, BLK, LANES), F32), pltpu.VMEM((2 * HALF, BLK, LANES), F32),
                        pltpu.VMEM((2 * HALF, LANES), F32), pltpu.VMEM((2 * HALF, LANES), F32),
                        pltpu.VMEM((2 * HALF, N_CHUNK, LANES), F32), pltpu.VMEM((2 * HALF, N_CHUNK, LANES), F32)],
        compiler_params=_params("arbitrary", "arbitrary"),
    )(u, dy, states, a_cat, b_mat, c_mat, d_skip)


_GELU_C = math.sqrt(2.0 / math.pi)
_GELU_K = 0.044715


def _gelu(y):
    t = jnp.tanh(_GELU_C * (y + _GELU_K * (y * y * y)))
    return y * (0.5 * (1.0 + t)), t


def _gelu_grad(y, t):
    return 0.5 * (1.0 + t) + 0.5 * y * (1.0 - t * t) * (_GELU_C * (1.0 + 3.0 * _GELU_K * y * y))


def _glu(y, wg, bias):
    z, t = _gelu(y)
    sg = jax.nn.sigmoid(_mm(z.astype(BF16), wg) + bias)
    return z, t, sg


def _mix_fwd(attn, y, x, wg, glu_b, ga, gs, wo):
    t_len = x.shape[0]
    tm = 512

    def body(attn_ref, y_ref, x_ref, wg_ref, b_ref, ga_ref, gs_ref, wo_ref, x2_ref, mix_ref, z_ref):
        z, _, sg = _glu(y_ref[...], wg_ref[...], b_ref[...])
        z_ref[...] = z.astype(BF16)
        s = z * sg
        av = attn_ref[...]
        an = (av * _rms(av) * ga_ref[...]).astype(BF16)
        sn = (s * _rms(s) * gs_ref[...]).astype(BF16)
        mix_ref[:, 0:ATTN_W] = an
        mix_ref[:, ATTN_W:] = sn
        x2_ref[...] = x_ref[...] + _mm(an, wo_ref[0:ATTN_W, :]) + _mm(sn, wo_ref[ATTN_W:, :])

    row = lambda i: (i, 0)
    const = lambda i: (0, 0)
    return pl.pallas_call(
        body, name="mix_fwd", grid=(t_len // tm,),
        in_specs=[pl.BlockSpec((tm, ATTN_W), row), pl.BlockSpec((tm, SSM_W), row), pl.BlockSpec((tm, D_MODEL), row),
                  pl.BlockSpec((SSM_W, SSM_W), const), pl.BlockSpec((1, SSM_W), const), pl.BlockSpec((1, ATTN_W), const),
                  pl.BlockSpec((1, SSM_W), const), pl.BlockSpec((D_MODEL, D_MODEL), const)],
        out_specs=[pl.BlockSpec((tm, D_MODEL), row), pl.BlockSpec((tm, D_MODEL), row), pl.BlockSpec((tm, SSM_W), row)],
        out_shape=[jax.ShapeDtypeStruct((t_len, D_MODEL), F32), jax.ShapeDtypeStruct((t_len, D_MODEL), BF16),
                   jax.ShapeDtypeStruct((t_len, SSM_W), BF16)],
        compiler_params=_params("arbitrary"),
    )(attn, y, x, wg, glu_b, ga, gs, wo)


def _mlp(x2, target, g2, wu, wd):
    t_len = x2.shape[0]
    tm = 256
    fc = 1024
    n_fc = D_FF // fc

    def body(x2_ref, tg_ref, g2_ref, wu_hbm, wd_hbm, dx2_ref, hdn_ref, dup_ref, h_ref, dyb_ref, dg2_ref, loss_ref,
             wu_s, wd_s, relu_s, sem):
        @pl.when(pl.program_id(0) == 0)
        def _():
            cu = pltpu.make_async_copy(wu_hbm, wu_s, sem.at[0])
            cd = pltpu.make_async_copy(wd_hbm, wd_s, sem.at[1])
            cu.start()
            cd.start()
            cu.wait()
            cd.wait()
            dg2_ref[...] = jnp.zeros_like(dg2_ref)
            loss_ref[...] = jnp.zeros_like(loss_ref)

        x2v = x2_ref[...]
        r = _rms(x2v)
        g2v = g2_ref[...]
        h = (x2v * r * g2v).astype(BF16)
        h_ref[...] = h
        yout = x2v
        for c in range(n_fc):
            cols = slice(c * fc, (c + 1) * fc)
            ru = jnp.maximum(_mm(h, wu_s[:, cols]), 0.0)
            relu_s[:, cols] = ru
            hd = (ru * ru).astype(BF16)
            hdn_ref[:, cols] = hd
            yout = yout + _mm(hd, wd_s[cols, :])
        err = yout - tg_ref[...]
        loss_ref[...] = loss_ref[...] + 0.5 * jnp.sum(err * err) * (1.0 / D_MODEL)
        dy = err * (1.0 / D_MODEL)
        dyb = dy.astype(BF16)
        dyb_ref[...] = dyb
        dh = jnp.zeros((tm, D_MODEL), F32)
        for c in range(n_fc):
            cols = slice(c * fc, (c + 1) * fc)
            dup = (_nt(dyb, wd_s[cols, :]) * (2.0 * relu_s[:, cols])).astype(BF16)
            dup_ref[:, cols] = dup
            dh = dh + _nt(dup, wu_s[:, cols])
        dxn, gterm = _rms_bwd(dh, x2v, r, g2v)
        dx2_ref[...] = dy + dxn
        dg2_ref[...] = dg2_ref[...] + _colsum(gterm)

    row = lambda i: (i, 0)
    const = lambda i: (0, 0)
    any_spec = pl.BlockSpec(memory_space=pl.ANY)
    return pl.pallas_call(
        body, name="mlp", grid=(t_len // tm,),
        in_specs=[pl.BlockSpec((tm, D_MODEL), row), pl.BlockSpec((tm, D_MODEL), row), pl.BlockSpec((1, D_MODEL), const),
                  any_spec, any_spec],
        out_specs=[pl.BlockSpec((tm, D_MODEL), row), pl.BlockSpec((tm, D_FF), row), pl.BlockSpec((tm, D_FF), row),
                   pl.BlockSpec((tm, D_MODEL), row), pl.BlockSpec((tm, D_MODEL), row), pl.BlockSpec((1, D_MODEL), const),
                   pl.BlockSpec((1, LANES), const)],
        out_shape=[jax.ShapeDtypeStruct((t_len, D_MODEL), F32), jax.ShapeDtypeStruct((t_len, D_FF), BF16),
                   jax.ShapeDtypeStruct((t_len, D_FF), BF16), jax.ShapeDtypeStruct((t_len, D_MODEL), BF16),
                   jax.ShapeDtypeStruct((t_len, D_MODEL), BF16), jax.ShapeDtypeStruct((1, D_MODEL), F32),
                   jax.ShapeDtypeStruct((1, LANES), F32)],
        scratch_shapes=[pltpu.VMEM((D_MODEL, D_FF), BF16), pltpu.VMEM((D_FF, D_MODEL), BF16),
                        pltpu.VMEM((tm, D_FF), F32), pltpu.SemaphoreType.DMA((2,))],
        compiler_params=_params("arbitrary"),
    )(x2, target, g2, wu, wd)


def _mix_bwd(dx2, attn, y, wg, glu_b, ga, gs, wo):
    t_len = dx2.shape[0]
    tm = 512

    def body(dx2_ref, attn_ref, y_ref, wg_ref, b_ref, ga_ref, gs_ref, wo_ref,
             dattn_ref, dy_ref, dx2b_ref, dgp_ref, dga_ref, dgs_ref, db_ref):
        @pl.when(pl.program_id(0) == 0)
        def _():
            dga_ref[...] = jnp.zeros_like(dga_ref)
            dgs_ref[...] = jnp.zeros_like(dgs_ref)
            db_ref[...] = jnp.zeros_like(db_ref)

        dx2b = dx2_ref[...].astype(BF16)
        dx2b_ref[...] = dx2b
        d_an = _nt(dx2b, wo_ref[0:ATTN_W, :])
        d_sn = _nt(dx2b, wo_ref[ATTN_W:, :])
        yv = y_ref[...]
        wg = wg_ref[...]
        z, t, sg = _glu(yv, wg, b_ref[...])
        s = z * sg
        av = attn_ref[...]
        d_attn, ga_term = _rms_bwd(d_an, av, _rms(av), ga_ref[...])
        d_s, gs_term = _rms_bwd(d_sn, s, _rms(s), gs_ref[...])
        dattn_ref[...] = d_attn
        dgp = d_s * z * sg * (1.0 - sg)
        dgpb = dgp.astype(BF16)
        dgp_ref[...] = dgpb
        dz = d_s * sg + _nt(dgpb, wg)
        dy_ref[...] = dz * _gelu_grad(yv, t)
        dga_ref[...] = dga_ref[...] + _colsum(ga_term)
        dgs_ref[...] = dgs_ref[...] + _colsum(gs_term)
        db_ref[...] = db_ref[...] + _colsum(dgp)

    row = lambda i: (i, 0)
    const = lambda i: (0, 0)
    vec = jax.ShapeDtypeStruct((1, SSM_W), F32)
    return pl.pallas_call(
        body, name="mix_bwd", grid=(t_len // tm,),
        in_specs=[pl.BlockSpec((tm, D_MODEL), row), pl.BlockSpec((tm, ATTN_W), row), pl.BlockSpec((tm, SSM_W), row),
                  pl.BlockSpec((SSM_W, SSM_W), const), pl.BlockSpec((1, SSM_W), const), pl.BlockSpec((1, ATTN_W), const),
                  pl.BlockSpec((1, SSM_W), const), pl.BlockSpec((D_MODEL, D_MODEL), const)],
        out_specs=[pl.BlockSpec((tm, ATTN_W), row), pl.BlockSpec((tm, SSM_W), row), pl.BlockSpec((tm, D_MODEL), row),
                   pl.BlockSpec((tm, SSM_W), row), pl.BlockSpec((1, ATTN_W), const), pl.BlockSpec((1, SSM_W), const),
                   pl.BlockSpec((1, SSM_W), const)],
        out_shape=[jax.ShapeDtypeStruct((t_len, ATTN_W), F32), jax.ShapeDtypeStruct((t_len, SSM_W), F32),
                   jax.ShapeDtypeStruct((t_len, D_MODEL), BF16), jax.ShapeDtypeStruct((t_len, SSM_W), BF16), vec, vec, vec],
        compiler_params=_params("arbitrary"),
    )(dx2, attn, y, wg, glu_b, ga, gs, wo)


def _inproj_bwd(dqs, dkn, dv, du, q_raw, k_raw, x, dx2, wi, g1, gq, gk, ones64):
    t_len = x.shape[0]
    tm = 512
    n_heads = ATTN_W // HEAD

    def body(dqs_ref, dkn_ref, dv_ref, du_ref, q_ref, k_ref, x_ref, dx2_ref, wi_ref, g1_ref, gq_ref, gk_ref, bd_ref,
             gx_ref, dproj_ref, dg1_ref, dgq_ref, dgk_ref, accq, acck):
        i = pl.program_id(0)

        @pl.when(i == 0)
        def _():
            dg1_ref[...] = jnp.zeros_like(dg1_ref)
            accq[...] = jnp.zeros_like(accq)
            acck[...] = jnp.zeros_like(acck)

        bd = bd_ref[...]

        def head_norm_bwd(dy, raw, gain, acc):
            r = lax.rsqrt(_group_mean(raw * raw, bd, HEAD) + EPS)
            xh = raw * r
            dxh = dy * gain
            acc[...] = acc[...] + _colsum(dy * xh)
            return r * (dxh - xh * _group_mean(dxh * xh, bd, HEAD))

        dq = head_norm_bwd(dqs_ref[...] * (HEAD ** -0.5), q_ref[...], gq_ref[...], accq)
        dk = head_norm_bwd(dkn_ref[...], k_ref[...], gk_ref[...], acck)
        dproj_ref[:, 0:ATTN_W] = dq.astype(BF16)
        dproj_ref[:, ATTN_W:2 * ATTN_W] = dk.astype(BF16)
        dproj_ref[:, 2 * ATTN_W:3 * ATTN_W] = dv_ref[...].astype(BF16)
        dproj_ref[:, 3 * ATTN_W:] = du_ref[...].astype(BF16)
        dxn = _nt(dproj_ref[...], wi_ref[...])
        xv = x_ref[...]
        g1v = g1_ref[...]
        dx, g1_term = _rms_bwd(dxn, xv, _rms(xv), g1v)
        gx_ref[...] = dx2_ref[...] + dx
        dg1_ref[...] = dg1_ref[...] + _colsum(g1_term)

        @pl.when(i == pl.num_programs(0) - 1)
        def _():
            for acc, out in ((accq, dgq_ref), (acck, dgk_ref)):
                tot = acc[:, 0:HEAD]
                for h in range(1, n_heads):
                    tot = tot + acc[:, h * HEAD:(h + 1) * HEAD]
                out[...] = tot

    row = lambda i: (i, 0)
    const = lambda i: (0, 0)
    aw = pl.BlockSpec((tm, ATTN_W), row)
    dm = pl.BlockSpec((tm, D_MODEL), row)
    return pl.pallas_call(
        body, name="inproj_bwd", grid=(t_len // tm,),
        in_specs=[aw, aw, aw, aw, aw, aw, dm, dm, pl.BlockSpec((D_MODEL, PROJ_W), const), pl.BlockSpec((1, D_MODEL), const),
                  pl.BlockSpec((1, ATTN_W), const), pl.BlockSpec((1, ATTN_W), const), pl.BlockSpec((ATTN_W, ATTN_W), const)],
        out_specs=[dm, pl.BlockSpec((tm, PROJ_W), row), pl.BlockSpec((1, D_MODEL), const),
                   pl.BlockSpec((1, HEAD), const), pl.BlockSpec((1, HEAD), const)],
        out_shape=[jax.ShapeDtypeStruct((t_len, D_MODEL), F32), jax.ShapeDtypeStruct((t_len, PROJ_W), BF16),
                   jax.ShapeDtypeStruct((1, D_MODEL), F32), jax.ShapeDtypeStruct((1, HEAD), F32),
                   jax.ShapeDtypeStruct((1, HEAD), F32)],
        scratch_shapes=[pltpu.VMEM((1, ATTN_W), F32), pltpu.VMEM((1, ATTN_W), F32)],
        compiler_params=_params("arbitrary"),
    )(dqs, dkn, dv, du, q_raw, k_raw, x, dx2, wi, g1, gq, gk, ones64)


def _grad_matmul(a, b, name):
    t_len, m = a.shape
    n = b.shape[1]
    bm, bn, bt = min(m, 1024), min(n, 1024), 1024

    def body(a_ref, b_ref, o_ref):
        @pl.when(pl.program_id(2) == 0)
        def _():
            o_ref[...] = jnp.zeros_like(o_ref)

        o_ref[...] = o_ref[...] + _tn(a_ref[...], b_ref[...])

    return pl.pallas_call(
        body, name=name, grid=(m // bm, n // bn, t_len // bt),
        in_specs=[pl.BlockSpec((bt, bm), lambda i, j, k: (k, i)), pl.BlockSpec((bt, bn), lambda i, j, k: (k, j))],
        out_specs=pl.BlockSpec((bm, bn), lambda i, j, k: (i, j)),
        out_shape=jax.ShapeDtypeStruct((m, n), F32),
        compiler_params=_params("arbitrary", "arbitrary", "arbitrary"),
    )(a, b)


def _adamw(w, g, m, v, name):
    rows, cols = w.shape
    br = _row_block(rows, 256)

    def body(w_ref, g_ref, m_ref, v_ref, d_o, m_o, v_o):
        gv = g_ref[...]
        mn = ADAM_B1 * m_ref[...] + (1.0 - ADAM_B1) * gv
        vn = ADAM_B2 * v_ref[...] + (1.0 - ADAM_B2) * jnp.square(gv)
        m_hat = mn / (1.0 - ADAM_B1 ** ADAM_STEP)
        v_hat = vn / (1.0 - ADAM_B2 ** ADAM_STEP)
        d_o[...] = -ADAM_LR * (m_hat / (jnp.sqrt(v_hat) + ADAM_EPS) + ADAM_WD * w_ref[...])
        m_o[...] = mn
        v_o[...] = vn

    spec = pl.BlockSpec((br, cols), lambda i: (i, 0))
    shape = jax.ShapeDtypeStruct((rows, cols), F32)
    return pl.pallas_call(
        body, name=name, grid=(rows // br,), in_specs=[spec] * 4, out_specs=[spec] * 3, out_shape=[shape] * 3,
        compiler_params=_params("arbitrary"),
    )(w, g, m, v)


def _sum_arrays(arrs, name, out_dtype=F32):
    rows, cols = arrs[0].shape
    br = _row_block(rows, 512)
    n = len(arrs)

    def body(*refs):
        tot = refs[0][...]
        for r in refs[1:n]:
            tot = tot + r[...]
        refs[n][...] = tot.astype(out_dtype)

    spec = pl.BlockSpec((br, cols), lambda i: (i, 0))
    return pl.pallas_call(
        body, name=name, grid=(rows // br,), in_specs=[spec] * n, out_specs=spec,
        out_shape=jax.ShapeDtypeStruct((rows, cols), out_dtype), compiler_params=_params("arbitrary"),
    )(*arrs)


GPL = N_GROUPS // N_LB
SW = GPL * N_STATE


def _eye_groups():
    return jnp.eye(GPL, dtype=F32)


def _s5_matrices(ab_r, ab_i, bb_r, bb_i, c_re, c_im, d_skip):
    eye = _eye_groups()
    a_cat = jnp.concatenate([ab_r.reshape(N_LB, HALF, LANES), ab_i.reshape(N_LB, HALF, LANES)], axis=1)

    def b_part(bb):
        b4 = jnp.transpose(bb.reshape(N_LB, GPL, N_STATE, GROUP_W), (0, 1, 3, 2))
        return (b4[:, :, :, None, :] * eye[None, :, None, :, None]).reshape(N_LB, LANES, SW)

    def c_part(cc):
        c4 = jnp.transpose(cc.reshape(N_LB, GPL, GROUP_W, N_STATE), (0, 1, 3, 2))
        return (c4[:, :, :, None, :] * eye[None, :, None, :, None]).reshape(N_LB, SW, LANES)

    b_mat = jnp.concatenate([b_part(bb_r), b_part(bb_i)], axis=2).astype(BF16)
    c_mat = jnp.concatenate([c_part(c_re), -c_part(c_im)], axis=1).astype(BF16)
    return a_cat, b_mat, c_mat, d_skip.reshape(N_LB, 1, LANES)


def _s5_unpack_grads(db, dc, da, dd):
    eye = _eye_groups()
    mask = eye[None, :, None, None, :, None]
    d6 = jnp.sum(db.reshape(N_LB, GPL, GROUP_W, 2, GPL, N_STATE) * mask, axis=4)
    dbb = jnp.transpose(d6, (3, 0, 1, 4, 2)).reshape(2, N_GROUPS * N_STATE, GROUP_W)
    c6 = jnp.sum(dc.reshape(N_LB, GPL, GROUP_W, 2, GPL, N_STATE) * mask, axis=4)
    dcc = jnp.transpose(c6, (3, 0, 1, 2, 4)).reshape(2, N_GROUPS, GROUP_W, N_STATE)
    dab_r = da[:, :HALF].reshape(N_GROUPS * N_STATE, 1)
    dab_i = da[:, HALF:].reshape(N_GROUPS * N_STATE, 1)
    return dab_r, dab_i, dbb[0], dbb[1], dcc[0], -dcc[1], dd.reshape(N_GROUPS, GROUP_W)


def _block_ones(n, width):
    i = lax.broadcasted_iota(jnp.int32, (n, n), 0) // width
    j = lax.broadcasted_iota(jnp.int32, (n, n), 1) // width
    return (i == j).astype(BF16)


def _tile_heads(g):
    return jnp.tile(g.reshape(1, HEAD), (1, ATTN_W // HEAD))


def _local_step(x, target, wi, rest, p, fwd_side=None, bwd_side=None):
    ones64 = _block_ones(ATTN_W, HEAD)
    ones_hp = _block_ones(LANES, HEAD)
    g1 = p["norm1_g"].reshape(1, D_MODEL)
    g2 = p["norm2_g"].reshape(1, D_MODEL)
    gq = _tile_heads(p["q_norm_g"])
    gk = _tile_heads(p["k_norm_g"])
    ga = p["attn_out_norm_g"].reshape(1, ATTN_W)
    gs = p["ssm_out_norm_g"].reshape(1, SSM_W)
    glu_b = p["glu_b"].reshape(1, SSM_W)
    n_gp = N_GROUPS * N_STATE
    lr = p["ssm_a_re"].reshape(n_gp, 1)
    li = p["ssm_a_im"].reshape(n_gp, 1)
    ldt = jnp.repeat(p["ssm_log_dt"].reshape(N_GROUPS), N_STATE).reshape(n_gp, 1)
    br = p["ssm_b_re"].reshape(n_gp, GROUP_W)
    bi = p["ssm_b_im"].reshape(n_gp, GROUP_W)
    ab_r, ab_i, bb_r, bb_i = _disc_fwd(lr, li, ldt, br, bi)
    a_cat, b_mat, c_mat, d_mat = _s5_matrices(
        ab_r, ab_i, bb_r, bb_i, p["ssm_c_re"].reshape(N_GROUPS, GROUP_W, N_STATE),
        p["ssm_c_im"].reshape(N_GROUPS, GROUP_W, N_STATE), p["ssm_d"])

    xn, qn, kn, vv, u, q_raw, k_raw = _inproj_fwd(x, g1, wi, gq, gk, ones64)
    if fwd_side is None:
        attn, lse = _attn_fwd(qn, kn, vv)
    else:
        attn, lse, *rest = _attn_fwd(qn, kn, vv, *fwd_side)
    wg, wo, wu, wd = rest
    y, states = _s5_fwd(u, a_cat, b_mat, c_mat, d_mat)
    x2, mix, z = _mix_fwd(attn, y, x, wg, glu_b, ga, gs, wo)
    dx2, hdn, dup, h, dyb, dg2, loss = _mlp(x2, target, g2, wu, wd)
    big = {"w_mlp_up": _grad_matmul(h, dup, "grad_w_mlp_up"), "w_mlp_down": _grad_matmul(hdn, dyb, "grad_w_mlp_down")}
    d_attn, dy_ssm, dx2b, dgp, dga, dgs, dglu_b = _mix_bwd(dx2, attn, y, wg, glu_b, ga, gs, wo)
    rode = []
    if bwd_side is None:
        dqs, dkn, dvv = _attn_bwd(qn, kn, vv, attn, d_attn, lse, ones_hp)
    else:
        dqs, dkn, dvv, *rode = _attn_bwd(qn, kn, vv, attn, d_attn, lse, ones_hp, *bwd_side(big))
    du, db, dc, da, dd = _s5_bwd(u, dy_ssm, states, a_cat, b_mat, c_mat, d_mat)
    grad_x, dproj, dg1, dgq, dgk = _inproj_bwd(dqs, dkn, dvv, du, q_raw, k_raw, x, dx2, wi, g1, gq, gk, ones64)

    big["w_in"] = _grad_matmul(xn, dproj, "grad_w_in")
    big["glu_w"] = _grad_matmul(z, dgp, "grad_glu_w")
    big["w_out"] = _grad_matmul(mix, dx2b, "grad_w_out")
    dab_r, dab_i, dbb_r, dbb_i, dc_re, dc_im, dd_g = _s5_unpack_grads(db, dc, da, dd)
    cot = {"norm1_g": dg1, "q_norm_g": dgq, "k_norm_g": dgk, "ab_r": dab_r, "ab_i": dab_i, "bb_r": dbb_r, "bb_i": dbb_i,
           "ssm_c_re": dc_re, "ssm_c_im": dc_im, "ssm_d": dd_g, "glu_b": dglu_b, "attn_out_norm_g": dga,
           "ssm_out_norm_g": dgs, "norm2_g": dg2}
    return loss[0, 0], grad_x, big, cot, (lr, li, ldt, br, bi), rode


COT_NAMES = ("norm1_g", "q_norm_g", "k_norm_g", "ab_r", "ab_i", "bb_r", "bb_i", "ssm_c_re", "ssm_c_im", "ssm_d",
             "glu_b", "attn_out_norm_g", "ssm_out_norm_g", "norm2_g")
SMALL_NAMES = ("norm1_g", "q_norm_g", "k_norm_g", "ssm_a_re", "ssm_a_im", "ssm_log_dt", "ssm_b_re", "ssm_b_im",
               "ssm_c_re", "ssm_c_im", "ssm_d", "glu_b", "attn_out_norm_g", "ssm_out_norm_g", "norm2_g")
BIG_NAMES = ("w_in", "glu_w", "w_out", "w_mlp_up", "w_mlp_down")
PACK_ROWS = 1152


def _pack(arrs):
    flat = jnp.concatenate([a.reshape(-1) for a in arrs])
    return jnp.pad(flat, (0, PACK_ROWS * LANES - flat.shape[0])).reshape(PACK_ROWS, LANES)


def _unpack(packed, like):
    flat = packed.reshape(-1)
    out, pos = [], 0
    for a in like:
        out.append(flat[pos:pos + a.size].reshape(a.shape))
        pos += a.size
    return out


def _small_grads(cot, disc_in, p):
    lr, li, ldt, br, bi = disc_in
    group_sum = (lax.broadcasted_iota(jnp.int32, (N_GROUPS, N_GROUPS * N_STATE), 1) // N_STATE
                 == lax.broadcasted_iota(jnp.int32, (N_GROUPS, N_GROUPS * N_STATE), 0)).astype(F32)
    dlr, dli, dldt, dbr, dbi = _disc_bwd(lr, li, ldt, br, bi, cot["ab_r"], cot["ab_i"], cot["bb_r"], cot["bb_i"], group_sum)
    g = dict(cot)
    g.update(ssm_a_re=dlr, ssm_a_im=dli, ssm_log_dt=dldt[:, 0], ssm_b_re=dbr, ssm_b_im=dbi)
    return {n: g[n].reshape(p[n].shape) for n in SMALL_NAMES}


BIG = {
    "w_in": ((D_MODEL, PROJ_W), 1, PROJ_W // 4, 0, D_MODEL // 2),
    "glu_w": ((SSM_W, SSM_W), 0, SSM_W // 4, 1, SSM_W // 2),
    "w_out": ((D_MODEL, D_MODEL), 0, D_MODEL // 4, 1, D_MODEL // 2),
    "w_mlp_up": ((D_MODEL, D_FF), 1, D_FF // 4, 0, D_MODEL // 2),
    "w_mlp_down": ((D_FF, D_MODEL), 0, D_FF // 4, 1, D_MODEL // 2),
}
N_BIG = len(BIG_NAMES)
N_CHIPS = 4
ANY = pl.BlockSpec(memory_space=pl.ANY)


def _cut(name, shard=False, half=False):
    shape, s_ax, s_sz, h_ax, h_sz = BIG[name]
    shape = list(shape)
    if shard:
        shape[s_ax] = s_sz
    if half:
        shape[h_ax] = h_sz
    return tuple(shape)


def _window(name, base, shard=None, half=None):
    _, s_ax, s_sz, h_ax, h_sz = BIG[name]
    idx = [pl.ds(0, base[0]), pl.ds(0, base[1])]
    if shard is not None:
        idx[s_ax] = pl.ds(pl.multiple_of(shard * s_sz, s_sz), s_sz)
    if half is not None:
        idx[h_ax] = pl.ds(pl.multiple_of(half * h_sz, h_sz), h_sz)
    return tuple(idx)


def _mesh_pos():
    return lax.axis_index("x"), lax.axis_index("y"), lax.axis_index("c")


def _other_chips(x, y):
    return [(1 - x, y, 2 * (1 - x) + y), (x, 1 - y, 2 * x + 1 - y), (1 - x, 1 - y, 2 * (1 - x) + 1 - y)]


def _remote(src, dst, send_sem, recv_sem, dev):
    return pltpu.make_async_remote_copy(src_ref=src, dst_ref=dst, send_sem=send_sem, recv_sem=recv_sem,
                                        device_id=dev, device_id_type=MESH)


def _start_remote(src, dst, send_sem, recv_sem, dev):
    cp = _remote(src, dst, send_sem, recv_sem, dev)
    cp.start()
    return cp


class _Gather:
    def __init__(self, names):
        self.names = tuple(names)
        self.n = len(self.names)

    def in_specs(self):
        return [pl.BlockSpec(memory_space=pltpu.VMEM)] * self.n

    def out_specs(self):
        return [ANY] * self.n

    def out_shapes(self):
        return [jax.ShapeDtypeStruct(BIG[w][0], BF16) for w in self.names]

    def scratch_shapes(self):
        n_sem = (N_CHIPS - 1) * self.n
        return ([pltpu.VMEM(_cut(w, shard=True), BF16) for w in self.names]
                + [pltpu.SemaphoreType.DMA((n_sem,))] * 4 + [pltpu.SemaphoreType.DMA((self.n,))])

    def bind(self, ins, outs, scratch):
        self.ins, self.outs = ins, outs
        self.stage = scratch[:self.n]
        self.send, self.recv, self.fsend, self.frecv, self.lsem = scratch[self.n:]

    def _copies(self):
        x, y, c = _mesh_pos()
        me = 2 * x + y
        sib = (x, y, 1 - c)
        local, sends, lands, fwds, flands = [], [], [], [], []
        for w, n in enumerate(self.names):
            local.append(pltpu.make_async_copy(self.stage[w], self.outs[w].at[_window(n, BIG[n][0], shard=me)], self.lsem.at[w]))
        for k, (px, py, pj) in enumerate(_other_chips(x, y)):
            for w, n in enumerate(self.names):
                s = k * self.n + w
                sends.append(_remote(self.stage[w].at[_window(n, _cut(n, shard=True), half=c)],
                                     self.outs[w].at[_window(n, BIG[n][0], shard=me, half=c)],
                                     self.send.at[s], self.recv.at[s], (px, py, c)))
                got = self.outs[w].at[_window(n, BIG[n][0], shard=pj, half=c)]
                lands.append(_remote(got, got, self.send.at[s], self.recv.at[s], (px, py, c)))
                fwds.append(_remote(got, got, self.fsend.at[s], self.frecv.at[s], sib))
                theirs = self.outs[w].at[_window(n, BIG[n][0], shard=pj, half=1 - c)]
                flands.append(_remote(theirs, theirs, self.fsend.at[s], self.frecv.at[s], sib))
        return local, sends, lands, fwds, flands

    def start(self):
        for w in range(self.n):
            self.stage[w][...] = self.ins[w][...].astype(BF16)
        local, sends, _, _, _ = self._copies()
        for cp in local + sends:
            cp.start()

    def forward(self):
        _, _, lands, fwds, _ = self._copies()
        for land, fwd in zip(lands, fwds):
            land.wait_recv()
            fwd.start()

    def finish(self):
        local, sends, _, fwds, flands = self._copies()
        for cp in flands:
            cp.wait_recv()
        for cp in sends + fwds:
            cp.wait_send()
        for cp in local:
            cp.wait()


def _gather_weights(shards, names):
    g = _Gather(names)

    def body(*refs):
        g.bind(refs[0:g.n], refs[g.n:2 * g.n], refs[2 * g.n:])
        g.start()
        g.forward()
        g.finish()

    return pl.pallas_call(
        body, name="gather_" + "_".join(names), in_specs=g.in_specs(), out_specs=g.out_specs(), out_shape=g.out_shapes(),
        scratch_shapes=g.scratch_shapes(), compiler_params=pltpu.CompilerParams(vmem_limit_bytes=VMEM_LIMIT),
    )(*[shards[n] for n in names])


def _pair_exchange(grads, names, packed=None):
    n_big = len(names)
    n_all = n_big + (packed is not None)

    def body(*refs):
        ins, got = refs[0:n_all], refs[n_all:2 * n_all]
        send, recv = refs[2 * n_all:]
        x, y, c = _mesh_pos()
        sib = (x, y, 1 - c)
        copies = []
        for w, n in enumerate(names):
            copies.append(_start_remote(ins[w].at[_window(n, BIG[n][0], half=1 - c)], got[w], send.at[w], recv.at[w], sib))
        if packed is not None:
            copies.append(_start_remote(ins[n_big], got[n_big], send.at[n_big], recv.at[n_big], sib))
        for cp in copies:
            cp.wait()

    shapes = [jax.ShapeDtypeStruct(_cut(n, half=True), F32) for n in names]
    args = [grads[n] for n in names]
    if packed is not None:
        shapes.append(jax.ShapeDtypeStruct(packed.shape, F32))
        args.append(packed)
    return pl.pallas_call(
        body, name="grad_pair_exchange_" + "_".join(names), in_specs=[ANY] * n_all, out_specs=[ANY] * n_all, out_shape=shapes,
        scratch_shapes=[pltpu.SemaphoreType.DMA((n_all,)), pltpu.SemaphoreType.DMA((n_all,))],
    )(*args)


def _pair_sum(name, full, got, core):
    _, _, _, h_ax, _ = BIG[name]
    rows, cols = _cut(name, half=True)
    br = _row_block(rows, 512)
    nb = rows // br
    own_map = (lambda i, c: (i + c[0] * nb, 0)) if h_ax == 0 else (lambda i, c: (i, c[0]))

    def body(c_ref, own_ref, got_ref, o_ref):
        o_ref[...] = (own_ref[...] + got_ref[...]).astype(BF16)

    plain = pl.BlockSpec((br, cols), lambda i, c: (i, 0))
    return pl.pallas_call(
        body, name="pair_sum_" + name,
        grid_spec=pltpu.PrefetchScalarGridSpec(num_scalar_prefetch=1, grid=(nb,),
                                               in_specs=[pl.BlockSpec((br, cols), own_map), plain], out_specs=plain),
        out_shape=jax.ShapeDtypeStruct((rows, cols), BF16), compiler_params=_params("arbitrary"),
    )(core, full, got)


class _ChipExchange:
    def __init__(self, names, packed_shape=None):
        self.names = tuple(names)
        self.packed_shape = packed_shape
        self.n = len(self.names) + (packed_shape is not None)

    def in_specs(self):
        return [ANY] * self.n

    def out_specs(self):
        return [ANY] * self.n

    def out_shapes(self):
        shapes = [jax.ShapeDtypeStruct((N_CHIPS,) + _cut(w, shard=True, half=True), BF16) for w in self.names]
        if self.packed_shape is not None:
            shapes.append(jax.ShapeDtypeStruct((N_CHIPS,) + tuple(self.packed_shape), F32))
        return shapes

    def scratch_shapes(self):
        n_sem = (N_CHIPS - 1) * self.n
        return [pltpu.SemaphoreType.DMA((n_sem,)), pltpu.SemaphoreType.DMA((n_sem,))]

    def bind(self, ins, outs, scratch):
        self.ins, self.outs = ins, outs
        self.send, self.recv = scratch

    def _piece(self, w, shard):
        if w >= len(self.names):
            return self.ins[w]
        n = self.names[w]
        return self.ins[w].at[_window(n, _cut(n, half=True), shard=shard)]

    def _copies(self):
        x, y, c = _mesh_pos()
        me = 2 * x + y
        sends, lands = [], []
        for k, (px, py, pj) in enumerate(_other_chips(x, y)):
            for w in range(self.n):
                s = k * self.n + w
                sends.append(_remote(self._piece(w, pj), self.outs[w].at[me], self.send.at[s], self.recv.at[s], (px, py, c)))
                lands.append(_remote(self._piece(w, me), self.outs[w].at[pj], self.send.at[s], self.recv.at[s], (px, py, c)))
        return sends, lands

    def start(self):
        for cp in self._copies()[0]:
            cp.start()

    def finish(self):
        sends, lands = self._copies()
        for cp in lands:
            cp.wait_recv()
        for cp in sends:
            cp.wait_send()


def _chip_exchange(halves, packed, names):
    ex = _ChipExchange(names, packed.shape)

    def body(*refs):
        ex.bind(refs[0:ex.n], refs[ex.n:2 * ex.n], refs[2 * ex.n:])
        ex.start()
        ex.finish()

    return pl.pallas_call(
        body, name="grad_chip_exchange", in_specs=ex.in_specs(), out_specs=ex.out_specs(), out_shape=ex.out_shapes(),
        scratch_shapes=ex.scratch_shapes(),
    )(*halves, packed)


def _chip_sum(name, own, slots, chip):
    n_slot, rows, cols = slots.shape
    br = _row_block(rows, 512)
    nb = rows // br
    if name in BIG and BIG[name][1] == 1:
        own_map = lambda i, m: (i, m[0])
    elif name in BIG:
        own_map = lambda i, m: (i + m[0] * nb, 0)
    else:
        own_map = lambda i, m: (i, 0)

    def slot_map(j):
        return lambda i, m: (jnp.where(m[0] == j, (j + 1) % n_slot, j), i, 0)

    def body(m_ref, own_ref, *refs):
        own_blk = own_ref[...].astype(F32)
        tot = None
        for j in range(n_slot):
            term = jnp.where(m_ref[0] == j, own_blk, refs[j][...].astype(F32))
            tot = term if tot is None else tot + term
        refs[n_slot][...] = tot

    in_specs = [pl.BlockSpec((br, cols), own_map)] + [pl.BlockSpec((None, br, cols), slot_map(j)) for j in range(n_slot)]
    return pl.pallas_call(
        body, name="chip_sum_" + name,
        grid_spec=pltpu.PrefetchScalarGridSpec(num_scalar_prefetch=1, grid=(nb,), in_specs=in_specs,
                                               out_specs=pl.BlockSpec((br, cols), lambda i, m: (i, 0))),
        out_shape=jax.ShapeDtypeStruct((rows, cols), F32), compiler_params=_params("arbitrary"),
    )(chip, own, *([slots] * n_slot))


def _half_exchange(pieces):
    def body(*refs):
        ins, outs = refs[0:N_BIG], refs[N_BIG:2 * N_BIG]
        send, recv = refs[2 * N_BIG:]
        x, y, c = _mesh_pos()
        sib = (x, y, 1 - c)
        copies = []
        for w, n in enumerate(BIG_NAMES):
            copies.append(_start_remote(ins[w], outs[w], send.at[w], recv.at[w], sib))
        for cp in copies:
            cp.wait()

    return pl.pallas_call(
        body, name="grad_half_exchange", in_specs=[ANY] * N_BIG, out_specs=[ANY] * N_BIG,
        out_shape=[jax.ShapeDtypeStruct(_cut(n, shard=True, half=True), F32) for n in BIG_NAMES],
        scratch_shapes=[pltpu.SemaphoreType.DMA((N_BIG,)), pltpu.SemaphoreType.DMA((N_BIG,))],
    )(*pieces)


WEIGHT_NAMES = ("norm1_g", "w_in", "q_norm_g", "k_norm_g", "ssm_a_re", "ssm_a_im", "ssm_log_dt", "ssm_b_re", "ssm_b_im",
                "ssm_c_re", "ssm_c_im", "ssm_d", "glu_w", "glu_b", "attn_out_norm_g", "ssm_out_norm_g", "w_out", "norm2_g",
                "w_mlp_up", "w_mlp_down")


def _train_step(a):
    x = a["x"][0]
    target = a["loss_target"][0]
    shards = {n: a[n][0] for n in BIG_NAMES}
    p = {n: a[n][0] for n in SMALL_NAMES}
    core = lax.axis_index("c").astype(jnp.int32).reshape(1)
    chip_id = (2 * lax.axis_index("x") + lax.axis_index("y")).astype(jnp.int32).reshape(1)

    later = ("glu_w", "w_out", "w_mlp_up", "w_mlp_down")
    early = ("w_mlp_up", "w_mlp_down")
    late = ("w_in", "glu_w", "w_out")
    (wi,) = _gather_weights(shards, ("w_in",))
    chip = {}

    def bwd_side(grads):
        got = _pair_exchange(grads, early)
        for n, g in zip(early, got):
            chip[n] = _pair_sum(n, grads[n], g, core)
        return _ChipExchange(early), [chip[n] for n in early]

    loss, grad_x, big, cot, disc_in, early_slots = _local_step(
        x, target, wi, None, p, fwd_side=(_Gather(later), [shards[n] for n in later]), bwd_side=bwd_side)
    slots = dict(zip(early, early_slots))

    cot_list = [cot[n] for n in COT_NAMES]
    packed = _pack(cot_list)
    *got, got_packed = _pair_exchange(big, late, packed)
    for n, g in zip(late, got):
        chip[n] = _pair_sum(n, big[n], g, core)
    chip_packed = _sum_arrays([packed, got_packed], "pair_sum_small")
    *late_slots, small_slots = _chip_exchange([chip[n] for n in late], chip_packed, late)
    slots.update(zip(late, late_slots))
    pieces = [_chip_sum(n, chip[n], slots[n], chip_id) for n in BIG_NAMES]
    small_sum = _chip_sum("small", chip_packed, small_slots, chip_id)
    shard_grads = {}
    for n, mine, theirs in zip(BIG_NAMES, pieces, _half_exchange(pieces)):
        h_ax = BIG[n][3]
        shard_grads[n] = jnp.where(core[0] == 0, jnp.concatenate([mine, theirs], axis=h_ax),
                                   jnp.concatenate([theirs, mine], axis=h_ax))
    small_grads = _small_grads(dict(zip(COT_NAMES, _unpack(small_sum, cot_list))), disc_in, p)

    grads, delta, new_m, new_v = {}, {}, {}, {}
    for n in BIG_NAMES:
        grads[n] = shard_grads[n]
        delta[n], new_m[n], new_v[n] = _adamw(a[n][0], grads[n], a["m_" + n][0], a["v_" + n][0], "adamw_" + n)
    small_w = [p[n] for n in SMALL_NAMES]
    res = _adamw(_pack(small_w), _pack([small_grads[n] for n in SMALL_NAMES]), _pack([a["m_" + n][0] for n in SMALL_NAMES]),
                 _pack([a["v_" + n][0] for n in SMALL_NAMES]), "adamw_small")
    for store, packed_out in zip((delta, new_m, new_v), res):
        store.update(zip(SMALL_NAMES, _unpack(packed_out, small_w)))
    grads.update(small_grads)

    total = lax.psum(loss, ("x", "y", "c"))
    out = [total, grad_x[None]]
    for store in (grads, delta, new_m, new_v):
        out += [store[n].reshape(a[n].shape) for n in WEIGHT_NAMES]
    return tuple(out)


def kernel(x, norm1_g, w_in, q_norm_g, k_norm_g, ssm_a_re, ssm_a_im, ssm_log_dt, ssm_b_re, ssm_b_im, ssm_c_re, ssm_c_im, ssm_d, glu_w, glu_b, attn_out_norm_g, ssm_out_norm_g, w_out, norm2_g, w_mlp_up, w_mlp_down, loss_target, m_norm1_g, m_w_in, m_q_norm_g, m_k_norm_g, m_ssm_a_re, m_ssm_a_im, m_ssm_log_dt, m_ssm_b_re, m_ssm_b_im, m_ssm_c_re, m_ssm_c_im, m_ssm_d, m_glu_w, m_glu_b, m_attn_out_norm_g, m_ssm_out_norm_g, m_w_out, m_norm2_g, m_w_mlp_up, m_w_mlp_down, v_norm1_g, v_w_in, v_q_norm_g, v_k_norm_g, v_ssm_a_re, v_ssm_a_im, v_ssm_log_dt, v_ssm_b_re, v_ssm_b_im, v_ssm_c_re, v_ssm_c_im, v_ssm_d, v_glu_w, v_glu_b, v_attn_out_norm_g, v_ssm_out_norm_g, v_w_out, v_norm2_g, v_w_mlp_up, v_w_mlp_down):
    return _train_step(dict(locals()))
```

```python
import functools
import math

import jax
import jax.numpy as jnp
from jax import lax
from jax.experimental import pallas as pl
from jax.experimental.pallas import tpu as pltpu

F32 = jnp.float32
BF16 = jnp.bfloat16
MESH = pl.DeviceIdType.MESH

D_MODEL = 1024
ATTN_W = 512
SSM_W = 512
HEAD = 64
D_FF = 4096
PROJ_W = 2048
N_GROUPS = 32
N_STATE = 64
GROUP_W = 16
EPS = 1e-6
NEG = -1e30
DILATIONS = (1, 4, 16)
BLK = 128
TILE = 2048
LANES = 128
N_LB = SSM_W // LANES
N_SLAB = 2 * N_LB * N_STATE * 8 // LANES // N_LB
VMEM_LIMIT = 56 * 1024 * 1024

ADAM_LR, ADAM_B1, ADAM_B2, ADAM_EPS, ADAM_WD, ADAM_STEP = 0.001, 0.9, 0.999, 1e-08, 0.01, 10


def _params(*sem):
    return pltpu.CompilerParams(dimension_semantics=sem, vmem_limit_bytes=VMEM_LIMIT)


def _nt(a, b):
    return lax.dot_general(a, b, (((1,), (1,)), ((), ())), preferred_element_type=F32)


def _tn(a, b):
    return lax.dot_general(a, b, (((0,), (0,)), ((), ())), preferred_element_type=F32)


def _mm(a, b):
    return jnp.dot(a, b, preferred_element_type=F32)


def _group_mean(t, ones_bd, width):
    hi = t.astype(BF16)
    lo = (t - hi.astype(F32)).astype(BF16)
    return (_mm(hi, ones_bd) + _mm(lo, ones_bd)) * (1.0 / width)


def _rms(x):
    return lax.rsqrt(jnp.mean(x * x, axis=-1, keepdims=True) + EPS)


def _rms_bwd(dy, x, r, g):
    xh = x * r
    dxh = dy * g
    dx = r * (dxh - xh * jnp.mean(dxh * xh, axis=-1, keepdims=True))
    return dx, dy * xh


def _colsum(x):
    return jnp.sum(x, axis=0, keepdims=True)


def _row_block(rows, cap):
    for b in range(min(rows, cap) // 8 * 8, 0, -8):
        if rows % b == 0:
            return b
    raise ValueError(f"no row block for {rows} rows")


def _inproj_fwd(x, g1, wi, gq, gk, ones64):
    t_len = x.shape[0]
    tm = 512
    n_hp = ATTN_W // LANES

    def body(x_ref, g1_ref, wi_ref, gq_ref, gk_ref, bd_ref, xn_ref, q_ref, k_ref, v_ref, u_ref, qr_ref, kr_ref):
        xv = x_ref[...]
        xn = (xv * _rms(xv) * g1_ref[...]).astype(BF16)
        xn_ref[...] = xn
        proj = _mm(xn, wi_ref[...])
        q = proj[:, 0:ATTN_W]
        k = proj[:, ATTN_W:2 * ATTN_W]
        v = proj[:, 2 * ATTN_W:3 * ATTN_W]
        u_ref[...] = proj[:, 3 * ATTN_W:]
        qr_ref[...] = q
        kr_ref[...] = k
        bd = bd_ref[...]
        qn = q * lax.rsqrt(_group_mean(q * q, bd, HEAD) + EPS) * gq_ref[...] * (HEAD ** -0.5)
        kn = k * lax.rsqrt(_group_mean(k * k, bd, HEAD) + EPS) * gk_ref[...]
        for hp in range(n_hp):
            sl = slice(hp * LANES, (hp + 1) * LANES)
            q_ref[hp] = qn[:, sl]
            k_ref[hp] = kn[:, sl]
            v_ref[hp] = v[:, sl]

    row = lambda i: (i, 0)
    const = lambda i: (0, 0)
    hp_spec = pl.BlockSpec((n_hp, tm, LANES), lambda i: (0, i, 0))
    hp_shape = jax.ShapeDtypeStruct((n_hp, t_len, LANES), F32)
    return pl.pallas_call(
        body, name="inproj_fwd", grid=(t_len // tm,),
        in_specs=[pl.BlockSpec((tm, D_MODEL), row), pl.BlockSpec((1, D_MODEL), const),
                  pl.BlockSpec((D_MODEL, PROJ_W), const), pl.BlockSpec((1, ATTN_W), const),
                  pl.BlockSpec((1, ATTN_W), const), pl.BlockSpec((ATTN_W, ATTN_W), const)],
        out_specs=[pl.BlockSpec((tm, D_MODEL), row), hp_spec, hp_spec, hp_spec,
                   pl.BlockSpec((tm, SSM_W), row), pl.BlockSpec((tm, ATTN_W), row), pl.BlockSpec((tm, ATTN_W), row)],
        out_shape=[jax.ShapeDtypeStruct((t_len, D_MODEL), BF16), hp_shape, hp_shape, hp_shape,
                   jax.ShapeDtypeStruct((t_len, SSM_W), F32), jax.ShapeDtypeStruct((t_len, ATTN_W), F32),
                   jax.ShapeDtypeStruct((t_len, ATTN_W), F32)],
        compiler_params=_params("arbitrary"),
    )(x, g1, wi, gq, gk, ones64)


def _attn_masks():
    head0 = lax.broadcasted_iota(jnp.int32, (BLK, LANES), 1) < HEAD
    row = lax.broadcasted_iota(jnp.int32, (2 * BLK, 2 * BLK), 0) & (BLK - 1)
    col = lax.broadcasted_iota(jnp.int32, (2 * BLK, 2 * BLK), 1)
    return head0, (col < BLK) & (col >= row), (col >= BLK) & (col - BLK <= row)


def _stack_heads(x, head0):
    return jnp.concatenate([jnp.where(head0, x, 0.0), jnp.where(head0, 0.0, x)], axis=0).astype(BF16)


def _unit_rows(uidx, d):
    nb = TILE // (BLK * d)
    r = lax.div(uidx, nb)
    b = lax.rem(uidx, nb)
    start = r + d * BLK * b
    if d == 1:
        start = pl.multiple_of(start, BLK)
        mk = lambda s: pl.ds(pl.multiple_of(s, BLK), BLK)
    else:
        mk = lambda s: pl.ds(s, BLK, stride=d)
    return b, mk(start), mk(TILE + start), mk(TILE + start - d * BLK)


def _attn_fwd(q, k, v, side=None, side_args=()):
    n_hp, t_len, _ = q.shape
    nt = t_len // TILE
    ns = side.n if side is not None else 0
    n_steps = n_hp * nt

    def body(*refs):
        q_ref, kp_ref, kc_ref, vp_ref, vc_ref = refs[0:5]
        o_ref, lse_ref = refs[5 + ns:7 + ns]
        kk, vv, m_s, l_s, acc_s = refs[7 + 2 * ns:12 + 2 * ns]
        t = pl.program_id(1)
        step = pl.program_id(0) * nt + t
        if side is not None:
            side.bind(refs[5:5 + ns], refs[7 + ns:7 + 2 * ns], refs[12 + 2 * ns:])
            pl.when(step == 0)(side.start)
            pl.when(step == n_steps // 2)(side.forward)
        kk[0:TILE] = kp_ref[0]
        kk[TILE:] = kc_ref[0]
        vv[0:TILE] = vp_ref[0]
        vv[TILE:] = vc_ref[0]
        head0, band_prev, band_cur = _attn_masks()

        for pi, d in enumerate(DILATIONS):
            def unit(uidx, carry, d=d, pi=pi):
                b, rows_q, rows_c, rows_p = _unit_rows(uidx, d)
                mask = band_cur | (band_prev & ((t > 0) | (b > 0)))
                q2 = _stack_heads(q_ref.at[0][rows_q, :], head0)
                kcat = jnp.concatenate([kk[rows_p, :], kk[rows_c, :]], axis=0).astype(BF16)
                vcat = jnp.concatenate([vv[rows_p, :], vv[rows_c, :]], axis=0).astype(BF16)
                s = jnp.where(mask, _nt(q2, kcat), NEG)
                m = jnp.max(s, axis=1, keepdims=True)
                p = jnp.exp(s - m)
                ls = jnp.sum(p, axis=1, keepdims=True)
                pv = _mm(p.astype(BF16), vcat)
                m_s.at[pi][rows_q, :] = jnp.where(head0, m[0:BLK], m[BLK:])
                l_s.at[pi][rows_q, :] = jnp.where(head0, ls[0:BLK], ls[BLK:])
                acc_s.at[pi][rows_q, :] = jnp.where(head0, pv[0:BLK], pv[BLK:])
                return carry

            lax.fori_loop(0, TILE // BLK, unit, 0, unroll=8)

        m_all = jnp.maximum(jnp.maximum(m_s[0], m_s[1]), m_s[2])
        num = jnp.zeros((TILE, LANES), F32)
        den = jnp.zeros((TILE, LANES), F32)
        for pi in range(len(DILATIONS)):
            wgt = jnp.exp(m_s[pi] - m_all)
            num = num + acc_s[pi] * wgt
            den = den + l_s[pi] * wgt
        o_ref[...] = num / den
        lse_ref[0] = m_all + jnp.log(den)
        if side is not None:
            pl.when(step == n_steps - 1)(side.finish)

    cur = lambda hp, t: (hp, t, 0)
    prev = lambda hp, t: (hp, jnp.maximum(t - 1, 0), 0)
    blk = (1, TILE, LANES)
    per_pattern = pltpu.VMEM((len(DILATIONS), TILE, LANES), F32)
    extra = (side.in_specs(), side.out_specs(), side.out_shapes(), side.scratch_shapes()) if side is not None else ([], [], [], [])
    return pl.pallas_call(
        body, name="attn_fwd", grid=(n_hp, nt),
        in_specs=[pl.BlockSpec(blk, cur), pl.BlockSpec(blk, prev), pl.BlockSpec(blk, cur),
                  pl.BlockSpec(blk, prev), pl.BlockSpec(blk, cur)] + extra[0],
        out_specs=[pl.BlockSpec((TILE, LANES), lambda hp, t: (t, hp)), pl.BlockSpec(blk, cur)] + extra[1],
        out_shape=[jax.ShapeDtypeStruct((t_len, ATTN_W), F32), jax.ShapeDtypeStruct((n_hp, t_len, LANES), F32)] + extra[2],
        scratch_shapes=[pltpu.VMEM((2 * TILE, LANES), F32), pltpu.VMEM((2 * TILE, LANES), F32),
                        per_pattern, per_pattern, per_pattern] + extra[3],
        compiler_params=_params("arbitrary", "arbitrary"),
    )(q, k, k, v, v, *side_args)


def _attn_bwd(q, k, v, o, do, lse, ones_hp, side=None, side_args=()):
    n_hp, t_len, _ = q.shape
    nt = t_len // TILE
    ns = side.n if side is not None else 0
    n_pat = len(DILATIONS)

    def body(*refs):
        q_ref, kp_ref, kc_ref, vp_ref, vc_ref, o_ref, do_ref, lse_ref, bd_ref = refs[0:9]
        dq_ref, dk_ref, dv_ref = refs[9 + ns:12 + ns]
        kk, vv, dq_s, dkc, dkp, dvc, dvp, hold_k, hold_v, dl_s = refs[12 + 2 * ns:22 + 2 * ns]
        t = pl.program_id(1)
        if side is not None:
            side.bind(refs[9:9 + ns], refs[12 + ns:12 + 2 * ns], refs[22 + 2 * ns:])
            pl.when((pl.program_id(0) == 0) & (t == 0))(side.start)

        @pl.when(t < nt)
        def _():
            kk[0:TILE] = kp_ref[0]
            kk[TILE:] = kc_ref[0]
            vv[0:TILE] = vp_ref[0]
            vv[TILE:] = vc_ref[0]
            dl_s[...] = _group_mean(do_ref[...] * o_ref[...], bd_ref[...], 1.0)
            head0, band_prev, band_cur = _attn_masks()

            for pi, d in enumerate(DILATIONS):
                def unit(uidx, carry, d=d, pi=pi):
                    b, rows_q, rows_c, rows_p = _unit_rows(uidx, d)
                    mask = band_cur | (band_prev & ((t > 0) | (b > 0)))
                    q2 = _stack_heads(q_ref.at[0][rows_q, :], head0)
                    do2 = _stack_heads(do_ref[rows_q, :], head0)
                    lse_f = lse_ref.at[0][rows_q, :]
                    dl_f = dl_s[rows_q, :]
                    lse2 = jnp.concatenate([lse_f[:, 0:1], lse_f[:, HEAD:HEAD + 1]], axis=0)
                    dl2 = jnp.concatenate([dl_f[:, 0:1], dl_f[:, HEAD:HEAD + 1]], axis=0)
                    kcat = jnp.concatenate([kk[rows_p, :], kk[rows_c, :]], axis=0).astype(BF16)
                    vcat = jnp.concatenate([vv[rows_p, :], vv[rows_c, :]], axis=0).astype(BF16)
                    p = jnp.where(mask, jnp.exp(_nt(q2, kcat) - lse2), 0.0)
                    ds = (p * (_nt(do2, vcat) - dl2)).astype(BF16)
                    dq2 = _mm(ds, kcat)
                    dq_s.at[pi][rows_q, :] = jnp.where(head0, dq2[0:BLK], dq2[BLK:])
                    dk2 = _tn(ds, q2)
                    dv2 = _tn(p.astype(BF16), do2)
                    dkp.at[pi][rows_q, :] = dk2[0:BLK]
                    dkc.at[pi][rows_q, :] = dk2[BLK:]
                    dvp.at[pi][rows_q, :] = dv2[0:BLK]
                    dvc.at[pi][rows_q, :] = dv2[BLK:]
                    return carry

                lax.fori_loop(0, TILE // BLK, unit, 0, unroll=8)

            dq_ref[...] = dq_s[0] + dq_s[1] + dq_s[2]

        @pl.when(t > 0)
        def _():
            dk_ref[...] = hold_k[...]
            dv_ref[...] = hold_v[...]

        @pl.when((t > 0) & (t < nt))
        def _():
            for pi, d in enumerate(DILATIONS):
                back = d * BLK
                dk_ref[TILE - back:, :] = dk_ref[TILE - back:, :] + dkp[pi, 0:back, :]
                dv_ref[TILE - back:, :] = dv_ref[TILE - back:, :] + dvp[pi, 0:back, :]

        @pl.when(t < nt)
        def _():
            hold_k[...] = dkc[0] + dkc[1] + dkc[2]
            hold_v[...] = dvc[0] + dvc[1] + dvc[2]
            for pi, d in enumerate(DILATIONS):
                back = d * BLK
                if back < TILE:
                    hold_k[0:TILE - back, :] = hold_k[0:TILE - back, :] + dkp[pi, back:, :]
                    hold_v[0:TILE - back, :] = hold_v[0:TILE - back, :] + dvp[pi, back:, :]

        if side is not None:
            pl.when((pl.program_id(0) == n_hp - 1) & (t == nt))(side.finish)

    last = nt - 1
    extra = (side.in_specs(), side.out_specs(), side.out_shapes(), side.scratch_shapes()) if side is not None else ([], [], [], [])
    cur = lambda hp, t: (hp, jnp.minimum(t, last), 0)
    prev = lambda hp, t: (hp, jnp.clip(t - 1, 0, last), 0)
    cur2 = lambda hp, t: (jnp.minimum(t, last), hp)
    prev2 = lambda hp, t: (jnp.maximum(t - 1, 0), hp)
    blk = (1, TILE, LANES)
    blk2 = (TILE, LANES)
    out = jax.ShapeDtypeStruct((t_len, ATTN_W), F32)
    return pl.pallas_call(
        body, name="attn_bwd", grid=(n_hp, nt + 1),
        in_specs=[pl.BlockSpec(blk, cur), pl.BlockSpec(blk, prev), pl.BlockSpec(blk, cur),
                  pl.BlockSpec(blk, prev), pl.BlockSpec(blk, cur), pl.BlockSpec(blk2, cur2),
                  pl.BlockSpec(blk2, cur2), pl.BlockSpec(blk, cur), pl.BlockSpec((LANES, LANES), lambda hp, t: (0, 0))]
        + extra[0],
        out_specs=[pl.BlockSpec(blk2, cur2), pl.BlockSpec(blk2, prev2), pl.BlockSpec(blk2, prev2)] + extra[1],
        out_shape=[out, out, out] + extra[2],
        scratch_shapes=[pltpu.VMEM((2 * TILE, LANES), F32), pltpu.VMEM((2 * TILE, LANES), F32)]
        + [pltpu.VMEM((n_pat, TILE, LANES), F32)] * 5 + [pltpu.VMEM((TILE, LANES), F32)] * 3 + extra[3],
        compiler_params=_params("arbitrary", "arbitrary"),
    )(q, k, k, v, v, o, do, lse, ones_hp, *side_args)


def _discretise(lr, li, ldt, br, bi):
    dt = jnp.exp(ldt)
    mag = jnp.exp(lr * dt)
    ab_r, ab_i = mag * jnp.cos(li * dt), mag * jnp.sin(li * dt)
    den = lr * lr + li * li
    nr, ni = ab_r - 1.0, ab_i
    cr = (nr * lr + ni * li) / den
    ci = (ni * lr - nr * li) / den
    return ab_r, ab_i, cr * br - ci * bi, cr * bi + ci * br


def _disc_fwd(lr, li, ldt, br, bi):
    def body(lr_ref, li_ref, ldt_ref, br_ref, bi_ref, ar_o, ai_o, bbr_o, bbi_o):
        outs = _discretise(lr_ref[...], li_ref[...], ldt_ref[...], br_ref[...], bi_ref[...])
        for o_ref, val in zip((ar_o, ai_o, bbr_o, bbi_o), outs):
            o_ref[...] = val

    col = jax.ShapeDtypeStruct(lr.shape, F32)
    mat = jax.ShapeDtypeStruct(br.shape, F32)
    return pl.pallas_call(body, name="s5_disc_fwd", out_shape=[col, col, mat, mat])(lr, li, ldt, br, bi)


def _disc_bwd(lr, li, ldt, br, bi, d_ar, d_ai, d_bbr, d_bbi, group_sum):
    def body(lr_ref, li_ref, ldt_ref, br_ref, bi_ref, c1, c2, c3, c4, gs_ref, dlr_o, dli_o, dldt_o, dbr_o, dbi_o):
        _, vjp = jax.vjp(_discretise, lr_ref[...], li_ref[...], ldt_ref[...], br_ref[...], bi_ref[...])
        dlr, dli, dldt, dbr, dbi = vjp((c1[...], c2[...], c3[...], c4[...]))
        dlr_o[...] = dlr
        dli_o[...] = dli
        dbr_o[...] = dbr
        dbi_o[...] = dbi
        wide = jnp.broadcast_to(dldt, (dldt.shape[0], LANES))
        dldt_o[...] = jnp.dot(gs_ref[...], wide, precision=lax.Precision.HIGHEST, preferred_element_type=F32)

    col = jax.ShapeDtypeStruct(lr.shape, F32)
    mat = jax.ShapeDtypeStruct(br.shape, F32)
    return pl.pallas_call(
        body, name="s5_disc_bwd", out_shape=[col, col, jax.ShapeDtypeStruct((N_GROUPS, LANES), F32), mat, mat],
    )(lr, li, ldt, br, bi, d_ar, d_ai, d_bbr, d_bbi, group_sum)


N_CHUNK = TILE // BLK
HALF = 4


def _cmul(ar, ai, xr, xi):
    return ar * xr - ai * xi, ar * xi + ai * xr


def _power_table(a_ref, tab, sign, reverse):
    ar = [a_ref[0, j:j + 1, :] for j in range(HALF)]
    ai = [sign * a_ref[0, HALF + j:HALF + j + 1, :] for j in range(HALF)]

    def step(s, cur):
        row = pl.ds((BLK - 1 - s) if reverse else s, 1)
        nxt = []
        for j in range(HALF):
            tab.at[j][row, :] = cur[j]
            tab.at[HALF + j][row, :] = cur[HALF + j]
            nxt.append(_cmul(ar[j], ai[j], cur[j], cur[HALF + j]))
        return tuple(p[0] for p in nxt) + tuple(p[1] for p in nxt)

    lax.fori_loop(0, BLK, step, tuple(ar) + tuple(ai))


def _interleave(src, dst):
    for c in range(N_CHUNK):
        dst[pl.ds(c, BLK, stride=N_CHUNK), :] = src[c * BLK:(c + 1) * BLK, :]


def _deinterleave(src, dst):
    for c in range(N_CHUNK):
        dst[c * BLK:(c + 1) * BLK, :] = src[pl.ds(c, BLK, stride=N_CHUNK), :]


def _step_rows(s):
    return pl.ds(pl.multiple_of(s * N_CHUNK, N_CHUNK), N_CHUNK)


def _chunk_scan(buf, a_ref, sign, reverse):
    ar = [jnp.broadcast_to(a_ref[0, j:j + 1, :], (N_CHUNK, LANES)) for j in range(HALF)]
    ai = [sign * jnp.broadcast_to(a_ref[0, HALF + j:HALF + j + 1, :], (N_CHUNK, LANES)) for j in range(HALF)]

    def step(i, carry):
        s = (BLK - 1 - i) if reverse else i
        rows = _step_rows(s)
        out = []
        for j in range(HALF):
            pr, pi = _cmul(ar[j], ai[j], carry[j], carry[HALF + j])
            xr = buf.at[j][rows, :] + pr
            xi = buf.at[HALF + j][rows, :] + pi
            buf.at[j][rows, :] = xr
            buf.at[HALF + j][rows, :] = xi
            out.append((xr, xi))
        return tuple(p[0] for p in out) + tuple(p[1] for p in out)

    zero = jnp.zeros((N_CHUNK, LANES), F32)
    lax.fori_loop(0, BLK, step, (zero,) * (2 * HALF), unroll=2)


def _chunk_states(buf, carry_s, xin_s, tab, reverse):
    edge = 0 if reverse else BLK - 1
    top = 0 if reverse else BLK - 1
    pw = [tab[j, top:top + 1, :] for j in range(2 * HALF)]
    cur = [carry_s[j:j + 1, :] for j in range(2 * HALF)]
    summary = [buf[j, edge * N_CHUNK:(edge + 1) * N_CHUNK, :] for j in range(2 * HALF)]
    order = range(N_CHUNK - 1, -1, -1) if reverse else range(N_CHUNK)
    for c in order:
        for j in range(2 * HALF):
            xin_s[j, c:c + 1, :] = cur[j]
        nxt = []
        for j in range(HALF):
            pr, pi = _cmul(pw[j], pw[HALF + j], cur[j], cur[HALF + j])
            nxt.append((pr + summary[j][c:c + 1, :], pi + summary[HALF + j][c:c + 1, :]))
        cur = [p[0] for p in nxt] + [p[1] for p in nxt]
    for j in range(2 * HALF):
        carry_s[j:j + 1, :] = cur[j]


def _s5_fwd(u, a_cat, b_mat, c_mat, d_skip):
    t_len = u.shape[0]
    nt = t_len // TILE

    def body(u_ref, a_ref, b_ref, c_ref, d_ref, y_ref, st_ref, xs, us, tab, carry_s, xin_s):
        sb = pl.program_id(1)

        @pl.when(sb == 0)
        def _():
            _power_table(a_ref, tab, 1.0, False)
            carry_s[...] = jnp.zeros_like(carry_s)

        st_ref[0, 0] = carry_s[...]
        _interleave(u_ref, us)
        uv = us[...]
        bu = _mm(uv.astype(BF16), b_ref[0])
        for j in range(2 * HALF):
            xs[j] = bu[:, j * LANES:(j + 1) * LANES]
        _chunk_scan(xs, a_ref, 1.0, False)
        _chunk_states(xs, carry_s, xin_s, tab, False)
        xin = [xin_s[j] for j in range(2 * HALF)]

        def fix(s, acc):
            rows = _step_rows(s)
            for j in range(HALF):
                pr, pi = _cmul(tab.at[j][pl.ds(s, 1), :], tab.at[HALF + j][pl.ds(s, 1), :], xin[j], xin[HALF + j])
                xs.at[j][rows, :] = xs.at[j][rows, :] + pr
                xs.at[HALF + j][rows, :] = xs.at[HALF + j][rows, :] + pi
            return acc

        lax.fori_loop(0, BLK, fix, 0, unroll=2)
        xcat = jnp.concatenate([xs[j].astype(BF16) for j in range(2 * HALF)], axis=1)
        us[...] = d_ref[0] * uv + _mm(xcat, c_ref[0])
        _deinterleave(us, y_ref)

    return pl.pallas_call(
        body, name="s5_fwd", grid=(N_LB, nt),
        in_specs=[pl.BlockSpec((TILE, LANES), lambda lb, sb: (sb, lb)),
                  pl.BlockSpec((1, 2 * HALF, LANES), lambda lb, sb: (lb, 0, 0)),
                  pl.BlockSpec((1, LANES, 2 * HALF * LANES), lambda lb, sb: (lb, 0, 0)),
                  pl.BlockSpec((1, 2 * HALF * LANES, LANES), lambda lb, sb: (lb, 0, 0)),
                  pl.BlockSpec((1, 1, LANES), lambda lb, sb: (lb, 0, 0))],
        out_specs=[pl.BlockSpec((TILE, LANES), lambda lb, sb: (sb, lb)),
                   pl.BlockSpec((1, 1, 2 * HALF, LANES), lambda lb, sb: (lb, sb, 0, 0))],
        out_shape=[jax.ShapeDtypeStruct((t_len, SSM_W), F32), jax.ShapeDtypeStruct((N_LB, nt, 2 * HALF, LANES), F32)],
        scratch_shapes=[pltpu.VMEM((2 * HALF, TILE, LANES), F32), pltpu.VMEM((TILE, LANES), F32),
                        pltpu.VMEM((2 * HALF, BLK, LANES), F32),
                        pltpu.VMEM((2 * HALF, LANES), F32), pltpu.VMEM((2 * HALF, N_CHUNK, LANES), F32)],
        compiler_params=_params("arbitrary", "arbitrary"),
    )(u, a_cat, b_mat, c_mat, d_skip)


def _s5_bwd(u, dy, states, a_cat, b_mat, c_mat, d_skip):
    t_len = u.shape[0]
    nt = t_len // TILE
    last = nt - 1

    def body(u_ref, dy_ref, st_ref, a_ref, b_ref, c_ref, d_ref, du_ref, db_ref, dc_ref, da_ref, dd_ref,
             xs, gs, us, dys, tab, tabc, carry_s, lam_s, xin_s, lin_s):
        sb = pl.program_id(1)

        @pl.when(sb == 0)
        def _():
            _power_table(a_ref, tab, 1.0, False)
            _power_table(a_ref, tabc, -1.0, True)
            lam_s[...] = jnp.zeros_like(lam_s)
            db_ref[...] = jnp.zeros_like(db_ref)
            dc_ref[...] = jnp.zeros_like(dc_ref)
            da_ref[...] = jnp.zeros_like(da_ref)
            dd_ref[...] = jnp.zeros_like(dd_ref)

        _interleave(u_ref, us)
        _interleave(dy_ref, dys)
        uv = us[...]
        dyv = dys[...]
        ub = uv.astype(BF16)
        dyb = dyv.astype(BF16)
        carry_s[...] = st_ref[0, 0]
        bu = _mm(ub, b_ref[0])
        gy = _nt(dyb, c_ref[0])
        for j in range(2 * HALF):
            xs[j] = bu[:, j * LANES:(j + 1) * LANES]
            gs[j] = gy[:, j * LANES:(j + 1) * LANES]
        _chunk_scan(xs, a_ref, 1.0, False)
        _chunk_states(xs, carry_s, xin_s, tab, False)
        _chunk_scan(gs, a_ref, -1.0, True)
        _chunk_states(gs, lam_s, lin_s, tabc, True)
        zero = jnp.zeros((N_CHUNK, LANES), F32)
        for grp in range(0, HALF, 2):
            slabs = (grp, grp + 1)
            xin = [(xin_s[j], xin_s[HALF + j]) for j in slabs]
            lin = [(lin_s[j], lin_s[HALF + j]) for j in slabs]

            def fix(i, carry, slabs=slabs, xin=xin, lin=lin):
                s = BLK - 1 - i
                rows = _step_rows(s)
                out = []
                for k, j in enumerate(slabs):
                    nr, ni, acc_r, acc_i = carry[4 * k:4 * k + 4]
                    pr, pi = _cmul(tab.at[j][pl.ds(s, 1), :], tab.at[HALF + j][pl.ds(s, 1), :], xin[k][0], xin[k][1])
                    xr = xs.at[j][rows, :] + pr
                    xi = xs.at[HALF + j][rows, :] + pi
                    xs.at[j][rows, :] = xr
                    xs.at[HALF + j][rows, :] = xi
                    qr, qi = _cmul(tabc.at[j][pl.ds(s, 1), :], tabc.at[HALF + j][pl.ds(s, 1), :], lin[k][0], lin[k][1])
                    lr_ = gs.at[j][rows, :] + qr
                    li_ = gs.at[HALF + j][rows, :] + qi
                    gs.at[j][rows, :] = lr_
                    gs.at[HALF + j][rows, :] = li_
                    out += [lr_, li_, acc_r + (xr * nr + xi * ni), acc_i + (xr * ni - xi * nr)]
                return tuple(out)

            init = []
            for k in range(len(slabs)):
                init += [lin[k][0], lin[k][1], zero, zero]
            res = lax.fori_loop(0, BLK, fix, tuple(init), unroll=2)
            for k, j in enumerate(slabs):
                da_ref[0, j:j + 1, :] = da_ref[0, j:j + 1, :] + _colsum(res[4 * k + 2])
                da_ref[0, HALF + j:HALF + j + 1, :] = da_ref[0, HALF + j:HALF + j + 1, :] + _colsum(res[4 * k + 3])
        lam = jnp.concatenate([gs[j].astype(BF16) for j in range(2 * HALF)], axis=1)
        xcat = jnp.concatenate([xs[j].astype(BF16) for j in range(2 * HALF)], axis=1)
        us[...] = _nt(lam, b_ref[0]) + d_ref[0] * dyv
        _deinterleave(us, du_ref)
        db_ref[0] = db_ref[0] + _tn(ub, lam)
        dc_ref[0] = dc_ref[0] + _tn(dyb, xcat)
        dd_ref[0] = dd_ref[0] + _colsum(dyv * uv)

    rev = lambda lb, sb: (last - sb, lb)
    per_lb = lambda lb, sb: (lb, 0, 0)
    wide = 2 * HALF * LANES
    return pl.pallas_call(
        body, name="s5_bwd", grid=(N_LB, nt),
        in_specs=[pl.BlockSpec((TILE, LANES), rev), pl.BlockSpec((TILE, LANES), rev),
                  pl.BlockSpec((1, 1, 2 * HALF, LANES), lambda lb, sb: (lb, last - sb, 0, 0)),
                  pl.BlockSpec((1, 2 * HALF, LANES), per_lb), pl.BlockSpec((1, LANES, wide), per_lb),
                  pl.BlockSpec((1, wide, LANES), per_lb), pl.BlockSpec((1, 1, LANES), per_lb)],
        out_specs=[pl.BlockSpec((TILE, LANES), rev), pl.BlockSpec((1, LANES, wide), per_lb),
                   pl.BlockSpec((1, LANES, wide), per_lb), pl.BlockSpec((1, 2 * HALF, LANES), per_lb),
                   pl.BlockSpec((1, 1, LANES), per_lb)],
        out_shape=[jax.ShapeDtypeStruct((t_len, SSM_W), F32), jax.ShapeDtypeStruct((N_LB, LANES, wide), F32),
                   jax.ShapeDtypeStruct((N_LB, LANES, wide), F32), jax.ShapeDtypeStruct((N_LB, 2 * HALF, LANES), F32),
                   jax.ShapeDtypeStruct((N_LB, 1, LANES), F32)],
        scratch_shapes=[pltpu.VMEM((2 * HALF, TILE, LANES), F32), pltpu.VMEM((2 * HALF, TILE, LANES), F32),
                        pltpu.VMEM((TILE, LANES), F32), pltpu.VMEM((TILE, LANES), F32),
                        pltpu.VMEM((2 * HALF, BLK, LANES), F32), pltpu.VMEM((2 * HALF, BLK, LANES), F32),
                        pltpu.VMEM((2 * HALF, LANES), F32), pltpu.VMEM((2 * HALF, LANES), F32),
                        pltpu.VMEM((2 * HALF, N_CHUNK, LANES), F32), pltpu.VMEM((2 * HALF, N_CHUNK, LANES), F32)],
        compiler_params=_params("arbitrary", "arbitrary"),
    )(u, dy, states, a_cat, b_mat, c_mat, d_skip)


_GELU_C = math.sqrt(2.0 / math.pi)
_GELU_K = 0.044715


def _gelu(y):
    t = jnp.tanh(_GELU_C * (y + _GELU_K * (y * y * y)))
    return y * (0.5 * (1.0 + t)), t


def _gelu_grad(y, t):
    return 0.5 * (1.0 + t) + 0.5 * y * (1.0 - t * t) * (_GELU_C * (1.0 + 3.0 * _GELU_K * y * y))


def _glu(y, wg, bias):
    z, t = _gelu(y)
    sg = jax.nn.sigmoid(_mm(z.astype(BF16), wg) + bias)
    return z, t, sg


def _mix_fwd(attn, y, x, wg, glu_b, ga, gs, wo):
    t_len = x.shape[0]
    tm = 512

    def body(attn_ref, y_ref, x_ref, wg_ref, b_ref, ga_ref, gs_ref, wo_ref, x2_ref, mix_ref, z_ref):
        z, _, sg = _glu(y_ref[...], wg_ref[...], b_ref[...])
        z_ref[...] = z.astype(BF16)
        s = z * sg
        av = attn_ref[...]
        an = (av * _rms(av) * ga_ref[...]).astype(BF16)
        sn = (s * _rms(s) * gs_ref[...]).astype(BF16)
        mix_ref[:, 0:ATTN_W] = an
        mix_ref[:, ATTN_W:] = sn
        x2_ref[...] = x_ref[...] + _mm(an, wo_ref[0:ATTN_W, :]) + _mm(sn, wo_ref[ATTN_W:, :])

    row = lambda i: (i, 0)
    const = lambda i: (0, 0)
    return pl.pallas_call(
        body, name="mix_fwd", grid=(t_len // tm,),
        in_specs=[pl.BlockSpec((tm, ATTN_W), row), pl.BlockSpec((tm, SSM_W), row), pl.BlockSpec((tm, D_MODEL), row),
                  pl.BlockSpec((SSM_W, SSM_W), const), pl.BlockSpec((1, SSM_W), const), pl.BlockSpec((1, ATTN_W), const),
                  pl.BlockSpec((1, SSM_W), const), pl.BlockSpec((D_MODEL, D_MODEL), const)],
        out_specs=[pl.BlockSpec((tm, D_MODEL), row), pl.BlockSpec((tm, D_MODEL), row), pl.BlockSpec((tm, SSM_W), row)],
        out_shape=[jax.ShapeDtypeStruct((t_len, D_MODEL), F32), jax.ShapeDtypeStruct((t_len, D_MODEL), BF16),
                   jax.ShapeDtypeStruct((t_len, SSM_W), BF16)],
        compiler_params=_params("arbitrary"),
    )(attn, y, x, wg, glu_b, ga, gs, wo)


def _mlp(x2, target, g2, wu, wd):
    t_len = x2.shape[0]
    tm = 256
    fc = 1024
    n_fc = D_FF // fc

    def body(x2_ref, tg_ref, g2_ref, wu_hbm, wd_hbm, dx2_ref, hdn_ref, dup_ref, h_ref, dyb_ref, dg2_ref, loss_ref,
             wu_s, wd_s, relu_s, sem):
        @pl.when(pl.program_id(0) == 0)
        def _():
            cu = pltpu.make_async_copy(wu_hbm, wu_s, sem.at[0])
            cd = pltpu.make_async_copy(wd_hbm, wd_s, sem.at[1])
            cu.start()
            cd.start()
            cu.wait()
            cd.wait()
            dg2_ref[...] = jnp.zeros_like(dg2_ref)
            loss_ref[...] = jnp.zeros_like(loss_ref)

        x2v = x2_ref[...]
        r = _rms(x2v)
        g2v = g2_ref[...]
        h = (x2v * r * g2v).astype(BF16)
        h_ref[...] = h
        yout = x2v
        for c in range(n_fc):
            cols = slice(c * fc, (c + 1) * fc)
            ru = jnp.maximum(_mm(h, wu_s[:, cols]), 0.0)
            relu_s[:, cols] = ru
            hd = (ru * ru).astype(BF16)
            hdn_ref[:, cols] = hd
            yout = yout + _mm(hd, wd_s[cols, :])
        err = yout - tg_ref[...]
        loss_ref[...] = loss_ref[...] + 0.5 * jnp.sum(err * err) * (1.0 / D_MODEL)
        dy = err * (1.0 / D_MODEL)
        dyb = dy.astype(BF16)
        dyb_ref[...] = dyb
        dh = jnp.zeros((tm, D_MODEL), F32)
        for c in range(n_fc):
            cols = slice(c * fc, (c + 1) * fc)
            dup = (_nt(dyb, wd_s[cols, :]) * (2.0 * relu_s[:, cols])).astype(BF16)
            dup_ref[:, cols] = dup
            dh = dh + _nt(dup, wu_s[:, cols])
        dxn, gterm = _rms_bwd(dh, x2v, r, g2v)
        dx2_ref[...] = dy + dxn
        dg2_ref[...] = dg2_ref[...] + _colsum(gterm)

    row = lambda i: (i, 0)
    const = lambda i: (0, 0)
    any_spec = pl.BlockSpec(memory_space=pl.ANY)
    return pl.pallas_call(
        body, name="mlp", grid=(t_len // tm,),
        in_specs=[pl.BlockSpec((tm, D_MODEL), row), pl.BlockSpec((tm, D_MODEL), row), pl.BlockSpec((1, D_MODEL), const),
                  any_spec, any_spec],
        out_specs=[pl.BlockSpec((tm, D_MODEL), row), pl.BlockSpec((tm, D_FF), row), pl.BlockSpec((tm, D_FF), row),
                   pl.BlockSpec((tm, D_MODEL), row), pl.BlockSpec((tm, D_MODEL), row), pl.BlockSpec((1, D_MODEL), const),
                   pl.BlockSpec((1, LANES), const)],
        out_shape=[jax.ShapeDtypeStruct((t_len, D_MODEL), F32), jax.ShapeDtypeStruct((t_len, D_FF), BF16),
                   jax.ShapeDtypeStruct((t_len, D_FF), BF16), jax.ShapeDtypeStruct((t_len, D_MODEL), BF16),
                   jax.ShapeDtypeStruct((t_len, D_MODEL), BF16), jax.ShapeDtypeStruct((1, D_MODEL), F32),
                   jax.ShapeDtypeStruct((1, LANES), F32)],
        scratch_shapes=[pltpu.VMEM((D_MODEL, D_FF), BF16), pltpu.VMEM((D_FF, D_MODEL), BF16),
                        pltpu.VMEM((tm, D_FF), F32), pltpu.SemaphoreType.DMA((2,))],
        compiler_params=_params("arbitrary"),
    )(x2, target, g2, wu, wd)


def _mix_bwd(dx2, attn, y, wg, glu_b, ga, gs, wo):
    t_len = dx2.shape[0]
    tm = 512

    def body(dx2_ref, attn_ref, y_ref, wg_ref, b_ref, ga_ref, gs_ref, wo_ref,
             dattn_ref, dy_ref, dx2b_ref, dgp_ref, dga_ref, dgs_ref, db_ref):
        @pl.when(pl.program_id(0) == 0)
        def _():
            dga_ref[...] = jnp.zeros_like(dga_ref)
            dgs_ref[...] = jnp.zeros_like(dgs_ref)
            db_ref[...] = jnp.zeros_like(db_ref)

        dx2b = dx2_ref[...].astype(BF16)
        dx2b_ref[...] = dx2b
        d_an = _nt(dx2b, wo_ref[0:ATTN_W, :])
        d_sn = _nt(dx2b, wo_ref[ATTN_W:, :])
        yv = y_ref[...]
        wg = wg_ref[...]
        z, t, sg = _glu(yv, wg, b_ref[...])
        s = z * sg
        av = attn_ref[...]
        d_attn, ga_term = _rms_bwd(d_an, av, _rms(av), ga_ref[...])
        d_s, gs_term = _rms_bwd(d_sn, s, _rms(s), gs_ref[...])
        dattn_ref[...] = d_attn
        dgp = d_s * z * sg * (1.0 - sg)
        dgpb = dgp.astype(BF16)
        dgp_ref[...] = dgpb
        dz = d_s * sg + _nt(dgpb, wg)
        dy_ref[...] = dz * _gelu_grad(yv, t)
        dga_ref[...] = dga_ref[...] + _colsum(ga_term)
        dgs_ref[...] = dgs_ref[...] + _colsum(gs_term)
        db_ref[...] = db_ref[...] + _colsum(dgp)

    row = lambda i: (i, 0)
    const = lambda i: (0, 0)
    vec = jax.ShapeDtypeStruct((1, SSM_W), F32)
    return pl.pallas_call(
        body, name="mix_bwd", grid=(t_len // tm,),
        in_specs=[pl.BlockSpec((tm, D_MODEL), row), pl.BlockSpec((tm, ATTN_W), row), pl.BlockSpec((tm, SSM_W), row),
                  pl.BlockSpec((SSM_W, SSM_W), const), pl.BlockSpec((1, SSM_W), const), pl.BlockSpec((1, ATTN_W), const),
                  pl.BlockSpec((1, SSM_W), const), pl.BlockSpec((D_MODEL, D_MODEL), const)],
        out_specs=[pl.BlockSpec((tm, ATTN_W), row), pl.BlockSpec((tm, SSM_W), row), pl.BlockSpec((tm, D_MODEL), row),
                   pl.BlockSpec((tm, SSM_W), row), pl.BlockSpec((1, ATTN_W), const), pl.BlockSpec((1, SSM_W), const),
                   pl.BlockSpec((1, SSM_W), const)],
        out_shape=[jax.ShapeDtypeStruct((t_len, ATTN_W), F32), jax.ShapeDtypeStruct((t_len, SSM_W), F32),
                   jax.ShapeDtypeStruct((t_len, D_MODEL), BF16), jax.ShapeDtypeStruct((t_len, SSM_W), BF16), vec, vec, vec],
        compiler_params=_params("arbitrary"),
    )(dx2, attn, y, wg, glu_b, ga, gs, wo)


def _inproj_bwd(dqs, dkn, dv, du, q_raw, k_raw, x, dx2, wi, g1, gq, gk, ones64):
    t_len = x.shape[0]
    tm = 512
    n_heads = ATTN_W // HEAD

    def body(dqs_ref, dkn_ref, dv_ref, du_ref, q_ref, k_ref, x_ref, dx2_ref, wi_ref, g1_ref, gq_ref, gk_ref, bd_ref,
             gx_ref, dproj_ref, dg1_ref, dgq_ref, dgk_ref, accq, acck):
        i = pl.program_id(0)

        @pl.when(i == 0)
        def _():
            dg1_ref[...] = jnp.zeros_like(dg1_ref)
            accq[...] = jnp.zeros_like(accq)
            acck[...] = jnp.zeros_like(acck)

        bd = bd_ref[...]

        def head_norm_bwd(dy, raw, gain, acc):
            r = lax.rsqrt(_group_mean(raw * raw, bd, HEAD) + EPS)
            xh = raw * r
            dxh = dy * gain
            acc[...] = acc[...] + _colsum(dy * xh)
            return r * (dxh - xh * _group_mean(dxh * xh, bd, HEAD))

        dq = head_norm_bwd(dqs_ref[...] * (HEAD ** -0.5), q_ref[...], gq_ref[...], accq)
        dk = head_norm_bwd(dkn_ref[...], k_ref[...], gk_ref[...], acck)
        dproj_ref[:, 0:ATTN_W] = dq.astype(BF16)
        dproj_ref[:, ATTN_W:2 * ATTN_W] = dk.astype(BF16)
        dproj_ref[:, 2 * ATTN_W:3 * ATTN_W] = dv_ref[...].astype(BF16)
        dproj_ref[:, 3 * ATTN_W:] = du_ref[...].astype(BF16)
        dxn = _nt(dproj_ref[...], wi_ref[...])
        xv = x_ref[...]
        g1v = g1_ref[...]
        dx, g1_term = _rms_bwd(dxn, xv, _rms(xv), g1v)
        gx_ref[...] = dx2_ref[...] + dx
        dg1_ref[...] = dg1_ref[...] + _colsum(g1_term)

        @pl.when(i == pl.num_programs(0) - 1)
        def _():
            for acc, out in ((accq, dgq_ref), (acck, dgk_ref)):
                tot = acc[:, 0:HEAD]
                for h in range(1, n_heads):
                    tot = tot + acc[:, h * HEAD:(h + 1) * HEAD]
                out[...] = tot

    row = lambda i: (i, 0)
    const = lambda i: (0, 0)
    aw = pl.BlockSpec((tm, ATTN_W), row)
    dm = pl.BlockSpec((tm, D_MODEL), row)
    return pl.pallas_call(
        body, name="inproj_bwd", grid=(t_len // tm,),
        in_specs=[aw, aw, aw, aw, aw, aw, dm, dm, pl.BlockSpec((D_MODEL, PROJ_W), const), pl.BlockSpec((1, D_MODEL), const),
                  pl.BlockSpec((1, ATTN_W), const), pl.BlockSpec((1, ATTN_W), const), pl.BlockSpec((ATTN_W, ATTN_W), const)],
        out_specs=[dm, pl.BlockSpec((tm, PROJ_W), row), pl.BlockSpec((1, D_MODEL), const),
                   pl.BlockSpec((1, HEAD), const), pl.BlockSpec((1, HEAD), const)],
        out_shape=[jax.ShapeDtypeStruct((t_len, D_MODEL), F32), jax.ShapeDtypeStruct((t_len, PROJ_W), BF16),
                   jax.ShapeDtypeStruct((1, D_MODEL), F32), jax.ShapeDtypeStruct((1, HEAD), F32),
                   jax.ShapeDtypeStruct((1, HEAD), F32)],
        scratch_shapes=[pltpu.VMEM((1, ATTN_W), F32), pltpu.VMEM((1, ATTN_W), F32)],
        compiler_params=_params("arbitrary"),
    )(dqs, dkn, dv, du, q_raw, k_raw, x, dx2, wi, g1, gq, gk, ones64)


def _grad_matmul(a, b, name):
    t_len, m = a.shape
    n = b.shape[1]
    bm, bn, bt = min(m, 1024), min(n, 1024), 1024

    def body(a_ref, b_ref, o_ref):
        @pl.when(pl.program_id(2) == 0)
        def _():
            o_ref[...] = jnp.zeros_like(o_ref)

        o_ref[...] = o_ref[...] + _tn(a_ref[...], b_ref[...])

    return pl.pallas_call(
        body, name=name, grid=(m // bm, n // bn, t_len // bt),
        in_specs=[pl.BlockSpec((bt, bm), lambda i, j, k: (k, i)), pl.BlockSpec((bt, bn), lambda i, j, k: (k, j))],
        out_specs=pl.BlockSpec((bm, bn), lambda i, j, k: (i, j)),
        out_shape=jax.ShapeDtypeStruct((m, n), F32),
        compiler_params=_params("arbitrary", "arbitrary", "arbitrary"),
    )(a, b)


def _adamw_update(w_ref, g_ref, m_ref, v_ref, d_o, m_o, v_o):
    gv = g_ref[...]
    mn = ADAM_B1 * m_ref[...] + (1.0 - ADAM_B1) * gv
    vn = ADAM_B2 * v_ref[...] + (1.0 - ADAM_B2) * jnp.square(gv)
    m_hat = mn / (1.0 - ADAM_B1 ** ADAM_STEP)
    v_hat = vn / (1.0 - ADAM_B2 ** ADAM_STEP)
    d_o[...] = -ADAM_LR * (m_hat / (jnp.sqrt(v_hat) + ADAM_EPS) + ADAM_WD * w_ref[...])
    m_o[...] = mn
    v_o[...] = vn


def _adamw_many(ws, gs, ms, vs):
    n = len(ws)

    def body(*refs):
        for i in range(n):
            _adamw_update(*[refs[k * n + i] for k in range(7)])

    shapes = [jax.ShapeDtypeStruct(w.shape, F32) for w in ws]
    outs = pl.pallas_call(body, name="adamw_small", out_shape=shapes * 3,
                          compiler_params=pltpu.CompilerParams(vmem_limit_bytes=VMEM_LIMIT))(*ws, *gs, *ms, *vs)
    return outs[0:n], outs[n:2 * n], outs[2 * n:]


def _adamw(w, g, m, v, name):
    rows, cols = w.shape
    br = _row_block(rows, 256)
    body = functools.partial(_adamw_update)

    spec = pl.BlockSpec((br, cols), lambda i: (i, 0))
    shape = jax.ShapeDtypeStruct((rows, cols), F32)
    return pl.pallas_call(
        body, name=name, grid=(rows // br,), in_specs=[spec] * 4, out_specs=[spec] * 3, out_shape=[shape] * 3,
        compiler_params=_params("arbitrary"),
    )(w, g, m, v)


def _sum_arrays(arrs, name, out_dtype=F32):
    rows, cols = arrs[0].shape
    br = _row_block(rows, 512)
    n = len(arrs)

    def body(*refs):
        tot = refs[0][...]
        for r in refs[1:n]:
            tot = tot + r[...]
        refs[n][...] = tot.astype(out_dtype)

    spec = pl.BlockSpec((br, cols), lambda i: (i, 0))
    return pl.pallas_call(
        body, name=name, grid=(rows // br,), in_specs=[spec] * n, out_specs=spec,
        out_shape=jax.ShapeDtypeStruct((rows, cols), out_dtype), compiler_params=_params("arbitrary"),
    )(*arrs)


GPL = N_GROUPS // N_LB
SW = GPL * N_STATE


def _eye_groups():
    return jnp.eye(GPL, dtype=F32)


def _s5_matrices(ab_r, ab_i, bb_r, bb_i, c_re, c_im, d_skip):
    eye = _eye_groups()
    a_cat = jnp.concatenate([ab_r.reshape(N_LB, HALF, LANES), ab_i.reshape(N_LB, HALF, LANES)], axis=1)

    def b_part(bb):
        b4 = jnp.transpose(bb.reshape(N_LB, GPL, N_STATE, GROUP_W), (0, 1, 3, 2))
        return (b4[:, :, :, None, :] * eye[None, :, None, :, None]).reshape(N_LB, LANES, SW)

    def c_part(cc):
        c4 = jnp.transpose(cc.reshape(N_LB, GPL, GROUP_W, N_STATE), (0, 1, 3, 2))
        return (c4[:, :, :, None, :] * eye[None, :, None, :, None]).reshape(N_LB, SW, LANES)

    b_mat = jnp.concatenate([b_part(bb_r), b_part(bb_i)], axis=2).astype(BF16)
    c_mat = jnp.concatenate([c_part(c_re), -c_part(c_im)], axis=1).astype(BF16)
    return a_cat, b_mat, c_mat, d_skip.reshape(N_LB, 1, LANES)


def _s5_unpack_grads(db, dc, da, dd):
    eye = _eye_groups()
    mask = eye[None, :, None, None, :, None]
    d6 = jnp.sum(db.reshape(N_LB, GPL, GROUP_W, 2, GPL, N_STATE) * mask, axis=4)
    dbb = jnp.transpose(d6, (3, 0, 1, 4, 2)).reshape(2, N_GROUPS * N_STATE, GROUP_W)
    c6 = jnp.sum(dc.reshape(N_LB, GPL, GROUP_W, 2, GPL, N_STATE) * mask, axis=4)
    dcc = jnp.transpose(c6, (3, 0, 1, 2, 4)).reshape(2, N_GROUPS, GROUP_W, N_STATE)
    dab_r = da[:, :HALF].reshape(N_GROUPS * N_STATE, 1)
    dab_i = da[:, HALF:].reshape(N_GROUPS * N_STATE, 1)
    return dab_r, dab_i, dbb[0], dbb[1], dcc[0], -dcc[1], dd.reshape(N_GROUPS, GROUP_W)


def _block_ones(n, width):
    i = lax.broadcasted_iota(jnp.int32, (n, n), 0) // width
    j = lax.broadcasted_iota(jnp.int32, (n, n), 1) // width
    return (i == j).astype(BF16)


def _tile_heads(g):
    return jnp.tile(g.reshape(1, HEAD), (1, ATTN_W // HEAD))


def _local_step(x, target, wi, rest, p, fwd_side=None, bwd_side=None):
    ones64 = _block_ones(ATTN_W, HEAD)
    ones_hp = _block_ones(LANES, HEAD)
    g1 = p["norm1_g"].reshape(1, D_MODEL)
    g2 = p["norm2_g"].reshape(1, D_MODEL)
    gq = _tile_heads(p["q_norm_g"])
    gk = _tile_heads(p["k_norm_g"])
    ga = p["attn_out_norm_g"].reshape(1, ATTN_W)
    gs = p["ssm_out_norm_g"].reshape(1, SSM_W)
    glu_b = p["glu_b"].reshape(1, SSM_W)
    n_gp = N_GROUPS * N_STATE
    lr = p["ssm_a_re"].reshape(n_gp, 1)
    li = p["ssm_a_im"].reshape(n_gp, 1)
    ldt = jnp.repeat(p["ssm_log_dt"].reshape(N_GROUPS), N_STATE).reshape(n_gp, 1)
    br = p["ssm_b_re"].reshape(n_gp, GROUP_W)
    bi = p["ssm_b_im"].reshape(n_gp, GROUP_W)
    ab_r, ab_i, bb_r, bb_i = _disc_fwd(lr, li, ldt, br, bi)
    a_cat, b_mat, c_mat, d_mat = _s5_matrices(
        ab_r, ab_i, bb_r, bb_i, p["ssm_c_re"].reshape(N_GROUPS, GROUP_W, N_STATE),
        p["ssm_c_im"].reshape(N_GROUPS, GROUP_W, N_STATE), p["ssm_d"])

    xn, qn, kn, vv, u, q_raw, k_raw = _inproj_fwd(x, g1, wi, gq, gk, ones64)
    if fwd_side is None:
        attn, lse = _attn_fwd(qn, kn, vv)
    else:
        attn, lse, *rest = _attn_fwd(qn, kn, vv, *fwd_side)
    wg, wo, wu, wd = rest
    y, states = _s5_fwd(u, a_cat, b_mat, c_mat, d_mat)
    x2, mix, z = _mix_fwd(attn, y, x, wg, glu_b, ga, gs, wo)
    dx2, hdn, dup, h, dyb, dg2, loss = _mlp(x2, target, g2, wu, wd)
    big = {"w_mlp_up": _grad_matmul(h, dup, "grad_w_mlp_up"), "w_mlp_down": _grad_matmul(hdn, dyb, "grad_w_mlp_down")}
    d_attn, dy_ssm, dx2b, dgp, dga, dgs, dglu_b = _mix_bwd(dx2, attn, y, wg, glu_b, ga, gs, wo)
    rode = []
    if bwd_side is None:
        dqs, dkn, dvv = _attn_bwd(qn, kn, vv, attn, d_attn, lse, ones_hp)
    else:
        dqs, dkn, dvv, *rode = _attn_bwd(qn, kn, vv, attn, d_attn, lse, ones_hp, *bwd_side(big))
    du, db, dc, da, dd = _s5_bwd(u, dy_ssm, states, a_cat, b_mat, c_mat, d_mat)
    grad_x, dproj, dg1, dgq, dgk = _inproj_bwd(dqs, dkn, dvv, du, q_raw, k_raw, x, dx2, wi, g1, gq, gk, ones64)

    big["w_in"] = _grad_matmul(xn, dproj, "grad_w_in")
    big["glu_w"] = _grad_matmul(z, dgp, "grad_glu_w")
    big["w_out"] = _grad_matmul(mix, dx2b, "grad_w_out")
    dab_r, dab_i, dbb_r, dbb_i, dc_re, dc_im, dd_g = _s5_unpack_grads(db, dc, da, dd)
    cot = {"norm1_g": dg1, "q_norm_g": dgq, "k_norm_g": dgk, "ab_r": dab_r, "ab_i": dab_i, "bb_r": dbb_r, "bb_i": dbb_i,
           "ssm_c_re": dc_re, "ssm_c_im": dc_im, "ssm_d": dd_g, "glu_b": dglu_b, "attn_out_norm_g": dga,
           "ssm_out_norm_g": dgs, "norm2_g": dg2}
    return loss[0, 0], grad_x, big, cot, (lr, li, ldt, br, bi), rode


COT_NAMES = ("norm1_g", "q_norm_g", "k_norm_g", "ab_r", "ab_i", "bb_r", "bb_i", "ssm_c_re", "ssm_c_im", "ssm_d",
             "glu_b", "attn_out_norm_g", "ssm_out_norm_g", "norm2_g")
SMALL_NAMES = ("norm1_g", "q_norm_g", "k_norm_g", "ssm_a_re", "ssm_a_im", "ssm_log_dt", "ssm_b_re", "ssm_b_im",
               "ssm_c_re", "ssm_c_im", "ssm_d", "glu_b", "attn_out_norm_g", "ssm_out_norm_g", "norm2_g")
BIG_NAMES = ("w_in", "glu_w", "w_out", "w_mlp_up", "w_mlp_down")
PACK_ROWS = 1152


def _pack(arrs):
    flat = jnp.concatenate([a.reshape(-1) for a in arrs])
    return jnp.pad(flat, (0, PACK_ROWS * LANES - flat.shape[0])).reshape(PACK_ROWS, LANES)


def _unpack(packed, like):
    flat = packed.reshape(-1)
    out, pos = [], 0
    for a in like:
        out.append(flat[pos:pos + a.size].reshape(a.shape))
        pos += a.size
    return out


def _small_grads(cot, disc_in, p):
    lr, li, ldt, br, bi = disc_in
    group_sum = (lax.broadcasted_iota(jnp.int32, (N_GROUPS, N_GROUPS * N_STATE), 1) // N_STATE
                 == lax.broadcasted_iota(jnp.int32, (N_GROUPS, N_GROUPS * N_STATE), 0)).astype(F32)
    dlr, dli, dldt, dbr, dbi = _disc_bwd(lr, li, ldt, br, bi, cot["ab_r"], cot["ab_i"], cot["bb_r"], cot["bb_i"], group_sum)
    g = dict(cot)
    g.update(ssm_a_re=dlr, ssm_a_im=dli, ssm_log_dt=dldt[:, 0], ssm_b_re=dbr, ssm_b_im=dbi)
    return {n: g[n].reshape(p[n].shape) for n in SMALL_NAMES}


BIG = {
    "w_in": ((D_MODEL, PROJ_W), 1, PROJ_W // 4, 0, D_MODEL // 2),
    "glu_w": ((SSM_W, SSM_W), 0, SSM_W // 4, 1, SSM_W // 2),
    "w_out": ((D_MODEL, D_MODEL), 0, D_MODEL // 4, 1, D_MODEL // 2),
    "w_mlp_up": ((D_MODEL, D_FF), 1, D_FF // 4, 0, D_MODEL // 2),
    "w_mlp_down": ((D_FF, D_MODEL), 0, D_FF // 4, 1, D_MODEL // 2),
}
N_BIG = len(BIG_NAMES)
N_CHIPS = 4
ANY = pl.BlockSpec(memory_space=pl.ANY)


def _cut(name, shard=False, half=False):
    shape, s_ax, s_sz, h_ax, h_sz = BIG[name]
    shape = list(shape)
    if shard:
        shape[s_ax] = s_sz
    if half:
        shape[h_ax] = h_sz
    return tuple(shape)


def _window(name, base, shard=None, half=None):
    _, s_ax, s_sz, h_ax, h_sz = BIG[name]
    idx = [pl.ds(0, base[0]), pl.ds(0, base[1])]
    if shard is not None:
        idx[s_ax] = pl.ds(pl.multiple_of(shard * s_sz, s_sz), s_sz)
    if half is not None:
        idx[h_ax] = pl.ds(pl.multiple_of(half * h_sz, h_sz), h_sz)
    return tuple(idx)


def _mesh_pos():
    return lax.axis_index("x"), lax.axis_index("y"), lax.axis_index("c")


def _other_chips(x, y):
    return [(1 - x, y, 2 * (1 - x) + y), (x, 1 - y, 2 * x + 1 - y), (1 - x, 1 - y, 2 * (1 - x) + 1 - y)]


def _remote(src, dst, send_sem, recv_sem, dev):
    return pltpu.make_async_remote_copy(src_ref=src, dst_ref=dst, send_sem=send_sem, recv_sem=recv_sem,
                                        device_id=dev, device_id_type=MESH)


def _start_remote(src, dst, send_sem, recv_sem, dev):
    cp = _remote(src, dst, send_sem, recv_sem, dev)
    cp.start()
    return cp


class _Gather:
    def __init__(self, names):
        self.names = tuple(names)
        self.n = len(self.names)

    def in_specs(self):
        return [pl.BlockSpec(memory_space=pltpu.VMEM)] * self.n

    def out_specs(self):
        return [ANY] * self.n

    def out_shapes(self):
        return [jax.ShapeDtypeStruct(BIG[w][0], BF16) for w in self.names]

    def scratch_shapes(self):
        n_sem = (N_CHIPS - 1) * self.n
        return ([pltpu.VMEM(_cut(w, shard=True), BF16) for w in self.names]
                + [pltpu.SemaphoreType.DMA((n_sem,))] * 4 + [pltpu.SemaphoreType.DMA((self.n,))])

    def bind(self, ins, outs, scratch):
        self.ins, self.outs = ins, outs
        self.stage = scratch[:self.n]
        self.send, self.recv, self.fsend, self.frecv, self.lsem = scratch[self.n:]

    def _copies(self):
        x, y, c = _mesh_pos()
        me = 2 * x + y
        sib = (x, y, 1 - c)
        local, sends, lands, fwds, flands = [], [], [], [], []
        for w, n in enumerate(self.names):
            local.append(pltpu.make_async_copy(self.stage[w], self.outs[w].at[_window(n, BIG[n][0], shard=me)], self.lsem.at[w]))
        for k, (px, py, pj) in enumerate(_other_chips(x, y)):
            for w, n in enumerate(self.names):
                s = k * self.n + w
                sends.append(_remote(self.stage[w].at[_window(n, _cut(n, shard=True), half=c)],
                                     self.outs[w].at[_window(n, BIG[n][0], shard=me, half=c)],
                                     self.send.at[s], self.recv.at[s], (px, py, c)))
                got = self.outs[w].at[_window(n, BIG[n][0], shard=pj, half=c)]
                lands.append(_remote(got, got, self.send.at[s], self.recv.at[s], (px, py, c)))
                fwds.append(_remote(got, got, self.fsend.at[s], self.frecv.at[s], sib))
                theirs = self.outs[w].at[_window(n, BIG[n][0], shard=pj, half=1 - c)]
                flands.append(_remote(theirs, theirs, self.fsend.at[s], self.frecv.at[s], sib))
        return local, sends, lands, fwds, flands

    def start(self):
        for w in range(self.n):
            self.stage[w][...] = self.ins[w][...].astype(BF16)
        local, sends, _, _, _ = self._copies()
        for cp in local + sends:
            cp.start()

    def forward(self):
        _, _, lands, fwds, _ = self._copies()
        for land, fwd in zip(lands, fwds):
            land.wait_recv()
            fwd.start()

    def finish(self):
        local, sends, _, fwds, flands = self._copies()
        for cp in flands:
            cp.wait_recv()
        for cp in sends + fwds:
            cp.wait_send()
        for cp in local:
            cp.wait()


def _gather_weights(shards, names):
    g = _Gather(names)

    def body(*refs):
        g.bind(refs[0:g.n], refs[g.n:2 * g.n], refs[2 * g.n:])
        g.start()
        g.forward()
        g.finish()

    return pl.pallas_call(
        body, name="gather_" + "_".join(names), in_specs=g.in_specs(), out_specs=g.out_specs(), out_shape=g.out_shapes(),
        scratch_shapes=g.scratch_shapes(), compiler_params=pltpu.CompilerParams(vmem_limit_bytes=VMEM_LIMIT),
    )(*[shards[n] for n in names])


def _pair_exchange(grads, names, packed=None):
    n_big = len(names)
    n_all = n_big + (packed is not None)

    def body(*refs):
        ins, got = refs[0:n_all], refs[n_all:2 * n_all]
        send, recv = refs[2 * n_all:]
        x, y, c = _mesh_pos()
        sib = (x, y, 1 - c)
        copies = []
        for w, n in enumerate(names):
            copies.append(_start_remote(ins[w].at[_window(n, BIG[n][0], half=1 - c)], got[w], send.at[w], recv.at[w], sib))
        if packed is not None:
            copies.append(_start_remote(ins[n_big], got[n_big], send.at[n_big], recv.at[n_big], sib))
        for cp in copies:
            cp.wait()

    shapes = [jax.ShapeDtypeStruct(_cut(n, half=True), F32) for n in names]
    args = [grads[n] for n in names]
    if packed is not None:
        shapes.append(jax.ShapeDtypeStruct(packed.shape, F32))
        args.append(packed)
    return pl.pallas_call(
        body, name="grad_pair_exchange_" + "_".join(names), in_specs=[ANY] * n_all, out_specs=[ANY] * n_all, out_shape=shapes,
        scratch_shapes=[pltpu.SemaphoreType.DMA((n_all,)), pltpu.SemaphoreType.DMA((n_all,))],
    )(*args)


def _pair_sum(name, full, got, core):
    _, _, _, h_ax, _ = BIG[name]
    rows, cols = _cut(name, half=True)
    br = _row_block(rows, 512)
    nb = rows // br
    own_map = (lambda i, c: (i + c[0] * nb, 0)) if h_ax == 0 else (lambda i, c: (i, c[0]))

    def body(c_ref, own_ref, got_ref, o_ref):
        o_ref[...] = (own_ref[...] + got_ref[...]).astype(BF16)

    plain = pl.BlockSpec((br, cols), lambda i, c: (i, 0))
    return pl.pallas_call(
        body, name="pair_sum_" + name,
        grid_spec=pltpu.PrefetchScalarGridSpec(num_scalar_prefetch=1, grid=(nb,),
                                               in_specs=[pl.BlockSpec((br, cols), own_map), plain], out_specs=plain),
        out_shape=jax.ShapeDtypeStruct((rows, cols), BF16), compiler_params=_params("arbitrary"),
    )(core, full, got)


class _ChipExchange:
    def __init__(self, names, packed_shape=None):
        self.names = tuple(names)
        self.packed_shape = packed_shape
        self.n = len(self.names) + (packed_shape is not None)

    def in_specs(self):
        return [ANY] * self.n

    def out_specs(self):
        return [ANY] * self.n

    def out_shapes(self):
        shapes = [jax.ShapeDtypeStruct((N_CHIPS,) + _cut(w, shard=True, half=True), BF16) for w in self.names]
        if self.packed_shape is not None:
            shapes.append(jax.ShapeDtypeStruct((N_CHIPS,) + tuple(self.packed_shape), F32))
        return shapes

    def scratch_shapes(self):
        n_sem = (N_CHIPS - 1) * self.n
        return [pltpu.SemaphoreType.DMA((n_sem,)), pltpu.SemaphoreType.DMA((n_sem,))]

    def bind(self, ins, outs, scratch):
        self.ins, self.outs = ins, outs
        self.send, self.recv = scratch

    def _piece(self, w, shard):
        if w >= len(self.names):
            return self.ins[w]
        n = self.names[w]
        return self.ins[w].at[_window(n, _cut(n, half=True), shard=shard)]

    def _copies(self):
        x, y, c = _mesh_pos()
        me = 2 * x + y
        sends, lands = [], []
        for k, (px, py, pj) in enumerate(_other_chips(x, y)):
            for w in range(self.n):
                s = k * self.n + w
                sends.append(_remote(self._piece(w, pj), self.outs[w].at[me], self.send.at[s], self.recv.at[s], (px, py, c)))
                lands.append(_remote(self._piece(w, me), self.outs[w].at[pj], self.send.at[s], self.recv.at[s], (px, py, c)))
        return sends, lands

    def start(self):
        for cp in self._copies()[0]:
            cp.start()

    def finish(self):
        sends, lands = self._copies()
        for cp in lands:
            cp.wait_recv()
        for cp in sends:
            cp.wait_send()


def _chip_exchange(halves, packed, names):
    ex = _ChipExchange(names, packed.shape)

    def body(*refs):
        ex.bind(refs[0:ex.n], refs[ex.n:2 * ex.n], refs[2 * ex.n:])
        ex.start()
        ex.finish()

    return pl.pallas_call(
        body, name="grad_chip_exchange", in_specs=ex.in_specs(), out_specs=ex.out_specs(), out_shape=ex.out_shapes(),
        scratch_shapes=ex.scratch_shapes(),
    )(*halves, packed)


def _chip_sum(name, own, slots, chip):
    n_slot, rows, cols = slots.shape
    br = _row_block(rows, 512)
    nb = rows // br
    if name in BIG and BIG[name][1] == 1:
        own_map = lambda i, m: (i, m[0])
    elif name in BIG:
        own_map = lambda i, m: (i + m[0] * nb, 0)
    else:
        own_map = lambda i, m: (i, 0)

    def slot_map(j):
        return lambda i, m: (jnp.where(m[0] == j, (j + 1) % n_slot, j), i, 0)

    def body(m_ref, own_ref, *refs):
        own_blk = own_ref[...].astype(F32)
        tot = None
        for j in range(n_slot):
            term = jnp.where(m_ref[0] == j, own_blk, refs[j][...].astype(F32))
            tot = term if tot is None else tot + term
        refs[n_slot][...] = tot

    in_specs = [pl.BlockSpec((br, cols), own_map)] + [pl.BlockSpec((None, br, cols), slot_map(j)) for j in range(n_slot)]
    return pl.pallas_call(
        body, name="chip_sum_" + name,
        grid_spec=pltpu.PrefetchScalarGridSpec(num_scalar_prefetch=1, grid=(nb,), in_specs=in_specs,
                                               out_specs=pl.BlockSpec((br, cols), lambda i, m: (i, 0))),
        out_shape=jax.ShapeDtypeStruct((rows, cols), F32), compiler_params=_params("arbitrary"),
    )(chip, own, *([slots] * n_slot))


def _half_exchange(pieces):
    def body(*refs):
        ins, outs = refs[0:N_BIG], refs[N_BIG:2 * N_BIG]
        send, recv = refs[2 * N_BIG:]
        x, y, c = _mesh_pos()
        sib = (x, y, 1 - c)
        copies = []
        for w, n in enumerate(BIG_NAMES):
            copies.append(_start_remote(ins[w], outs[w], send.at[w], recv.at[w], sib))
        for cp in copies:
            cp.wait()

    return pl.pallas_call(
        body, name="grad_half_exchange", in_specs=[ANY] * N_BIG, out_specs=[ANY] * N_BIG,
        out_shape=[jax.ShapeDtypeStruct(_cut(n, shard=True, half=True), F32) for n in BIG_NAMES],
        scratch_shapes=[pltpu.SemaphoreType.DMA((N_BIG,)), pltpu.SemaphoreType.DMA((N_BIG,))],
    )(*pieces)


WEIGHT_NAMES = ("norm1_g", "w_in", "q_norm_g", "k_norm_g", "ssm_a_re", "ssm_a_im", "ssm_log_dt", "ssm_b_re", "ssm_b_im",
                "ssm_c_re", "ssm_c_im", "ssm_d", "glu_w", "glu_b", "attn_out_norm_g", "ssm_out_norm_g", "w_out", "norm2_g",
                "w_mlp_up", "w_mlp_down")


def _train_step(a):
    x = a["x"][0]
    target = a["loss_target"][0]
    shards = {n: a[n][0] for n in BIG_NAMES}
    p = {n: a[n][0] for n in SMALL_NAMES}
    core = lax.axis_index("c").astype(jnp.int32).reshape(1)
    chip_id = (2 * lax.axis_index("x") + lax.axis_index("y")).astype(jnp.int32).reshape(1)

    later = ("glu_w", "w_out", "w_mlp_up", "w_mlp_down")
    early = ("w_mlp_up", "w_mlp_down")
    late = ("w_in", "glu_w", "w_out")
    (wi,) = _gather_weights(shards, ("w_in",))
    chip = {}

    def bwd_side(grads):
        got = _pair_exchange(grads, early)
        for n, g in zip(early, got):
            chip[n] = _pair_sum(n, grads[n], g, core)
        return _ChipExchange(early), [chip[n] for n in early]

    loss, grad_x, big, cot, disc_in, early_slots = _local_step(
        x, target, wi, None, p, fwd_side=(_Gather(later), [shards[n] for n in later]), bwd_side=bwd_side)
    slots = dict(zip(early, early_slots))

    cot_list = [cot[n] for n in COT_NAMES]
    packed = _pack(cot_list)
    *got, got_packed = _pair_exchange(big, late, packed)
    for n, g in zip(late, got):
        chip[n] = _pair_sum(n, big[n], g, core)
    chip_packed = _sum_arrays([packed, got_packed], "pair_sum_small")
    *late_slots, small_slots = _chip_exchange([chip[n] for n in late], chip_packed, late)
    slots.update(zip(late, late_slots))
    pieces = [_chip_sum(n, chip[n], slots[n], chip_id) for n in BIG_NAMES]
    small_sum = _chip_sum("small", chip_packed, small_slots, chip_id)
    shard_grads = {}
    for n, mine, theirs in zip(BIG_NAMES, pieces, _half_exchange(pieces)):
        h_ax = BIG[n][3]
        shard_grads[n] = jnp.where(core[0] == 0, jnp.concatenate([mine, theirs], axis=h_ax),
                                   jnp.concatenate([theirs, mine], axis=h_ax))
    small_grads = _small_grads(dict(zip(COT_NAMES, _unpack(small_sum, cot_list))), disc_in, p)

    grads, delta, new_m, new_v = {}, {}, {}, {}
    for n in BIG_NAMES:
        grads[n] = shard_grads[n]
        delta[n], new_m[n], new_v[n] = _adamw(a[n][0], grads[n], a["m_" + n][0], a["v_" + n][0], "adamw_" + n)
    flat2 = lambda t: t.reshape(-1, t.shape[-1])
    res = _adamw_many([flat2(p[n]) for n in SMALL_NAMES], [flat2(small_grads[n]) for n in SMALL_NAMES],
                      [flat2(a["m_" + n][0]) for n in SMALL_NAMES], [flat2(a["v_" + n][0]) for n in SMALL_NAMES])
    for store, outs in zip((delta, new_m, new_v), res):
        store.update(zip(SMALL_NAMES, outs))
    grads.update(small_grads)

    total = lax.psum(loss, ("x", "y", "c"))
    out = [total, grad_x[None]]
    for store in (grads, delta, new_m, new_v):
        out += [store[n].reshape(a[n].shape) for n in WEIGHT_NAMES]
    return tuple(out)


def kernel(x, norm1_g, w_in, q_norm_g, k_norm_g, ssm_a_re, ssm_a_im, ssm_log_dt, ssm_b_re, ssm_b_im, ssm_c_re, ssm_c_im, ssm_d, glu_w, glu_b, attn_out_norm_g, ssm_out_norm_g, w_out, norm2_g, w_mlp_up, w_mlp_down, loss_target, m_norm1_g, m_w_in, m_q_norm_g, m_k_norm_g, m_ssm_a_re, m_ssm_a_im, m_ssm_log_dt, m_ssm_b_re, m_ssm_b_im, m_ssm_c_re, m_ssm_c_im, m_ssm_d, m_glu_w, m_glu_b, m_attn_out_norm_g, m_ssm_out_norm_g, m_w_out, m_norm2_g, m_w_mlp_up, m_w_mlp_down, v_norm1_g, v_w_in, v_q_norm_g, v_k_norm_g, v_ssm_a_re, v_ssm_a_im, v_ssm_log_dt, v_ssm_b_re, v_ssm_b_im, v_ssm_c_re, v_ssm_c_im, v_ssm_d, v_glu_w, v_glu_b, v_attn_out_norm_g, v_ssm_out_norm_g, v_w_out, v_norm2_g, v_w_mlp_up, v_w_mlp_down):
    return _train_step(dict(locals()))
```

```python
import functools
import math

import jax
import jax.numpy as jnp
from jax import lax
from jax.experimental import pallas as pl
from jax.experimental.pallas import tpu as pltpu

F32 = jnp.float32
BF16 = jnp.bfloat16
MESH = pl.DeviceIdType.MESH

D_MODEL = 1024
ATTN_W = 512
SSM_W = 512
HEAD = 64
D_FF = 4096
PROJ_W = 2048
N_GROUPS = 32
N_STATE = 64
GROUP_W = 16
EPS = 1e-6
NEG = -1e30
DILATIONS = (1, 4, 16)
BLK = 128
TILE = 2048
LANES = 128
N_LB = SSM_W // LANES
N_SLAB = 2 * N_LB * N_STATE * 8 // LANES // N_LB
VMEM_LIMIT = 56 * 1024 * 1024

ADAM_LR, ADAM_B1, ADAM_B2, ADAM_EPS, ADAM_WD, ADAM_STEP = 0.001, 0.9, 0.999, 1e-08, 0.01, 10


def _params(*sem):
    return pltpu.CompilerParams(dimension_semantics=sem, vmem_limit_bytes=VMEM_LIMIT)


def _nt(a, b):
    return lax.dot_general(a, b, (((1,), (1,)), ((), ())), preferred_element_type=F32)


def _tn(a, b):
    return lax.dot_general(a, b, (((0,), (0,)), ((), ())), preferred_element_type=F32)


def _mm(a, b):
    return jnp.dot(a, b, preferred_element_type=F32)


def _group_mean(t, ones_bd, width):
    hi = t.astype(BF16)
    lo = (t - hi.astype(F32)).astype(BF16)
    return (_mm(hi, ones_bd) + _mm(lo, ones_bd)) * (1.0 / width)


def _rms(x):
    return lax.rsqrt(jnp.mean(x * x, axis=-1, keepdims=True) + EPS)


def _rms_bwd(dy, x, r, g):
    xh = x * r
    dxh = dy * g
    dx = r * (dxh - xh * jnp.mean(dxh * xh, axis=-1, keepdims=True))
    return dx, dy * xh


def _colsum(x):
    return jnp.sum(x, axis=0, keepdims=True)


def _row_block(rows, cap):
    for b in range(min(rows, cap) // 8 * 8, 0, -8):
        if rows % b == 0:
            return b
    raise ValueError(f"no row block for {rows} rows")


def _inproj_fwd(x, g1, wi, gq, gk, ones64):
    t_len = x.shape[0]
    tm = 512
    n_hp = ATTN_W // LANES

    def body(x_ref, g1_ref, wi_ref, gq_ref, gk_ref, bd_ref, xn_ref, q_ref, k_ref, v_ref, u_ref, qr_ref, kr_ref):
        xv = x_ref[...]
        xn = (xv * _rms(xv) * g1_ref[...]).astype(BF16)
        xn_ref[...] = xn
        proj = _mm(xn, wi_ref[...])
        q = proj[:, 0:ATTN_W]
        k = proj[:, ATTN_W:2 * ATTN_W]
        v = proj[:, 2 * ATTN_W:3 * ATTN_W]
        u_ref[...] = proj[:, 3 * ATTN_W:]
        qr_ref[...] = q
        kr_ref[...] = k
        bd = bd_ref[...]
        qn = q * lax.rsqrt(_group_mean(q * q, bd, HEAD) + EPS) * gq_ref[...] * (HEAD ** -0.5)
        kn = k * lax.rsqrt(_group_mean(k * k, bd, HEAD) + EPS) * gk_ref[...]
        for hp in range(n_hp):
            sl = slice(hp * LANES, (hp + 1) * LANES)
            q_ref[hp] = qn[:, sl]
            k_ref[hp] = kn[:, sl]
            v_ref[hp] = v[:, sl]

    row = lambda i: (i, 0)
    const = lambda i: (0, 0)
    hp_spec = pl.BlockSpec((n_hp, tm, LANES), lambda i: (0, i, 0))
    hp_shape = jax.ShapeDtypeStruct((n_hp, t_len, LANES), F32)
    return pl.pallas_call(
        body, name="inproj_fwd", grid=(t_len // tm,),
        in_specs=[pl.BlockSpec((tm, D_MODEL), row), pl.BlockSpec((1, D_MODEL), const),
                  pl.BlockSpec((D_MODEL, PROJ_W), const), pl.BlockSpec((1, ATTN_W), const),
                  pl.BlockSpec((1, ATTN_W), const), pl.BlockSpec((ATTN_W, ATTN_W), const)],
        out_specs=[pl.BlockSpec((tm, D_MODEL), row), hp_spec, hp_spec, hp_spec,
                   pl.BlockSpec((tm, SSM_W), row), pl.BlockSpec((tm, ATTN_W), row), pl.BlockSpec((tm, ATTN_W), row)],
        out_shape=[jax.ShapeDtypeStruct((t_len, D_MODEL), BF16), hp_shape, hp_shape, hp_shape,
                   jax.ShapeDtypeStruct((t_len, SSM_W), F32), jax.ShapeDtypeStruct((t_len, ATTN_W), F32),
                   jax.ShapeDtypeStruct((t_len, ATTN_W), F32)],
        compiler_params=_params("arbitrary"),
    )(x, g1, wi, gq, gk, ones64)


def _attn_masks():
    head0 = lax.broadcasted_iota(jnp.int32, (BLK, LANES), 1) < HEAD
    row = lax.broadcasted_iota(jnp.int32, (2 * BLK, 2 * BLK), 0) & (BLK - 1)
    col = lax.broadcasted_iota(jnp.int32, (2 * BLK, 2 * BLK), 1)
    return head0, (col < BLK) & (col >= row), (col >= BLK) & (col - BLK <= row)


def _stack_heads(x, head0):
    return jnp.concatenate([jnp.where(head0, x, 0.0), jnp.where(head0, 0.0, x)], axis=0).astype(BF16)


def _unit_rows(uidx, d):
    nb = TILE // (BLK * d)
    r = lax.div(uidx, nb)
    b = lax.rem(uidx, nb)
    start = r + d * BLK * b
    if d == 1:
        start = pl.multiple_of(start, BLK)
        mk = lambda s: pl.ds(pl.multiple_of(s, BLK), BLK)
    else:
        mk = lambda s: pl.ds(s, BLK, stride=d)
    return b, mk(start), mk(TILE + start), mk(TILE + start - d * BLK)


def _attn_fwd(q, k, v, side=None, side_args=()):
    n_hp, t_len, _ = q.shape
    nt = t_len // TILE
    ns = side.n if side is not None else 0
    n_steps = n_hp * nt

    def body(*refs):
        q_ref, kp_ref, kc_ref, vp_ref, vc_ref = refs[0:5]
        o_ref, lse_ref = refs[5 + ns:7 + ns]
        kk, vv, m_s, l_s, acc_s = refs[7 + 2 * ns:12 + 2 * ns]
        t = pl.program_id(1)
        step = pl.program_id(0) * nt + t
        if side is not None:
            side.bind(refs[5:5 + ns], refs[7 + ns:7 + 2 * ns], refs[12 + 2 * ns:])
            pl.when(step == 0)(side.start)
            pl.when(step == n_steps // 2)(side.forward)
        kk[0:TILE] = kp_ref[0]
        kk[TILE:] = kc_ref[0]
        vv[0:TILE] = vp_ref[0]
        vv[TILE:] = vc_ref[0]
        head0, band_prev, band_cur = _attn_masks()

        for pi, d in enumerate(DILATIONS):
            def unit(uidx, carry, d=d, pi=pi):
                b, rows_q, rows_c, rows_p = _unit_rows(uidx, d)
                mask = band_cur | (band_prev & ((t > 0) | (b > 0)))
                q2 = _stack_heads(q_ref.at[0][rows_q, :], head0)
                kcat = jnp.concatenate([kk[rows_p, :], kk[rows_c, :]], axis=0).astype(BF16)
                vcat = jnp.concatenate([vv[rows_p, :], vv[rows_c, :]], axis=0).astype(BF16)
                s = jnp.where(mask, _nt(q2, kcat), NEG)
                m = jnp.max(s, axis=1, keepdims=True)
                p = jnp.exp(s - m)
                ls = jnp.sum(p, axis=1, keepdims=True)
                pv = _mm(p.astype(BF16), vcat)
                m_s.at[pi][rows_q, :] = jnp.where(head0, m[0:BLK], m[BLK:])
                l_s.at[pi][rows_q, :] = jnp.where(head0, ls[0:BLK], ls[BLK:])
                acc_s.at[pi][rows_q, :] = jnp.where(head0, pv[0:BLK], pv[BLK:])
                return carry

            lax.fori_loop(0, TILE // BLK, unit, 0, unroll=8)

        m_all = jnp.maximum(jnp.maximum(m_s[0], m_s[1]), m_s[2])
        num = jnp.zeros((TILE, LANES), F32)
        den = jnp.zeros((TILE, LANES), F32)
        for pi in range(len(DILATIONS)):
            wgt = jnp.exp(m_s[pi] - m_all)
            num = num + acc_s[pi] * wgt
            den = den + l_s[pi] * wgt
        o_ref[...] = num / den
        lse_ref[0] = m_all + jnp.log(den)
        if side is not None:
            pl.when(step == n_steps - 1)(side.finish)

    cur = lambda hp, t: (hp, t, 0)
    prev = lambda hp, t: (hp, jnp.maximum(t - 1, 0), 0)
    blk = (1, TILE, LANES)
    per_pattern = pltpu.VMEM((len(DILATIONS), TILE, LANES), F32)
    extra = (side.in_specs(), side.out_specs(), side.out_shapes(), side.scratch_shapes()) if side is not None else ([], [], [], [])
    return pl.pallas_call(
        body, name="attn_fwd", grid=(n_hp, nt),
        in_specs=[pl.BlockSpec(blk, cur), pl.BlockSpec(blk, prev), pl.BlockSpec(blk, cur),
                  pl.BlockSpec(blk, prev), pl.BlockSpec(blk, cur)] + extra[0],
        out_specs=[pl.BlockSpec((TILE, LANES), lambda hp, t: (t, hp)), pl.BlockSpec(blk, cur)] + extra[1],
        out_shape=[jax.ShapeDtypeStruct((t_len, ATTN_W), F32), jax.ShapeDtypeStruct((n_hp, t_len, LANES), F32)] + extra[2],
        scratch_shapes=[pltpu.VMEM((2 * TILE, LANES), F32), pltpu.VMEM((2 * TILE, LANES), F32),
                        per_pattern, per_pattern, per_pattern] + extra[3],
        compiler_params=_params("arbitrary", "arbitrary"),
    )(q, k, k, v, v, *side_args)


def _attn_bwd(q, k, v, o, do, lse, ones_hp, side=None, side_args=()):
    n_hp, t_len, _ = q.shape
    nt = t_len // TILE
    ns = side.n if side is not None else 0
    n_pat = len(DILATIONS)

    def body(*refs):
        q_ref, kp_ref, kc_ref, vp_ref, vc_ref, o_ref, do_ref, lse_ref, bd_ref = refs[0:9]
        dq_ref, dk_ref, dv_ref = refs[9 + ns:12 + ns]
        kk, vv, dq_s, dkc, dkp, dvc, dvp, hold_k, hold_v, dl_s = refs[12 + 2 * ns:22 + 2 * ns]
        t = pl.program_id(1)
        if side is not None:
            side.bind(refs[9:9 + ns], refs[12 + ns:12 + 2 * ns], refs[22 + 2 * ns:])
            pl.when((pl.program_id(0) == 0) & (t == 0))(side.start)

        @pl.when(t < nt)
        def _():
            kk[0:TILE] = kp_ref[0]
            kk[TILE:] = kc_ref[0]
            vv[0:TILE] = vp_ref[0]
            vv[TILE:] = vc_ref[0]
            dl_s[...] = _group_mean(do_ref[...] * o_ref[...], bd_ref[...], 1.0)
            head0, band_prev, band_cur = _attn_masks()

            for pi, d in enumerate(DILATIONS):
                def unit(uidx, carry, d=d, pi=pi):
                    b, rows_q, rows_c, rows_p = _unit_rows(uidx, d)
                    mask = band_cur | (band_prev & ((t > 0) | (b > 0)))
                    q2 = _stack_heads(q_ref.at[0][rows_q, :], head0)
                    do2 = _stack_heads(do_ref[rows_q, :], head0)
                    lse_f = lse_ref.at[0][rows_q, :]
                    dl_f = dl_s[rows_q, :]
                    lse2 = jnp.concatenate([lse_f[:, 0:1], lse_f[:, HEAD:HEAD + 1]], axis=0)
                    dl2 = jnp.concatenate([dl_f[:, 0:1], dl_f[:, HEAD:HEAD + 1]], axis=0)
                    kcat = jnp.concatenate([kk[rows_p, :], kk[rows_c, :]], axis=0).astype(BF16)
                    vcat = jnp.concatenate([vv[rows_p, :], vv[rows_c, :]], axis=0).astype(BF16)
                    p = jnp.where(mask, jnp.exp(_nt(q2, kcat) - lse2), 0.0)
                    ds = (p * (_nt(do2, vcat) - dl2)).astype(BF16)
                    dq2 = _mm(ds, kcat)
                    dq_s.at[pi][rows_q, :] = jnp.where(head0, dq2[0:BLK], dq2[BLK:])
                    dk2 = _tn(ds, q2)
                    dv2 = _tn(p.astype(BF16), do2)
                    dkp.at[pi][rows_q, :] = dk2[0:BLK]
                    dkc.at[pi][rows_q, :] = dk2[BLK:]
                    dvp.at[pi][rows_q, :] = dv2[0:BLK]
                    dvc.at[pi][rows_q, :] = dv2[BLK:]
                    return carry

                lax.fori_loop(0, TILE // BLK, unit, 0, unroll=8)

            dq_ref[...] = dq_s[0] + dq_s[1] + dq_s[2]

        @pl.when(t > 0)
        def _():
            dk_ref[...] = hold_k[...]
            dv_ref[...] = hold_v[...]

        @pl.when((t > 0) & (t < nt))
        def _():
            for pi, d in enumerate(DILATIONS):
                back = d * BLK
                dk_ref[TILE - back:, :] = dk_ref[TILE - back:, :] + dkp[pi, 0:back, :]
                dv_ref[TILE - back:, :] = dv_ref[TILE - back:, :] + dvp[pi, 0:back, :]

        @pl.when(t < nt)
        def _():
            hold_k[...] = dkc[0] + dkc[1] + dkc[2]
            hold_v[...] = dvc[0] + dvc[1] + dvc[2]
            for pi, d in enumerate(DILATIONS):
                back = d * BLK
                if back < TILE:
                    hold_k[0:TILE - back, :] = hold_k[0:TILE - back, :] + dkp[pi, back:, :]
                    hold_v[0:TILE - back, :] = hold_v[0:TILE - back, :] + dvp[pi, back:, :]

        if side is not None:
            pl.when((pl.program_id(0) == n_hp - 1) & (t == nt))(side.finish)

    last = nt - 1
    extra = (side.in_specs(), side.out_specs(), side.out_shapes(), side.scratch_shapes()) if side is not None else ([], [], [], [])
    cur = lambda hp, t: (hp, jnp.minimum(t, last), 0)
    prev = lambda hp, t: (hp, jnp.clip(t - 1, 0, last), 0)
    cur2 = lambda hp, t: (jnp.minimum(t, last), hp)
    prev2 = lambda hp, t: (jnp.maximum(t - 1, 0), hp)
    blk = (1, TILE, LANES)
    blk2 = (TILE, LANES)
    out = jax.ShapeDtypeStruct((t_len, ATTN_W), F32)
    return pl.pallas_call(
        body, name="attn_bwd", grid=(n_hp, nt + 1),
        in_specs=[pl.BlockSpec(blk, cur), pl.BlockSpec(blk, prev), pl.BlockSpec(blk, cur),
                  pl.BlockSpec(blk, prev), pl.BlockSpec(blk, cur), pl.BlockSpec(blk2, cur2),
                  pl.BlockSpec(blk2, cur2), pl.BlockSpec(blk, cur), pl.BlockSpec((LANES, LANES), lambda hp, t: (0, 0))]
        + extra[0],
        out_specs=[pl.BlockSpec(blk2, cur2), pl.BlockSpec(blk2, prev2), pl.BlockSpec(blk2, prev2)] + extra[1],
        out_shape=[out, out, out] + extra[2],
        scratch_shapes=[pltpu.VMEM((2 * TILE, LANES), F32), pltpu.VMEM((2 * TILE, LANES), F32)]
        + [pltpu.VMEM((n_pat, TILE, LANES), F32)] * 5 + [pltpu.VMEM((TILE, LANES), F32)] * 3 + extra[3],
        compiler_params=_params("arbitrary", "arbitrary"),
    )(q, k, k, v, v, o, do, lse, ones_hp, *side_args)


def _discretise(lr, li, ldt, br, bi):
    dt = jnp.exp(ldt)
    mag = jnp.exp(lr * dt)
    ab_r, ab_i = mag * jnp.cos(li * dt), mag * jnp.sin(li * dt)
    den = lr * lr + li * li
    nr, ni = ab_r - 1.0, ab_i
    cr = (nr * lr + ni * li) / den
    ci = (ni * lr - nr * li) / den
    return ab_r, ab_i, cr * br - ci * bi, cr * bi + ci * br


def _disc_fwd(lr, li, ldt, br, bi):
    def body(lr_ref, li_ref, ldt_ref, br_ref, bi_ref, ar_o, ai_o, bbr_o, bbi_o):
        outs = _discretise(lr_ref[...], li_ref[...], ldt_ref[...], br_ref[...], bi_ref[...])
        for o_ref, val in zip((ar_o, ai_o, bbr_o, bbi_o), outs):
            o_ref[...] = val

    col = jax.ShapeDtypeStruct(lr.shape, F32)
    mat = jax.ShapeDtypeStruct(br.shape, F32)
    return pl.pallas_call(body, name="s5_disc_fwd", out_shape=[col, col, mat, mat])(lr, li, ldt, br, bi)


def _disc_bwd(lr, li, ldt, br, bi, d_ar, d_ai, d_bbr, d_bbi, group_sum):
    def body(lr_ref, li_ref, ldt_ref, br_ref, bi_ref, c1, c2, c3, c4, gs_ref, dlr_o, dli_o, dldt_o, dbr_o, dbi_o):
        _, vjp = jax.vjp(_discretise, lr_ref[...], li_ref[...], ldt_ref[...], br_ref[...], bi_ref[...])
        dlr, dli, dldt, dbr, dbi = vjp((c1[...], c2[...], c3[...], c4[...]))
        dlr_o[...] = dlr
        dli_o[...] = dli
        dbr_o[...] = dbr
        dbi_o[...] = dbi
        wide = jnp.broadcast_to(dldt, (dldt.shape[0], LANES))
        dldt_o[...] = jnp.dot(gs_ref[...], wide, precision=lax.Precision.HIGHEST, preferred_element_type=F32)

    col = jax.ShapeDtypeStruct(lr.shape, F32)
    mat = jax.ShapeDtypeStruct(br.shape, F32)
    return pl.pallas_call(
        body, name="s5_disc_bwd", out_shape=[col, col, jax.ShapeDtypeStruct((N_GROUPS, LANES), F32), mat, mat],
    )(lr, li, ldt, br, bi, d_ar, d_ai, d_bbr, d_bbi, group_sum)


N_CHUNK = TILE // BLK
HALF = 4


def _cmul(ar, ai, xr, xi):
    return ar * xr - ai * xi, ar * xi + ai * xr


def _power_table(a_ref, tab, sign, reverse):
    ar = [a_ref[0, j:j + 1, :] for j in range(HALF)]
    ai = [sign * a_ref[0, HALF + j:HALF + j + 1, :] for j in range(HALF)]

    def step(s, cur):
        row = pl.ds((BLK - 1 - s) if reverse else s, 1)
        nxt = []
        for j in range(HALF):
            tab.at[j][row, :] = cur[j]
            tab.at[HALF + j][row, :] = cur[HALF + j]
            nxt.append(_cmul(ar[j], ai[j], cur[j], cur[HALF + j]))
        return tuple(p[0] for p in nxt) + tuple(p[1] for p in nxt)

    lax.fori_loop(0, BLK, step, tuple(ar) + tuple(ai))


def _interleave(src, dst):
    for c in range(N_CHUNK):
        dst[pl.ds(c, BLK, stride=N_CHUNK), :] = src[c * BLK:(c + 1) * BLK, :]


def _deinterleave(src, dst):
    for c in range(N_CHUNK):
        dst[c * BLK:(c + 1) * BLK, :] = src[pl.ds(c, BLK, stride=N_CHUNK), :]


def _step_rows(s):
    return pl.ds(pl.multiple_of(s * N_CHUNK, N_CHUNK), N_CHUNK)


def _chunk_scan(buf, a_ref, sign, reverse):
    ar = [jnp.broadcast_to(a_ref[0, j:j + 1, :], (N_CHUNK, LANES)) for j in range(HALF)]
    ai = [sign * jnp.broadcast_to(a_ref[0, HALF + j:HALF + j + 1, :], (N_CHUNK, LANES)) for j in range(HALF)]

    def step(i, carry):
        s = (BLK - 1 - i) if reverse else i
        rows = _step_rows(s)
        out = []
        for j in range(HALF):
            pr, pi = _cmul(ar[j], ai[j], carry[j], carry[HALF + j])
            xr = buf.at[j][rows, :] + pr
            xi = buf.at[HALF + j][rows, :] + pi
            buf.at[j][rows, :] = xr
            buf.at[HALF + j][rows, :] = xi
            out.append((xr, xi))
        return tuple(p[0] for p in out) + tuple(p[1] for p in out)

    zero = jnp.zeros((N_CHUNK, LANES), F32)
    lax.fori_loop(0, BLK, step, (zero,) * (2 * HALF), unroll=2)


def _chunk_states(buf, carry_s, xin_s, tab, reverse):
    edge = 0 if reverse else BLK - 1
    top = 0 if reverse else BLK - 1
    pw = [tab[j, top:top + 1, :] for j in range(2 * HALF)]
    cur = [carry_s[j:j + 1, :] for j in range(2 * HALF)]
    summary = [buf[j, edge * N_CHUNK:(edge + 1) * N_CHUNK, :] for j in range(2 * HALF)]
    order = range(N_CHUNK - 1, -1, -1) if reverse else range(N_CHUNK)
    for c in order:
        for j in range(2 * HALF):
            xin_s[j, c:c + 1, :] = cur[j]
        nxt = []
        for j in range(HALF):
            pr, pi = _cmul(pw[j], pw[HALF + j], cur[j], cur[HALF + j])
            nxt.append((pr + summary[j][c:c + 1, :], pi + summary[HALF + j][c:c + 1, :]))
        cur = [p[0] for p in nxt] + [p[1] for p in nxt]
    for j in range(2 * HALF):
        carry_s[j:j + 1, :] = cur[j]


def _s5_fwd(u, a_cat, b_mat, c_mat, d_skip):
    t_len = u.shape[0]
    nt = t_len // TILE

    def body(u_ref, a_ref, b_ref, c_ref, d_ref, y_ref, x_ref, xs, us, tab, carry_s, xin_s):
        sb = pl.program_id(1)

        @pl.when(sb == 0)
        def _():
            _power_table(a_ref, tab, 1.0, False)
            carry_s[...] = jnp.zeros_like(carry_s)

        _interleave(u_ref, us)
        uv = us[...]
        bu = _mm(uv.astype(BF16), b_ref[0])
        for j in range(2 * HALF):
            xs[j] = bu[:, j * LANES:(j + 1) * LANES]
        _chunk_scan(xs, a_ref, 1.0, False)
        _chunk_states(xs, carry_s, xin_s, tab, False)
        xin = [xin_s[j] for j in range(2 * HALF)]

        def fix(s, acc):
            rows = _step_rows(s)
            for j in range(HALF):
                pr, pi = _cmul(tab.at[j][pl.ds(s, 1), :], tab.at[HALF + j][pl.ds(s, 1), :], xin[j], xin[HALF + j])
                xs.at[j][rows, :] = xs.at[j][rows, :] + pr
                xs.at[HALF + j][rows, :] = xs.at[HALF + j][rows, :] + pi
            return acc

        lax.fori_loop(0, BLK, fix, 0, unroll=2)
        xcat = jnp.concatenate([xs[j].astype(BF16) for j in range(2 * HALF)], axis=1)
        x_ref[0] = xcat
        us[...] = d_ref[0] * uv + _mm(xcat, c_ref[0])
        _deinterleave(us, y_ref)

    return pl.pallas_call(
        body, name="s5_fwd", grid=(N_LB, nt),
        in_specs=[pl.BlockSpec((TILE, LANES), lambda lb, sb: (sb, lb)),
                  pl.BlockSpec((1, 2 * HALF, LANES), lambda lb, sb: (lb, 0, 0)),
                  pl.BlockSpec((1, LANES, 2 * HALF * LANES), lambda lb, sb: (lb, 0, 0)),
                  pl.BlockSpec((1, 2 * HALF * LANES, LANES), lambda lb, sb: (lb, 0, 0)),
                  pl.BlockSpec((1, 1, LANES), lambda lb, sb: (lb, 0, 0))],
        out_specs=[pl.BlockSpec((TILE, LANES), lambda lb, sb: (sb, lb)),
                   pl.BlockSpec((1, TILE, 2 * HALF * LANES), lambda lb, sb: (lb, sb, 0))],
        out_shape=[jax.ShapeDtypeStruct((t_len, SSM_W), F32), jax.ShapeDtypeStruct((N_LB, t_len, 2 * HALF * LANES), BF16)],
        scratch_shapes=[pltpu.VMEM((2 * HALF, TILE, LANES), F32), pltpu.VMEM((TILE, LANES), F32),
                        pltpu.VMEM((2 * HALF, BLK, LANES), F32),
                        pltpu.VMEM((2 * HALF, LANES), F32), pltpu.VMEM((2 * HALF, N_CHUNK, LANES), F32)],
        compiler_params=_params("arbitrary", "arbitrary"),
    )(u, a_cat, b_mat, c_mat, d_skip)


def _s5_bwd(u, dy, states, a_cat, b_mat, c_mat, d_skip):
    t_len = u.shape[0]
    nt = t_len // TILE
    last = nt - 1

    def body(u_ref, dy_ref, x_ref, a_ref, b_ref, c_ref, d_ref, du_ref, db_ref, dc_ref, da_ref, dd_ref,
             gs, us, dys, tabc, lam_s, lin_s):
        sb = pl.program_id(1)

        @pl.when(sb == 0)
        def _():
            _power_table(a_ref, tabc, -1.0, True)
            lam_s[...] = jnp.zeros_like(lam_s)
            db_ref[...] = jnp.zeros_like(db_ref)
            dc_ref[...] = jnp.zeros_like(dc_ref)
            da_ref[...] = jnp.zeros_like(da_ref)
            dd_ref[...] = jnp.zeros_like(dd_ref)

        _interleave(u_ref, us)
        _interleave(dy_ref, dys)
        uv = us[...]
        dyv = dys[...]
        ub = uv.astype(BF16)
        dyb = dyv.astype(BF16)
        gy = _nt(dyb, c_ref[0])
        for j in range(2 * HALF):
            gs[j] = gy[:, j * LANES:(j + 1) * LANES]
        _chunk_scan(gs, a_ref, -1.0, True)
        _chunk_states(gs, lam_s, lin_s, tabc, True)
        zero = jnp.zeros((N_CHUNK, LANES), F32)
        x_tile = x_ref.at[0]
        for grp in range(0, HALF, 2):
            slabs = (grp, grp + 1)
            lin = [(lin_s[j], lin_s[HALF + j]) for j in slabs]

            def fix(i, carry, slabs=slabs, lin=lin):
                s = BLK - 1 - i
                rows = _step_rows(s)
                out = []
                for k, j in enumerate(slabs):
                    nr, ni, acc_r, acc_i = carry[4 * k:4 * k + 4]
                    xr = x_tile[rows, pl.ds(j * LANES, LANES)].astype(F32)
                    xi = x_tile[rows, pl.ds((HALF + j) * LANES, LANES)].astype(F32)
                    qr, qi = _cmul(tabc.at[j][pl.ds(s, 1), :], tabc.at[HALF + j][pl.ds(s, 1), :], lin[k][0], lin[k][1])
                    lr_ = gs.at[j][rows, :] + qr
                    li_ = gs.at[HALF + j][rows, :] + qi
                    gs.at[j][rows, :] = lr_
                    gs.at[HALF + j][rows, :] = li_
                    out += [lr_, li_, acc_r + (xr * nr + xi * ni), acc_i + (xr * ni - xi * nr)]
                return tuple(out)

            init = []
            for k in range(len(slabs)):
                init += [lin[k][0], lin[k][1], zero, zero]
            res = lax.fori_loop(0, BLK, fix, tuple(init), unroll=2)
            for k, j in enumerate(slabs):
                da_ref[0, j:j + 1, :] = da_ref[0, j:j + 1, :] + _colsum(res[4 * k + 2])
                da_ref[0, HALF + j:HALF + j + 1, :] = da_ref[0, HALF + j:HALF + j + 1, :] + _colsum(res[4 * k + 3])
        lam = jnp.concatenate([gs[j].astype(BF16) for j in range(2 * HALF)], axis=1)
        us[...] = _nt(lam, b_ref[0]) + d_ref[0] * dyv
        _deinterleave(us, du_ref)
        db_ref[0] = db_ref[0] + _tn(ub, lam)
        dc_ref[0] = dc_ref[0] + _tn(dyb, x_ref[0])
        dd_ref[0] = dd_ref[0] + _colsum(dyv * uv)

    rev = lambda lb, sb: (last - sb, lb)
    per_lb = lambda lb, sb: (lb, 0, 0)
    wide = 2 * HALF * LANES
    return pl.pallas_call(
        body, name="s5_bwd", grid=(N_LB, nt),
        in_specs=[pl.BlockSpec((TILE, LANES), rev), pl.BlockSpec((TILE, LANES), rev),
                  pl.BlockSpec((1, TILE, wide), lambda lb, sb: (lb, last - sb, 0)),
                  pl.BlockSpec((1, 2 * HALF, LANES), per_lb), pl.BlockSpec((1, LANES, wide), per_lb),
                  pl.BlockSpec((1, wide, LANES), per_lb), pl.BlockSpec((1, 1, LANES), per_lb)],
        out_specs=[pl.BlockSpec((TILE, LANES), rev), pl.BlockSpec((1, LANES, wide), per_lb),
                   pl.BlockSpec((1, LANES, wide), per_lb), pl.BlockSpec((1, 2 * HALF, LANES), per_lb),
                   pl.BlockSpec((1, 1, LANES), per_lb)],
        out_shape=[jax.ShapeDtypeStruct((t_len, SSM_W), F32), jax.ShapeDtypeStruct((N_LB, LANES, wide), F32),
                   jax.ShapeDtypeStruct((N_LB, LANES, wide), F32), jax.ShapeDtypeStruct((N_LB, 2 * HALF, LANES), F32),
                   jax.ShapeDtypeStruct((N_LB, 1, LANES), F32)],
        scratch_shapes=[pltpu.VMEM((2 * HALF, TILE, LANES), F32),
                        pltpu.VMEM((TILE, LANES), F32), pltpu.VMEM((TILE, LANES), F32),
                        pltpu.VMEM((2 * HALF, BLK, LANES), F32), pltpu.VMEM((2 * HALF, LANES), F32),
                        pltpu.VMEM((2 * HALF, N_CHUNK, LANES), F32)],
        compiler_params=_params("arbitrary", "arbitrary"),
    )(u, dy, states, a_cat, b_mat, c_mat, d_skip)


_GELU_C = math.sqrt(2.0 / math.pi)
_GELU_K = 0.044715


def _gelu(y):
    t = jnp.tanh(_GELU_C * (y + _GELU_K * (y * y * y)))
    return y * (0.5 * (1.0 + t)), t


def _gelu_grad(y, t):
    return 0.5 * (1.0 + t) + 0.5 * y * (1.0 - t * t) * (_GELU_C * (1.0 + 3.0 * _GELU_K * y * y))


def _glu(y, wg, bias):
    z, t = _gelu(y)
    sg = jax.nn.sigmoid(_mm(z.astype(BF16), wg) + bias)
    return z, t, sg


def _mix_fwd(attn, y, x, wg, glu_b, ga, gs, wo):
    t_len = x.shape[0]
    tm = 512

    def body(attn_ref, y_ref, x_ref, wg_ref, b_ref, ga_ref, gs_ref, wo_ref, x2_ref, mix_ref, z_ref):
        z, _, sg = _glu(y_ref[...], wg_ref[...], b_ref[...])
        z_ref[...] = z.astype(BF16)
        s = z * sg
        av = attn_ref[...]
        an = (av * _rms(av) * ga_ref[...]).astype(BF16)
        sn = (s * _rms(s) * gs_ref[...]).astype(BF16)
        mix_ref[:, 0:ATTN_W] = an
        mix_ref[:, ATTN_W:] = sn
        x2_ref[...] = x_ref[...] + _mm(an, wo_ref[0:ATTN_W, :]) + _mm(sn, wo_ref[ATTN_W:, :])

    row = lambda i: (i, 0)
    const = lambda i: (0, 0)
    return pl.pallas_call(
        body, name="mix_fwd", grid=(t_len // tm,),
        in_specs=[pl.BlockSpec((tm, ATTN_W), row), pl.BlockSpec((tm, SSM_W), row), pl.BlockSpec((tm, D_MODEL), row),
                  pl.BlockSpec((SSM_W, SSM_W), const), pl.BlockSpec((1, SSM_W), const), pl.BlockSpec((1, ATTN_W), const),
                  pl.BlockSpec((1, SSM_W), const), pl.BlockSpec((D_MODEL, D_MODEL), const)],
        out_specs=[pl.BlockSpec((tm, D_MODEL), row), pl.BlockSpec((tm, D_MODEL), row), pl.BlockSpec((tm, SSM_W), row)],
        out_shape=[jax.ShapeDtypeStruct((t_len, D_MODEL), F32), jax.ShapeDtypeStruct((t_len, D_MODEL), BF16),
                   jax.ShapeDtypeStruct((t_len, SSM_W), BF16)],
        compiler_params=_params("arbitrary"),
    )(attn, y, x, wg, glu_b, ga, gs, wo)


def _mlp(x2, target, g2, wu, wd):
    t_len = x2.shape[0]
    tm = 256
    fc = 1024
    n_fc = D_FF // fc

    def body(x2_ref, tg_ref, g2_ref, wu_hbm, wd_hbm, dx2_ref, hdn_ref, dup_ref, h_ref, dyb_ref, dg2_ref, loss_ref,
             wu_s, wd_s, relu_s, sem):
        @pl.when(pl.program_id(0) == 0)
        def _():
            cu = pltpu.make_async_copy(wu_hbm, wu_s, sem.at[0])
            cd = pltpu.make_async_copy(wd_hbm, wd_s, sem.at[1])
            cu.start()
            cd.start()
            cu.wait()
            cd.wait()
            dg2_ref[...] = jnp.zeros_like(dg2_ref)
            loss_ref[...] = jnp.zeros_like(loss_ref)

        x2v = x2_ref[...]
        r = _rms(x2v)
        g2v = g2_ref[...]
        h = (x2v * r * g2v).astype(BF16)
        h_ref[...] = h
        yout = x2v
        for c in range(n_fc):
            cols = slice(c * fc, (c + 1) * fc)
            ru = jnp.maximum(_mm(h, wu_s[:, cols]), 0.0)
            relu_s[:, cols] = ru
            hd = (ru * ru).astype(BF16)
            hdn_ref[:, cols] = hd
            yout = yout + _mm(hd, wd_s[cols, :])
        err = yout - tg_ref[...]
        loss_ref[...] = loss_ref[...] + 0.5 * jnp.sum(err * err) * (1.0 / D_MODEL)
        dy = err * (1.0 / D_MODEL)
        dyb = dy.astype(BF16)
        dyb_ref[...] = dyb
        dh = jnp.zeros((tm, D_MODEL), F32)
        for c in range(n_fc):
            cols = slice(c * fc, (c + 1) * fc)
            dup = (_nt(dyb, wd_s[cols, :]) * (2.0 * relu_s[:, cols])).astype(BF16)
            dup_ref[:, cols] = dup
            dh = dh + _nt(dup, wu_s[:, cols])
        dxn, gterm = _rms_bwd(dh, x2v, r, g2v)
        dx2_ref[...] = dy + dxn
        dg2_ref[...] = dg2_ref[...] + _colsum(gterm)

    row = lambda i: (i, 0)
    const = lambda i: (0, 0)
    any_spec = pl.BlockSpec(memory_space=pl.ANY)
    return pl.pallas_call(
        body, name="mlp", grid=(t_len // tm,),
        in_specs=[pl.BlockSpec((tm, D_MODEL), row), pl.BlockSpec((tm, D_MODEL), row), pl.BlockSpec((1, D_MODEL), const),
                  any_spec, any_spec],
        out_specs=[pl.BlockSpec((tm, D_MODEL), row), pl.BlockSpec((tm, D_FF), row), pl.BlockSpec((tm, D_FF), row),
                   pl.BlockSpec((tm, D_MODEL), row), pl.BlockSpec((tm, D_MODEL), row), pl.BlockSpec((1, D_MODEL), const),
                   pl.BlockSpec((1, LANES), const)],
        out_shape=[jax.ShapeDtypeStruct((t_len, D_MODEL), F32), jax.ShapeDtypeStruct((t_len, D_FF), BF16),
                   jax.ShapeDtypeStruct((t_len, D_FF), BF16), jax.ShapeDtypeStruct((t_len, D_MODEL), BF16),
                   jax.ShapeDtypeStruct((t_len, D_MODEL), BF16), jax.ShapeDtypeStruct((1, D_MODEL), F32),
                   jax.ShapeDtypeStruct((1, LANES), F32)],
        scratch_shapes=[pltpu.VMEM((D_MODEL, D_FF), BF16), pltpu.VMEM((D_FF, D_MODEL), BF16),
                        pltpu.VMEM((tm, D_FF), F32), pltpu.SemaphoreType.DMA((2,))],
        compiler_params=_params("arbitrary"),
    )(x2, target, g2, wu, wd)


def _mix_bwd(dx2, attn, y, wg, glu_b, ga, gs, wo):
    t_len = dx2.shape[0]
    tm = 512

    def body(dx2_ref, attn_ref, y_ref, wg_ref, b_ref, ga_ref, gs_ref, wo_ref,
             dattn_ref, dy_ref, dx2b_ref, dgp_ref, dga_ref, dgs_ref, db_ref):
        @pl.when(pl.program_id(0) == 0)
        def _():
            dga_ref[...] = jnp.zeros_like(dga_ref)
            dgs_ref[...] = jnp.zeros_like(dgs_ref)
            db_ref[...] = jnp.zeros_like(db_ref)

        dx2b = dx2_ref[...].astype(BF16)
        dx2b_ref[...] = dx2b
        d_an = _nt(dx2b, wo_ref[0:ATTN_W, :])
        d_sn = _nt(dx2b, wo_ref[ATTN_W:, :])
        yv = y_ref[...]
        wg = wg_ref[...]
        z, t, sg = _glu(yv, wg, b_ref[...])
        s = z * sg
        av = attn_ref[...]
        d_attn, ga_term = _rms_bwd(d_an, av, _rms(av), ga_ref[...])
        d_s, gs_term = _rms_bwd(d_sn, s, _rms(s), gs_ref[...])
        dattn_ref[...] = d_attn
        dgp = d_s * z * sg * (1.0 - sg)
        dgpb = dgp.astype(BF16)
        dgp_ref[...] = dgpb
        dz = d_s * sg + _nt(dgpb, wg)
        dy_ref[...] = dz * _gelu_grad(yv, t)
        dga_ref[...] = dga_ref[...] + _colsum(ga_term)
        dgs_ref[...] = dgs_ref[...] + _colsum(gs_term)
        db_ref[...] = db_ref[...] + _colsum(dgp)

    row = lambda i: (i, 0)
    const = lambda i: (0, 0)
    vec = jax.ShapeDtypeStruct((1, SSM_W), F32)
    return pl.pallas_call(
        body, name="mix_bwd", grid=(t_len // tm,),
        in_specs=[pl.BlockSpec((tm, D_MODEL), row), pl.BlockSpec((tm, ATTN_W), row), pl.BlockSpec((tm, SSM_W), row),
                  pl.BlockSpec((SSM_W, SSM_W), const), pl.BlockSpec((1, SSM_W), const), pl.BlockSpec((1, ATTN_W), const),
                  pl.BlockSpec((1, SSM_W), const), pl.BlockSpec((D_MODEL, D_MODEL), const)],
        out_specs=[pl.BlockSpec((tm, ATTN_W), row), pl.BlockSpec((tm, SSM_W), row), pl.BlockSpec((tm, D_MODEL), row),
                   pl.BlockSpec((tm, SSM_W), row), pl.BlockSpec((1, ATTN_W), const), pl.BlockSpec((1, SSM_W), const),
                   pl.BlockSpec((1, SSM_W), const)],
        out_shape=[jax.ShapeDtypeStruct((t_len, ATTN_W), F32), jax.ShapeDtypeStruct((t_len, SSM_W), F32),
                   jax.ShapeDtypeStruct((t_len, D_MODEL), BF16), jax.ShapeDtypeStruct((t_len, SSM_W), BF16), vec, vec, vec],
        compiler_params=_params("arbitrary"),
    )(dx2, attn, y, wg, glu_b, ga, gs, wo)


def _inproj_bwd(dqs, dkn, dv, du, q_raw, k_raw, x, dx2, wi, g1, gq, gk, ones64):
    t_len = x.shape[0]
    tm = 512
    n_heads = ATTN_W // HEAD

    def body(dqs_ref, dkn_ref, dv_ref, du_ref, q_ref, k_ref, x_ref, dx2_ref, wi_ref, g1_ref, gq_ref, gk_ref, bd_ref,
             gx_ref, dproj_ref, dg1_ref, dgq_ref, dgk_ref, accq, acck):
        i = pl.program_id(0)

        @pl.when(i == 0)
        def _():
            dg1_ref[...] = jnp.zeros_like(dg1_ref)
            accq[...] = jnp.zeros_like(accq)
            acck[...] = jnp.zeros_like(acck)

        bd = bd_ref[...]

        def head_norm_bwd(dy, raw, gain, acc):
            r = lax.rsqrt(_group_mean(raw * raw, bd, HEAD) + EPS)
            xh = raw * r
            dxh = dy * gain
            acc[...] = acc[...] + _colsum(dy * xh)
            return r * (dxh - xh * _group_mean(dxh * xh, bd, HEAD))

        dq = head_norm_bwd(dqs_ref[...] * (HEAD ** -0.5), q_ref[...], gq_ref[...], accq)
        dk = head_norm_bwd(dkn_ref[...], k_ref[...], gk_ref[...], acck)
        dproj_ref[:, 0:ATTN_W] = dq.astype(BF16)
        dproj_ref[:, ATTN_W:2 * ATTN_W] = dk.astype(BF16)
        dproj_ref[:, 2 * ATTN_W:3 * ATTN_W] = dv_ref[...].astype(BF16)
        dproj_ref[:, 3 * ATTN_W:] = du_ref[...].astype(BF16)
        dxn = _nt(dproj_ref[...], wi_ref[...])
        xv = x_ref[...]
        g1v = g1_ref[...]
        dx, g1_term = _rms_bwd(dxn, xv, _rms(xv), g1v)
        gx_ref[...] = dx2_ref[...] + dx
        dg1_ref[...] = dg1_ref[...] + _colsum(g1_term)

        @pl.when(i == pl.num_programs(0) - 1)
        def _():
            for acc, out in ((accq, dgq_ref), (acck, dgk_ref)):
                tot = acc[:, 0:HEAD]
                for h in range(1, n_heads):
                    tot = tot + acc[:, h * HEAD:(h + 1) * HEAD]
                out[...] = tot

    row = lambda i: (i, 0)
    const = lambda i: (0, 0)
    aw = pl.BlockSpec((tm, ATTN_W), row)
    dm = pl.BlockSpec((tm, D_MODEL), row)
    return pl.pallas_call(
        body, name="inproj_bwd", grid=(t_len // tm,),
        in_specs=[aw, aw, aw, aw, aw, aw, dm, dm, pl.BlockSpec((D_MODEL, PROJ_W), const), pl.BlockSpec((1, D_MODEL), const),
                  pl.BlockSpec((1, ATTN_W), const), pl.BlockSpec((1, ATTN_W), const), pl.BlockSpec((ATTN_W, ATTN_W), const)],
        out_specs=[dm, pl.BlockSpec((tm, PROJ_W), row), pl.BlockSpec((1, D_MODEL), const),
                   pl.BlockSpec((1, HEAD), const), pl.BlockSpec((1, HEAD), const)],
        out_shape=[jax.ShapeDtypeStruct((t_len, D_MODEL), F32), jax.ShapeDtypeStruct((t_len, PROJ_W), BF16),
                   jax.ShapeDtypeStruct((1, D_MODEL), F32), jax.ShapeDtypeStruct((1, HEAD), F32),
                   jax.ShapeDtypeStruct((1, HEAD), F32)],
        scratch_shapes=[pltpu.VMEM((1, ATTN_W), F32), pltpu.VMEM((1, ATTN_W), F32)],
        compiler_params=_params("arbitrary"),
    )(dqs, dkn, dv, du, q_raw, k_raw, x, dx2, wi, g1, gq, gk, ones64)


def _grad_matmul(a, b, name):
    t_len, m = a.shape
    n = b.shape[1]
    bm, bn, bt = min(m, 1024), min(n, 1024), 1024

    def body(a_ref, b_ref, o_ref):
        @pl.when(pl.program_id(2) == 0)
        def _():
            o_ref[...] = jnp.zeros_like(o_ref)

        o_ref[...] = o_ref[...] + _tn(a_ref[...], b_ref[...])

    return pl.pallas_call(
        body, name=name, grid=(m // bm, n // bn, t_len // bt),
        in_specs=[pl.BlockSpec((bt, bm), lambda i, j, k: (k, i)), pl.BlockSpec((bt, bn), lambda i, j, k: (k, j))],
        out_specs=pl.BlockSpec((bm, bn), lambda i, j, k: (i, j)),
        out_shape=jax.ShapeDtypeStruct((m, n), F32),
        compiler_params=_params("arbitrary", "arbitrary", "arbitrary"),
    )(a, b)


def _adamw_update(w_ref, g_ref, m_ref, v_ref, d_o, m_o, v_o):
    gv = g_ref[...]
    mn = ADAM_B1 * m_ref[...] + (1.0 - ADAM_B1) * gv
    vn = ADAM_B2 * v_ref[...] + (1.0 - ADAM_B2) * jnp.square(gv)
    m_hat = mn / (1.0 - ADAM_B1 ** ADAM_STEP)
    v_hat = vn / (1.0 - ADAM_B2 ** ADAM_STEP)
    d_o[...] = -ADAM_LR * (m_hat / (jnp.sqrt(v_hat) + ADAM_EPS) + ADAM_WD * w_ref[...])
    m_o[...] = mn
    v_o[...] = vn


def _adamw_many(ws, gs, ms, vs):
    n = len(ws)

    def body(*refs):
        for i in range(n):
            _adamw_update(*[refs[k * n + i] for k in range(7)])

    shapes = [jax.ShapeDtypeStruct(w.shape, F32) for w in ws]
    outs = pl.pallas_call(body, name="adamw_small", out_shape=shapes * 3,
                          compiler_params=pltpu.CompilerParams(vmem_limit_bytes=VMEM_LIMIT))(*ws, *gs, *ms, *vs)
    return outs[0:n], outs[n:2 * n], outs[2 * n:]


def _adamw(w, g, m, v, name):
    rows, cols = w.shape
    br = _row_block(rows, 256)
    body = functools.partial(_adamw_update)

    spec = pl.BlockSpec((br, cols), lambda i: (i, 0))
    shape = jax.ShapeDtypeStruct((rows, cols), F32)
    return pl.pallas_call(
        body, name=name, grid=(rows // br,), in_specs=[spec] * 4, out_specs=[spec] * 3, out_shape=[shape] * 3,
        compiler_params=_params("arbitrary"),
    )(w, g, m, v)


def _sum_arrays(arrs, name, out_dtype=F32):
    rows, cols = arrs[0].shape
    br = _row_block(rows, 512)
    n = len(arrs)

    def body(*refs):
        tot = refs[0][...]
        for r in refs[1:n]:
            tot = tot + r[...]
        refs[n][...] = tot.astype(out_dtype)

    spec = pl.BlockSpec((br, cols), lambda i: (i, 0))
    return pl.pallas_call(
        body, name=name, grid=(rows // br,), in_specs=[spec] * n, out_specs=spec,
        out_shape=jax.ShapeDtypeStruct((rows, cols), out_dtype), compiler_params=_params("arbitrary"),
    )(*arrs)


GPL = N_GROUPS // N_LB
SW = GPL * N_STATE


def _eye_groups():
    return jnp.eye(GPL, dtype=F32)


def _s5_matrices(ab_r, ab_i, bb_r, bb_i, c_re, c_im, d_skip):
    eye = _eye_groups()
    a_cat = jnp.concatenate([ab_r.reshape(N_LB, HALF, LANES), ab_i.reshape(N_LB, HALF, LANES)], axis=1)

    def b_part(bb):
        b4 = jnp.transpose(bb.reshape(N_LB, GPL, N_STATE, GROUP_W), (0, 1, 3, 2))
        return (b4[:, :, :, None, :] * eye[None, :, None, :, None]).reshape(N_LB, LANES, SW)

    def c_part(cc):
        c4 = jnp.transpose(cc.reshape(N_LB, GPL, GROUP_W, N_STATE), (0, 1, 3, 2))
        return (c4[:, :, :, None, :] * eye[None, :, None, :, None]).reshape(N_LB, SW, LANES)

    b_mat = jnp.concatenate([b_part(bb_r), b_part(bb_i)], axis=2).astype(BF16)
    c_mat = jnp.concatenate([c_part(c_re), -c_part(c_im)], axis=1).astype(BF16)
    return a_cat, b_mat, c_mat, d_skip.reshape(N_LB, 1, LANES)


def _s5_unpack_grads(db, dc, da, dd):
    eye = _eye_groups()
    mask = eye[None, :, None, None, :, None]
    d6 = jnp.sum(db.reshape(N_LB, GPL, GROUP_W, 2, GPL, N_STATE) * mask, axis=4)
    dbb = jnp.transpose(d6, (3, 0, 1, 4, 2)).reshape(2, N_GROUPS * N_STATE, GROUP_W)
    c6 = jnp.sum(dc.reshape(N_LB, GPL, GROUP_W, 2, GPL, N_STATE) * mask, axis=4)
    dcc = jnp.transpose(c6, (3, 0, 1, 2, 4)).reshape(2, N_GROUPS, GROUP_W, N_STATE)
    dab_r = da[:, :HALF].reshape(N_GROUPS * N_STATE, 1)
    dab_i = da[:, HALF:].reshape(N_GROUPS * N_STATE, 1)
    return dab_r, dab_i, dbb[0], dbb[1], dcc[0], -dcc[1], dd.reshape(N_GROUPS, GROUP_W)


def _block_ones(n, width):
    i = lax.broadcasted_iota(jnp.int32, (n, n), 0) // width
    j = lax.broadcasted_iota(jnp.int32, (n, n), 1) // width
    return (i == j).astype(BF16)


def _tile_heads(g):
    return jnp.tile(g.reshape(1, HEAD), (1, ATTN_W // HEAD))


def _local_step(x, target, wi, rest, p, fwd_side=None, bwd_side=None):
    ones64 = _block_ones(ATTN_W, HEAD)
    ones_hp = _block_ones(LANES, HEAD)
    g1 = p["norm1_g"].reshape(1, D_MODEL)
    g2 = p["norm2_g"].reshape(1, D_MODEL)
    gq = _tile_heads(p["q_norm_g"])
    gk = _tile_heads(p["k_norm_g"])
    ga = p["attn_out_norm_g"].reshape(1, ATTN_W)
    gs = p["ssm_out_norm_g"].reshape(1, SSM_W)
    glu_b = p["glu_b"].reshape(1, SSM_W)
    n_gp = N_GROUPS * N_STATE
    lr = p["ssm_a_re"].reshape(n_gp, 1)
    li = p["ssm_a_im"].reshape(n_gp, 1)
    ldt = jnp.repeat(p["ssm_log_dt"].reshape(N_GROUPS), N_STATE).reshape(n_gp, 1)
    br = p["ssm_b_re"].reshape(n_gp, GROUP_W)
    bi = p["ssm_b_im"].reshape(n_gp, GROUP_W)
    ab_r, ab_i, bb_r, bb_i = _disc_fwd(lr, li, ldt, br, bi)
    a_cat, b_mat, c_mat, d_mat = _s5_matrices(
        ab_r, ab_i, bb_r, bb_i, p["ssm_c_re"].reshape(N_GROUPS, GROUP_W, N_STATE),
        p["ssm_c_im"].reshape(N_GROUPS, GROUP_W, N_STATE), p["ssm_d"])

    xn, qn, kn, vv, u, q_raw, k_raw = _inproj_fwd(x, g1, wi, gq, gk, ones64)
    if fwd_side is None:
        attn, lse = _attn_fwd(qn, kn, vv)
    else:
        attn, lse, *rest = _attn_fwd(qn, kn, vv, *fwd_side)
    wg, wo, wu, wd = rest
    y, states = _s5_fwd(u, a_cat, b_mat, c_mat, d_mat)
    x2, mix, z = _mix_fwd(attn, y, x, wg, glu_b, ga, gs, wo)
    dx2, hdn, dup, h, dyb, dg2, loss = _mlp(x2, target, g2, wu, wd)
    big = {"w_mlp_up": _grad_matmul(h, dup, "grad_w_mlp_up"), "w_mlp_down": _grad_matmul(hdn, dyb, "grad_w_mlp_down")}
    d_attn, dy_ssm, dx2b, dgp, dga, dgs, dglu_b = _mix_bwd(dx2, attn, y, wg, glu_b, ga, gs, wo)
    rode = []
    if bwd_side is None:
        dqs, dkn, dvv = _attn_bwd(qn, kn, vv, attn, d_attn, lse, ones_hp)
    else:
        dqs, dkn, dvv, *rode = _attn_bwd(qn, kn, vv, attn, d_attn, lse, ones_hp, *bwd_side(big))
    du, db, dc, da, dd = _s5_bwd(u, dy_ssm, states, a_cat, b_mat, c_mat, d_mat)
    grad_x, dproj, dg1, dgq, dgk = _inproj_bwd(dqs, dkn, dvv, du, q_raw, k_raw, x, dx2, wi, g1, gq, gk, ones64)

    big["w_in"] = _grad_matmul(xn, dproj, "grad_w_in")
    big["glu_w"] = _grad_matmul(z, dgp, "grad_glu_w")
    big["w_out"] = _grad_matmul(mix, dx2b, "grad_w_out")
    dab_r, dab_i, dbb_r, dbb_i, dc_re, dc_im, dd_g = _s5_unpack_grads(db, dc, da, dd)
    cot = {"norm1_g": dg1, "q_norm_g": dgq, "k_norm_g": dgk, "ab_r": dab_r, "ab_i": dab_i, "bb_r": dbb_r, "bb_i": dbb_i,
           "ssm_c_re": dc_re, "ssm_c_im": dc_im, "ssm_d": dd_g, "glu_b": dglu_b, "attn_out_norm_g": dga,
           "ssm_out_norm_g": dgs, "norm2_g": dg2}
    return loss[0, 0], grad_x, big, cot, (lr, li, ldt, br, bi), rode


COT_NAMES = ("norm1_g", "q_norm_g", "k_norm_g", "ab_r", "ab_i", "bb_r", "bb_i", "ssm_c_re", "ssm_c_im", "ssm_d",
             "glu_b", "attn_out_norm_g", "ssm_out_norm_g", "norm2_g")
SMALL_NAMES = ("norm1_g", "q_norm_g", "k_norm_g", "ssm_a_re", "ssm_a_im", "ssm_log_dt", "ssm_b_re", "ssm_b_im",
               "ssm_c_re", "ssm_c_im", "ssm_d", "glu_b", "attn_out_norm_g", "ssm_out_norm_g", "norm2_g")
BIG_NAMES = ("w_in", "glu_w", "w_out", "w_mlp_up", "w_mlp_down")
PACK_ROWS = 1152


def _pack(arrs):
    flat = jnp.concatenate([a.reshape(-1) for a in arrs])
    return jnp.pad(flat, (0, PACK_ROWS * LANES - flat.shape[0])).reshape(PACK_ROWS, LANES)


def _unpack(packed, like):
    flat = packed.reshape(-1)
    out, pos = [], 0
    for a in like:
        out.append(flat[pos:pos + a.size].reshape(a.shape))
        pos += a.size
    return out


def _small_grads(cot, disc_in, p):
    lr, li, ldt, br, bi = disc_in
    group_sum = (lax.broadcasted_iota(jnp.int32, (N_GROUPS, N_GROUPS * N_STATE), 1) // N_STATE
                 == lax.broadcasted_iota(jnp.int32, (N_GROUPS, N_GROUPS * N_STATE), 0)).astype(F32)
    dlr, dli, dldt, dbr, dbi = _disc_bwd(lr, li, ldt, br, bi, cot["ab_r"], cot["ab_i"], cot["bb_r"], cot["bb_i"], group_sum)
    g = dict(cot)
    g.update(ssm_a_re=dlr, ssm_a_im=dli, ssm_log_dt=dldt[:, 0], ssm_b_re=dbr, ssm_b_im=dbi)
    return {n: g[n].reshape(p[n].shape) for n in SMALL_NAMES}


BIG = {
    "w_in": ((D_MODEL, PROJ_W), 1, PROJ_W // 4, 0, D_MODEL // 2),
    "glu_w": ((SSM_W, SSM_W), 0, SSM_W // 4, 1, SSM_W // 2),
    "w_out": ((D_MODEL, D_MODEL), 0, D_MODEL // 4, 1, D_MODEL // 2),
    "w_mlp_up": ((D_MODEL, D_FF), 1, D_FF // 4, 0, D_MODEL // 2),
    "w_mlp_down": ((D_FF, D_MODEL), 0, D_FF // 4, 1, D_MODEL // 2),
}
N_BIG = len(BIG_NAMES)
N_CHIPS = 4
ANY = pl.BlockSpec(memory_space=pl.ANY)


def _cut(name, shard=False, half=False):
    shape, s_ax, s_sz, h_ax, h_sz = BIG[name]
    shape = list(shape)
    if shard:
        shape[s_ax] = s_sz
    if half:
        shape[h_ax] = h_sz
    return tuple(shape)


def _window(name, base, shard=None, half=None):
    _, s_ax, s_sz, h_ax, h_sz = BIG[name]
    idx = [pl.ds(0, base[0]), pl.ds(0, base[1])]
    if shard is not None:
        idx[s_ax] = pl.ds(pl.multiple_of(shard * s_sz, s_sz), s_sz)
    if half is not None:
        idx[h_ax] = pl.ds(pl.multiple_of(half * h_sz, h_sz), h_sz)
    return tuple(idx)


def _mesh_pos():
    return lax.axis_index("x"), lax.axis_index("y"), lax.axis_index("c")


def _other_chips(x, y):
    return [(1 - x, y, 2 * (1 - x) + y), (x, 1 - y, 2 * x + 1 - y), (1 - x, 1 - y, 2 * (1 - x) + 1 - y)]


def _remote(src, dst, send_sem, recv_sem, dev):
    return pltpu.make_async_remote_copy(src_ref=src, dst_ref=dst, send_sem=send_sem, recv_sem=recv_sem,
                                        device_id=dev, device_id_type=MESH)


def _start_remote(src, dst, send_sem, recv_sem, dev):
    cp = _remote(src, dst, send_sem, recv_sem, dev)
    cp.start()
    return cp


class _Gather:
    def __init__(self, names):
        self.names = tuple(names)
        self.n = len(self.names)

    def in_specs(self):
        return [pl.BlockSpec(memory_space=pltpu.VMEM)] * self.n

    def out_specs(self):
        return [ANY] * self.n

    def out_shapes(self):
        return [jax.ShapeDtypeStruct(BIG[w][0], BF16) for w in self.names]

    def scratch_shapes(self):
        n_sem = (N_CHIPS - 1) * self.n
        return ([pltpu.VMEM(_cut(w, shard=True), BF16) for w in self.names]
                + [pltpu.SemaphoreType.DMA((n_sem,))] * 4 + [pltpu.SemaphoreType.DMA((self.n,))])

    def bind(self, ins, outs, scratch):
        self.ins, self.outs = ins, outs
        self.stage = scratch[:self.n]
        self.send, self.recv, self.fsend, self.frecv, self.lsem = scratch[self.n:]

    def _copies(self):
        x, y, c = _mesh_pos()
        me = 2 * x + y
        sib = (x, y, 1 - c)
        local, sends, lands, fwds, flands = [], [], [], [], []
        for w, n in enumerate(self.names):
            local.append(pltpu.make_async_copy(self.stage[w], self.outs[w].at[_window(n, BIG[n][0], shard=me)], self.lsem.at[w]))
        for k, (px, py, pj) in enumerate(_other_chips(x, y)):
            for w, n in enumerate(self.names):
                s = k * self.n + w
                sends.append(_remote(self.stage[w].at[_window(n, _cut(n, shard=True), half=c)],
                                     self.outs[w].at[_window(n, BIG[n][0], shard=me, half=c)],
                                     self.send.at[s], self.recv.at[s], (px, py, c)))
                got = self.outs[w].at[_window(n, BIG[n][0], shard=pj, half=c)]
                lands.append(_remote(got, got, self.send.at[s], self.recv.at[s], (px, py, c)))
                fwds.append(_remote(got, got, self.fsend.at[s], self.frecv.at[s], sib))
                theirs = self.outs[w].at[_window(n, BIG[n][0], shard=pj, half=1 - c)]
                flands.append(_remote(theirs, theirs, self.fsend.at[s], self.frecv.at[s], sib))
        return local, sends, lands, fwds, flands

    def start(self):
        for w in range(self.n):
            self.stage[w][...] = self.ins[w][...].astype(BF16)
        local, sends, _, _, _ = self._copies()
        for cp in local + sends:
            cp.start()

    def forward(self):
        _, _, lands, fwds, _ = self._copies()
        for land, fwd in zip(lands, fwds):
            land.wait_recv()
            fwd.start()

    def finish(self):
        local, sends, _, fwds, flands = self._copies()
        for cp in flands:
            cp.wait_recv()
        for cp in sends + fwds:
            cp.wait_send()
        for cp in local:
            cp.wait()


def _gather_weights(shards, names):
    g = _Gather(names)

    def body(*refs):
        g.bind(refs[0:g.n], refs[g.n:2 * g.n], refs[2 * g.n:])
        g.start()
        g.forward()
        g.finish()

    return pl.pallas_call(
        body, name="gather_" + "_".join(names), in_specs=g.in_specs(), out_specs=g.out_specs(), out_shape=g.out_shapes(),
        scratch_shapes=g.scratch_shapes(), compiler_params=pltpu.CompilerParams(vmem_limit_bytes=VMEM_LIMIT),
    )(*[shards[n] for n in names])


def _pair_exchange(grads, names, packed=None):
    n_big = len(names)
    n_all = n_big + (packed is not None)

    def body(*refs):
        ins, got = refs[0:n_all], refs[n_all:2 * n_all]
        send, recv = refs[2 * n_all:]
        x, y, c = _mesh_pos()
        sib = (x, y, 1 - c)
        copies = []
        for w, n in enumerate(names):
            copies.append(_start_remote(ins[w].at[_window(n, BIG[n][0], half=1 - c)], got[w], send.at[w], recv.at[w], sib))
        if packed is not None:
            copies.append(_start_remote(ins[n_big], got[n_big], send.at[n_big], recv.at[n_big], sib))
        for cp in copies:
            cp.wait()

    shapes = [jax.ShapeDtypeStruct(_cut(n, half=True), F32) for n in names]
    args = [grads[n] for n in names]
    if packed is not None:
        shapes.append(jax.ShapeDtypeStruct(packed.shape, F32))
        args.append(packed)
    return pl.pallas_call(
        body, name="grad_pair_exchange_" + "_".join(names), in_specs=[ANY] * n_all, out_specs=[ANY] * n_all, out_shape=shapes,
        scratch_shapes=[pltpu.SemaphoreType.DMA((n_all,)), pltpu.SemaphoreType.DMA((n_all,))],
    )(*args)


def _pair_sum(name, full, got, core):
    _, _, _, h_ax, _ = BIG[name]
    rows, cols = _cut(name, half=True)
    br = _row_block(rows, 512)
    nb = rows // br
    own_map = (lambda i, c: (i + c[0] * nb, 0)) if h_ax == 0 else (lambda i, c: (i, c[0]))

    def body(c_ref, own_ref, got_ref, o_ref):
        o_ref[...] = (own_ref[...] + got_ref[...]).astype(BF16)

    plain = pl.BlockSpec((br, cols), lambda i, c: (i, 0))
    return pl.pallas_call(
        body, name="pair_sum_" + name,
        grid_spec=pltpu.PrefetchScalarGridSpec(num_scalar_prefetch=1, grid=(nb,),
                                               in_specs=[pl.BlockSpec((br, cols), own_map), plain], out_specs=plain),
        out_shape=jax.ShapeDtypeStruct((rows, cols), BF16), compiler_params=_params("arbitrary"),
    )(core, full, got)


class _ChipExchange:
    def __init__(self, names, packed_shape=None):
        self.names = tuple(names)
        self.packed_shape = packed_shape
        self.n = len(self.names) + (packed_shape is not None)

    def in_specs(self):
        return [ANY] * self.n

    def out_specs(self):
        return [ANY] * self.n

    def out_shapes(self):
        shapes = [jax.ShapeDtypeStruct((N_CHIPS,) + _cut(w, shard=True, half=True), BF16) for w in self.names]
        if self.packed_shape is not None:
            shapes.append(jax.ShapeDtypeStruct((N_CHIPS,) + tuple(self.packed_shape), F32))
        return shapes

    def scratch_shapes(self):
        n_sem = (N_CHIPS - 1) * self.n
        return [pltpu.SemaphoreType.DMA((n_sem,)), pltpu.SemaphoreType.DMA((n_sem,))]

    def bind(self, ins, outs, scratch):
        self.ins, self.outs = ins, outs
        self.send, self.recv = scratch

    def _piece(self, w, shard):
        if w >= len(self.names):
            return self.ins[w]
        n = self.names[w]
        return self.ins[w].at[_window(n, _cut(n, half=True), shard=shard)]

    def _copies(self):
        x, y, c = _mesh_pos()
        me = 2 * x + y
        sends, lands = [], []
        for k, (px, py, pj) in enumerate(_other_chips(x, y)):
            for w in range(self.n):
                s = k * self.n + w
                sends.append(_remote(self._piece(w, pj), self.outs[w].at[me], self.send.at[s], self.recv.at[s], (px, py, c)))
                lands.append(_remote(self._piece(w, me), self.outs[w].at[pj], self.send.at[s], self.recv.at[s], (px, py, c)))
        return sends, lands

    def start(self):
        for cp in self._copies()[0]:
            cp.start()

    def finish(self):
        sends, lands = self._copies()
        for cp in lands:
            cp.wait_recv()
        for cp in sends:
            cp.wait_send()


def _chip_exchange(halves, packed, names):
    ex = _ChipExchange(names, packed.shape)

    def body(*refs):
        ex.bind(refs[0:ex.n], refs[ex.n:2 * ex.n], refs[2 * ex.n:])
        ex.start()
        ex.finish()

    return pl.pallas_call(
        body, name="grad_chip_exchange", in_specs=ex.in_specs(), out_specs=ex.out_specs(), out_shape=ex.out_shapes(),
        scratch_shapes=ex.scratch_shapes(),
    )(*halves, packed)


def _chip_sum(name, own, slots, chip):
    n_slot, rows, cols = slots.shape
    br = _row_block(rows, 512)
    nb = rows // br
    if name in BIG and BIG[name][1] == 1:
        own_map = lambda i, m: (i, m[0])
    elif name in BIG:
        own_map = lambda i, m: (i + m[0] * nb, 0)
    else:
        own_map = lambda i, m: (i, 0)

    def slot_map(j):
        return lambda i, m: (jnp.where(m[0] == j, (j + 1) % n_slot, j), i, 0)

    def body(m_ref, own_ref, *refs):
        own_blk = own_ref[...].astype(F32)
        tot = None
        for j in range(n_slot):
            term = jnp.where(m_ref[0] == j, own_blk, refs[j][...].astype(F32))
            tot = term if tot is None else tot + term
        refs[n_slot][...] = tot

    in_specs = [pl.BlockSpec((br, cols), own_map)] + [pl.BlockSpec((None, br, cols), slot_map(j)) for j in range(n_slot)]
    return pl.pallas_call(
        body, name="chip_sum_" + name,
        grid_spec=pltpu.PrefetchScalarGridSpec(num_scalar_prefetch=1, grid=(nb,), in_specs=in_specs,
                                               out_specs=pl.BlockSpec((br, cols), lambda i, m: (i, 0))),
        out_shape=jax.ShapeDtypeStruct((rows, cols), F32), compiler_params=_params("arbitrary"),
    )(chip, own, *([slots] * n_slot))


def _half_exchange(pieces):
    def body(*refs):
        ins, outs = refs[0:N_BIG], refs[N_BIG:2 * N_BIG]
        send, recv = refs[2 * N_BIG:]
        x, y, c = _mesh_pos()
        sib = (x, y, 1 - c)
        copies = []
        for w, n in enumerate(BIG_NAMES):
            copies.append(_start_remote(ins[w], outs[w], send.at[w], recv.at[w], sib))
        for cp in copies:
            cp.wait()

    return pl.pallas_call(
        body, name="grad_half_exchange", in_specs=[ANY] * N_BIG, out_specs=[ANY] * N_BIG,
        out_shape=[jax.ShapeDtypeStruct(_cut(n, shard=True, half=True), F32) for n in BIG_NAMES],
        scratch_shapes=[pltpu.SemaphoreType.DMA((N_BIG,)), pltpu.SemaphoreType.DMA((N_BIG,))],
    )(*pieces)


WEIGHT_NAMES = ("norm1_g", "w_in", "q_norm_g", "k_norm_g", "ssm_a_re", "ssm_a_im", "ssm_log_dt", "ssm_b_re", "ssm_b_im",
                "ssm_c_re", "ssm_c_im", "ssm_d", "glu_w", "glu_b", "attn_out_norm_g", "ssm_out_norm_g", "w_out", "norm2_g",
                "w_mlp_up", "w_mlp_down")


def _train_step(a):
    x = a["x"][0]
    target = a["loss_target"][0]
    shards = {n: a[n][0] for n in BIG_NAMES}
    p = {n: a[n][0] for n in SMALL_NAMES}
    core = lax.axis_index("c").astype(jnp.int32).reshape(1)
    chip_id = (2 * lax.axis_index("x") + lax.axis_index("y")).astype(jnp.int32).reshape(1)

    later = ("glu_w", "w_out", "w_mlp_up", "w_mlp_down")
    early = ("w_mlp_up", "w_mlp_down")
    late = ("w_in", "glu_w", "w_out")
    (wi,) = _gather_weights(shards, ("w_in",))
    chip = {}

    def bwd_side(grads):
        got = _pair_exchange(grads, early)
        for n, g in zip(early, got):
            chip[n] = _pair_sum(n, grads[n], g, core)
        return _ChipExchange(early), [chip[n] for n in early]

    loss, grad_x, big, cot, disc_in, early_slots = _local_step(
        x, target, wi, None, p, fwd_side=(_Gather(later), [shards[n] for n in later]), bwd_side=bwd_side)
    slots = dict(zip(early, early_slots))

    cot_list = [cot[n] for n in COT_NAMES]
    packed = _pack(cot_list)
    *got, got_packed = _pair_exchange(big, late, packed)
    for n, g in zip(late, got):
        chip[n] = _pair_sum(n, big[n], g, core)
    chip_packed = _sum_arrays([packed, got_packed], "pair_sum_small")
    *late_slots, small_slots = _chip_exchange([chip[n] for n in late], chip_packed, late)
    slots.update(zip(late, late_slots))
    pieces = [_chip_sum(n, chip[n], slots[n], chip_id) for n in BIG_NAMES]
    small_sum = _chip_sum("small", chip_packed, small_slots, chip_id)
    shard_grads = {}
    for n, mine, theirs in zip(BIG_NAMES, pieces, _half_exchange(pieces)):
        h_ax = BIG[n][3]
        shard_grads[n] = jnp.where(core[0] == 0, jnp.concatenate([mine, theirs], axis=h_ax),
                                   jnp.concatenate([theirs, mine], axis=h_ax))
    small_grads = _small_grads(dict(zip(COT_NAMES, _unpack(small_sum, cot_list))), disc_in, p)

    grads, delta, new_m, new_v = {}, {}, {}, {}
    for n in BIG_NAMES:
        grads[n] = shard_grads[n]
        delta[n], new_m[n], new_v[n] = _adamw(a[n][0], grads[n], a["m_" + n][0], a["v_" + n][0], "adamw_" + n)
    flat2 = lambda t: t.reshape(-1, t.shape[-1])
    res = _adamw_many([flat2(p[n]) for n in SMALL_NAMES], [flat2(small_grads[n]) for n in SMALL_NAMES],
                      [flat2(a["m_" + n][0]) for n in SMALL_NAMES], [flat2(a["v_" + n][0]) for n in SMALL_NAMES])
    for store, outs in zip((delta, new_m, new_v), res):
        store.update(zip(SMALL_NAMES, outs))
    grads.update(small_grads)

    total = lax.psum(loss, ("x", "y", "c"))
    out = [total, grad_x[None]]
    for store in (grads, delta, new_m, new_v):
        out += [store[n].reshape(a[n].shape) for n in WEIGHT_NAMES]
    return tuple(out)


def kernel(x, norm1_g, w_in, q_norm_g, k_norm_g, ssm_a_re, ssm_a_im, ssm_log_dt, ssm_b_re, ssm_b_im, ssm_c_re, ssm_c_im, ssm_d, glu_w, glu_b, attn_out_norm_g, ssm_out_norm_g, w_out, norm2_g, w_mlp_up, w_mlp_down, loss_target, m_norm1_g, m_w_in, m_q_norm_g, m_k_norm_g, m_ssm_a_re, m_ssm_a_im, m_ssm_log_dt, m_ssm_b_re, m_ssm_b_im, m_ssm_c_re, m_ssm_c_im, m_ssm_d, m_glu_w, m_glu_b, m_attn_out_norm_g, m_ssm_out_norm_g, m_w_out, m_norm2_g, m_w_mlp_up, m_w_mlp_down, v_norm1_g, v_w_in, v_q_norm_g, v_k_norm_g, v_ssm_a_re, v_ssm_a_im, v_ssm_log_dt, v_ssm_b_re, v_ssm_b_im, v_ssm_c_re, v_ssm_c_im, v_ssm_d, v_glu_w, v_glu_b, v_attn_out_norm_g, v_ssm_out_norm_g, v_w_out, v_norm2_g, v_w_mlp_up, v_w_mlp_down):
    return _train_step(dict(locals()))
```

```python
import functools
import math

import jax
import jax.numpy as jnp
from jax import lax
from jax.experimental import pallas as pl
from jax.experimental.pallas import tpu as pltpu

F32 = jnp.float32
BF16 = jnp.bfloat16
MESH = pl.DeviceIdType.MESH

D_MODEL = 1024
ATTN_W = 512
SSM_W = 512
HEAD = 64
D_FF = 4096
PROJ_W = 2048
N_GROUPS = 32
N_STATE = 64
GROUP_W = 16
EPS = 1e-6
NEG = -1e30
DILATIONS = (1, 4, 16)
BLK = 128
TILE = 2048
LANES = 128
N_LB = SSM_W // LANES
N_SLAB = 2 * N_LB * N_STATE * 8 // LANES // N_LB
VMEM_LIMIT = 56 * 1024 * 1024

ADAM_LR, ADAM_B1, ADAM_B2, ADAM_EPS, ADAM_WD, ADAM_STEP = 0.001, 0.9, 0.999, 1e-08, 0.01, 10


def _params(*sem):
    return pltpu.CompilerParams(dimension_semantics=sem, vmem_limit_bytes=VMEM_LIMIT)


def _nt(a, b):
    return lax.dot_general(a, b, (((1,), (1,)), ((), ())), preferred_element_type=F32)


def _tn(a, b):
    return lax.dot_general(a, b, (((0,), (0,)), ((), ())), preferred_element_type=F32)


def _mm(a, b):
    return jnp.dot(a, b, preferred_element_type=F32)


def _group_mean(t, ones_bd, width):
    hi = t.astype(BF16)
    lo = (t - hi.astype(F32)).astype(BF16)
    return (_mm(hi, ones_bd) + _mm(lo, ones_bd)) * (1.0 / width)


def _rms(x):
    return lax.rsqrt(jnp.mean(x * x, axis=-1, keepdims=True) + EPS)


def _rms_bwd(dy, x, r, g):
    xh = x * r
    dxh = dy * g
    dx = r * (dxh - xh * jnp.mean(dxh * xh, axis=-1, keepdims=True))
    return dx, dy * xh


def _colsum(x):
    return jnp.sum(x, axis=0, keepdims=True)


def _row_block(rows, cap):
    for b in range(min(rows, cap) // 8 * 8, 0, -8):
        if rows % b == 0:
            return b
    raise ValueError(f"no row block for {rows} rows")


def _inproj_fwd(x, g1, wi, gq, gk, ones64):
    t_len = x.shape[0]
    tm = 512
    n_hp = ATTN_W // LANES

    def body(x_ref, g1_ref, wi_ref, gq_ref, gk_ref, bd_ref, xn_ref, q_ref, k_ref, v_ref, u_ref, qr_ref, kr_ref):
        xv = x_ref[...]
        xn = (xv * _rms(xv) * g1_ref[...]).astype(BF16)
        xn_ref[...] = xn
        proj = _mm(xn, wi_ref[...])
        q = proj[:, 0:ATTN_W]
        k = proj[:, ATTN_W:2 * ATTN_W]
        v = proj[:, 2 * ATTN_W:3 * ATTN_W]
        u_ref[...] = proj[:, 3 * ATTN_W:]
        qr_ref[...] = q
        kr_ref[...] = k
        bd = bd_ref[...]
        qn = q * lax.rsqrt(_group_mean(q * q, bd, HEAD) + EPS) * gq_ref[...] * (HEAD ** -0.5)
        kn = k * lax.rsqrt(_group_mean(k * k, bd, HEAD) + EPS) * gk_ref[...]
        for hp in range(n_hp):
            sl = slice(hp * LANES, (hp + 1) * LANES)
            q_ref[hp] = qn[:, sl]
            k_ref[hp] = kn[:, sl]
            v_ref[hp] = v[:, sl]

    row = lambda i: (i, 0)
    const = lambda i: (0, 0)
    hp_spec = pl.BlockSpec((n_hp, tm, LANES), lambda i: (0, i, 0))
    hp_shape = jax.ShapeDtypeStruct((n_hp, t_len, LANES), F32)
    return pl.pallas_call(
        body, name="inproj_fwd", grid=(t_len // tm,),
        in_specs=[pl.BlockSpec((tm, D_MODEL), row), pl.BlockSpec((1, D_MODEL), const),
                  pl.BlockSpec((D_MODEL, PROJ_W), const), pl.BlockSpec((1, ATTN_W), const),
                  pl.BlockSpec((1, ATTN_W), const), pl.BlockSpec((ATTN_W, ATTN_W), const)],
        out_specs=[pl.BlockSpec((tm, D_MODEL), row), hp_spec, hp_spec, hp_spec,
                   pl.BlockSpec((tm, SSM_W), row), pl.BlockSpec((tm, ATTN_W), row), pl.BlockSpec((tm, ATTN_W), row)],
        out_shape=[jax.ShapeDtypeStruct((t_len, D_MODEL), BF16), hp_shape, hp_shape, hp_shape,
                   jax.ShapeDtypeStruct((t_len, SSM_W), F32), jax.ShapeDtypeStruct((t_len, ATTN_W), F32),
                   jax.ShapeDtypeStruct((t_len, ATTN_W), F32)],
        compiler_params=_params("arbitrary"),
    )(x, g1, wi, gq, gk, ones64)


def _attn_masks():
    head0 = lax.broadcasted_iota(jnp.int32, (BLK, LANES), 1) < HEAD
    row = lax.broadcasted_iota(jnp.int32, (2 * BLK, 2 * BLK), 0) & (BLK - 1)
    col = lax.broadcasted_iota(jnp.int32, (2 * BLK, 2 * BLK), 1)
    return head0, (col < BLK) & (col >= row), (col >= BLK) & (col - BLK <= row)


def _stack_heads(x, head0):
    return jnp.concatenate([jnp.where(head0, x, 0.0), jnp.where(head0, 0.0, x)], axis=0).astype(BF16)


def _unit_rows(uidx, d):
    nb = TILE // (BLK * d)
    r = lax.div(uidx, nb)
    b = lax.rem(uidx, nb)
    start = r + d * BLK * b
    if d == 1:
        start = pl.multiple_of(start, BLK)
        mk = lambda s: pl.ds(pl.multiple_of(s, BLK), BLK)
    else:
        mk = lambda s: pl.ds(s, BLK, stride=d)
    return b, mk(start), mk(TILE + start), mk(TILE + start - d * BLK)


def _attn_fwd(q, k, v, side=None, side_args=()):
    n_hp, t_len, _ = q.shape
    nt = t_len // TILE
    ns = side.n if side is not None else 0
    n_steps = n_hp * nt

    def body(*refs):
        q_ref, kp_ref, kc_ref, vp_ref, vc_ref = refs[0:5]
        o_ref, lse_ref = refs[5 + ns:7 + ns]
        kk, vv, m_s, l_s, acc_s = refs[7 + 2 * ns:12 + 2 * ns]
        t = pl.program_id(1)
        step = pl.program_id(0) * nt + t
        if side is not None:
            side.bind(refs[5:5 + ns], refs[7 + ns:7 + 2 * ns], refs[12 + 2 * ns:])
            pl.when(step == 0)(side.start)
            pl.when(step == n_steps // 2)(side.forward)
        kk[0:TILE] = kp_ref[0]
        kk[TILE:] = kc_ref[0]
        vv[0:TILE] = vp_ref[0]
        vv[TILE:] = vc_ref[0]
        head0, band_prev, band_cur = _attn_masks()

        for pi, d in enumerate(DILATIONS):
            def unit(uidx, carry, d=d, pi=pi):
                b, rows_q, rows_c, rows_p = _unit_rows(uidx, d)
                mask = band_cur | (band_prev & ((t > 0) | (b > 0)))
                q2 = _stack_heads(q_ref.at[0][rows_q, :], head0)
                kcat = jnp.concatenate([kk[rows_p, :], kk[rows_c, :]], axis=0).astype(BF16)
                vcat = jnp.concatenate([vv[rows_p, :], vv[rows_c, :]], axis=0).astype(BF16)
                s = jnp.where(mask, _nt(q2, kcat), NEG)
                m = jnp.max(s, axis=1, keepdims=True)
                p = jnp.exp(s - m)
                ls = jnp.sum(p, axis=1, keepdims=True)
                pv = _mm(p.astype(BF16), vcat)
                m_s.at[pi][rows_q, :] = jnp.where(head0, m[0:BLK], m[BLK:])
                l_s.at[pi][rows_q, :] = jnp.where(head0, ls[0:BLK], ls[BLK:])
                acc_s.at[pi][rows_q, :] = jnp.where(head0, pv[0:BLK], pv[BLK:])
                return carry

            lax.fori_loop(0, TILE // BLK, unit, 0, unroll=8)

        m_all = jnp.maximum(jnp.maximum(m_s[0], m_s[1]), m_s[2])
        num = jnp.zeros((TILE, LANES), F32)
        den = jnp.zeros((TILE, LANES), F32)
        for pi in range(len(DILATIONS)):
            wgt = jnp.exp(m_s[pi] - m_all)
            num = num + acc_s[pi] * wgt
            den = den + l_s[pi] * wgt
        o_ref[...] = num / den
        lse_ref[0] = m_all + jnp.log(den)
        if side is not None:
            pl.when(step == n_steps - 1)(side.finish)

    cur = lambda hp, t: (hp, t, 0)
    prev = lambda hp, t: (hp, jnp.maximum(t - 1, 0), 0)
    blk = (1, TILE, LANES)
    per_pattern = pltpu.VMEM((len(DILATIONS), TILE, LANES), F32)
    extra = (side.in_specs(), side.out_specs(), side.out_shapes(), side.scratch_shapes()) if side is not None else ([], [], [], [])
    return pl.pallas_call(
        body, name="attn_fwd", grid=(n_hp, nt),
        in_specs=[pl.BlockSpec(blk, cur), pl.BlockSpec(blk, prev), pl.BlockSpec(blk, cur),
                  pl.BlockSpec(blk, prev), pl.BlockSpec(blk, cur)] + extra[0],
        out_specs=[pl.BlockSpec((TILE, LANES), lambda hp, t: (t, hp)), pl.BlockSpec(blk, cur)] + extra[1],
        out_shape=[jax.ShapeDtypeStruct((t_len, ATTN_W), F32), jax.ShapeDtypeStruct((n_hp, t_len, LANES), F32)] + extra[2],
        scratch_shapes=[pltpu.VMEM((2 * TILE, LANES), F32), pltpu.VMEM((2 * TILE, LANES), F32),
                        per_pattern, per_pattern, per_pattern] + extra[3],
        compiler_params=_params("arbitrary", "arbitrary"),
    )(q, k, k, v, v, *side_args)


def _attn_bwd(q, k, v, o, do, lse, ones_hp, side=None, side_args=()):
    n_hp, t_len, _ = q.shape
    nt = t_len // TILE
    ns = side.n if side is not None else 0
    n_pat = len(DILATIONS)

    def body(*refs):
        q_ref, kp_ref, kc_ref, vp_ref, vc_ref, o_ref, do_ref, lse_ref, bd_ref = refs[0:9]
        dq_ref, dk_ref, dv_ref = refs[9 + ns:12 + ns]
        kk, vv, dq_s, dkc, dkp, dvc, dvp, hold_k, hold_v, dl_s = refs[12 + 2 * ns:22 + 2 * ns]
        t = pl.program_id(1)
        if side is not None:
            side.bind(refs[9:9 + ns], refs[12 + ns:12 + 2 * ns], refs[22 + 2 * ns:])
            pl.when((pl.program_id(0) == 0) & (t == 0))(side.start)

        @pl.when(t < nt)
        def _():
            kk[0:TILE] = kp_ref[0]
            kk[TILE:] = kc_ref[0]
            vv[0:TILE] = vp_ref[0]
            vv[TILE:] = vc_ref[0]
            dl_s[...] = _group_mean(do_ref[...] * o_ref[...], bd_ref[...], 1.0)
            head0, band_prev, band_cur = _attn_masks()

            for pi, d in enumerate(DILATIONS):
                def unit(uidx, carry, d=d, pi=pi):
                    b, rows_q, rows_c, rows_p = _unit_rows(uidx, d)
                    mask = band_cur | (band_prev & ((t > 0) | (b > 0)))
                    q2 = _stack_heads(q_ref.at[0][rows_q, :], head0)
                    do2 = _stack_heads(do_ref[rows_q, :], head0)
                    lse_f = lse_ref.at[0][rows_q, :]
                    dl_f = dl_s[rows_q, :]
                    lse2 = jnp.concatenate([lse_f[:, 0:1], lse_f[:, HEAD:HEAD + 1]], axis=0)
                    dl2 = jnp.concatenate([dl_f[:, 0:1], dl_f[:, HEAD:HEAD + 1]], axis=0)
                    kcat = jnp.concatenate([kk[rows_p, :], kk[rows_c, :]], axis=0).astype(BF16)
                    vcat = jnp.concatenate([vv[rows_p, :], vv[rows_c, :]], axis=0).astype(BF16)
                    p = jnp.where(mask, jnp.exp(_nt(q2, kcat) - lse2), 0.0)
                    ds = (p * (_nt(do2, vcat) - dl2)).astype(BF16)
                    dq2 = _mm(ds, kcat)
                    dq_s.at[pi][rows_q, :] = jnp.where(head0, dq2[0:BLK], dq2[BLK:])
                    dk2 = _tn(ds, q2)
                    dv2 = _tn(p.astype(BF16), do2)
                    dkp.at[pi][rows_q, :] = dk2[0:BLK]
                    dkc.at[pi][rows_q, :] = dk2[BLK:]
                    dvp.at[pi][rows_q, :] = dv2[0:BLK]
                    dvc.at[pi][rows_q, :] = dv2[BLK:]
                    return carry

                lax.fori_loop(0, TILE // BLK, unit, 0, unroll=8)

            dq_ref[...] = dq_s[0] + dq_s[1] + dq_s[2]

        @pl.when(t > 0)
        def _():
            dk_ref[...] = hold_k[...]
            dv_ref[...] = hold_v[...]

        @pl.when((t > 0) & (t < nt))
        def _():
            for pi, d in enumerate(DILATIONS):
                back = d * BLK
                dk_ref[TILE - back:, :] = dk_ref[TILE - back:, :] + dkp[pi, 0:back, :]
                dv_ref[TILE - back:, :] = dv_ref[TILE - back:, :] + dvp[pi, 0:back, :]

        @pl.when(t < nt)
        def _():
            hold_k[...] = dkc[0] + dkc[1] + dkc[2]
            hold_v[...] = dvc[0] + dvc[1] + dvc[2]
            for pi, d in enumerate(DILATIONS):
                back = d * BLK
                if back < TILE:
                    hold_k[0:TILE - back, :] = hold_k[0:TILE - back, :] + dkp[pi, back:, :]
                    hold_v[0:TILE - back, :] = hold_v[0:TILE - back, :] + dvp[pi, back:, :]

        if side is not None:
            pl.when((pl.program_id(0) == n_hp - 1) & (t == nt))(side.finish)

    last = nt - 1
    extra = (side.in_specs(), side.out_specs(), side.out_shapes(), side.scratch_shapes()) if side is not None else ([], [], [], [])
    cur = lambda hp, t: (hp, jnp.minimum(t, last), 0)
    prev = lambda hp, t: (hp, jnp.clip(t - 1, 0, last), 0)
    cur2 = lambda hp, t: (jnp.minimum(t, last), hp)
    prev2 = lambda hp, t: (jnp.maximum(t - 1, 0), hp)
    blk = (1, TILE, LANES)
    blk2 = (TILE, LANES)
    out = jax.ShapeDtypeStruct((t_len, ATTN_W), F32)
    return pl.pallas_call(
        body, name="attn_bwd", grid=(n_hp, nt + 1),
        in_specs=[pl.BlockSpec(blk, cur), pl.BlockSpec(blk, prev), pl.BlockSpec(blk, cur),
                  pl.BlockSpec(blk, prev), pl.BlockSpec(blk, cur), pl.BlockSpec(blk2, cur2),
                  pl.BlockSpec(blk2, cur2), pl.BlockSpec(blk, cur), pl.BlockSpec((LANES, LANES), lambda hp, t: (0, 0))]
        + extra[0],
        out_specs=[pl.BlockSpec(blk2, cur2), pl.BlockSpec(blk2, prev2), pl.BlockSpec(blk2, prev2)] + extra[1],
        out_shape=[out, out, out] + extra[2],
        scratch_shapes=[pltpu.VMEM((2 * TILE, LANES), F32), pltpu.VMEM((2 * TILE, LANES), F32)]
        + [pltpu.VMEM((n_pat, TILE, LANES), F32)] * 5 + [pltpu.VMEM((TILE, LANES), F32)] * 3 + extra[3],
        compiler_params=_params("arbitrary", "arbitrary"),
    )(q, k, k, v, v, o, do, lse, ones_hp, *side_args)


def _discretise(lr, li, ldt, br, bi):
    dt = jnp.exp(ldt)
    mag = jnp.exp(lr * dt)
    ab_r, ab_i = mag * jnp.cos(li * dt), mag * jnp.sin(li * dt)
    den = lr * lr + li * li
    nr, ni = ab_r - 1.0, ab_i
    cr = (nr * lr + ni * li) / den
    ci = (ni * lr - nr * li) / den
    return ab_r, ab_i, cr * br - ci * bi, cr * bi + ci * br


def _disc_fwd(lr, li, ldt, br, bi):
    def body(lr_ref, li_ref, ldt_ref, br_ref, bi_ref, ar_o, ai_o, bbr_o, bbi_o):
        outs = _discretise(lr_ref[...], li_ref[...], ldt_ref[...], br_ref[...], bi_ref[...])
        for o_ref, val in zip((ar_o, ai_o, bbr_o, bbi_o), outs):
            o_ref[...] = val

    col = jax.ShapeDtypeStruct(lr.shape, F32)
    mat = jax.ShapeDtypeStruct(br.shape, F32)
    return pl.pallas_call(body, name="s5_disc_fwd", out_shape=[col, col, mat, mat])(lr, li, ldt, br, bi)


def _disc_bwd(lr, li, ldt, br, bi, d_ar, d_ai, d_bbr, d_bbi, group_sum):
    def body(lr_ref, li_ref, ldt_ref, br_ref, bi_ref, c1, c2, c3, c4, gs_ref, dlr_o, dli_o, dldt_o, dbr_o, dbi_o):
        _, vjp = jax.vjp(_discretise, lr_ref[...], li_ref[...], ldt_ref[...], br_ref[...], bi_ref[...])
        dlr, dli, dldt, dbr, dbi = vjp((c1[...], c2[...], c3[...], c4[...]))
        dlr_o[...] = dlr
        dli_o[...] = dli
        dbr_o[...] = dbr
        dbi_o[...] = dbi
        wide = jnp.broadcast_to(dldt, (dldt.shape[0], LANES))
        dldt_o[...] = jnp.dot(gs_ref[...], wide, precision=lax.Precision.HIGHEST, preferred_element_type=F32)

    col = jax.ShapeDtypeStruct(lr.shape, F32)
    mat = jax.ShapeDtypeStruct(br.shape, F32)
    return pl.pallas_call(
        body, name="s5_disc_bwd", out_shape=[col, col, jax.ShapeDtypeStruct((N_GROUPS, LANES), F32), mat, mat],
    )(lr, li, ldt, br, bi, d_ar, d_ai, d_bbr, d_bbi, group_sum)


N_CHUNK = TILE // BLK
HALF = 4


def _cmul(ar, ai, xr, xi):
    return ar * xr - ai * xi, ar * xi + ai * xr


def _power_table(a_ref, tab, sign, reverse):
    ar = [a_ref[0, j:j + 1, :] for j in range(HALF)]
    ai = [sign * a_ref[0, HALF + j:HALF + j + 1, :] for j in range(HALF)]

    def step(s, cur):
        row = pl.ds((BLK - 1 - s) if reverse else s, 1)
        nxt = []
        for j in range(HALF):
            tab.at[j][row, :] = cur[j]
            tab.at[HALF + j][row, :] = cur[HALF + j]
            nxt.append(_cmul(ar[j], ai[j], cur[j], cur[HALF + j]))
        return tuple(p[0] for p in nxt) + tuple(p[1] for p in nxt)

    lax.fori_loop(0, BLK, step, tuple(ar) + tuple(ai))


def _interleave(src, dst):
    for c in range(N_CHUNK):
        dst[pl.ds(c, BLK, stride=N_CHUNK), :] = src[c * BLK:(c + 1) * BLK, :]


def _deinterleave(src, dst):
    for c in range(N_CHUNK):
        dst[c * BLK:(c + 1) * BLK, :] = src[pl.ds(c, BLK, stride=N_CHUNK), :]


def _step_rows(s):
    return pl.ds(pl.multiple_of(s * N_CHUNK, N_CHUNK), N_CHUNK)


def _chunk_scan(buf, a_ref, sign, reverse):
    ar = [jnp.broadcast_to(a_ref[0, j:j + 1, :], (N_CHUNK, LANES)) for j in range(HALF)]
    ai = [sign * jnp.broadcast_to(a_ref[0, HALF + j:HALF + j + 1, :], (N_CHUNK, LANES)) for j in range(HALF)]

    def step(i, carry):
        s = (BLK - 1 - i) if reverse else i
        rows = _step_rows(s)
        out = []
        for j in range(HALF):
            pr, pi = _cmul(ar[j], ai[j], carry[j], carry[HALF + j])
            xr = buf.at[j][rows, :] + pr
            xi = buf.at[HALF + j][rows, :] + pi
            buf.at[j][rows, :] = xr
            buf.at[HALF + j][rows, :] = xi
            out.append((xr, xi))
        return tuple(p[0] for p in out) + tuple(p[1] for p in out)

    zero = jnp.zeros((N_CHUNK, LANES), F32)
    lax.fori_loop(0, BLK, step, (zero,) * (2 * HALF), unroll=2)


def _chunk_states(buf, carry_s, xin_s, tab, reverse):
    edge = 0 if reverse else BLK - 1
    top = 0 if reverse else BLK - 1
    pw = [tab[j, top:top + 1, :] for j in range(2 * HALF)]
    cur = [carry_s[j:j + 1, :] for j in range(2 * HALF)]
    summary = [buf[j, edge * N_CHUNK:(edge + 1) * N_CHUNK, :] for j in range(2 * HALF)]
    order = range(N_CHUNK - 1, -1, -1) if reverse else range(N_CHUNK)
    for c in order:
        for j in range(2 * HALF):
            xin_s[j, c:c + 1, :] = cur[j]
        nxt = []
        for j in range(HALF):
            pr, pi = _cmul(pw[j], pw[HALF + j], cur[j], cur[HALF + j])
            nxt.append((pr + summary[j][c:c + 1, :], pi + summary[HALF + j][c:c + 1, :]))
        cur = [p[0] for p in nxt] + [p[1] for p in nxt]
    for j in range(2 * HALF):
        carry_s[j:j + 1, :] = cur[j]


def _s5_fwd(u, a_cat, b_mat, c_mat, d_skip):
    t_len = u.shape[0]
    nt = t_len // TILE

    def body(u_ref, a_ref, b_ref, c_ref, d_ref, y_ref, x_ref, xs, us, tab, carry_s, xin_s):
        sb = pl.program_id(1)

        @pl.when(sb == 0)
        def _():
            _power_table(a_ref, tab, 1.0, False)
            carry_s[...] = jnp.zeros_like(carry_s)

        _interleave(u_ref, us)
        uv = us[...]
        bu = _mm(uv.astype(BF16), b_ref[0])
        for j in range(2 * HALF):
            xs[j] = bu[:, j * LANES:(j + 1) * LANES]
        _chunk_scan(xs, a_ref, 1.0, False)
        _chunk_states(xs, carry_s, xin_s, tab, False)
        xin = [xin_s[j] for j in range(2 * HALF)]

        def fix(s, acc):
            rows = _step_rows(s)
            for j in range(HALF):
                pr, pi = _cmul(tab.at[j][pl.ds(s, 1), :], tab.at[HALF + j][pl.ds(s, 1), :], xin[j], xin[HALF + j])
                xs.at[j][rows, :] = xs.at[j][rows, :] + pr
                xs.at[HALF + j][rows, :] = xs.at[HALF + j][rows, :] + pi
            return acc

        lax.fori_loop(0, BLK, fix, 0, unroll=2)
        xcat = jnp.concatenate([xs[j].astype(BF16) for j in range(2 * HALF)], axis=1)
        x_ref[0] = xcat
        us[...] = d_ref[0] * uv + _mm(xcat, c_ref[0])
        _deinterleave(us, y_ref)

    return pl.pallas_call(
        body, name="s5_fwd", grid=(N_LB, nt),
        in_specs=[pl.BlockSpec((TILE, LANES), lambda lb, sb: (sb, lb)),
                  pl.BlockSpec((1, 2 * HALF, LANES), lambda lb, sb: (lb, 0, 0)),
                  pl.BlockSpec((1, LANES, 2 * HALF * LANES), lambda lb, sb: (lb, 0, 0)),
                  pl.BlockSpec((1, 2 * HALF * LANES, LANES), lambda lb, sb: (lb, 0, 0)),
                  pl.BlockSpec((1, 1, LANES), lambda lb, sb: (lb, 0, 0))],
        out_specs=[pl.BlockSpec((TILE, LANES), lambda lb, sb: (sb, lb)),
                   pl.BlockSpec((1, TILE, 2 * HALF * LANES), lambda lb, sb: (lb, sb, 0))],
        out_shape=[jax.ShapeDtypeStruct((t_len, SSM_W), F32), jax.ShapeDtypeStruct((N_LB, t_len, 2 * HALF * LANES), BF16)],
        scratch_shapes=[pltpu.VMEM((2 * HALF, TILE, LANES), F32), pltpu.VMEM((TILE, LANES), F32),
                        pltpu.VMEM((2 * HALF, BLK, LANES), F32),
                        pltpu.VMEM((2 * HALF, LANES), F32), pltpu.VMEM((2 * HALF, N_CHUNK, LANES), F32)],
        compiler_params=_params("arbitrary", "arbitrary"),
    )(u, a_cat, b_mat, c_mat, d_skip)


def _s5_bwd(u, dy, states, a_cat, b_mat, c_mat, d_skip):
    t_len = u.shape[0]
    nt = t_len // TILE
    last = nt - 1

    def body(u_ref, dy_ref, x_ref, a_ref, b_ref, c_ref, d_ref, du_ref, db_ref, dc_ref, da_ref, dd_ref,
             gs, us, dys, tabc, lam_s, lin_s):
        sb = pl.program_id(1)

        @pl.when(sb == 0)
        def _():
            _power_table(a_ref, tabc, -1.0, True)
            lam_s[...] = jnp.zeros_like(lam_s)
            db_ref[...] = jnp.zeros_like(db_ref)
            dc_ref[...] = jnp.zeros_like(dc_ref)
            da_ref[...] = jnp.zeros_like(da_ref)
            dd_ref[...] = jnp.zeros_like(dd_ref)

        _interleave(u_ref, us)
        _interleave(dy_ref, dys)
        uv = us[...]
        dyv = dys[...]
        ub = uv.astype(BF16)
        dyb = dyv.astype(BF16)
        gy = _nt(dyb, c_ref[0])
        for j in range(2 * HALF):
            gs[j] = gy[:, j * LANES:(j + 1) * LANES]
        _chunk_scan(gs, a_ref, -1.0, True)
        _chunk_states(gs, lam_s, lin_s, tabc, True)
        zero = jnp.zeros((N_CHUNK, LANES), F32)
        x_tile = x_ref.at[0]
        for grp in range(0, HALF, 2):
            slabs = (grp, grp + 1)
            lin = [(lin_s[j], lin_s[HALF + j]) for j in slabs]

            def fix(i, carry, slabs=slabs, lin=lin):
                s = BLK - 1 - i
                rows = _step_rows(s)
                out = []
                for k, j in enumerate(slabs):
                    nr, ni, acc_r, acc_i = carry[4 * k:4 * k + 4]
                    xr = x_tile[rows, pl.ds(j * LANES, LANES)].astype(F32)
                    xi = x_tile[rows, pl.ds((HALF + j) * LANES, LANES)].astype(F32)
                    qr, qi = _cmul(tabc.at[j][pl.ds(s, 1), :], tabc.at[HALF + j][pl.ds(s, 1), :], lin[k][0], lin[k][1])
                    lr_ = gs.at[j][rows, :] + qr
                    li_ = gs.at[HALF + j][rows, :] + qi
                    gs.at[j][rows, :] = lr_
                    gs.at[HALF + j][rows, :] = li_
                    out += [lr_, li_, acc_r + (xr * nr + xi * ni), acc_i + (xr * ni - xi * nr)]
                return tuple(out)

            init = []
            for k in range(len(slabs)):
                init += [lin[k][0], lin[k][1], zero, zero]
            res = lax.fori_loop(0, BLK, fix, tuple(init), unroll=2)
            for k, j in enumerate(slabs):
                da_ref[0, j:j + 1, :] = da_ref[0, j:j + 1, :] + _colsum(res[4 * k + 2])
                da_ref[0, HALF + j:HALF + j + 1, :] = da_ref[0, HALF + j:HALF + j + 1, :] + _colsum(res[4 * k + 3])
        lam = jnp.concatenate([gs[j].astype(BF16) for j in range(2 * HALF)], axis=1)
        us[...] = _nt(lam, b_ref[0]) + d_ref[0] * dyv
        _deinterleave(us, du_ref)
        db_ref[0] = db_ref[0] + _tn(ub, lam)
        dc_ref[0] = dc_ref[0] + _tn(dyb, x_ref[0])
        dd_ref[0] = dd_ref[0] + _colsum(dyv * uv)

    rev = lambda lb, sb: (last - sb, lb)
    per_lb = lambda lb, sb: (lb, 0, 0)
    wide = 2 * HALF * LANES
    return pl.pallas_call(
        body, name="s5_bwd", grid=(N_LB, nt),
        in_specs=[pl.BlockSpec((TILE, LANES), rev), pl.BlockSpec((TILE, LANES), rev),
                  pl.BlockSpec((1, TILE, wide), lambda lb, sb: (lb, last - sb, 0)),
                  pl.BlockSpec((1, 2 * HALF, LANES), per_lb), pl.BlockSpec((1, LANES, wide), per_lb),
                  pl.BlockSpec((1, wide, LANES), per_lb), pl.BlockSpec((1, 1, LANES), per_lb)],
        out_specs=[pl.BlockSpec((TILE, LANES), rev), pl.BlockSpec((1, LANES, wide), per_lb),
                   pl.BlockSpec((1, LANES, wide), per_lb), pl.BlockSpec((1, 2 * HALF, LANES), per_lb),
                   pl.BlockSpec((1, 1, LANES), per_lb)],
        out_shape=[jax.ShapeDtypeStruct((t_len, SSM_W), F32), jax.ShapeDtypeStruct((N_LB, LANES, wide), F32),
                   jax.ShapeDtypeStruct((N_LB, LANES, wide), F32), jax.ShapeDtypeStruct((N_LB, 2 * HALF, LANES), F32),
                   jax.ShapeDtypeStruct((N_LB, 1, LANES), F32)],
        scratch_shapes=[pltpu.VMEM((2 * HALF, TILE, LANES), F32),
                        pltpu.VMEM((TILE, LANES), F32), pltpu.VMEM((TILE, LANES), F32),
                        pltpu.VMEM((2 * HALF, BLK, LANES), F32), pltpu.VMEM((2 * HALF, LANES), F32),
                        pltpu.VMEM((2 * HALF, N_CHUNK, LANES), F32)],
        compiler_params=_params("arbitrary", "arbitrary"),
    )(u, dy, states, a_cat, b_mat, c_mat, d_skip)


_GELU_C = math.sqrt(2.0 / math.pi)
_GELU_K = 0.044715


def _gelu(y):
    t = jnp.tanh(_GELU_C * (y + _GELU_K * (y * y * y)))
    return y * (0.5 * (1.0 + t)), t


def _gelu_grad(y, t):
    return 0.5 * (1.0 + t) + 0.5 * y * (1.0 - t * t) * (_GELU_C * (1.0 + 3.0 * _GELU_K * y * y))


def _glu(y, wg, bias):
    z, t = _gelu(y)
    sg = jax.nn.sigmoid(_mm(z.astype(BF16), wg) + bias)
    return z, t, sg


def _mix_fwd(attn, y, x, wg, glu_b, ga, gs, wo):
    t_len = x.shape[0]
    tm = 512

    def body(attn_ref, y_ref, x_ref, wg_ref, b_ref, ga_ref, gs_ref, wo_ref, x2_ref, mix_ref, z_ref):
        z, _, sg = _glu(y_ref[...], wg_ref[...], b_ref[...])
        z_ref[...] = z.astype(BF16)
        s = z * sg
        av = attn_ref[...]
        an = (av * _rms(av) * ga_ref[...]).astype(BF16)
        sn = (s * _rms(s) * gs_ref[...]).astype(BF16)
        mix_ref[:, 0:ATTN_W] = an
        mix_ref[:, ATTN_W:] = sn
        x2_ref[...] = x_ref[...] + _mm(an, wo_ref[0:ATTN_W, :]) + _mm(sn, wo_ref[ATTN_W:, :])

    row = lambda i: (i, 0)
    const = lambda i: (0, 0)
    return pl.pallas_call(
        body, name="mix_fwd", grid=(t_len // tm,),
        in_specs=[pl.BlockSpec((tm, ATTN_W), row), pl.BlockSpec((tm, SSM_W), row), pl.BlockSpec((tm, D_MODEL), row),
                  pl.BlockSpec((SSM_W, SSM_W), const), pl.BlockSpec((1, SSM_W), const), pl.BlockSpec((1, ATTN_W), const),
                  pl.BlockSpec((1, SSM_W), const), pl.BlockSpec((D_MODEL, D_MODEL), const)],
        out_specs=[pl.BlockSpec((tm, D_MODEL), row), pl.BlockSpec((tm, D_MODEL), row), pl.BlockSpec((tm, SSM_W), row)],
        out_shape=[jax.ShapeDtypeStruct((t_len, D_MODEL), F32), jax.ShapeDtypeStruct((t_len, D_MODEL), BF16),
                   jax.ShapeDtypeStruct((t_len, SSM_W), BF16)],
        compiler_params=_params("arbitrary"),
    )(attn, y, x, wg, glu_b, ga, gs, wo)


def _mlp(x2, target, g2, wu, wd):
    t_len = x2.shape[0]
    tm = 256
    fc = 1024
    n_fc = D_FF // fc

    def body(x2_ref, tg_ref, g2_ref, wu_hbm, wd_hbm, dx2_ref, hdn_ref, dup_ref, h_ref, dyb_ref, dg2_ref, loss_ref,
             wu_s, wd_s, relu_s, sem):
        @pl.when(pl.program_id(0) == 0)
        def _():
            cu = pltpu.make_async_copy(wu_hbm, wu_s, sem.at[0])
            cd = pltpu.make_async_copy(wd_hbm, wd_s, sem.at[1])
            cu.start()
            cd.start()
            cu.wait()
            cd.wait()
            dg2_ref[...] = jnp.zeros_like(dg2_ref)
            loss_ref[...] = jnp.zeros_like(loss_ref)

        x2v = x2_ref[...]
        r = _rms(x2v)
        g2v = g2_ref[...]
        h = (x2v * r * g2v).astype(BF16)
        h_ref[...] = h
        yout = x2v
        for c in range(n_fc):
            cols = slice(c * fc, (c + 1) * fc)
            ru = jnp.maximum(_mm(h, wu_s[:, cols]), 0.0)
            relu_s[:, cols] = ru
            hd = (ru * ru).astype(BF16)
            hdn_ref[:, cols] = hd
            yout = yout + _mm(hd, wd_s[cols, :])
        err = yout - tg_ref[...]
        loss_ref[...] = loss_ref[...] + 0.5 * jnp.sum(err * err) * (1.0 / D_MODEL)
        dy = err * (1.0 / D_MODEL)
        dyb = dy.astype(BF16)
        dyb_ref[...] = dyb
        dh = jnp.zeros((tm, D_MODEL), F32)
        for c in range(n_fc):
            cols = slice(c * fc, (c + 1) * fc)
            dup = (_nt(dyb, wd_s[cols, :]) * (2.0 * relu_s[:, cols])).astype(BF16)
            dup_ref[:, cols] = dup
            dh = dh + _nt(dup, wu_s[:, cols])
        dxn, gterm = _rms_bwd(dh, x2v, r, g2v)
        dx2_ref[...] = dy + dxn
        dg2_ref[...] = dg2_ref[...] + _colsum(gterm)

    row = lambda i: (i, 0)
    const = lambda i: (0, 0)
    any_spec = pl.BlockSpec(memory_space=pl.ANY)
    return pl.pallas_call(
        body, name="mlp", grid=(t_len // tm,),
        in_specs=[pl.BlockSpec((tm, D_MODEL), row), pl.BlockSpec((tm, D_MODEL), row), pl.BlockSpec((1, D_MODEL), const),
                  any_spec, any_spec],
        out_specs=[pl.BlockSpec((tm, D_MODEL), row), pl.BlockSpec((tm, D_FF), row), pl.BlockSpec((tm, D_FF), row),
                   pl.BlockSpec((tm, D_MODEL), row), pl.BlockSpec((tm, D_MODEL), row), pl.BlockSpec((1, D_MODEL), const),
                   pl.BlockSpec((1, LANES), const)],
        out_shape=[jax.ShapeDtypeStruct((t_len, D_MODEL), F32), jax.ShapeDtypeStruct((t_len, D_FF), BF16),
                   jax.ShapeDtypeStruct((t_len, D_FF), BF16), jax.ShapeDtypeStruct((t_len, D_MODEL), BF16),
                   jax.ShapeDtypeStruct((t_len, D_MODEL), BF16), jax.ShapeDtypeStruct((1, D_MODEL), F32),
                   jax.ShapeDtypeStruct((1, LANES), F32)],
        scratch_shapes=[pltpu.VMEM((D_MODEL, D_FF), BF16), pltpu.VMEM((D_FF, D_MODEL), BF16),
                        pltpu.VMEM((tm, D_FF), F32), pltpu.SemaphoreType.DMA((2,))],
        compiler_params=_params("arbitrary"),
    )(x2, target, g2, wu, wd)


def _mix_bwd(dx2, attn, y, wg, glu_b, ga, gs, wo):
    t_len = dx2.shape[0]
    tm = 512

    def body(dx2_ref, attn_ref, y_ref, wg_ref, b_ref, ga_ref, gs_ref, wo_ref,
             dattn_ref, dy_ref, dx2b_ref, dgp_ref, dga_ref, dgs_ref, db_ref):
        @pl.when(pl.program_id(0) == 0)
        def _():
            dga_ref[...] = jnp.zeros_like(dga_ref)
            dgs_ref[...] = jnp.zeros_like(dgs_ref)
            db_ref[...] = jnp.zeros_like(db_ref)

        dx2b = dx2_ref[...].astype(BF16)
        dx2b_ref[...] = dx2b
        d_an = _nt(dx2b, wo_ref[0:ATTN_W, :])
        d_sn = _nt(dx2b, wo_ref[ATTN_W:, :])
        yv = y_ref[...]
        wg = wg_ref[...]
        z, t, sg = _glu(yv, wg, b_ref[...])
        s = z * sg
        av = attn_ref[...]
        d_attn, ga_term = _rms_bwd(d_an, av, _rms(av), ga_ref[...])
        d_s, gs_term = _rms_bwd(d_sn, s, _rms(s), gs_ref[...])
        dattn_ref[...] = d_attn
        dgp = d_s * z * sg * (1.0 - sg)
        dgpb = dgp.astype(BF16)
        dgp_ref[...] = dgpb
        dz = d_s * sg + _nt(dgpb, wg)
        dy_ref[...] = dz * _gelu_grad(yv, t)
        dga_ref[...] = dga_ref[...] + _colsum(ga_term)
        dgs_ref[...] = dgs_ref[...] + _colsum(gs_term)
        db_ref[...] = db_ref[...] + _colsum(dgp)

    row = lambda i: (i, 0)
    const = lambda i: (0, 0)
    vec = jax.ShapeDtypeStruct((1, SSM_W), F32)
    return pl.pallas_call(
        body, name="mix_bwd", grid=(t_len // tm,),
        in_specs=[pl.BlockSpec((tm, D_MODEL), row), pl.BlockSpec((tm, ATTN_W), row), pl.BlockSpec((tm, SSM_W), row),
                  pl.BlockSpec((SSM_W, SSM_W), const), pl.BlockSpec((1, SSM_W), const), pl.BlockSpec((1, ATTN_W), const),
                  pl.BlockSpec((1, SSM_W), const), pl.BlockSpec((D_MODEL, D_MODEL), const)],
        out_specs=[pl.BlockSpec((tm, ATTN_W), row), pl.BlockSpec((tm, SSM_W), row), pl.BlockSpec((tm, D_MODEL), row),
                   pl.BlockSpec((tm, SSM_W), row), pl.BlockSpec((1, ATTN_W), const), pl.BlockSpec((1, SSM_W), const),
                   pl.BlockSpec((1, SSM_W), const)],
        out_shape=[jax.ShapeDtypeStruct((t_len, ATTN_W), F32), jax.ShapeDtypeStruct((t_len, SSM_W), F32),
                   jax.ShapeDtypeStruct((t_len, D_MODEL), BF16), jax.ShapeDtypeStruct((t_len, SSM_W), BF16), vec, vec, vec],
        compiler_params=_params("arbitrary"),
    )(dx2, attn, y, wg, glu_b, ga, gs, wo)


def _mixer_mlp(attn, y, x, target, wg, glu_b, ga, gs, wo, g2, wu, wd):
    t_len = x.shape[0]
    tm = 256
    fc = 1024
    n_fc = D_FF // fc

    def body(attn_ref, y_ref, x_ref, tg_ref, b_ref, ga_ref, gs_ref, g2_ref, wg_hbm, wo_hbm, wu_hbm, wd_hbm,
             dx2_ref, hdn_ref, dup_ref, h_ref, dyb_ref, mix_ref, z_ref, dg2_ref, loss_ref,
             wg_s, wo_s, wu_s, wd_s, relu_s, sem):
        @pl.when(pl.program_id(0) == 0)
        def _():
            copies = [pltpu.make_async_copy(src, dst, sem.at[k])
                      for k, (src, dst) in enumerate(((wg_hbm, wg_s), (wo_hbm, wo_s), (wu_hbm, wu_s), (wd_hbm, wd_s)))]
            for cp in copies:
                cp.start()
            for cp in copies:
                cp.wait()
            for acc in (dg2_ref, loss_ref):
                acc[...] = jnp.zeros_like(acc)

        av = attn_ref[...]
        z, _, sg = _glu(y_ref[...], wg_s[...], b_ref[...])
        z_ref[...] = z.astype(BF16)
        s = z * sg
        an = (av * _rms(av) * ga_ref[...]).astype(BF16)
        sn = (s * _rms(s) * gs_ref[...]).astype(BF16)
        mix_ref[:, 0:ATTN_W] = an
        mix_ref[:, ATTN_W:] = sn
        x2v = x_ref[...] + _mm(an, wo_s[0:ATTN_W, :]) + _mm(sn, wo_s[ATTN_W:, :])
        r = _rms(x2v)
        g2v = g2_ref[...]
        h = (x2v * r * g2v).astype(BF16)
        h_ref[...] = h
        yout = x2v
        for c in range(n_fc):
            cols = slice(c * fc, (c + 1) * fc)
            ru = jnp.maximum(_mm(h, wu_s[:, cols]), 0.0)
            relu_s[:, cols] = ru
            hd = (ru * ru).astype(BF16)
            hdn_ref[:, cols] = hd
            yout = yout + _mm(hd, wd_s[cols, :])
        err = yout - tg_ref[...]
        loss_ref[...] = loss_ref[...] + 0.5 * jnp.sum(err * err) * (1.0 / D_MODEL)
        dy = err * (1.0 / D_MODEL)
        dyb = dy.astype(BF16)
        dyb_ref[...] = dyb
        dh = jnp.zeros((tm, D_MODEL), F32)
        for c in range(n_fc):
            cols = slice(c * fc, (c + 1) * fc)
            dup = (_nt(dyb, wd_s[cols, :]) * (2.0 * relu_s[:, cols])).astype(BF16)
            dup_ref[:, cols] = dup
            dh = dh + _nt(dup, wu_s[:, cols])
        dxn, g2_term = _rms_bwd(dh, x2v, r, g2v)
        dx2 = dy + dxn
        dx2_ref[...] = dx2
        dg2_ref[...] = dg2_ref[...] + _colsum(g2_term)

    row = lambda i: (i, 0)
    const = lambda i: (0, 0)
    wide = lambda n: pl.BlockSpec((tm, n), row)
    vec = lambda n: pl.BlockSpec((1, n), const)
    any_spec = pl.BlockSpec(memory_space=pl.ANY)
    f32 = lambda n: jax.ShapeDtypeStruct((t_len, n), F32)
    b16 = lambda n: jax.ShapeDtypeStruct((t_len, n), BF16)
    acc = lambda n: jax.ShapeDtypeStruct((1, n), F32)
    return pl.pallas_call(
        body, name="mixer_mlp", grid=(t_len // tm,),
        in_specs=[wide(ATTN_W), wide(SSM_W), wide(D_MODEL), wide(D_MODEL), vec(SSM_W), vec(ATTN_W), vec(SSM_W), vec(D_MODEL),
                  any_spec, any_spec, any_spec, any_spec],
        out_specs=[wide(D_MODEL), wide(D_FF), wide(D_FF), wide(D_MODEL), wide(D_MODEL), wide(D_MODEL), wide(SSM_W),
                   vec(D_MODEL), vec(LANES)],
        out_shape=[f32(D_MODEL), b16(D_FF), b16(D_FF), b16(D_MODEL), b16(D_MODEL), b16(D_MODEL), b16(SSM_W),
                   acc(D_MODEL), acc(LANES)],
        scratch_shapes=[pltpu.VMEM((SSM_W, SSM_W), BF16), pltpu.VMEM((D_MODEL, D_MODEL), BF16),
                        pltpu.VMEM((D_MODEL, D_FF), BF16), pltpu.VMEM((D_FF, D_MODEL), BF16),
                        pltpu.VMEM((tm, D_FF), F32), pltpu.SemaphoreType.DMA((4,))],
        compiler_params=_params("arbitrary"),
    )(attn, y, x, target, glu_b, ga, gs, g2, wg, wo, wu, wd)


def _inproj_bwd(dqs, dkn, dv, du, q_raw, k_raw, x, dx2, wi, g1, gq, gk, ones64):
    t_len = x.shape[0]
    tm = 512
    n_heads = ATTN_W // HEAD

    def body(dqs_ref, dkn_ref, dv_ref, du_ref, q_ref, k_ref, x_ref, dx2_ref, wi_ref, g1_ref, gq_ref, gk_ref, bd_ref,
             gx_ref, dproj_ref, dg1_ref, dgq_ref, dgk_ref, accq, acck):
        i = pl.program_id(0)

        @pl.when(i == 0)
        def _():
            dg1_ref[...] = jnp.zeros_like(dg1_ref)
            accq[...] = jnp.zeros_like(accq)
            acck[...] = jnp.zeros_like(acck)

        bd = bd_ref[...]

        def head_norm_bwd(dy, raw, gain, acc):
            r = lax.rsqrt(_group_mean(raw * raw, bd, HEAD) + EPS)
            xh = raw * r
            dxh = dy * gain
            acc[...] = acc[...] + _colsum(dy * xh)
            return r * (dxh - xh * _group_mean(dxh * xh, bd, HEAD))

        dq = head_norm_bwd(dqs_ref[...] * (HEAD ** -0.5), q_ref[...], gq_ref[...], accq)
        dk = head_norm_bwd(dkn_ref[...], k_ref[...], gk_ref[...], acck)
        dproj_ref[:, 0:ATTN_W] = dq.astype(BF16)
        dproj_ref[:, ATTN_W:2 * ATTN_W] = dk.astype(BF16)
        dproj_ref[:, 2 * ATTN_W:3 * ATTN_W] = dv_ref[...].astype(BF16)
        dproj_ref[:, 3 * ATTN_W:] = du_ref[...].astype(BF16)
        dxn = _nt(dproj_ref[...], wi_ref[...])
        xv = x_ref[...]
        g1v = g1_ref[...]
        dx, g1_term = _rms_bwd(dxn, xv, _rms(xv), g1v)
        gx_ref[...] = dx2_ref[...] + dx
        dg1_ref[...] = dg1_ref[...] + _colsum(g1_term)

        @pl.when(i == pl.num_programs(0) - 1)
        def _():
            for acc, out in ((accq, dgq_ref), (acck, dgk_ref)):
                tot = acc[:, 0:HEAD]
                for h in range(1, n_heads):
                    tot = tot + acc[:, h * HEAD:(h + 1) * HEAD]
                out[...] = tot

    row = lambda i: (i, 0)
    const = lambda i: (0, 0)
    aw = pl.BlockSpec((tm, ATTN_W), row)
    dm = pl.BlockSpec((tm, D_MODEL), row)
    return pl.pallas_call(
        body, name="inproj_bwd", grid=(t_len // tm,),
        in_specs=[aw, aw, aw, aw, aw, aw, dm, dm, pl.BlockSpec((D_MODEL, PROJ_W), const), pl.BlockSpec((1, D_MODEL), const),
                  pl.BlockSpec((1, ATTN_W), const), pl.BlockSpec((1, ATTN_W), const), pl.BlockSpec((ATTN_W, ATTN_W), const)],
        out_specs=[dm, pl.BlockSpec((tm, PROJ_W), row), pl.BlockSpec((1, D_MODEL), const),
                   pl.BlockSpec((1, HEAD), const), pl.BlockSpec((1, HEAD), const)],
        out_shape=[jax.ShapeDtypeStruct((t_len, D_MODEL), F32), jax.ShapeDtypeStruct((t_len, PROJ_W), BF16),
                   jax.ShapeDtypeStruct((1, D_MODEL), F32), jax.ShapeDtypeStruct((1, HEAD), F32),
                   jax.ShapeDtypeStruct((1, HEAD), F32)],
        scratch_shapes=[pltpu.VMEM((1, ATTN_W), F32), pltpu.VMEM((1, ATTN_W), F32)],
        compiler_params=_params("arbitrary"),
    )(dqs, dkn, dv, du, q_raw, k_raw, x, dx2, wi, g1, gq, gk, ones64)


def _grad_matmul(a, b, name):
    t_len, m = a.shape
    n = b.shape[1]
    bm, bn, bt = min(m, 1024), min(n, 1024), 1024

    def body(a_ref, b_ref, o_ref):
        @pl.when(pl.program_id(2) == 0)
        def _():
            o_ref[...] = jnp.zeros_like(o_ref)

        o_ref[...] = o_ref[...] + _tn(a_ref[...], b_ref[...])

    return pl.pallas_call(
        body, name=name, grid=(m // bm, n // bn, t_len // bt),
        in_specs=[pl.BlockSpec((bt, bm), lambda i, j, k: (k, i)), pl.BlockSpec((bt, bn), lambda i, j, k: (k, j))],
        out_specs=pl.BlockSpec((bm, bn), lambda i, j, k: (i, j)),
        out_shape=jax.ShapeDtypeStruct((m, n), F32),
        compiler_params=_params("arbitrary", "arbitrary", "arbitrary"),
    )(a, b)


def _adamw_update(w_ref, g_ref, m_ref, v_ref, d_o, m_o, v_o):
    gv = g_ref[...]
    mn = ADAM_B1 * m_ref[...] + (1.0 - ADAM_B1) * gv
    vn = ADAM_B2 * v_ref[...] + (1.0 - ADAM_B2) * jnp.square(gv)
    m_hat = mn / (1.0 - ADAM_B1 ** ADAM_STEP)
    v_hat = vn / (1.0 - ADAM_B2 ** ADAM_STEP)
    d_o[...] = -ADAM_LR * (m_hat / (jnp.sqrt(v_hat) + ADAM_EPS) + ADAM_WD * w_ref[...])
    m_o[...] = mn
    v_o[...] = vn


def _adamw_many(ws, gs, ms, vs):
    n = len(ws)

    def body(*refs):
        for i in range(n):
            _adamw_update(*[refs[k * n + i] for k in range(7)])

    shapes = [jax.ShapeDtypeStruct(w.shape, F32) for w in ws]
    outs = pl.pallas_call(body, name="adamw_small", out_shape=shapes * 3,
                          compiler_params=pltpu.CompilerParams(vmem_limit_bytes=VMEM_LIMIT))(*ws, *gs, *ms, *vs)
    return outs[0:n], outs[n:2 * n], outs[2 * n:]


def _adamw(w, g, m, v, name):
    rows, cols = w.shape
    br = _row_block(rows, 256)
    body = functools.partial(_adamw_update)

    spec = pl.BlockSpec((br, cols), lambda i: (i, 0))
    shape = jax.ShapeDtypeStruct((rows, cols), F32)
    return pl.pallas_call(
        body, name=name, grid=(rows // br,), in_specs=[spec] * 4, out_specs=[spec] * 3, out_shape=[shape] * 3,
        compiler_params=_params("arbitrary"),
    )(w, g, m, v)


def _sum_arrays(arrs, name, out_dtype=F32):
    rows, cols = arrs[0].shape
    br = _row_block(rows, 512)
    n = len(arrs)

    def body(*refs):
        tot = refs[0][...]
        for r in refs[1:n]:
            tot = tot + r[...]
        refs[n][...] = tot.astype(out_dtype)

    spec = pl.BlockSpec((br, cols), lambda i: (i, 0))
    return pl.pallas_call(
        body, name=name, grid=(rows // br,), in_specs=[spec] * n, out_specs=spec,
        out_shape=jax.ShapeDtypeStruct((rows, cols), out_dtype), compiler_params=_params("arbitrary"),
    )(*arrs)


GPL = N_GROUPS // N_LB
SW = GPL * N_STATE


def _eye_groups():
    return jnp.eye(GPL, dtype=F32)


def _s5_matrices(ab_r, ab_i, bb_r, bb_i, c_re, c_im, d_skip):
    eye = _eye_groups()
    a_cat = jnp.concatenate([ab_r.reshape(N_LB, HALF, LANES), ab_i.reshape(N_LB, HALF, LANES)], axis=1)

    def b_part(bb):
        b4 = jnp.transpose(bb.reshape(N_LB, GPL, N_STATE, GROUP_W), (0, 1, 3, 2))
        return (b4[:, :, :, None, :] * eye[None, :, None, :, None]).reshape(N_LB, LANES, SW)

    def c_part(cc):
        c4 = jnp.transpose(cc.reshape(N_LB, GPL, GROUP_W, N_STATE), (0, 1, 3, 2))
        return (c4[:, :, :, None, :] * eye[None, :, None, :, None]).reshape(N_LB, SW, LANES)

    b_mat = jnp.concatenate([b_part(bb_r), b_part(bb_i)], axis=2).astype(BF16)
    c_mat = jnp.concatenate([c_part(c_re), -c_part(c_im)], axis=1).astype(BF16)
    return a_cat, b_mat, c_mat, d_skip.reshape(N_LB, 1, LANES)


def _s5_unpack_grads(db, dc, da, dd):
    eye = _eye_groups()
    mask = eye[None, :, None, None, :, None]
    d6 = jnp.sum(db.reshape(N_LB, GPL, GROUP_W, 2, GPL, N_STATE) * mask, axis=4)
    dbb = jnp.transpose(d6, (3, 0, 1, 4, 2)).reshape(2, N_GROUPS * N_STATE, GROUP_W)
    c6 = jnp.sum(dc.reshape(N_LB, GPL, GROUP_W, 2, GPL, N_STATE) * mask, axis=4)
    dcc = jnp.transpose(c6, (3, 0, 1, 2, 4)).reshape(2, N_GROUPS, GROUP_W, N_STATE)
    dab_r = da[:, :HALF].reshape(N_GROUPS * N_STATE, 1)
    dab_i = da[:, HALF:].reshape(N_GROUPS * N_STATE, 1)
    return dab_r, dab_i, dbb[0], dbb[1], dcc[0], -dcc[1], dd.reshape(N_GROUPS, GROUP_W)


def _block_ones(n, width):
    i = lax.broadcasted_iota(jnp.int32, (n, n), 0) // width
    j = lax.broadcasted_iota(jnp.int32, (n, n), 1) // width
    return (i == j).astype(BF16)


def _tile_heads(g):
    return jnp.tile(g.reshape(1, HEAD), (1, ATTN_W // HEAD))


def _local_step(x, target, wi, rest, p, fwd_side=None, bwd_side=None):
    ones64 = _block_ones(ATTN_W, HEAD)
    ones_hp = _block_ones(LANES, HEAD)
    g1 = p["norm1_g"].reshape(1, D_MODEL)
    g2 = p["norm2_g"].reshape(1, D_MODEL)
    gq = _tile_heads(p["q_norm_g"])
    gk = _tile_heads(p["k_norm_g"])
    ga = p["attn_out_norm_g"].reshape(1, ATTN_W)
    gs = p["ssm_out_norm_g"].reshape(1, SSM_W)
    glu_b = p["glu_b"].reshape(1, SSM_W)
    n_gp = N_GROUPS * N_STATE
    lr = p["ssm_a_re"].reshape(n_gp, 1)
    li = p["ssm_a_im"].reshape(n_gp, 1)
    ldt = jnp.repeat(p["ssm_log_dt"].reshape(N_GROUPS), N_STATE).reshape(n_gp, 1)
    br = p["ssm_b_re"].reshape(n_gp, GROUP_W)
    bi = p["ssm_b_im"].reshape(n_gp, GROUP_W)
    ab_r, ab_i, bb_r, bb_i = _disc_fwd(lr, li, ldt, br, bi)
    a_cat, b_mat, c_mat, d_mat = _s5_matrices(
        ab_r, ab_i, bb_r, bb_i, p["ssm_c_re"].reshape(N_GROUPS, GROUP_W, N_STATE),
        p["ssm_c_im"].reshape(N_GROUPS, GROUP_W, N_STATE), p["ssm_d"])

    xn, qn, kn, vv, u, q_raw, k_raw = _inproj_fwd(x, g1, wi, gq, gk, ones64)
    if fwd_side is None:
        attn, lse = _attn_fwd(qn, kn, vv)
    else:
        attn, lse, *rest = _attn_fwd(qn, kn, vv, *fwd_side)
    wg, wo, wu, wd = rest
    y, states = _s5_fwd(u, a_cat, b_mat, c_mat, d_mat)
    x2, mix, z = _mix_fwd(attn, y, x, wg, glu_b, ga, gs, wo)
    dx2, hdn, dup, h, dyb, dg2, loss = _mlp(x2, target, g2, wu, wd)
    d_attn, dy_ssm, dx2b, dgp, dga, dgs, dglu_b = _mix_bwd(dx2, attn, y, wg, glu_b, ga, gs, wo)
    big = {"w_mlp_up": _grad_matmul(h, dup, "grad_w_mlp_up"), "w_mlp_down": _grad_matmul(hdn, dyb, "grad_w_mlp_down"),
           "w_out": _grad_matmul(mix, dx2b, "grad_w_out"), "glu_w": _grad_matmul(z, dgp, "grad_glu_w")}
    rode = []
    if bwd_side is None:
        dqs, dkn, dvv = _attn_bwd(qn, kn, vv, attn, d_attn, lse, ones_hp)
    else:
        dqs, dkn, dvv, *rode = _attn_bwd(qn, kn, vv, attn, d_attn, lse, ones_hp, *bwd_side(big))
    du, db, dc, da, dd = _s5_bwd(u, dy_ssm, states, a_cat, b_mat, c_mat, d_mat)
    grad_x, dproj, dg1, dgq, dgk = _inproj_bwd(dqs, dkn, dvv, du, q_raw, k_raw, x, dx2, wi, g1, gq, gk, ones64)

    big["w_in"] = _grad_matmul(xn, dproj, "grad_w_in")
    dab_r, dab_i, dbb_r, dbb_i, dc_re, dc_im, dd_g = _s5_unpack_grads(db, dc, da, dd)
    cot = {"norm1_g": dg1, "q_norm_g": dgq, "k_norm_g": dgk, "ab_r": dab_r, "ab_i": dab_i, "bb_r": dbb_r, "bb_i": dbb_i,
           "ssm_c_re": dc_re, "ssm_c_im": dc_im, "ssm_d": dd_g, "glu_b": dglu_b, "attn_out_norm_g": dga,
           "ssm_out_norm_g": dgs, "norm2_g": dg2}
    return loss[0, 0], grad_x, big, cot, (lr, li, ldt, br, bi), rode


COT_NAMES = ("norm1_g", "q_norm_g", "k_norm_g", "ab_r", "ab_i", "bb_r", "bb_i", "ssm_c_re", "ssm_c_im", "ssm_d",
             "glu_b", "attn_out_norm_g", "ssm_out_norm_g", "norm2_g")
SMALL_NAMES = ("norm1_g", "q_norm_g", "k_norm_g", "ssm_a_re", "ssm_a_im", "ssm_log_dt", "ssm_b_re", "ssm_b_im",
               "ssm_c_re", "ssm_c_im", "ssm_d", "glu_b", "attn_out_norm_g", "ssm_out_norm_g", "norm2_g")
BIG_NAMES = ("w_in", "glu_w", "w_out", "w_mlp_up", "w_mlp_down")
PACK_ROWS = 1152


def _pack(arrs):
    flat = jnp.concatenate([a.reshape(-1) for a in arrs])
    return jnp.pad(flat, (0, PACK_ROWS * LANES - flat.shape[0])).reshape(PACK_ROWS, LANES)


def _unpack(packed, like):
    flat = packed.reshape(-1)
    out, pos = [], 0
    for a in like:
        out.append(flat[pos:pos + a.size].reshape(a.shape))
        pos += a.size
    return out


def _small_grads(cot, disc_in, p):
    lr, li, ldt, br, bi = disc_in
    group_sum = (lax.broadcasted_iota(jnp.int32, (N_GROUPS, N_GROUPS * N_STATE), 1) // N_STATE
                 == lax.broadcasted_iota(jnp.int32, (N_GROUPS, N_GROUPS * N_STATE), 0)).astype(F32)
    dlr, dli, dldt, dbr, dbi = _disc_bwd(lr, li, ldt, br, bi, cot["ab_r"], cot["ab_i"], cot["bb_r"], cot["bb_i"], group_sum)
    g = dict(cot)
    g.update(ssm_a_re=dlr, ssm_a_im=dli, ssm_log_dt=dldt[:, 0], ssm_b_re=dbr, ssm_b_im=dbi)
    return {n: g[n].reshape(p[n].shape) for n in SMALL_NAMES}


BIG = {
    "w_in": ((D_MODEL, PROJ_W), 1, PROJ_W // 4, 0, D_MODEL // 2),
    "glu_w": ((SSM_W, SSM_W), 0, SSM_W // 4, 1, SSM_W // 2),
    "w_out": ((D_MODEL, D_MODEL), 0, D_MODEL // 4, 1, D_MODEL // 2),
    "w_mlp_up": ((D_MODEL, D_FF), 1, D_FF // 4, 0, D_MODEL // 2),
    "w_mlp_down": ((D_FF, D_MODEL), 0, D_FF // 4, 1, D_MODEL // 2),
}
N_BIG = len(BIG_NAMES)
N_CHIPS = 4
ANY = pl.BlockSpec(memory_space=pl.ANY)


def _cut(name, shard=False, half=False):
    shape, s_ax, s_sz, h_ax, h_sz = BIG[name]
    shape = list(shape)
    if shard:
        shape[s_ax] = s_sz
    if half:
        shape[h_ax] = h_sz
    return tuple(shape)


def _window(name, base, shard=None, half=None):
    _, s_ax, s_sz, h_ax, h_sz = BIG[name]
    idx = [pl.ds(0, base[0]), pl.ds(0, base[1])]
    if shard is not None:
        idx[s_ax] = pl.ds(pl.multiple_of(shard * s_sz, s_sz), s_sz)
    if half is not None:
        idx[h_ax] = pl.ds(pl.multiple_of(half * h_sz, h_sz), h_sz)
    return tuple(idx)


def _mesh_pos():
    return lax.axis_index("x"), lax.axis_index("y"), lax.axis_index("c")


def _other_chips(x, y):
    return [(1 - x, y, 2 * (1 - x) + y), (x, 1 - y, 2 * x + 1 - y), (1 - x, 1 - y, 2 * (1 - x) + 1 - y)]


def _remote(src, dst, send_sem, recv_sem, dev):
    return pltpu.make_async_remote_copy(src_ref=src, dst_ref=dst, send_sem=send_sem, recv_sem=recv_sem,
                                        device_id=dev, device_id_type=MESH)


def _start_remote(src, dst, send_sem, recv_sem, dev):
    cp = _remote(src, dst, send_sem, recv_sem, dev)
    cp.start()
    return cp


class _Gather:
    def __init__(self, names):
        self.names = tuple(names)
        self.n = len(self.names)

    def in_specs(self):
        return [pl.BlockSpec(memory_space=pltpu.VMEM)] * self.n

    def out_specs(self):
        return [ANY] * self.n

    def out_shapes(self):
        return [jax.ShapeDtypeStruct(BIG[w][0], BF16) for w in self.names]

    def scratch_shapes(self):
        n_sem = (N_CHIPS - 1) * self.n
        return ([pltpu.VMEM(_cut(w, shard=True), BF16) for w in self.names]
                + [pltpu.SemaphoreType.DMA((n_sem,))] * 4 + [pltpu.SemaphoreType.DMA((self.n,))])

    def bind(self, ins, outs, scratch):
        self.ins, self.outs = ins, outs
        self.stage = scratch[:self.n]
        self.send, self.recv, self.fsend, self.frecv, self.lsem = scratch[self.n:]

    def _copies(self):
        x, y, c = _mesh_pos()
        me = 2 * x + y
        sib = (x, y, 1 - c)
        local, sends, lands, fwds, flands = [], [], [], [], []
        for w, n in enumerate(self.names):
            local.append(pltpu.make_async_copy(self.stage[w], self.outs[w].at[_window(n, BIG[n][0], shard=me)], self.lsem.at[w]))
        for k, (px, py, pj) in enumerate(_other_chips(x, y)):
            for w, n in enumerate(self.names):
                s = k * self.n + w
                sends.append(_remote(self.stage[w].at[_window(n, _cut(n, shard=True), half=c)],
                                     self.outs[w].at[_window(n, BIG[n][0], shard=me, half=c)],
                                     self.send.at[s], self.recv.at[s], (px, py, c)))
                got = self.outs[w].at[_window(n, BIG[n][0], shard=pj, half=c)]
                lands.append(_remote(got, got, self.send.at[s], self.recv.at[s], (px, py, c)))
                fwds.append(_remote(got, got, self.fsend.at[s], self.frecv.at[s], sib))
                theirs = self.outs[w].at[_window(n, BIG[n][0], shard=pj, half=1 - c)]
                flands.append(_remote(theirs, theirs, self.fsend.at[s], self.frecv.at[s], sib))
        return local, sends, lands, fwds, flands

    def start(self):
        for w in range(self.n):
            self.stage[w][...] = self.ins[w][...].astype(BF16)
        local, sends, _, _, _ = self._copies()
        for cp in local + sends:
            cp.start()

    def forward(self):
        _, _, lands, fwds, _ = self._copies()
        for land, fwd in zip(lands, fwds):
            land.wait_recv()
            fwd.start()

    def finish(self):
        local, sends, _, fwds, flands = self._copies()
        for cp in flands:
            cp.wait_recv()
        for cp in sends + fwds:
            cp.wait_send()
        for cp in local:
            cp.wait()


def _gather_weights(shards, names):
    g = _Gather(names)

    def body(*refs):
        g.bind(refs[0:g.n], refs[g.n:2 * g.n], refs[2 * g.n:])
        g.start()
        g.forward()
        g.finish()

    return pl.pallas_call(
        body, name="gather_" + "_".join(names), in_specs=g.in_specs(), out_specs=g.out_specs(), out_shape=g.out_shapes(),
        scratch_shapes=g.scratch_shapes(), compiler_params=pltpu.CompilerParams(vmem_limit_bytes=VMEM_LIMIT),
    )(*[shards[n] for n in names])


def _pair_exchange(grads, names, packed=None):
    n_big = len(names)
    n_all = n_big + (packed is not None)

    def body(*refs):
        ins, got = refs[0:n_all], refs[n_all:2 * n_all]
        send, recv = refs[2 * n_all:]
        x, y, c = _mesh_pos()
        sib = (x, y, 1 - c)
        copies = []
        for w, n in enumerate(names):
            copies.append(_start_remote(ins[w].at[_window(n, BIG[n][0], half=1 - c)], got[w], send.at[w], recv.at[w], sib))
        if packed is not None:
            copies.append(_start_remote(ins[n_big], got[n_big], send.at[n_big], recv.at[n_big], sib))
        for cp in copies:
            cp.wait()

    shapes = [jax.ShapeDtypeStruct(_cut(n, half=True), F32) for n in names]
    args = [grads[n] for n in names]
    if packed is not None:
        shapes.append(jax.ShapeDtypeStruct(packed.shape, F32))
        args.append(packed)
    return pl.pallas_call(
        body, name="grad_pair_exchange_" + "_".join(names), in_specs=[ANY] * n_all, out_specs=[ANY] * n_all, out_shape=shapes,
        scratch_shapes=[pltpu.SemaphoreType.DMA((n_all,)), pltpu.SemaphoreType.DMA((n_all,))],
    )(*args)


def _pair_sum(name, full, got, core):
    _, _, _, h_ax, _ = BIG[name]
    rows, cols = _cut(name, half=True)
    br = _row_block(rows, 512)
    nb = rows // br
    own_map = (lambda i, c: (i + c[0] * nb, 0)) if h_ax == 0 else (lambda i, c: (i, c[0]))

    def body(c_ref, own_ref, got_ref, o_ref):
        o_ref[...] = (own_ref[...] + got_ref[...]).astype(BF16)

    plain = pl.BlockSpec((br, cols), lambda i, c: (i, 0))
    return pl.pallas_call(
        body, name="pair_sum_" + name,
        grid_spec=pltpu.PrefetchScalarGridSpec(num_scalar_prefetch=1, grid=(nb,),
                                               in_specs=[pl.BlockSpec((br, cols), own_map), plain], out_specs=plain),
        out_shape=jax.ShapeDtypeStruct((rows, cols), BF16), compiler_params=_params("arbitrary"),
    )(core, full, got)


class _ChipExchange:
    def __init__(self, names, packed_shape=None):
        self.names = tuple(names)
        self.packed_shape = packed_shape
        self.n = len(self.names) + (packed_shape is not None)

    def in_specs(self):
        return [ANY] * self.n

    def out_specs(self):
        return [ANY] * self.n

    def out_shapes(self):
        shapes = [jax.ShapeDtypeStruct((N_CHIPS,) + _cut(w, shard=True, half=True), BF16) for w in self.names]
        if self.packed_shape is not None:
            shapes.append(jax.ShapeDtypeStruct((N_CHIPS,) + tuple(self.packed_shape), F32))
        return shapes

    def scratch_shapes(self):
        n_sem = (N_CHIPS - 1) * self.n
        return [pltpu.SemaphoreType.DMA((n_sem,)), pltpu.SemaphoreType.DMA((n_sem,))]

    def bind(self, ins, outs, scratch):
        self.ins, self.outs = ins, outs
        self.send, self.recv = scratch

    def _piece(self, w, shard):
        if w >= len(self.names):
            return self.ins[w]
        n = self.names[w]
        return self.ins[w].at[_window(n, _cut(n, half=True), shard=shard)]

    def _copies(self):
        x, y, c = _mesh_pos()
        me = 2 * x + y
        sends, lands = [], []
        for k, (px, py, pj) in enumerate(_other_chips(x, y)):
            for w in range(self.n):
                s = k * self.n + w
                sends.append(_remote(self._piece(w, pj), self.outs[w].at[me], self.send.at[s], self.recv.at[s], (px, py, c)))
                lands.append(_remote(self._piece(w, me), self.outs[w].at[pj], self.send.at[s], self.recv.at[s], (px, py, c)))
        return sends, lands

    def start(self):
        for cp in self._copies()[0]:
            cp.start()

    def finish(self):
        sends, lands = self._copies()
        for cp in lands:
            cp.wait_recv()
        for cp in sends:
            cp.wait_send()


def _chip_exchange(halves, packed, names):
    ex = _ChipExchange(names, packed.shape)

    def body(*refs):
        ex.bind(refs[0:ex.n], refs[ex.n:2 * ex.n], refs[2 * ex.n:])
        ex.start()
        ex.finish()

    return pl.pallas_call(
        body, name="grad_chip_exchange", in_specs=ex.in_specs(), out_specs=ex.out_specs(), out_shape=ex.out_shapes(),
        scratch_shapes=ex.scratch_shapes(),
    )(*halves, packed)


def _chip_sum(name, own, slots, chip):
    n_slot, rows, cols = slots.shape
    br = _row_block(rows, 512)
    nb = rows // br
    if name in BIG and BIG[name][1] == 1:
        own_map = lambda i, m: (i, m[0])
    elif name in BIG:
        own_map = lambda i, m: (i + m[0] * nb, 0)
    else:
        own_map = lambda i, m: (i, 0)

    def slot_map(j):
        return lambda i, m: (jnp.where(m[0] == j, (j + 1) % n_slot, j), i, 0)

    def body(m_ref, own_ref, *refs):
        own_blk = own_ref[...].astype(F32)
        tot = None
        for j in range(n_slot):
            term = jnp.where(m_ref[0] == j, own_blk, refs[j][...].astype(F32))
            tot = term if tot is None else tot + term
        refs[n_slot][...] = tot

    in_specs = [pl.BlockSpec((br, cols), own_map)] + [pl.BlockSpec((None, br, cols), slot_map(j)) for j in range(n_slot)]
    return pl.pallas_call(
        body, name="chip_sum_" + name,
        grid_spec=pltpu.PrefetchScalarGridSpec(num_scalar_prefetch=1, grid=(nb,), in_specs=in_specs,
                                               out_specs=pl.BlockSpec((br, cols), lambda i, m: (i, 0))),
        out_shape=jax.ShapeDtypeStruct((rows, cols), F32), compiler_params=_params("arbitrary"),
    )(chip, own, *([slots] * n_slot))


def _half_exchange(pieces):
    def body(*refs):
        ins, outs = refs[0:N_BIG], refs[N_BIG:2 * N_BIG]
        send, recv = refs[2 * N_BIG:]
        x, y, c = _mesh_pos()
        sib = (x, y, 1 - c)
        copies = []
        for w, n in enumerate(BIG_NAMES):
            copies.append(_start_remote(ins[w], outs[w], send.at[w], recv.at[w], sib))
        for cp in copies:
            cp.wait()

    return pl.pallas_call(
        body, name="grad_half_exchange", in_specs=[ANY] * N_BIG, out_specs=[ANY] * N_BIG,
        out_shape=[jax.ShapeDtypeStruct(_cut(n, shard=True, half=True), F32) for n in BIG_NAMES],
        scratch_shapes=[pltpu.SemaphoreType.DMA((N_BIG,)), pltpu.SemaphoreType.DMA((N_BIG,))],
    )(*pieces)


WEIGHT_NAMES = ("norm1_g", "w_in", "q_norm_g", "k_norm_g", "ssm_a_re", "ssm_a_im", "ssm_log_dt", "ssm_b_re", "ssm_b_im",
                "ssm_c_re", "ssm_c_im", "ssm_d", "glu_w", "glu_b", "attn_out_norm_g", "ssm_out_norm_g", "w_out", "norm2_g",
                "w_mlp_up", "w_mlp_down")


def _train_step(a):
    x = a["x"][0]
    target = a["loss_target"][0]
    shards = {n: a[n][0] for n in BIG_NAMES}
    p = {n: a[n][0] for n in SMALL_NAMES}
    core = lax.axis_index("c").astype(jnp.int32).reshape(1)
    chip_id = (2 * lax.axis_index("x") + lax.axis_index("y")).astype(jnp.int32).reshape(1)

    later = ("glu_w", "w_out", "w_mlp_up", "w_mlp_down")
    early = ("w_mlp_up", "w_mlp_down", "w_out", "glu_w")
    late = ("w_in",)
    (wi,) = _gather_weights(shards, ("w_in",))
    chip = {}

    def bwd_side(grads):
        got = _pair_exchange(grads, early)
        for n, g in zip(early, got):
            chip[n] = _pair_sum(n, grads[n], g, core)
        return _ChipExchange(early), [chip[n] for n in early]

    loss, grad_x, big, cot, disc_in, early_slots = _local_step(
        x, target, wi, None, p, fwd_side=(_Gather(later), [shards[n] for n in later]), bwd_side=bwd_side)
    slots = dict(zip(early, early_slots))

    cot_list = [cot[n] for n in COT_NAMES]
    packed = _pack(cot_list)
    *got, got_packed = _pair_exchange(big, late, packed)
    for n, g in zip(late, got):
        chip[n] = _pair_sum(n, big[n], g, core)
    chip_packed = _sum_arrays([packed, got_packed], "pair_sum_small")
    *late_slots, small_slots = _chip_exchange([chip[n] for n in late], chip_packed, late)
    slots.update(zip(late, late_slots))
    pieces = [_chip_sum(n, chip[n], slots[n], chip_id) for n in BIG_NAMES]
    small_sum = _chip_sum("small", chip_packed, small_slots, chip_id)
    shard_grads = {}
    for n, mine, theirs in zip(BIG_NAMES, pieces, _half_exchange(pieces)):
        h_ax = BIG[n][3]
        shard_grads[n] = jnp.where(core[0] == 0, jnp.concatenate([mine, theirs], axis=h_ax),
                                   jnp.concatenate([theirs, mine], axis=h_ax))
    small_grads = _small_grads(dict(zip(COT_NAMES, _unpack(small_sum, cot_list))), disc_in, p)

    grads, delta, new_m, new_v = {}, {}, {}, {}
    for n in BIG_NAMES:
        grads[n] = shard_grads[n]
        delta[n], new_m[n], new_v[n] = _adamw(a[n][0], grads[n], a["m_" + n][0], a["v_" + n][0], "adamw_" + n)
    flat2 = lambda t: t.reshape(-1, t.shape[-1])
    res = _adamw_many([flat2(p[n]) for n in SMALL_NAMES], [flat2(small_grads[n]) for n in SMALL_NAMES],
                      [flat2(a["m_" + n][0]) for n in SMALL_NAMES], [flat2(a["v_" + n][0]) for n in SMALL_NAMES])
    for store, outs in zip((delta, new_m, new_v), res):
        store.update(zip(SMALL_NAMES, outs))
    grads.update(small_grads)

    total = lax.psum(loss, ("x", "y", "c"))
    out = [total, grad_x[None]]
    for store in (grads, delta, new_m, new_v):
        out += [store[n].reshape(a[n].shape) for n in WEIGHT_NAMES]
    return tuple(out)


def kernel(x, norm1_g, w_in, q_norm_g, k_norm_g, ssm_a_re, ssm_a_im, ssm_log_dt, ssm_b_re, ssm_b_im, ssm_c_re, ssm_c_im, ssm_d, glu_w, glu_b, attn_out_norm_g, ssm_out_norm_g, w_out, norm2_g, w_mlp_up, w_mlp_down, loss_target, m_norm1_g, m_w_in, m_q_norm_g, m_k_norm_g, m_ssm_a_re, m_ssm_a_im, m_ssm_log_dt, m_ssm_b_re, m_ssm_b_im, m_ssm_c_re, m_ssm_c_im, m_ssm_d, m_glu_w, m_glu_b, m_attn_out_norm_g, m_ssm_out_norm_g, m_w_out, m_norm2_g, m_w_mlp_up, m_w_mlp_down, v_norm1_g, v_w_in, v_q_norm_g, v_k_norm_g, v_ssm_a_re, v_ssm_a_im, v_ssm_log_dt, v_ssm_b_re, v_ssm_b_im, v_ssm_c_re, v_ssm_c_im, v_ssm_d, v_glu_w, v_glu_b, v_attn_out_norm_g, v_ssm_out_norm_g, v_w_out, v_norm2_g, v_w_mlp_up, v_w_mlp_down):
    return _train_step(dict(locals()))
```

```python
import functools
import math

import jax
import jax.numpy as jnp
from jax import lax
from jax.experimental import pallas as pl
from jax.experimental.pallas import tpu as pltpu

F32 = jnp.float32
BF16 = jnp.bfloat16
MESH = pl.DeviceIdType.MESH

D_MODEL = 1024
ATTN_W = 512
SSM_W = 512
HEAD = 64
D_FF = 4096
PROJ_W = 2048
N_GROUPS = 32
N_STATE = 64
GROUP_W = 16
EPS = 1e-6
NEG = -1e30
DILATIONS = (1, 4, 16)
BLK = 128
TILE = 2048
LANES = 128
N_LB = SSM_W // LANES
N_SLAB = 2 * N_LB * N_STATE * 8 // LANES // N_LB
VMEM_LIMIT = 56 * 1024 * 1024

ADAM_LR, ADAM_B1, ADAM_B2, ADAM_EPS, ADAM_WD, ADAM_STEP = 0.001, 0.9, 0.999, 1e-08, 0.01, 10


def _params(*sem):
    return pltpu.CompilerParams(dimension_semantics=sem, vmem_limit_bytes=VMEM_LIMIT)


def _nt(a, b):
    return lax.dot_general(a, b, (((1,), (1,)), ((), ())), preferred_element_type=F32)


def _tn(a, b):
    return lax.dot_general(a, b, (((0,), (0,)), ((), ())), preferred_element_type=F32)


def _mm(a, b):
    return jnp.dot(a, b, preferred_element_type=F32)


def _group_mean(t, ones_bd, width):
    hi = t.astype(BF16)
    lo = (t - hi.astype(F32)).astype(BF16)
    return (_mm(hi, ones_bd) + _mm(lo, ones_bd)) * (1.0 / width)


def _rms(x):
    return lax.rsqrt(jnp.mean(x * x, axis=-1, keepdims=True) + EPS)


def _rms_bwd(dy, x, r, g):
    xh = x * r
    dxh = dy * g
    dx = r * (dxh - xh * jnp.mean(dxh * xh, axis=-1, keepdims=True))
    return dx, dy * xh


def _colsum(x):
    return jnp.sum(x, axis=0, keepdims=True)


def _row_block(rows, cap):
    for b in range(min(rows, cap) // 8 * 8, 0, -8):
        if rows % b == 0:
            return b
    raise ValueError(f"no row block for {rows} rows")


def _inproj_fwd(x, g1, wi, gq, gk, ones64):
    t_len = x.shape[0]
    tm = 512
    n_hp = ATTN_W // LANES

    def body(x_ref, g1_ref, wi_ref, gq_ref, gk_ref, bd_ref, xn_ref, q_ref, k_ref, v_ref, u_ref, qr_ref, kr_ref):
        xv = x_ref[...]
        xn = (xv * _rms(xv) * g1_ref[...]).astype(BF16)
        xn_ref[...] = xn
        proj = _mm(xn, wi_ref[...])
        q = proj[:, 0:ATTN_W]
        k = proj[:, ATTN_W:2 * ATTN_W]
        v = proj[:, 2 * ATTN_W:3 * ATTN_W]
        u_ref[...] = proj[:, 3 * ATTN_W:]
        qr_ref[...] = q
        kr_ref[...] = k
        bd = bd_ref[...]
        qn = q * lax.rsqrt(_group_mean(q * q, bd, HEAD) + EPS) * gq_ref[...] * (HEAD ** -0.5)
        kn = k * lax.rsqrt(_group_mean(k * k, bd, HEAD) + EPS) * gk_ref[...]
        for hp in range(n_hp):
            sl = slice(hp * LANES, (hp + 1) * LANES)
            q_ref[hp] = qn[:, sl]
            k_ref[hp] = kn[:, sl]
            v_ref[hp] = v[:, sl]

    row = lambda i: (i, 0)
    const = lambda i: (0, 0)
    hp_spec = pl.BlockSpec((n_hp, tm, LANES), lambda i: (0, i, 0))
    hp_shape = jax.ShapeDtypeStruct((n_hp, t_len, LANES), F32)
    return pl.pallas_call(
        body, name="inproj_fwd", grid=(t_len // tm,),
        in_specs=[pl.BlockSpec((tm, D_MODEL), row), pl.BlockSpec((1, D_MODEL), const),
                  pl.BlockSpec((D_MODEL, PROJ_W), const), pl.BlockSpec((1, ATTN_W), const),
                  pl.BlockSpec((1, ATTN_W), const), pl.BlockSpec((ATTN_W, ATTN_W), const)],
        out_specs=[pl.BlockSpec((tm, D_MODEL), row), hp_spec, hp_spec, hp_spec,
                   pl.BlockSpec((tm, SSM_W), row), pl.BlockSpec((tm, ATTN_W), row), pl.BlockSpec((tm, ATTN_W), row)],
        out_shape=[jax.ShapeDtypeStruct((t_len, D_MODEL), BF16), hp_shape, hp_shape, hp_shape,
                   jax.ShapeDtypeStruct((t_len, SSM_W), F32), jax.ShapeDtypeStruct((t_len, ATTN_W), F32),
                   jax.ShapeDtypeStruct((t_len, ATTN_W), F32)],
        compiler_params=_params("arbitrary"),
    )(x, g1, wi, gq, gk, ones64)


def _attn_masks():
    head0 = lax.broadcasted_iota(jnp.int32, (BLK, LANES), 1) < HEAD
    row = lax.broadcasted_iota(jnp.int32, (2 * BLK, 2 * BLK), 0) & (BLK - 1)
    col = lax.broadcasted_iota(jnp.int32, (2 * BLK, 2 * BLK), 1)
    return head0, (col < BLK) & (col >= row), (col >= BLK) & (col - BLK <= row)


def _stack_heads(x, head0):
    return jnp.concatenate([jnp.where(head0, x, 0.0), jnp.where(head0, 0.0, x)], axis=0).astype(BF16)


def _unit_rows(uidx, d):
    nb = TILE // (BLK * d)
    r = lax.div(uidx, nb)
    b = lax.rem(uidx, nb)
    start = r + d * BLK * b
    if d == 1:
        start = pl.multiple_of(start, BLK)
        mk = lambda s: pl.ds(pl.multiple_of(s, BLK), BLK)
    else:
        mk = lambda s: pl.ds(s, BLK, stride=d)
    return b, mk(start), mk(TILE + start), mk(TILE + start - d * BLK)


def _attn_fwd(q, k, v, side=None, side_args=()):
    n_hp, t_len, _ = q.shape
    nt = t_len // TILE
    ns = side.n if side is not None else 0
    n_steps = n_hp * nt

    def body(*refs):
        q_ref, kp_ref, kc_ref, vp_ref, vc_ref = refs[0:5]
        o_ref, lse_ref = refs[5 + ns:7 + ns]
        kk, vv, m_s, l_s, acc_s = refs[7 + 2 * ns:12 + 2 * ns]
        t = pl.program_id(1)
        step = pl.program_id(0) * nt + t
        if side is not None:
            side.bind(refs[5:5 + ns], refs[7 + ns:7 + 2 * ns], refs[12 + 2 * ns:])
            pl.when(step == 0)(side.start)
            pl.when(step == n_steps // 2)(side.forward)
        kk[0:TILE] = kp_ref[0]
        kk[TILE:] = kc_ref[0]
        vv[0:TILE] = vp_ref[0]
        vv[TILE:] = vc_ref[0]
        head0, band_prev, band_cur = _attn_masks()

        for pi, d in enumerate(DILATIONS):
            def unit(uidx, carry, d=d, pi=pi):
                b, rows_q, rows_c, rows_p = _unit_rows(uidx, d)
                mask = band_cur | (band_prev & ((t > 0) | (b > 0)))
                q2 = _stack_heads(q_ref.at[0][rows_q, :], head0)
                kcat = jnp.concatenate([kk[rows_p, :], kk[rows_c, :]], axis=0).astype(BF16)
                vcat = jnp.concatenate([vv[rows_p, :], vv[rows_c, :]], axis=0).astype(BF16)
                s = jnp.where(mask, _nt(q2, kcat), NEG)
                m = jnp.max(s, axis=1, keepdims=True)
                p = jnp.exp(s - m)
                ls = jnp.sum(p, axis=1, keepdims=True)
                pv = _mm(p.astype(BF16), vcat)
                m_s.at[pi][rows_q, :] = jnp.where(head0, m[0:BLK], m[BLK:])
                l_s.at[pi][rows_q, :] = jnp.where(head0, ls[0:BLK], ls[BLK:])
                acc_s.at[pi][rows_q, :] = jnp.where(head0, pv[0:BLK], pv[BLK:])
                return carry

            lax.fori_loop(0, TILE // BLK, unit, 0, unroll=8)

        m_all = jnp.maximum(jnp.maximum(m_s[0], m_s[1]), m_s[2])
        num = jnp.zeros((TILE, LANES), F32)
        den = jnp.zeros((TILE, LANES), F32)
        for pi in range(len(DILATIONS)):
            wgt = jnp.exp(m_s[pi] - m_all)
            num = num + acc_s[pi] * wgt
            den = den + l_s[pi] * wgt
        o_ref[...] = num / den
        lse_ref[0] = m_all + jnp.log(den)
        if side is not None:
            pl.when(step == n_steps - 1)(side.finish)

    cur = lambda hp, t: (hp, t, 0)
    prev = lambda hp, t: (hp, jnp.maximum(t - 1, 0), 0)
    blk = (1, TILE, LANES)
    per_pattern = pltpu.VMEM((len(DILATIONS), TILE, LANES), F32)
    extra = (side.in_specs(), side.out_specs(), side.out_shapes(), side.scratch_shapes()) if side is not None else ([], [], [], [])
    return pl.pallas_call(
        body, name="attn_fwd", grid=(n_hp, nt),
        in_specs=[pl.BlockSpec(blk, cur), pl.BlockSpec(blk, prev), pl.BlockSpec(blk, cur),
                  pl.BlockSpec(blk, prev), pl.BlockSpec(blk, cur)] + extra[0],
        out_specs=[pl.BlockSpec((TILE, LANES), lambda hp, t: (t, hp)), pl.BlockSpec(blk, cur)] + extra[1],
        out_shape=[jax.ShapeDtypeStruct((t_len, ATTN_W), F32), jax.ShapeDtypeStruct((n_hp, t_len, LANES), F32)] + extra[2],
        scratch_shapes=[pltpu.VMEM((2 * TILE, LANES), F32), pltpu.VMEM((2 * TILE, LANES), F32),
                        per_pattern, per_pattern, per_pattern] + extra[3],
        compiler_params=_params("arbitrary", "arbitrary"),
    )(q, k, k, v, v, *side_args)


def _attn_bwd(q, k, v, o, do, lse, ones_hp, side=None, side_args=()):
    n_hp, t_len, _ = q.shape
    nt = t_len // TILE
    ns = side.n if side is not None else 0
    n_pat = len(DILATIONS)

    def body(*refs):
        q_ref, kp_ref, kc_ref, vp_ref, vc_ref, o_ref, do_ref, lse_ref, bd_ref = refs[0:9]
        dq_ref, dk_ref, dv_ref = refs[9 + ns:12 + ns]
        kk, vv, dq_s, dkc, dkp, dvc, dvp, hold_k, hold_v, dl_s = refs[12 + 2 * ns:22 + 2 * ns]
        t = pl.program_id(1)
        if side is not None:
            side.bind(refs[9:9 + ns], refs[12 + ns:12 + 2 * ns], refs[22 + 2 * ns:])
            pl.when((pl.program_id(0) == 0) & (t == 0))(side.start)

        @pl.when(t < nt)
        def _():
            kk[0:TILE] = kp_ref[0]
            kk[TILE:] = kc_ref[0]
            vv[0:TILE] = vp_ref[0]
            vv[TILE:] = vc_ref[0]
            dl_s[...] = _group_mean(do_ref[...] * o_ref[...], bd_ref[...], 1.0)
            head0, band_prev, band_cur = _attn_masks()

            for pi, d in enumerate(DILATIONS):
                def unit(uidx, carry, d=d, pi=pi):
                    b, rows_q, rows_c, rows_p = _unit_rows(uidx, d)
                    mask = band_cur | (band_prev & ((t > 0) | (b > 0)))
                    q2 = _stack_heads(q_ref.at[0][rows_q, :], head0)
                    do2 = _stack_heads(do_ref[rows_q, :], head0)
                    lse_f = lse_ref.at[0][rows_q, :]
                    dl_f = dl_s[rows_q, :]
                    lse2 = jnp.concatenate([lse_f[:, 0:1], lse_f[:, HEAD:HEAD + 1]], axis=0)
                    dl2 = jnp.concatenate([dl_f[:, 0:1], dl_f[:, HEAD:HEAD + 1]], axis=0)
                    kcat = jnp.concatenate([kk[rows_p, :], kk[rows_c, :]], axis=0).astype(BF16)
                    vcat = jnp.concatenate([vv[rows_p, :], vv[rows_c, :]], axis=0).astype(BF16)
                    p = jnp.where(mask, jnp.exp(_nt(q2, kcat) - lse2), 0.0)
                    ds = (p * (_nt(do2, vcat) - dl2)).astype(BF16)
                    dq2 = _mm(ds, kcat)
                    dq_s.at[pi][rows_q, :] = jnp.where(head0, dq2[0:BLK], dq2[BLK:])
                    dk2 = _tn(ds, q2)
                    dv2 = _tn(p.astype(BF16), do2)
                    dkp.at[pi][rows_q, :] = dk2[0:BLK]
                    dkc.at[pi][rows_q, :] = dk2[BLK:]
                    dvp.at[pi][rows_q, :] = dv2[0:BLK]
                    dvc.at[pi][rows_q, :] = dv2[BLK:]
                    return carry

                lax.fori_loop(0, TILE // BLK, unit, 0, unroll=8)

            dq_ref[...] = dq_s[0] + dq_s[1] + dq_s[2]

        @pl.when(t > 0)
        def _():
            dk_ref[...] = hold_k[...]
            dv_ref[...] = hold_v[...]

        @pl.when((t > 0) & (t < nt))
        def _():
            for pi, d in enumerate(DILATIONS):
                back = d * BLK
                dk_ref[TILE - back:, :] = dk_ref[TILE - back:, :] + dkp[pi, 0:back, :]
                dv_ref[TILE - back:, :] = dv_ref[TILE - back:, :] + dvp[pi, 0:back, :]

        @pl.when(t < nt)
        def _():
            hold_k[...] = dkc[0] + dkc[1] + dkc[2]
            hold_v[...] = dvc[0] + dvc[1] + dvc[2]
            for pi, d in enumerate(DILATIONS):
                back = d * BLK
                if back < TILE:
                    hold_k[0:TILE - back, :] = hold_k[0:TILE - back, :] + dkp[pi, back:, :]
                    hold_v[0:TILE - back, :] = hold_v[0:TILE - back, :] + dvp[pi, back:, :]

        if side is not None:
            pl.when((pl.program_id(0) == n_hp - 1) & (t == nt))(side.finish)

    last = nt - 1
    extra = (side.in_specs(), side.out_specs(), side.out_shapes(), side.scratch_shapes()) if side is not None else ([], [], [], [])
    cur = lambda hp, t: (hp, jnp.minimum(t, last), 0)
    prev = lambda hp, t: (hp, jnp.clip(t - 1, 0, last), 0)
    cur2 = lambda hp, t: (jnp.minimum(t, last), hp)
    prev2 = lambda hp, t: (jnp.maximum(t - 1, 0), hp)
    blk = (1, TILE, LANES)
    blk2 = (TILE, LANES)
    out = jax.ShapeDtypeStruct((t_len, ATTN_W), F32)
    return pl.pallas_call(
        body, name="attn_bwd", grid=(n_hp, nt + 1),
        in_specs=[pl.BlockSpec(blk, cur), pl.BlockSpec(blk, prev), pl.BlockSpec(blk, cur),
                  pl.BlockSpec(blk, prev), pl.BlockSpec(blk, cur), pl.BlockSpec(blk2, cur2),
                  pl.BlockSpec(blk2, cur2), pl.BlockSpec(blk, cur), pl.BlockSpec((LANES, LANES), lambda hp, t: (0, 0))]
        + extra[0],
        out_specs=[pl.BlockSpec(blk2, cur2), pl.BlockSpec(blk2, prev2), pl.BlockSpec(blk2, prev2)] + extra[1],
        out_shape=[out, out, out] + extra[2],
        scratch_shapes=[pltpu.VMEM((2 * TILE, LANES), F32), pltpu.VMEM((2 * TILE, LANES), F32)]
        + [pltpu.VMEM((n_pat, TILE, LANES), F32)] * 5 + [pltpu.VMEM((TILE, LANES), F32)] * 3 + extra[3],
        compiler_params=_params("arbitrary", "arbitrary"),
    )(q, k, k, v, v, o, do, lse, ones_hp, *side_args)


def _discretise(lr, li, ldt, br, bi):
    dt = jnp.exp(ldt)
    mag = jnp.exp(lr * dt)
    ab_r, ab_i = mag * jnp.cos(li * dt), mag * jnp.sin(li * dt)
    den = lr * lr + li * li
    nr, ni = ab_r - 1.0, ab_i
    cr = (nr * lr + ni * li) / den
    ci = (ni * lr - nr * li) / den
    return ab_r, ab_i, cr * br - ci * bi, cr * bi + ci * br


def _disc_fwd(lr, li, ldt, br, bi):
    def body(lr_ref, li_ref, ldt_ref, br_ref, bi_ref, ar_o, ai_o, bbr_o, bbi_o):
        outs = _discretise(lr_ref[...], li_ref[...], ldt_ref[...], br_ref[...], bi_ref[...])
        for o_ref, val in zip((ar_o, ai_o, bbr_o, bbi_o), outs):
            o_ref[...] = val

    col = jax.ShapeDtypeStruct(lr.shape, F32)
    mat = jax.ShapeDtypeStruct(br.shape, F32)
    return pl.pallas_call(body, name="s5_disc_fwd", out_shape=[col, col, mat, mat])(lr, li, ldt, br, bi)


def _disc_bwd(lr, li, ldt, br, bi, d_ar, d_ai, d_bbr, d_bbi, group_sum):
    def body(lr_ref, li_ref, ldt_ref, br_ref, bi_ref, c1, c2, c3, c4, gs_ref, dlr_o, dli_o, dldt_o, dbr_o, dbi_o):
        _, vjp = jax.vjp(_discretise, lr_ref[...], li_ref[...], ldt_ref[...], br_ref[...], bi_ref[...])
        dlr, dli, dldt, dbr, dbi = vjp((c1[...], c2[...], c3[...], c4[...]))
        dlr_o[...] = dlr
        dli_o[...] = dli
        dbr_o[...] = dbr
        dbi_o[...] = dbi
        wide = jnp.broadcast_to(dldt, (dldt.shape[0], LANES))
        dldt_o[...] = jnp.dot(gs_ref[...], wide, precision=lax.Precision.HIGHEST, preferred_element_type=F32)

    col = jax.ShapeDtypeStruct(lr.shape, F32)
    mat = jax.ShapeDtypeStruct(br.shape, F32)
    return pl.pallas_call(
        body, name="s5_disc_bwd", out_shape=[col, col, jax.ShapeDtypeStruct((N_GROUPS, LANES), F32), mat, mat],
    )(lr, li, ldt, br, bi, d_ar, d_ai, d_bbr, d_bbi, group_sum)


N_CHUNK = TILE // BLK
HALF = 4


def _cmul(ar, ai, xr, xi):
    return ar * xr - ai * xi, ar * xi + ai * xr


def _power_table(a_ref, tab, sign, reverse):
    ar = [a_ref[0, j:j + 1, :] for j in range(HALF)]
    ai = [sign * a_ref[0, HALF + j:HALF + j + 1, :] for j in range(HALF)]

    def step(s, cur):
        row = pl.ds((BLK - 1 - s) if reverse else s, 1)
        nxt = []
        for j in range(HALF):
            tab.at[j][row, :] = cur[j]
            tab.at[HALF + j][row, :] = cur[HALF + j]
            nxt.append(_cmul(ar[j], ai[j], cur[j], cur[HALF + j]))
        return tuple(p[0] for p in nxt) + tuple(p[1] for p in nxt)

    lax.fori_loop(0, BLK, step, tuple(ar) + tuple(ai))


def _interleave(src, dst):
    for c in range(N_CHUNK):
        dst[pl.ds(c, BLK, stride=N_CHUNK), :] = src[c * BLK:(c + 1) * BLK, :]


def _deinterleave(src, dst):
    for c in range(N_CHUNK):
        dst[c * BLK:(c + 1) * BLK, :] = src[pl.ds(c, BLK, stride=N_CHUNK), :]


def _step_rows(s):
    return pl.ds(pl.multiple_of(s * N_CHUNK, N_CHUNK), N_CHUNK)


def _chunk_scan(buf, a_ref, sign, reverse):
    ar = [jnp.broadcast_to(a_ref[0, j:j + 1, :], (N_CHUNK, LANES)) for j in range(HALF)]
    ai = [sign * jnp.broadcast_to(a_ref[0, HALF + j:HALF + j + 1, :], (N_CHUNK, LANES)) for j in range(HALF)]

    def step(i, carry):
        s = (BLK - 1 - i) if reverse else i
        rows = _step_rows(s)
        out = []
        for j in range(HALF):
            pr, pi = _cmul(ar[j], ai[j], carry[j], carry[HALF + j])
            xr = buf.at[j][rows, :] + pr
            xi = buf.at[HALF + j][rows, :] + pi
            buf.at[j][rows, :] = xr
            buf.at[HALF + j][rows, :] = xi
            out.append((xr, xi))
        return tuple(p[0] for p in out) + tuple(p[1] for p in out)

    zero = jnp.zeros((N_CHUNK, LANES), F32)
    lax.fori_loop(0, BLK, step, (zero,) * (2 * HALF), unroll=2)


def _chunk_states(buf, carry_s, xin_s, tab, reverse):
    edge = 0 if reverse else BLK - 1
    top = 0 if reverse else BLK - 1
    pw = [tab[j, top:top + 1, :] for j in range(2 * HALF)]
    cur = [carry_s[j:j + 1, :] for j in range(2 * HALF)]
    summary = [buf[j, edge * N_CHUNK:(edge + 1) * N_CHUNK, :] for j in range(2 * HALF)]
    order = range(N_CHUNK - 1, -1, -1) if reverse else range(N_CHUNK)
    for c in order:
        for j in range(2 * HALF):
            xin_s[j, c:c + 1, :] = cur[j]
        nxt = []
        for j in range(HALF):
            pr, pi = _cmul(pw[j], pw[HALF + j], cur[j], cur[HALF + j])
            nxt.append((pr + summary[j][c:c + 1, :], pi + summary[HALF + j][c:c + 1, :]))
        cur = [p[0] for p in nxt] + [p[1] for p in nxt]
    for j in range(2 * HALF):
        carry_s[j:j + 1, :] = cur[j]


def _s5_fwd(u, a_cat, b_mat, c_mat, d_skip):
    t_len = u.shape[0]
    nt = t_len // TILE

    def body(u_ref, a_ref, b_ref, c_ref, d_ref, y_ref, x_ref, xs, us, tab, carry_s, xin_s):
        sb = pl.program_id(1)

        @pl.when(sb == 0)
        def _():
            _power_table(a_ref, tab, 1.0, False)
            carry_s[...] = jnp.zeros_like(carry_s)

        _interleave(u_ref, us)
        uv = us[...]
        bu = _mm(uv.astype(BF16), b_ref[0])
        for j in range(2 * HALF):
            xs[j] = bu[:, j * LANES:(j + 1) * LANES]
        _chunk_scan(xs, a_ref, 1.0, False)
        _chunk_states(xs, carry_s, xin_s, tab, False)
        xin = [xin_s[j] for j in range(2 * HALF)]

        def fix(s, acc):
            rows = _step_rows(s)
            for j in range(HALF):
                pr, pi = _cmul(tab.at[j][pl.ds(s, 1), :], tab.at[HALF + j][pl.ds(s, 1), :], xin[j], xin[HALF + j])
                xs.at[j][rows, :] = xs.at[j][rows, :] + pr
                xs.at[HALF + j][rows, :] = xs.at[HALF + j][rows, :] + pi
            return acc

        lax.fori_loop(0, BLK, fix, 0, unroll=2)
        xcat = jnp.concatenate([xs[j].astype(BF16) for j in range(2 * HALF)], axis=1)
        x_ref[0] = xcat
        us[...] = d_ref[0] * uv + _mm(xcat, c_ref[0])
        _deinterleave(us, y_ref)

    return pl.pallas_call(
        body, name="s5_fwd", grid=(N_LB, nt),
        in_specs=[pl.BlockSpec((TILE, LANES), lambda lb, sb: (sb, lb)),
                  pl.BlockSpec((1, 2 * HALF, LANES), lambda lb, sb: (lb, 0, 0)),
                  pl.BlockSpec((1, LANES, 2 * HALF * LANES), lambda lb, sb: (lb, 0, 0)),
                  pl.BlockSpec((1, 2 * HALF * LANES, LANES), lambda lb, sb: (lb, 0, 0)),
                  pl.BlockSpec((1, 1, LANES), lambda lb, sb: (lb, 0, 0))],
        out_specs=[pl.BlockSpec((TILE, LANES), lambda lb, sb: (sb, lb)),
                   pl.BlockSpec((1, TILE, 2 * HALF * LANES), lambda lb, sb: (lb, sb, 0))],
        out_shape=[jax.ShapeDtypeStruct((t_len, SSM_W), F32), jax.ShapeDtypeStruct((N_LB, t_len, 2 * HALF * LANES), BF16)],
        scratch_shapes=[pltpu.VMEM((2 * HALF, TILE, LANES), F32), pltpu.VMEM((TILE, LANES), F32),
                        pltpu.VMEM((2 * HALF, BLK, LANES), F32),
                        pltpu.VMEM((2 * HALF, LANES), F32), pltpu.VMEM((2 * HALF, N_CHUNK, LANES), F32)],
        compiler_params=_params("arbitrary", "arbitrary"),
    )(u, a_cat, b_mat, c_mat, d_skip)


def _s5_bwd(u, dy, states, a_cat, b_mat, c_mat, d_skip):
    t_len = u.shape[0]
    nt = t_len // TILE
    last = nt - 1

    def body(u_ref, dy_ref, x_ref, a_ref, b_ref, c_ref, d_ref, du_ref, db_ref, dc_ref, da_ref, dd_ref,
             gs, us, dys, tabc, lam_s, lin_s):
        sb = pl.program_id(1)

        @pl.when(sb == 0)
        def _():
            _power_table(a_ref, tabc, -1.0, True)
            lam_s[...] = jnp.zeros_like(lam_s)
            db_ref[...] = jnp.zeros_like(db_ref)
            dc_ref[...] = jnp.zeros_like(dc_ref)
            da_ref[...] = jnp.zeros_like(da_ref)
            dd_ref[...] = jnp.zeros_like(dd_ref)

        _interleave(u_ref, us)
        _interleave(dy_ref, dys)
        uv = us[...]
        dyv = dys[...]
        ub = uv.astype(BF16)
        dyb = dyv.astype(BF16)
        gy = _nt(dyb, c_ref[0])
        for j in range(2 * HALF):
            gs[j] = gy[:, j * LANES:(j + 1) * LANES]
        _chunk_scan(gs, a_ref, -1.0, True)
        _chunk_states(gs, lam_s, lin_s, tabc, True)
        zero = jnp.zeros((N_CHUNK, LANES), F32)
        x_tile = x_ref.at[0]
        for grp in range(0, HALF, 2):
            slabs = (grp, grp + 1)
            lin = [(lin_s[j], lin_s[HALF + j]) for j in slabs]

            def fix(i, carry, slabs=slabs, lin=lin):
                s = BLK - 1 - i
                rows = _step_rows(s)
                out = []
                for k, j in enumerate(slabs):
                    nr, ni, acc_r, acc_i = carry[4 * k:4 * k + 4]
                    xr = x_tile[rows, pl.ds(j * LANES, LANES)].astype(F32)
                    xi = x_tile[rows, pl.ds((HALF + j) * LANES, LANES)].astype(F32)
                    qr, qi = _cmul(tabc.at[j][pl.ds(s, 1), :], tabc.at[HALF + j][pl.ds(s, 1), :], lin[k][0], lin[k][1])
                    lr_ = gs.at[j][rows, :] + qr
                    li_ = gs.at[HALF + j][rows, :] + qi
                    gs.at[j][rows, :] = lr_
                    gs.at[HALF + j][rows, :] = li_
                    out += [lr_, li_, acc_r + (xr * nr + xi * ni), acc_i + (xr * ni - xi * nr)]
                return tuple(out)

            init = []
            for k in range(len(slabs)):
                init += [lin[k][0], lin[k][1], zero, zero]
            res = lax.fori_loop(0, BLK, fix, tuple(init), unroll=2)
            for k, j in enumerate(slabs):
                da_ref[0, j:j + 1, :] = da_ref[0, j:j + 1, :] + _colsum(res[4 * k + 2])
                da_ref[0, HALF + j:HALF + j + 1, :] = da_ref[0, HALF + j:HALF + j + 1, :] + _colsum(res[4 * k + 3])
        lam = jnp.concatenate([gs[j].astype(BF16) for j in range(2 * HALF)], axis=1)
        us[...] = _nt(lam, b_ref[0]) + d_ref[0] * dyv
        _deinterleave(us, du_ref)
        db_ref[0] = db_ref[0] + _tn(ub, lam)
        dc_ref[0] = dc_ref[0] + _tn(dyb, x_ref[0])
        dd_ref[0] = dd_ref[0] + _colsum(dyv * uv)

    rev = lambda lb, sb: (last - sb, lb)
    per_lb = lambda lb, sb: (lb, 0, 0)
    wide = 2 * HALF * LANES
    return pl.pallas_call(
        body, name="s5_bwd", grid=(N_LB, nt),
        in_specs=[pl.BlockSpec((TILE, LANES), rev), pl.BlockSpec((TILE, LANES), rev),
                  pl.BlockSpec((1, TILE, wide), lambda lb, sb: (lb, last - sb, 0)),
                  pl.BlockSpec((1, 2 * HALF, LANES), per_lb), pl.BlockSpec((1, LANES, wide), per_lb),
                  pl.BlockSpec((1, wide, LANES), per_lb), pl.BlockSpec((1, 1, LANES), per_lb)],
        out_specs=[pl.BlockSpec((TILE, LANES), rev), pl.BlockSpec((1, LANES, wide), per_lb),
                   pl.BlockSpec((1, LANES, wide), per_lb), pl.BlockSpec((1, 2 * HALF, LANES), per_lb),
                   pl.BlockSpec((1, 1, LANES), per_lb)],
        out_shape=[jax.ShapeDtypeStruct((t_len, SSM_W), F32), jax.ShapeDtypeStruct((N_LB, LANES, wide), F32),
                   jax.ShapeDtypeStruct((N_LB, LANES, wide), F32), jax.ShapeDtypeStruct((N_LB, 2 * HALF, LANES), F32),
                   jax.ShapeDtypeStruct((N_LB, 1, LANES), F32)],
        scratch_shapes=[pltpu.VMEM((2 * HALF, TILE, LANES), F32),
                        pltpu.VMEM((TILE, LANES), F32), pltpu.VMEM((TILE, LANES), F32),
                        pltpu.VMEM((2 * HALF, BLK, LANES), F32), pltpu.VMEM((2 * HALF, LANES), F32),
                        pltpu.VMEM((2 * HALF, N_CHUNK, LANES), F32)],
        compiler_params=_params("arbitrary", "arbitrary"),
    )(u, dy, states, a_cat, b_mat, c_mat, d_skip)


_GELU_C = math.sqrt(2.0 / math.pi)
_GELU_K = 0.044715


def _gelu(y):
    t = jnp.tanh(_GELU_C * (y + _GELU_K * (y * y * y)))
    return y * (0.5 * (1.0 + t)), t


def _gelu_grad(y, t):
    return 0.5 * (1.0 + t) + 0.5 * y * (1.0 - t * t) * (_GELU_C * (1.0 + 3.0 * _GELU_K * y * y))


def _glu(y, wg, bias):
    z, t = _gelu(y)
    sg = jax.nn.sigmoid(_mm(z.astype(BF16), wg) + bias)
    return z, t, sg


def _mix_fwd(attn, y, x, wg, glu_b, ga, gs, wo):
    t_len = x.shape[0]
    tm = 512

    def body(attn_ref, y_ref, x_ref, wg_ref, b_ref, ga_ref, gs_ref, wo_ref, x2_ref, mix_ref, z_ref):
        z, _, sg = _glu(y_ref[...], wg_ref[...], b_ref[...])
        z_ref[...] = z.astype(BF16)
        s = z * sg
        av = attn_ref[...]
        an = (av * _rms(av) * ga_ref[...]).astype(BF16)
        sn = (s * _rms(s) * gs_ref[...]).astype(BF16)
        mix_ref[:, 0:ATTN_W] = an
        mix_ref[:, ATTN_W:] = sn
        x2_ref[...] = x_ref[...] + _mm(an, wo_ref[0:ATTN_W, :]) + _mm(sn, wo_ref[ATTN_W:, :])

    row = lambda i: (i, 0)
    const = lambda i: (0, 0)
    return pl.pallas_call(
        body, name="mix_fwd", grid=(t_len // tm,),
        in_specs=[pl.BlockSpec((tm, ATTN_W), row), pl.BlockSpec((tm, SSM_W), row), pl.BlockSpec((tm, D_MODEL), row),
                  pl.BlockSpec((SSM_W, SSM_W), const), pl.BlockSpec((1, SSM_W), const), pl.BlockSpec((1, ATTN_W), const),
                  pl.BlockSpec((1, SSM_W), const), pl.BlockSpec((D_MODEL, D_MODEL), const)],
        out_specs=[pl.BlockSpec((tm, D_MODEL), row), pl.BlockSpec((tm, D_MODEL), row), pl.BlockSpec((tm, SSM_W), row)],
        out_shape=[jax.ShapeDtypeStruct((t_len, D_MODEL), F32), jax.ShapeDtypeStruct((t_len, D_MODEL), BF16),
                   jax.ShapeDtypeStruct((t_len, SSM_W), BF16)],
        compiler_params=_params("arbitrary"),
    )(attn, y, x, wg, glu_b, ga, gs, wo)


def _mlp(x2, target, g2, wu, wd):
    t_len = x2.shape[0]
    tm = 256
    fc = 1024
    n_fc = D_FF // fc

    def body(x2_ref, tg_ref, g2_ref, wu_hbm, wd_hbm, dx2_ref, hdn_ref, dup_ref, h_ref, dyb_ref, dg2_ref, loss_ref,
             wu_s, wd_s, relu_s, sem):
        @pl.when(pl.program_id(0) == 0)
        def _():
            cu = pltpu.make_async_copy(wu_hbm, wu_s, sem.at[0])
            cd = pltpu.make_async_copy(wd_hbm, wd_s, sem.at[1])
            cu.start()
            cd.start()
            cu.wait()
            cd.wait()
            dg2_ref[...] = jnp.zeros_like(dg2_ref)
            loss_ref[...] = jnp.zeros_like(loss_ref)

        x2v = x2_ref[...]
        r = _rms(x2v)
        g2v = g2_ref[...]
        h = (x2v * r * g2v).astype(BF16)
        h_ref[...] = h
        yout = x2v
        for c in range(n_fc):
            cols = slice(c * fc, (c + 1) * fc)
            ru = jnp.maximum(_mm(h, wu_s[:, cols]), 0.0)
            relu_s[:, cols] = ru
            hd = (ru * ru).astype(BF16)
            hdn_ref[:, cols] = hd
            yout = yout + _mm(hd, wd_s[cols, :])
        err = yout - tg_ref[...]
        loss_ref[...] = loss_ref[...] + 0.5 * jnp.sum(err * err) * (1.0 / D_MODEL)
        dy = err * (1.0 / D_MODEL)
        dyb = dy.astype(BF16)
        dyb_ref[...] = dyb
        dh = jnp.zeros((tm, D_MODEL), F32)
        for c in range(n_fc):
            cols = slice(c * fc, (c + 1) * fc)
            dup = (_nt(dyb, wd_s[cols, :]) * (2.0 * relu_s[:, cols])).astype(BF16)
            dup_ref[:, cols] = dup
            dh = dh + _nt(dup, wu_s[:, cols])
        dxn, gterm = _rms_bwd(dh, x2v, r, g2v)
        dx2_ref[...] = dy + dxn
        dg2_ref[...] = dg2_ref[...] + _colsum(gterm)

    row = lambda i: (i, 0)
    const = lambda i: (0, 0)
    any_spec = pl.BlockSpec(memory_space=pl.ANY)
    return pl.pallas_call(
        body, name="mlp", grid=(t_len // tm,),
        in_specs=[pl.BlockSpec((tm, D_MODEL), row), pl.BlockSpec((tm, D_MODEL), row), pl.BlockSpec((1, D_MODEL), const),
                  any_spec, any_spec],
        out_specs=[pl.BlockSpec((tm, D_MODEL), row), pl.BlockSpec((tm, D_FF), row), pl.BlockSpec((tm, D_FF), row),
                   pl.BlockSpec((tm, D_MODEL), row), pl.BlockSpec((tm, D_MODEL), row), pl.BlockSpec((1, D_MODEL), const),
                   pl.BlockSpec((1, LANES), const)],
        out_shape=[jax.ShapeDtypeStruct((t_len, D_MODEL), F32), jax.ShapeDtypeStruct((t_len, D_FF), BF16),
                   jax.ShapeDtypeStruct((t_len, D_FF), BF16), jax.ShapeDtypeStruct((t_len, D_MODEL), BF16),
                   jax.ShapeDtypeStruct((t_len, D_MODEL), BF16), jax.ShapeDtypeStruct((1, D_MODEL), F32),
                   jax.ShapeDtypeStruct((1, LANES), F32)],
        scratch_shapes=[pltpu.VMEM((D_MODEL, D_FF), BF16), pltpu.VMEM((D_FF, D_MODEL), BF16),
                        pltpu.VMEM((tm, D_FF), F32), pltpu.SemaphoreType.DMA((2,))],
        compiler_params=_params("arbitrary"),
    )(x2, target, g2, wu, wd)


def _mix_bwd(dx2, attn, y, wg, glu_b, ga, gs, wo):
    t_len = dx2.shape[0]
    tm = 512

    def body(dx2_ref, attn_ref, y_ref, wg_ref, b_ref, ga_ref, gs_ref, wo_ref,
             dattn_ref, dy_ref, dx2b_ref, dgp_ref, dga_ref, dgs_ref, db_ref):
        @pl.when(pl.program_id(0) == 0)
        def _():
            dga_ref[...] = jnp.zeros_like(dga_ref)
            dgs_ref[...] = jnp.zeros_like(dgs_ref)
            db_ref[...] = jnp.zeros_like(db_ref)

        dx2b = dx2_ref[...].astype(BF16)
        dx2b_ref[...] = dx2b
        d_an = _nt(dx2b, wo_ref[0:ATTN_W, :])
        d_sn = _nt(dx2b, wo_ref[ATTN_W:, :])
        yv = y_ref[...]
        wg = wg_ref[...]
        z, t, sg = _glu(yv, wg, b_ref[...])
        s = z * sg
        av = attn_ref[...]
        d_attn, ga_term = _rms_bwd(d_an, av, _rms(av), ga_ref[...])
        d_s, gs_term = _rms_bwd(d_sn, s, _rms(s), gs_ref[...])
        dattn_ref[...] = d_attn
        dgp = d_s * z * sg * (1.0 - sg)
        dgpb = dgp.astype(BF16)
        dgp_ref[...] = dgpb
        dz = d_s * sg + _nt(dgpb, wg)
        dy_ref[...] = dz * _gelu_grad(yv, t)
        dga_ref[...] = dga_ref[...] + _colsum(ga_term)
        dgs_ref[...] = dgs_ref[...] + _colsum(gs_term)
        db_ref[...] = db_ref[...] + _colsum(dgp)

    row = lambda i: (i, 0)
    const = lambda i: (0, 0)
    vec = jax.ShapeDtypeStruct((1, SSM_W), F32)
    return pl.pallas_call(
        body, name="mix_bwd", grid=(t_len // tm,),
        in_specs=[pl.BlockSpec((tm, D_MODEL), row), pl.BlockSpec((tm, ATTN_W), row), pl.BlockSpec((tm, SSM_W), row),
                  pl.BlockSpec((SSM_W, SSM_W), const), pl.BlockSpec((1, SSM_W), const), pl.BlockSpec((1, ATTN_W), const),
                  pl.BlockSpec((1, SSM_W), const), pl.BlockSpec((D_MODEL, D_MODEL), const)],
        out_specs=[pl.BlockSpec((tm, ATTN_W), row), pl.BlockSpec((tm, SSM_W), row), pl.BlockSpec((tm, D_MODEL), row),
                   pl.BlockSpec((tm, SSM_W), row), pl.BlockSpec((1, ATTN_W), const), pl.BlockSpec((1, SSM_W), const),
                   pl.BlockSpec((1, SSM_W), const)],
        out_shape=[jax.ShapeDtypeStruct((t_len, ATTN_W), F32), jax.ShapeDtypeStruct((t_len, SSM_W), F32),
                   jax.ShapeDtypeStruct((t_len, D_MODEL), BF16), jax.ShapeDtypeStruct((t_len, SSM_W), BF16), vec, vec, vec],
        compiler_params=_params("arbitrary"),
    )(dx2, attn, y, wg, glu_b, ga, gs, wo)


def _mixer_mlp(attn, y, x, target, wg, glu_b, ga, gs, wo, g2, wu, wd):
    t_len = x.shape[0]
    tm = 256
    fc = 1024
    n_fc = D_FF // fc

    def body(attn_ref, y_ref, x_ref, tg_ref, b_ref, ga_ref, gs_ref, g2_ref, wg_s, wo_s, wu_hbm, wd_hbm,
             dx2_ref, hdn_ref, dup_ref, h_ref, dyb_ref, mix_ref, z_ref, dg2_ref, loss_ref,
             wu_s, wd_s, relu_s, sem):
        @pl.when(pl.program_id(0) == 0)
        def _():
            copies = [pltpu.make_async_copy(src, dst, sem.at[k]) for k, (src, dst) in enumerate(((wu_hbm, wu_s), (wd_hbm, wd_s)))]
            for cp in copies:
                cp.start()
            for cp in copies:
                cp.wait()
            for acc in (dg2_ref, loss_ref):
                acc[...] = jnp.zeros_like(acc)

        av = attn_ref[...]
        z, _, sg = _glu(y_ref[...], wg_s[...], b_ref[...])
        z_ref[...] = z.astype(BF16)
        s = z * sg
        an = (av * _rms(av) * ga_ref[...]).astype(BF16)
        sn = (s * _rms(s) * gs_ref[...]).astype(BF16)
        mix_ref[:, 0:ATTN_W] = an
        mix_ref[:, ATTN_W:] = sn
        dx2_ref[...] = x_ref[...] + _mm(an, wo_s[0:ATTN_W, :]) + _mm(sn, wo_s[ATTN_W:, :])
        r = _rms(dx2_ref[...])
        g2v = g2_ref[...]
        h = (dx2_ref[...] * r * g2v).astype(BF16)
        h_ref[...] = h
        yout = dx2_ref[...]
        for c in range(n_fc):
            cols = slice(c * fc, (c + 1) * fc)
            ru = jnp.maximum(_mm(h, wu_s[:, cols]), 0.0)
            relu_s[:, cols] = ru
            hd = (ru * ru).astype(BF16)
            hdn_ref[:, cols] = hd
            yout = yout + _mm(hd, wd_s[cols, :])
        err = yout - tg_ref[...]
        loss_ref[...] = loss_ref[...] + 0.5 * jnp.sum(err * err) * (1.0 / D_MODEL)
        dy = err * (1.0 / D_MODEL)
        dyb = dy.astype(BF16)
        dyb_ref[...] = dyb
        dh = jnp.zeros((tm, D_MODEL), F32)
        for c in range(n_fc):
            cols = slice(c * fc, (c + 1) * fc)
            dup = (_nt(dyb, wd_s[cols, :]) * (2.0 * relu_s[:, cols])).astype(BF16)
            dup_ref[:, cols] = dup
            dh = dh + _nt(dup, wu_s[:, cols])
        dxn, g2_term = _rms_bwd(dh, dx2_ref[...], r, g2v)
        dx2_ref[...] = dy + dxn
        dg2_ref[...] = dg2_ref[...] + _colsum(g2_term)

    row = lambda i: (i, 0)
    const = lambda i: (0, 0)
    wide = lambda n: pl.BlockSpec((tm, n), row)
    vec = lambda n: pl.BlockSpec((1, n), const)
    any_spec = pl.BlockSpec(memory_space=pl.ANY)
    f32 = lambda n: jax.ShapeDtypeStruct((t_len, n), F32)
    b16 = lambda n: jax.ShapeDtypeStruct((t_len, n), BF16)
    acc = lambda n: jax.ShapeDtypeStruct((1, n), F32)
    return pl.pallas_call(
        body, name="mixer_mlp", grid=(t_len // tm,),
        in_specs=[wide(ATTN_W), wide(SSM_W), wide(D_MODEL), wide(D_MODEL), vec(SSM_W), vec(ATTN_W), vec(SSM_W), vec(D_MODEL),
                  pl.BlockSpec((SSM_W, SSM_W), const), pl.BlockSpec((D_MODEL, D_MODEL), const), any_spec, any_spec],
        out_specs=[wide(D_MODEL), wide(D_FF), wide(D_FF), wide(D_MODEL), wide(D_MODEL), wide(D_MODEL), wide(SSM_W),
                   vec(D_MODEL), vec(LANES)],
        out_shape=[f32(D_MODEL), b16(D_FF), b16(D_FF), b16(D_MODEL), b16(D_MODEL), b16(D_MODEL), b16(SSM_W),
                   acc(D_MODEL), acc(LANES)],
        scratch_shapes=[pltpu.VMEM((D_MODEL, D_FF), BF16), pltpu.VMEM((D_FF, D_MODEL), BF16),
                        pltpu.VMEM((tm, D_FF), F32), pltpu.SemaphoreType.DMA((2,))],
        compiler_params=_params("arbitrary"),
    )(attn, y, x, target, glu_b, ga, gs, g2, wg, wo, wu, wd)


def _inproj_bwd(dqs, dkn, dv, du, q_raw, k_raw, x, dx2, wi, g1, gq, gk, ones64):
    t_len = x.shape[0]
    tm = 512
    n_heads = ATTN_W // HEAD

    def body(dqs_ref, dkn_ref, dv_ref, du_ref, q_ref, k_ref, x_ref, dx2_ref, wi_ref, g1_ref, gq_ref, gk_ref, bd_ref,
             gx_ref, dproj_ref, dg1_ref, dgq_ref, dgk_ref, accq, acck):
        i = pl.program_id(0)

        @pl.when(i == 0)
        def _():
            dg1_ref[...] = jnp.zeros_like(dg1_ref)
            accq[...] = jnp.zeros_like(accq)
            acck[...] = jnp.zeros_like(acck)

        bd = bd_ref[...]

        def head_norm_bwd(dy, raw, gain, acc):
            r = lax.rsqrt(_group_mean(raw * raw, bd, HEAD) + EPS)
            xh = raw * r
            dxh = dy * gain
            acc[...] = acc[...] + _colsum(dy * xh)
            return r * (dxh - xh * _group_mean(dxh * xh, bd, HEAD))

        dq = head_norm_bwd(dqs_ref[...] * (HEAD ** -0.5), q_ref[...], gq_ref[...], accq)
        dk = head_norm_bwd(dkn_ref[...], k_ref[...], gk_ref[...], acck)
        dproj_ref[:, 0:ATTN_W] = dq.astype(BF16)
        dproj_ref[:, ATTN_W:2 * ATTN_W] = dk.astype(BF16)
        dproj_ref[:, 2 * ATTN_W:3 * ATTN_W] = dv_ref[...].astype(BF16)
        dproj_ref[:, 3 * ATTN_W:] = du_ref[...].astype(BF16)
        dxn = _nt(dproj_ref[...], wi_ref[...])
        xv = x_ref[...]
        g1v = g1_ref[...]
        dx, g1_term = _rms_bwd(dxn, xv, _rms(xv), g1v)
        gx_ref[...] = dx2_ref[...] + dx
        dg1_ref[...] = dg1_ref[...] + _colsum(g1_term)

        @pl.when(i == pl.num_programs(0) - 1)
        def _():
            for acc, out in ((accq, dgq_ref), (acck, dgk_ref)):
                tot = acc[:, 0:HEAD]
                for h in range(1, n_heads):
                    tot = tot + acc[:, h * HEAD:(h + 1) * HEAD]
                out[...] = tot

    row = lambda i: (i, 0)
    const = lambda i: (0, 0)
    aw = pl.BlockSpec((tm, ATTN_W), row)
    dm = pl.BlockSpec((tm, D_MODEL), row)
    return pl.pallas_call(
        body, name="inproj_bwd", grid=(t_len // tm,),
        in_specs=[aw, aw, aw, aw, aw, aw, dm, dm, pl.BlockSpec((D_MODEL, PROJ_W), const), pl.BlockSpec((1, D_MODEL), const),
                  pl.BlockSpec((1, ATTN_W), const), pl.BlockSpec((1, ATTN_W), const), pl.BlockSpec((ATTN_W, ATTN_W), const)],
        out_specs=[dm, pl.BlockSpec((tm, PROJ_W), row), pl.BlockSpec((1, D_MODEL), const),
                   pl.BlockSpec((1, HEAD), const), pl.BlockSpec((1, HEAD), const)],
        out_shape=[jax.ShapeDtypeStruct((t_len, D_MODEL), F32), jax.ShapeDtypeStruct((t_len, PROJ_W), BF16),
                   jax.ShapeDtypeStruct((1, D_MODEL), F32), jax.ShapeDtypeStruct((1, HEAD), F32),
                   jax.ShapeDtypeStruct((1, HEAD), F32)],
        scratch_shapes=[pltpu.VMEM((1, ATTN_W), F32), pltpu.VMEM((1, ATTN_W), F32)],
        compiler_params=_params("arbitrary"),
    )(dqs, dkn, dv, du, q_raw, k_raw, x, dx2, wi, g1, gq, gk, ones64)


def _grad_matmul(a, b, name):
    t_len, m = a.shape
    n = b.shape[1]
    bm, bn, bt = min(m, 1024), min(n, 1024), 1024

    def body(a_ref, b_ref, o_ref):
        @pl.when(pl.program_id(2) == 0)
        def _():
            o_ref[...] = jnp.zeros_like(o_ref)

        o_ref[...] = o_ref[...] + _tn(a_ref[...], b_ref[...])

    return pl.pallas_call(
        body, name=name, grid=(m // bm, n // bn, t_len // bt),
        in_specs=[pl.BlockSpec((bt, bm), lambda i, j, k: (k, i)), pl.BlockSpec((bt, bn), lambda i, j, k: (k, j))],
        out_specs=pl.BlockSpec((bm, bn), lambda i, j, k: (i, j)),
        out_shape=jax.ShapeDtypeStruct((m, n), F32),
        compiler_params=_params("arbitrary", "arbitrary", "arbitrary"),
    )(a, b)


def _adamw_update(w_ref, g_ref, m_ref, v_ref, d_o, m_o, v_o):
    gv = g_ref[...]
    mn = ADAM_B1 * m_ref[...] + (1.0 - ADAM_B1) * gv
    vn = ADAM_B2 * v_ref[...] + (1.0 - ADAM_B2) * jnp.square(gv)
    m_hat = mn / (1.0 - ADAM_B1 ** ADAM_STEP)
    v_hat = vn / (1.0 - ADAM_B2 ** ADAM_STEP)
    d_o[...] = -ADAM_LR * (m_hat / (jnp.sqrt(v_hat) + ADAM_EPS) + ADAM_WD * w_ref[...])
    m_o[...] = mn
    v_o[...] = vn


def _adamw_many(ws, gs, ms, vs):
    n = len(ws)

    def body(*refs):
        for i in range(n):
            _adamw_update(*[refs[k * n + i] for k in range(7)])

    shapes = [jax.ShapeDtypeStruct(w.shape, F32) for w in ws]
    outs = pl.pallas_call(body, name="adamw_small", out_shape=shapes * 3,
                          compiler_params=pltpu.CompilerParams(vmem_limit_bytes=VMEM_LIMIT))(*ws, *gs, *ms, *vs)
    return outs[0:n], outs[n:2 * n], outs[2 * n:]


def _adamw(w, g, m, v, name):
    rows, cols = w.shape
    br = _row_block(rows, 256)
    body = functools.partial(_adamw_update)

    spec = pl.BlockSpec((br, cols), lambda i: (i, 0))
    shape = jax.ShapeDtypeStruct((rows, cols), F32)
    return pl.pallas_call(
        body, name=name, grid=(rows // br,), in_specs=[spec] * 4, out_specs=[spec] * 3, out_shape=[shape] * 3,
        compiler_params=_params("arbitrary"),
    )(w, g, m, v)


def _sum_arrays(arrs, name, out_dtype=F32):
    rows, cols = arrs[0].shape
    br = _row_block(rows, 512)
    n = len(arrs)

    def body(*refs):
        tot = refs[0][...]
        for r in refs[1:n]:
            tot = tot + r[...]
        refs[n][...] = tot.astype(out_dtype)

    spec = pl.BlockSpec((br, cols), lambda i: (i, 0))
    return pl.pallas_call(
        body, name=name, grid=(rows // br,), in_specs=[spec] * n, out_specs=spec,
        out_shape=jax.ShapeDtypeStruct((rows, cols), out_dtype), compiler_params=_params("arbitrary"),
    )(*arrs)


GPL = N_GROUPS // N_LB
SW = GPL * N_STATE


def _eye_groups():
    return jnp.eye(GPL, dtype=F32)


def _s5_matrices(ab_r, ab_i, bb_r, bb_i, c_re, c_im, d_skip):
    eye = _eye_groups()
    a_cat = jnp.concatenate([ab_r.reshape(N_LB, HALF, LANES), ab_i.reshape(N_LB, HALF, LANES)], axis=1)

    def b_part(bb):
        b4 = jnp.transpose(bb.reshape(N_LB, GPL, N_STATE, GROUP_W), (0, 1, 3, 2))
        return (b4[:, :, :, None, :] * eye[None, :, None, :, None]).reshape(N_LB, LANES, SW)

    def c_part(cc):
        c4 = jnp.transpose(cc.reshape(N_LB, GPL, GROUP_W, N_STATE), (0, 1, 3, 2))
        return (c4[:, :, :, None, :] * eye[None, :, None, :, None]).reshape(N_LB, SW, LANES)

    b_mat = jnp.concatenate([b_part(bb_r), b_part(bb_i)], axis=2).astype(BF16)
    c_mat = jnp.concatenate([c_part(c_re), -c_part(c_im)], axis=1).astype(BF16)
    return a_cat, b_mat, c_mat, d_skip.reshape(N_LB, 1, LANES)


def _s5_unpack_grads(db, dc, da, dd):
    eye = _eye_groups()
    mask = eye[None, :, None, None, :, None]
    d6 = jnp.sum(db.reshape(N_LB, GPL, GROUP_W, 2, GPL, N_STATE) * mask, axis=4)
    dbb = jnp.transpose(d6, (3, 0, 1, 4, 2)).reshape(2, N_GROUPS * N_STATE, GROUP_W)
    c6 = jnp.sum(dc.reshape(N_LB, GPL, GROUP_W, 2, GPL, N_STATE) * mask, axis=4)
    dcc = jnp.transpose(c6, (3, 0, 1, 2, 4)).reshape(2, N_GROUPS, GROUP_W, N_STATE)
    dab_r = da[:, :HALF].reshape(N_GROUPS * N_STATE, 1)
    dab_i = da[:, HALF:].reshape(N_GROUPS * N_STATE, 1)
    return dab_r, dab_i, dbb[0], dbb[1], dcc[0], -dcc[1], dd.reshape(N_GROUPS, GROUP_W)


def _block_ones(n, width):
    i = lax.broadcasted_iota(jnp.int32, (n, n), 0) // width
    j = lax.broadcasted_iota(jnp.int32, (n, n), 1) // width
    return (i == j).astype(BF16)


def _tile_heads(g):
    return jnp.tile(g.reshape(1, HEAD), (1, ATTN_W // HEAD))


def _local_step(x, target, wi, rest, p, fwd_side=None, bwd_side=None):
    ones64 = _block_ones(ATTN_W, HEAD)
    ones_hp = _block_ones(LANES, HEAD)
    g1 = p["norm1_g"].reshape(1, D_MODEL)
    g2 = p["norm2_g"].reshape(1, D_MODEL)
    gq = _tile_heads(p["q_norm_g"])
    gk = _tile_heads(p["k_norm_g"])
    ga = p["attn_out_norm_g"].reshape(1, ATTN_W)
    gs = p["ssm_out_norm_g"].reshape(1, SSM_W)
    glu_b = p["glu_b"].reshape(1, SSM_W)
    n_gp = N_GROUPS * N_STATE
    lr = p["ssm_a_re"].reshape(n_gp, 1)
    li = p["ssm_a_im"].reshape(n_gp, 1)
    ldt = jnp.repeat(p["ssm_log_dt"].reshape(N_GROUPS), N_STATE).reshape(n_gp, 1)
    br = p["ssm_b_re"].reshape(n_gp, GROUP_W)
    bi = p["ssm_b_im"].reshape(n_gp, GROUP_W)
    ab_r, ab_i, bb_r, bb_i = _disc_fwd(lr, li, ldt, br, bi)
    a_cat, b_mat, c_mat, d_mat = _s5_matrices(
        ab_r, ab_i, bb_r, bb_i, p["ssm_c_re"].reshape(N_GROUPS, GROUP_W, N_STATE),
        p["ssm_c_im"].reshape(N_GROUPS, GROUP_W, N_STATE), p["ssm_d"])

    xn, qn, kn, vv, u, q_raw, k_raw = _inproj_fwd(x, g1, wi, gq, gk, ones64)
    if fwd_side is None:
        attn, lse = _attn_fwd(qn, kn, vv)
    else:
        attn, lse, *rest = _attn_fwd(qn, kn, vv, *fwd_side)
    wg, wo, wu, wd = rest
    y, states = _s5_fwd(u, a_cat, b_mat, c_mat, d_mat)
    dx2, hdn, dup, h, dyb, mix, z, dg2, loss = _mixer_mlp(attn, y, x, target, wg, glu_b, ga, gs, wo, g2, wu, wd)
    d_attn, dy_ssm, dx2b, dgp, dga, dgs, dglu_b = _mix_bwd(dx2, attn, y, wg, glu_b, ga, gs, wo)
    big = {"w_mlp_up": _grad_matmul(h, dup, "grad_w_mlp_up"), "w_mlp_down": _grad_matmul(hdn, dyb, "grad_w_mlp_down"),
           "w_out": _grad_matmul(mix, dx2b, "grad_w_out"), "glu_w": _grad_matmul(z, dgp, "grad_glu_w")}
    rode = []
    if bwd_side is None:
        dqs, dkn, dvv = _attn_bwd(qn, kn, vv, attn, d_attn, lse, ones_hp)
    else:
        dqs, dkn, dvv, *rode = _attn_bwd(qn, kn, vv, attn, d_attn, lse, ones_hp, *bwd_side(big))
    du, db, dc, da, dd = _s5_bwd(u, dy_ssm, states, a_cat, b_mat, c_mat, d_mat)
    grad_x, dproj, dg1, dgq, dgk = _inproj_bwd(dqs, dkn, dvv, du, q_raw, k_raw, x, dx2, wi, g1, gq, gk, ones64)

    big["w_in"] = _grad_matmul(xn, dproj, "grad_w_in")
    dab_r, dab_i, dbb_r, dbb_i, dc_re, dc_im, dd_g = _s5_unpack_grads(db, dc, da, dd)
    cot = {"norm1_g": dg1, "q_norm_g": dgq, "k_norm_g": dgk, "ab_r": dab_r, "ab_i": dab_i, "bb_r": dbb_r, "bb_i": dbb_i,
           "ssm_c_re": dc_re, "ssm_c_im": dc_im, "ssm_d": dd_g, "glu_b": dglu_b, "attn_out_norm_g": dga,
           "ssm_out_norm_g": dgs, "norm2_g": dg2}
    return loss[0, 0], grad_x, big, cot, (lr, li, ldt, br, bi), rode


COT_NAMES = ("norm1_g", "q_norm_g", "k_norm_g", "ab_r", "ab_i", "bb_r", "bb_i", "ssm_c_re", "ssm_c_im", "ssm_d",
             "glu_b", "attn_out_norm_g", "ssm_out_norm_g", "norm2_g")
SMALL_NAMES = ("norm1_g", "q_norm_g", "k_norm_g", "ssm_a_re", "ssm_a_im", "ssm_log_dt", "ssm_b_re", "ssm_b_im",
               "ssm_c_re", "ssm_c_im", "ssm_d", "glu_b", "attn_out_norm_g", "ssm_out_norm_g", "norm2_g")
BIG_NAMES = ("w_in", "glu_w", "w_out", "w_mlp_up", "w_mlp_down")
PACK_ROWS = 1152


def _pack(arrs):
    flat = jnp.concatenate([a.reshape(-1) for a in arrs])
    return jnp.pad(flat, (0, PACK_ROWS * LANES - flat.shape[0])).reshape(PACK_ROWS, LANES)


def _unpack(packed, like):
    flat = packed.reshape(-1)
    out, pos = [], 0
    for a in like:
        out.append(flat[pos:pos + a.size].reshape(a.shape))
        pos += a.size
    return out


def _small_grads(cot, disc_in, p):
    lr, li, ldt, br, bi = disc_in
    group_sum = (lax.broadcasted_iota(jnp.int32, (N_GROUPS, N_GROUPS * N_STATE), 1) // N_STATE
                 == lax.broadcasted_iota(jnp.int32, (N_GROUPS, N_GROUPS * N_STATE), 0)).astype(F32)
    dlr, dli, dldt, dbr, dbi = _disc_bwd(lr, li, ldt, br, bi, cot["ab_r"], cot["ab_i"], cot["bb_r"], cot["bb_i"], group_sum)
    g = dict(cot)
    g.update(ssm_a_re=dlr, ssm_a_im=dli, ssm_log_dt=dldt[:, 0], ssm_b_re=dbr, ssm_b_im=dbi)
    return {n: g[n].reshape(p[n].shape) for n in SMALL_NAMES}


BIG = {
    "w_in": ((D_MODEL, PROJ_W), 1, PROJ_W // 4, 0, D_MODEL // 2),
    "glu_w": ((SSM_W, SSM_W), 0, SSM_W // 4, 1, SSM_W // 2),
    "w_out": ((D_MODEL, D_MODEL), 0, D_MODEL // 4, 1, D_MODEL // 2),
    "w_mlp_up": ((D_MODEL, D_FF), 1, D_FF // 4, 0, D_MODEL // 2),
    "w_mlp_down": ((D_FF, D_MODEL), 0, D_FF // 4, 1, D_MODEL // 2),
}
N_BIG = len(BIG_NAMES)
N_CHIPS = 4
ANY = pl.BlockSpec(memory_space=pl.ANY)


def _cut(name, shard=False, half=False):
    shape, s_ax, s_sz, h_ax, h_sz = BIG[name]
    shape = list(shape)
    if shard:
        shape[s_ax] = s_sz
    if half:
        shape[h_ax] = h_sz
    return tuple(shape)


def _window(name, base, shard=None, half=None):
    _, s_ax, s_sz, h_ax, h_sz = BIG[name]
    idx = [pl.ds(0, base[0]), pl.ds(0, base[1])]
    if shard is not None:
        idx[s_ax] = pl.ds(pl.multiple_of(shard * s_sz, s_sz), s_sz)
    if half is not None:
        idx[h_ax] = pl.ds(pl.multiple_of(half * h_sz, h_sz), h_sz)
    return tuple(idx)


def _mesh_pos():
    return lax.axis_index("x"), lax.axis_index("y"), lax.axis_index("c")


def _other_chips(x, y):
    return [(1 - x, y, 2 * (1 - x) + y), (x, 1 - y, 2 * x + 1 - y), (1 - x, 1 - y, 2 * (1 - x) + 1 - y)]


def _remote(src, dst, send_sem, recv_sem, dev):
    return pltpu.make_async_remote_copy(src_ref=src, dst_ref=dst, send_sem=send_sem, recv_sem=recv_sem,
                                        device_id=dev, device_id_type=MESH)


def _start_remote(src, dst, send_sem, recv_sem, dev):
    cp = _remote(src, dst, send_sem, recv_sem, dev)
    cp.start()
    return cp


class _Gather:
    def __init__(self, names):
        self.names = tuple(names)
        self.n = len(self.names)

    def in_specs(self):
        return [pl.BlockSpec(memory_space=pltpu.VMEM)] * self.n

    def out_specs(self):
        return [ANY] * self.n

    def out_shapes(self):
        return [jax.ShapeDtypeStruct(BIG[w][0], BF16) for w in self.names]

    def scratch_shapes(self):
        n_sem = (N_CHIPS - 1) * self.n
        return ([pltpu.VMEM(_cut(w, shard=True), BF16) for w in self.names]
                + [pltpu.SemaphoreType.DMA((n_sem,))] * 4 + [pltpu.SemaphoreType.DMA((self.n,))])

    def bind(self, ins, outs, scratch):
        self.ins, self.outs = ins, outs
        self.stage = scratch[:self.n]
        self.send, self.recv, self.fsend, self.frecv, self.lsem = scratch[self.n:]

    def _copies(self):
        x, y, c = _mesh_pos()
        me = 2 * x + y
        sib = (x, y, 1 - c)
        local, sends, lands, fwds, flands = [], [], [], [], []
        for w, n in enumerate(self.names):
            local.append(pltpu.make_async_copy(self.stage[w], self.outs[w].at[_window(n, BIG[n][0], shard=me)], self.lsem.at[w]))
        for k, (px, py, pj) in enumerate(_other_chips(x, y)):
            for w, n in enumerate(self.names):
                s = k * self.n + w
                sends.append(_remote(self.stage[w].at[_window(n, _cut(n, shard=True), half=c)],
                                     self.outs[w].at[_window(n, BIG[n][0], shard=me, half=c)],
                                     self.send.at[s], self.recv.at[s], (px, py, c)))
                got = self.outs[w].at[_window(n, BIG[n][0], shard=pj, half=c)]
                lands.append(_remote(got, got, self.send.at[s], self.recv.at[s], (px, py, c)))
                fwds.append(_remote(got, got, self.fsend.at[s], self.frecv.at[s], sib))
                theirs = self.outs[w].at[_window(n, BIG[n][0], shard=pj, half=1 - c)]
                flands.append(_remote(theirs, theirs, self.fsend.at[s], self.frecv.at[s], sib))
        return local, sends, lands, fwds, flands

    def start(self):
        for w in range(self.n):
            self.stage[w][...] = self.ins[w][...].astype(BF16)
        local, sends, _, _, _ = self._copies()
        for cp in local + sends:
            cp.start()

    def forward(self):
        _, _, lands, fwds, _ = self._copies()
        for land, fwd in zip(lands, fwds):
            land.wait_recv()
            fwd.start()

    def finish(self):
        local, sends, _, fwds, flands = self._copies()
        for cp in flands:
            cp.wait_recv()
        for cp in sends + fwds:
            cp.wait_send()
        for cp in local:
            cp.wait()


def _gather_weights(shards, names):
    g = _Gather(names)

    def body(*refs):
        g.bind(refs[0:g.n], refs[g.n:2 * g.n], refs[2 * g.n:])
        g.start()
        g.forward()
        g.finish()

    return pl.pallas_call(
        body, name="gather_" + "_".join(names), in_specs=g.in_specs(), out_specs=g.out_specs(), out_shape=g.out_shapes(),
        scratch_shapes=g.scratch_shapes(), compiler_params=pltpu.CompilerParams(vmem_limit_bytes=VMEM_LIMIT),
    )(*[shards[n] for n in names])


def _pair_exchange(grads, names, packed=None):
    n_big = len(names)
    n_all = n_big + (packed is not None)

    def body(*refs):
        ins, got = refs[0:n_all], refs[n_all:2 * n_all]
        send, recv = refs[2 * n_all:]
        x, y, c = _mesh_pos()
        sib = (x, y, 1 - c)
        copies = []
        for w, n in enumerate(names):
            copies.append(_start_remote(ins[w].at[_window(n, BIG[n][0], half=1 - c)], got[w], send.at[w], recv.at[w], sib))
        if packed is not None:
            copies.append(_start_remote(ins[n_big], got[n_big], send.at[n_big], recv.at[n_big], sib))
        for cp in copies:
            cp.wait()

    shapes = [jax.ShapeDtypeStruct(_cut(n, half=True), F32) for n in names]
    args = [grads[n] for n in names]
    if packed is not None:
        shapes.append(jax.ShapeDtypeStruct(packed.shape, F32))
        args.append(packed)
    return pl.pallas_call(
        body, name="grad_pair_exchange_" + "_".join(names), in_specs=[ANY] * n_all, out_specs=[ANY] * n_all, out_shape=shapes,
        scratch_shapes=[pltpu.SemaphoreType.DMA((n_all,)), pltpu.SemaphoreType.DMA((n_all,))],
    )(*args)


def _pair_sum(name, full, got, core):
    _, _, _, h_ax, _ = BIG[name]
    rows, cols = _cut(name, half=True)
    br = _row_block(rows, 512)
    nb = rows // br
    own_map = (lambda i, c: (i + c[0] * nb, 0)) if h_ax == 0 else (lambda i, c: (i, c[0]))

    def body(c_ref, own_ref, got_ref, o_ref):
        o_ref[...] = (own_ref[...] + got_ref[...]).astype(BF16)

    plain = pl.BlockSpec((br, cols), lambda i, c: (i, 0))
    return pl.pallas_call(
        body, name="pair_sum_" + name,
        grid_spec=pltpu.PrefetchScalarGridSpec(num_scalar_prefetch=1, grid=(nb,),
                                               in_specs=[pl.BlockSpec((br, cols), own_map), plain], out_specs=plain),
        out_shape=jax.ShapeDtypeStruct((rows, cols), BF16), compiler_params=_params("arbitrary"),
    )(core, full, got)


class _ChipExchange:
    def __init__(self, names, packed_shape=None):
        self.names = tuple(names)
        self.packed_shape = packed_shape
        self.n = len(self.names) + (packed_shape is not None)

    def in_specs(self):
        return [ANY] * self.n

    def out_specs(self):
        return [ANY] * self.n

    def out_shapes(self):
        shapes = [jax.ShapeDtypeStruct((N_CHIPS,) + _cut(w, shard=True, half=True), BF16) for w in self.names]
        if self.packed_shape is not None:
            shapes.append(jax.ShapeDtypeStruct((N_CHIPS,) + tuple(self.packed_shape), F32))
        return shapes

    def scratch_shapes(self):
        n_sem = (N_CHIPS - 1) * self.n
        return [pltpu.SemaphoreType.DMA((n_sem,)), pltpu.SemaphoreType.DMA((n_sem,))]

    def bind(self, ins, outs, scratch):
        self.ins, self.outs = ins, outs
        self.send, self.recv = scratch

    def _piece(self, w, shard):
        if w >= len(self.names):
            return self.ins[w]
        n = self.names[w]
        return self.ins[w].at[_window(n, _cut(n, half=True), shard=shard)]

    def _copies(self):
        x, y, c = _mesh_pos()
        me = 2 * x + y
        sends, lands = [], []
        for k, (px, py, pj) in enumerate(_other_chips(x, y)):
            for w in range(self.n):
                s = k * self.n + w
                sends.append(_remote(self._piece(w, pj), self.outs[w].at[me], self.send.at[s], self.recv.at[s], (px, py, c)))
                lands.append(_remote(self._piece(w, me), self.outs[w].at[pj], self.send.at[s], self.recv.at[s], (px, py, c)))
        return sends, lands

    def start(self):
        for cp in self._copies()[0]:
            cp.start()

    def finish(self):
        sends, lands = self._copies()
        for cp in lands:
            cp.wait_recv()
        for cp in sends:
            cp.wait_send()


def _chip_exchange(halves, packed, names):
    ex = _ChipExchange(names, packed.shape)

    def body(*refs):
        ex.bind(refs[0:ex.n], refs[ex.n:2 * ex.n], refs[2 * ex.n:])
        ex.start()
        ex.finish()

    return pl.pallas_call(
        body, name="grad_chip_exchange", in_specs=ex.in_specs(), out_specs=ex.out_specs(), out_shape=ex.out_shapes(),
        scratch_shapes=ex.scratch_shapes(),
    )(*halves, packed)


def _chip_sum(name, own, slots, chip):
    n_slot, rows, cols = slots.shape
    br = _row_block(rows, 512)
    nb = rows // br
    if name in BIG and BIG[name][1] == 1:
        own_map = lambda i, m: (i, m[0])
    elif name in BIG:
        own_map = lambda i, m: (i + m[0] * nb, 0)
    else:
        own_map = lambda i, m: (i, 0)

    def slot_map(j):
        return lambda i, m: (jnp.where(m[0] == j, (j + 1) % n_slot, j), i, 0)

    def body(m_ref, own_ref, *refs):
        own_blk = own_ref[...].astype(F32)
        tot = None
        for j in range(n_slot):
            term = jnp.where(m_ref[0] == j, own_blk, refs[j][...].astype(F32))
            tot = term if tot is None else tot + term
        refs[n_slot][...] = tot

    in_specs = [pl.BlockSpec((br, cols), own_map)] + [pl.BlockSpec((None, br, cols), slot_map(j)) for j in range(n_slot)]
    return pl.pallas_call(
        body, name="chip_sum_" + name,
        grid_spec=pltpu.PrefetchScalarGridSpec(num_scalar_prefetch=1, grid=(nb,), in_specs=in_specs,
                                               out_specs=pl.BlockSpec((br, cols), lambda i, m: (i, 0))),
        out_shape=jax.ShapeDtypeStruct((rows, cols), F32), compiler_params=_params("arbitrary"),
    )(chip, own, *([slots] * n_slot))


def _half_exchange(pieces):
    def body(*refs):
        ins, outs = refs[0:N_BIG], refs[N_BIG:2 * N_BIG]
        send, recv = refs[2 * N_BIG:]
        x, y, c = _mesh_pos()
        sib = (x, y, 1 - c)
        copies = []
        for w, n in enumerate(BIG_NAMES):
            copies.append(_start_remote(ins[w], outs[w], send.at[w], recv.at[w], sib))
        for cp in copies:
            cp.wait()

    return pl.pallas_call(
        body, name="grad_half_exchange", in_specs=[ANY] * N_BIG, out_specs=[ANY] * N_BIG,
        out_shape=[jax.ShapeDtypeStruct(_cut(n, shard=True, half=True), F32) for n in BIG_NAMES],
        scratch_shapes=[pltpu.SemaphoreType.DMA((N_BIG,)), pltpu.SemaphoreType.DMA((N_BIG,))],
    )(*pieces)


WEIGHT_NAMES = ("norm1_g", "w_in", "q_norm_g", "k_norm_g", "ssm_a_re", "ssm_a_im", "ssm_log_dt", "ssm_b_re", "ssm_b_im",
                "ssm_c_re", "ssm_c_im", "ssm_d", "glu_w", "glu_b", "attn_out_norm_g", "ssm_out_norm_g", "w_out", "norm2_g",
                "w_mlp_up", "w_mlp_down")


def _train_step(a):
    x = a["x"][0]
    target = a["loss_target"][0]
    shards = {n: a[n][0] for n in BIG_NAMES}
    p = {n: a[n][0] for n in SMALL_NAMES}
    core = lax.axis_index("c").astype(jnp.int32).reshape(1)
    chip_id = (2 * lax.axis_index("x") + lax.axis_index("y")).astype(jnp.int32).reshape(1)

    later = ("glu_w", "w_out", "w_mlp_up", "w_mlp_down")
    early = ("w_mlp_up", "w_mlp_down", "w_out", "glu_w")
    late = ("w_in",)
    (wi,) = _gather_weights(shards, ("w_in",))
    chip = {}

    def bwd_side(grads):
        got = _pair_exchange(grads, early)
        for n, g in zip(early, got):
            chip[n] = _pair_sum(n, grads[n], g, core)
        return _ChipExchange(early), [chip[n] for n in early]

    loss, grad_x, big, cot, disc_in, early_slots = _local_step(
        x, target, wi, None, p, fwd_side=(_Gather(later), [shards[n] for n in later]), bwd_side=bwd_side)
    slots = dict(zip(early, early_slots))

    cot_list = [cot[n] for n in COT_NAMES]
    packed = _pack(cot_list)
    *got, got_packed = _pair_exchange(big, late, packed)
    for n, g in zip(late, got):
        chip[n] = _pair_sum(n, big[n], g, core)
    chip_packed = _sum_arrays([packed, got_packed], "pair_sum_small")
    *late_slots, small_slots = _chip_exchange([chip[n] for n in late], chip_packed, late)
    slots.update(zip(late, late_slots))
    pieces = [_chip_sum(n, chip[n], slots[n], chip_id) for n in BIG_NAMES]
    small_sum = _chip_sum("small", chip_packed, small_slots, chip_id)
    shard_grads = {}
    for n, mine, theirs in zip(BIG_NAMES, pieces, _half_exchange(pieces)):
        h_ax = BIG[n][3]
        shard_grads[n] = jnp.where(core[0] == 0, jnp.concatenate([mine, theirs], axis=h_ax),
                                   jnp.concatenate([theirs, mine], axis=h_ax))
    small_grads = _small_grads(dict(zip(COT_NAMES, _unpack(small_sum, cot_list))), disc_in, p)

    grads, delta, new_m, new_v = {}, {}, {}, {}
    for n in BIG_NAMES:
        grads[n] = shard_grads[n]
        delta[n], new_m[n], new_v[n] = _adamw(a[n][0], grads[n], a["m_" + n][0], a["v_" + n][0], "adamw_" + n)
    flat2 = lambda t: t.reshape(-1, t.shape[-1])
    res = _adamw_many([flat2(p[n]) for n in SMALL_NAMES], [flat2(small_grads[n]) for n in SMALL_NAMES],
                      [flat2(a["m_" + n][0]) for n in SMALL_NAMES], [flat2(a["v_" + n][0]) for n in SMALL_NAMES])
    for store, outs in zip((delta, new_m, new_v), res):
        store.update(zip(SMALL_NAMES, outs))
    grads.update(small_grads)

    total = lax.psum(loss, ("x", "y", "c"))
    out = [total, grad_x[None]]
    for store in (grads, delta, new_m, new_v):
        out += [store[n].reshape(a[n].shape) for n in WEIGHT_NAMES]
    return tuple(out)


def kernel(x, norm1_g, w_in, q_norm_g, k_norm_g, ssm_a_re, ssm_a_im, ssm_log_dt, ssm_b_re, ssm_b_im, ssm_c_re, ssm_c_im, ssm_d, glu_w, glu_b, attn_out_norm_g, ssm_out_norm_g, w_out, norm2_g, w_mlp_up, w_mlp_down, loss_target, m_norm1_g, m_w_in, m_q_norm_g, m_k_norm_g, m_ssm_a_re, m_ssm_a_im, m_ssm_log_dt, m_ssm_b_re, m_ssm_b_im, m_ssm_c_re, m_ssm_c_im, m_ssm_d, m_glu_w, m_glu_b, m_attn_out_norm_g, m_ssm_out_norm_g, m_w_out, m_norm2_g, m_w_mlp_up, m_w_mlp_down, v_norm1_g, v_w_in, v_q_norm_g, v_k_norm_g, v_ssm_a_re, v_ssm_a_im, v_ssm_log_dt, v_ssm_b_re, v_ssm_b_im, v_ssm_c_re, v_ssm_c_im, v_ssm_d, v_glu_w, v_glu_b, v_attn_out_norm_g, v_ssm_out_norm_g, v_w_out, v_norm2_g, v_w_mlp_up, v_w_mlp_down):
    return _train_step(dict(locals()))
```

```python
import functools
import math

import jax
import jax.numpy as jnp
from jax import lax
from jax.experimental import pallas as pl
from jax.experimental.pallas import tpu as pltpu

F32 = jnp.float32
BF16 = jnp.bfloat16
MESH = pl.DeviceIdType.MESH

D_MODEL = 1024
ATTN_W = 512
SSM_W = 512
HEAD = 64
D_FF = 4096
PROJ_W = 2048
N_GROUPS = 32
N_STATE = 64
GROUP_W = 16
EPS = 1e-6
NEG = -1e30
DILATIONS = (1, 4, 16)
BLK = 128
TILE = 2048
LANES = 128
N_LB = SSM_W // LANES
N_SLAB = 2 * N_LB * N_STATE * 8 // LANES // N_LB
VMEM_LIMIT = 56 * 1024 * 1024

ADAM_LR, ADAM_B1, ADAM_B2, ADAM_EPS, ADAM_WD, ADAM_STEP = 0.001, 0.9, 0.999, 1e-08, 0.01, 10


def _params(*sem):
    return pltpu.CompilerParams(dimension_semantics=sem, vmem_limit_bytes=VMEM_LIMIT)


def _nt(a, b):
    return lax.dot_general(a, b, (((1,), (1,)), ((), ())), preferred_element_type=F32)


def _tn(a, b):
    return lax.dot_general(a, b, (((0,), (0,)), ((), ())), preferred_element_type=F32)


def _mm(a, b):
    return jnp.dot(a, b, preferred_element_type=F32)


def _group_mean(t, ones_bd, width):
    hi = t.astype(BF16)
    lo = (t - hi.astype(F32)).astype(BF16)
    return (_mm(hi, ones_bd) + _mm(lo, ones_bd)) * (1.0 / width)


def _rms(x):
    return lax.rsqrt(jnp.mean(x * x, axis=-1, keepdims=True) + EPS)


def _rms_bwd(dy, x, r, g):
    xh = x * r
    dxh = dy * g
    dx = r * (dxh - xh * jnp.mean(dxh * xh, axis=-1, keepdims=True))
    return dx, dy * xh


def _colsum(x):
    return jnp.sum(x, axis=0, keepdims=True)


def _row_block(rows, cap):
    for b in range(min(rows, cap) // 8 * 8, 0, -8):
        if rows % b == 0:
            return b
    raise ValueError(f"no row block for {rows} rows")


def _inproj_fwd(x, g1, wi, gq, gk, ones64):
    t_len = x.shape[0]
    tm = 512
    n_hp = ATTN_W // LANES

    def body(x_ref, g1_ref, wi_ref, gq_ref, gk_ref, bd_ref, xn_ref, q_ref, k_ref, v_ref, u_ref, qr_ref, kr_ref):
        xv = x_ref[...]
        xn = (xv * _rms(xv) * g1_ref[...]).astype(BF16)
        xn_ref[...] = xn
        proj = _mm(xn, wi_ref[...])
        q = proj[:, 0:ATTN_W]
        k = proj[:, ATTN_W:2 * ATTN_W]
        v = proj[:, 2 * ATTN_W:3 * ATTN_W]
        u_ref[...] = proj[:, 3 * ATTN_W:]
        qr_ref[...] = q
        kr_ref[...] = k
        bd = bd_ref[...]
        qn = q * lax.rsqrt(_group_mean(q * q, bd, HEAD) + EPS) * gq_ref[...] * (HEAD ** -0.5)
        kn = k * lax.rsqrt(_group_mean(k * k, bd, HEAD) + EPS) * gk_ref[...]
        for hp in range(n_hp):
            sl = slice(hp * LANES, (hp + 1) * LANES)
            q_ref[hp] = qn[:, sl]
            k_ref[hp] = kn[:, sl]
            v_ref[hp] = v[:, sl]

    row = lambda i: (i, 0)
    const = lambda i: (0, 0)
    hp_spec = pl.BlockSpec((n_hp, tm, LANES), lambda i: (0, i, 0))
    hp_shape = jax.ShapeDtypeStruct((n_hp, t_len, LANES), F32)
    return pl.pallas_call(
        body, name="inproj_fwd", grid=(t_len // tm,),
        in_specs=[pl.BlockSpec((tm, D_MODEL), row), pl.BlockSpec((1, D_MODEL), const),
                  pl.BlockSpec((D_MODEL, PROJ_W), const), pl.BlockSpec((1, ATTN_W), const),
                  pl.BlockSpec((1, ATTN_W), const), pl.BlockSpec((ATTN_W, ATTN_W), const)],
        out_specs=[pl.BlockSpec((tm, D_MODEL), row), hp_spec, hp_spec, hp_spec,
                   pl.BlockSpec((tm, SSM_W), row), pl.BlockSpec((tm, ATTN_W), row), pl.BlockSpec((tm, ATTN_W), row)],
        out_shape=[jax.ShapeDtypeStruct((t_len, D_MODEL), BF16), hp_shape, hp_shape, hp_shape,
                   jax.ShapeDtypeStruct((t_len, SSM_W), F32), jax.ShapeDtypeStruct((t_len, ATTN_W), F32),
                   jax.ShapeDtypeStruct((t_len, ATTN_W), F32)],
        compiler_params=_params("arbitrary"),
    )(x, g1, wi, gq, gk, ones64)


def _attn_masks():
    head0 = lax.broadcasted_iota(jnp.int32, (BLK, LANES), 1) < HEAD
    row = lax.broadcasted_iota(jnp.int32, (2 * BLK, 2 * BLK), 0) & (BLK - 1)
    col = lax.broadcasted_iota(jnp.int32, (2 * BLK, 2 * BLK), 1)
    return head0, (col < BLK) & (col >= row), (col >= BLK) & (col - BLK <= row)


def _stack_heads(x, head0):
    return jnp.concatenate([jnp.where(head0, x, 0.0), jnp.where(head0, 0.0, x)], axis=0).astype(BF16)


def _unit_rows(uidx, d):
    nb = TILE // (BLK * d)
    r = lax.div(uidx, nb)
    b = lax.rem(uidx, nb)
    start = r + d * BLK * b
    if d == 1:
        start = pl.multiple_of(start, BLK)
        mk = lambda s: pl.ds(pl.multiple_of(s, BLK), BLK)
    else:
        mk = lambda s: pl.ds(s, BLK, stride=d)
    return b, mk(start), mk(TILE + start), mk(TILE + start - d * BLK)


def _attn_fwd(q, k, v, side=None, side_args=()):
    n_hp, t_len, _ = q.shape
    nt = t_len // TILE
    ns = side.n if side is not None else 0
    n_steps = n_hp * nt

    def body(*refs):
        q_ref, kp_ref, kc_ref, vp_ref, vc_ref = refs[0:5]
        o_ref, lse_ref = refs[5 + ns:7 + ns]
        kk, vv, m_s, l_s, acc_s = refs[7 + 2 * ns:12 + 2 * ns]
        t = pl.program_id(1)
        step = pl.program_id(0) * nt + t
        if side is not None:
            side.bind(refs[5:5 + ns], refs[7 + ns:7 + 2 * ns], refs[12 + 2 * ns:])
            pl.when(step == 0)(side.start)
            pl.when(step == n_steps // 2)(side.forward)
        kk[0:TILE] = kp_ref[0]
        kk[TILE:] = kc_ref[0]
        vv[0:TILE] = vp_ref[0]
        vv[TILE:] = vc_ref[0]
        head0, band_prev, band_cur = _attn_masks()

        for pi, d in enumerate(DILATIONS):
            def unit(uidx, carry, d=d, pi=pi):
                b, rows_q, rows_c, rows_p = _unit_rows(uidx, d)
                mask = band_cur | (band_prev & ((t > 0) | (b > 0)))
                q2 = _stack_heads(q_ref.at[0][rows_q, :], head0)
                kcat = jnp.concatenate([kk[rows_p, :], kk[rows_c, :]], axis=0).astype(BF16)
                vcat = jnp.concatenate([vv[rows_p, :], vv[rows_c, :]], axis=0).astype(BF16)
                s = jnp.where(mask, _nt(q2, kcat), NEG)
                m = jnp.max(s, axis=1, keepdims=True)
                p = jnp.exp(s - m)
                ls = jnp.sum(p, axis=1, keepdims=True)
                pv = _mm(p.astype(BF16), vcat)
                m_s.at[pi][rows_q, :] = jnp.where(head0, m[0:BLK], m[BLK:])
                l_s.at[pi][rows_q, :] = jnp.where(head0, ls[0:BLK], ls[BLK:])
                acc_s.at[pi][rows_q, :] = jnp.where(head0, pv[0:BLK], pv[BLK:])
                return carry

            lax.fori_loop(0, TILE // BLK, unit, 0, unroll=8)

        m_all = jnp.maximum(jnp.maximum(m_s[0], m_s[1]), m_s[2])
        num = jnp.zeros((TILE, LANES), F32)
        den = jnp.zeros((TILE, LANES), F32)
        for pi in range(len(DILATIONS)):
            wgt = jnp.exp(m_s[pi] - m_all)
            num = num + acc_s[pi] * wgt
            den = den + l_s[pi] * wgt
        o_ref[...] = num / den
        lse_ref[0] = m_all + jnp.log(den)
        if side is not None:
            pl.when(step == n_steps - 1)(side.finish)

    cur = lambda hp, t: (hp, t, 0)
    prev = lambda hp, t: (hp, jnp.maximum(t - 1, 0), 0)
    blk = (1, TILE, LANES)
    per_pattern = pltpu.VMEM((len(DILATIONS), TILE, LANES), F32)
    extra = (side.in_specs(), side.out_specs(), side.out_shapes(), side.scratch_shapes()) if side is not None else ([], [], [], [])
    return pl.pallas_call(
        body, name="attn_fwd", grid=(n_hp, nt),
        in_specs=[pl.BlockSpec(blk, cur), pl.BlockSpec(blk, prev), pl.BlockSpec(blk, cur),
                  pl.BlockSpec(blk, prev), pl.BlockSpec(blk, cur)] + extra[0],
        out_specs=[pl.BlockSpec((TILE, LANES), lambda hp, t: (t, hp)), pl.BlockSpec(blk, cur)] + extra[1],
        out_shape=[jax.ShapeDtypeStruct((t_len, ATTN_W), F32), jax.ShapeDtypeStruct((n_hp, t_len, LANES), F32)] + extra[2],
        scratch_shapes=[pltpu.VMEM((2 * TILE, LANES), F32), pltpu.VMEM((2 * TILE, LANES), F32),
                        per_pattern, per_pattern, per_pattern] + extra[3],
        compiler_params=_params("arbitrary", "arbitrary"),
    )(q, k, k, v, v, *side_args)


def _attn_bwd(q, k, v, o, do, lse, ones_hp, side=None, side_args=()):
    n_hp, t_len, _ = q.shape
    nt = t_len // TILE
    ns = side.n if side is not None else 0
    n_pat = len(DILATIONS)

    def body(*refs):
        q_ref, kp_ref, kc_ref, vp_ref, vc_ref, o_ref, do_ref, lse_ref, bd_ref = refs[0:9]
        dq_ref, dk_ref, dv_ref = refs[9 + ns:12 + ns]
        kk, vv, dq_s, dkc, dkp, dvc, dvp, hold_k, hold_v, dl_s = refs[12 + 2 * ns:22 + 2 * ns]
        t = pl.program_id(1)
        if side is not None:
            side.bind(refs[9:9 + ns], refs[12 + ns:12 + 2 * ns], refs[22 + 2 * ns:])
            pl.when((pl.program_id(0) == 0) & (t == 0))(side.start)

        @pl.when(t < nt)
        def _():
            kk[0:TILE] = kp_ref[0]
            kk[TILE:] = kc_ref[0]
            vv[0:TILE] = vp_ref[0]
            vv[TILE:] = vc_ref[0]
            dl_s[...] = _group_mean(do_ref[...] * o_ref[...], bd_ref[...], 1.0)
            head0, band_prev, band_cur = _attn_masks()

            for pi, d in enumerate(DILATIONS):
                def unit(uidx, carry, d=d, pi=pi):
                    b, rows_q, rows_c, rows_p = _unit_rows(uidx, d)
                    mask = band_cur | (band_prev & ((t > 0) | (b > 0)))
                    q2 = _stack_heads(q_ref.at[0][rows_q, :], head0)
                    do2 = _stack_heads(do_ref[rows_q, :], head0)
                    lse_f = lse_ref.at[0][rows_q, :]
                    dl_f = dl_s[rows_q, :]
                    lse2 = jnp.concatenate([lse_f[:, 0:1], lse_f[:, HEAD:HEAD + 1]], axis=0)
                    dl2 = jnp.concatenate([dl_f[:, 0:1], dl_f[:, HEAD:HEAD + 1]], axis=0)
                    kcat = jnp.concatenate([kk[rows_p, :], kk[rows_c, :]], axis=0).astype(BF16)
                    vcat = jnp.concatenate([vv[rows_p, :], vv[rows_c, :]], axis=0).astype(BF16)
                    p = jnp.where(mask, jnp.exp(_nt(q2, kcat) - lse2), 0.0)
                    ds = (p * (_nt(do2, vcat) - dl2)).astype(BF16)
                    dq2 = _mm(ds, kcat)
                    dq_s.at[pi][rows_q, :] = jnp.where(head0, dq2[0:BLK], dq2[BLK:])
                    dk2 = _tn(ds, q2)
                    dv2 = _tn(p.astype(BF16), do2)
                    dkp.at[pi][rows_q, :] = dk2[0:BLK]
                    dkc.at[pi][rows_q, :] = dk2[BLK:]
                    dvp.at[pi][rows_q, :] = dv2[0:BLK]
                    dvc.at[pi][rows_q, :] = dv2[BLK:]
                    return carry

                lax.fori_loop(0, TILE // BLK, unit, 0, unroll=8)

            dq_ref[...] = dq_s[0] + dq_s[1] + dq_s[2]

        @pl.when(t > 0)
        def _():
            dk_ref[...] = hold_k[...]
            dv_ref[...] = hold_v[...]

        @pl.when((t > 0) & (t < nt))
        def _():
            for pi, d in enumerate(DILATIONS):
                back = d * BLK
                dk_ref[TILE - back:, :] = dk_ref[TILE - back:, :] + dkp[pi, 0:back, :]
                dv_ref[TILE - back:, :] = dv_ref[TILE - back:, :] + dvp[pi, 0:back, :]

        @pl.when(t < nt)
        def _():
            hold_k[...] = dkc[0] + dkc[1] + dkc[2]
            hold_v[...] = dvc[0] + dvc[1] + dvc[2]
            for pi, d in enumerate(DILATIONS):
                back = d * BLK
                if back < TILE:
                    hold_k[0:TILE - back, :] = hold_k[0:TILE - back, :] + dkp[pi, back:, :]
                    hold_v[0:TILE - back, :] = hold_v[0:TILE - back, :] + dvp[pi, back:, :]

        if side is not None:
            pl.when((pl.program_id(0) == n_hp - 1) & (t == nt))(side.finish)

    last = nt - 1
    extra = (side.in_specs(), side.out_specs(), side.out_shapes(), side.scratch_shapes()) if side is not None else ([], [], [], [])
    cur = lambda hp, t: (hp, jnp.minimum(t, last), 0)
    prev = lambda hp, t: (hp, jnp.clip(t - 1, 0, last), 0)
    cur2 = lambda hp, t: (jnp.minimum(t, last), hp)
    prev2 = lambda hp, t: (jnp.maximum(t - 1, 0), hp)
    blk = (1, TILE, LANES)
    blk2 = (TILE, LANES)
    out = jax.ShapeDtypeStruct((t_len, ATTN_W), F32)
    return pl.pallas_call(
        body, name="attn_bwd", grid=(n_hp, nt + 1),
        in_specs=[pl.BlockSpec(blk, cur), pl.BlockSpec(blk, prev), pl.BlockSpec(blk, cur),
                  pl.BlockSpec(blk, prev), pl.BlockSpec(blk, cur), pl.BlockSpec(blk2, cur2),
                  pl.BlockSpec(blk2, cur2), pl.BlockSpec(blk, cur), pl.BlockSpec((LANES, LANES), lambda hp, t: (0, 0))]
        + extra[0],
        out_specs=[pl.BlockSpec(blk2, cur2), pl.BlockSpec(blk2, prev2), pl.BlockSpec(blk2, prev2)] + extra[1],
        out_shape=[out, out, out] + extra[2],
        scratch_shapes=[pltpu.VMEM((2 * TILE, LANES), F32), pltpu.VMEM((2 * TILE, LANES), F32)]
        + [pltpu.VMEM((n_pat, TILE, LANES), F32)] * 5 + [pltpu.VMEM((TILE, LANES), F32)] * 3 + extra[3],
        compiler_params=_params("arbitrary", "arbitrary"),
    )(q, k, k, v, v, o, do, lse, ones_hp, *side_args)


def _discretise(lr, li, ldt, br, bi):
    dt = jnp.exp(ldt)
    mag = jnp.exp(lr * dt)
    ab_r, ab_i = mag * jnp.cos(li * dt), mag * jnp.sin(li * dt)
    den = lr * lr + li * li
    nr, ni = ab_r - 1.0, ab_i
    cr = (nr * lr + ni * li) / den
    ci = (ni * lr - nr * li) / den
    return ab_r, ab_i, cr * br - ci * bi, cr * bi + ci * br


def _disc_fwd(lr, li, ldt, br, bi):
    def body(lr_ref, li_ref, ldt_ref, br_ref, bi_ref, ar_o, ai_o, bbr_o, bbi_o):
        outs = _discretise(lr_ref[...], li_ref[...], ldt_ref[...], br_ref[...], bi_ref[...])
        for o_ref, val in zip((ar_o, ai_o, bbr_o, bbi_o), outs):
            o_ref[...] = val

    col = jax.ShapeDtypeStruct(lr.shape, F32)
    mat = jax.ShapeDtypeStruct(br.shape, F32)
    return pl.pallas_call(body, name="s5_disc_fwd", out_shape=[col, col, mat, mat])(lr, li, ldt, br, bi)


def _disc_bwd(lr, li, ldt, br, bi, d_ar, d_ai, d_bbr, d_bbi, group_sum):
    def body(lr_ref, li_ref, ldt_ref, br_ref, bi_ref, c1, c2, c3, c4, gs_ref, dlr_o, dli_o, dldt_o, dbr_o, dbi_o):
        _, vjp = jax.vjp(_discretise, lr_ref[...], li_ref[...], ldt_ref[...], br_ref[...], bi_ref[...])
        dlr, dli, dldt, dbr, dbi = vjp((c1[...], c2[...], c3[...], c4[...]))
        dlr_o[...] = dlr
        dli_o[...] = dli
        dbr_o[...] = dbr
        dbi_o[...] = dbi
        wide = jnp.broadcast_to(dldt, (dldt.shape[0], LANES))
        dldt_o[...] = jnp.dot(gs_ref[...], wide, precision=lax.Precision.HIGHEST, preferred_element_type=F32)

    col = jax.ShapeDtypeStruct(lr.shape, F32)
    mat = jax.ShapeDtypeStruct(br.shape, F32)
    return pl.pallas_call(
        body, name="s5_disc_bwd", out_shape=[col, col, jax.ShapeDtypeStruct((N_GROUPS, LANES), F32), mat, mat],
    )(lr, li, ldt, br, bi, d_ar, d_ai, d_bbr, d_bbi, group_sum)


N_CHUNK = TILE // BLK
HALF = 4


def _cmul(ar, ai, xr, xi):
    return ar * xr - ai * xi, ar * xi + ai * xr


def _power_table(a_ref, tab, sign, reverse):
    ar = [a_ref[0, j:j + 1, :] for j in range(HALF)]
    ai = [sign * a_ref[0, HALF + j:HALF + j + 1, :] for j in range(HALF)]

    def step(s, cur):
        row = pl.ds((BLK - 1 - s) if reverse else s, 1)
        nxt = []
        for j in range(HALF):
            tab.at[j][row, :] = cur[j]
            tab.at[HALF + j][row, :] = cur[HALF + j]
            nxt.append(_cmul(ar[j], ai[j], cur[j], cur[HALF + j]))
        return tuple(p[0] for p in nxt) + tuple(p[1] for p in nxt)

    lax.fori_loop(0, BLK, step, tuple(ar) + tuple(ai))


def _interleave(src, dst):
    for c in range(N_CHUNK):
        dst[pl.ds(c, BLK, stride=N_CHUNK), :] = src[c * BLK:(c + 1) * BLK, :]


def _deinterleave(src, dst):
    for c in range(N_CHUNK):
        dst[c * BLK:(c + 1) * BLK, :] = src[pl.ds(c, BLK, stride=N_CHUNK), :]


def _step_rows(s):
    return pl.ds(pl.multiple_of(s * N_CHUNK, N_CHUNK), N_CHUNK)


def _chunk_scan(buf, a_ref, sign, reverse):
    ar = [jnp.broadcast_to(a_ref[0, j:j + 1, :], (N_CHUNK, LANES)) for j in range(HALF)]
    ai = [sign * jnp.broadcast_to(a_ref[0, HALF + j:HALF + j + 1, :], (N_CHUNK, LANES)) for j in range(HALF)]

    def step(i, carry):
        s = (BLK - 1 - i) if reverse else i
        rows = _step_rows(s)
        out = []
        for j in range(HALF):
            pr, pi = _cmul(ar[j], ai[j], carry[j], carry[HALF + j])
            xr = buf.at[j][rows, :] + pr
            xi = buf.at[HALF + j][rows, :] + pi
            buf.at[j][rows, :] = xr
            buf.at[HALF + j][rows, :] = xi
            out.append((xr, xi))
        return tuple(p[0] for p in out) + tuple(p[1] for p in out)

    zero = jnp.zeros((N_CHUNK, LANES), F32)
    lax.fori_loop(0, BLK, step, (zero,) * (2 * HALF), unroll=2)


def _chunk_states(buf, carry_s, xin_s, tab, reverse):
    edge = 0 if reverse else BLK - 1
    top = 0 if reverse else BLK - 1
    pw = [tab[j, top:top + 1, :] for j in range(2 * HALF)]
    cur = [carry_s[j:j + 1, :] for j in range(2 * HALF)]
    summary = [buf[j, edge * N_CHUNK:(edge + 1) * N_CHUNK, :] for j in range(2 * HALF)]
    order = range(N_CHUNK - 1, -1, -1) if reverse else range(N_CHUNK)
    for c in order:
        for j in range(2 * HALF):
            xin_s[j, c:c + 1, :] = cur[j]
        nxt = []
        for j in range(HALF):
            pr, pi = _cmul(pw[j], pw[HALF + j], cur[j], cur[HALF + j])
            nxt.append((pr + summary[j][c:c + 1, :], pi + summary[HALF + j][c:c + 1, :]))
        cur = [p[0] for p in nxt] + [p[1] for p in nxt]
    for j in range(2 * HALF):
        carry_s[j:j + 1, :] = cur[j]


def _s5_fwd(u, a_cat, b_mat, c_mat, d_skip):
    t_len = u.shape[0]
    nt = t_len // TILE

    def body(u_ref, a_ref, b_ref, c_ref, d_ref, y_ref, x_ref, xs, us, tab, carry_s, xin_s):
        sb = pl.program_id(1)

        @pl.when(sb == 0)
        def _():
            _power_table(a_ref, tab, 1.0, False)
            carry_s[...] = jnp.zeros_like(carry_s)

        _interleave(u_ref, us)
        uv = us[...]
        bu = _mm(uv.astype(BF16), b_ref[0])
        for j in range(2 * HALF):
            xs[j] = bu[:, j * LANES:(j + 1) * LANES]
        _chunk_scan(xs, a_ref, 1.0, False)
        _chunk_states(xs, carry_s, xin_s, tab, False)
        xin = [xin_s[j] for j in range(2 * HALF)]

        def fix(s, acc):
            rows = _step_rows(s)
            for j in range(HALF):
                pr, pi = _cmul(tab.at[j][pl.ds(s, 1), :], tab.at[HALF + j][pl.ds(s, 1), :], xin[j], xin[HALF + j])
                xs.at[j][rows, :] = xs.at[j][rows, :] + pr
                xs.at[HALF + j][rows, :] = xs.at[HALF + j][rows, :] + pi
            return acc

        lax.fori_loop(0, BLK, fix, 0, unroll=2)
        xcat = jnp.concatenate([xs[j].astype(BF16) for j in range(2 * HALF)], axis=1)
        x_ref[0] = xcat
        us[...] = d_ref[0] * uv + _mm(xcat, c_ref[0])
        _deinterleave(us, y_ref)

    return pl.pallas_call(
        body, name="s5_fwd", grid=(N_LB, nt),
        in_specs=[pl.BlockSpec((TILE, LANES), lambda lb, sb: (sb, lb)),
                  pl.BlockSpec((1, 2 * HALF, LANES), lambda lb, sb: (lb, 0, 0)),
                  pl.BlockSpec((1, LANES, 2 * HALF * LANES), lambda lb, sb: (lb, 0, 0)),
                  pl.BlockSpec((1, 2 * HALF * LANES, LANES), lambda lb, sb: (lb, 0, 0)),
                  pl.BlockSpec((1, 1, LANES), lambda lb, sb: (lb, 0, 0))],
        out_specs=[pl.BlockSpec((TILE, LANES), lambda lb, sb: (sb, lb)),
                   pl.BlockSpec((1, TILE, 2 * HALF * LANES), lambda lb, sb: (lb, sb, 0))],
        out_shape=[jax.ShapeDtypeStruct((t_len, SSM_W), F32), jax.ShapeDtypeStruct((N_LB, t_len, 2 * HALF * LANES), BF16)],
        scratch_shapes=[pltpu.VMEM((2 * HALF, TILE, LANES), F32), pltpu.VMEM((TILE, LANES), F32),
                        pltpu.VMEM((2 * HALF, BLK, LANES), F32),
                        pltpu.VMEM((2 * HALF, LANES), F32), pltpu.VMEM((2 * HALF, N_CHUNK, LANES), F32)],
        compiler_params=_params("arbitrary", "arbitrary"),
    )(u, a_cat, b_mat, c_mat, d_skip)


def _s5_bwd(u, dy, states, a_cat, b_mat, c_mat, d_skip):
    t_len = u.shape[0]
    nt = t_len // TILE
    last = nt - 1

    def body(u_ref, dy_ref, x_ref, a_ref, b_ref, c_ref, d_ref, du_ref, db_ref, dc_ref, da_ref, dd_ref,
             gs, us, dys, tabc, lam_s, lin_s):
        sb = pl.program_id(1)

        @pl.when(sb == 0)
        def _():
            _power_table(a_ref, tabc, -1.0, True)
            lam_s[...] = jnp.zeros_like(lam_s)
            db_ref[...] = jnp.zeros_like(db_ref)
            dc_ref[...] = jnp.zeros_like(dc_ref)
            da_ref[...] = jnp.zeros_like(da_ref)
            dd_ref[...] = jnp.zeros_like(dd_ref)

        _interleave(u_ref, us)
        _interleave(dy_ref, dys)
        uv = us[...]
        dyv = dys[...]
        ub = uv.astype(BF16)
        dyb = dyv.astype(BF16)
        gy = _nt(dyb, c_ref[0])
        for j in range(2 * HALF):
            gs[j] = gy[:, j * LANES:(j + 1) * LANES]
        _chunk_scan(gs, a_ref, -1.0, True)
        _chunk_states(gs, lam_s, lin_s, tabc, True)
        zero = jnp.zeros((N_CHUNK, LANES), F32)
        x_tile = x_ref.at[0]
        for grp in range(0, HALF, 2):
            slabs = (grp, grp + 1)
            lin = [(lin_s[j], lin_s[HALF + j]) for j in slabs]

            def fix(i, carry, slabs=slabs, lin=lin):
                s = BLK - 1 - i
                rows = _step_rows(s)
                out = []
                for k, j in enumerate(slabs):
                    nr, ni, acc_r, acc_i = carry[4 * k:4 * k + 4]
                    xr = x_tile[rows, pl.ds(j * LANES, LANES)].astype(F32)
                    xi = x_tile[rows, pl.ds((HALF + j) * LANES, LANES)].astype(F32)
                    qr, qi = _cmul(tabc.at[j][pl.ds(s, 1), :], tabc.at[HALF + j][pl.ds(s, 1), :], lin[k][0], lin[k][1])
                    lr_ = gs.at[j][rows, :] + qr
                    li_ = gs.at[HALF + j][rows, :] + qi
                    gs.at[j][rows, :] = lr_
                    gs.at[HALF + j][rows, :] = li_
                    out += [lr_, li_, acc_r + (xr * nr + xi * ni), acc_i + (xr * ni - xi * nr)]
                return tuple(out)

            init = []
            for k in range(len(slabs)):
                init += [lin[k][0], lin[k][1], zero, zero]
            res = lax.fori_loop(0, BLK, fix, tuple(init), unroll=2)
            for k, j in enumerate(slabs):
                da_ref[0, j:j + 1, :] = da_ref[0, j:j + 1, :] + _colsum(res[4 * k + 2])
                da_ref[0, HALF + j:HALF + j + 1, :] = da_ref[0, HALF + j:HALF + j + 1, :] + _colsum(res[4 * k + 3])
        lam = jnp.concatenate([gs[j].astype(BF16) for j in range(2 * HALF)], axis=1)
        us[...] = _nt(lam, b_ref[0]) + d_ref[0] * dyv
        _deinterleave(us, du_ref)
        db_ref[0] = db_ref[0] + _tn(ub, lam)
        dc_ref[0] = dc_ref[0] + _tn(dyb, x_ref[0])
        dd_ref[0] = dd_ref[0] + _colsum(dyv * uv)

    rev = lambda lb, sb: (last - sb, lb)
    per_lb = lambda lb, sb: (lb, 0, 0)
    wide = 2 * HALF * LANES
    return pl.pallas_call(
        body, name="s5_bwd", grid=(N_LB, nt),
        in_specs=[pl.BlockSpec((TILE, LANES), rev), pl.BlockSpec((TILE, LANES), rev),
                  pl.BlockSpec((1, TILE, wide), lambda lb, sb: (lb, last - sb, 0)),
                  pl.BlockSpec((1, 2 * HALF, LANES), per_lb), pl.BlockSpec((1, LANES, wide), per_lb),
                  pl.BlockSpec((1, wide, LANES), per_lb), pl.BlockSpec((1, 1, LANES), per_lb)],
        out_specs=[pl.BlockSpec((TILE, LANES), rev), pl.BlockSpec((1, LANES, wide), per_lb),
                   pl.BlockSpec((1, LANES, wide), per_lb), pl.BlockSpec((1, 2 * HALF, LANES), per_lb),
                   pl.BlockSpec((1, 1, LANES), per_lb)],
        out_shape=[jax.ShapeDtypeStruct((t_len, SSM_W), F32), jax.ShapeDtypeStruct((N_LB, LANES, wide), F32),
                   jax.ShapeDtypeStruct((N_LB, LANES, wide), F32), jax.ShapeDtypeStruct((N_LB, 2 * HALF, LANES), F32),
                   jax.ShapeDtypeStruct((N_LB, 1, LANES), F32)],
        scratch_shapes=[pltpu.VMEM((2 * HALF, TILE, LANES), F32),
                        pltpu.VMEM((TILE, LANES), F32), pltpu.VMEM((TILE, LANES), F32),
                        pltpu.VMEM((2 * HALF, BLK, LANES), F32), pltpu.VMEM((2 * HALF, LANES), F32),
                        pltpu.VMEM((2 * HALF, N_CHUNK, LANES), F32)],
        compiler_params=_params("arbitrary", "arbitrary"),
    )(u, dy, states, a_cat, b_mat, c_mat, d_skip)


_GELU_C = math.sqrt(2.0 / math.pi)
_GELU_K = 0.044715


def _gelu(y):
    t = jnp.tanh(_GELU_C * (y + _GELU_K * (y * y * y)))
    return y * (0.5 * (1.0 + t)), t


def _gelu_grad(y, t):
    return 0.5 * (1.0 + t) + 0.5 * y * (1.0 - t * t) * (_GELU_C * (1.0 + 3.0 * _GELU_K * y * y))


def _glu(y, wg, bias):
    z, t = _gelu(y)
    sg = jax.nn.sigmoid(_mm(z.astype(BF16), wg) + bias)
    return z, t, sg


def _mix_fwd(attn, y, x, wg, glu_b, ga, gs, wo):
    t_len = x.shape[0]
    tm = 512

    def body(attn_ref, y_ref, x_ref, wg_ref, b_ref, ga_ref, gs_ref, wo_ref, x2_ref, mix_ref, z_ref):
        z, _, sg = _glu(y_ref[...], wg_ref[...], b_ref[...])
        z_ref[...] = z.astype(BF16)
        s = z * sg
        av = attn_ref[...]
        an = (av * _rms(av) * ga_ref[...]).astype(BF16)
        sn = (s * _rms(s) * gs_ref[...]).astype(BF16)
        mix_ref[:, 0:ATTN_W] = an
        mix_ref[:, ATTN_W:] = sn
        x2_ref[...] = x_ref[...] + _mm(an, wo_ref[0:ATTN_W, :]) + _mm(sn, wo_ref[ATTN_W:, :])

    row = lambda i: (i, 0)
    const = lambda i: (0, 0)
    return pl.pallas_call(
        body, name="mix_fwd", grid=(t_len // tm,),
        in_specs=[pl.BlockSpec((tm, ATTN_W), row), pl.BlockSpec((tm, SSM_W), row), pl.BlockSpec((tm, D_MODEL), row),
                  pl.BlockSpec((SSM_W, SSM_W), const), pl.BlockSpec((1, SSM_W), const), pl.BlockSpec((1, ATTN_W), const),
                  pl.BlockSpec((1, SSM_W), const), pl.BlockSpec((D_MODEL, D_MODEL), const)],
        out_specs=[pl.BlockSpec((tm, D_MODEL), row), pl.BlockSpec((tm, D_MODEL), row), pl.BlockSpec((tm, SSM_W), row)],
        out_shape=[jax.ShapeDtypeStruct((t_len, D_MODEL), F32), jax.ShapeDtypeStruct((t_len, D_MODEL), BF16),
                   jax.ShapeDtypeStruct((t_len, SSM_W), BF16)],
        compiler_params=_params("arbitrary"),
    )(attn, y, x, wg, glu_b, ga, gs, wo)


def _mlp(x2, target, g2, wu, wd):
    t_len = x2.shape[0]
    tm = 256
    fc = 1024
    n_fc = D_FF // fc

    def body(x2_ref, tg_ref, g2_ref, wu_hbm, wd_hbm, dx2_ref, hdn_ref, dup_ref, h_ref, dyb_ref, dg2_ref, loss_ref,
             wu_s, wd_s, relu_s, sem):
        @pl.when(pl.program_id(0) == 0)
        def _():
            cu = pltpu.make_async_copy(wu_hbm, wu_s, sem.at[0])
            cd = pltpu.make_async_copy(wd_hbm, wd_s, sem.at[1])
            cu.start()
            cd.start()
            cu.wait()
            cd.wait()
            dg2_ref[...] = jnp.zeros_like(dg2_ref)
            loss_ref[...] = jnp.zeros_like(loss_ref)

        x2v = x2_ref[...]
        r = _rms(x2v)
        g2v = g2_ref[...]
        h = (x2v * r * g2v).astype(BF16)
        h_ref[...] = h
        yout = x2v
        for c in range(n_fc):
            cols = slice(c * fc, (c + 1) * fc)
            ru = jnp.maximum(_mm(h, wu_s[:, cols]), 0.0)
            relu_s[:, cols] = ru
            hd = (ru * ru).astype(BF16)
            hdn_ref[:, cols] = hd
            yout = yout + _mm(hd, wd_s[cols, :])
        err = yout - tg_ref[...]
        loss_ref[...] = loss_ref[...] + 0.5 * jnp.sum(err * err) * (1.0 / D_MODEL)
        dy = err * (1.0 / D_MODEL)
        dyb = dy.astype(BF16)
        dyb_ref[...] = dyb
        dh = jnp.zeros((tm, D_MODEL), F32)
        for c in range(n_fc):
            cols = slice(c * fc, (c + 1) * fc)
            dup = (_nt(dyb, wd_s[cols, :]) * (2.0 * relu_s[:, cols])).astype(BF16)
            dup_ref[:, cols] = dup
            dh = dh + _nt(dup, wu_s[:, cols])
        dxn, gterm = _rms_bwd(dh, x2v, r, g2v)
        dx2_ref[...] = dy + dxn
        dg2_ref[...] = dg2_ref[...] + _colsum(gterm)

    row = lambda i: (i, 0)
    const = lambda i: (0, 0)
    any_spec = pl.BlockSpec(memory_space=pl.ANY)
    return pl.pallas_call(
        body, name="mlp", grid=(t_len // tm,),
        in_specs=[pl.BlockSpec((tm, D_MODEL), row), pl.BlockSpec((tm, D_MODEL), row), pl.BlockSpec((1, D_MODEL), const),
                  any_spec, any_spec],
        out_specs=[pl.BlockSpec((tm, D_MODEL), row), pl.BlockSpec((tm, D_FF), row), pl.BlockSpec((tm, D_FF), row),
                   pl.BlockSpec((tm, D_MODEL), row), pl.BlockSpec((tm, D_MODEL), row), pl.BlockSpec((1, D_MODEL), const),
                   pl.BlockSpec((1, LANES), const)],
        out_shape=[jax.ShapeDtypeStruct((t_len, D_MODEL), F32), jax.ShapeDtypeStruct((t_len, D_FF), BF16),
                   jax.ShapeDtypeStruct((t_len, D_FF), BF16), jax.ShapeDtypeStruct((t_len, D_MODEL), BF16),
                   jax.ShapeDtypeStruct((t_len, D_MODEL), BF16), jax.ShapeDtypeStruct((1, D_MODEL), F32),
                   jax.ShapeDtypeStruct((1, LANES), F32)],
        scratch_shapes=[pltpu.VMEM((D_MODEL, D_FF), BF16), pltpu.VMEM((D_FF, D_MODEL), BF16),
                        pltpu.VMEM((tm, D_FF), F32), pltpu.SemaphoreType.DMA((2,))],
        compiler_params=_params("arbitrary"),
    )(x2, target, g2, wu, wd)


def _mix_bwd(dx2, attn, y, wg, glu_b, ga, gs, wo):
    t_len = dx2.shape[0]
    tm = 512

    def body(dx2_ref, attn_ref, y_ref, wg_ref, b_ref, ga_ref, gs_ref, wo_ref,
             dattn_ref, dy_ref, dx2b_ref, dgp_ref, dga_ref, dgs_ref, db_ref):
        @pl.when(pl.program_id(0) == 0)
        def _():
            dga_ref[...] = jnp.zeros_like(dga_ref)
            dgs_ref[...] = jnp.zeros_like(dgs_ref)
            db_ref[...] = jnp.zeros_like(db_ref)

        dx2b = dx2_ref[...].astype(BF16)
        dx2b_ref[...] = dx2b
        d_an = _nt(dx2b, wo_ref[0:ATTN_W, :])
        d_sn = _nt(dx2b, wo_ref[ATTN_W:, :])
        yv = y_ref[...]
        wg = wg_ref[...]
        z, t, sg = _glu(yv, wg, b_ref[...])
        s = z * sg
        av = attn_ref[...]
        d_attn, ga_term = _rms_bwd(d_an, av, _rms(av), ga_ref[...])
        d_s, gs_term = _rms_bwd(d_sn, s, _rms(s), gs_ref[...])
        dattn_ref[...] = d_attn
        dgp = d_s * z * sg * (1.0 - sg)
        dgpb = dgp.astype(BF16)
        dgp_ref[...] = dgpb
        dz = d_s * sg + _nt(dgpb, wg)
        dy_ref[...] = dz * _gelu_grad(yv, t)
        dga_ref[...] = dga_ref[...] + _colsum(ga_term)
        dgs_ref[...] = dgs_ref[...] + _colsum(gs_term)
        db_ref[...] = db_ref[...] + _colsum(dgp)

    row = lambda i: (i, 0)
    const = lambda i: (0, 0)
    vec = jax.ShapeDtypeStruct((1, SSM_W), F32)
    return pl.pallas_call(
        body, name="mix_bwd", grid=(t_len // tm,),
        in_specs=[pl.BlockSpec((tm, D_MODEL), row), pl.BlockSpec((tm, ATTN_W), row), pl.BlockSpec((tm, SSM_W), row),
                  pl.BlockSpec((SSM_W, SSM_W), const), pl.BlockSpec((1, SSM_W), const), pl.BlockSpec((1, ATTN_W), const),
                  pl.BlockSpec((1, SSM_W), const), pl.BlockSpec((D_MODEL, D_MODEL), const)],
        out_specs=[pl.BlockSpec((tm, ATTN_W), row), pl.BlockSpec((tm, SSM_W), row), pl.BlockSpec((tm, D_MODEL), row),
                   pl.BlockSpec((tm, SSM_W), row), pl.BlockSpec((1, ATTN_W), const), pl.BlockSpec((1, SSM_W), const),
                   pl.BlockSpec((1, SSM_W), const)],
        out_shape=[jax.ShapeDtypeStruct((t_len, ATTN_W), F32), jax.ShapeDtypeStruct((t_len, SSM_W), F32),
                   jax.ShapeDtypeStruct((t_len, D_MODEL), BF16), jax.ShapeDtypeStruct((t_len, SSM_W), BF16), vec, vec, vec],
        compiler_params=_params("arbitrary"),
    )(dx2, attn, y, wg, glu_b, ga, gs, wo)


def _mixer_mlp(attn, y, x, target, wg, glu_b, ga, gs, wo, g2, wu, wd):
    t_len = x.shape[0]
    tm = 256
    fc = 1024
    n_fc = D_FF // fc

    def body(attn_ref, y_ref, x_ref, tg_ref, b_ref, ga_ref, gs_ref, g2_ref, wg_s, wo_s, wu_hbm, wd_hbm,
             dx2_ref, hdn_ref, dup_ref, h_ref, dyb_ref, mix_ref, z_ref, dattn_ref, dy_ref, dx2b_ref, dgp_ref,
             dg2_ref, loss_ref, dga_ref, dgs_ref, db_ref, wu_s, wd_s, relu_s, sem):
        @pl.when(pl.program_id(0) == 0)
        def _():
            copies = [pltpu.make_async_copy(src, dst, sem.at[k]) for k, (src, dst) in enumerate(((wu_hbm, wu_s), (wd_hbm, wd_s)))]
            for cp in copies:
                cp.start()
            for cp in copies:
                cp.wait()
            for acc in (dg2_ref, loss_ref, dga_ref, dgs_ref, db_ref):
                acc[...] = jnp.zeros_like(acc)

        av = attn_ref[...]
        z, _, sg = _glu(y_ref[...], wg_s[...], b_ref[...])
        z_ref[...] = z.astype(BF16)
        s = z * sg
        an = (av * _rms(av) * ga_ref[...]).astype(BF16)
        sn = (s * _rms(s) * gs_ref[...]).astype(BF16)
        mix_ref[:, 0:ATTN_W] = an
        mix_ref[:, ATTN_W:] = sn
        dx2_ref[...] = x_ref[...] + _mm(an, wo_s[0:ATTN_W, :]) + _mm(sn, wo_s[ATTN_W:, :])
        r = _rms(dx2_ref[...])
        g2v = g2_ref[...]
        h = (dx2_ref[...] * r * g2v).astype(BF16)
        h_ref[...] = h
        yout = dx2_ref[...]
        for c in range(n_fc):
            cols = slice(c * fc, (c + 1) * fc)
            ru = jnp.maximum(_mm(h, wu_s[:, cols]), 0.0)
            relu_s[:, cols] = ru
            hd = (ru * ru).astype(BF16)
            hdn_ref[:, cols] = hd
            yout = yout + _mm(hd, wd_s[cols, :])
        err = yout - tg_ref[...]
        loss_ref[...] = loss_ref[...] + 0.5 * jnp.sum(err * err) * (1.0 / D_MODEL)
        dy = err * (1.0 / D_MODEL)
        dyb = dy.astype(BF16)
        dyb_ref[...] = dyb
        dh = jnp.zeros((tm, D_MODEL), F32)
        for c in range(n_fc):
            cols = slice(c * fc, (c + 1) * fc)
            dup = (_nt(dyb, wd_s[cols, :]) * (2.0 * relu_s[:, cols])).astype(BF16)
            dup_ref[:, cols] = dup
            dh = dh + _nt(dup, wu_s[:, cols])
        dxn, g2_term = _rms_bwd(dh, dx2_ref[...], r, g2v)
        dx2_ref[...] = dy + dxn
        dg2_ref[...] = dg2_ref[...] + _colsum(g2_term)
        dx2b = dx2_ref[...].astype(BF16)
        dx2b_ref[...] = dx2b
        yv = y_ref[...]
        av = attn_ref[...]
        z, t, sg = _glu(yv, wg_s[...], b_ref[...])
        s = z * sg
        d_attn, ga_term = _rms_bwd(_nt(dx2b, wo_s[0:ATTN_W, :]), av, _rms(av), ga_ref[...])
        d_s, gs_term = _rms_bwd(_nt(dx2b, wo_s[ATTN_W:, :]), s, _rms(s), gs_ref[...])
        dattn_ref[...] = d_attn
        dgp = d_s * z * sg * (1.0 - sg)
        dgpb = dgp.astype(BF16)
        dgp_ref[...] = dgpb
        dy_ref[...] = (d_s * sg + _nt(dgpb, wg_s[...])) * _gelu_grad(yv, t)
        dga_ref[...] = dga_ref[...] + _colsum(ga_term)
        dgs_ref[...] = dgs_ref[...] + _colsum(gs_term)
        db_ref[...] = db_ref[...] + _colsum(dgp)

    row = lambda i: (i, 0)
    const = lambda i: (0, 0)
    wide = lambda n: pl.BlockSpec((tm, n), row)
    vec = lambda n: pl.BlockSpec((1, n), const)
    any_spec = pl.BlockSpec(memory_space=pl.ANY)
    f32 = lambda n: jax.ShapeDtypeStruct((t_len, n), F32)
    b16 = lambda n: jax.ShapeDtypeStruct((t_len, n), BF16)
    acc = lambda n: jax.ShapeDtypeStruct((1, n), F32)
    return pl.pallas_call(
        body, name="mixer_mlp", grid=(t_len // tm,),
        in_specs=[wide(ATTN_W), wide(SSM_W), wide(D_MODEL), wide(D_MODEL), vec(SSM_W), vec(ATTN_W), vec(SSM_W), vec(D_MODEL),
                  pl.BlockSpec((SSM_W, SSM_W), const), pl.BlockSpec((D_MODEL, D_MODEL), const), any_spec, any_spec],
        out_specs=[wide(D_MODEL), wide(D_FF), wide(D_FF), wide(D_MODEL), wide(D_MODEL), wide(D_MODEL), wide(SSM_W),
                   wide(ATTN_W), wide(SSM_W), wide(D_MODEL), wide(SSM_W),
                   vec(D_MODEL), vec(LANES), vec(ATTN_W), vec(SSM_W), vec(SSM_W)],
        out_shape=[f32(D_MODEL), b16(D_FF), b16(D_FF), b16(D_MODEL), b16(D_MODEL), b16(D_MODEL), b16(SSM_W),
                   f32(ATTN_W), f32(SSM_W), b16(D_MODEL), b16(SSM_W),
                   acc(D_MODEL), acc(LANES), acc(ATTN_W), acc(SSM_W), acc(SSM_W)],
        scratch_shapes=[pltpu.VMEM((D_MODEL, D_FF), BF16), pltpu.VMEM((D_FF, D_MODEL), BF16),
                        pltpu.VMEM((tm, D_FF), F32), pltpu.SemaphoreType.DMA((2,))],
        compiler_params=_params("arbitrary"),
    )(attn, y, x, target, glu_b, ga, gs, g2, wg, wo, wu, wd)


def _inproj_bwd(dqs, dkn, dv, du, q_raw, k_raw, x, dx2, wi, g1, gq, gk, ones64):
    t_len = x.shape[0]
    tm = 512
    n_heads = ATTN_W // HEAD

    def body(dqs_ref, dkn_ref, dv_ref, du_ref, q_ref, k_ref, x_ref, dx2_ref, wi_ref, g1_ref, gq_ref, gk_ref, bd_ref,
             gx_ref, dproj_ref, dg1_ref, dgq_ref, dgk_ref, accq, acck):
        i = pl.program_id(0)

        @pl.when(i == 0)
        def _():
            dg1_ref[...] = jnp.zeros_like(dg1_ref)
            accq[...] = jnp.zeros_like(accq)
            acck[...] = jnp.zeros_like(acck)

        bd = bd_ref[...]

        def head_norm_bwd(dy, raw, gain, acc):
            r = lax.rsqrt(_group_mean(raw * raw, bd, HEAD) + EPS)
            xh = raw * r
            dxh = dy * gain
            acc[...] = acc[...] + _colsum(dy * xh)
            return r * (dxh - xh * _group_mean(dxh * xh, bd, HEAD))

        dq = head_norm_bwd(dqs_ref[...] * (HEAD ** -0.5), q_ref[...], gq_ref[...], accq)
        dk = head_norm_bwd(dkn_ref[...], k_ref[...], gk_ref[...], acck)
        dproj_ref[:, 0:ATTN_W] = dq.astype(BF16)
        dproj_ref[:, ATTN_W:2 * ATTN_W] = dk.astype(BF16)
        dproj_ref[:, 2 * ATTN_W:3 * ATTN_W] = dv_ref[...].astype(BF16)
        dproj_ref[:, 3 * ATTN_W:] = du_ref[...].astype(BF16)
        dxn = _nt(dproj_ref[...], wi_ref[...])
        xv = x_ref[...]
        g1v = g1_ref[...]
        dx, g1_term = _rms_bwd(dxn, xv, _rms(xv), g1v)
        gx_ref[...] = dx2_ref[...] + dx
        dg1_ref[...] = dg1_ref[...] + _colsum(g1_term)

        @pl.when(i == pl.num_programs(0) - 1)
        def _():
            for acc, out in ((accq, dgq_ref), (acck, dgk_ref)):
                tot = acc[:, 0:HEAD]
                for h in range(1, n_heads):
                    tot = tot + acc[:, h * HEAD:(h + 1) * HEAD]
                out[...] = tot

    row = lambda i: (i, 0)
    const = lambda i: (0, 0)
    aw = pl.BlockSpec((tm, ATTN_W), row)
    dm = pl.BlockSpec((tm, D_MODEL), row)
    return pl.pallas_call(
        body, name="inproj_bwd", grid=(t_len // tm,),
        in_specs=[aw, aw, aw, aw, aw, aw, dm, dm, pl.BlockSpec((D_MODEL, PROJ_W), const), pl.BlockSpec((1, D_MODEL), const),
                  pl.BlockSpec((1, ATTN_W), const), pl.BlockSpec((1, ATTN_W), const), pl.BlockSpec((ATTN_W, ATTN_W), const)],
        out_specs=[dm, pl.BlockSpec((tm, PROJ_W), row), pl.BlockSpec((1, D_MODEL), const),
                   pl.BlockSpec((1, HEAD), const), pl.BlockSpec((1, HEAD), const)],
        out_shape=[jax.ShapeDtypeStruct((t_len, D_MODEL), F32), jax.ShapeDtypeStruct((t_len, PROJ_W), BF16),
                   jax.ShapeDtypeStruct((1, D_MODEL), F32), jax.ShapeDtypeStruct((1, HEAD), F32),
                   jax.ShapeDtypeStruct((1, HEAD), F32)],
        scratch_shapes=[pltpu.VMEM((1, ATTN_W), F32), pltpu.VMEM((1, ATTN_W), F32)],
        compiler_params=_params("arbitrary"),
    )(dqs, dkn, dv, du, q_raw, k_raw, x, dx2, wi, g1, gq, gk, ones64)


def _grad_matmul(a, b, name):
    t_len, m = a.shape
    n = b.shape[1]
    bm, bn, bt = min(m, 1024), min(n, 1024), 1024

    def body(a_ref, b_ref, o_ref):
        @pl.when(pl.program_id(2) == 0)
        def _():
            o_ref[...] = jnp.zeros_like(o_ref)

        o_ref[...] = o_ref[...] + _tn(a_ref[...], b_ref[...])

    return pl.pallas_call(
        body, name=name, grid=(m // bm, n // bn, t_len // bt),
        in_specs=[pl.BlockSpec((bt, bm), lambda i, j, k: (k, i)), pl.BlockSpec((bt, bn), lambda i, j, k: (k, j))],
        out_specs=pl.BlockSpec((bm, bn), lambda i, j, k: (i, j)),
        out_shape=jax.ShapeDtypeStruct((m, n), F32),
        compiler_params=_params("arbitrary", "arbitrary", "arbitrary"),
    )(a, b)


def _adamw_update(w_ref, g_ref, m_ref, v_ref, d_o, m_o, v_o):
    gv = g_ref[...]
    mn = ADAM_B1 * m_ref[...] + (1.0 - ADAM_B1) * gv
    vn = ADAM_B2 * v_ref[...] + (1.0 - ADAM_B2) * jnp.square(gv)
    m_hat = mn / (1.0 - ADAM_B1 ** ADAM_STEP)
    v_hat = vn / (1.0 - ADAM_B2 ** ADAM_STEP)
    d_o[...] = -ADAM_LR * (m_hat / (jnp.sqrt(v_hat) + ADAM_EPS) + ADAM_WD * w_ref[...])
    m_o[...] = mn
    v_o[...] = vn


def _adamw_many(ws, gs, ms, vs):
    n = len(ws)

    def body(*refs):
        for i in range(n):
            _adamw_update(*[refs[k * n + i] for k in range(7)])

    shapes = [jax.ShapeDtypeStruct(w.shape, F32) for w in ws]
    outs = pl.pallas_call(body, name="adamw_small", out_shape=shapes * 3,
                          compiler_params=pltpu.CompilerParams(vmem_limit_bytes=VMEM_LIMIT))(*ws, *gs, *ms, *vs)
    return outs[0:n], outs[n:2 * n], outs[2 * n:]


def _adamw(w, g, m, v, name):
    rows, cols = w.shape
    br = _row_block(rows, 256)
    body = functools.partial(_adamw_update)

    spec = pl.BlockSpec((br, cols), lambda i: (i, 0))
    shape = jax.ShapeDtypeStruct((rows, cols), F32)
    return pl.pallas_call(
        body, name=name, grid=(rows // br,), in_specs=[spec] * 4, out_specs=[spec] * 3, out_shape=[shape] * 3,
        compiler_params=_params("arbitrary"),
    )(w, g, m, v)


def _sum_arrays(arrs, name, out_dtype=F32):
    rows, cols = arrs[0].shape
    br = _row_block(rows, 512)
    n = len(arrs)

    def body(*refs):
        tot = refs[0][...]
        for r in refs[1:n]:
            tot = tot + r[...]
        refs[n][...] = tot.astype(out_dtype)

    spec = pl.BlockSpec((br, cols), lambda i: (i, 0))
    return pl.pallas_call(
        body, name=name, grid=(rows // br,), in_specs=[spec] * n, out_specs=spec,
        out_shape=jax.ShapeDtypeStruct((rows, cols), out_dtype), compiler_params=_params("arbitrary"),
    )(*arrs)


GPL = N_GROUPS // N_LB
SW = GPL * N_STATE


def _eye_groups():
    return jnp.eye(GPL, dtype=F32)


def _s5_matrices(ab_r, ab_i, bb_r, bb_i, c_re, c_im, d_skip):
    eye = _eye_groups()
    a_cat = jnp.concatenate([ab_r.reshape(N_LB, HALF, LANES), ab_i.reshape(N_LB, HALF, LANES)], axis=1)

    def b_part(bb):
        b4 = jnp.transpose(bb.reshape(N_LB, GPL, N_STATE, GROUP_W), (0, 1, 3, 2))
        return (b4[:, :, :, None, :] * eye[None, :, None, :, None]).reshape(N_LB, LANES, SW)

    def c_part(cc):
        c4 = jnp.transpose(cc.reshape(N_LB, GPL, GROUP_W, N_STATE), (0, 1, 3, 2))
        return (c4[:, :, :, None, :] * eye[None, :, None, :, None]).reshape(N_LB, SW, LANES)

    b_mat = jnp.concatenate([b_part(bb_r), b_part(bb_i)], axis=2).astype(BF16)
    c_mat = jnp.concatenate([c_part(c_re), -c_part(c_im)], axis=1).astype(BF16)
    return a_cat, b_mat, c_mat, d_skip.reshape(N_LB, 1, LANES)


def _s5_unpack_grads(db, dc, da, dd):
    eye = _eye_groups()
    mask = eye[None, :, None, None, :, None]
    d6 = jnp.sum(db.reshape(N_LB, GPL, GROUP_W, 2, GPL, N_STATE) * mask, axis=4)
    dbb = jnp.transpose(d6, (3, 0, 1, 4, 2)).reshape(2, N_GROUPS * N_STATE, GROUP_W)
    c6 = jnp.sum(dc.reshape(N_LB, GPL, GROUP_W, 2, GPL, N_STATE) * mask, axis=4)
    dcc = jnp.transpose(c6, (3, 0, 1, 2, 4)).reshape(2, N_GROUPS, GROUP_W, N_STATE)
    dab_r = da[:, :HALF].reshape(N_GROUPS * N_STATE, 1)
    dab_i = da[:, HALF:].reshape(N_GROUPS * N_STATE, 1)
    return dab_r, dab_i, dbb[0], dbb[1], dcc[0], -dcc[1], dd.reshape(N_GROUPS, GROUP_W)


def _block_ones(n, width):
    i = lax.broadcasted_iota(jnp.int32, (n, n), 0) // width
    j = lax.broadcasted_iota(jnp.int32, (n, n), 1) // width
    return (i == j).astype(BF16)


def _tile_heads(g):
    return jnp.tile(g.reshape(1, HEAD), (1, ATTN_W // HEAD))


def _local_step(x, target, wi, rest, p, fwd_side=None, bwd_side=None):
    ones64 = _block_ones(ATTN_W, HEAD)
    ones_hp = _block_ones(LANES, HEAD)
    g1 = p["norm1_g"].reshape(1, D_MODEL)
    g2 = p["norm2_g"].reshape(1, D_MODEL)
    gq = _tile_heads(p["q_norm_g"])
    gk = _tile_heads(p["k_norm_g"])
    ga = p["attn_out_norm_g"].reshape(1, ATTN_W)
    gs = p["ssm_out_norm_g"].reshape(1, SSM_W)
    glu_b = p["glu_b"].reshape(1, SSM_W)
    n_gp = N_GROUPS * N_STATE
    lr = p["ssm_a_re"].reshape(n_gp, 1)
    li = p["ssm_a_im"].reshape(n_gp, 1)
    ldt = jnp.repeat(p["ssm_log_dt"].reshape(N_GROUPS), N_STATE).reshape(n_gp, 1)
    br = p["ssm_b_re"].reshape(n_gp, GROUP_W)
    bi = p["ssm_b_im"].reshape(n_gp, GROUP_W)
    ab_r, ab_i, bb_r, bb_i = _disc_fwd(lr, li, ldt, br, bi)
    a_cat, b_mat, c_mat, d_mat = _s5_matrices(
        ab_r, ab_i, bb_r, bb_i, p["ssm_c_re"].reshape(N_GROUPS, GROUP_W, N_STATE),
        p["ssm_c_im"].reshape(N_GROUPS, GROUP_W, N_STATE), p["ssm_d"])

    xn, qn, kn, vv, u, q_raw, k_raw = _inproj_fwd(x, g1, wi, gq, gk, ones64)
    if fwd_side is None:
        attn, lse = _attn_fwd(qn, kn, vv)
    else:
        attn, lse, *rest = _attn_fwd(qn, kn, vv, *fwd_side)
    wg, wo, wu, wd = rest
    y, states = _s5_fwd(u, a_cat, b_mat, c_mat, d_mat)
    (dx2, hdn, dup, h, dyb, mix, z, d_attn, dy_ssm, dx2b, dgp, dg2, loss, dga, dgs, dglu_b) = _mixer_mlp(
        attn, y, x, target, wg, glu_b, ga, gs, wo, g2, wu, wd)
    big = {"w_mlp_up": _grad_matmul(h, dup, "grad_w_mlp_up"), "w_mlp_down": _grad_matmul(hdn, dyb, "grad_w_mlp_down"),
           "w_out": _grad_matmul(mix, dx2b, "grad_w_out"), "glu_w": _grad_matmul(z, dgp, "grad_glu_w")}
    rode = []
    if bwd_side is None:
        dqs, dkn, dvv = _attn_bwd(qn, kn, vv, attn, d_attn, lse, ones_hp)
    else:
        dqs, dkn, dvv, *rode = _attn_bwd(qn, kn, vv, attn, d_attn, lse, ones_hp, *bwd_side(big))
    du, db, dc, da, dd = _s5_bwd(u, dy_ssm, states, a_cat, b_mat, c_mat, d_mat)
    grad_x, dproj, dg1, dgq, dgk = _inproj_bwd(dqs, dkn, dvv, du, q_raw, k_raw, x, dx2, wi, g1, gq, gk, ones64)

    big["w_in"] = _grad_matmul(xn, dproj, "grad_w_in")
    dab_r, dab_i, dbb_r, dbb_i, dc_re, dc_im, dd_g = _s5_unpack_grads(db, dc, da, dd)
    cot = {"norm1_g": dg1, "q_norm_g": dgq, "k_norm_g": dgk, "ab_r": dab_r, "ab_i": dab_i, "bb_r": dbb_r, "bb_i": dbb_i,
           "ssm_c_re": dc_re, "ssm_c_im": dc_im, "ssm_d": dd_g, "glu_b": dglu_b, "attn_out_norm_g": dga,
           "ssm_out_norm_g": dgs, "norm2_g": dg2}
    return loss[0, 0], grad_x, big, cot, (lr, li, ldt, br, bi), rode


COT_NAMES = ("norm1_g", "q_norm_g", "k_norm_g", "ab_r", "ab_i", "bb_r", "bb_i", "ssm_c_re", "ssm_c_im", "ssm_d",
             "glu_b", "attn_out_norm_g", "ssm_out_norm_g", "norm2_g")
SMALL_NAMES = ("norm1_g", "q_norm_g", "k_norm_g", "ssm_a_re", "ssm_a_im", "ssm_log_dt", "ssm_b_re", "ssm_b_im",
               "ssm_c_re", "ssm_c_im", "ssm_d", "glu_b", "attn_out_norm_g", "ssm_out_norm_g", "norm2_g")
BIG_NAMES = ("w_in", "glu_w", "w_out", "w_mlp_up", "w_mlp_down")
PACK_ROWS = 1152


def _pack(arrs):
    flat = jnp.concatenate([a.reshape(-1) for a in arrs])
    return jnp.pad(flat, (0, PACK_ROWS * LANES - flat.shape[0])).reshape(PACK_ROWS, LANES)


def _unpack(packed, like):
    flat = packed.reshape(-1)
    out, pos = [], 0
    for a in like:
        out.append(flat[pos:pos + a.size].reshape(a.shape))
        pos += a.size
    return out


def _small_grads(cot, disc_in, p):
    lr, li, ldt, br, bi = disc_in
    group_sum = (lax.broadcasted_iota(jnp.int32, (N_GROUPS, N_GROUPS * N_STATE), 1) // N_STATE
                 == lax.broadcasted_iota(jnp.int32, (N_GROUPS, N_GROUPS * N_STATE), 0)).astype(F32)
    dlr, dli, dldt, dbr, dbi = _disc_bwd(lr, li, ldt, br, bi, cot["ab_r"], cot["ab_i"], cot["bb_r"], cot["bb_i"], group_sum)
    g = dict(cot)
    g.update(ssm_a_re=dlr, ssm_a_im=dli, ssm_log_dt=dldt[:, 0], ssm_b_re=dbr, ssm_b_im=dbi)
    return {n: g[n].reshape(p[n].shape) for n in SMALL_NAMES}


BIG = {
    "w_in": ((D_MODEL, PROJ_W), 1, PROJ_W // 4, 0, D_MODEL // 2),
    "glu_w": ((SSM_W, SSM_W), 0, SSM_W // 4, 1, SSM_W // 2),
    "w_out": ((D_MODEL, D_MODEL), 0, D_MODEL // 4, 1, D_MODEL // 2),
    "w_mlp_up": ((D_MODEL, D_FF), 1, D_FF // 4, 0, D_MODEL // 2),
    "w_mlp_down": ((D_FF, D_MODEL), 0, D_FF // 4, 1, D_MODEL // 2),
}
N_BIG = len(BIG_NAMES)
N_CHIPS = 4
ANY = pl.BlockSpec(memory_space=pl.ANY)


def _cut(name, shard=False, half=False):
    shape, s_ax, s_sz, h_ax, h_sz = BIG[name]
    shape = list(shape)
    if shard:
        shape[s_ax] = s_sz
    if half:
        shape[h_ax] = h_sz
    return tuple(shape)


def _window(name, base, shard=None, half=None):
    _, s_ax, s_sz, h_ax, h_sz = BIG[name]
    idx = [pl.ds(0, base[0]), pl.ds(0, base[1])]
    if shard is not None:
        idx[s_ax] = pl.ds(pl.multiple_of(shard * s_sz, s_sz), s_sz)
    if half is not None:
        idx[h_ax] = pl.ds(pl.multiple_of(half * h_sz, h_sz), h_sz)
    return tuple(idx)


def _mesh_pos():
    return lax.axis_index("x"), lax.axis_index("y"), lax.axis_index("c")


def _other_chips(x, y):
    return [(1 - x, y, 2 * (1 - x) + y), (x, 1 - y, 2 * x + 1 - y), (1 - x, 1 - y, 2 * (1 - x) + 1 - y)]


def _remote(src, dst, send_sem, recv_sem, dev):
    return pltpu.make_async_remote_copy(src_ref=src, dst_ref=dst, send_sem=send_sem, recv_sem=recv_sem,
                                        device_id=dev, device_id_type=MESH)


def _start_remote(src, dst, send_sem, recv_sem, dev):
    cp = _remote(src, dst, send_sem, recv_sem, dev)
    cp.start()
    return cp


class _Gather:
    def __init__(self, names):
        self.names = tuple(names)
        self.n = len(self.names)

    def in_specs(self):
        return [pl.BlockSpec(memory_space=pltpu.VMEM)] * self.n

    def out_specs(self):
        return [ANY] * self.n

    def out_shapes(self):
        return [jax.ShapeDtypeStruct(BIG[w][0], BF16) for w in self.names]

    def scratch_shapes(self):
        n_sem = (N_CHIPS - 1) * self.n
        return ([pltpu.VMEM(_cut(w, shard=True), BF16) for w in self.names]
                + [pltpu.SemaphoreType.DMA((n_sem,))] * 4 + [pltpu.SemaphoreType.DMA((self.n,))])

    def bind(self, ins, outs, scratch):
        self.ins, self.outs = ins, outs
        self.stage = scratch[:self.n]
        self.send, self.recv, self.fsend, self.frecv, self.lsem = scratch[self.n:]

    def _copies(self):
        x, y, c = _mesh_pos()
        me = 2 * x + y
        sib = (x, y, 1 - c)
        local, sends, lands, fwds, flands = [], [], [], [], []
        for w, n in enumerate(self.names):
            local.append(pltpu.make_async_copy(self.stage[w], self.outs[w].at[_window(n, BIG[n][0], shard=me)], self.lsem.at[w]))
        for k, (px, py, pj) in enumerate(_other_chips(x, y)):
            for w, n in enumerate(self.names):
                s = k * self.n + w
                sends.append(_remote(self.stage[w].at[_window(n, _cut(n, shard=True), half=c)],
                                     self.outs[w].at[_window(n, BIG[n][0], shard=me, half=c)],
                                     self.send.at[s], self.recv.at[s], (px, py, c)))
                got = self.outs[w].at[_window(n, BIG[n][0], shard=pj, half=c)]
                lands.append(_remote(got, got, self.send.at[s], self.recv.at[s], (px, py, c)))
                fwds.append(_remote(got, got, self.fsend.at[s], self.frecv.at[s], sib))
                theirs = self.outs[w].at[_window(n, BIG[n][0], shard=pj, half=1 - c)]
                flands.append(_remote(theirs, theirs, self.fsend.at[s], self.frecv.at[s], sib))
        return local, sends, lands, fwds, flands

    def start(self):
        for w in range(self.n):
            self.stage[w][...] = self.ins[w][...].astype(BF16)
        local, sends, _, _, _ = self._copies()
        for cp in local + sends:
            cp.start()

    def forward(self):
        _, _, lands, fwds, _ = self._copies()
        for land, fwd in zip(lands, fwds):
            land.wait_recv()
            fwd.start()

    def finish(self):
        local, sends, _, fwds, flands = self._copies()
        for cp in flands:
            cp.wait_recv()
        for cp in sends + fwds:
            cp.wait_send()
        for cp in local:
            cp.wait()


def _gather_weights(shards, names):
    g = _Gather(names)

    def body(*refs):
        g.bind(refs[0:g.n], refs[g.n:2 * g.n], refs[2 * g.n:])
        g.start()
        g.forward()
        g.finish()

    return pl.pallas_call(
        body, name="gather_" + "_".join(names), in_specs=g.in_specs(), out_specs=g.out_specs(), out_shape=g.out_shapes(),
        scratch_shapes=g.scratch_shapes(), compiler_params=pltpu.CompilerParams(vmem_limit_bytes=VMEM_LIMIT),
    )(*[shards[n] for n in names])


def _pair_exchange(grads, names, packed=None):
    n_big = len(names)
    n_all = n_big + (packed is not None)

    def body(*refs):
        ins, got = refs[0:n_all], refs[n_all:2 * n_all]
        send, recv = refs[2 * n_all:]
        x, y, c = _mesh_pos()
        sib = (x, y, 1 - c)
        copies = []
        for w, n in enumerate(names):
            copies.append(_start_remote(ins[w].at[_window(n, BIG[n][0], half=1 - c)], got[w], send.at[w], recv.at[w], sib))
        if packed is not None:
            copies.append(_start_remote(ins[n_big], got[n_big], send.at[n_big], recv.at[n_big], sib))
        for cp in copies:
            cp.wait()

    shapes = [jax.ShapeDtypeStruct(_cut(n, half=True), F32) for n in names]
    args = [grads[n] for n in names]
    if packed is not None:
        shapes.append(jax.ShapeDtypeStruct(packed.shape, F32))
        args.append(packed)
    return pl.pallas_call(
        body, name="grad_pair_exchange_" + "_".join(names), in_specs=[ANY] * n_all, out_specs=[ANY] * n_all, out_shape=shapes,
        scratch_shapes=[pltpu.SemaphoreType.DMA((n_all,)), pltpu.SemaphoreType.DMA((n_all,))],
    )(*args)


def _pair_sum(name, full, got, core):
    _, _, _, h_ax, _ = BIG[name]
    rows, cols = _cut(name, half=True)
    br = _row_block(rows, 512)
    nb = rows // br
    own_map = (lambda i, c: (i + c[0] * nb, 0)) if h_ax == 0 else (lambda i, c: (i, c[0]))

    def body(c_ref, own_ref, got_ref, o_ref):
        o_ref[...] = (own_ref[...] + got_ref[...]).astype(BF16)

    plain = pl.BlockSpec((br, cols), lambda i, c: (i, 0))
    return pl.pallas_call(
        body, name="pair_sum_" + name,
        grid_spec=pltpu.PrefetchScalarGridSpec(num_scalar_prefetch=1, grid=(nb,),
                                               in_specs=[pl.BlockSpec((br, cols), own_map), plain], out_specs=plain),
        out_shape=jax.ShapeDtypeStruct((rows, cols), BF16), compiler_params=_params("arbitrary"),
    )(core, full, got)


class _ChipExchange:
    def __init__(self, names, packed_shape=None):
        self.names = tuple(names)
        self.packed_shape = packed_shape
        self.n = len(self.names) + (packed_shape is not None)

    def in_specs(self):
        return [ANY] * self.n

    def out_specs(self):
        return [ANY] * self.n

    def out_shapes(self):
        shapes = [jax.ShapeDtypeStruct((N_CHIPS,) + _cut(w, shard=True, half=True), BF16) for w in self.names]
        if self.packed_shape is not None:
            shapes.append(jax.ShapeDtypeStruct((N_CHIPS,) + tuple(self.packed_shape), F32))
        return shapes

    def scratch_shapes(self):
        n_sem = (N_CHIPS - 1) * self.n
        return [pltpu.SemaphoreType.DMA((n_sem,)), pltpu.SemaphoreType.DMA((n_sem,))]

    def bind(self, ins, outs, scratch):
        self.ins, self.outs = ins, outs
        self.send, self.recv = scratch

    def _piece(self, w, shard):
        if w >= len(self.names):
            return self.ins[w]
        n = self.names[w]
        return self.ins[w].at[_window(n, _cut(n, half=True), shard=shard)]

    def _copies(self):
        x, y, c = _mesh_pos()
        me = 2 * x + y
        sends, lands = [], []
        for k, (px, py, pj) in enumerate(_other_chips(x, y)):
            for w in range(self.n):
                s = k * self.n + w
                sends.append(_remote(self._piece(w, pj), self.outs[w].at[me], self.send.at[s], self.recv.at[s], (px, py, c)))
                lands.append(_remote(self._piece(w, me), self.outs[w].at[pj], self.send.at[s], self.recv.at[s], (px, py, c)))
        return sends, lands

    def start(self):
        for cp in self._copies()[0]:
            cp.start()

    def finish(self):
        sends, lands = self._copies()
        for cp in lands:
            cp.wait_recv()
        for cp in sends:
            cp.wait_send()


def _chip_exchange(halves, packed, names):
    ex = _ChipExchange(names, packed.shape)

    def body(*refs):
        ex.bind(refs[0:ex.n], refs[ex.n:2 * ex.n], refs[2 * ex.n:])
        ex.start()
        ex.finish()

    return pl.pallas_call(
        body, name="grad_chip_exchange", in_specs=ex.in_specs(), out_specs=ex.out_specs(), out_shape=ex.out_shapes(),
        scratch_shapes=ex.scratch_shapes(),
    )(*halves, packed)


def _chip_sum(name, own, slots, chip):
    n_slot, rows, cols = slots.shape
    br = _row_block(rows, 512)
    nb = rows // br
    if name in BIG and BIG[name][1] == 1:
        own_map = lambda i, m: (i, m[0])
    elif name in BIG:
        own_map = lambda i, m: (i + m[0] * nb, 0)
    else:
        own_map = lambda i, m: (i, 0)

    def slot_map(j):
        return lambda i, m: (jnp.where(m[0] == j, (j + 1) % n_slot, j), i, 0)

    def body(m_ref, own_ref, *refs):
        own_blk = own_ref[...].astype(F32)
        tot = None
        for j in range(n_slot):
            term = jnp.where(m_ref[0] == j, own_blk, refs[j][...].astype(F32))
            tot = term if tot is None else tot + term
        refs[n_slot][...] = tot

    in_specs = [pl.BlockSpec((br, cols), own_map)] + [pl.BlockSpec((None, br, cols), slot_map(j)) for j in range(n_slot)]
    return pl.pallas_call(
        body, name="chip_sum_" + name,
        grid_spec=pltpu.PrefetchScalarGridSpec(num_scalar_prefetch=1, grid=(nb,), in_specs=in_specs,
                                               out_specs=pl.BlockSpec((br, cols), lambda i, m: (i, 0))),
        out_shape=jax.ShapeDtypeStruct((rows, cols), F32), compiler_params=_params("arbitrary"),
    )(chip, own, *([slots] * n_slot))


def _half_exchange(pieces):
    def body(*refs):
        ins, outs = refs[0:N_BIG], refs[N_BIG:2 * N_BIG]
        send, recv = refs[2 * N_BIG:]
        x, y, c = _mesh_pos()
        sib = (x, y, 1 - c)
        copies = []
        for w, n in enumerate(BIG_NAMES):
            copies.append(_start_remote(ins[w], outs[w], send.at[w], recv.at[w], sib))
        for cp in copies:
            cp.wait()

    return pl.pallas_call(
        body, name="grad_half_exchange", in_specs=[ANY] * N_BIG, out_specs=[ANY] * N_BIG,
        out_shape=[jax.ShapeDtypeStruct(_cut(n, shard=True, half=True), F32) for n in BIG_NAMES],
        scratch_shapes=[pltpu.SemaphoreType.DMA((N_BIG,)), pltpu.SemaphoreType.DMA((N_BIG,))],
    )(*pieces)


WEIGHT_NAMES = ("norm1_g", "w_in", "q_norm_g", "k_norm_g", "ssm_a_re", "ssm_a_im", "ssm_log_dt", "ssm_b_re", "ssm_b_im",
                "ssm_c_re", "ssm_c_im", "ssm_d", "glu_w", "glu_b", "attn_out_norm_g", "ssm_out_norm_g", "w_out", "norm2_g",
                "w_mlp_up", "w_mlp_down")


def _train_step(a):
    x = a["x"][0]
    target = a["loss_target"][0]
    shards = {n: a[n][0] for n in BIG_NAMES}
    p = {n: a[n][0] for n in SMALL_NAMES}
    core = lax.axis_index("c").astype(jnp.int32).reshape(1)
    chip_id = (2 * lax.axis_index("x") + lax.axis_index("y")).astype(jnp.int32).reshape(1)

    later = ("glu_w", "w_out", "w_mlp_up", "w_mlp_down")
    early = ("w_mlp_up", "w_mlp_down", "w_out", "glu_w")
    late = ("w_in",)
    (wi,) = _gather_weights(shards, ("w_in",))
    chip = {}

    def bwd_side(grads):
        got = _pair_exchange(grads, early)
        for n, g in zip(early, got):
            chip[n] = _pair_sum(n, grads[n], g, core)
        return _ChipExchange(early), [chip[n] for n in early]

    loss, grad_x, big, cot, disc_in, early_slots = _local_step(
        x, target, wi, None, p, fwd_side=(_Gather(later), [shards[n] for n in later]), bwd_side=bwd_side)
    slots = dict(zip(early, early_slots))

    cot_list = [cot[n] for n in COT_NAMES]
    packed = _pack(cot_list)
    *got, got_packed = _pair_exchange(big, late, packed)
    for n, g in zip(late, got):
        chip[n] = _pair_sum(n, big[n], g, core)
    chip_packed = _sum_arrays([packed, got_packed], "pair_sum_small")
    *late_slots, small_slots = _chip_exchange([chip[n] for n in late], chip_packed, late)
    slots.update(zip(late, late_slots))
    pieces = [_chip_sum(n, chip[n], slots[n], chip_id) for n in BIG_NAMES]
    small_sum = _chip_sum("small", chip_packed, small_slots, chip_id)
    shard_grads = {}
    for n, mine, theirs in zip(BIG_NAMES, pieces, _half_exchange(pieces)):
        h_ax = BIG[n][3]
        shard_grads[n] = jnp.where(core[0] == 0, jnp.concatenate([mine, theirs], axis=h_ax),
                                   jnp.concatenate([theirs, mine], axis=h_ax))
    small_grads = _small_grads(dict(zip(COT_NAMES, _unpack(small_sum, cot_list))), disc_in, p)

    grads, delta, new_m, new_v = {}, {}, {}, {}
    for n in BIG_NAMES:
        grads[n] = shard_grads[n]
        delta[n], new_m[n], new_v[n] = _adamw(a[n][0], grads[n], a["m_" + n][0], a["v_" + n][0], "adamw_" + n)
    flat2 = lambda t: t.reshape(-1, t.shape[-1])
    res = _adamw_many([flat2(p[n]) for n in SMALL_NAMES], [flat2(small_grads[n]) for n in SMALL_NAMES],
                      [flat2(a["m_" + n][0]) for n in SMALL_NAMES], [flat2(a["v_" + n][0]) for n in SMALL_NAMES])
    for store, outs in zip((delta, new_m, new_v), res):
        store.update(zip(SMALL_NAMES, outs))
    grads.update(small_grads)

    total = lax.psum(loss, ("x", "y", "c"))
    out = [total, grad_x[None]]
    for store in (grads, delta, new_m, new_v):
        out += [store[n].reshape(a[n].shape) for n in WEIGHT_NAMES]
    return tuple(out)


def kernel(x, norm1_g, w_in, q_norm_g, k_norm_g, ssm_a_re, ssm_a_im, ssm_log_dt, ssm_b_re, ssm_b_im, ssm_c_re, ssm_c_im, ssm_d, glu_w, glu_b, attn_out_norm_g, ssm_out_norm_g, w_out, norm2_g, w_mlp_up, w_mlp_down, loss_target, m_norm1_g, m_w_in, m_q_norm_g, m_k_norm_g, m_ssm_a_re, m_ssm_a_im, m_ssm_log_dt, m_ssm_b_re, m_ssm_b_im, m_ssm_c_re, m_ssm_c_im, m_ssm_d, m_glu_w, m_glu_b, m_attn_out_norm_g, m_ssm_out_norm_g, m_w_out, m_norm2_g, m_w_mlp_up, m_w_mlp_down, v_norm1_g, v_w_in, v_q_norm_g, v_k_norm_g, v_ssm_a_re, v_ssm_a_im, v_ssm_log_dt, v_ssm_b_re, v_ssm_b_im, v_ssm_c_re, v_ssm_c_im, v_ssm_d, v_glu_w, v_glu_b, v_attn_out_norm_g, v_ssm_out_norm_g, v_w_out, v_norm2_g, v_w_mlp_up, v_w_mlp_down):
    return _train_step(dict(locals()))
```

```python
import functools
import math

import jax
import jax.numpy as jnp
from jax import lax
from jax.experimental import pallas as pl
from jax.experimental.pallas import tpu as pltpu

F32 = jnp.float32
BF16 = jnp.bfloat16
MESH = pl.DeviceIdType.MESH

D_MODEL = 1024
ATTN_W = 512
SSM_W = 512
HEAD = 64
D_FF = 4096
PROJ_W = 2048
N_GROUPS = 32
N_STATE = 64
GROUP_W = 16
EPS = 1e-6
NEG = -1e30
DILATIONS = (1, 4, 16)
BLK = 128
TILE = 2048
LANES = 128
N_LB = SSM_W // LANES
VMEM_LIMIT = 56 * 1024 * 1024

ADAM_LR, ADAM_B1, ADAM_B2, ADAM_EPS, ADAM_WD, ADAM_STEP = 0.001, 0.9, 0.999, 1e-08, 0.01, 10


def _params(*sem):
    return pltpu.CompilerParams(dimension_semantics=sem, vmem_limit_bytes=VMEM_LIMIT)


def _nt(a, b):
    return lax.dot_general(a, b, (((1,), (1,)), ((), ())), preferred_element_type=F32)


def _tn(a, b):
    return lax.dot_general(a, b, (((0,), (0,)), ((), ())), preferred_element_type=F32)


def _mm(a, b):
    return jnp.dot(a, b, preferred_element_type=F32)


def _group_mean(t, ones_bd, width):
    hi = t.astype(BF16)
    lo = (t - hi.astype(F32)).astype(BF16)
    return (_mm(hi, ones_bd) + _mm(lo, ones_bd)) * (1.0 / width)


def _rms(x):
    return lax.rsqrt(jnp.mean(x * x, axis=-1, keepdims=True) + EPS)


def _rms_bwd(dy, x, r, g):
    xh = x * r
    dxh = dy * g
    dx = r * (dxh - xh * jnp.mean(dxh * xh, axis=-1, keepdims=True))
    return dx, dy * xh


def _colsum(x):
    return jnp.sum(x, axis=0, keepdims=True)


def _row_block(rows, cap):
    for b in range(min(rows, cap) // 8 * 8, 0, -8):
        if rows % b == 0:
            return b
    raise ValueError(f"no row block for {rows} rows")


def _inproj_fwd(x, g1, wi, gq, gk, ones64):
    t_len = x.shape[0]
    tm = 512
    n_hp = ATTN_W // LANES

    def body(x_ref, g1_ref, wi_ref, gq_ref, gk_ref, bd_ref, xn_ref, q_ref, k_ref, v_ref, u_ref, qr_ref, kr_ref):
        xv = x_ref[...]
        xn = (xv * _rms(xv) * g1_ref[...]).astype(BF16)
        xn_ref[...] = xn
        proj = _mm(xn, wi_ref[...])
        q = proj[:, 0:ATTN_W]
        k = proj[:, ATTN_W:2 * ATTN_W]
        v = proj[:, 2 * ATTN_W:3 * ATTN_W]
        u_ref[...] = proj[:, 3 * ATTN_W:]
        qr_ref[...] = q
        kr_ref[...] = k
        bd = bd_ref[...]
        qn = q * lax.rsqrt(_group_mean(q * q, bd, HEAD) + EPS) * gq_ref[...] * (HEAD ** -0.5)
        kn = k * lax.rsqrt(_group_mean(k * k, bd, HEAD) + EPS) * gk_ref[...]
        for hp in range(n_hp):
            sl = slice(hp * LANES, (hp + 1) * LANES)
            q_ref[hp] = qn[:, sl]
            k_ref[hp] = kn[:, sl]
            v_ref[hp] = v[:, sl]

    row = lambda i: (i, 0)
    const = lambda i: (0, 0)
    hp_spec = pl.BlockSpec((n_hp, tm, LANES), lambda i: (0, i, 0))
    hp_shape = jax.ShapeDtypeStruct((n_hp, t_len, LANES), F32)
    return pl.pallas_call(
        body, name="inproj_fwd", grid=(t_len // tm,),
        in_specs=[pl.BlockSpec((tm, D_MODEL), row), pl.BlockSpec((1, D_MODEL), const),
                  pl.BlockSpec((D_MODEL, PROJ_W), const), pl.BlockSpec((1, ATTN_W), const),
                  pl.BlockSpec((1, ATTN_W), const), pl.BlockSpec((ATTN_W, ATTN_W), const)],
        out_specs=[pl.BlockSpec((tm, D_MODEL), row), hp_spec, hp_spec, hp_spec,
                   pl.BlockSpec((tm, SSM_W), row), pl.BlockSpec((tm, ATTN_W), row), pl.BlockSpec((tm, ATTN_W), row)],
        out_shape=[jax.ShapeDtypeStruct((t_len, D_MODEL), BF16), hp_shape, hp_shape, hp_shape,
                   jax.ShapeDtypeStruct((t_len, SSM_W), F32), jax.ShapeDtypeStruct((t_len, ATTN_W), F32),
                   jax.ShapeDtypeStruct((t_len, ATTN_W), F32)],
        compiler_params=_params("arbitrary"),
    )(x, g1, wi, gq, gk, ones64)


def _attn_masks():
    head0 = lax.broadcasted_iota(jnp.int32, (BLK, LANES), 1) < HEAD
    row = lax.broadcasted_iota(jnp.int32, (2 * BLK, 2 * BLK), 0) & (BLK - 1)
    col = lax.broadcasted_iota(jnp.int32, (2 * BLK, 2 * BLK), 1)
    return head0, (col < BLK) & (col >= row), (col >= BLK) & (col - BLK <= row)


def _stack_heads(x, head0):
    return jnp.concatenate([jnp.where(head0, x, 0.0), jnp.where(head0, 0.0, x)], axis=0).astype(BF16)


def _unit_rows(uidx, d):
    nb = TILE // (BLK * d)
    r = lax.div(uidx, nb)
    b = lax.rem(uidx, nb)
    start = r + d * BLK * b
    if d == 1:
        start = pl.multiple_of(start, BLK)
        mk = lambda s: pl.ds(pl.multiple_of(s, BLK), BLK)
    else:
        mk = lambda s: pl.ds(s, BLK, stride=d)
    return b, mk(start), mk(TILE + start), mk(TILE + start - d * BLK)


def _attn_fwd(q, k, v, side=None, side_args=()):
    n_hp, t_len, _ = q.shape
    nt = t_len // TILE
    ns = side.n if side is not None else 0
    n_steps = n_hp * nt

    def body(*refs):
        q_ref, kp_ref, kc_ref, vp_ref, vc_ref = refs[0:5]
        o_ref, lse_ref = refs[5 + ns:7 + ns]
        kk, vv, m_s, l_s, acc_s = refs[7 + 2 * ns:12 + 2 * ns]
        t = pl.program_id(1)
        step = pl.program_id(0) * nt + t
        if side is not None:
            side.bind(refs[5:5 + ns], refs[7 + ns:7 + 2 * ns], refs[12 + 2 * ns:])
            pl.when(step == 0)(side.start)
            pl.when(step == n_steps // 2)(side.forward)
        kk[0:TILE] = kp_ref[0]
        kk[TILE:] = kc_ref[0]
        vv[0:TILE] = vp_ref[0]
        vv[TILE:] = vc_ref[0]
        head0, band_prev, band_cur = _attn_masks()

        for pi, d in enumerate(DILATIONS):
            def unit(uidx, carry, d=d, pi=pi):
                b, rows_q, rows_c, rows_p = _unit_rows(uidx, d)
                mask = band_cur | (band_prev & ((t > 0) | (b > 0)))
                q2 = _stack_heads(q_ref.at[0][rows_q, :], head0)
                kcat = jnp.concatenate([kk[rows_p, :], kk[rows_c, :]], axis=0).astype(BF16)
                vcat = jnp.concatenate([vv[rows_p, :], vv[rows_c, :]], axis=0).astype(BF16)
                s = jnp.where(mask, _nt(q2, kcat), NEG)
                m = jnp.max(s, axis=1, keepdims=True)
                p = jnp.exp(s - m)
                ls = jnp.sum(p, axis=1, keepdims=True)
                pv = _mm(p.astype(BF16), vcat)
                m_s.at[pi][rows_q, :] = jnp.where(head0, m[0:BLK], m[BLK:])
                l_s.at[pi][rows_q, :] = jnp.where(head0, ls[0:BLK], ls[BLK:])
                acc_s.at[pi][rows_q, :] = jnp.where(head0, pv[0:BLK], pv[BLK:])
                return carry

            lax.fori_loop(0, TILE // BLK, unit, 0, unroll=8)

        m_all = jnp.maximum(jnp.maximum(m_s[0], m_s[1]), m_s[2])
        num = jnp.zeros((TILE, LANES), F32)
        den = jnp.zeros((TILE, LANES), F32)
        for pi in range(len(DILATIONS)):
            wgt = jnp.exp(m_s[pi] - m_all)
            num = num + acc_s[pi] * wgt
            den = den + l_s[pi] * wgt
        o_ref[...] = num / den
        lse_ref[0] = m_all + jnp.log(den)
        if side is not None:
            pl.when(step == n_steps - 1)(side.finish)

    cur = lambda hp, t: (hp, t, 0)
    prev = lambda hp, t: (hp, jnp.maximum(t - 1, 0), 0)
    blk = (1, TILE, LANES)
    per_pattern = pltpu.VMEM((len(DILATIONS), TILE, LANES), F32)
    extra = (side.in_specs(), side.out_specs(), side.out_shapes(), side.scratch_shapes()) if side is not None else ([], [], [], [])
    return pl.pallas_call(
        body, name="attn_fwd", grid=(n_hp, nt),
        in_specs=[pl.BlockSpec(blk, cur), pl.BlockSpec(blk, prev), pl.BlockSpec(blk, cur),
                  pl.BlockSpec(blk, prev), pl.BlockSpec(blk, cur)] + extra[0],
        out_specs=[pl.BlockSpec((TILE, LANES), lambda hp, t: (t, hp)), pl.BlockSpec(blk, cur)] + extra[1],
        out_shape=[jax.ShapeDtypeStruct((t_len, ATTN_W), F32), jax.ShapeDtypeStruct((n_hp, t_len, LANES), F32)] + extra[2],
        scratch_shapes=[pltpu.VMEM((2 * TILE, LANES), F32), pltpu.VMEM((2 * TILE, LANES), F32),
                        per_pattern, per_pattern, per_pattern] + extra[3],
        compiler_params=_params("arbitrary", "arbitrary"),
    )(q, k, k, v, v, *side_args)


def _attn_bwd(q, k, v, o, do, lse, ones_hp, side=None, side_args=()):
    n_hp, t_len, _ = q.shape
    nt = t_len // TILE
    ns = side.n if side is not None else 0
    n_pat = len(DILATIONS)

    def body(*refs):
        q_ref, kp_ref, kc_ref, vp_ref, vc_ref, o_ref, do_ref, lse_ref, bd_ref = refs[0:9]
        dq_ref, dk_ref, dv_ref = refs[9 + ns:12 + ns]
        kk, vv, dq_s, dkc, dkp, dvc, dvp, hold_k, hold_v, dl_s = refs[12 + 2 * ns:22 + 2 * ns]
        t = pl.program_id(1)
        if side is not None:
            side.bind(refs[9:9 + ns], refs[12 + ns:12 + 2 * ns], refs[22 + 2 * ns:])
            pl.when((pl.program_id(0) == 0) & (t == 0))(side.start)

        @pl.when(t < nt)
        def _():
            kk[0:TILE] = kp_ref[0]
            kk[TILE:] = kc_ref[0]
            vv[0:TILE] = vp_ref[0]
            vv[TILE:] = vc_ref[0]
            dl_s[...] = _group_mean(do_ref[...] * o_ref[...], bd_ref[...], 1.0)
            head0, band_prev, band_cur = _attn_masks()

            for pi, d in enumerate(DILATIONS):
                def unit(uidx, carry, d=d, pi=pi):
                    b, rows_q, rows_c, rows_p = _unit_rows(uidx, d)
                    mask = band_cur | (band_prev & ((t > 0) | (b > 0)))
                    q2 = _stack_heads(q_ref.at[0][rows_q, :], head0)
                    do2 = _stack_heads(do_ref[rows_q, :], head0)
                    lse_f = lse_ref.at[0][rows_q, :]
                    dl_f = dl_s[rows_q, :]
                    lse2 = jnp.concatenate([lse_f[:, 0:1], lse_f[:, HEAD:HEAD + 1]], axis=0)
                    dl2 = jnp.concatenate([dl_f[:, 0:1], dl_f[:, HEAD:HEAD + 1]], axis=0)
                    kcat = jnp.concatenate([kk[rows_p, :], kk[rows_c, :]], axis=0).astype(BF16)
                    vcat = jnp.concatenate([vv[rows_p, :], vv[rows_c, :]], axis=0).astype(BF16)
                    p = jnp.where(mask, jnp.exp(_nt(q2, kcat) - lse2), 0.0)
                    ds = (p * (_nt(do2, vcat) - dl2)).astype(BF16)
                    dq2 = _mm(ds, kcat)
                    dq_s.at[pi][rows_q, :] = jnp.where(head0, dq2[0:BLK], dq2[BLK:])
                    dk2 = _tn(ds, q2)
                    dv2 = _tn(p.astype(BF16), do2)
                    dkp.at[pi][rows_q, :] = dk2[0:BLK]
                    dkc.at[pi][rows_q, :] = dk2[BLK:]
                    dvp.at[pi][rows_q, :] = dv2[0:BLK]
                    dvc.at[pi][rows_q, :] = dv2[BLK:]
                    return carry

                lax.fori_loop(0, TILE // BLK, unit, 0, unroll=8)

            dq_ref[...] = dq_s[0] + dq_s[1] + dq_s[2]

        @pl.when(t > 0)
        def _():
            dk_ref[...] = hold_k[...]
            dv_ref[...] = hold_v[...]

        @pl.when((t > 0) & (t < nt))
        def _():
            for pi, d in enumerate(DILATIONS):
                back = d * BLK
                dk_ref[TILE - back:, :] = dk_ref[TILE - back:, :] + dkp[pi, 0:back, :]
                dv_ref[TILE - back:, :] = dv_ref[TILE - back:, :] + dvp[pi, 0:back, :]

        @pl.when(t < nt)
        def _():
            hold_k[...] = dkc[0] + dkc[1] + dkc[2]
            hold_v[...] = dvc[0] + dvc[1] + dvc[2]
            for pi, d in enumerate(DILATIONS):
                back = d * BLK
                if back < TILE:
                    hold_k[0:TILE - back, :] = hold_k[0:TILE - back, :] + dkp[pi, back:, :]
                    hold_v[0:TILE - back, :] = hold_v[0:TILE - back, :] + dvp[pi, back:, :]

        if side is not None:
            pl.when((pl.program_id(0) == n_hp - 1) & (t == nt))(side.finish)

    last = nt - 1
    extra = (side.in_specs(), side.out_specs(), side.out_shapes(), side.scratch_shapes()) if side is not None else ([], [], [], [])
    cur = lambda hp, t: (hp, jnp.minimum(t, last), 0)
    prev = lambda hp, t: (hp, jnp.clip(t - 1, 0, last), 0)
    cur2 = lambda hp, t: (jnp.minimum(t, last), hp)
    prev2 = lambda hp, t: (jnp.maximum(t - 1, 0), hp)
    blk = (1, TILE, LANES)
    blk2 = (TILE, LANES)
    out = jax.ShapeDtypeStruct((t_len, ATTN_W), F32)
    return pl.pallas_call(
        body, name="attn_bwd", grid=(n_hp, nt + 1),
        in_specs=[pl.BlockSpec(blk, cur), pl.BlockSpec(blk, prev), pl.BlockSpec(blk, cur),
                  pl.BlockSpec(blk, prev), pl.BlockSpec(blk, cur), pl.BlockSpec(blk2, cur2),
                  pl.BlockSpec(blk2, cur2), pl.BlockSpec(blk, cur), pl.BlockSpec((LANES, LANES), lambda hp, t: (0, 0))]
        + extra[0],
        out_specs=[pl.BlockSpec(blk2, cur2), pl.BlockSpec(blk2, prev2), pl.BlockSpec(blk2, prev2)] + extra[1],
        out_shape=[out, out, out] + extra[2],
        scratch_shapes=[pltpu.VMEM((2 * TILE, LANES), F32), pltpu.VMEM((2 * TILE, LANES), F32)]
        + [pltpu.VMEM((n_pat, TILE, LANES), F32)] * 5 + [pltpu.VMEM((TILE, LANES), F32)] * 3 + extra[3],
        compiler_params=_params("arbitrary", "arbitrary"),
    )(q, k, k, v, v, o, do, lse, ones_hp, *side_args)


def _discretise(lr, li, ldt, br, bi):
    dt = jnp.exp(ldt)
    mag = jnp.exp(lr * dt)
    ab_r, ab_i = mag * jnp.cos(li * dt), mag * jnp.sin(li * dt)
    den = lr * lr + li * li
    nr, ni = ab_r - 1.0, ab_i
    cr = (nr * lr + ni * li) / den
    ci = (ni * lr - nr * li) / den
    return ab_r, ab_i, cr * br - ci * bi, cr * bi + ci * br


def _disc_fwd(lr, li, ldt, br, bi):
    def body(lr_ref, li_ref, ldt_ref, br_ref, bi_ref, ar_o, ai_o, bbr_o, bbi_o):
        outs = _discretise(lr_ref[...], li_ref[...], ldt_ref[...], br_ref[...], bi_ref[...])
        for o_ref, val in zip((ar_o, ai_o, bbr_o, bbi_o), outs):
            o_ref[...] = val

    col = jax.ShapeDtypeStruct(lr.shape, F32)
    mat = jax.ShapeDtypeStruct(br.shape, F32)
    return pl.pallas_call(body, name="s5_disc_fwd", out_shape=[col, col, mat, mat])(lr, li, ldt, br, bi)


def _disc_bwd(lr, li, ldt, br, bi, d_ar, d_ai, d_bbr, d_bbi, group_sum):
    def body(lr_ref, li_ref, ldt_ref, br_ref, bi_ref, c1, c2, c3, c4, gs_ref, dlr_o, dli_o, dldt_o, dbr_o, dbi_o):
        _, vjp = jax.vjp(_discretise, lr_ref[...], li_ref[...], ldt_ref[...], br_ref[...], bi_ref[...])
        dlr, dli, dldt, dbr, dbi = vjp((c1[...], c2[...], c3[...], c4[...]))
        dlr_o[...] = dlr
        dli_o[...] = dli
        dbr_o[...] = dbr
        dbi_o[...] = dbi
        wide = jnp.broadcast_to(dldt, (dldt.shape[0], LANES))
        dldt_o[...] = jnp.dot(gs_ref[...], wide, precision=lax.Precision.HIGHEST, preferred_element_type=F32)

    col = jax.ShapeDtypeStruct(lr.shape, F32)
    mat = jax.ShapeDtypeStruct(br.shape, F32)
    return pl.pallas_call(
        body, name="s5_disc_bwd", out_shape=[col, col, jax.ShapeDtypeStruct((N_GROUPS, LANES), F32), mat, mat],
    )(lr, li, ldt, br, bi, d_ar, d_ai, d_bbr, d_bbi, group_sum)


N_CHUNK = TILE // BLK
HALF = 4


def _cmul(ar, ai, xr, xi):
    return ar * xr - ai * xi, ar * xi + ai * xr


def _power_table(a_ref, tab, sign, reverse):
    ar = [a_ref[0, j:j + 1, :] for j in range(HALF)]
    ai = [sign * a_ref[0, HALF + j:HALF + j + 1, :] for j in range(HALF)]

    def step(s, cur):
        row = pl.ds((BLK - 1 - s) if reverse else s, 1)
        nxt = []
        for j in range(HALF):
            tab.at[j][row, :] = cur[j]
            tab.at[HALF + j][row, :] = cur[HALF + j]
            nxt.append(_cmul(ar[j], ai[j], cur[j], cur[HALF + j]))
        return tuple(p[0] for p in nxt) + tuple(p[1] for p in nxt)

    lax.fori_loop(0, BLK, step, tuple(ar) + tuple(ai))


def _interleave(src, dst):
    for c in range(N_CHUNK):
        dst[pl.ds(c, BLK, stride=N_CHUNK), :] = src[c * BLK:(c + 1) * BLK, :]


def _deinterleave(src, dst):
    for c in range(N_CHUNK):
        dst[c * BLK:(c + 1) * BLK, :] = src[pl.ds(c, BLK, stride=N_CHUNK), :]


def _step_rows(s):
    return pl.ds(pl.multiple_of(s * N_CHUNK, N_CHUNK), N_CHUNK)


def _chunk_scan(buf, a_ref, sign, reverse):
    ar = [jnp.broadcast_to(a_ref[0, j:j + 1, :], (N_CHUNK, LANES)) for j in range(HALF)]
    ai = [sign * jnp.broadcast_to(a_ref[0, HALF + j:HALF + j + 1, :], (N_CHUNK, LANES)) for j in range(HALF)]

    def step(i, carry):
        s = (BLK - 1 - i) if reverse else i
        rows = _step_rows(s)
        out = []
        for j in range(HALF):
            pr, pi = _cmul(ar[j], ai[j], carry[j], carry[HALF + j])
            xr = buf.at[j][rows, :] + pr
            xi = buf.at[HALF + j][rows, :] + pi
            buf.at[j][rows, :] = xr
            buf.at[HALF + j][rows, :] = xi
            out.append((xr, xi))
        return tuple(p[0] for p in out) + tuple(p[1] for p in out)

    zero = jnp.zeros((N_CHUNK, LANES), F32)
    lax.fori_loop(0, BLK, step, (zero,) * (2 * HALF), unroll=2)


def _chunk_states(buf, carry_s, xin_s, tab, reverse):
    edge = 0 if reverse else BLK - 1
    top = 0 if reverse else BLK - 1
    pw = [tab[j, top:top + 1, :] for j in range(2 * HALF)]
    cur = [carry_s[j:j + 1, :] for j in range(2 * HALF)]
    summary = [buf[j, edge * N_CHUNK:(edge + 1) * N_CHUNK, :] for j in range(2 * HALF)]
    order = range(N_CHUNK - 1, -1, -1) if reverse else range(N_CHUNK)
    for c in order:
        for j in range(2 * HALF):
            xin_s[j, c:c + 1, :] = cur[j]
        nxt = []
        for j in range(HALF):
            pr, pi = _cmul(pw[j], pw[HALF + j], cur[j], cur[HALF + j])
            nxt.append((pr + summary[j][c:c + 1, :], pi + summary[HALF + j][c:c + 1, :]))
        cur = [p[0] for p in nxt] + [p[1] for p in nxt]
    for j in range(2 * HALF):
        carry_s[j:j + 1, :] = cur[j]


def _s5_fwd(u, a_cat, b_mat, c_mat, d_skip):
    t_len = u.shape[0]
    nt = t_len // TILE

    def body(u_ref, a_ref, b_ref, c_ref, d_ref, y_ref, x_ref, xs, us, tab, carry_s, xin_s):
        sb = pl.program_id(1)

        @pl.when(sb == 0)
        def _():
            _power_table(a_ref, tab, 1.0, False)
            carry_s[...] = jnp.zeros_like(carry_s)

        _interleave(u_ref, us)
        uv = us[...]
        bu = _mm(uv.astype(BF16), b_ref[0])
        for j in range(2 * HALF):
            xs[j] = bu[:, j * LANES:(j + 1) * LANES]
        _chunk_scan(xs, a_ref, 1.0, False)
        _chunk_states(xs, carry_s, xin_s, tab, False)
        xin = [xin_s[j] for j in range(2 * HALF)]

        def fix(s, acc):
            rows = _step_rows(s)
            for j in range(HALF):
                pr, pi = _cmul(tab.at[j][pl.ds(s, 1), :], tab.at[HALF + j][pl.ds(s, 1), :], xin[j], xin[HALF + j])
                xs.at[j][rows, :] = xs.at[j][rows, :] + pr
                xs.at[HALF + j][rows, :] = xs.at[HALF + j][rows, :] + pi
            return acc

        lax.fori_loop(0, BLK, fix, 0, unroll=2)
        xcat = jnp.concatenate([xs[j].astype(BF16) for j in range(2 * HALF)], axis=1)
        x_ref[0] = xcat
        us[...] = d_ref[0] * uv + _mm(xcat, c_ref[0])
        _deinterleave(us, y_ref)

    return pl.pallas_call(
        body, name="s5_fwd", grid=(N_LB, nt),
        in_specs=[pl.BlockSpec((TILE, LANES), lambda lb, sb: (sb, lb)),
                  pl.BlockSpec((1, 2 * HALF, LANES), lambda lb, sb: (lb, 0, 0)),
                  pl.BlockSpec((1, LANES, 2 * HALF * LANES), lambda lb, sb: (lb, 0, 0)),
                  pl.BlockSpec((1, 2 * HALF * LANES, LANES), lambda lb, sb: (lb, 0, 0)),
                  pl.BlockSpec((1, 1, LANES), lambda lb, sb: (lb, 0, 0))],
        out_specs=[pl.BlockSpec((TILE, LANES), lambda lb, sb: (sb, lb)),
                   pl.BlockSpec((1, TILE, 2 * HALF * LANES), lambda lb, sb: (lb, sb, 0))],
        out_shape=[jax.ShapeDtypeStruct((t_len, SSM_W), F32), jax.ShapeDtypeStruct((N_LB, t_len, 2 * HALF * LANES), BF16)],
        scratch_shapes=[pltpu.VMEM((2 * HALF, TILE, LANES), F32), pltpu.VMEM((TILE, LANES), F32),
                        pltpu.VMEM((2 * HALF, BLK, LANES), F32),
                        pltpu.VMEM((2 * HALF, LANES), F32), pltpu.VMEM((2 * HALF, N_CHUNK, LANES), F32)],
        compiler_params=_params("arbitrary", "arbitrary"),
    )(u, a_cat, b_mat, c_mat, d_skip)


def _s5_bwd(u, dy, states, a_cat, b_mat, c_mat, d_skip):
    t_len = u.shape[0]
    nt = t_len // TILE
    last = nt - 1

    def body(u_ref, dy_ref, x_ref, a_ref, b_ref, c_ref, d_ref, du_ref, db_ref, dc_ref, da_ref, dd_ref,
             gs, us, dys, tabc, lam_s, lin_s):
        sb = pl.program_id(1)

        @pl.when(sb == 0)
        def _():
            _power_table(a_ref, tabc, -1.0, True)
            lam_s[...] = jnp.zeros_like(lam_s)
            db_ref[...] = jnp.zeros_like(db_ref)
            dc_ref[...] = jnp.zeros_like(dc_ref)
            da_ref[...] = jnp.zeros_like(da_ref)
            dd_ref[...] = jnp.zeros_like(dd_ref)

        _interleave(u_ref, us)
        _interleave(dy_ref, dys)
        uv = us[...]
        dyv = dys[...]
        ub = uv.astype(BF16)
        dyb = dyv.astype(BF16)
        gy = _nt(dyb, c_ref[0])
        for j in range(2 * HALF):
            gs[j] = gy[:, j * LANES:(j + 1) * LANES]
        _chunk_scan(gs, a_ref, -1.0, True)
        _chunk_states(gs, lam_s, lin_s, tabc, True)
        zero = jnp.zeros((N_CHUNK, LANES), F32)
        x_tile = x_ref.at[0]
        for grp in range(0, HALF, 2):
            slabs = (grp, grp + 1)
            lin = [(lin_s[j], lin_s[HALF + j]) for j in slabs]

            def fix(i, carry, slabs=slabs, lin=lin):
                s = BLK - 1 - i
                rows = _step_rows(s)
                out = []
                for k, j in enumerate(slabs):
                    nr, ni, acc_r, acc_i = carry[4 * k:4 * k + 4]
                    xr = x_tile[rows, pl.ds(j * LANES, LANES)].astype(F32)
                    xi = x_tile[rows, pl.ds((HALF + j) * LANES, LANES)].astype(F32)
                    qr, qi = _cmul(tabc.at[j][pl.ds(s, 1), :], tabc.at[HALF + j][pl.ds(s, 1), :], lin[k][0], lin[k][1])
                    lr_ = gs.at[j][rows, :] + qr
                    li_ = gs.at[HALF + j][rows, :] + qi
                    gs.at[j][rows, :] = lr_
                    gs.at[HALF + j][rows, :] = li_
                    out += [lr_, li_, acc_r + (xr * nr + xi * ni), acc_i + (xr * ni - xi * nr)]
                return tuple(out)

            init = []
            for k in range(len(slabs)):
                init += [lin[k][0], lin[k][1], zero, zero]
            res = lax.fori_loop(0, BLK, fix, tuple(init), unroll=2)
            for k, j in enumerate(slabs):
                da_ref[0, j:j + 1, :] = da_ref[0, j:j + 1, :] + _colsum(res[4 * k + 2])
                da_ref[0, HALF + j:HALF + j + 1, :] = da_ref[0, HALF + j:HALF + j + 1, :] + _colsum(res[4 * k + 3])
        lam = jnp.concatenate([gs[j].astype(BF16) for j in range(2 * HALF)], axis=1)
        us[...] = _nt(lam, b_ref[0]) + d_ref[0] * dyv
        _deinterleave(us, du_ref)
        db_ref[0] = db_ref[0] + _tn(ub, lam)
        dc_ref[0] = dc_ref[0] + _tn(dyb, x_ref[0])
        dd_ref[0] = dd_ref[0] + _colsum(dyv * uv)

    rev = lambda lb, sb: (last - sb, lb)
    per_lb = lambda lb, sb: (lb, 0, 0)
    wide = 2 * HALF * LANES
    return pl.pallas_call(
        body, name="s5_bwd", grid=(N_LB, nt),
        in_specs=[pl.BlockSpec((TILE, LANES), rev), pl.BlockSpec((TILE, LANES), rev),
                  pl.BlockSpec((1, TILE, wide), lambda lb, sb: (lb, last - sb, 0)),
                  pl.BlockSpec((1, 2 * HALF, LANES), per_lb), pl.BlockSpec((1, LANES, wide), per_lb),
                  pl.BlockSpec((1, wide, LANES), per_lb), pl.BlockSpec((1, 1, LANES), per_lb)],
        out_specs=[pl.BlockSpec((TILE, LANES), rev), pl.BlockSpec((1, LANES, wide), per_lb),
                   pl.BlockSpec((1, LANES, wide), per_lb), pl.BlockSpec((1, 2 * HALF, LANES), per_lb),
                   pl.BlockSpec((1, 1, LANES), per_lb)],
        out_shape=[jax.ShapeDtypeStruct((t_len, SSM_W), F32), jax.ShapeDtypeStruct((N_LB, LANES, wide), F32),
                   jax.ShapeDtypeStruct((N_LB, LANES, wide), F32), jax.ShapeDtypeStruct((N_LB, 2 * HALF, LANES), F32),
                   jax.ShapeDtypeStruct((N_LB, 1, LANES), F32)],
        scratch_shapes=[pltpu.VMEM((2 * HALF, TILE, LANES), F32),
                        pltpu.VMEM((TILE, LANES), F32), pltpu.VMEM((TILE, LANES), F32),
                        pltpu.VMEM((2 * HALF, BLK, LANES), F32), pltpu.VMEM((2 * HALF, LANES), F32),
                        pltpu.VMEM((2 * HALF, N_CHUNK, LANES), F32)],
        compiler_params=_params("arbitrary", "arbitrary"),
    )(u, dy, states, a_cat, b_mat, c_mat, d_skip)


_GELU_C = math.sqrt(2.0 / math.pi)
_GELU_K = 0.044715


def _gelu(y):
    t = jnp.tanh(_GELU_C * (y + _GELU_K * (y * y * y)))
    return y * (0.5 * (1.0 + t)), t


def _gelu_grad(y, t):
    return 0.5 * (1.0 + t) + 0.5 * y * (1.0 - t * t) * (_GELU_C * (1.0 + 3.0 * _GELU_K * y * y))


def _glu(y, wg, bias):
    z, t = _gelu(y)
    sg = jax.nn.sigmoid(_mm(z.astype(BF16), wg) + bias)
    return z, t, sg


def _mixer_mlp(attn, y, x, target, wg, glu_b, ga, gs, wo, g2, wu, wd):
    t_len = x.shape[0]
    tm = 256
    fc = 1024
    n_fc = D_FF // fc

    def body(attn_ref, y_ref, x_ref, tg_ref, b_ref, ga_ref, gs_ref, g2_ref, wg_s, wo_s, wu_hbm, wd_hbm,
             dx2_ref, hdn_ref, dup_ref, h_ref, dyb_ref, mix_ref, z_ref, dattn_ref, dy_ref, dx2b_ref, dgp_ref,
             dg2_ref, loss_ref, dga_ref, dgs_ref, db_ref, wu_s, wd_s, relu_s, sem):
        @pl.when(pl.program_id(0) == 0)
        def _():
            copies = [pltpu.make_async_copy(src, dst, sem.at[k]) for k, (src, dst) in enumerate(((wu_hbm, wu_s), (wd_hbm, wd_s)))]
            for cp in copies:
                cp.start()
            for cp in copies:
                cp.wait()
            for acc in (dg2_ref, loss_ref, dga_ref, dgs_ref, db_ref):
                acc[...] = jnp.zeros_like(acc)

        av = attn_ref[...]
        z, _, sg = _glu(y_ref[...], wg_s[...], b_ref[...])
        z_ref[...] = z.astype(BF16)
        s = z * sg
        an = (av * _rms(av) * ga_ref[...]).astype(BF16)
        sn = (s * _rms(s) * gs_ref[...]).astype(BF16)
        mix_ref[:, 0:ATTN_W] = an
        mix_ref[:, ATTN_W:] = sn
        dx2_ref[...] = x_ref[...] + _mm(an, wo_s[0:ATTN_W, :]) + _mm(sn, wo_s[ATTN_W:, :])
        r = _rms(dx2_ref[...])
        g2v = g2_ref[...]
        h = (dx2_ref[...] * r * g2v).astype(BF16)
        h_ref[...] = h
        yout = dx2_ref[...]
        for c in range(n_fc):
            cols = slice(c * fc, (c + 1) * fc)
            ru = jnp.maximum(_mm(h, wu_s[:, cols]), 0.0)
            relu_s[:, cols] = ru
            hd = (ru * ru).astype(BF16)
            hdn_ref[:, cols] = hd
            yout = yout + _mm(hd, wd_s[cols, :])
        err = yout - tg_ref[...]
        loss_ref[...] = loss_ref[...] + 0.5 * jnp.sum(err * err) * (1.0 / D_MODEL)
        dy = err * (1.0 / D_MODEL)
        dyb = dy.astype(BF16)
        dyb_ref[...] = dyb
        dh = jnp.zeros((tm, D_MODEL), F32)
        for c in range(n_fc):
            cols = slice(c * fc, (c + 1) * fc)
            dup = (_nt(dyb, wd_s[cols, :]) * (2.0 * relu_s[:, cols])).astype(BF16)
            dup_ref[:, cols] = dup
            dh = dh + _nt(dup, wu_s[:, cols])
        dxn, g2_term = _rms_bwd(dh, dx2_ref[...], r, g2v)
        dx2_ref[...] = dy + dxn
        dg2_ref[...] = dg2_ref[...] + _colsum(g2_term)
        dx2b = dx2_ref[...].astype(BF16)
        dx2b_ref[...] = dx2b
        yv = y_ref[...]
        av = attn_ref[...]
        z, t, sg = _glu(yv, wg_s[...], b_ref[...])
        s = z * sg
        d_attn, ga_term = _rms_bwd(_nt(dx2b, wo_s[0:ATTN_W, :]), av, _rms(av), ga_ref[...])
        d_s, gs_term = _rms_bwd(_nt(dx2b, wo_s[ATTN_W:, :]), s, _rms(s), gs_ref[...])
        dattn_ref[...] = d_attn
        dgp = d_s * z * sg * (1.0 - sg)
        dgpb = dgp.astype(BF16)
        dgp_ref[...] = dgpb
        dy_ref[...] = (d_s * sg + _nt(dgpb, wg_s[...])) * _gelu_grad(yv, t)
        dga_ref[...] = dga_ref[...] + _colsum(ga_term)
        dgs_ref[...] = dgs_ref[...] + _colsum(gs_term)
        db_ref[...] = db_ref[...] + _colsum(dgp)

    row = lambda i: (i, 0)
    const = lambda i: (0, 0)
    wide = lambda n: pl.BlockSpec((tm, n), row)
    vec = lambda n: pl.BlockSpec((1, n), const)
    any_spec = pl.BlockSpec(memory_space=pl.ANY)
    f32 = lambda n: jax.ShapeDtypeStruct((t_len, n), F32)
    b16 = lambda n: jax.ShapeDtypeStruct((t_len, n), BF16)
    acc = lambda n: jax.ShapeDtypeStruct((1, n), F32)
    return pl.pallas_call(
        body, name="mixer_mlp", grid=(t_len // tm,),
        in_specs=[wide(ATTN_W), wide(SSM_W), wide(D_MODEL), wide(D_MODEL), vec(SSM_W), vec(ATTN_W), vec(SSM_W), vec(D_MODEL),
                  pl.BlockSpec((SSM_W, SSM_W), const), pl.BlockSpec((D_MODEL, D_MODEL), const), any_spec, any_spec],
        out_specs=[wide(D_MODEL), wide(D_FF), wide(D_FF), wide(D_MODEL), wide(D_MODEL), wide(D_MODEL), wide(SSM_W),
                   wide(ATTN_W), wide(SSM_W), wide(D_MODEL), wide(SSM_W),
                   vec(D_MODEL), vec(LANES), vec(ATTN_W), vec(SSM_W), vec(SSM_W)],
        out_shape=[f32(D_MODEL), b16(D_FF), b16(D_FF), b16(D_MODEL), b16(D_MODEL), b16(D_MODEL), b16(SSM_W),
                   f32(ATTN_W), f32(SSM_W), b16(D_MODEL), b16(SSM_W),
                   acc(D_MODEL), acc(LANES), acc(ATTN_W), acc(SSM_W), acc(SSM_W)],
        scratch_shapes=[pltpu.VMEM((D_MODEL, D_FF), BF16), pltpu.VMEM((D_FF, D_MODEL), BF16),
                        pltpu.VMEM((tm, D_FF), F32), pltpu.SemaphoreType.DMA((2,))],
        compiler_params=_params("arbitrary"),
    )(attn, y, x, target, glu_b, ga, gs, g2, wg, wo, wu, wd)


def _inproj_bwd(dqs, dkn, dv, du, q_raw, k_raw, x, dx2, wi, g1, gq, gk, ones64):
    t_len = x.shape[0]
    tm = 512
    n_heads = ATTN_W // HEAD

    def body(dqs_ref, dkn_ref, dv_ref, du_ref, q_ref, k_ref, x_ref, dx2_ref, wi_ref, g1_ref, gq_ref, gk_ref, bd_ref,
             gx_ref, dproj_ref, dg1_ref, dgq_ref, dgk_ref, accq, acck):
        i = pl.program_id(0)

        @pl.when(i == 0)
        def _():
            dg1_ref[...] = jnp.zeros_like(dg1_ref)
            accq[...] = jnp.zeros_like(accq)
            acck[...] = jnp.zeros_like(acck)

        bd = bd_ref[...]

        def head_norm_bwd(dy, raw, gain, acc):
            r = lax.rsqrt(_group_mean(raw * raw, bd, HEAD) + EPS)
            xh = raw * r
            dxh = dy * gain
            acc[...] = acc[...] + _colsum(dy * xh)
            return r * (dxh - xh * _group_mean(dxh * xh, bd, HEAD))

        dq = head_norm_bwd(dqs_ref[...] * (HEAD ** -0.5), q_ref[...], gq_ref[...], accq)
        dk = head_norm_bwd(dkn_ref[...], k_ref[...], gk_ref[...], acck)
        dproj_ref[:, 0:ATTN_W] = dq.astype(BF16)
        dproj_ref[:, ATTN_W:2 * ATTN_W] = dk.astype(BF16)
        dproj_ref[:, 2 * ATTN_W:3 * ATTN_W] = dv_ref[...].astype(BF16)
        dproj_ref[:, 3 * ATTN_W:] = du_ref[...].astype(BF16)
        dxn = _nt(dproj_ref[...], wi_ref[...])
        xv = x_ref[...]
        g1v = g1_ref[...]
        dx, g1_term = _rms_bwd(dxn, xv, _rms(xv), g1v)
        gx_ref[...] = dx2_ref[...] + dx
        dg1_ref[...] = dg1_ref[...] + _colsum(g1_term)

        @pl.when(i == pl.num_programs(0) - 1)
        def _():
            for acc, out in ((accq, dgq_ref), (acck, dgk_ref)):
                tot = acc[:, 0:HEAD]
                for h in range(1, n_heads):
                    tot = tot + acc[:, h * HEAD:(h + 1) * HEAD]
                out[...] = tot

    row = lambda i: (i, 0)
    const = lambda i: (0, 0)
    aw = pl.BlockSpec((tm, ATTN_W), row)
    dm = pl.BlockSpec((tm, D_MODEL), row)
    return pl.pallas_call(
        body, name="inproj_bwd", grid=(t_len // tm,),
        in_specs=[aw, aw, aw, aw, aw, aw, dm, dm, pl.BlockSpec((D_MODEL, PROJ_W), const), pl.BlockSpec((1, D_MODEL), const),
                  pl.BlockSpec((1, ATTN_W), const), pl.BlockSpec((1, ATTN_W), const), pl.BlockSpec((ATTN_W, ATTN_W), const)],
        out_specs=[dm, pl.BlockSpec((tm, PROJ_W), row), pl.BlockSpec((1, D_MODEL), const),
                   pl.BlockSpec((1, HEAD), const), pl.BlockSpec((1, HEAD), const)],
        out_shape=[jax.ShapeDtypeStruct((t_len, D_MODEL), F32), jax.ShapeDtypeStruct((t_len, PROJ_W), BF16),
                   jax.ShapeDtypeStruct((1, D_MODEL), F32), jax.ShapeDtypeStruct((1, HEAD), F32),
                   jax.ShapeDtypeStruct((1, HEAD), F32)],
        scratch_shapes=[pltpu.VMEM((1, ATTN_W), F32), pltpu.VMEM((1, ATTN_W), F32)],
        compiler_params=_params("arbitrary"),
    )(dqs, dkn, dv, du, q_raw, k_raw, x, dx2, wi, g1, gq, gk, ones64)


def _grad_matmul(a, b, name):
    t_len, m = a.shape
    n = b.shape[1]
    bm, bn, bt = min(m, 1024), min(n, 1024), 4096

    def body(a_ref, b_ref, o_ref):
        @pl.when(pl.program_id(2) == 0)
        def _():
            o_ref[...] = jnp.zeros_like(o_ref)

        o_ref[...] = o_ref[...] + _tn(a_ref[...], b_ref[...])

    return pl.pallas_call(
        body, name=name, grid=(m // bm, n // bn, t_len // bt),
        in_specs=[pl.BlockSpec((bt, bm), lambda i, j, k: (k, i)), pl.BlockSpec((bt, bn), lambda i, j, k: (k, j))],
        out_specs=pl.BlockSpec((bm, bn), lambda i, j, k: (i, j)),
        out_shape=jax.ShapeDtypeStruct((m, n), F32),
        compiler_params=_params("arbitrary", "arbitrary", "arbitrary"),
    )(a, b)


def _adamw_update(w_ref, g_ref, m_ref, v_ref, d_o, m_o, v_o):
    gv = g_ref[...]
    mn = ADAM_B1 * m_ref[...] + (1.0 - ADAM_B1) * gv
    vn = ADAM_B2 * v_ref[...] + (1.0 - ADAM_B2) * jnp.square(gv)
    m_hat = mn / (1.0 - ADAM_B1 ** ADAM_STEP)
    v_hat = vn / (1.0 - ADAM_B2 ** ADAM_STEP)
    d_o[...] = -ADAM_LR * (m_hat / (jnp.sqrt(v_hat) + ADAM_EPS) + ADAM_WD * w_ref[...])
    m_o[...] = mn
    v_o[...] = vn


def _adamw_many(ws, gs, ms, vs):
    n = len(ws)

    def body(*refs):
        for i in range(n):
            _adamw_update(*[refs[k * n + i] for k in range(7)])

    shapes = [jax.ShapeDtypeStruct(w.shape, F32) for w in ws]
    outs = pl.pallas_call(body, name="adamw_small", out_shape=shapes * 3,
                          compiler_params=pltpu.CompilerParams(vmem_limit_bytes=VMEM_LIMIT))(*ws, *gs, *ms, *vs)
    return outs[0:n], outs[n:2 * n], outs[2 * n:]


def _adamw(w, g, m, v, name):
    rows, cols = w.shape
    br = _row_block(rows, 256)
    body = functools.partial(_adamw_update)

    spec = pl.BlockSpec((br, cols), lambda i: (i, 0))
    shape = jax.ShapeDtypeStruct((rows, cols), F32)
    return pl.pallas_call(
        body, name=name, grid=(rows // br,), in_specs=[spec] * 4, out_specs=[spec] * 3, out_shape=[shape] * 3,
        compiler_params=_params("arbitrary"),
    )(w, g, m, v)


def _sum_arrays(arrs, name, out_dtype=F32):
    rows, cols = arrs[0].shape
    br = _row_block(rows, 512)
    n = len(arrs)

    def body(*refs):
        tot = refs[0][...]
        for r in refs[1:n]:
            tot = tot + r[...]
        refs[n][...] = tot.astype(out_dtype)

    spec = pl.BlockSpec((br, cols), lambda i: (i, 0))
    return pl.pallas_call(
        body, name=name, grid=(rows // br,), in_specs=[spec] * n, out_specs=spec,
        out_shape=jax.ShapeDtypeStruct((rows, cols), out_dtype), compiler_params=_params("arbitrary"),
    )(*arrs)


GPL = N_GROUPS // N_LB
SW = GPL * N_STATE


def _eye_groups():
    return jnp.eye(GPL, dtype=F32)


def _s5_matrices(ab_r, ab_i, bb_r, bb_i, c_re, c_im, d_skip):
    eye = _eye_groups()
    a_cat = jnp.concatenate([ab_r.reshape(N_LB, HALF, LANES), ab_i.reshape(N_LB, HALF, LANES)], axis=1)

    def b_part(bb):
        b4 = jnp.transpose(bb.reshape(N_LB, GPL, N_STATE, GROUP_W), (0, 1, 3, 2))
        return (b4[:, :, :, None, :] * eye[None, :, None, :, None]).reshape(N_LB, LANES, SW)

    def c_part(cc):
        c4 = jnp.transpose(cc.reshape(N_LB, GPL, GROUP_W, N_STATE), (0, 1, 3, 2))
        return (c4[:, :, :, None, :] * eye[None, :, None, :, None]).reshape(N_LB, SW, LANES)

    b_mat = jnp.concatenate([b_part(bb_r), b_part(bb_i)], axis=2).astype(BF16)
    c_mat = jnp.concatenate([c_part(c_re), -c_part(c_im)], axis=1).astype(BF16)
    return a_cat, b_mat, c_mat, d_skip.reshape(N_LB, 1, LANES)


def _s5_unpack_grads(db, dc, da, dd):
    eye = _eye_groups()
    mask = eye[None, :, None, None, :, None]
    d6 = jnp.sum(db.reshape(N_LB, GPL, GROUP_W, 2, GPL, N_STATE) * mask, axis=4)
    dbb = jnp.transpose(d6, (3, 0, 1, 4, 2)).reshape(2, N_GROUPS * N_STATE, GROUP_W)
    c6 = jnp.sum(dc.reshape(N_LB, GPL, GROUP_W, 2, GPL, N_STATE) * mask, axis=4)
    dcc = jnp.transpose(c6, (3, 0, 1, 2, 4)).reshape(2, N_GROUPS, GROUP_W, N_STATE)
    dab_r = da[:, :HALF].reshape(N_GROUPS * N_STATE, 1)
    dab_i = da[:, HALF:].reshape(N_GROUPS * N_STATE, 1)
    return dab_r, dab_i, dbb[0], dbb[1], dcc[0], -dcc[1], dd.reshape(N_GROUPS, GROUP_W)


def _block_ones(n, width):
    i = lax.broadcasted_iota(jnp.int32, (n, n), 0) // width
    j = lax.broadcasted_iota(jnp.int32, (n, n), 1) // width
    return (i == j).astype(BF16)


def _tile_heads(g):
    return jnp.tile(g.reshape(1, HEAD), (1, ATTN_W // HEAD))


def _local_step(x, target, wi, rest, p, fwd_side=None, bwd_side=None):
    ones64 = _block_ones(ATTN_W, HEAD)
    ones_hp = _block_ones(LANES, HEAD)
    g1 = p["norm1_g"].reshape(1, D_MODEL)
    g2 = p["norm2_g"].reshape(1, D_MODEL)
    gq = _tile_heads(p["q_norm_g"])
    gk = _tile_heads(p["k_norm_g"])
    ga = p["attn_out_norm_g"].reshape(1, ATTN_W)
    gs = p["ssm_out_norm_g"].reshape(1, SSM_W)
    glu_b = p["glu_b"].reshape(1, SSM_W)
    n_gp = N_GROUPS * N_STATE
    lr = p["ssm_a_re"].reshape(n_gp, 1)
    li = p["ssm_a_im"].reshape(n_gp, 1)
    ldt = jnp.repeat(p["ssm_log_dt"].reshape(N_GROUPS), N_STATE).reshape(n_gp, 1)
    br = p["ssm_b_re"].reshape(n_gp, GROUP_W)
    bi = p["ssm_b_im"].reshape(n_gp, GROUP_W)
    ab_r, ab_i, bb_r, bb_i = _disc_fwd(lr, li, ldt, br, bi)
    a_cat, b_mat, c_mat, d_mat = _s5_matrices(
        ab_r, ab_i, bb_r, bb_i, p["ssm_c_re"].reshape(N_GROUPS, GROUP_W, N_STATE),
        p["ssm_c_im"].reshape(N_GROUPS, GROUP_W, N_STATE), p["ssm_d"])

    xn, qn, kn, vv, u, q_raw, k_raw = _inproj_fwd(x, g1, wi, gq, gk, ones64)
    if fwd_side is None:
        attn, lse = _attn_fwd(qn, kn, vv)
    else:
        attn, lse, *rest = _attn_fwd(qn, kn, vv, *fwd_side)
    wg, wo, wu, wd = rest
    y, states = _s5_fwd(u, a_cat, b_mat, c_mat, d_mat)
    (dx2, hdn, dup, h, dyb, mix, z, d_attn, dy_ssm, dx2b, dgp, dg2, loss, dga, dgs, dglu_b) = _mixer_mlp(
        attn, y, x, target, wg, glu_b, ga, gs, wo, g2, wu, wd)
    big = {"w_mlp_up": _grad_matmul(h, dup, "grad_w_mlp_up"), "w_mlp_down": _grad_matmul(hdn, dyb, "grad_w_mlp_down"),
           "w_out": _grad_matmul(mix, dx2b, "grad_w_out"), "glu_w": _grad_matmul(z, dgp, "grad_glu_w")}
    rode = []
    if bwd_side is None:
        dqs, dkn, dvv = _attn_bwd(qn, kn, vv, attn, d_attn, lse, ones_hp)
    else:
        dqs, dkn, dvv, *rode = _attn_bwd(qn, kn, vv, attn, d_attn, lse, ones_hp, *bwd_side(big))
    du, db, dc, da, dd = _s5_bwd(u, dy_ssm, states, a_cat, b_mat, c_mat, d_mat)
    grad_x, dproj, dg1, dgq, dgk = _inproj_bwd(dqs, dkn, dvv, du, q_raw, k_raw, x, dx2, wi, g1, gq, gk, ones64)

    big["w_in"] = _grad_matmul(xn, dproj, "grad_w_in")
    dab_r, dab_i, dbb_r, dbb_i, dc_re, dc_im, dd_g = _s5_unpack_grads(db, dc, da, dd)
    cot = {"norm1_g": dg1, "q_norm_g": dgq, "k_norm_g": dgk, "ab_r": dab_r, "ab_i": dab_i, "bb_r": dbb_r, "bb_i": dbb_i,
           "ssm_c_re": dc_re, "ssm_c_im": dc_im, "ssm_d": dd_g, "glu_b": dglu_b, "attn_out_norm_g": dga,
           "ssm_out_norm_g": dgs, "norm2_g": dg2}
    return loss[0, 0], grad_x, big, cot, (lr, li, ldt, br, bi), rode


COT_NAMES = ("norm1_g", "q_norm_g", "k_norm_g", "ab_r", "ab_i", "bb_r", "bb_i", "ssm_c_re", "ssm_c_im", "ssm_d",
             "glu_b", "attn_out_norm_g", "ssm_out_norm_g", "norm2_g")
SMALL_NAMES = ("norm1_g", "q_norm_g", "k_norm_g", "ssm_a_re", "ssm_a_im", "ssm_log_dt", "ssm_b_re", "ssm_b_im",
               "ssm_c_re", "ssm_c_im", "ssm_d", "glu_b", "attn_out_norm_g", "ssm_out_norm_g", "norm2_g")
BIG_NAMES = ("w_in", "glu_w", "w_out", "w_mlp_up", "w_mlp_down")
PACK_ROWS = 1152


def _pack(arrs):
    flat = jnp.concatenate([a.reshape(-1) for a in arrs])
    return jnp.pad(flat, (0, PACK_ROWS * LANES - flat.shape[0])).reshape(PACK_ROWS, LANES)


def _unpack(packed, like):
    flat = packed.reshape(-1)
    out, pos = [], 0
    for a in like:
        out.append(flat[pos:pos + a.size].reshape(a.shape))
        pos += a.size
    return out


def _small_grads(cot, disc_in, p):
    lr, li, ldt, br, bi = disc_in
    group_sum = (lax.broadcasted_iota(jnp.int32, (N_GROUPS, N_GROUPS * N_STATE), 1) // N_STATE
                 == lax.broadcasted_iota(jnp.int32, (N_GROUPS, N_GROUPS * N_STATE), 0)).astype(F32)
    dlr, dli, dldt, dbr, dbi = _disc_bwd(lr, li, ldt, br, bi, cot["ab_r"], cot["ab_i"], cot["bb_r"], cot["bb_i"], group_sum)
    g = dict(cot)
    g.update(ssm_a_re=dlr, ssm_a_im=dli, ssm_log_dt=dldt[:, 0], ssm_b_re=dbr, ssm_b_im=dbi)
    return {n: g[n].reshape(p[n].shape) for n in SMALL_NAMES}


BIG = {
    "w_in": ((D_MODEL, PROJ_W), 1, PROJ_W // 4, 0, D_MODEL // 2),
    "glu_w": ((SSM_W, SSM_W), 0, SSM_W // 4, 1, SSM_W // 2),
    "w_out": ((D_MODEL, D_MODEL), 0, D_MODEL // 4, 1, D_MODEL // 2),
    "w_mlp_up": ((D_MODEL, D_FF), 1, D_FF // 4, 0, D_MODEL // 2),
    "w_mlp_down": ((D_FF, D_MODEL), 0, D_FF // 4, 1, D_MODEL // 2),
}
N_BIG = len(BIG_NAMES)
N_CHIPS = 4
ANY = pl.BlockSpec(memory_space=pl.ANY)


def _cut(name, shard=False, half=False):
    shape, s_ax, s_sz, h_ax, h_sz = BIG[name]
    shape = list(shape)
    if shard:
        shape[s_ax] = s_sz
    if half:
        shape[h_ax] = h_sz
    return tuple(shape)


def _window(name, base, shard=None, half=None):
    _, s_ax, s_sz, h_ax, h_sz = BIG[name]
    idx = [pl.ds(0, base[0]), pl.ds(0, base[1])]
    if shard is not None:
        idx[s_ax] = pl.ds(pl.multiple_of(shard * s_sz, s_sz), s_sz)
    if half is not None:
        idx[h_ax] = pl.ds(pl.multiple_of(half * h_sz, h_sz), h_sz)
    return tuple(idx)


def _mesh_pos():
    return lax.axis_index("x"), lax.axis_index("y"), lax.axis_index("c")


def _other_chips(x, y):
    return [(1 - x, y, 2 * (1 - x) + y), (x, 1 - y, 2 * x + 1 - y), (1 - x, 1 - y, 2 * (1 - x) + 1 - y)]


def _remote(src, dst, send_sem, recv_sem, dev):
    return pltpu.make_async_remote_copy(src_ref=src, dst_ref=dst, send_sem=send_sem, recv_sem=recv_sem,
                                        device_id=dev, device_id_type=MESH)


def _start_remote(src, dst, send_sem, recv_sem, dev):
    cp = _remote(src, dst, send_sem, recv_sem, dev)
    cp.start()
    return cp


class _Gather:
    def __init__(self, names):
        self.names = tuple(names)
        self.n = len(self.names)

    def in_specs(self):
        return [pl.BlockSpec(memory_space=pltpu.VMEM)] * self.n

    def out_specs(self):
        return [ANY] * self.n

    def out_shapes(self):
        return [jax.ShapeDtypeStruct(BIG[w][0], BF16) for w in self.names]

    def scratch_shapes(self):
        n_sem = (N_CHIPS - 1) * self.n
        return ([pltpu.VMEM(_cut(w, shard=True), BF16) for w in self.names]
                + [pltpu.SemaphoreType.DMA((n_sem,))] * 4 + [pltpu.SemaphoreType.DMA((self.n,))])

    def bind(self, ins, outs, scratch):
        self.ins, self.outs = ins, outs
        self.stage = scratch[:self.n]
        self.send, self.recv, self.fsend, self.frecv, self.lsem = scratch[self.n:]

    def _copies(self):
        x, y, c = _mesh_pos()
        me = 2 * x + y
        sib = (x, y, 1 - c)
        local, sends, lands, fwds, flands = [], [], [], [], []
        for w, n in enumerate(self.names):
            local.append(pltpu.make_async_copy(self.stage[w], self.outs[w].at[_window(n, BIG[n][0], shard=me)], self.lsem.at[w]))
        for k, (px, py, pj) in enumerate(_other_chips(x, y)):
            for w, n in enumerate(self.names):
                s = k * self.n + w
                sends.append(_remote(self.stage[w].at[_window(n, _cut(n, shard=True), half=c)],
                                     self.outs[w].at[_window(n, BIG[n][0], shard=me, half=c)],
                                     self.send.at[s], self.recv.at[s], (px, py, c)))
                got = self.outs[w].at[_window(n, BIG[n][0], shard=pj, half=c)]
                lands.append(_remote(got, got, self.send.at[s], self.recv.at[s], (px, py, c)))
                fwds.append(_remote(got, got, self.fsend.at[s], self.frecv.at[s], sib))
                theirs = self.outs[w].at[_window(n, BIG[n][0], shard=pj, half=1 - c)]
                flands.append(_remote(theirs, theirs, self.fsend.at[s], self.frecv.at[s], sib))
        return local, sends, lands, fwds, flands

    def start(self):
        for w in range(self.n):
            self.stage[w][...] = self.ins[w][...].astype(BF16)
        local, sends, _, _, _ = self._copies()
        for cp in local + sends:
            cp.start()

    def forward(self):
        _, _, lands, fwds, _ = self._copies()
        for land, fwd in zip(lands, fwds):
            land.wait_recv()
            fwd.start()

    def finish(self):
        local, sends, _, fwds, flands = self._copies()
        for cp in flands:
            cp.wait_recv()
        for cp in sends + fwds:
            cp.wait_send()
        for cp in local:
            cp.wait()


def _gather_weights(shards, names):
    g = _Gather(names)

    def body(*refs):
        g.bind(refs[0:g.n], refs[g.n:2 * g.n], refs[2 * g.n:])
        g.start()
        g.forward()
        g.finish()

    return pl.pallas_call(
        body, name="gather_" + "_".join(names), in_specs=g.in_specs(), out_specs=g.out_specs(), out_shape=g.out_shapes(),
        scratch_shapes=g.scratch_shapes(), compiler_params=pltpu.CompilerParams(vmem_limit_bytes=VMEM_LIMIT),
    )(*[shards[n] for n in names])


def _pair_exchange(grads, names, packed=None):
    n_big = len(names)
    n_all = n_big + (packed is not None)

    def body(*refs):
        ins, got = refs[0:n_all], refs[n_all:2 * n_all]
        send, recv = refs[2 * n_all:]
        x, y, c = _mesh_pos()
        sib = (x, y, 1 - c)
        copies = []
        for w, n in enumerate(names):
            copies.append(_start_remote(ins[w].at[_window(n, BIG[n][0], half=1 - c)], got[w], send.at[w], recv.at[w], sib))
        if packed is not None:
            copies.append(_start_remote(ins[n_big], got[n_big], send.at[n_big], recv.at[n_big], sib))
        for cp in copies:
            cp.wait()

    shapes = [jax.ShapeDtypeStruct(_cut(n, half=True), F32) for n in names]
    args = [grads[n] for n in names]
    if packed is not None:
        shapes.append(jax.ShapeDtypeStruct(packed.shape, F32))
        args.append(packed)
    return pl.pallas_call(
        body, name="grad_pair_exchange_" + "_".join(names), in_specs=[ANY] * n_all, out_specs=[ANY] * n_all, out_shape=shapes,
        scratch_shapes=[pltpu.SemaphoreType.DMA((n_all,)), pltpu.SemaphoreType.DMA((n_all,))],
    )(*args)


def _pair_sum(name, full, got, core):
    _, _, _, h_ax, _ = BIG[name]
    rows, cols = _cut(name, half=True)
    br = _row_block(rows, 512)
    nb = rows // br
    own_map = (lambda i, c: (i + c[0] * nb, 0)) if h_ax == 0 else (lambda i, c: (i, c[0]))

    def body(c_ref, own_ref, got_ref, o_ref):
        o_ref[...] = (own_ref[...] + got_ref[...]).astype(BF16)

    plain = pl.BlockSpec((br, cols), lambda i, c: (i, 0))
    return pl.pallas_call(
        body, name="pair_sum_" + name,
        grid_spec=pltpu.PrefetchScalarGridSpec(num_scalar_prefetch=1, grid=(nb,),
                                               in_specs=[pl.BlockSpec((br, cols), own_map), plain], out_specs=plain),
        out_shape=jax.ShapeDtypeStruct((rows, cols), BF16), compiler_params=_params("arbitrary"),
    )(core, full, got)


class _ChipExchange:
    def __init__(self, names, packed_shape=None):
        self.names = tuple(names)
        self.packed_shape = packed_shape
        self.n = len(self.names) + (packed_shape is not None)

    def in_specs(self):
        return [ANY] * self.n

    def out_specs(self):
        return [ANY] * self.n

    def out_shapes(self):
        shapes = [jax.ShapeDtypeStruct((N_CHIPS,) + _cut(w, shard=True, half=True), BF16) for w in self.names]
        if self.packed_shape is not None:
            shapes.append(jax.ShapeDtypeStruct((N_CHIPS,) + tuple(self.packed_shape), F32))
        return shapes

    def scratch_shapes(self):
        n_sem = (N_CHIPS - 1) * self.n
        return [pltpu.SemaphoreType.DMA((n_sem,)), pltpu.SemaphoreType.DMA((n_sem,))]

    def bind(self, ins, outs, scratch):
        self.ins, self.outs = ins, outs
        self.send, self.recv = scratch

    def _piece(self, w, shard):
        if w >= len(self.names):
            return self.ins[w]
        n = self.names[w]
        return self.ins[w].at[_window(n, _cut(n, half=True), shard=shard)]

    def _copies(self):
        x, y, c = _mesh_pos()
        me = 2 * x + y
        sends, lands = [], []
        for k, (px, py, pj) in enumerate(_other_chips(x, y)):
            for w in range(self.n):
                s = k * self.n + w
                sends.append(_remote(self._piece(w, pj), self.outs[w].at[me], self.send.at[s], self.recv.at[s], (px, py, c)))
                lands.append(_remote(self._piece(w, me), self.outs[w].at[pj], self.send.at[s], self.recv.at[s], (px, py, c)))
        return sends, lands

    def start(self):
        for cp in self._copies()[0]:
            cp.start()

    def finish(self):
        sends, lands = self._copies()
        for cp in lands:
            cp.wait_recv()
        for cp in sends:
            cp.wait_send()


def _chip_exchange(halves, packed, names):
    ex = _ChipExchange(names, packed.shape)

    def body(*refs):
        ex.bind(refs[0:ex.n], refs[ex.n:2 * ex.n], refs[2 * ex.n:])
        ex.start()
        ex.finish()

    return pl.pallas_call(
        body, name="grad_chip_exchange", in_specs=ex.in_specs(), out_specs=ex.out_specs(), out_shape=ex.out_shapes(),
        scratch_shapes=ex.scratch_shapes(),
    )(*halves, packed)


def _chip_sum(name, own, slots, chip):
    n_slot, rows, cols = slots.shape
    br = _row_block(rows, 512)
    nb = rows // br
    if name in BIG and BIG[name][1] == 1:
        own_map = lambda i, m: (i, m[0])
    elif name in BIG:
        own_map = lambda i, m: (i + m[0] * nb, 0)
    else:
        own_map = lambda i, m: (i, 0)

    def slot_map(j):
        return lambda i, m: (jnp.where(m[0] == j, (j + 1) % n_slot, j), i, 0)

    def body(m_ref, own_ref, *refs):
        own_blk = own_ref[...].astype(F32)
        tot = None
        for j in range(n_slot):
            term = jnp.where(m_ref[0] == j, own_blk, refs[j][...].astype(F32))
            tot = term if tot is None else tot + term
        refs[n_slot][...] = tot

    in_specs = [pl.BlockSpec((br, cols), own_map)] + [pl.BlockSpec((None, br, cols), slot_map(j)) for j in range(n_slot)]
    return pl.pallas_call(
        body, name="chip_sum_" + name,
        grid_spec=pltpu.PrefetchScalarGridSpec(num_scalar_prefetch=1, grid=(nb,), in_specs=in_specs,
                                               out_specs=pl.BlockSpec((br, cols), lambda i, m: (i, 0))),
        out_shape=jax.ShapeDtypeStruct((rows, cols), F32), compiler_params=_params("arbitrary"),
    )(chip, own, *([slots] * n_slot))


def _half_exchange(pieces):
    def body(*refs):
        ins, outs = refs[0:N_BIG], refs[N_BIG:2 * N_BIG]
        send, recv = refs[2 * N_BIG:]
        x, y, c = _mesh_pos()
        sib = (x, y, 1 - c)
        copies = []
        for w, n in enumerate(BIG_NAMES):
            copies.append(_start_remote(ins[w], outs[w], send.at[w], recv.at[w], sib))
        for cp in copies:
            cp.wait()

    return pl.pallas_call(
        body, name="grad_half_exchange", in_specs=[ANY] * N_BIG, out_specs=[ANY] * N_BIG,
        out_shape=[jax.ShapeDtypeStruct(_cut(n, shard=True, half=True), F32) for n in BIG_NAMES],
        scratch_shapes=[pltpu.SemaphoreType.DMA((N_BIG,)), pltpu.SemaphoreType.DMA((N_BIG,))],
    )(*pieces)


WEIGHT_NAMES = ("norm1_g", "w_in", "q_norm_g", "k_norm_g", "ssm_a_re", "ssm_a_im", "ssm_log_dt", "ssm_b_re", "ssm_b_im",
                "ssm_c_re", "ssm_c_im", "ssm_d", "glu_w", "glu_b", "attn_out_norm_g", "ssm_out_norm_g", "w_out", "norm2_g",
                "w_mlp_up", "w_mlp_down")


def _train_step(a):
    x = a["x"][0]
    target = a["loss_target"][0]
    shards = {n: a[n][0] for n in BIG_NAMES}
    p = {n: a[n][0] for n in SMALL_NAMES}
    core = lax.axis_index("c").astype(jnp.int32).reshape(1)
    chip_id = (2 * lax.axis_index("x") + lax.axis_index("y")).astype(jnp.int32).reshape(1)

    later = ("glu_w", "w_out", "w_mlp_up", "w_mlp_down")
    early = ("w_mlp_up", "w_mlp_down", "w_out", "glu_w")
    late = ("w_in",)
    (wi,) = _gather_weights(shards, ("w_in",))
    chip = {}

    def bwd_side(grads):
        got = _pair_exchange(grads, early)
        for n, g in zip(early, got):
            chip[n] = _pair_sum(n, grads[n], g, core)
        return _ChipExchange(early), [chip[n] for n in early]

    loss, grad_x, big, cot, disc_in, early_slots = _local_step(
        x, target, wi, None, p, fwd_side=(_Gather(later), [shards[n] for n in later]), bwd_side=bwd_side)
    slots = dict(zip(early, early_slots))

    cot_list = [cot[n] for n in COT_NAMES]
    packed = _pack(cot_list)
    *got, got_packed = _pair_exchange(big, late, packed)
    for n, g in zip(late, got):
        chip[n] = _pair_sum(n, big[n], g, core)
    chip_packed = _sum_arrays([packed, got_packed], "pair_sum_small")
    *late_slots, small_slots = _chip_exchange([chip[n] for n in late], chip_packed, late)
    slots.update(zip(late, late_slots))
    pieces = [_chip_sum(n, chip[n], slots[n], chip_id) for n in BIG_NAMES]
    small_sum = _chip_sum("small", chip_packed, small_slots, chip_id)
    shard_grads = {}
    for n, mine, theirs in zip(BIG_NAMES, pieces, _half_exchange(pieces)):
        h_ax = BIG[n][3]
        shard_grads[n] = jnp.where(core[0] == 0, jnp.concatenate([mine, theirs], axis=h_ax),
                                   jnp.concatenate([theirs, mine], axis=h_ax))
    small_grads = _small_grads(dict(zip(COT_NAMES, _unpack(small_sum, cot_list))), disc_in, p)

    grads, delta, new_m, new_v = {}, {}, {}, {}
    for n in BIG_NAMES:
        grads[n] = shard_grads[n]
        delta[n], new_m[n], new_v[n] = _adamw(a[n][0], grads[n], a["m_" + n][0], a["v_" + n][0], "adamw_" + n)
    flat2 = lambda t: t.reshape(-1, t.shape[-1])
    res = _adamw_many([flat2(p[n]) for n in SMALL_NAMES], [flat2(small_grads[n]) for n in SMALL_NAMES],
                      [flat2(a["m_" + n][0]) for n in SMALL_NAMES], [flat2(a["v_" + n][0]) for n in SMALL_NAMES])
    for store, outs in zip((delta, new_m, new_v), res):
        store.update(zip(SMALL_NAMES, outs))
    grads.update(small_grads)

    total = lax.psum(loss, ("x", "y", "c"))
    out = [total, grad_x[None]]
    for store in (grads, delta, new_m, new_v):
        out += [store[n].reshape(a[n].shape) for n in WEIGHT_NAMES]
    return tuple(out)


def kernel(x, norm1_g, w_in, q_norm_g, k_norm_g, ssm_a_re, ssm_a_im, ssm_log_dt, ssm_b_re, ssm_b_im, ssm_c_re, ssm_c_im, ssm_d, glu_w, glu_b, attn_out_norm_g, ssm_out_norm_g, w_out, norm2_g, w_mlp_up, w_mlp_down, loss_target, m_norm1_g, m_w_in, m_q_norm_g, m_k_norm_g, m_ssm_a_re, m_ssm_a_im, m_ssm_log_dt, m_ssm_b_re, m_ssm_b_im, m_ssm_c_re, m_ssm_c_im, m_ssm_d, m_glu_w, m_glu_b, m_attn_out_norm_g, m_ssm_out_norm_g, m_w_out, m_norm2_g, m_w_mlp_up, m_w_mlp_down, v_norm1_g, v_w_in, v_q_norm_g, v_k_norm_g, v_ssm_a_re, v_ssm_a_im, v_ssm_log_dt, v_ssm_b_re, v_ssm_b_im, v_ssm_c_re, v_ssm_c_im, v_ssm_d, v_glu_w, v_glu_b, v_attn_out_norm_g, v_ssm_out_norm_g, v_w_out, v_norm2_g, v_w_mlp_up, v_w_mlp_down):
    return _train_step(dict(locals()))
```

```python
import functools
import math

import jax
import jax.numpy as jnp
from jax import lax
from jax.experimental import pallas as pl
from jax.experimental.pallas import tpu as pltpu

F32 = jnp.float32
BF16 = jnp.bfloat16
MESH = pl.DeviceIdType.MESH

D_MODEL = 1024
ATTN_W = 512
SSM_W = 512
HEAD = 64
D_FF = 4096
PROJ_W = 2048
N_GROUPS = 32
N_STATE = 64
GROUP_W = 16
EPS = 1e-6
NEG = -1e30
DILATIONS = (1, 4, 16)
BLK = 128
TILE = 2048
LANES = 128
MXU_WIDTH = 256
N_LB = SSM_W // LANES
VMEM_LIMIT = 56 * 1024 * 1024

ADAM_LR, ADAM_B1, ADAM_B2, ADAM_EPS, ADAM_WD, ADAM_STEP = 0.001, 0.9, 0.999, 1e-08, 0.01, 10


def _params(*sem):
    return pltpu.CompilerParams(dimension_semantics=sem, vmem_limit_bytes=VMEM_LIMIT)


def _nt(a, b):
    return lax.dot_general(a, b, (((1,), (1,)), ((), ())), preferred_element_type=F32)


def _tn(a, b):
    return lax.dot_general(a, b, (((0,), (0,)), ((), ())), preferred_element_type=F32)


def _mm(a, b):
    return jnp.dot(a, b, preferred_element_type=F32)


def _group_mean(t, ones_bd, width):
    span = ones_bd.shape[0]
    hi = t.astype(BF16)
    lo = (t - hi.astype(F32)).astype(BF16)
    parts = [_mm(hi[:, k:k + span], ones_bd) + _mm(lo[:, k:k + span], ones_bd) for k in range(0, t.shape[1], span)]
    return (parts[0] if len(parts) == 1 else jnp.concatenate(parts, axis=1)) * (1.0 / width)


def _rms(x):
    return lax.rsqrt(jnp.mean(x * x, axis=-1, keepdims=True) + EPS)


def _rms_bwd(dy, x, r, g):
    xh = x * r
    dxh = dy * g
    dx = r * (dxh - xh * jnp.mean(dxh * xh, axis=-1, keepdims=True))
    return dx, dy * xh


def _colsum(x):
    return jnp.sum(x, axis=0, keepdims=True)


def _row_block(rows, cap):
    for b in range(min(rows, cap) // 8 * 8, 0, -8):
        if rows % b == 0:
            return b
    raise ValueError(f"no row block for {rows} rows")


def _inproj_fwd(x, g1, wi, gq, gk, ones64):
    t_len = x.shape[0]
    tm = 512
    n_hp = ATTN_W // LANES

    def body(x_ref, g1_ref, wi_ref, gq_ref, gk_ref, bd_ref, xn_ref, q_ref, k_ref, v_ref, u_ref, qr_ref, kr_ref):
        xv = x_ref[...]
        xn = (xv * _rms(xv) * g1_ref[...]).astype(BF16)
        xn_ref[...] = xn
        proj = _mm(xn, wi_ref[...])
        q = proj[:, 0:ATTN_W]
        k = proj[:, ATTN_W:2 * ATTN_W]
        v = proj[:, 2 * ATTN_W:3 * ATTN_W]
        u_ref[...] = proj[:, 3 * ATTN_W:]
        qr_ref[...] = q
        kr_ref[...] = k
        bd = bd_ref[...]
        qn = q * lax.rsqrt(_group_mean(q * q, bd, HEAD) + EPS) * gq_ref[...] * (HEAD ** -0.5)
        kn = k * lax.rsqrt(_group_mean(k * k, bd, HEAD) + EPS) * gk_ref[...]
        for hp in range(n_hp):
            sl = slice(hp * LANES, (hp + 1) * LANES)
            q_ref[hp] = qn[:, sl]
            k_ref[hp] = kn[:, sl]
            v_ref[hp] = v[:, sl]

    row = lambda i: (i, 0)
    const = lambda i: (0, 0)
    hp_spec = pl.BlockSpec((n_hp, tm, LANES), lambda i: (0, i, 0))
    hp_shape = jax.ShapeDtypeStruct((n_hp, t_len, LANES), F32)
    return pl.pallas_call(
        body, name="inproj_fwd", grid=(t_len // tm,),
        in_specs=[pl.BlockSpec((tm, D_MODEL), row), pl.BlockSpec((1, D_MODEL), const),
                  pl.BlockSpec((D_MODEL, PROJ_W), const), pl.BlockSpec((1, ATTN_W), const),
                  pl.BlockSpec((1, ATTN_W), const), pl.BlockSpec(ones64.shape, const)],
        out_specs=[pl.BlockSpec((tm, D_MODEL), row), hp_spec, hp_spec, hp_spec,
                   pl.BlockSpec((tm, SSM_W), row), pl.BlockSpec((tm, ATTN_W), row), pl.BlockSpec((tm, ATTN_W), row)],
        out_shape=[jax.ShapeDtypeStruct((t_len, D_MODEL), BF16), hp_shape, hp_shape, hp_shape,
                   jax.ShapeDtypeStruct((t_len, SSM_W), F32), jax.ShapeDtypeStruct((t_len, ATTN_W), F32),
                   jax.ShapeDtypeStruct((t_len, ATTN_W), F32)],
        compiler_params=_params("arbitrary"),
    )(x, g1, wi, gq, gk, ones64)


def _attn_masks():
    head0 = lax.broadcasted_iota(jnp.int32, (BLK, LANES), 1) < HEAD
    row = lax.broadcasted_iota(jnp.int32, (2 * BLK, 2 * BLK), 0) & (BLK - 1)
    col = lax.broadcasted_iota(jnp.int32, (2 * BLK, 2 * BLK), 1)
    return head0, (col < BLK) & (col >= row), (col >= BLK) & (col - BLK <= row)


def _stack_heads(x, head0):
    return jnp.concatenate([jnp.where(head0, x, 0.0), jnp.where(head0, 0.0, x)], axis=0).astype(BF16)


def _unit_rows(uidx, d):
    nb = TILE // (BLK * d)
    r = lax.div(uidx, nb)
    b = lax.rem(uidx, nb)
    start = r + d * BLK * b
    if d == 1:
        start = pl.multiple_of(start, BLK)
        mk = lambda s: pl.ds(pl.multiple_of(s, BLK), BLK)
    else:
        mk = lambda s: pl.ds(s, BLK, stride=d)
    return b, mk(start), mk(TILE + start), mk(TILE + start - d * BLK)


def _attn_fwd(q, k, v, side=None, side_args=()):
    n_hp, t_len, _ = q.shape
    nt = t_len // TILE
    ns = side.n if side is not None else 0
    n_steps = n_hp * nt

    def body(*refs):
        q_ref, kp_ref, kc_ref, vp_ref, vc_ref = refs[0:5]
        o_ref, lse_ref = refs[5 + ns:7 + ns]
        kk, vv, m_s, l_s, acc_s = refs[7 + 2 * ns:12 + 2 * ns]
        t = pl.program_id(1)
        step = pl.program_id(0) * nt + t
        if side is not None:
            side.bind(refs[5:5 + ns], refs[7 + ns:7 + 2 * ns], refs[12 + 2 * ns:])
            pl.when(step == 0)(side.start)
            pl.when(step == n_steps // 2)(side.forward)
        kk[0:TILE] = kp_ref[0]
        kk[TILE:] = kc_ref[0]
        vv[0:TILE] = vp_ref[0]
        vv[TILE:] = vc_ref[0]
        head0, band_prev, band_cur = _attn_masks()

        for pi, d in enumerate(DILATIONS):
            def unit(uidx, carry, d=d, pi=pi):
                b, rows_q, rows_c, rows_p = _unit_rows(uidx, d)
                mask = band_cur | (band_prev & ((t > 0) | (b > 0)))
                q2 = _stack_heads(q_ref.at[0][rows_q, :], head0)
                kcat = jnp.concatenate([kk[rows_p, :], kk[rows_c, :]], axis=0).astype(BF16)
                vcat = jnp.concatenate([vv[rows_p, :], vv[rows_c, :]], axis=0).astype(BF16)
                s = jnp.where(mask, _nt(q2, kcat), NEG)
                m = jnp.max(s, axis=1, keepdims=True)
                p = jnp.exp(s - m)
                ls = jnp.sum(p, axis=1, keepdims=True)
                pv = _mm(p.astype(BF16), vcat)
                m_s.at[pi][rows_q, :] = jnp.where(head0, m[0:BLK], m[BLK:])
                l_s.at[pi][rows_q, :] = jnp.where(head0, ls[0:BLK], ls[BLK:])
                acc_s.at[pi][rows_q, :] = jnp.where(head0, pv[0:BLK], pv[BLK:])
                return carry

            lax.fori_loop(0, TILE // BLK, unit, 0, unroll=8)

        m_all = jnp.maximum(jnp.maximum(m_s[0], m_s[1]), m_s[2])
        num = jnp.zeros((TILE, LANES), F32)
        den = jnp.zeros((TILE, LANES), F32)
        for pi in range(len(DILATIONS)):
            wgt = jnp.exp(m_s[pi] - m_all)
            num = num + acc_s[pi] * wgt
            den = den + l_s[pi] * wgt
        o_ref[...] = num / den
        lse_ref[0] = m_all + jnp.log(den)
        if side is not None:
            pl.when(step == n_steps - 1)(side.finish)

    cur = lambda hp, t: (hp, t, 0)
    prev = lambda hp, t: (hp, jnp.maximum(t - 1, 0), 0)
    blk = (1, TILE, LANES)
    per_pattern = pltpu.VMEM((len(DILATIONS), TILE, LANES), F32)
    extra = (side.in_specs(), side.out_specs(), side.out_shapes(), side.scratch_shapes()) if side is not None else ([], [], [], [])
    return pl.pallas_call(
        body, name="attn_fwd", grid=(n_hp, nt),
        in_specs=[pl.BlockSpec(blk, cur), pl.BlockSpec(blk, prev), pl.BlockSpec(blk, cur),
                  pl.BlockSpec(blk, prev), pl.BlockSpec(blk, cur)] + extra[0],
        out_specs=[pl.BlockSpec((TILE, LANES), lambda hp, t: (t, hp)), pl.BlockSpec(blk, cur)] + extra[1],
        out_shape=[jax.ShapeDtypeStruct((t_len, ATTN_W), F32), jax.ShapeDtypeStruct((n_hp, t_len, LANES), F32)] + extra[2],
        scratch_shapes=[pltpu.VMEM((2 * TILE, LANES), F32), pltpu.VMEM((2 * TILE, LANES), F32),
                        per_pattern, per_pattern, per_pattern] + extra[3],
        compiler_params=_params("arbitrary", "arbitrary"),
    )(q, k, k, v, v, *side_args)


def _attn_bwd(q, k, v, o, do, lse, ones_hp, side=None, side_args=()):
    n_hp, t_len, _ = q.shape
    nt = t_len // TILE
    ns = side.n if side is not None else 0
    n_pat = len(DILATIONS)

    def body(*refs):
        q_ref, kp_ref, kc_ref, vp_ref, vc_ref, o_ref, do_ref, lse_ref, bd_ref = refs[0:9]
        dq_ref, dk_ref, dv_ref = refs[9 + ns:12 + ns]
        kk, vv, dq_s, dkc, dkp, dvc, dvp, hold_k, hold_v, dl_s = refs[12 + 2 * ns:22 + 2 * ns]
        t = pl.program_id(1)
        if side is not None:
            side.bind(refs[9:9 + ns], refs[12 + ns:12 + 2 * ns], refs[22 + 2 * ns:])
            pl.when((pl.program_id(0) == 0) & (t == 0))(side.start)

        @pl.when(t < nt)
        def _():
            kk[0:TILE] = kp_ref[0]
            kk[TILE:] = kc_ref[0]
            vv[0:TILE] = vp_ref[0]
            vv[TILE:] = vc_ref[0]
            dl_s[...] = _group_mean(do_ref[...] * o_ref[...], bd_ref[...], 1.0)
            head0, band_prev, band_cur = _attn_masks()

            for pi, d in enumerate(DILATIONS):
                def unit(uidx, carry, d=d, pi=pi):
                    b, rows_q, rows_c, rows_p = _unit_rows(uidx, d)
                    mask = band_cur | (band_prev & ((t > 0) | (b > 0)))
                    q2 = _stack_heads(q_ref.at[0][rows_q, :], head0)
                    do2 = _stack_heads(do_ref[rows_q, :], head0)
                    lse_f = lse_ref.at[0][rows_q, :]
                    dl_f = dl_s[rows_q, :]
                    lse2 = jnp.concatenate([lse_f[:, 0:1], lse_f[:, HEAD:HEAD + 1]], axis=0)
                    dl2 = jnp.concatenate([dl_f[:, 0:1], dl_f[:, HEAD:HEAD + 1]], axis=0)
                    kcat = jnp.concatenate([kk[rows_p, :], kk[rows_c, :]], axis=0).astype(BF16)
                    vcat = jnp.concatenate([vv[rows_p, :], vv[rows_c, :]], axis=0).astype(BF16)
                    p = jnp.where(mask, jnp.exp(_nt(q2, kcat) - lse2), 0.0)
                    ds = (p * (_nt(do2, vcat) - dl2)).astype(BF16)
                    dq2 = _mm(ds, kcat)
                    dq_s.at[pi][rows_q, :] = jnp.where(head0, dq2[0:BLK], dq2[BLK:])
                    dk2 = _tn(ds, q2)
                    dv2 = _tn(p.astype(BF16), do2)
                    dkp.at[pi][rows_q, :] = dk2[0:BLK]
                    dkc.at[pi][rows_q, :] = dk2[BLK:]
                    dvp.at[pi][rows_q, :] = dv2[0:BLK]
                    dvc.at[pi][rows_q, :] = dv2[BLK:]
                    return carry

                lax.fori_loop(0, TILE // BLK, unit, 0, unroll=8)

            dq_ref[...] = dq_s[0] + dq_s[1] + dq_s[2]

        @pl.when(t > 0)
        def _():
            dk_ref[...] = hold_k[...]
            dv_ref[...] = hold_v[...]

        @pl.when((t > 0) & (t < nt))
        def _():
            for pi, d in enumerate(DILATIONS):
                back = d * BLK
                dk_ref[TILE - back:, :] = dk_ref[TILE - back:, :] + dkp[pi, 0:back, :]
                dv_ref[TILE - back:, :] = dv_ref[TILE - back:, :] + dvp[pi, 0:back, :]

        @pl.when(t < nt)
        def _():
            hold_k[...] = dkc[0] + dkc[1] + dkc[2]
            hold_v[...] = dvc[0] + dvc[1] + dvc[2]
            for pi, d in enumerate(DILATIONS):
                back = d * BLK
                if back < TILE:
                    hold_k[0:TILE - back, :] = hold_k[0:TILE - back, :] + dkp[pi, back:, :]
                    hold_v[0:TILE - back, :] = hold_v[0:TILE - back, :] + dvp[pi, back:, :]

        if side is not None:
            pl.when((pl.program_id(0) == n_hp - 1) & (t == nt))(side.finish)

    last = nt - 1
    extra = (side.in_specs(), side.out_specs(), side.out_shapes(), side.scratch_shapes()) if side is not None else ([], [], [], [])
    cur = lambda hp, t: (hp, jnp.minimum(t, last), 0)
    prev = lambda hp, t: (hp, jnp.clip(t - 1, 0, last), 0)
    cur2 = lambda hp, t: (jnp.minimum(t, last), hp)
    prev2 = lambda hp, t: (jnp.maximum(t - 1, 0), hp)
    blk = (1, TILE, LANES)
    blk2 = (TILE, LANES)
    out = jax.ShapeDtypeStruct((t_len, ATTN_W), F32)
    return pl.pallas_call(
        body, name="attn_bwd", grid=(n_hp, nt + 1),
        in_specs=[pl.BlockSpec(blk, cur), pl.BlockSpec(blk, prev), pl.BlockSpec(blk, cur),
                  pl.BlockSpec(blk, prev), pl.BlockSpec(blk, cur), pl.BlockSpec(blk2, cur2),
                  pl.BlockSpec(blk2, cur2), pl.BlockSpec(blk, cur), pl.BlockSpec((LANES, LANES), lambda hp, t: (0, 0))]
        + extra[0],
        out_specs=[pl.BlockSpec(blk2, cur2), pl.BlockSpec(blk2, prev2), pl.BlockSpec(blk2, prev2)] + extra[1],
        out_shape=[out, out, out] + extra[2],
        scratch_shapes=[pltpu.VMEM((2 * TILE, LANES), F32), pltpu.VMEM((2 * TILE, LANES), F32)]
        + [pltpu.VMEM((n_pat, TILE, LANES), F32)] * 5 + [pltpu.VMEM((TILE, LANES), F32)] * 3 + extra[3],
        compiler_params=_params("arbitrary", "arbitrary"),
    )(q, k, k, v, v, o, do, lse, ones_hp, *side_args)


def _discretise(lr, li, ldt, br, bi):
    dt = jnp.exp(ldt)
    mag = jnp.exp(lr * dt)
    ab_r, ab_i = mag * jnp.cos(li * dt), mag * jnp.sin(li * dt)
    den = lr * lr + li * li
    nr, ni = ab_r - 1.0, ab_i
    cr = (nr * lr + ni * li) / den
    ci = (ni * lr - nr * li) / den
    return ab_r, ab_i, cr * br - ci * bi, cr * bi + ci * br


def _disc_fwd(lr, li, ldt, br, bi):
    def body(lr_ref, li_ref, ldt_ref, br_ref, bi_ref, ar_o, ai_o, bbr_o, bbi_o):
        outs = _discretise(lr_ref[...], li_ref[...], ldt_ref[...], br_ref[...], bi_ref[...])
        for o_ref, val in zip((ar_o, ai_o, bbr_o, bbi_o), outs):
            o_ref[...] = val

    col = jax.ShapeDtypeStruct(lr.shape, F32)
    mat = jax.ShapeDtypeStruct(br.shape, F32)
    return pl.pallas_call(body, name="s5_disc_fwd", out_shape=[col, col, mat, mat])(lr, li, ldt, br, bi)


def _disc_bwd(lr, li, ldt, br, bi, d_ar, d_ai, d_bbr, d_bbi, group_sum):
    def body(lr_ref, li_ref, ldt_ref, br_ref, bi_ref, c1, c2, c3, c4, gs_ref, dlr_o, dli_o, dldt_o, dbr_o, dbi_o):
        _, vjp = jax.vjp(_discretise, lr_ref[...], li_ref[...], ldt_ref[...], br_ref[...], bi_ref[...])
        dlr, dli, dldt, dbr, dbi = vjp((c1[...], c2[...], c3[...], c4[...]))
        dlr_o[...] = dlr
        dli_o[...] = dli
        dbr_o[...] = dbr
        dbi_o[...] = dbi
        wide = jnp.broadcast_to(dldt, (dldt.shape[0], LANES))
        dldt_o[...] = jnp.dot(gs_ref[...], wide, precision=lax.Precision.HIGHEST, preferred_element_type=F32)

    col = jax.ShapeDtypeStruct(lr.shape, F32)
    mat = jax.ShapeDtypeStruct(br.shape, F32)
    return pl.pallas_call(
        body, name="s5_disc_bwd", out_shape=[col, col, jax.ShapeDtypeStruct((N_GROUPS, LANES), F32), mat, mat],
    )(lr, li, ldt, br, bi, d_ar, d_ai, d_bbr, d_bbi, group_sum)


N_CHUNK = TILE // BLK
HALF = 4


def _cmul(ar, ai, xr, xi):
    return ar * xr - ai * xi, ar * xi + ai * xr


def _power_table(a_ref, tab, sign, reverse):
    ar = [a_ref[0, j:j + 1, :] for j in range(HALF)]
    ai = [sign * a_ref[0, HALF + j:HALF + j + 1, :] for j in range(HALF)]

    def step(s, cur):
        row = pl.ds((BLK - 1 - s) if reverse else s, 1)
        nxt = []
        for j in range(HALF):
            tab.at[j][row, :] = cur[j]
            tab.at[HALF + j][row, :] = cur[HALF + j]
            nxt.append(_cmul(ar[j], ai[j], cur[j], cur[HALF + j]))
        return tuple(p[0] for p in nxt) + tuple(p[1] for p in nxt)

    lax.fori_loop(0, BLK, step, tuple(ar) + tuple(ai))


def _interleave(src, dst):
    for c in range(N_CHUNK):
        dst[pl.ds(c, BLK, stride=N_CHUNK), :] = src[c * BLK:(c + 1) * BLK, :]


def _deinterleave(src, dst):
    for c in range(N_CHUNK):
        dst[c * BLK:(c + 1) * BLK, :] = src[pl.ds(c, BLK, stride=N_CHUNK), :]


def _step_rows(s):
    return pl.ds(pl.multiple_of(s * N_CHUNK, N_CHUNK), N_CHUNK)


def _chunk_scan(buf, a_ref, sign, reverse):
    ar = [jnp.broadcast_to(a_ref[0, j:j + 1, :], (N_CHUNK, LANES)) for j in range(HALF)]
    ai = [sign * jnp.broadcast_to(a_ref[0, HALF + j:HALF + j + 1, :], (N_CHUNK, LANES)) for j in range(HALF)]

    def step(i, carry):
        s = (BLK - 1 - i) if reverse else i
        rows = _step_rows(s)
        out = []
        for j in range(HALF):
            pr, pi = _cmul(ar[j], ai[j], carry[j], carry[HALF + j])
            xr = buf.at[j][rows, :] + pr
            xi = buf.at[HALF + j][rows, :] + pi
            buf.at[j][rows, :] = xr
            buf.at[HALF + j][rows, :] = xi
            out.append((xr, xi))
        return tuple(p[0] for p in out) + tuple(p[1] for p in out)

    zero = jnp.zeros((N_CHUNK, LANES), F32)
    lax.fori_loop(0, BLK, step, (zero,) * (2 * HALF), unroll=2)


def _chunk_states(buf, carry_s, xin_s, tab, reverse):
    edge = 0 if reverse else BLK - 1
    top = 0 if reverse else BLK - 1
    pw = [tab[j, top:top + 1, :] for j in range(2 * HALF)]
    cur = [carry_s[j:j + 1, :] for j in range(2 * HALF)]
    summary = [buf[j, edge * N_CHUNK:(edge + 1) * N_CHUNK, :] for j in range(2 * HALF)]
    order = range(N_CHUNK - 1, -1, -1) if reverse else range(N_CHUNK)
    for c in order:
        for j in range(2 * HALF):
            xin_s[j, c:c + 1, :] = cur[j]
        nxt = []
        for j in range(HALF):
            pr, pi = _cmul(pw[j], pw[HALF + j], cur[j], cur[HALF + j])
            nxt.append((pr + summary[j][c:c + 1, :], pi + summary[HALF + j][c:c + 1, :]))
        cur = [p[0] for p in nxt] + [p[1] for p in nxt]
    for j in range(2 * HALF):
        carry_s[j:j + 1, :] = cur[j]


def _s5_fwd(u, a_cat, b_mat, c_mat, d_skip):
    t_len = u.shape[0]
    nt = t_len // TILE

    def body(u_ref, a_ref, b_ref, c_ref, d_ref, y_ref, x_ref, xs, us, tab, carry_s, xin_s):
        sb = pl.program_id(1)

        @pl.when(sb == 0)
        def _():
            _power_table(a_ref, tab, 1.0, False)
            carry_s[...] = jnp.zeros_like(carry_s)

        _interleave(u_ref, us)
        uv = us[...]
        bu = _mm(uv.astype(BF16), b_ref[0])
        for j in range(2 * HALF):
            xs[j] = bu[:, j * LANES:(j + 1) * LANES]
        _chunk_scan(xs, a_ref, 1.0, False)
        _chunk_states(xs, carry_s, xin_s, tab, False)
        xin = [xin_s[j] for j in range(2 * HALF)]

        def fix(s, acc):
            rows = _step_rows(s)
            for j in range(HALF):
                pr, pi = _cmul(tab.at[j][pl.ds(s, 1), :], tab.at[HALF + j][pl.ds(s, 1), :], xin[j], xin[HALF + j])
                xs.at[j][rows, :] = xs.at[j][rows, :] + pr
                xs.at[HALF + j][rows, :] = xs.at[HALF + j][rows, :] + pi
            return acc

        lax.fori_loop(0, BLK, fix, 0, unroll=2)
        xcat = jnp.concatenate([xs[j].astype(BF16) for j in range(2 * HALF)], axis=1)
        x_ref[0] = xcat
        us[...] = d_ref[0] * uv + _mm(xcat, c_ref[0])
        _deinterleave(us, y_ref)

    return pl.pallas_call(
        body, name="s5_fwd", grid=(N_LB, nt),
        in_specs=[pl.BlockSpec((TILE, LANES), lambda lb, sb: (sb, lb)),
                  pl.BlockSpec((1, 2 * HALF, LANES), lambda lb, sb: (lb, 0, 0)),
                  pl.BlockSpec((1, LANES, 2 * HALF * LANES), lambda lb, sb: (lb, 0, 0)),
                  pl.BlockSpec((1, 2 * HALF * LANES, LANES), lambda lb, sb: (lb, 0, 0)),
                  pl.BlockSpec((1, 1, LANES), lambda lb, sb: (lb, 0, 0))],
        out_specs=[pl.BlockSpec((TILE, LANES), lambda lb, sb: (sb, lb)),
                   pl.BlockSpec((1, TILE, 2 * HALF * LANES), lambda lb, sb: (lb, sb, 0))],
        out_shape=[jax.ShapeDtypeStruct((t_len, SSM_W), F32), jax.ShapeDtypeStruct((N_LB, t_len, 2 * HALF * LANES), BF16)],
        scratch_shapes=[pltpu.VMEM((2 * HALF, TILE, LANES), F32), pltpu.VMEM((TILE, LANES), F32),
                        pltpu.VMEM((2 * HALF, BLK, LANES), F32),
                        pltpu.VMEM((2 * HALF, LANES), F32), pltpu.VMEM((2 * HALF, N_CHUNK, LANES), F32)],
        compiler_params=_params("arbitrary", "arbitrary"),
    )(u, a_cat, b_mat, c_mat, d_skip)


def _s5_bwd(u, dy, states, a_cat, b_mat, c_mat, d_skip):
    t_len = u.shape[0]
    nt = t_len // TILE
    last = nt - 1

    def body(u_ref, dy_ref, x_ref, a_ref, b_ref, c_ref, d_ref, du_ref, db_ref, dc_ref, da_ref, dd_ref,
             gs, us, dys, tabc, lam_s, lin_s):
        sb = pl.program_id(1)

        @pl.when(sb == 0)
        def _():
            _power_table(a_ref, tabc, -1.0, True)
            lam_s[...] = jnp.zeros_like(lam_s)
            db_ref[...] = jnp.zeros_like(db_ref)
            dc_ref[...] = jnp.zeros_like(dc_ref)
            da_ref[...] = jnp.zeros_like(da_ref)
            dd_ref[...] = jnp.zeros_like(dd_ref)

        _interleave(u_ref, us)
        _interleave(dy_ref, dys)
        uv = us[...]
        dyv = dys[...]
        ub = uv.astype(BF16)
        dyb = dyv.astype(BF16)
        gy = _nt(dyb, c_ref[0])
        for j in range(2 * HALF):
            gs[j] = gy[:, j * LANES:(j + 1) * LANES]
        _chunk_scan(gs, a_ref, -1.0, True)
        _chunk_states(gs, lam_s, lin_s, tabc, True)
        zero = jnp.zeros((N_CHUNK, LANES), F32)
        x_tile = x_ref.at[0]
        for grp in range(0, HALF, 2):
            slabs = (grp, grp + 1)
            lin = [(lin_s[j], lin_s[HALF + j]) for j in slabs]

            def fix(i, carry, slabs=slabs, lin=lin):
                s = BLK - 1 - i
                rows = _step_rows(s)
                out = []
                for k, j in enumerate(slabs):
                    nr, ni, acc_r, acc_i = carry[4 * k:4 * k + 4]
                    xr = x_tile[rows, pl.ds(j * LANES, LANES)].astype(F32)
                    xi = x_tile[rows, pl.ds((HALF + j) * LANES, LANES)].astype(F32)
                    qr, qi = _cmul(tabc.at[j][pl.ds(s, 1), :], tabc.at[HALF + j][pl.ds(s, 1), :], lin[k][0], lin[k][1])
                    lr_ = gs.at[j][rows, :] + qr
                    li_ = gs.at[HALF + j][rows, :] + qi
                    gs.at[j][rows, :] = lr_
                    gs.at[HALF + j][rows, :] = li_
                    out += [lr_, li_, acc_r + (xr * nr + xi * ni), acc_i + (xr * ni - xi * nr)]
                return tuple(out)

            init = []
            for k in range(len(slabs)):
                init += [lin[k][0], lin[k][1], zero, zero]
            res = lax.fori_loop(0, BLK, fix, tuple(init), unroll=2)
            for k, j in enumerate(slabs):
                da_ref[0, j:j + 1, :] = da_ref[0, j:j + 1, :] + _colsum(res[4 * k + 2])
                da_ref[0, HALF + j:HALF + j + 1, :] = da_ref[0, HALF + j:HALF + j + 1, :] + _colsum(res[4 * k + 3])
        lam = jnp.concatenate([gs[j].astype(BF16) for j in range(2 * HALF)], axis=1)
        us[...] = _nt(lam, b_ref[0]) + d_ref[0] * dyv
        _deinterleave(us, du_ref)
        db_ref[0] = db_ref[0] + _tn(ub, lam)
        dc_ref[0] = dc_ref[0] + _tn(dyb, x_ref[0])
        dd_ref[0] = dd_ref[0] + _colsum(dyv * uv)

    rev = lambda lb, sb: (last - sb, lb)
    per_lb = lambda lb, sb: (lb, 0, 0)
    wide = 2 * HALF * LANES
    return pl.pallas_call(
        body, name="s5_bwd", grid=(N_LB, nt),
        in_specs=[pl.BlockSpec((TILE, LANES), rev), pl.BlockSpec((TILE, LANES), rev),
                  pl.BlockSpec((1, TILE, wide), lambda lb, sb: (lb, last - sb, 0)),
                  pl.BlockSpec((1, 2 * HALF, LANES), per_lb), pl.BlockSpec((1, LANES, wide), per_lb),
                  pl.BlockSpec((1, wide, LANES), per_lb), pl.BlockSpec((1, 1, LANES), per_lb)],
        out_specs=[pl.BlockSpec((TILE, LANES), rev), pl.BlockSpec((1, LANES, wide), per_lb),
                   pl.BlockSpec((1, LANES, wide), per_lb), pl.BlockSpec((1, 2 * HALF, LANES), per_lb),
                   pl.BlockSpec((1, 1, LANES), per_lb)],
        out_shape=[jax.ShapeDtypeStruct((t_len, SSM_W), F32), jax.ShapeDtypeStruct((N_LB, LANES, wide), F32),
                   jax.ShapeDtypeStruct((N_LB, LANES, wide), F32), jax.ShapeDtypeStruct((N_LB, 2 * HALF, LANES), F32),
                   jax.ShapeDtypeStruct((N_LB, 1, LANES), F32)],
        scratch_shapes=[pltpu.VMEM((2 * HALF, TILE, LANES), F32),
                        pltpu.VMEM((TILE, LANES), F32), pltpu.VMEM((TILE, LANES), F32),
                        pltpu.VMEM((2 * HALF, BLK, LANES), F32), pltpu.VMEM((2 * HALF, LANES), F32),
                        pltpu.VMEM((2 * HALF, N_CHUNK, LANES), F32)],
        compiler_params=_params("arbitrary", "arbitrary"),
    )(u, dy, states, a_cat, b_mat, c_mat, d_skip)


_GELU_C = math.sqrt(2.0 / math.pi)
_GELU_K = 0.044715


def _gelu(y):
    t = jnp.tanh(_GELU_C * (y + _GELU_K * (y * y * y)))
    return y * (0.5 * (1.0 + t)), t


def _gelu_grad(y, t):
    return 0.5 * (1.0 + t) + 0.5 * y * (1.0 - t * t) * (_GELU_C * (1.0 + 3.0 * _GELU_K * y * y))


def _glu(y, wg, bias):
    z, t = _gelu(y)
    sg = jax.nn.sigmoid(_mm(z.astype(BF16), wg) + bias)
    return z, t, sg


def _mixer_mlp(attn, y, x, target, wg, glu_b, ga, gs, wo, g2, wu, wd):
    t_len = x.shape[0]
    tm = 256
    fc = 1024
    n_fc = D_FF // fc

    def body(attn_ref, y_ref, x_ref, tg_ref, b_ref, ga_ref, gs_ref, g2_ref, wg_s, wo_s, wu_hbm, wd_hbm,
             dx2_ref, hdn_ref, dup_ref, h_ref, dyb_ref, mix_ref, z_ref, dattn_ref, dy_ref, dx2b_ref, dgp_ref,
             dg2_ref, loss_ref, dga_ref, dgs_ref, db_ref, wu_s, wd_s, relu_s, sem):
        @pl.when(pl.program_id(0) == 0)
        def _():
            copies = [pltpu.make_async_copy(src, dst, sem.at[k]) for k, (src, dst) in enumerate(((wu_hbm, wu_s), (wd_hbm, wd_s)))]
            for cp in copies:
                cp.start()
            for cp in copies:
                cp.wait()
            for acc in (dg2_ref, loss_ref, dga_ref, dgs_ref, db_ref):
                acc[...] = jnp.zeros_like(acc)

        av = attn_ref[...]
        z, _, sg = _glu(y_ref[...], wg_s[...], b_ref[...])
        z_ref[...] = z.astype(BF16)
        s = z * sg
        an = (av * _rms(av) * ga_ref[...]).astype(BF16)
        sn = (s * _rms(s) * gs_ref[...]).astype(BF16)
        mix_ref[:, 0:ATTN_W] = an
        mix_ref[:, ATTN_W:] = sn
        dx2_ref[...] = x_ref[...] + _mm(an, wo_s[0:ATTN_W, :]) + _mm(sn, wo_s[ATTN_W:, :])
        r = _rms(dx2_ref[...])
        g2v = g2_ref[...]
        h = (dx2_ref[...] * r * g2v).astype(BF16)
        h_ref[...] = h
        yout = dx2_ref[...]
        for c in range(n_fc):
            cols = slice(c * fc, (c + 1) * fc)
            ru = jnp.maximum(_mm(h, wu_s[:, cols]), 0.0)
            relu_s[:, cols] = ru
            hd = (ru * ru).astype(BF16)
            hdn_ref[:, cols] = hd
            yout = yout + _mm(hd, wd_s[cols, :])
        err = yout - tg_ref[...]
        loss_ref[...] = loss_ref[...] + 0.5 * jnp.sum(err * err) * (1.0 / D_MODEL)
        dy = err * (1.0 / D_MODEL)
        dyb = dy.astype(BF16)
        dyb_ref[...] = dyb
        dh = jnp.zeros((tm, D_MODEL), F32)
        for c in range(n_fc):
            cols = slice(c * fc, (c + 1) * fc)
            dup = (_nt(dyb, wd_s[cols, :]) * (2.0 * relu_s[:, cols])).astype(BF16)
            dup_ref[:, cols] = dup
            dh = dh + _nt(dup, wu_s[:, cols])
        dxn, g2_term = _rms_bwd(dh, dx2_ref[...], r, g2v)
        dx2_ref[...] = dy + dxn
        dg2_ref[...] = dg2_ref[...] + _colsum(g2_term)
        dx2b = dx2_ref[...].astype(BF16)
        dx2b_ref[...] = dx2b
        yv = y_ref[...]
        av = attn_ref[...]
        z, t, sg = _glu(yv, wg_s[...], b_ref[...])
        s = z * sg
        d_attn, ga_term = _rms_bwd(_nt(dx2b, wo_s[0:ATTN_W, :]), av, _rms(av), ga_ref[...])
        d_s, gs_term = _rms_bwd(_nt(dx2b, wo_s[ATTN_W:, :]), s, _rms(s), gs_ref[...])
        dattn_ref[...] = d_attn
        dgp = d_s * z * sg * (1.0 - sg)
        dgpb = dgp.astype(BF16)
        dgp_ref[...] = dgpb
        dy_ref[...] = (d_s * sg + _nt(dgpb, wg_s[...])) * _gelu_grad(yv, t)
        dga_ref[...] = dga_ref[...] + _colsum(ga_term)
        dgs_ref[...] = dgs_ref[...] + _colsum(gs_term)
        db_ref[...] = db_ref[...] + _colsum(dgp)

    row = lambda i: (i, 0)
    const = lambda i: (0, 0)
    wide = lambda n: pl.BlockSpec((tm, n), row)
    vec = lambda n: pl.BlockSpec((1, n), const)
    any_spec = pl.BlockSpec(memory_space=pl.ANY)
    f32 = lambda n: jax.ShapeDtypeStruct((t_len, n), F32)
    b16 = lambda n: jax.ShapeDtypeStruct((t_len, n), BF16)
    acc = lambda n: jax.ShapeDtypeStruct((1, n), F32)
    return pl.pallas_call(
        body, name="mixer_mlp", grid=(t_len // tm,),
        in_specs=[wide(ATTN_W), wide(SSM_W), wide(D_MODEL), wide(D_MODEL), vec(SSM_W), vec(ATTN_W), vec(SSM_W), vec(D_MODEL),
                  pl.BlockSpec((SSM_W, SSM_W), const), pl.BlockSpec((D_MODEL, D_MODEL), const), any_spec, any_spec],
        out_specs=[wide(D_MODEL), wide(D_FF), wide(D_FF), wide(D_MODEL), wide(D_MODEL), wide(D_MODEL), wide(SSM_W),
                   wide(ATTN_W), wide(SSM_W), wide(D_MODEL), wide(SSM_W),
                   vec(D_MODEL), vec(LANES), vec(ATTN_W), vec(SSM_W), vec(SSM_W)],
        out_shape=[f32(D_MODEL), b16(D_FF), b16(D_FF), b16(D_MODEL), b16(D_MODEL), b16(D_MODEL), b16(SSM_W),
                   f32(ATTN_W), f32(SSM_W), b16(D_MODEL), b16(SSM_W),
                   acc(D_MODEL), acc(LANES), acc(ATTN_W), acc(SSM_W), acc(SSM_W)],
        scratch_shapes=[pltpu.VMEM((D_MODEL, D_FF), BF16), pltpu.VMEM((D_FF, D_MODEL), BF16),
                        pltpu.VMEM((tm, D_FF), F32), pltpu.SemaphoreType.DMA((2,))],
        compiler_params=_params("arbitrary"),
    )(attn, y, x, target, glu_b, ga, gs, g2, wg, wo, wu, wd)


def _inproj_bwd(dqs, dkn, dv, du, q_raw, k_raw, x, dx2, wi, g1, gq, gk, ones64):
    t_len = x.shape[0]
    tm = 512
    n_heads = ATTN_W // HEAD

    def body(dqs_ref, dkn_ref, dv_ref, du_ref, q_ref, k_ref, x_ref, dx2_ref, wi_ref, g1_ref, gq_ref, gk_ref, bd_ref,
             gx_ref, dproj_ref, dg1_ref, dgq_ref, dgk_ref, accq, acck):
        i = pl.program_id(0)

        @pl.when(i == 0)
        def _():
            dg1_ref[...] = jnp.zeros_like(dg1_ref)
            accq[...] = jnp.zeros_like(accq)
            acck[...] = jnp.zeros_like(acck)

        bd = bd_ref[...]

        def head_norm_bwd(dy, raw, gain, acc):
            r = lax.rsqrt(_group_mean(raw * raw, bd, HEAD) + EPS)
            xh = raw * r
            dxh = dy * gain
            acc[...] = acc[...] + _colsum(dy * xh)
            return r * (dxh - xh * _group_mean(dxh * xh, bd, HEAD))

        dq = head_norm_bwd(dqs_ref[...] * (HEAD ** -0.5), q_ref[...], gq_ref[...], accq)
        dk = head_norm_bwd(dkn_ref[...], k_ref[...], gk_ref[...], acck)
        dproj_ref[:, 0:ATTN_W] = dq.astype(BF16)
        dproj_ref[:, ATTN_W:2 * ATTN_W] = dk.astype(BF16)
        dproj_ref[:, 2 * ATTN_W:3 * ATTN_W] = dv_ref[...].astype(BF16)
        dproj_ref[:, 3 * ATTN_W:] = du_ref[...].astype(BF16)
        dxn = _nt(dproj_ref[...], wi_ref[...])
        xv = x_ref[...]
        g1v = g1_ref[...]
        dx, g1_term = _rms_bwd(dxn, xv, _rms(xv), g1v)
        gx_ref[...] = dx2_ref[...] + dx
        dg1_ref[...] = dg1_ref[...] + _colsum(g1_term)

        @pl.when(i == pl.num_programs(0) - 1)
        def _():
            for acc, out in ((accq, dgq_ref), (acck, dgk_ref)):
                tot = acc[:, 0:HEAD]
                for h in range(1, n_heads):
                    tot = tot + acc[:, h * HEAD:(h + 1) * HEAD]
                out[...] = tot

    row = lambda i: (i, 0)
    const = lambda i: (0, 0)
    aw = pl.BlockSpec((tm, ATTN_W), row)
    dm = pl.BlockSpec((tm, D_MODEL), row)
    return pl.pallas_call(
        body, name="inproj_bwd", grid=(t_len // tm,),
        in_specs=[aw, aw, aw, aw, aw, aw, dm, dm, pl.BlockSpec((D_MODEL, PROJ_W), const), pl.BlockSpec((1, D_MODEL), const),
                  pl.BlockSpec((1, ATTN_W), const), pl.BlockSpec((1, ATTN_W), const), pl.BlockSpec(ones64.shape, const)],
        out_specs=[dm, pl.BlockSpec((tm, PROJ_W), row), pl.BlockSpec((1, D_MODEL), const),
                   pl.BlockSpec((1, HEAD), const), pl.BlockSpec((1, HEAD), const)],
        out_shape=[jax.ShapeDtypeStruct((t_len, D_MODEL), F32), jax.ShapeDtypeStruct((t_len, PROJ_W), BF16),
                   jax.ShapeDtypeStruct((1, D_MODEL), F32), jax.ShapeDtypeStruct((1, HEAD), F32),
                   jax.ShapeDtypeStruct((1, HEAD), F32)],
        scratch_shapes=[pltpu.VMEM((1, ATTN_W), F32), pltpu.VMEM((1, ATTN_W), F32)],
        compiler_params=_params("arbitrary"),
    )(dqs, dkn, dv, du, q_raw, k_raw, x, dx2, wi, g1, gq, gk, ones64)


def _grad_matmul(a, b, name):
    t_len, m = a.shape
    n = b.shape[1]
    bm, bn, bt = min(m, 1024), min(n, 1024), 4096

    def body(a_ref, b_ref, o_ref):
        @pl.when(pl.program_id(2) == 0)
        def _():
            o_ref[...] = jnp.zeros_like(o_ref)

        o_ref[...] = o_ref[...] + _tn(a_ref[...], b_ref[...])

    return pl.pallas_call(
        body, name=name, grid=(m // bm, n // bn, t_len // bt),
        in_specs=[pl.BlockSpec((bt, bm), lambda i, j, k: (k, i)), pl.BlockSpec((bt, bn), lambda i, j, k: (k, j))],
        out_specs=pl.BlockSpec((bm, bn), lambda i, j, k: (i, j)),
        out_shape=jax.ShapeDtypeStruct((m, n), F32),
        compiler_params=_params("arbitrary", "arbitrary", "arbitrary"),
    )(a, b)


def _adamw_update(w_ref, g_ref, m_ref, v_ref, d_o, m_o, v_o):
    gv = g_ref[...]
    mn = ADAM_B1 * m_ref[...] + (1.0 - ADAM_B1) * gv
    vn = ADAM_B2 * v_ref[...] + (1.0 - ADAM_B2) * jnp.square(gv)
    m_hat = mn / (1.0 - ADAM_B1 ** ADAM_STEP)
    v_hat = vn / (1.0 - ADAM_B2 ** ADAM_STEP)
    d_o[...] = -ADAM_LR * (m_hat / (jnp.sqrt(v_hat) + ADAM_EPS) + ADAM_WD * w_ref[...])
    m_o[...] = mn
    v_o[...] = vn


def _adamw_many(ws, gs, ms, vs):
    n = len(ws)

    def body(*refs):
        for i in range(n):
            _adamw_update(*[refs[k * n + i] for k in range(7)])

    shapes = [jax.ShapeDtypeStruct(w.shape, F32) for w in ws]
    outs = pl.pallas_call(body, name="adamw_small", out_shape=shapes * 3,
                          compiler_params=pltpu.CompilerParams(vmem_limit_bytes=VMEM_LIMIT))(*ws, *gs, *ms, *vs)
    return outs[0:n], outs[n:2 * n], outs[2 * n:]


def _adamw(w, g, m, v, name):
    rows, cols = w.shape
    br = _row_block(rows, 256)
    body = functools.partial(_adamw_update)

    spec = pl.BlockSpec((br, cols), lambda i: (i, 0))
    shape = jax.ShapeDtypeStruct((rows, cols), F32)
    return pl.pallas_call(
        body, name=name, grid=(rows // br,), in_specs=[spec] * 4, out_specs=[spec] * 3, out_shape=[shape] * 3,
        compiler_params=_params("arbitrary"),
    )(w, g, m, v)


def _sum_arrays(arrs, name, out_dtype=F32):
    rows, cols = arrs[0].shape
    br = _row_block(rows, 512)
    n = len(arrs)

    def body(*refs):
        tot = refs[0][...]
        for r in refs[1:n]:
            tot = tot + r[...]
        refs[n][...] = tot.astype(out_dtype)

    spec = pl.BlockSpec((br, cols), lambda i: (i, 0))
    return pl.pallas_call(
        body, name=name, grid=(rows // br,), in_specs=[spec] * n, out_specs=spec,
        out_shape=jax.ShapeDtypeStruct((rows, cols), out_dtype), compiler_params=_params("arbitrary"),
    )(*arrs)


GPL = N_GROUPS // N_LB
SW = GPL * N_STATE


def _eye_groups():
    return jnp.eye(GPL, dtype=F32)


def _s5_matrices(ab_r, ab_i, bb_r, bb_i, c_re, c_im, d_skip):
    eye = _eye_groups()
    a_cat = jnp.concatenate([ab_r.reshape(N_LB, HALF, LANES), ab_i.reshape(N_LB, HALF, LANES)], axis=1)

    def b_part(bb):
        b4 = jnp.transpose(bb.reshape(N_LB, GPL, N_STATE, GROUP_W), (0, 1, 3, 2))
        return (b4[:, :, :, None, :] * eye[None, :, None, :, None]).reshape(N_LB, LANES, SW)

    def c_part(cc):
        c4 = jnp.transpose(cc.reshape(N_LB, GPL, GROUP_W, N_STATE), (0, 1, 3, 2))
        return (c4[:, :, :, None, :] * eye[None, :, None, :, None]).reshape(N_LB, SW, LANES)

    b_mat = jnp.concatenate([b_part(bb_r), b_part(bb_i)], axis=2).astype(BF16)
    c_mat = jnp.concatenate([c_part(c_re), -c_part(c_im)], axis=1).astype(BF16)
    return a_cat, b_mat, c_mat, d_skip.reshape(N_LB, 1, LANES)


def _s5_unpack_grads(db, dc, da, dd):
    eye = _eye_groups()
    mask = eye[None, :, None, None, :, None]
    d6 = jnp.sum(db.reshape(N_LB, GPL, GROUP_W, 2, GPL, N_STATE) * mask, axis=4)
    dbb = jnp.transpose(d6, (3, 0, 1, 4, 2)).reshape(2, N_GROUPS * N_STATE, GROUP_W)
    c6 = jnp.sum(dc.reshape(N_LB, GPL, GROUP_W, 2, GPL, N_STATE) * mask, axis=4)
    dcc = jnp.transpose(c6, (3, 0, 1, 2, 4)).reshape(2, N_GROUPS, GROUP_W, N_STATE)
    dab_r = da[:, :HALF].reshape(N_GROUPS * N_STATE, 1)
    dab_i = da[:, HALF:].reshape(N_GROUPS * N_STATE, 1)
    return dab_r, dab_i, dbb[0], dbb[1], dcc[0], -dcc[1], dd.reshape(N_GROUPS, GROUP_W)


def _block_ones(n, width):
    i = lax.broadcasted_iota(jnp.int32, (n, n), 0) // width
    j = lax.broadcasted_iota(jnp.int32, (n, n), 1) // width
    return (i == j).astype(BF16)


def _tile_heads(g):
    return jnp.tile(g.reshape(1, HEAD), (1, ATTN_W // HEAD))


def _local_step(x, target, wi, rest, p, fwd_side=None, bwd_side=None):
    ones64 = _block_ones(MXU_WIDTH, HEAD)
    ones_hp = _block_ones(LANES, HEAD)
    g1 = p["norm1_g"].reshape(1, D_MODEL)
    g2 = p["norm2_g"].reshape(1, D_MODEL)
    gq = _tile_heads(p["q_norm_g"])
    gk = _tile_heads(p["k_norm_g"])
    ga = p["attn_out_norm_g"].reshape(1, ATTN_W)
    gs = p["ssm_out_norm_g"].reshape(1, SSM_W)
    glu_b = p["glu_b"].reshape(1, SSM_W)
    n_gp = N_GROUPS * N_STATE
    lr = p["ssm_a_re"].reshape(n_gp, 1)
    li = p["ssm_a_im"].reshape(n_gp, 1)
    ldt = jnp.repeat(p["ssm_log_dt"].reshape(N_GROUPS), N_STATE).reshape(n_gp, 1)
    br = p["ssm_b_re"].reshape(n_gp, GROUP_W)
    bi = p["ssm_b_im"].reshape(n_gp, GROUP_W)
    ab_r, ab_i, bb_r, bb_i = _disc_fwd(lr, li, ldt, br, bi)
    a_cat, b_mat, c_mat, d_mat = _s5_matrices(
        ab_r, ab_i, bb_r, bb_i, p["ssm_c_re"].reshape(N_GROUPS, GROUP_W, N_STATE),
        p["ssm_c_im"].reshape(N_GROUPS, GROUP_W, N_STATE), p["ssm_d"])

    xn, qn, kn, vv, u, q_raw, k_raw = _inproj_fwd(x, g1, wi, gq, gk, ones64)
    if fwd_side is None:
        attn, lse = _attn_fwd(qn, kn, vv)
    else:
        attn, lse, *rest = _attn_fwd(qn, kn, vv, *fwd_side)
    wg, wo, wu, wd = rest
    y, states = _s5_fwd(u, a_cat, b_mat, c_mat, d_mat)
    (dx2, hdn, dup, h, dyb, mix, z, d_attn, dy_ssm, dx2b, dgp, dg2, loss, dga, dgs, dglu_b) = _mixer_mlp(
        attn, y, x, target, wg, glu_b, ga, gs, wo, g2, wu, wd)
    big = {"w_mlp_up": _grad_matmul(h, dup, "grad_w_mlp_up"), "w_mlp_down": _grad_matmul(hdn, dyb, "grad_w_mlp_down"),
           "w_out": _grad_matmul(mix, dx2b, "grad_w_out"), "glu_w": _grad_matmul(z, dgp, "grad_glu_w")}
    rode = []
    if bwd_side is None:
        dqs, dkn, dvv = _attn_bwd(qn, kn, vv, attn, d_attn, lse, ones_hp)
    else:
        dqs, dkn, dvv, *rode = _attn_bwd(qn, kn, vv, attn, d_attn, lse, ones_hp, *bwd_side(big))
    du, db, dc, da, dd = _s5_bwd(u, dy_ssm, states, a_cat, b_mat, c_mat, d_mat)
    grad_x, dproj, dg1, dgq, dgk = _inproj_bwd(dqs, dkn, dvv, du, q_raw, k_raw, x, dx2, wi, g1, gq, gk, ones64)

    big["w_in"] = _grad_matmul(xn, dproj, "grad_w_in")
    dab_r, dab_i, dbb_r, dbb_i, dc_re, dc_im, dd_g = _s5_unpack_grads(db, dc, da, dd)
    cot = {"norm1_g": dg1, "q_norm_g": dgq, "k_norm_g": dgk, "ab_r": dab_r, "ab_i": dab_i, "bb_r": dbb_r, "bb_i": dbb_i,
           "ssm_c_re": dc_re, "ssm_c_im": dc_im, "ssm_d": dd_g, "glu_b": dglu_b, "attn_out_norm_g": dga,
           "ssm_out_norm_g": dgs, "norm2_g": dg2}
    return loss[0, 0], grad_x, big, cot, (lr, li, ldt, br, bi), rode


COT_NAMES = ("norm1_g", "q_norm_g", "k_norm_g", "ab_r", "ab_i", "bb_r", "bb_i", "ssm_c_re", "ssm_c_im", "ssm_d",
             "glu_b", "attn_out_norm_g", "ssm_out_norm_g", "norm2_g")
SMALL_NAMES = ("norm1_g", "q_norm_g", "k_norm_g", "ssm_a_re", "ssm_a_im", "ssm_log_dt", "ssm_b_re", "ssm_b_im",
               "ssm_c_re", "ssm_c_im", "ssm_d", "glu_b", "attn_out_norm_g", "ssm_out_norm_g", "norm2_g")
BIG_NAMES = ("w_in", "glu_w", "w_out", "w_mlp_up", "w_mlp_down")
PACK_ROWS = 1152


def _pack(arrs):
    flat = jnp.concatenate([a.reshape(-1) for a in arrs])
    return jnp.pad(flat, (0, PACK_ROWS * LANES - flat.shape[0])).reshape(PACK_ROWS, LANES)


def _unpack(packed, like):
    flat = packed.reshape(-1)
    out, pos = [], 0
    for a in like:
        out.append(flat[pos:pos + a.size].reshape(a.shape))
        pos += a.size
    return out


def _small_grads(cot, disc_in, p):
    lr, li, ldt, br, bi = disc_in
    group_sum = (lax.broadcasted_iota(jnp.int32, (N_GROUPS, N_GROUPS * N_STATE), 1) // N_STATE
                 == lax.broadcasted_iota(jnp.int32, (N_GROUPS, N_GROUPS * N_STATE), 0)).astype(F32)
    dlr, dli, dldt, dbr, dbi = _disc_bwd(lr, li, ldt, br, bi, cot["ab_r"], cot["ab_i"], cot["bb_r"], cot["bb_i"], group_sum)
    g = dict(cot)
    g.update(ssm_a_re=dlr, ssm_a_im=dli, ssm_log_dt=dldt[:, 0], ssm_b_re=dbr, ssm_b_im=dbi)
    return {n: g[n].reshape(p[n].shape) for n in SMALL_NAMES}


BIG = {
    "w_in": ((D_MODEL, PROJ_W), 1, PROJ_W // 4, 0, D_MODEL // 2),
    "glu_w": ((SSM_W, SSM_W), 0, SSM_W // 4, 1, SSM_W // 2),
    "w_out": ((D_MODEL, D_MODEL), 0, D_MODEL // 4, 1, D_MODEL // 2),
    "w_mlp_up": ((D_MODEL, D_FF), 1, D_FF // 4, 0, D_MODEL // 2),
    "w_mlp_down": ((D_FF, D_MODEL), 0, D_FF // 4, 1, D_MODEL // 2),
}
N_BIG = len(BIG_NAMES)
N_CHIPS = 4
ANY = pl.BlockSpec(memory_space=pl.ANY)


def _cut(name, shard=False, half=False):
    shape, s_ax, s_sz, h_ax, h_sz = BIG[name]
    shape = list(shape)
    if shard:
        shape[s_ax] = s_sz
    if half:
        shape[h_ax] = h_sz
    return tuple(shape)


def _window(name, base, shard=None, half=None):
    _, s_ax, s_sz, h_ax, h_sz = BIG[name]
    idx = [pl.ds(0, base[0]), pl.ds(0, base[1])]
    if shard is not None:
        idx[s_ax] = pl.ds(pl.multiple_of(shard * s_sz, s_sz), s_sz)
    if half is not None:
        idx[h_ax] = pl.ds(pl.multiple_of(half * h_sz, h_sz), h_sz)
    return tuple(idx)


def _mesh_pos():
    return lax.axis_index("x"), lax.axis_index("y"), lax.axis_index("c")


def _other_chips(x, y):
    return [(1 - x, y, 2 * (1 - x) + y), (x, 1 - y, 2 * x + 1 - y), (1 - x, 1 - y, 2 * (1 - x) + 1 - y)]


def _remote(src, dst, send_sem, recv_sem, dev):
    return pltpu.make_async_remote_copy(src_ref=src, dst_ref=dst, send_sem=send_sem, recv_sem=recv_sem,
                                        device_id=dev, device_id_type=MESH)


def _start_remote(src, dst, send_sem, recv_sem, dev):
    cp = _remote(src, dst, send_sem, recv_sem, dev)
    cp.start()
    return cp


class _Gather:
    def __init__(self, names):
        self.names = tuple(names)
        self.n = len(self.names)

    def in_specs(self):
        return [pl.BlockSpec(memory_space=pltpu.VMEM)] * self.n

    def out_specs(self):
        return [ANY] * self.n

    def out_shapes(self):
        return [jax.ShapeDtypeStruct(BIG[w][0], BF16) for w in self.names]

    def scratch_shapes(self):
        n_sem = (N_CHIPS - 1) * self.n
        return ([pltpu.VMEM(_cut(w, shard=True), BF16) for w in self.names]
                + [pltpu.SemaphoreType.DMA((n_sem,))] * 4 + [pltpu.SemaphoreType.DMA((self.n,))])

    def bind(self, ins, outs, scratch):
        self.ins, self.outs = ins, outs
        self.stage = scratch[:self.n]
        self.send, self.recv, self.fsend, self.frecv, self.lsem = scratch[self.n:]

    def _copies(self):
        x, y, c = _mesh_pos()
        me = 2 * x + y
        sib = (x, y, 1 - c)
        local, sends, lands, fwds, flands = [], [], [], [], []
        for w, n in enumerate(self.names):
            local.append(pltpu.make_async_copy(self.stage[w], self.outs[w].at[_window(n, BIG[n][0], shard=me)], self.lsem.at[w]))
        for k, (px, py, pj) in enumerate(_other_chips(x, y)):
            for w, n in enumerate(self.names):
                s = k * self.n + w
                sends.append(_remote(self.stage[w].at[_window(n, _cut(n, shard=True), half=c)],
                                     self.outs[w].at[_window(n, BIG[n][0], shard=me, half=c)],
                                     self.send.at[s], self.recv.at[s], (px, py, c)))
                got = self.outs[w].at[_window(n, BIG[n][0], shard=pj, half=c)]
                lands.append(_remote(got, got, self.send.at[s], self.recv.at[s], (px, py, c)))
                fwds.append(_remote(got, got, self.fsend.at[s], self.frecv.at[s], sib))
                theirs = self.outs[w].at[_window(n, BIG[n][0], shard=pj, half=1 - c)]
                flands.append(_remote(theirs, theirs, self.fsend.at[s], self.frecv.at[s], sib))
        return local, sends, lands, fwds, flands

    def start(self):
        for w in range(self.n):
            self.stage[w][...] = self.ins[w][...].astype(BF16)
        local, sends, _, _, _ = self._copies()
        for cp in local + sends:
            cp.start()

    def forward(self):
        _, _, lands, fwds, _ = self._copies()
        for land, fwd in zip(lands, fwds):
            land.wait_recv()
            fwd.start()

    def finish(self):
        local, sends, _, fwds, flands = self._copies()
        for cp in flands:
            cp.wait_recv()
        for cp in sends + fwds:
            cp.wait_send()
        for cp in local:
            cp.wait()


def _gather_weights(shards, names):
    g = _Gather(names)

    def body(*refs):
        g.bind(refs[0:g.n], refs[g.n:2 * g.n], refs[2 * g.n:])
        g.start()
        g.forward()
        g.finish()

    return pl.pallas_call(
        body, name="gather_" + "_".join(names), in_specs=g.in_specs(), out_specs=g.out_specs(), out_shape=g.out_shapes(),
        scratch_shapes=g.scratch_shapes(), compiler_params=pltpu.CompilerParams(vmem_limit_bytes=VMEM_LIMIT),
    )(*[shards[n] for n in names])


def _pair_exchange(grads, names, packed=None):
    n_big = len(names)
    n_all = n_big + (packed is not None)

    def body(*refs):
        ins, got = refs[0:n_all], refs[n_all:2 * n_all]
        send, recv = refs[2 * n_all:]
        x, y, c = _mesh_pos()
        sib = (x, y, 1 - c)
        copies = []
        for w, n in enumerate(names):
            copies.append(_start_remote(ins[w].at[_window(n, BIG[n][0], half=1 - c)], got[w], send.at[w], recv.at[w], sib))
        if packed is not None:
            copies.append(_start_remote(ins[n_big], got[n_big], send.at[n_big], recv.at[n_big], sib))
        for cp in copies:
            cp.wait()

    shapes = [jax.ShapeDtypeStruct(_cut(n, half=True), F32) for n in names]
    args = [grads[n] for n in names]
    if packed is not None:
        shapes.append(jax.ShapeDtypeStruct(packed.shape, F32))
        args.append(packed)
    return pl.pallas_call(
        body, name="grad_pair_exchange_" + "_".join(names), in_specs=[ANY] * n_all, out_specs=[ANY] * n_all, out_shape=shapes,
        scratch_shapes=[pltpu.SemaphoreType.DMA((n_all,)), pltpu.SemaphoreType.DMA((n_all,))],
    )(*args)


def _pair_sum(name, full, got, core):
    _, _, _, h_ax, _ = BIG[name]
    rows, cols = _cut(name, half=True)
    br = _row_block(rows, 512)
    nb = rows // br
    own_map = (lambda i, c: (i + c[0] * nb, 0)) if h_ax == 0 else (lambda i, c: (i, c[0]))

    def body(c_ref, own_ref, got_ref, o_ref):
        o_ref[...] = (own_ref[...] + got_ref[...]).astype(BF16)

    plain = pl.BlockSpec((br, cols), lambda i, c: (i, 0))
    return pl.pallas_call(
        body, name="pair_sum_" + name,
        grid_spec=pltpu.PrefetchScalarGridSpec(num_scalar_prefetch=1, grid=(nb,),
                                               in_specs=[pl.BlockSpec((br, cols), own_map), plain], out_specs=plain),
        out_shape=jax.ShapeDtypeStruct((rows, cols), BF16), compiler_params=_params("arbitrary"),
    )(core, full, got)


class _ChipExchange:
    def __init__(self, names, packed_shape=None):
        self.names = tuple(names)
        self.packed_shape = packed_shape
        self.n = len(self.names) + (packed_shape is not None)

    def in_specs(self):
        return [ANY] * self.n

    def out_specs(self):
        return [ANY] * self.n

    def out_shapes(self):
        shapes = [jax.ShapeDtypeStruct((N_CHIPS,) + _cut(w, shard=True, half=True), BF16) for w in self.names]
        if self.packed_shape is not None:
            shapes.append(jax.ShapeDtypeStruct((N_CHIPS,) + tuple(self.packed_shape), F32))
        return shapes

    def scratch_shapes(self):
        n_sem = (N_CHIPS - 1) * self.n
        return [pltpu.SemaphoreType.DMA((n_sem,)), pltpu.SemaphoreType.DMA((n_sem,))]

    def bind(self, ins, outs, scratch):
        self.ins, self.outs = ins, outs
        self.send, self.recv = scratch

    def _piece(self, w, shard):
        if w >= len(self.names):
            return self.ins[w]
        n = self.names[w]
        return self.ins[w].at[_window(n, _cut(n, half=True), shard=shard)]

    def _copies(self):
        x, y, c = _mesh_pos()
        me = 2 * x + y
        sends, lands = [], []
        for k, (px, py, pj) in enumerate(_other_chips(x, y)):
            for w in range(self.n):
                s = k * self.n + w
                sends.append(_remote(self._piece(w, pj), self.outs[w].at[me], self.send.at[s], self.recv.at[s], (px, py, c)))
                lands.append(_remote(self._piece(w, me), self.outs[w].at[pj], self.send.at[s], self.recv.at[s], (px, py, c)))
        return sends, lands

    def start(self):
        for cp in self._copies()[0]:
            cp.start()

    def finish(self):
        sends, lands = self._copies()
        for cp in lands:
            cp.wait_recv()
        for cp in sends:
            cp.wait_send()


def _chip_exchange(halves, packed, names):
    ex = _ChipExchange(names, packed.shape)

    def body(*refs):
        ex.bind(refs[0:ex.n], refs[ex.n:2 * ex.n], refs[2 * ex.n:])
        ex.start()
        ex.finish()

    return pl.pallas_call(
        body, name="grad_chip_exchange", in_specs=ex.in_specs(), out_specs=ex.out_specs(), out_shape=ex.out_shapes(),
        scratch_shapes=ex.scratch_shapes(),
    )(*halves, packed)


def _chip_sum(name, own, slots, chip):
    n_slot, rows, cols = slots.shape
    br = _row_block(rows, 512)
    nb = rows // br
    if name in BIG and BIG[name][1] == 1:
        own_map = lambda i, m: (i, m[0])
    elif name in BIG:
        own_map = lambda i, m: (i + m[0] * nb, 0)
    else:
        own_map = lambda i, m: (i, 0)

    def slot_map(j):
        return lambda i, m: (jnp.where(m[0] == j, (j + 1) % n_slot, j), i, 0)

    def body(m_ref, own_ref, *refs):
        own_blk = own_ref[...].astype(F32)
        tot = None
        for j in range(n_slot):
            term = jnp.where(m_ref[0] == j, own_blk, refs[j][...].astype(F32))
            tot = term if tot is None else tot + term
        refs[n_slot][...] = tot

    in_specs = [pl.BlockSpec((br, cols), own_map)] + [pl.BlockSpec((None, br, cols), slot_map(j)) for j in range(n_slot)]
    return pl.pallas_call(
        body, name="chip_sum_" + name,
        grid_spec=pltpu.PrefetchScalarGridSpec(num_scalar_prefetch=1, grid=(nb,), in_specs=in_specs,
                                               out_specs=pl.BlockSpec((br, cols), lambda i, m: (i, 0))),
        out_shape=jax.ShapeDtypeStruct((rows, cols), F32), compiler_params=_params("arbitrary"),
    )(chip, own, *([slots] * n_slot))


def _half_exchange(pieces):
    def body(*refs):
        ins, outs = refs[0:N_BIG], refs[N_BIG:2 * N_BIG]
        send, recv = refs[2 * N_BIG:]
        x, y, c = _mesh_pos()
        sib = (x, y, 1 - c)
        copies = []
        for w, n in enumerate(BIG_NAMES):
            copies.append(_start_remote(ins[w], outs[w], send.at[w], recv.at[w], sib))
        for cp in copies:
            cp.wait()

    return pl.pallas_call(
        body, name="grad_half_exchange", in_specs=[ANY] * N_BIG, out_specs=[ANY] * N_BIG,
        out_shape=[jax.ShapeDtypeStruct(_cut(n, shard=True, half=True), F32) for n in BIG_NAMES],
        scratch_shapes=[pltpu.SemaphoreType.DMA((N_BIG,)), pltpu.SemaphoreType.DMA((N_BIG,))],
    )(*pieces)


WEIGHT_NAMES = ("norm1_g", "w_in", "q_norm_g", "k_norm_g", "ssm_a_re", "ssm_a_im", "ssm_log_dt", "ssm_b_re", "ssm_b_im",
                "ssm_c_re", "ssm_c_im", "ssm_d", "glu_w", "glu_b", "attn_out_norm_g", "ssm_out_norm_g", "w_out", "norm2_g",
                "w_mlp_up", "w_mlp_down")


def _train_step(a):
    x = a["x"][0]
    target = a["loss_target"][0]
    shards = {n: a[n][0] for n in BIG_NAMES}
    p = {n: a[n][0] for n in SMALL_NAMES}
    core = lax.axis_index("c").astype(jnp.int32).reshape(1)
    chip_id = (2 * lax.axis_index("x") + lax.axis_index("y")).astype(jnp.int32).reshape(1)

    later = ("glu_w", "w_out", "w_mlp_up", "w_mlp_down")
    early = ("w_mlp_up", "w_mlp_down", "w_out", "glu_w")
    late = ("w_in",)
    (wi,) = _gather_weights(shards, ("w_in",))
    chip = {}

    def bwd_side(grads):
        got = _pair_exchange(grads, early)
        for n, g in zip(early, got):
            chip[n] = _pair_sum(n, grads[n], g, core)
        return _ChipExchange(early), [chip[n] for n in early]

    loss, grad_x, big, cot, disc_in, early_slots = _local_step(
        x, target, wi, None, p, fwd_side=(_Gather(later), [shards[n] for n in later]), bwd_side=bwd_side)
    slots = dict(zip(early, early_slots))

    cot_list = [cot[n] for n in COT_NAMES]
    packed = _pack(cot_list)
    *got, got_packed = _pair_exchange(big, late, packed)
    for n, g in zip(late, got):
        chip[n] = _pair_sum(n, big[n], g, core)
    chip_packed = _sum_arrays([packed, got_packed], "pair_sum_small")
    *late_slots, small_slots = _chip_exchange([chip[n] for n in late], chip_packed, late)
    slots.update(zip(late, late_slots))
    pieces = [_chip_sum(n, chip[n], slots[n], chip_id) for n in BIG_NAMES]
    small_sum = _chip_sum("small", chip_packed, small_slots, chip_id)
    shard_grads = {}
    for n, mine, theirs in zip(BIG_NAMES, pieces, _half_exchange(pieces)):
        h_ax = BIG[n][3]
        shard_grads[n] = jnp.where(core[0] == 0, jnp.concatenate([mine, theirs], axis=h_ax),
                                   jnp.concatenate([theirs, mine], axis=h_ax))
    small_grads = _small_grads(dict(zip(COT_NAMES, _unpack(small_sum, cot_list))), disc_in, p)

    grads, delta, new_m, new_v = {}, {}, {}, {}
    for n in BIG_NAMES:
        grads[n] = shard_grads[n]
        delta[n], new_m[n], new_v[n] = _adamw(a[n][0], grads[n], a["m_" + n][0], a["v_" + n][0], "adamw_" + n)
    flat2 = lambda t: t.reshape(-1, t.shape[-1])
    res = _adamw_many([flat2(p[n]) for n in SMALL_NAMES], [flat2(small_grads[n]) for n in SMALL_NAMES],
                      [flat2(a["m_" + n][0]) for n in SMALL_NAMES], [flat2(a["v_" + n][0]) for n in SMALL_NAMES])
    for store, outs in zip((delta, new_m, new_v), res):
        store.update(zip(SMALL_NAMES, outs))
    grads.update(small_grads)

    total = lax.psum(loss, ("x", "y", "c"))
    out = [total, grad_x[None]]
    for store in (grads, delta, new_m, new_v):
        out += [store[n].reshape(a[n].shape) for n in WEIGHT_NAMES]
    return tuple(out)


def kernel(x, norm1_g, w_in, q_norm_g, k_norm_g, ssm_a_re, ssm_a_im, ssm_log_dt, ssm_b_re, ssm_b_im, ssm_c_re, ssm_c_im, ssm_d, glu_w, glu_b, attn_out_norm_g, ssm_out_norm_g, w_out, norm2_g, w_mlp_up, w_mlp_down, loss_target, m_norm1_g, m_w_in, m_q_norm_g, m_k_norm_g, m_ssm_a_re, m_ssm_a_im, m_ssm_log_dt, m_ssm_b_re, m_ssm_b_im, m_ssm_c_re, m_ssm_c_im, m_ssm_d, m_glu_w, m_glu_b, m_attn_out_norm_g, m_ssm_out_norm_g, m_w_out, m_norm2_g, m_w_mlp_up, m_w_mlp_down, v_norm1_g, v_w_in, v_q_norm_g, v_k_norm_g, v_ssm_a_re, v_ssm_a_im, v_ssm_log_dt, v_ssm_b_re, v_ssm_b_im, v_ssm_c_re, v_ssm_c_im, v_ssm_d, v_glu_w, v_glu_b, v_attn_out_norm_g, v_ssm_out_norm_g, v_w_out, v_norm2_g, v_w_mlp_up, v_w_mlp_down):
    return _train_step(dict(locals()))
```

```python
import functools
import math

import jax
import jax.numpy as jnp
from jax import lax
from jax.experimental import pallas as pl
from jax.experimental.pallas import tpu as pltpu

F32 = jnp.float32
BF16 = jnp.bfloat16
MESH = pl.DeviceIdType.MESH

D_MODEL = 1024
ATTN_W = 512
SSM_W = 512
HEAD = 64
D_FF = 4096
PROJ_W = 2048
N_GROUPS = 32
N_STATE = 64
GROUP_W = 16
EPS = 1e-6
NEG = -1e30
DILATIONS = (1, 4, 16)
BLK = 128
TILE = 2048
LANES = 128
MXU_WIDTH = 256
N_LB = SSM_W // LANES
VMEM_LIMIT = 56 * 1024 * 1024

ADAM_LR, ADAM_B1, ADAM_B2, ADAM_EPS, ADAM_WD, ADAM_STEP = 0.001, 0.9, 0.999, 1e-08, 0.01, 10


def _params(*sem):
    return pltpu.CompilerParams(dimension_semantics=sem, vmem_limit_bytes=VMEM_LIMIT)


def _nt(a, b):
    return lax.dot_general(a, b, (((1,), (1,)), ((), ())), preferred_element_type=F32)


def _tn(a, b):
    return lax.dot_general(a, b, (((0,), (0,)), ((), ())), preferred_element_type=F32)


def _mm(a, b):
    return jnp.dot(a, b, preferred_element_type=F32)


def _group_mean(t, ones_bd, width):
    span = ones_bd.shape[0]
    hi = t.astype(BF16)
    lo = (t - hi.astype(F32)).astype(BF16)
    parts = [_mm(hi[:, k:k + span], ones_bd) + _mm(lo[:, k:k + span], ones_bd) for k in range(0, t.shape[1], span)]
    return (parts[0] if len(parts) == 1 else jnp.concatenate(parts, axis=1)) * (1.0 / width)


def _rms(x):
    return lax.rsqrt(jnp.mean(x * x, axis=-1, keepdims=True) + EPS)


def _rms_bwd(dy, x, r, g):
    xh = x * r
    dxh = dy * g
    dx = r * (dxh - xh * jnp.mean(dxh * xh, axis=-1, keepdims=True))
    return dx, dy * xh


def _colsum(x):
    return jnp.sum(x, axis=0, keepdims=True)


def _row_block(rows, cap):
    for b in range(min(rows, cap) // 8 * 8, 0, -8):
        if rows % b == 0:
            return b
    raise ValueError(f"no row block for {rows} rows")


def _inproj_fwd(x, g1, wi, gq, gk, ones64):
    t_len = x.shape[0]
    tm = 512
    n_hp = ATTN_W // LANES

    def body(x_ref, g1_ref, wi_ref, gq_ref, gk_ref, bd_ref, xn_ref, q_ref, k_ref, v_ref, u_ref, qr_ref, kr_ref):
        xv = x_ref[...]
        xn = (xv * _rms(xv) * g1_ref[...]).astype(BF16)
        xn_ref[...] = xn
        proj = _mm(xn, wi_ref[...])
        q = proj[:, 0:ATTN_W]
        k = proj[:, ATTN_W:2 * ATTN_W]
        v = proj[:, 2 * ATTN_W:3 * ATTN_W]
        u_ref[...] = proj[:, 3 * ATTN_W:]
        qr_ref[...] = q
        kr_ref[...] = k
        bd = bd_ref[...]
        qn = q * lax.rsqrt(_group_mean(q * q, bd, HEAD) + EPS) * gq_ref[...] * (HEAD ** -0.5)
        kn = k * lax.rsqrt(_group_mean(k * k, bd, HEAD) + EPS) * gk_ref[...]
        for hp in range(n_hp):
            sl = slice(hp * LANES, (hp + 1) * LANES)
            q_ref[hp] = qn[:, sl]
            k_ref[hp] = kn[:, sl]
            v_ref[hp] = v[:, sl]

    row = lambda i: (i, 0)
    const = lambda i: (0, 0)
    hp_spec = pl.BlockSpec((n_hp, tm, LANES), lambda i: (0, i, 0))
    hp_shape = jax.ShapeDtypeStruct((n_hp, t_len, LANES), F32)
    return pl.pallas_call(
        body, name="inproj_fwd", grid=(t_len // tm,),
        in_specs=[pl.BlockSpec((tm, D_MODEL), row), pl.BlockSpec((1, D_MODEL), const),
                  pl.BlockSpec((D_MODEL, PROJ_W), const), pl.BlockSpec((1, ATTN_W), const),
                  pl.BlockSpec((1, ATTN_W), const), pl.BlockSpec(ones64.shape, const)],
        out_specs=[pl.BlockSpec((tm, D_MODEL), row), hp_spec, hp_spec, hp_spec,
                   pl.BlockSpec((tm, SSM_W), row), pl.BlockSpec((tm, ATTN_W), row), pl.BlockSpec((tm, ATTN_W), row)],
        out_shape=[jax.ShapeDtypeStruct((t_len, D_MODEL), BF16), hp_shape, hp_shape, hp_shape,
                   jax.ShapeDtypeStruct((t_len, SSM_W), F32), jax.ShapeDtypeStruct((t_len, ATTN_W), F32),
                   jax.ShapeDtypeStruct((t_len, ATTN_W), F32)],
        compiler_params=_params("arbitrary"),
    )(x, g1, wi, gq, gk, ones64)


def _attn_masks():
    head0 = lax.broadcasted_iota(jnp.int32, (BLK, LANES), 1) < HEAD
    row = lax.broadcasted_iota(jnp.int32, (2 * BLK, 2 * BLK), 0) & (BLK - 1)
    col = lax.broadcasted_iota(jnp.int32, (2 * BLK, 2 * BLK), 1)
    return head0, (col < BLK) & (col >= row), (col >= BLK) & (col - BLK <= row)


def _stack_heads(x, head0):
    return jnp.concatenate([jnp.where(head0, x, 0.0), jnp.where(head0, 0.0, x)], axis=0).astype(BF16)


def _unit_rows(uidx, d):
    nb = TILE // (BLK * d)
    r = lax.div(uidx, nb)
    b = lax.rem(uidx, nb)
    start = r + d * BLK * b
    if d == 1:
        start = pl.multiple_of(start, BLK)
        mk = lambda s: pl.ds(pl.multiple_of(s, BLK), BLK)
    else:
        mk = lambda s: pl.ds(s, BLK, stride=d)
    return b, mk(start), mk(TILE + start), mk(TILE + start - d * BLK)


def _attn_fwd(q, k, v, side=None, side_args=()):
    n_hp, t_len, _ = q.shape
    nt = t_len // TILE
    ns = side.n if side is not None else 0
    n_steps = n_hp * nt

    def body(*refs):
        q_ref, kp_ref, kc_ref, vp_ref, vc_ref = refs[0:5]
        o_ref, lse_ref = refs[5 + ns:7 + ns]
        kk, vv, m_s, l_s, acc_s = refs[7 + 2 * ns:12 + 2 * ns]
        t = pl.program_id(1)
        step = pl.program_id(0) * nt + t
        if side is not None:
            side.bind(refs[5:5 + ns], refs[7 + ns:7 + 2 * ns], refs[12 + 2 * ns:])
            pl.when(step == 0)(side.start)
            pl.when(step == n_steps // 2)(side.forward)
        kk[0:TILE] = kp_ref[0]
        kk[TILE:] = kc_ref[0]
        vv[0:TILE] = vp_ref[0]
        vv[TILE:] = vc_ref[0]
        head0, band_prev, band_cur = _attn_masks()

        for pi, d in enumerate(DILATIONS):
            def unit(uidx, carry, d=d, pi=pi):
                b, rows_q, rows_c, rows_p = _unit_rows(uidx, d)
                mask = band_cur | (band_prev & ((t > 0) | (b > 0)))
                q2 = _stack_heads(q_ref.at[0][rows_q, :], head0)
                kcat = jnp.concatenate([kk[rows_p, :], kk[rows_c, :]], axis=0).astype(BF16)
                vcat = jnp.concatenate([vv[rows_p, :], vv[rows_c, :]], axis=0).astype(BF16)
                s = jnp.where(mask, _nt(q2, kcat), NEG)
                m = jnp.max(s, axis=1, keepdims=True)
                p = jnp.exp(s - m)
                ls = jnp.sum(p, axis=1, keepdims=True)
                pv = _mm(p.astype(BF16), vcat)
                m_s.at[pi][rows_q, :] = jnp.where(head0, m[0:BLK], m[BLK:])
                l_s.at[pi][rows_q, :] = jnp.where(head0, ls[0:BLK], ls[BLK:])
                acc_s.at[pi][rows_q, :] = jnp.where(head0, pv[0:BLK], pv[BLK:])
                return carry

            lax.fori_loop(0, TILE // BLK, unit, 0, unroll=8)

        m_all = jnp.maximum(jnp.maximum(m_s[0], m_s[1]), m_s[2])
        num = jnp.zeros((TILE, LANES), F32)
        den = jnp.zeros((TILE, LANES), F32)
        for pi in range(len(DILATIONS)):
            wgt = jnp.exp(m_s[pi] - m_all)
            num = num + acc_s[pi] * wgt
            den = den + l_s[pi] * wgt
        o_ref[...] = num / den
        lse_ref[0] = m_all + jnp.log(den)
        if side is not None:
            pl.when(step == n_steps - 1)(side.finish)

    cur = lambda hp, t: (hp, t, 0)
    prev = lambda hp, t: (hp, jnp.maximum(t - 1, 0), 0)
    blk = (1, TILE, LANES)
    per_pattern = pltpu.VMEM((len(DILATIONS), TILE, LANES), F32)
    extra = (side.in_specs(), side.out_specs(), side.out_shapes(), side.scratch_shapes()) if side is not None else ([], [], [], [])
    return pl.pallas_call(
        body, name="attn_fwd", grid=(n_hp, nt),
        in_specs=[pl.BlockSpec(blk, cur), pl.BlockSpec(blk, prev), pl.BlockSpec(blk, cur),
                  pl.BlockSpec(blk, prev), pl.BlockSpec(blk, cur)] + extra[0],
        out_specs=[pl.BlockSpec((TILE, LANES), lambda hp, t: (t, hp)), pl.BlockSpec(blk, cur)] + extra[1],
        out_shape=[jax.ShapeDtypeStruct((t_len, ATTN_W), F32), jax.ShapeDtypeStruct((n_hp, t_len, LANES), F32)] + extra[2],
        scratch_shapes=[pltpu.VMEM((2 * TILE, LANES), F32), pltpu.VMEM((2 * TILE, LANES), F32),
                        per_pattern, per_pattern, per_pattern] + extra[3],
        compiler_params=_params("arbitrary", "arbitrary"),
    )(q, k, k, v, v, *side_args)


def _attn_bwd(q, k, v, o, do, lse, ones_hp, side=None, side_args=()):
    n_hp, t_len, _ = q.shape
    nt = t_len // TILE
    ns = side.n if side is not None else 0
    n_pat = len(DILATIONS)

    def body(*refs):
        q_ref, kp_ref, kc_ref, vp_ref, vc_ref, o_ref, do_ref, lse_ref, bd_ref = refs[0:9]
        dq_ref, dk_ref, dv_ref = refs[9 + ns:12 + ns]
        kk, vv, dq_s, dkc, dkp, dvc, dvp, hold_k, hold_v, dl_s = refs[12 + 2 * ns:22 + 2 * ns]
        t = pl.program_id(1)
        if side is not None:
            side.bind(refs[9:9 + ns], refs[12 + ns:12 + 2 * ns], refs[22 + 2 * ns:])
            pl.when((pl.program_id(0) == 0) & (t == 0))(side.start)

        @pl.when(t < nt)
        def _():
            kk[0:TILE] = kp_ref[0]
            kk[TILE:] = kc_ref[0]
            vv[0:TILE] = vp_ref[0]
            vv[TILE:] = vc_ref[0]
            dl_s[...] = _group_mean(do_ref[...] * o_ref[...], bd_ref[...], 1.0)
            head0, band_prev, band_cur = _attn_masks()

            for pi, d in enumerate(DILATIONS):
                def unit(uidx, carry, d=d, pi=pi):
                    b, rows_q, rows_c, rows_p = _unit_rows(uidx, d)
                    mask = band_cur | (band_prev & ((t > 0) | (b > 0)))
                    q2 = _stack_heads(q_ref.at[0][rows_q, :], head0)
                    do2 = _stack_heads(do_ref[rows_q, :], head0)
                    lse_f = lse_ref.at[0][rows_q, :]
                    dl_f = dl_s[rows_q, :]
                    lse2 = jnp.concatenate([lse_f[:, 0:1], lse_f[:, HEAD:HEAD + 1]], axis=0)
                    dl2 = jnp.concatenate([dl_f[:, 0:1], dl_f[:, HEAD:HEAD + 1]], axis=0)
                    kcat = jnp.concatenate([kk[rows_p, :], kk[rows_c, :]], axis=0).astype(BF16)
                    vcat = jnp.concatenate([vv[rows_p, :], vv[rows_c, :]], axis=0).astype(BF16)
                    p = jnp.where(mask, jnp.exp(_nt(q2, kcat) - lse2), 0.0)
                    ds = (p * (_nt(do2, vcat) - dl2)).astype(BF16)
                    dq2 = _mm(ds, kcat)
                    dq_s.at[pi][rows_q, :] = jnp.where(head0, dq2[0:BLK], dq2[BLK:])
                    dk2 = _tn(ds, q2)
                    dv2 = _tn(p.astype(BF16), do2)
                    dkp.at[pi][rows_q, :] = dk2[0:BLK]
                    dkc.at[pi][rows_q, :] = dk2[BLK:]
                    dvp.at[pi][rows_q, :] = dv2[0:BLK]
                    dvc.at[pi][rows_q, :] = dv2[BLK:]
                    return carry

                lax.fori_loop(0, TILE // BLK, unit, 0, unroll=8)

            dq_ref[...] = dq_s[0] + dq_s[1] + dq_s[2]

        @pl.when(t > 0)
        def _():
            dk_ref[...] = hold_k[...]
            dv_ref[...] = hold_v[...]

        @pl.when((t > 0) & (t < nt))
        def _():
            for pi, d in enumerate(DILATIONS):
                back = d * BLK
                dk_ref[TILE - back:, :] = dk_ref[TILE - back:, :] + dkp[pi, 0:back, :]
                dv_ref[TILE - back:, :] = dv_ref[TILE - back:, :] + dvp[pi, 0:back, :]

        @pl.when(t < nt)
        def _():
            hold_k[...] = dkc[0] + dkc[1] + dkc[2]
            hold_v[...] = dvc[0] + dvc[1] + dvc[2]
            for pi, d in enumerate(DILATIONS):
                back = d * BLK
                if back < TILE:
                    hold_k[0:TILE - back, :] = hold_k[0:TILE - back, :] + dkp[pi, back:, :]
                    hold_v[0:TILE - back, :] = hold_v[0:TILE - back, :] + dvp[pi, back:, :]

        if side is not None:
            pl.when((pl.program_id(0) == n_hp - 1) & (t == nt))(side.finish)

    last = nt - 1
    extra = (side.in_specs(), side.out_specs(), side.out_shapes(), side.scratch_shapes()) if side is not None else ([], [], [], [])
    cur = lambda hp, t: (hp, jnp.minimum(t, last), 0)
    prev = lambda hp, t: (hp, jnp.clip(t - 1, 0, last), 0)
    cur2 = lambda hp, t: (jnp.minimum(t, last), hp)
    prev2 = lambda hp, t: (jnp.maximum(t - 1, 0), hp)
    blk = (1, TILE, LANES)
    blk2 = (TILE, LANES)
    out = jax.ShapeDtypeStruct((t_len, ATTN_W), F32)
    return pl.pallas_call(
        body, name="attn_bwd", grid=(n_hp, nt + 1),
        in_specs=[pl.BlockSpec(blk, cur), pl.BlockSpec(blk, prev), pl.BlockSpec(blk, cur),
                  pl.BlockSpec(blk, prev), pl.BlockSpec(blk, cur), pl.BlockSpec(blk2, cur2),
                  pl.BlockSpec(blk2, cur2), pl.BlockSpec(blk, cur), pl.BlockSpec((LANES, LANES), lambda hp, t: (0, 0))]
        + extra[0],
        out_specs=[pl.BlockSpec(blk2, cur2), pl.BlockSpec(blk2, prev2), pl.BlockSpec(blk2, prev2)] + extra[1],
        out_shape=[out, out, out] + extra[2],
        scratch_shapes=[pltpu.VMEM((2 * TILE, LANES), F32), pltpu.VMEM((2 * TILE, LANES), F32)]
        + [pltpu.VMEM((n_pat, TILE, LANES), F32)] * 5 + [pltpu.VMEM((TILE, LANES), F32)] * 3 + extra[3],
        compiler_params=_params("arbitrary", "arbitrary"),
    )(q, k, k, v, v, o, do, lse, ones_hp, *side_args)


def _discretise(lr, li, ldt, br, bi):
    dt = jnp.exp(ldt)
    mag = jnp.exp(lr * dt)
    ab_r, ab_i = mag * jnp.cos(li * dt), mag * jnp.sin(li * dt)
    den = lr * lr + li * li
    nr, ni = ab_r - 1.0, ab_i
    cr = (nr * lr + ni * li) / den
    ci = (ni * lr - nr * li) / den
    return ab_r, ab_i, cr * br - ci * bi, cr * bi + ci * br


def _disc_fwd(lr, li, ldt, br, bi):
    def body(lr_ref, li_ref, ldt_ref, br_ref, bi_ref, ar_o, ai_o, bbr_o, bbi_o):
        outs = _discretise(lr_ref[...], li_ref[...], ldt_ref[...], br_ref[...], bi_ref[...])
        for o_ref, val in zip((ar_o, ai_o, bbr_o, bbi_o), outs):
            o_ref[...] = val

    col = jax.ShapeDtypeStruct(lr.shape, F32)
    mat = jax.ShapeDtypeStruct(br.shape, F32)
    return pl.pallas_call(body, name="s5_disc_fwd", out_shape=[col, col, mat, mat])(lr, li, ldt, br, bi)


def _disc_bwd(lr, li, ldt, br, bi, d_ar, d_ai, d_bbr, d_bbi, group_sum):
    def body(lr_ref, li_ref, ldt_ref, br_ref, bi_ref, c1, c2, c3, c4, gs_ref, dlr_o, dli_o, dldt_o, dbr_o, dbi_o):
        _, vjp = jax.vjp(_discretise, lr_ref[...], li_ref[...], ldt_ref[...], br_ref[...], bi_ref[...])
        dlr, dli, dldt, dbr, dbi = vjp((c1[...], c2[...], c3[...], c4[...]))
        dlr_o[...] = dlr
        dli_o[...] = dli
        dbr_o[...] = dbr
        dbi_o[...] = dbi
        wide = jnp.broadcast_to(dldt, (dldt.shape[0], LANES))
        dldt_o[...] = jnp.dot(gs_ref[...], wide, precision=lax.Precision.HIGHEST, preferred_element_type=F32)

    col = jax.ShapeDtypeStruct(lr.shape, F32)
    mat = jax.ShapeDtypeStruct(br.shape, F32)
    return pl.pallas_call(
        body, name="s5_disc_bwd", out_shape=[col, col, jax.ShapeDtypeStruct((N_GROUPS, LANES), F32), mat, mat],
    )(lr, li, ldt, br, bi, d_ar, d_ai, d_bbr, d_bbi, group_sum)


N_CHUNK = TILE // BLK
HALF = 4


def _cmul(ar, ai, xr, xi):
    return ar * xr - ai * xi, ar * xi + ai * xr


def _power_table(a_ref, tab, sign, reverse):
    ar = [a_ref[0, j:j + 1, :] for j in range(HALF)]
    ai = [sign * a_ref[0, HALF + j:HALF + j + 1, :] for j in range(HALF)]

    def step(s, cur):
        row = pl.ds((BLK - 1 - s) if reverse else s, 1)
        nxt = []
        for j in range(HALF):
            tab.at[j][row, :] = cur[j]
            tab.at[HALF + j][row, :] = cur[HALF + j]
            nxt.append(_cmul(ar[j], ai[j], cur[j], cur[HALF + j]))
        return tuple(p[0] for p in nxt) + tuple(p[1] for p in nxt)

    lax.fori_loop(0, BLK, step, tuple(ar) + tuple(ai))


def _interleave(src, dst):
    for c in range(N_CHUNK):
        dst[pl.ds(c, BLK, stride=N_CHUNK), :] = src[c * BLK:(c + 1) * BLK, :]


def _deinterleave(src, dst):
    for c in range(N_CHUNK):
        dst[c * BLK:(c + 1) * BLK, :] = src[pl.ds(c, BLK, stride=N_CHUNK), :]


def _step_rows(s):
    return pl.ds(pl.multiple_of(s * N_CHUNK, N_CHUNK), N_CHUNK)


def _chunk_scan(buf, a_ref, sign, reverse):
    ar = [jnp.broadcast_to(a_ref[0, j:j + 1, :], (N_CHUNK, LANES)) for j in range(HALF)]
    ai = [sign * jnp.broadcast_to(a_ref[0, HALF + j:HALF + j + 1, :], (N_CHUNK, LANES)) for j in range(HALF)]

    def step(i, carry):
        s = (BLK - 1 - i) if reverse else i
        rows = _step_rows(s)
        out = []
        for j in range(HALF):
            pr, pi = _cmul(ar[j], ai[j], carry[j], carry[HALF + j])
            xr = buf.at[j][rows, :] + pr
            xi = buf.at[HALF + j][rows, :] + pi
            buf.at[j][rows, :] = xr
            buf.at[HALF + j][rows, :] = xi
            out.append((xr, xi))
        return tuple(p[0] for p in out) + tuple(p[1] for p in out)

    zero = jnp.zeros((N_CHUNK, LANES), F32)
    lax.fori_loop(0, BLK, step, (zero,) * (2 * HALF), unroll=2)


def _chunk_states(buf, carry_s, xin_s, tab, reverse):
    edge = 0 if reverse else BLK - 1
    top = 0 if reverse else BLK - 1
    pw = [tab[j, top:top + 1, :] for j in range(2 * HALF)]
    cur = [carry_s[j:j + 1, :] for j in range(2 * HALF)]
    summary = [buf[j, edge * N_CHUNK:(edge + 1) * N_CHUNK, :] for j in range(2 * HALF)]
    order = range(N_CHUNK - 1, -1, -1) if reverse else range(N_CHUNK)
    for c in order:
        for j in range(2 * HALF):
            xin_s[j, c:c + 1, :] = cur[j]
        nxt = []
        for j in range(HALF):
            pr, pi = _cmul(pw[j], pw[HALF + j], cur[j], cur[HALF + j])
            nxt.append((pr + summary[j][c:c + 1, :], pi + summary[HALF + j][c:c + 1, :]))
        cur = [p[0] for p in nxt] + [p[1] for p in nxt]
    for j in range(2 * HALF):
        carry_s[j:j + 1, :] = cur[j]


def _s5_fwd(u, a_cat, b_mat, c_mat, d_skip):
    t_len = u.shape[0]
    nt = t_len // TILE

    def body(u_ref, a_ref, b_ref, c_ref, d_ref, y_ref, x_ref, xs, us, tab, carry_s, xin_s):
        sb = pl.program_id(1)

        @pl.when(sb == 0)
        def _():
            _power_table(a_ref, tab, 1.0, False)
            carry_s[...] = jnp.zeros_like(carry_s)

        _interleave(u_ref, us)
        uv = us[...]
        bu = _mm(uv.astype(BF16), b_ref[0])
        for j in range(2 * HALF):
            xs[j] = bu[:, j * LANES:(j + 1) * LANES]
        _chunk_scan(xs, a_ref, 1.0, False)
        _chunk_states(xs, carry_s, xin_s, tab, False)
        xin = [xin_s[j] for j in range(2 * HALF)]

        def fix(s, acc):
            rows = _step_rows(s)
            for j in range(HALF):
                pr, pi = _cmul(tab.at[j][pl.ds(s, 1), :], tab.at[HALF + j][pl.ds(s, 1), :], xin[j], xin[HALF + j])
                xs.at[j][rows, :] = xs.at[j][rows, :] + pr
                xs.at[HALF + j][rows, :] = xs.at[HALF + j][rows, :] + pi
            return acc

        lax.fori_loop(0, BLK, fix, 0, unroll=2)
        xcat = jnp.concatenate([xs[j].astype(BF16) for j in range(2 * HALF)], axis=1)
        x_ref[0] = xcat
        us[...] = d_ref[0] * uv + _mm(xcat, c_ref[0])
        _deinterleave(us, y_ref)

    return pl.pallas_call(
        body, name="s5_fwd", grid=(N_LB, nt),
        in_specs=[pl.BlockSpec((TILE, LANES), lambda lb, sb: (sb, lb)),
                  pl.BlockSpec((1, 2 * HALF, LANES), lambda lb, sb: (lb, 0, 0)),
                  pl.BlockSpec((1, LANES, 2 * HALF * LANES), lambda lb, sb: (lb, 0, 0)),
                  pl.BlockSpec((1, 2 * HALF * LANES, LANES), lambda lb, sb: (lb, 0, 0)),
                  pl.BlockSpec((1, 1, LANES), lambda lb, sb: (lb, 0, 0))],
        out_specs=[pl.BlockSpec((TILE, LANES), lambda lb, sb: (sb, lb)),
                   pl.BlockSpec((1, TILE, 2 * HALF * LANES), lambda lb, sb: (lb, sb, 0))],
        out_shape=[jax.ShapeDtypeStruct((t_len, SSM_W), F32), jax.ShapeDtypeStruct((N_LB, t_len, 2 * HALF * LANES), BF16)],
        scratch_shapes=[pltpu.VMEM((2 * HALF, TILE, LANES), F32), pltpu.VMEM((TILE, LANES), F32),
                        pltpu.VMEM((2 * HALF, BLK, LANES), F32),
                        pltpu.VMEM((2 * HALF, LANES), F32), pltpu.VMEM((2 * HALF, N_CHUNK, LANES), F32)],
        compiler_params=_params("arbitrary", "arbitrary"),
    )(u, a_cat, b_mat, c_mat, d_skip)


def _s5_bwd(u, dy, states, a_cat, b_mat, c_mat, d_skip):
    t_len = u.shape[0]
    nt = t_len // TILE
    last = nt - 1

    def body(u_ref, dy_ref, x_ref, a_ref, b_ref, c_ref, d_ref, du_ref, db_ref, dc_ref, da_ref, dd_ref,
             gs, us, dys, tabc, lam_s, lin_s):
        sb = pl.program_id(1)

        @pl.when(sb == 0)
        def _():
            _power_table(a_ref, tabc, -1.0, True)
            lam_s[...] = jnp.zeros_like(lam_s)
            db_ref[...] = jnp.zeros_like(db_ref)
            dc_ref[...] = jnp.zeros_like(dc_ref)
            da_ref[...] = jnp.zeros_like(da_ref)
            dd_ref[...] = jnp.zeros_like(dd_ref)

        _interleave(u_ref, us)
        _interleave(dy_ref, dys)
        uv = us[...]
        dyv = dys[...]
        ub = uv.astype(BF16)
        dyb = dyv.astype(BF16)
        gy = _nt(dyb, c_ref[0])
        for j in range(2 * HALF):
            gs[j] = gy[:, j * LANES:(j + 1) * LANES]
        _chunk_scan(gs, a_ref, -1.0, True)
        _chunk_states(gs, lam_s, lin_s, tabc, True)
        zero = jnp.zeros((N_CHUNK, LANES), F32)
        x_tile = x_ref.at[0]
        for grp in range(0, HALF, 2):
            slabs = (grp, grp + 1)
            lin = [(lin_s[j], lin_s[HALF + j]) for j in slabs]

            def fix(i, carry, slabs=slabs, lin=lin):
                s = BLK - 1 - i
                rows = _step_rows(s)
                out = []
                for k, j in enumerate(slabs):
                    nr, ni, acc_r, acc_i = carry[4 * k:4 * k + 4]
                    xr = x_tile[rows, pl.ds(j * LANES, LANES)].astype(F32)
                    xi = x_tile[rows, pl.ds((HALF + j) * LANES, LANES)].astype(F32)
                    qr, qi = _cmul(tabc.at[j][pl.ds(s, 1), :], tabc.at[HALF + j][pl.ds(s, 1), :], lin[k][0], lin[k][1])
                    lr_ = gs.at[j][rows, :] + qr
                    li_ = gs.at[HALF + j][rows, :] + qi
                    gs.at[j][rows, :] = lr_
                    gs.at[HALF + j][rows, :] = li_
                    out += [lr_, li_, acc_r + (xr * nr + xi * ni), acc_i + (xr * ni - xi * nr)]
                return tuple(out)

            init = []
            for k in range(len(slabs)):
                init += [lin[k][0], lin[k][1], zero, zero]
            res = lax.fori_loop(0, BLK, fix, tuple(init), unroll=2)
            for k, j in enumerate(slabs):
                da_ref[0, j:j + 1, :] = da_ref[0, j:j + 1, :] + _colsum(res[4 * k + 2])
                da_ref[0, HALF + j:HALF + j + 1, :] = da_ref[0, HALF + j:HALF + j + 1, :] + _colsum(res[4 * k + 3])
        lam = jnp.concatenate([gs[j].astype(BF16) for j in range(2 * HALF)], axis=1)
        us[...] = _nt(lam, b_ref[0]) + d_ref[0] * dyv
        _deinterleave(us, du_ref)
        db_ref[0] = db_ref[0] + _tn(ub, lam)
        dc_ref[0] = dc_ref[0] + _tn(dyb, x_ref[0])
        dd_ref[0] = dd_ref[0] + _colsum(dyv * uv)

    rev = lambda lb, sb: (last - sb, lb)
    per_lb = lambda lb, sb: (lb, 0, 0)
    wide = 2 * HALF * LANES
    return pl.pallas_call(
        body, name="s5_bwd", grid=(N_LB, nt),
        in_specs=[pl.BlockSpec((TILE, LANES), rev), pl.BlockSpec((TILE, LANES), rev),
                  pl.BlockSpec((1, TILE, wide), lambda lb, sb: (lb, last - sb, 0)),
                  pl.BlockSpec((1, 2 * HALF, LANES), per_lb), pl.BlockSpec((1, LANES, wide), per_lb),
                  pl.BlockSpec((1, wide, LANES), per_lb), pl.BlockSpec((1, 1, LANES), per_lb)],
        out_specs=[pl.BlockSpec((TILE, LANES), rev), pl.BlockSpec((1, LANES, wide), per_lb),
                   pl.BlockSpec((1, LANES, wide), per_lb), pl.BlockSpec((1, 2 * HALF, LANES), per_lb),
                   pl.BlockSpec((1, 1, LANES), per_lb)],
        out_shape=[jax.ShapeDtypeStruct((t_len, SSM_W), F32), jax.ShapeDtypeStruct((N_LB, LANES, wide), F32),
                   jax.ShapeDtypeStruct((N_LB, LANES, wide), F32), jax.ShapeDtypeStruct((N_LB, 2 * HALF, LANES), F32),
                   jax.ShapeDtypeStruct((N_LB, 1, LANES), F32)],
        scratch_shapes=[pltpu.VMEM((2 * HALF, TILE, LANES), F32),
                        pltpu.VMEM((TILE, LANES), F32), pltpu.VMEM((TILE, LANES), F32),
                        pltpu.VMEM((2 * HALF, BLK, LANES), F32), pltpu.VMEM((2 * HALF, LANES), F32),
                        pltpu.VMEM((2 * HALF, N_CHUNK, LANES), F32)],
        compiler_params=_params("arbitrary", "arbitrary"),
    )(u, dy, states, a_cat, b_mat, c_mat, d_skip)


_GELU_C = math.sqrt(2.0 / math.pi)
_GELU_K = 0.044715


def _gelu(y):
    t = jnp.tanh(_GELU_C * (y + _GELU_K * (y * y * y)))
    return y * (0.5 * (1.0 + t)), t


def _gelu_grad(y, t):
    return 0.5 * (1.0 + t) + 0.5 * y * (1.0 - t * t) * (_GELU_C * (1.0 + 3.0 * _GELU_K * y * y))


def _glu(y, wg, bias):
    z, t = _gelu(y)
    sg = jax.nn.sigmoid(_mm(z.astype(BF16), wg) + bias)
    return z, t, sg


def _mixer_mlp(attn, y, x, target, wg, glu_b, ga, gs, wo, g2, wu, wd):
    t_len = x.shape[0]
    tm = 256
    fc = 1024
    n_fc = D_FF // fc

    def body(attn_ref, y_ref, x_ref, tg_ref, b_ref, ga_ref, gs_ref, g2_ref, wg_s, wo_s, wu_hbm, wd_hbm,
             dx2_ref, hdn_ref, dup_ref, h_ref, dyb_ref, mix_ref, z_ref, dattn_ref, dy_ref, dx2b_ref, dgp_ref,
             dg2_ref, loss_ref, dga_ref, dgs_ref, db_ref, wu_s, wd_s, relu_s, sem):
        @pl.when(pl.program_id(0) == 0)
        def _():
            copies = [pltpu.make_async_copy(src, dst, sem.at[k]) for k, (src, dst) in enumerate(((wu_hbm, wu_s), (wd_hbm, wd_s)))]
            for cp in copies:
                cp.start()
            for cp in copies:
                cp.wait()
            for acc in (dg2_ref, loss_ref, dga_ref, dgs_ref, db_ref):
                acc[...] = jnp.zeros_like(acc)

        av = attn_ref[...]
        z, _, sg = _glu(y_ref[...], wg_s[...], b_ref[...])
        z_ref[...] = z.astype(BF16)
        s = z * sg
        an = (av * _rms(av) * ga_ref[...]).astype(BF16)
        sn = (s * _rms(s) * gs_ref[...]).astype(BF16)
        mix_ref[:, 0:ATTN_W] = an
        mix_ref[:, ATTN_W:] = sn
        dx2_ref[...] = x_ref[...] + _mm(an, wo_s[0:ATTN_W, :]) + _mm(sn, wo_s[ATTN_W:, :])
        r = _rms(dx2_ref[...])
        g2v = g2_ref[...]
        h = (dx2_ref[...] * r * g2v).astype(BF16)
        h_ref[...] = h
        yout = dx2_ref[...]
        for c in range(n_fc):
            cols = slice(c * fc, (c + 1) * fc)
            ru = jnp.maximum(_mm(h, wu_s[:, cols]), 0.0)
            relu_s[:, cols] = ru
            hd = (ru * ru).astype(BF16)
            hdn_ref[:, cols] = hd
            yout = yout + _mm(hd, wd_s[cols, :])
        err = yout - tg_ref[...]
        loss_ref[...] = loss_ref[...] + 0.5 * jnp.sum(err * err) * (1.0 / D_MODEL)
        dy = err * (1.0 / D_MODEL)
        dyb = dy.astype(BF16)
        dyb_ref[...] = dyb
        dh = jnp.zeros((tm, D_MODEL), F32)
        for c in range(n_fc):
            cols = slice(c * fc, (c + 1) * fc)
            dup = (_nt(dyb, wd_s[cols, :]) * (2.0 * relu_s[:, cols])).astype(BF16)
            dup_ref[:, cols] = dup
            dh = dh + _nt(dup, wu_s[:, cols])
        dxn, g2_term = _rms_bwd(dh, dx2_ref[...], r, g2v)
        dx2_ref[...] = dy + dxn
        dg2_ref[...] = dg2_ref[...] + _colsum(g2_term)
        dx2b = dx2_ref[...].astype(BF16)
        dx2b_ref[...] = dx2b
        yv = y_ref[...]
        av = attn_ref[...]
        z, t, sg = _glu(yv, wg_s[...], b_ref[...])
        s = z * sg
        d_attn, ga_term = _rms_bwd(_nt(dx2b, wo_s[0:ATTN_W, :]), av, _rms(av), ga_ref[...])
        d_s, gs_term = _rms_bwd(_nt(dx2b, wo_s[ATTN_W:, :]), s, _rms(s), gs_ref[...])
        dattn_ref[...] = d_attn
        dgp = d_s * z * sg * (1.0 - sg)
        dgpb = dgp.astype(BF16)
        dgp_ref[...] = dgpb
        dy_ref[...] = (d_s * sg + _nt(dgpb, wg_s[...])) * _gelu_grad(yv, t)
        dga_ref[...] = dga_ref[...] + _colsum(ga_term)
        dgs_ref[...] = dgs_ref[...] + _colsum(gs_term)
        db_ref[...] = db_ref[...] + _colsum(dgp)

    row = lambda i: (i, 0)
    const = lambda i: (0, 0)
    wide = lambda n: pl.BlockSpec((tm, n), row)
    vec = lambda n: pl.BlockSpec((1, n), const)
    any_spec = pl.BlockSpec(memory_space=pl.ANY)
    f32 = lambda n: jax.ShapeDtypeStruct((t_len, n), F32)
    b16 = lambda n: jax.ShapeDtypeStruct((t_len, n), BF16)
    acc = lambda n: jax.ShapeDtypeStruct((1, n), F32)
    return pl.pallas_call(
        body, name="mixer_mlp", grid=(t_len // tm,),
        in_specs=[wide(ATTN_W), wide(SSM_W), wide(D_MODEL), wide(D_MODEL), vec(SSM_W), vec(ATTN_W), vec(SSM_W), vec(D_MODEL),
                  pl.BlockSpec((SSM_W, SSM_W), const), pl.BlockSpec((D_MODEL, D_MODEL), const), any_spec, any_spec],
        out_specs=[wide(D_MODEL), wide(D_FF), wide(D_FF), wide(D_MODEL), wide(D_MODEL), wide(D_MODEL), wide(SSM_W),
                   wide(ATTN_W), wide(SSM_W), wide(D_MODEL), wide(SSM_W),
                   vec(D_MODEL), vec(LANES), vec(ATTN_W), vec(SSM_W), vec(SSM_W)],
        out_shape=[f32(D_MODEL), b16(D_FF), b16(D_FF), b16(D_MODEL), b16(D_MODEL), b16(D_MODEL), b16(SSM_W),
                   f32(ATTN_W), f32(SSM_W), b16(D_MODEL), b16(SSM_W),
                   acc(D_MODEL), acc(LANES), acc(ATTN_W), acc(SSM_W), acc(SSM_W)],
        scratch_shapes=[pltpu.VMEM((D_MODEL, D_FF), BF16), pltpu.VMEM((D_FF, D_MODEL), BF16),
                        pltpu.VMEM((tm, D_FF), F32), pltpu.SemaphoreType.DMA((2,))],
        compiler_params=_params("arbitrary"),
    )(attn, y, x, target, glu_b, ga, gs, g2, wg, wo, wu, wd)


def _inproj_bwd(dqs, dkn, dv, du, q_raw, k_raw, x, dx2, wi, g1, gq, gk, ones64):
    t_len = x.shape[0]
    tm = 512
    n_heads = ATTN_W // HEAD

    def body(dqs_ref, dkn_ref, dv_ref, du_ref, q_ref, k_ref, x_ref, dx2_ref, wi_ref, g1_ref, gq_ref, gk_ref, bd_ref,
             gx_ref, dproj_ref, dg1_ref, dgq_ref, dgk_ref, accq, acck):
        i = pl.program_id(0)

        @pl.when(i == 0)
        def _():
            dg1_ref[...] = jnp.zeros_like(dg1_ref)
            accq[...] = jnp.zeros_like(accq)
            acck[...] = jnp.zeros_like(acck)

        bd = bd_ref[...]

        def head_norm_bwd(dy, raw, gain, acc):
            r = lax.rsqrt(_group_mean(raw * raw, bd, HEAD) + EPS)
            xh = raw * r
            dxh = dy * gain
            acc[...] = acc[...] + _colsum(dy * xh)
            return r * (dxh - xh * _group_mean(dxh * xh, bd, HEAD))

        dq = head_norm_bwd(dqs_ref[...] * (HEAD ** -0.5), q_ref[...], gq_ref[...], accq)
        dk = head_norm_bwd(dkn_ref[...], k_ref[...], gk_ref[...], acck)
        dproj_ref[:, 0:ATTN_W] = dq.astype(BF16)
        dproj_ref[:, ATTN_W:2 * ATTN_W] = dk.astype(BF16)
        dproj_ref[:, 2 * ATTN_W:3 * ATTN_W] = dv_ref[...].astype(BF16)
        dproj_ref[:, 3 * ATTN_W:] = du_ref[...].astype(BF16)
        dxn = _nt(dproj_ref[...], wi_ref[...])
        xv = x_ref[...]
        g1v = g1_ref[...]
        dx, g1_term = _rms_bwd(dxn, xv, _rms(xv), g1v)
        gx_ref[...] = dx2_ref[...] + dx
        dg1_ref[...] = dg1_ref[...] + _colsum(g1_term)

        @pl.when(i == pl.num_programs(0) - 1)
        def _():
            for acc, out in ((accq, dgq_ref), (acck, dgk_ref)):
                tot = acc[:, 0:HEAD]
                for h in range(1, n_heads):
                    tot = tot + acc[:, h * HEAD:(h + 1) * HEAD]
                out[...] = tot

    row = lambda i: (i, 0)
    const = lambda i: (0, 0)
    aw = pl.BlockSpec((tm, ATTN_W), row)
    dm = pl.BlockSpec((tm, D_MODEL), row)
    return pl.pallas_call(
        body, name="inproj_bwd", grid=(t_len // tm,),
        in_specs=[aw, aw, aw, aw, aw, aw, dm, dm, pl.BlockSpec((D_MODEL, PROJ_W), const), pl.BlockSpec((1, D_MODEL), const),
                  pl.BlockSpec((1, ATTN_W), const), pl.BlockSpec((1, ATTN_W), const), pl.BlockSpec(ones64.shape, const)],
        out_specs=[dm, pl.BlockSpec((tm, PROJ_W), row), pl.BlockSpec((1, D_MODEL), const),
                   pl.BlockSpec((1, HEAD), const), pl.BlockSpec((1, HEAD), const)],
        out_shape=[jax.ShapeDtypeStruct((t_len, D_MODEL), F32), jax.ShapeDtypeStruct((t_len, PROJ_W), BF16),
                   jax.ShapeDtypeStruct((1, D_MODEL), F32), jax.ShapeDtypeStruct((1, HEAD), F32),
                   jax.ShapeDtypeStruct((1, HEAD), F32)],
        scratch_shapes=[pltpu.VMEM((1, ATTN_W), F32), pltpu.VMEM((1, ATTN_W), F32)],
        compiler_params=_params("arbitrary"),
    )(dqs, dkn, dv, du, q_raw, k_raw, x, dx2, wi, g1, gq, gk, ones64)


def _grad_matmul(a, b, name):
    t_len, m = a.shape
    n = b.shape[1]
    bm, bn, bt = min(m, 1024), min(n, 1024), min(t_len, 4096)

    def body(a_ref, b_ref, o_ref):
        @pl.when(pl.program_id(2) == 0)
        def _():
            o_ref[...] = jnp.zeros_like(o_ref)

        o_ref[...] = o_ref[...] + _tn(a_ref[...], b_ref[...])

    return pl.pallas_call(
        body, name=name, grid=(m // bm, n // bn, t_len // bt),
        in_specs=[pl.BlockSpec((bt, bm), lambda i, j, k: (k, i)), pl.BlockSpec((bt, bn), lambda i, j, k: (k, j))],
        out_specs=pl.BlockSpec((bm, bn), lambda i, j, k: (i, j)),
        out_shape=jax.ShapeDtypeStruct((m, n), F32),
        compiler_params=_params("arbitrary", "arbitrary", "arbitrary"),
    )(a, b)


def _adamw_update(w_ref, g_ref, m_ref, v_ref, d_o, m_o, v_o):
    gv = g_ref[...]
    mn = ADAM_B1 * m_ref[...] + (1.0 - ADAM_B1) * gv
    vn = ADAM_B2 * v_ref[...] + (1.0 - ADAM_B2) * jnp.square(gv)
    m_hat = mn / (1.0 - ADAM_B1 ** ADAM_STEP)
    v_hat = vn / (1.0 - ADAM_B2 ** ADAM_STEP)
    d_o[...] = -ADAM_LR * (m_hat / (jnp.sqrt(v_hat) + ADAM_EPS) + ADAM_WD * w_ref[...])
    m_o[...] = mn
    v_o[...] = vn


def _adamw_many(ws, gs, ms, vs):
    n = len(ws)

    def body(*refs):
        for i in range(n):
            _adamw_update(*[refs[k * n + i] for k in range(7)])

    shapes = [jax.ShapeDtypeStruct(w.shape, F32) for w in ws]
    outs = pl.pallas_call(body, name="adamw_small", out_shape=shapes * 3,
                          compiler_params=pltpu.CompilerParams(vmem_limit_bytes=VMEM_LIMIT))(*ws, *gs, *ms, *vs)
    return outs[0:n], outs[n:2 * n], outs[2 * n:]


def _adamw(w, g, m, v, name):
    rows, cols = w.shape
    br = _row_block(rows, 256)
    body = functools.partial(_adamw_update)

    spec = pl.BlockSpec((br, cols), lambda i: (i, 0))
    shape = jax.ShapeDtypeStruct((rows, cols), F32)
    return pl.pallas_call(
        body, name=name, grid=(rows // br,), in_specs=[spec] * 4, out_specs=[spec] * 3, out_shape=[shape] * 3,
        compiler_params=_params("arbitrary"),
    )(w, g, m, v)


def _sum_arrays(arrs, name, out_dtype=F32):
    rows, cols = arrs[0].shape
    br = _row_block(rows, 512)
    n = len(arrs)

    def body(*refs):
        tot = refs[0][...]
        for r in refs[1:n]:
            tot = tot + r[...]
        refs[n][...] = tot.astype(out_dtype)

    spec = pl.BlockSpec((br, cols), lambda i: (i, 0))
    return pl.pallas_call(
        body, name=name, grid=(rows // br,), in_specs=[spec] * n, out_specs=spec,
        out_shape=jax.ShapeDtypeStruct((rows, cols), out_dtype), compiler_params=_params("arbitrary"),
    )(*arrs)


GPL = N_GROUPS // N_LB
SW = GPL * N_STATE


def _eye_groups():
    return jnp.eye(GPL, dtype=F32)


def _s5_matrices(ab_r, ab_i, bb_r, bb_i, c_re, c_im, d_skip):
    eye = _eye_groups()
    a_cat = jnp.concatenate([ab_r.reshape(N_LB, HALF, LANES), ab_i.reshape(N_LB, HALF, LANES)], axis=1)

    def b_part(bb):
        b4 = jnp.transpose(bb.reshape(N_LB, GPL, N_STATE, GROUP_W), (0, 1, 3, 2))
        return (b4[:, :, :, None, :] * eye[None, :, None, :, None]).reshape(N_LB, LANES, SW)

    def c_part(cc):
        c4 = jnp.transpose(cc.reshape(N_LB, GPL, GROUP_W, N_STATE), (0, 1, 3, 2))
        return (c4[:, :, :, None, :] * eye[None, :, None, :, None]).reshape(N_LB, SW, LANES)

    b_mat = jnp.concatenate([b_part(bb_r), b_part(bb_i)], axis=2).astype(BF16)
    c_mat = jnp.concatenate([c_part(c_re), -c_part(c_im)], axis=1).astype(BF16)
    return a_cat, b_mat, c_mat, d_skip.reshape(N_LB, 1, LANES)


def _s5_unpack_grads(db, dc, da, dd):
    eye = _eye_groups()
    mask = eye[None, :, None, None, :, None]
    d6 = jnp.sum(db.reshape(N_LB, GPL, GROUP_W, 2, GPL, N_STATE) * mask, axis=4)
    dbb = jnp.transpose(d6, (3, 0, 1, 4, 2)).reshape(2, N_GROUPS * N_STATE, GROUP_W)
    c6 = jnp.sum(dc.reshape(N_LB, GPL, GROUP_W, 2, GPL, N_STATE) * mask, axis=4)
    dcc = jnp.transpose(c6, (3, 0, 1, 2, 4)).reshape(2, N_GROUPS, GROUP_W, N_STATE)
    dab_r = da[:, :HALF].reshape(N_GROUPS * N_STATE, 1)
    dab_i = da[:, HALF:].reshape(N_GROUPS * N_STATE, 1)
    return dab_r, dab_i, dbb[0], dbb[1], dcc[0], -dcc[1], dd.reshape(N_GROUPS, GROUP_W)


def _block_ones(n, width):
    i = lax.broadcasted_iota(jnp.int32, (n, n), 0) // width
    j = lax.broadcasted_iota(jnp.int32, (n, n), 1) // width
    return (i == j).astype(BF16)


def _tile_heads(g):
    return jnp.tile(g.reshape(1, HEAD), (1, ATTN_W // HEAD))


def _local_step(x, target, wi, rest, p, fwd_side=None, bwd_side=None):
    ones64 = _block_ones(MXU_WIDTH, HEAD)
    ones_hp = _block_ones(LANES, HEAD)
    g1 = p["norm1_g"].reshape(1, D_MODEL)
    g2 = p["norm2_g"].reshape(1, D_MODEL)
    gq = _tile_heads(p["q_norm_g"])
    gk = _tile_heads(p["k_norm_g"])
    ga = p["attn_out_norm_g"].reshape(1, ATTN_W)
    gs = p["ssm_out_norm_g"].reshape(1, SSM_W)
    glu_b = p["glu_b"].reshape(1, SSM_W)
    n_gp = N_GROUPS * N_STATE
    lr = p["ssm_a_re"].reshape(n_gp, 1)
    li = p["ssm_a_im"].reshape(n_gp, 1)
    ldt = jnp.repeat(p["ssm_log_dt"].reshape(N_GROUPS), N_STATE).reshape(n_gp, 1)
    br = p["ssm_b_re"].reshape(n_gp, GROUP_W)
    bi = p["ssm_b_im"].reshape(n_gp, GROUP_W)
    ab_r, ab_i, bb_r, bb_i = _disc_fwd(lr, li, ldt, br, bi)
    a_cat, b_mat, c_mat, d_mat = _s5_matrices(
        ab_r, ab_i, bb_r, bb_i, p["ssm_c_re"].reshape(N_GROUPS, GROUP_W, N_STATE),
        p["ssm_c_im"].reshape(N_GROUPS, GROUP_W, N_STATE), p["ssm_d"])

    xn, qn, kn, vv, u, q_raw, k_raw = _inproj_fwd(x, g1, wi, gq, gk, ones64)
    if fwd_side is None:
        attn, lse = _attn_fwd(qn, kn, vv)
    else:
        attn, lse, *rest = _attn_fwd(qn, kn, vv, *fwd_side)
    wg, wo, wu, wd = rest
    y, states = _s5_fwd(u, a_cat, b_mat, c_mat, d_mat)
    (dx2, hdn, dup, h, dyb, mix, z, d_attn, dy_ssm, dx2b, dgp, dg2, loss, dga, dgs, dglu_b) = _mixer_mlp(
        attn, y, x, target, wg, glu_b, ga, gs, wo, g2, wu, wd)
    big = {"w_mlp_up": _grad_matmul(h, dup, "grad_w_mlp_up"), "w_mlp_down": _grad_matmul(hdn, dyb, "grad_w_mlp_down"),
           "w_out": _grad_matmul(mix, dx2b, "grad_w_out"), "glu_w": _grad_matmul(z, dgp, "grad_glu_w")}
    rode = []
    if bwd_side is None:
        dqs, dkn, dvv = _attn_bwd(qn, kn, vv, attn, d_attn, lse, ones_hp)
    else:
        dqs, dkn, dvv, *rode = _attn_bwd(qn, kn, vv, attn, d_attn, lse, ones_hp, *bwd_side(big))
    du, db, dc, da, dd = _s5_bwd(u, dy_ssm, states, a_cat, b_mat, c_mat, d_mat)
    grad_x, dproj, dg1, dgq, dgk = _inproj_bwd(dqs, dkn, dvv, du, q_raw, k_raw, x, dx2, wi, g1, gq, gk, ones64)

    big["w_in"] = _grad_matmul(xn, dproj, "grad_w_in")
    dab_r, dab_i, dbb_r, dbb_i, dc_re, dc_im, dd_g = _s5_unpack_grads(db, dc, da, dd)
    cot = {"norm1_g": dg1, "q_norm_g": dgq, "k_norm_g": dgk, "ab_r": dab_r, "ab_i": dab_i, "bb_r": dbb_r, "bb_i": dbb_i,
           "ssm_c_re": dc_re, "ssm_c_im": dc_im, "ssm_d": dd_g, "glu_b": dglu_b, "attn_out_norm_g": dga,
           "ssm_out_norm_g": dgs, "norm2_g": dg2}
    return loss[0, 0], grad_x, big, cot, (lr, li, ldt, br, bi), rode


COT_NAMES = ("norm1_g", "q_norm_g", "k_norm_g", "ab_r", "ab_i", "bb_r", "bb_i", "ssm_c_re", "ssm_c_im", "ssm_d",
             "glu_b", "attn_out_norm_g", "ssm_out_norm_g", "norm2_g")
SMALL_NAMES = ("norm1_g", "q_norm_g", "k_norm_g", "ssm_a_re", "ssm_a_im", "ssm_log_dt", "ssm_b_re", "ssm_b_im",
               "ssm_c_re", "ssm_c_im", "ssm_d", "glu_b", "attn_out_norm_g", "ssm_out_norm_g", "norm2_g")
BIG_NAMES = ("w_in", "glu_w", "w_out", "w_mlp_up", "w_mlp_down")
PACK_ROWS = 1152


def _pack(arrs):
    flat = jnp.concatenate([a.reshape(-1) for a in arrs])
    return jnp.pad(flat, (0, PACK_ROWS * LANES - flat.shape[0])).reshape(PACK_ROWS, LANES)


def _unpack(packed, like):
    flat = packed.reshape(-1)
    out, pos = [], 0
    for a in like:
        out.append(flat[pos:pos + a.size].reshape(a.shape))
        pos += a.size
    return out


def _small_grads(cot, disc_in, p):
    lr, li, ldt, br, bi = disc_in
    group_sum = (lax.broadcasted_iota(jnp.int32, (N_GROUPS, N_GROUPS * N_STATE), 1) // N_STATE
                 == lax.broadcasted_iota(jnp.int32, (N_GROUPS, N_GROUPS * N_STATE), 0)).astype(F32)
    dlr, dli, dldt, dbr, dbi = _disc_bwd(lr, li, ldt, br, bi, cot["ab_r"], cot["ab_i"], cot["bb_r"], cot["bb_i"], group_sum)
    g = dict(cot)
    g.update(ssm_a_re=dlr, ssm_a_im=dli, ssm_log_dt=dldt[:, 0], ssm_b_re=dbr, ssm_b_im=dbi)
    return {n: g[n].reshape(p[n].shape) for n in SMALL_NAMES}


BIG = {
    "w_in": ((D_MODEL, PROJ_W), 1, PROJ_W // 4, 0, D_MODEL // 2),
    "glu_w": ((SSM_W, SSM_W), 0, SSM_W // 4, 1, SSM_W // 2),
    "w_out": ((D_MODEL, D_MODEL), 0, D_MODEL // 4, 1, D_MODEL // 2),
    "w_mlp_up": ((D_MODEL, D_FF), 1, D_FF // 4, 0, D_MODEL // 2),
    "w_mlp_down": ((D_FF, D_MODEL), 0, D_FF // 4, 1, D_MODEL // 2),
}
N_BIG = len(BIG_NAMES)
N_CHIPS = 4
ANY = pl.BlockSpec(memory_space=pl.ANY)


def _cut(name, shard=False, half=False):
    shape, s_ax, s_sz, h_ax, h_sz = BIG[name]
    shape = list(shape)
    if shard:
        shape[s_ax] = s_sz
    if half:
        shape[h_ax] = h_sz
    return tuple(shape)


def _window(name, base, shard=None, half=None):
    _, s_ax, s_sz, h_ax, h_sz = BIG[name]
    idx = [pl.ds(0, base[0]), pl.ds(0, base[1])]
    if shard is not None:
        idx[s_ax] = pl.ds(pl.multiple_of(shard * s_sz, s_sz), s_sz)
    if half is not None:
        idx[h_ax] = pl.ds(pl.multiple_of(half * h_sz, h_sz), h_sz)
    return tuple(idx)


def _mesh_pos():
    return lax.axis_index("x"), lax.axis_index("y"), lax.axis_index("c")


def _other_chips(x, y):
    return [(1 - x, y, 2 * (1 - x) + y), (x, 1 - y, 2 * x + 1 - y), (1 - x, 1 - y, 2 * (1 - x) + 1 - y)]


def _remote(src, dst, send_sem, recv_sem, dev):
    return pltpu.make_async_remote_copy(src_ref=src, dst_ref=dst, send_sem=send_sem, recv_sem=recv_sem,
                                        device_id=dev, device_id_type=MESH)


def _start_remote(src, dst, send_sem, recv_sem, dev):
    cp = _remote(src, dst, send_sem, recv_sem, dev)
    cp.start()
    return cp


class _Gather:
    def __init__(self, names):
        self.names = tuple(names)
        self.n = len(self.names)

    def in_specs(self):
        return [pl.BlockSpec(memory_space=pltpu.VMEM)] * self.n

    def out_specs(self):
        return [ANY] * self.n

    def out_shapes(self):
        return [jax.ShapeDtypeStruct(BIG[w][0], BF16) for w in self.names]

    def scratch_shapes(self):
        n_sem = (N_CHIPS - 1) * self.n
        return ([pltpu.VMEM(_cut(w, shard=True), BF16) for w in self.names]
                + [pltpu.SemaphoreType.DMA((n_sem,))] * 4 + [pltpu.SemaphoreType.DMA((self.n,))])

    def bind(self, ins, outs, scratch):
        self.ins, self.outs = ins, outs
        self.stage = scratch[:self.n]
        self.send, self.recv, self.fsend, self.frecv, self.lsem = scratch[self.n:]

    def _copies(self):
        x, y, c = _mesh_pos()
        me = 2 * x + y
        sib = (x, y, 1 - c)
        local, sends, lands, fwds, flands = [], [], [], [], []
        for w, n in enumerate(self.names):
            local.append(pltpu.make_async_copy(self.stage[w], self.outs[w].at[_window(n, BIG[n][0], shard=me)], self.lsem.at[w]))
        for k, (px, py, pj) in enumerate(_other_chips(x, y)):
            for w, n in enumerate(self.names):
                s = k * self.n + w
                sends.append(_remote(self.stage[w].at[_window(n, _cut(n, shard=True), half=c)],
                                     self.outs[w].at[_window(n, BIG[n][0], shard=me, half=c)],
                                     self.send.at[s], self.recv.at[s], (px, py, c)))
                got = self.outs[w].at[_window(n, BIG[n][0], shard=pj, half=c)]
                lands.append(_remote(got, got, self.send.at[s], self.recv.at[s], (px, py, c)))
                fwds.append(_remote(got, got, self.fsend.at[s], self.frecv.at[s], sib))
                theirs = self.outs[w].at[_window(n, BIG[n][0], shard=pj, half=1 - c)]
                flands.append(_remote(theirs, theirs, self.fsend.at[s], self.frecv.at[s], sib))
        return local, sends, lands, fwds, flands

    def start(self):
        for w in range(self.n):
            self.stage[w][...] = self.ins[w][...].astype(BF16)
        local, sends, _, _, _ = self._copies()
        for cp in local + sends:
            cp.start()

    def forward(self):
        _, _, lands, fwds, _ = self._copies()
        for land, fwd in zip(lands, fwds):
            land.wait_recv()
            fwd.start()

    def finish(self):
        local, sends, _, fwds, flands = self._copies()
        for cp in flands:
            cp.wait_recv()
        for cp in sends + fwds:
            cp.wait_send()
        for cp in local:
            cp.wait()


def _gather_weights(shards, names):
    g = _Gather(names)

    def body(*refs):
        g.bind(refs[0:g.n], refs[g.n:2 * g.n], refs[2 * g.n:])
        g.start()
        g.forward()
        g.finish()

    return pl.pallas_call(
        body, name="gather_" + "_".join(names), in_specs=g.in_specs(), out_specs=g.out_specs(), out_shape=g.out_shapes(),
        scratch_shapes=g.scratch_shapes(), compiler_params=pltpu.CompilerParams(vmem_limit_bytes=VMEM_LIMIT),
    )(*[shards[n] for n in names])


def _pair_exchange(grads, names, packed=None):
    n_big = len(names)
    n_all = n_big + (packed is not None)

    def body(*refs):
        ins, got = refs[0:n_all], refs[n_all:2 * n_all]
        send, recv = refs[2 * n_all:]
        x, y, c = _mesh_pos()
        sib = (x, y, 1 - c)
        copies = []
        for w, n in enumerate(names):
            copies.append(_start_remote(ins[w].at[_window(n, BIG[n][0], half=1 - c)], got[w], send.at[w], recv.at[w], sib))
        if packed is not None:
            copies.append(_start_remote(ins[n_big], got[n_big], send.at[n_big], recv.at[n_big], sib))
        for cp in copies:
            cp.wait()

    shapes = [jax.ShapeDtypeStruct(_cut(n, half=True), F32) for n in names]
    args = [grads[n] for n in names]
    if packed is not None:
        shapes.append(jax.ShapeDtypeStruct(packed.shape, F32))
        args.append(packed)
    return pl.pallas_call(
        body, name="grad_pair_exchange_" + "_".join(names), in_specs=[ANY] * n_all, out_specs=[ANY] * n_all, out_shape=shapes,
        scratch_shapes=[pltpu.SemaphoreType.DMA((n_all,)), pltpu.SemaphoreType.DMA((n_all,))],
    )(*args)


def _pair_sum(name, full, got, core):
    _, _, _, h_ax, _ = BIG[name]
    rows, cols = _cut(name, half=True)
    br = _row_block(rows, 512)
    nb = rows // br
    own_map = (lambda i, c: (i + c[0] * nb, 0)) if h_ax == 0 else (lambda i, c: (i, c[0]))

    def body(c_ref, own_ref, got_ref, o_ref):
        o_ref[...] = (own_ref[...] + got_ref[...]).astype(BF16)

    plain = pl.BlockSpec((br, cols), lambda i, c: (i, 0))
    return pl.pallas_call(
        body, name="pair_sum_" + name,
        grid_spec=pltpu.PrefetchScalarGridSpec(num_scalar_prefetch=1, grid=(nb,),
                                               in_specs=[pl.BlockSpec((br, cols), own_map), plain], out_specs=plain),
        out_shape=jax.ShapeDtypeStruct((rows, cols), BF16), compiler_params=_params("arbitrary"),
    )(core, full, got)


class _ChipExchange:
    def __init__(self, names, packed_shape=None):
        self.names = tuple(names)
        self.packed_shape = packed_shape
        self.n = len(self.names) + (packed_shape is not None)

    def in_specs(self):
        return [ANY] * self.n

    def out_specs(self):
        return [ANY] * self.n

    def out_shapes(self):
        shapes = [jax.ShapeDtypeStruct((N_CHIPS,) + _cut(w, shard=True, half=True), BF16) for w in self.names]
        if self.packed_shape is not None:
            shapes.append(jax.ShapeDtypeStruct((N_CHIPS,) + tuple(self.packed_shape), F32))
        return shapes

    def scratch_shapes(self):
        n_sem = (N_CHIPS - 1) * self.n
        return [pltpu.SemaphoreType.DMA((n_sem,)), pltpu.SemaphoreType.DMA((n_sem,))]

    def bind(self, ins, outs, scratch):
        self.ins, self.outs = ins, outs
        self.send, self.recv = scratch

    def _piece(self, w, shard):
        if w >= len(self.names):
            return self.ins[w]
        n = self.names[w]
        return self.ins[w].at[_window(n, _cut(n, half=True), shard=shard)]

    def _copies(self):
        x, y, c = _mesh_pos()
        me = 2 * x + y
        sends, lands = [], []
        for k, (px, py, pj) in enumerate(_other_chips(x, y)):
            for w in range(self.n):
                s = k * self.n + w
                sends.append(_remote(self._piece(w, pj), self.outs[w].at[me], self.send.at[s], self.recv.at[s], (px, py, c)))
                lands.append(_remote(self._piece(w, me), self.outs[w].at[pj], self.send.at[s], self.recv.at[s], (px, py, c)))
        return sends, lands

    def start(self):
        for cp in self._copies()[0]:
            cp.start()

    def finish(self):
        sends, lands = self._copies()
        for cp in lands:
            cp.wait_recv()
        for cp in sends:
            cp.wait_send()


def _chip_exchange(halves, packed, names):
    ex = _ChipExchange(names, packed.shape)

    def body(*refs):
        ex.bind(refs[0:ex.n], refs[ex.n:2 * ex.n], refs[2 * ex.n:])
        ex.start()
        ex.finish()

    return pl.pallas_call(
        body, name="grad_chip_exchange", in_specs=ex.in_specs(), out_specs=ex.out_specs(), out_shape=ex.out_shapes(),
        scratch_shapes=ex.scratch_shapes(),
    )(*halves, packed)


def _chip_sum(name, own, slots, chip):
    n_slot, rows, cols = slots.shape
    br = _row_block(rows, 512)
    nb = rows // br
    if name in BIG and BIG[name][1] == 1:
        own_map = lambda i, m: (i, m[0])
    elif name in BIG:
        own_map = lambda i, m: (i + m[0] * nb, 0)
    else:
        own_map = lambda i, m: (i, 0)

    def slot_map(j):
        return lambda i, m: (jnp.where(m[0] == j, (j + 1) % n_slot, j), i, 0)

    def body(m_ref, own_ref, *refs):
        own_blk = own_ref[...].astype(F32)
        tot = None
        for j in range(n_slot):
            term = jnp.where(m_ref[0] == j, own_blk, refs[j][...].astype(F32))
            tot = term if tot is None else tot + term
        refs[n_slot][...] = tot

    in_specs = [pl.BlockSpec((br, cols), own_map)] + [pl.BlockSpec((None, br, cols), slot_map(j)) for j in range(n_slot)]
    return pl.pallas_call(
        body, name="chip_sum_" + name,
        grid_spec=pltpu.PrefetchScalarGridSpec(num_scalar_prefetch=1, grid=(nb,), in_specs=in_specs,
                                               out_specs=pl.BlockSpec((br, cols), lambda i, m: (i, 0))),
        out_shape=jax.ShapeDtypeStruct((rows, cols), F32), compiler_params=_params("arbitrary"),
    )(chip, own, *([slots] * n_slot))


def _half_exchange(pieces):
    def body(*refs):
        ins, outs = refs[0:N_BIG], refs[N_BIG:2 * N_BIG]
        send, recv = refs[2 * N_BIG:]
        x, y, c = _mesh_pos()
        sib = (x, y, 1 - c)
        copies = []
        for w, n in enumerate(BIG_NAMES):
            copies.append(_start_remote(ins[w], outs[w], send.at[w], recv.at[w], sib))
        for cp in copies:
            cp.wait()

    return pl.pallas_call(
        body, name="grad_half_exchange", in_specs=[ANY] * N_BIG, out_specs=[ANY] * N_BIG,
        out_shape=[jax.ShapeDtypeStruct(_cut(n, shard=True, half=True), F32) for n in BIG_NAMES],
        scratch_shapes=[pltpu.SemaphoreType.DMA((N_BIG,)), pltpu.SemaphoreType.DMA((N_BIG,))],
    )(*pieces)


WEIGHT_NAMES = ("norm1_g", "w_in", "q_norm_g", "k_norm_g", "ssm_a_re", "ssm_a_im", "ssm_log_dt", "ssm_b_re", "ssm_b_im",
                "ssm_c_re", "ssm_c_im", "ssm_d", "glu_w", "glu_b", "attn_out_norm_g", "ssm_out_norm_g", "w_out", "norm2_g",
                "w_mlp_up", "w_mlp_down")


def _train_step(a):
    x = a["x"][0]
    target = a["loss_target"][0]
    shards = {n: a[n][0] for n in BIG_NAMES}
    p = {n: a[n][0] for n in SMALL_NAMES}
    core = lax.axis_index("c").astype(jnp.int32).reshape(1)
    chip_id = (2 * lax.axis_index("x") + lax.axis_index("y")).astype(jnp.int32).reshape(1)

    later = ("glu_w", "w_out", "w_mlp_up", "w_mlp_down")
    early = ("w_mlp_up", "w_mlp_down", "w_out", "glu_w")
    late = ("w_in",)
    (wi,) = _gather_weights(shards, ("w_in",))
    chip = {}

    def bwd_side(grads):
        got = _pair_exchange(grads, early)
        for n, g in zip(early, got):
            chip[n] = _pair_sum(n, grads[n], g, core)
        return _ChipExchange(early), [chip[n] for n in early]

    loss, grad_x, big, cot, disc_in, early_slots = _local_step(
        x, target, wi, None, p, fwd_side=(_Gather(later), [shards[n] for n in later]), bwd_side=bwd_side)
    slots = dict(zip(early, early_slots))

    cot_list = [cot[n] for n in COT_NAMES]
    packed = _pack(cot_list)
    *got, got_packed = _pair_exchange(big, late, packed)
    for n, g in zip(late, got):
        chip[n] = _pair_sum(n, big[n], g, core)
    chip_packed = _sum_arrays([packed, got_packed], "pair_sum_small")
    *late_slots, small_slots = _chip_exchange([chip[n] for n in late], chip_packed, late)
    slots.update(zip(late, late_slots))
    pieces = [_chip_sum(n, chip[n], slots[n], chip_id) for n in BIG_NAMES]
    small_sum = _chip_sum("small", chip_packed, small_slots, chip_id)
    shard_grads = {}
    for n, mine, theirs in zip(BIG_NAMES, pieces, _half_exchange(pieces)):
        h_ax = BIG[n][3]
        shard_grads[n] = jnp.where(core[0] == 0, jnp.concatenate([mine, theirs], axis=h_ax),
                                   jnp.concatenate([theirs, mine], axis=h_ax))
    small_grads = _small_grads(dict(zip(COT_NAMES, _unpack(small_sum, cot_list))), disc_in, p)

    grads, delta, new_m, new_v = {}, {}, {}, {}
    for n in BIG_NAMES:
        grads[n] = shard_grads[n]
        delta[n], new_m[n], new_v[n] = _adamw(a[n][0], grads[n], a["m_" + n][0], a["v_" + n][0], "adamw_" + n)
    swapped = ("ssm_b_re", "ssm_b_im", "ssm_d")

    def flat2(n, t):
        t = jnp.swapaxes(t, -1, -2) if n in swapped else t
        return t.reshape(-1, t.shape[-1])

    def unflat(n, t2):
        shape = p[n].shape
        if n in swapped:
            return jnp.swapaxes(t2.reshape(shape[:-2] + (shape[-1], shape[-2])), -1, -2)
        return t2.reshape(shape)

    res = _adamw_many([flat2(n, p[n]) for n in SMALL_NAMES], [flat2(n, small_grads[n]) for n in SMALL_NAMES],
                      [flat2(n, a["m_" + n][0]) for n in SMALL_NAMES], [flat2(n, a["v_" + n][0]) for n in SMALL_NAMES])
    for store, outs in zip((delta, new_m, new_v), res):
        store.update((n, unflat(n, t2)) for n, t2 in zip(SMALL_NAMES, outs))
    grads.update(small_grads)

    total = lax.psum(loss, ("x", "y", "c"))
    out = [total, grad_x[None]]
    for store in (grads, delta, new_m, new_v):
        out += [store[n].reshape(a[n].shape) for n in WEIGHT_NAMES]
    return tuple(out)


def kernel(x, norm1_g, w_in, q_norm_g, k_norm_g, ssm_a_re, ssm_a_im, ssm_log_dt, ssm_b_re, ssm_b_im, ssm_c_re, ssm_c_im, ssm_d, glu_w, glu_b, attn_out_norm_g, ssm_out_norm_g, w_out, norm2_g, w_mlp_up, w_mlp_down, loss_target, m_norm1_g, m_w_in, m_q_norm_g, m_k_norm_g, m_ssm_a_re, m_ssm_a_im, m_ssm_log_dt, m_ssm_b_re, m_ssm_b_im, m_ssm_c_re, m_ssm_c_im, m_ssm_d, m_glu_w, m_glu_b, m_attn_out_norm_g, m_ssm_out_norm_g, m_w_out, m_norm2_g, m_w_mlp_up, m_w_mlp_down, v_norm1_g, v_w_in, v_q_norm_g, v_k_norm_g, v_ssm_a_re, v_ssm_a_im, v_ssm_log_dt, v_ssm_b_re, v_ssm_b_im, v_ssm_c_re, v_ssm_c_im, v_ssm_d, v_glu_w, v_glu_b, v_attn_out_norm_g, v_ssm_out_norm_g, v_w_out, v_norm2_g, v_w_mlp_up, v_w_mlp_down):
    return _train_step(dict(locals()))
```

```python
import functools
import math

import jax
import jax.numpy as jnp
from jax import lax
from jax.experimental import pallas as pl
from jax.experimental.pallas import tpu as pltpu

F32 = jnp.float32
BF16 = jnp.bfloat16
MESH = pl.DeviceIdType.MESH

D_MODEL = 1024
ATTN_W = 512
SSM_W = 512
HEAD = 64
D_FF = 4096
PROJ_W = 2048
N_GROUPS = 32
N_STATE = 64
GROUP_W = 16
EPS = 1e-6
NEG = -1e30
DILATIONS = (1, 4, 16)
BLK = 128
TILE = 2048
LANES = 128
MXU_WIDTH = 256
N_LB = SSM_W // LANES
VMEM_LIMIT = 56 * 1024 * 1024

ADAM_LR, ADAM_B1, ADAM_B2, ADAM_EPS, ADAM_WD, ADAM_STEP = 0.001, 0.9, 0.999, 1e-08, 0.01, 10


def _params(*sem):
    return pltpu.CompilerParams(dimension_semantics=sem, vmem_limit_bytes=VMEM_LIMIT)


def _nt(a, b):
    return lax.dot_general(a, b, (((1,), (1,)), ((), ())), preferred_element_type=F32)


def _tn(a, b):
    return lax.dot_general(a, b, (((0,), (0,)), ((), ())), preferred_element_type=F32)


def _mm(a, b):
    return jnp.dot(a, b, preferred_element_type=F32)


def _group_mean(t, ones_bd, width):
    span = ones_bd.shape[0]
    hi = t.astype(BF16)
    lo = (t - hi.astype(F32)).astype(BF16)
    parts = [_mm(hi[:, k:k + span], ones_bd) + _mm(lo[:, k:k + span], ones_bd) for k in range(0, t.shape[1], span)]
    return (parts[0] if len(parts) == 1 else jnp.concatenate(parts, axis=1)) * (1.0 / width)


def _rms(x):
    return lax.rsqrt(jnp.mean(x * x, axis=-1, keepdims=True) + EPS)


def _rms_bwd(dy, x, r, g):
    xh = x * r
    dxh = dy * g
    dx = r * (dxh - xh * jnp.mean(dxh * xh, axis=-1, keepdims=True))
    return dx, dy * xh


def _colsum(x):
    return jnp.sum(x, axis=0, keepdims=True)


def _row_block(rows, cap):
    for b in range(min(rows, cap) // 8 * 8, 0, -8):
        if rows % b == 0:
            return b
    raise ValueError(f"no row block for {rows} rows")


def _inproj_fwd(x, g1, wi, gq, gk, ones64):
    t_len = x.shape[0]
    tm = 512
    n_hp = ATTN_W // LANES

    def body(x_ref, g1_ref, wi_ref, gq_ref, gk_ref, bd_ref, xn_ref, q_ref, k_ref, v_ref, u_ref, qr_ref, kr_ref):
        xv = x_ref[...]
        xn = (xv * _rms(xv) * g1_ref[...]).astype(BF16)
        xn_ref[...] = xn
        proj = _mm(xn, wi_ref[...])
        q = proj[:, 0:ATTN_W]
        k = proj[:, ATTN_W:2 * ATTN_W]
        v = proj[:, 2 * ATTN_W:3 * ATTN_W]
        u_ref[...] = proj[:, 3 * ATTN_W:]
        qr_ref[...] = q
        kr_ref[...] = k
        bd = bd_ref[...]
        qn = q * lax.rsqrt(_group_mean(q * q, bd, HEAD) + EPS) * gq_ref[...] * (HEAD ** -0.5)
        kn = k * lax.rsqrt(_group_mean(k * k, bd, HEAD) + EPS) * gk_ref[...]
        for hp in range(n_hp):
            sl = slice(hp * LANES, (hp + 1) * LANES)
            q_ref[hp] = qn[:, sl]
            k_ref[hp] = kn[:, sl]
            v_ref[hp] = v[:, sl]

    row = lambda i: (i, 0)
    const = lambda i: (0, 0)
    hp_spec = pl.BlockSpec((n_hp, tm, LANES), lambda i: (0, i, 0))
    hp_shape = jax.ShapeDtypeStruct((n_hp, t_len, LANES), F32)
    return pl.pallas_call(
        body, name="inproj_fwd", grid=(t_len // tm,),
        in_specs=[pl.BlockSpec((tm, D_MODEL), row), pl.BlockSpec((1, D_MODEL), const),
                  pl.BlockSpec((D_MODEL, PROJ_W), const), pl.BlockSpec((1, ATTN_W), const),
                  pl.BlockSpec((1, ATTN_W), const), pl.BlockSpec(ones64.shape, const)],
        out_specs=[pl.BlockSpec((tm, D_MODEL), row), hp_spec, hp_spec, hp_spec,
                   pl.BlockSpec((tm, SSM_W), row), pl.BlockSpec((tm, ATTN_W), row), pl.BlockSpec((tm, ATTN_W), row)],
        out_shape=[jax.ShapeDtypeStruct((t_len, D_MODEL), BF16), hp_shape, hp_shape, hp_shape,
                   jax.ShapeDtypeStruct((t_len, SSM_W), F32), jax.ShapeDtypeStruct((t_len, ATTN_W), F32),
                   jax.ShapeDtypeStruct((t_len, ATTN_W), F32)],
        compiler_params=_params("arbitrary"),
    )(x, g1, wi, gq, gk, ones64)


def _attn_masks():
    head0 = lax.broadcasted_iota(jnp.int32, (BLK, LANES), 1) < HEAD
    row = lax.broadcasted_iota(jnp.int32, (2 * BLK, 2 * BLK), 0) & (BLK - 1)
    col = lax.broadcasted_iota(jnp.int32, (2 * BLK, 2 * BLK), 1)
    return head0, (col < BLK) & (col >= row), (col >= BLK) & (col - BLK <= row)


def _stack_heads(x, head0):
    return jnp.concatenate([jnp.where(head0, x, 0.0), jnp.where(head0, 0.0, x)], axis=0).astype(BF16)


def _unit_rows(uidx, d):
    nb = TILE // (BLK * d)
    r = lax.div(uidx, nb)
    b = lax.rem(uidx, nb)
    start = r + d * BLK * b
    if d == 1:
        start = pl.multiple_of(start, BLK)
        mk = lambda s: pl.ds(pl.multiple_of(s, BLK), BLK)
    else:
        mk = lambda s: pl.ds(s, BLK, stride=d)
    return b, mk(start), mk(TILE + start), mk(TILE + start - d * BLK)


def _attn_fwd(q, k, v, side=None, side_args=()):
    n_hp, t_len, _ = q.shape
    nt = t_len // TILE
    ns = side.n if side is not None else 0
    n_steps = n_hp * nt

    def body(*refs):
        q_ref, kp_ref, kc_ref, vp_ref, vc_ref = refs[0:5]
        o_ref, lse_ref = refs[5 + ns:7 + ns]
        kk, vv, m_s, l_s, acc_s = refs[7 + 2 * ns:12 + 2 * ns]
        t = pl.program_id(1)
        step = pl.program_id(0) * nt + t
        if side is not None:
            side.bind(refs[5:5 + ns], refs[7 + ns:7 + 2 * ns], refs[12 + 2 * ns:])
            pl.when(step == 0)(side.start)
            pl.when(step == n_steps // 2)(side.forward)
        kk[0:TILE] = kp_ref[0]
        kk[TILE:] = kc_ref[0]
        vv[0:TILE] = vp_ref[0]
        vv[TILE:] = vc_ref[0]
        head0, band_prev, band_cur = _attn_masks()

        for pi, d in enumerate(DILATIONS):
            def unit(uidx, carry, d=d, pi=pi):
                b, rows_q, rows_c, rows_p = _unit_rows(uidx, d)
                mask = band_cur | (band_prev & ((t > 0) | (b > 0)))
                q2 = _stack_heads(q_ref.at[0][rows_q, :], head0)
                kcat = jnp.concatenate([kk[rows_p, :], kk[rows_c, :]], axis=0).astype(BF16)
                vcat = jnp.concatenate([vv[rows_p, :], vv[rows_c, :]], axis=0).astype(BF16)
                s = jnp.where(mask, _nt(q2, kcat), NEG)
                m = jnp.max(s, axis=1, keepdims=True)
                p = jnp.exp(s - m)
                ls = jnp.sum(p, axis=1, keepdims=True)
                pv = _mm(p.astype(BF16), vcat)
                m_s.at[pi][rows_q, :] = jnp.where(head0, m[0:BLK], m[BLK:])
                l_s.at[pi][rows_q, :] = jnp.where(head0, ls[0:BLK], ls[BLK:])
                acc_s.at[pi][rows_q, :] = jnp.where(head0, pv[0:BLK], pv[BLK:])
                return carry

            lax.fori_loop(0, TILE // BLK, unit, 0, unroll=16)

        m_all = jnp.maximum(jnp.maximum(m_s[0], m_s[1]), m_s[2])
        num = jnp.zeros((TILE, LANES), F32)
        den = jnp.zeros((TILE, LANES), F32)
        for pi in range(len(DILATIONS)):
            wgt = jnp.exp(m_s[pi] - m_all)
            num = num + acc_s[pi] * wgt
            den = den + l_s[pi] * wgt
        o_ref[...] = num / den
        lse_ref[0] = m_all + jnp.log(den)
        if side is not None:
            pl.when(step == n_steps - 1)(side.finish)

    cur = lambda hp, t: (hp, t, 0)
    prev = lambda hp, t: (hp, jnp.maximum(t - 1, 0), 0)
    blk = (1, TILE, LANES)
    per_pattern = pltpu.VMEM((len(DILATIONS), TILE, LANES), F32)
    extra = (side.in_specs(), side.out_specs(), side.out_shapes(), side.scratch_shapes()) if side is not None else ([], [], [], [])
    return pl.pallas_call(
        body, name="attn_fwd", grid=(n_hp, nt),
        in_specs=[pl.BlockSpec(blk, cur), pl.BlockSpec(blk, prev), pl.BlockSpec(blk, cur),
                  pl.BlockSpec(blk, prev), pl.BlockSpec(blk, cur)] + extra[0],
        out_specs=[pl.BlockSpec((TILE, LANES), lambda hp, t: (t, hp)), pl.BlockSpec(blk, cur)] + extra[1],
        out_shape=[jax.ShapeDtypeStruct((t_len, ATTN_W), F32), jax.ShapeDtypeStruct((n_hp, t_len, LANES), F32)] + extra[2],
        scratch_shapes=[pltpu.VMEM((2 * TILE, LANES), F32), pltpu.VMEM((2 * TILE, LANES), F32),
                        per_pattern, per_pattern, per_pattern] + extra[3],
        compiler_params=_params("arbitrary", "arbitrary"),
    )(q, k, k, v, v, *side_args)


def _attn_bwd(q, k, v, o, do, lse, ones_hp, side=None, side_args=()):
    n_hp, t_len, _ = q.shape
    nt = t_len // TILE
    ns = side.n if side is not None else 0
    n_pat = len(DILATIONS)

    def body(*refs):
        q_ref, kp_ref, kc_ref, vp_ref, vc_ref, o_ref, do_ref, lse_ref, bd_ref = refs[0:9]
        dq_ref, dk_ref, dv_ref = refs[9 + ns:12 + ns]
        kk, vv, dq_s, dkc, dkp, dvc, dvp, hold_k, hold_v, dl_s = refs[12 + 2 * ns:22 + 2 * ns]
        t = pl.program_id(1)
        if side is not None:
            side.bind(refs[9:9 + ns], refs[12 + ns:12 + 2 * ns], refs[22 + 2 * ns:])
            pl.when((pl.program_id(0) == 0) & (t == 0))(side.start)

        @pl.when(t < nt)
        def _():
            kk[0:TILE] = kp_ref[0]
            kk[TILE:] = kc_ref[0]
            vv[0:TILE] = vp_ref[0]
            vv[TILE:] = vc_ref[0]
            dl_s[...] = _group_mean(do_ref[...] * o_ref[...], bd_ref[...], 1.0)
            head0, band_prev, band_cur = _attn_masks()

            for pi, d in enumerate(DILATIONS):
                def unit(uidx, carry, d=d, pi=pi):
                    b, rows_q, rows_c, rows_p = _unit_rows(uidx, d)
                    mask = band_cur | (band_prev & ((t > 0) | (b > 0)))
                    q2 = _stack_heads(q_ref.at[0][rows_q, :], head0)
                    do2 = _stack_heads(do_ref[rows_q, :], head0)
                    lse_f = lse_ref.at[0][rows_q, :]
                    dl_f = dl_s[rows_q, :]
                    lse2 = jnp.concatenate([lse_f[:, 0:1], lse_f[:, HEAD:HEAD + 1]], axis=0)
                    dl2 = jnp.concatenate([dl_f[:, 0:1], dl_f[:, HEAD:HEAD + 1]], axis=0)
                    kcat = jnp.concatenate([kk[rows_p, :], kk[rows_c, :]], axis=0).astype(BF16)
                    vcat = jnp.concatenate([vv[rows_p, :], vv[rows_c, :]], axis=0).astype(BF16)
                    p = jnp.where(mask, jnp.exp(_nt(q2, kcat) - lse2), 0.0)
                    ds = (p * (_nt(do2, vcat) - dl2)).astype(BF16)
                    dq2 = _mm(ds, kcat)
                    dq_s.at[pi][rows_q, :] = jnp.where(head0, dq2[0:BLK], dq2[BLK:])
                    dk2 = _tn(ds, q2)
                    dv2 = _tn(p.astype(BF16), do2)
                    dkp.at[pi][rows_q, :] = dk2[0:BLK]
                    dkc.at[pi][rows_q, :] = dk2[BLK:]
                    dvp.at[pi][rows_q, :] = dv2[0:BLK]
                    dvc.at[pi][rows_q, :] = dv2[BLK:]
                    return carry

                lax.fori_loop(0, TILE // BLK, unit, 0, unroll=16)

            dq_ref[...] = dq_s[0] + dq_s[1] + dq_s[2]

        @pl.when(t > 0)
        def _():
            dk_ref[...] = hold_k[...]
            dv_ref[...] = hold_v[...]

        @pl.when((t > 0) & (t < nt))
        def _():
            for pi, d in enumerate(DILATIONS):
                back = d * BLK
                dk_ref[TILE - back:, :] = dk_ref[TILE - back:, :] + dkp[pi, 0:back, :]
                dv_ref[TILE - back:, :] = dv_ref[TILE - back:, :] + dvp[pi, 0:back, :]

        @pl.when(t < nt)
        def _():
            hold_k[...] = dkc[0] + dkc[1] + dkc[2]
            hold_v[...] = dvc[0] + dvc[1] + dvc[2]
            for pi, d in enumerate(DILATIONS):
                back = d * BLK
                if back < TILE:
                    hold_k[0:TILE - back, :] = hold_k[0:TILE - back, :] + dkp[pi, back:, :]
                    hold_v[0:TILE - back, :] = hold_v[0:TILE - back, :] + dvp[pi, back:, :]

        if side is not None:
            pl.when((pl.program_id(0) == n_hp - 1) & (t == nt))(side.finish)

    last = nt - 1
    extra = (side.in_specs(), side.out_specs(), side.out_shapes(), side.scratch_shapes()) if side is not None else ([], [], [], [])
    cur = lambda hp, t: (hp, jnp.minimum(t, last), 0)
    prev = lambda hp, t: (hp, jnp.clip(t - 1, 0, last), 0)
    cur2 = lambda hp, t: (jnp.minimum(t, last), hp)
    prev2 = lambda hp, t: (jnp.maximum(t - 1, 0), hp)
    blk = (1, TILE, LANES)
    blk2 = (TILE, LANES)
    out = jax.ShapeDtypeStruct((t_len, ATTN_W), F32)
    return pl.pallas_call(
        body, name="attn_bwd", grid=(n_hp, nt + 1),
        in_specs=[pl.BlockSpec(blk, cur), pl.BlockSpec(blk, prev), pl.BlockSpec(blk, cur),
                  pl.BlockSpec(blk, prev), pl.BlockSpec(blk, cur), pl.BlockSpec(blk2, cur2),
                  pl.BlockSpec(blk2, cur2), pl.BlockSpec(blk, cur), pl.BlockSpec((LANES, LANES), lambda hp, t: (0, 0))]
        + extra[0],
        out_specs=[pl.BlockSpec(blk2, cur2), pl.BlockSpec(blk2, prev2), pl.BlockSpec(blk2, prev2)] + extra[1],
        out_shape=[out, out, out] + extra[2],
        scratch_shapes=[pltpu.VMEM((2 * TILE, LANES), F32), pltpu.VMEM((2 * TILE, LANES), F32)]
        + [pltpu.VMEM((n_pat, TILE, LANES), F32)] * 5 + [pltpu.VMEM((TILE, LANES), F32)] * 3 + extra[3],
        compiler_params=_params("arbitrary", "arbitrary"),
    )(q, k, k, v, v, o, do, lse, ones_hp, *side_args)


def _discretise(lr, li, ldt, br, bi):
    dt = jnp.exp(ldt)
    mag = jnp.exp(lr * dt)
    ab_r, ab_i = mag * jnp.cos(li * dt), mag * jnp.sin(li * dt)
    den = lr * lr + li * li
    nr, ni = ab_r - 1.0, ab_i
    cr = (nr * lr + ni * li) / den
    ci = (ni * lr - nr * li) / den
    return ab_r, ab_i, cr * br - ci * bi, cr * bi + ci * br


def _disc_fwd(lr, li, ldt, br, bi):
    def body(lr_ref, li_ref, ldt_ref, br_ref, bi_ref, ar_o, ai_o, bbr_o, bbi_o):
        outs = _discretise(lr_ref[...], li_ref[...], ldt_ref[...], br_ref[...], bi_ref[...])
        for o_ref, val in zip((ar_o, ai_o, bbr_o, bbi_o), outs):
            o_ref[...] = val

    col = jax.ShapeDtypeStruct(lr.shape, F32)
    mat = jax.ShapeDtypeStruct(br.shape, F32)
    return pl.pallas_call(body, name="s5_disc_fwd", out_shape=[col, col, mat, mat])(lr, li, ldt, br, bi)


def _disc_bwd(lr, li, ldt, br, bi, d_ar, d_ai, d_bbr, d_bbi, group_sum):
    def body(lr_ref, li_ref, ldt_ref, br_ref, bi_ref, c1, c2, c3, c4, gs_ref, dlr_o, dli_o, dldt_o, dbr_o, dbi_o):
        _, vjp = jax.vjp(_discretise, lr_ref[...], li_ref[...], ldt_ref[...], br_ref[...], bi_ref[...])
        dlr, dli, dldt, dbr, dbi = vjp((c1[...], c2[...], c3[...], c4[...]))
        dlr_o[...] = dlr
        dli_o[...] = dli
        dbr_o[...] = dbr
        dbi_o[...] = dbi
        wide = jnp.broadcast_to(dldt, (dldt.shape[0], LANES))
        dldt_o[...] = jnp.dot(gs_ref[...], wide, precision=lax.Precision.HIGHEST, preferred_element_type=F32)

    col = jax.ShapeDtypeStruct(lr.shape, F32)
    mat = jax.ShapeDtypeStruct(br.shape, F32)
    return pl.pallas_call(
        body, name="s5_disc_bwd", out_shape=[col, col, jax.ShapeDtypeStruct((N_GROUPS, LANES), F32), mat, mat],
    )(lr, li, ldt, br, bi, d_ar, d_ai, d_bbr, d_bbi, group_sum)


N_CHUNK = TILE // BLK
HALF = 4


def _cmul(ar, ai, xr, xi):
    return ar * xr - ai * xi, ar * xi + ai * xr


def _power_table(a_ref, tab, sign, reverse):
    ar = [a_ref[0, j:j + 1, :] for j in range(HALF)]
    ai = [sign * a_ref[0, HALF + j:HALF + j + 1, :] for j in range(HALF)]

    def step(s, cur):
        row = pl.ds((BLK - 1 - s) if reverse else s, 1)
        nxt = []
        for j in range(HALF):
            tab.at[j][row, :] = cur[j]
            tab.at[HALF + j][row, :] = cur[HALF + j]
            nxt.append(_cmul(ar[j], ai[j], cur[j], cur[HALF + j]))
        return tuple(p[0] for p in nxt) + tuple(p[1] for p in nxt)

    lax.fori_loop(0, BLK, step, tuple(ar) + tuple(ai))


def _interleave(src, dst):
    for c in range(N_CHUNK):
        dst[pl.ds(c, BLK, stride=N_CHUNK), :] = src[c * BLK:(c + 1) * BLK, :]


def _deinterleave(src, dst):
    for c in range(N_CHUNK):
        dst[c * BLK:(c + 1) * BLK, :] = src[pl.ds(c, BLK, stride=N_CHUNK), :]


def _step_rows(s):
    return pl.ds(pl.multiple_of(s * N_CHUNK, N_CHUNK), N_CHUNK)


def _chunk_scan(buf, a_ref, sign, reverse):
    ar = [jnp.broadcast_to(a_ref[0, j:j + 1, :], (N_CHUNK, LANES)) for j in range(HALF)]
    ai = [sign * jnp.broadcast_to(a_ref[0, HALF + j:HALF + j + 1, :], (N_CHUNK, LANES)) for j in range(HALF)]

    def step(i, carry):
        s = (BLK - 1 - i) if reverse else i
        rows = _step_rows(s)
        out = []
        for j in range(HALF):
            pr, pi = _cmul(ar[j], ai[j], carry[j], carry[HALF + j])
            xr = buf.at[j][rows, :] + pr
            xi = buf.at[HALF + j][rows, :] + pi
            buf.at[j][rows, :] = xr
            buf.at[HALF + j][rows, :] = xi
            out.append((xr, xi))
        return tuple(p[0] for p in out) + tuple(p[1] for p in out)

    zero = jnp.zeros((N_CHUNK, LANES), F32)
    lax.fori_loop(0, BLK, step, (zero,) * (2 * HALF), unroll=2)


def _chunk_states(buf, carry_s, xin_s, tab, reverse):
    edge = 0 if reverse else BLK - 1
    top = 0 if reverse else BLK - 1
    pw = [tab[j, top:top + 1, :] for j in range(2 * HALF)]
    cur = [carry_s[j:j + 1, :] for j in range(2 * HALF)]
    summary = [buf[j, edge * N_CHUNK:(edge + 1) * N_CHUNK, :] for j in range(2 * HALF)]
    order = range(N_CHUNK - 1, -1, -1) if reverse else range(N_CHUNK)
    for c in order:
        for j in range(2 * HALF):
            xin_s[j, c:c + 1, :] = cur[j]
        nxt = []
        for j in range(HALF):
            pr, pi = _cmul(pw[j], pw[HALF + j], cur[j], cur[HALF + j])
            nxt.append((pr + summary[j][c:c + 1, :], pi + summary[HALF + j][c:c + 1, :]))
        cur = [p[0] for p in nxt] + [p[1] for p in nxt]
    for j in range(2 * HALF):
        carry_s[j:j + 1, :] = cur[j]


def _s5_fwd(u, a_cat, b_mat, c_mat, d_skip):
    t_len = u.shape[0]
    nt = t_len // TILE

    def body(u_ref, a_ref, b_ref, c_ref, d_ref, y_ref, x_ref, xs, us, tab, carry_s, xin_s):
        sb = pl.program_id(1)

        @pl.when(sb == 0)
        def _():
            _power_table(a_ref, tab, 1.0, False)
            carry_s[...] = jnp.zeros_like(carry_s)

        _interleave(u_ref, us)
        uv = us[...]
        bu = _mm(uv.astype(BF16), b_ref[0])
        for j in range(2 * HALF):
            xs[j] = bu[:, j * LANES:(j + 1) * LANES]
        _chunk_scan(xs, a_ref, 1.0, False)
        _chunk_states(xs, carry_s, xin_s, tab, False)
        xin = [xin_s[j] for j in range(2 * HALF)]

        def fix(s, acc):
            rows = _step_rows(s)
            for j in range(HALF):
                pr, pi = _cmul(tab.at[j][pl.ds(s, 1), :], tab.at[HALF + j][pl.ds(s, 1), :], xin[j], xin[HALF + j])
                xs.at[j][rows, :] = xs.at[j][rows, :] + pr
                xs.at[HALF + j][rows, :] = xs.at[HALF + j][rows, :] + pi
            return acc

        lax.fori_loop(0, BLK, fix, 0, unroll=2)
        xcat = jnp.concatenate([xs[j].astype(BF16) for j in range(2 * HALF)], axis=1)
        x_ref[0] = xcat
        us[...] = d_ref[0] * uv + _mm(xcat, c_ref[0])
        _deinterleave(us, y_ref)

    return pl.pallas_call(
        body, name="s5_fwd", grid=(N_LB, nt),
        in_specs=[pl.BlockSpec((TILE, LANES), lambda lb, sb: (sb, lb)),
                  pl.BlockSpec((1, 2 * HALF, LANES), lambda lb, sb: (lb, 0, 0)),
                  pl.BlockSpec((1, LANES, 2 * HALF * LANES), lambda lb, sb: (lb, 0, 0)),
                  pl.BlockSpec((1, 2 * HALF * LANES, LANES), lambda lb, sb: (lb, 0, 0)),
                  pl.BlockSpec((1, 1, LANES), lambda lb, sb: (lb, 0, 0))],
        out_specs=[pl.BlockSpec((TILE, LANES), lambda lb, sb: (sb, lb)),
                   pl.BlockSpec((1, TILE, 2 * HALF * LANES), lambda lb, sb: (lb, sb, 0))],
        out_shape=[jax.ShapeDtypeStruct((t_len, SSM_W), F32), jax.ShapeDtypeStruct((N_LB, t_len, 2 * HALF * LANES), BF16)],
        scratch_shapes=[pltpu.VMEM((2 * HALF, TILE, LANES), F32), pltpu.VMEM((TILE, LANES), F32),
                        pltpu.VMEM((2 * HALF, BLK, LANES), F32),
                        pltpu.VMEM((2 * HALF, LANES), F32), pltpu.VMEM((2 * HALF, N_CHUNK, LANES), F32)],
        compiler_params=_params("arbitrary", "arbitrary"),
    )(u, a_cat, b_mat, c_mat, d_skip)


def _s5_bwd(u, dy, states, a_cat, b_mat, c_mat, d_skip):
    t_len = u.shape[0]
    nt = t_len // TILE
    last = nt - 1

    def body(u_ref, dy_ref, x_ref, a_ref, b_ref, c_ref, d_ref, du_ref, db_ref, dc_ref, da_ref, dd_ref,
             gs, us, dys, tabc, lam_s, lin_s):
        sb = pl.program_id(1)

        @pl.when(sb == 0)
        def _():
            _power_table(a_ref, tabc, -1.0, True)
            lam_s[...] = jnp.zeros_like(lam_s)
            db_ref[...] = jnp.zeros_like(db_ref)
            dc_ref[...] = jnp.zeros_like(dc_ref)
            da_ref[...] = jnp.zeros_like(da_ref)
            dd_ref[...] = jnp.zeros_like(dd_ref)

        _interleave(u_ref, us)
        _interleave(dy_ref, dys)
        uv = us[...]
        dyv = dys[...]
        ub = uv.astype(BF16)
        dyb = dyv.astype(BF16)
        gy = _nt(dyb, c_ref[0])
        for j in range(2 * HALF):
            gs[j] = gy[:, j * LANES:(j + 1) * LANES]
        _chunk_scan(gs, a_ref, -1.0, True)
        _chunk_states(gs, lam_s, lin_s, tabc, True)
        zero = jnp.zeros((N_CHUNK, LANES), F32)
        x_tile = x_ref.at[0]
        for grp in range(0, HALF, 2):
            slabs = (grp, grp + 1)
            lin = [(lin_s[j], lin_s[HALF + j]) for j in slabs]

            def fix(i, carry, slabs=slabs, lin=lin):
                s = BLK - 1 - i
                rows = _step_rows(s)
                out = []
                for k, j in enumerate(slabs):
                    nr, ni, acc_r, acc_i = carry[4 * k:4 * k + 4]
                    xr = x_tile[rows, pl.ds(j * LANES, LANES)].astype(F32)
                    xi = x_tile[rows, pl.ds((HALF + j) * LANES, LANES)].astype(F32)
                    qr, qi = _cmul(tabc.at[j][pl.ds(s, 1), :], tabc.at[HALF + j][pl.ds(s, 1), :], lin[k][0], lin[k][1])
                    lr_ = gs.at[j][rows, :] + qr
                    li_ = gs.at[HALF + j][rows, :] + qi
                    gs.at[j][rows, :] = lr_
                    gs.at[HALF + j][rows, :] = li_
                    out += [lr_, li_, acc_r + (xr * nr + xi * ni), acc_i + (xr * ni - xi * nr)]
                return tuple(out)

            init = []
            for k in range(len(slabs)):
                init += [lin[k][0], lin[k][1], zero, zero]
            res = lax.fori_loop(0, BLK, fix, tuple(init), unroll=2)
            for k, j in enumerate(slabs):
                da_ref[0, j:j + 1, :] = da_ref[0, j:j + 1, :] + _colsum(res[4 * k + 2])
                da_ref[0, HALF + j:HALF + j + 1, :] = da_ref[0, HALF + j:HALF + j + 1, :] + _colsum(res[4 * k + 3])
        lam = jnp.concatenate([gs[j].astype(BF16) for j in range(2 * HALF)], axis=1)
        us[...] = _nt(lam, b_ref[0]) + d_ref[0] * dyv
        _deinterleave(us, du_ref)
        db_ref[0] = db_ref[0] + _tn(ub, lam)
        dc_ref[0] = dc_ref[0] + _tn(dyb, x_ref[0])
        dd_ref[0] = dd_ref[0] + _colsum(dyv * uv)

    rev = lambda lb, sb: (last - sb, lb)
    per_lb = lambda lb, sb: (lb, 0, 0)
    wide = 2 * HALF * LANES
    return pl.pallas_call(
        body, name="s5_bwd", grid=(N_LB, nt),
        in_specs=[pl.BlockSpec((TILE, LANES), rev), pl.BlockSpec((TILE, LANES), rev),
                  pl.BlockSpec((1, TILE, wide), lambda lb, sb: (lb, last - sb, 0)),
                  pl.BlockSpec((1, 2 * HALF, LANES), per_lb), pl.BlockSpec((1, LANES, wide), per_lb),
                  pl.BlockSpec((1, wide, LANES), per_lb), pl.BlockSpec((1, 1, LANES), per_lb)],
        out_specs=[pl.BlockSpec((TILE, LANES), rev), pl.BlockSpec((1, LANES, wide), per_lb),
                   pl.BlockSpec((1, LANES, wide), per_lb), pl.BlockSpec((1, 2 * HALF, LANES), per_lb),
                   pl.BlockSpec((1, 1, LANES), per_lb)],
        out_shape=[jax.ShapeDtypeStruct((t_len, SSM_W), F32), jax.ShapeDtypeStruct((N_LB, LANES, wide), F32),
                   jax.ShapeDtypeStruct((N_LB, LANES, wide), F32), jax.ShapeDtypeStruct((N_LB, 2 * HALF, LANES), F32),
                   jax.ShapeDtypeStruct((N_LB, 1, LANES), F32)],
        scratch_shapes=[pltpu.VMEM((2 * HALF, TILE, LANES), F32),
                        pltpu.VMEM((TILE, LANES), F32), pltpu.VMEM((TILE, LANES), F32),
                        pltpu.VMEM((2 * HALF, BLK, LANES), F32), pltpu.VMEM((2 * HALF, LANES), F32),
                        pltpu.VMEM((2 * HALF, N_CHUNK, LANES), F32)],
        compiler_params=_params("arbitrary", "arbitrary"),
    )(u, dy, states, a_cat, b_mat, c_mat, d_skip)


_GELU_C = math.sqrt(2.0 / math.pi)
_GELU_K = 0.044715


def _gelu(y):
    t = jnp.tanh(_GELU_C * (y + _GELU_K * (y * y * y)))
    return y * (0.5 * (1.0 + t)), t


def _gelu_grad(y, t):
    return 0.5 * (1.0 + t) + 0.5 * y * (1.0 - t * t) * (_GELU_C * (1.0 + 3.0 * _GELU_K * y * y))


def _glu(y, wg, bias):
    z, t = _gelu(y)
    sg = jax.nn.sigmoid(_mm(z.astype(BF16), wg) + bias)
    return z, t, sg


def _mixer_mlp(attn, y, x, target, wg, glu_b, ga, gs, wo, g2, wu, wd):
    t_len = x.shape[0]
    tm = 256
    fc = 1024
    n_fc = D_FF // fc

    def body(attn_ref, y_ref, x_ref, tg_ref, b_ref, ga_ref, gs_ref, g2_ref, wg_s, wo_s, wu_hbm, wd_hbm,
             dx2_ref, hdn_ref, dup_ref, h_ref, dyb_ref, mix_ref, z_ref, dattn_ref, dy_ref, dx2b_ref, dgp_ref,
             dg2_ref, loss_ref, dga_ref, dgs_ref, db_ref, wu_s, wd_s, relu_s, sem):
        @pl.when(pl.program_id(0) == 0)
        def _():
            copies = [pltpu.make_async_copy(src, dst, sem.at[k]) for k, (src, dst) in enumerate(((wu_hbm, wu_s), (wd_hbm, wd_s)))]
            for cp in copies:
                cp.start()
            for cp in copies:
                cp.wait()
            for acc in (dg2_ref, loss_ref, dga_ref, dgs_ref, db_ref):
                acc[...] = jnp.zeros_like(acc)

        av = attn_ref[...]
        z, _, sg = _glu(y_ref[...], wg_s[...], b_ref[...])
        z_ref[...] = z.astype(BF16)
        s = z * sg
        an = (av * _rms(av) * ga_ref[...]).astype(BF16)
        sn = (s * _rms(s) * gs_ref[...]).astype(BF16)
        mix_ref[:, 0:ATTN_W] = an
        mix_ref[:, ATTN_W:] = sn
        dx2_ref[...] = x_ref[...] + _mm(an, wo_s[0:ATTN_W, :]) + _mm(sn, wo_s[ATTN_W:, :])
        r = _rms(dx2_ref[...])
        g2v = g2_ref[...]
        h = (dx2_ref[...] * r * g2v).astype(BF16)
        h_ref[...] = h
        yout = dx2_ref[...]
        for c in range(n_fc):
            cols = slice(c * fc, (c + 1) * fc)
            ru = jnp.maximum(_mm(h, wu_s[:, cols]), 0.0)
            relu_s[:, cols] = ru
            hd = (ru * ru).astype(BF16)
            hdn_ref[:, cols] = hd
            yout = yout + _mm(hd, wd_s[cols, :])
        err = yout - tg_ref[...]
        loss_ref[...] = loss_ref[...] + 0.5 * jnp.sum(err * err) * (1.0 / D_MODEL)
        dy = err * (1.0 / D_MODEL)
        dyb = dy.astype(BF16)
        dyb_ref[...] = dyb
        dh = jnp.zeros((tm, D_MODEL), F32)
        for c in range(n_fc):
            cols = slice(c * fc, (c + 1) * fc)
            dup = (_nt(dyb, wd_s[cols, :]) * (2.0 * relu_s[:, cols])).astype(BF16)
            dup_ref[:, cols] = dup
            dh = dh + _nt(dup, wu_s[:, cols])
        dxn, g2_term = _rms_bwd(dh, dx2_ref[...], r, g2v)
        dx2_ref[...] = dy + dxn
        dg2_ref[...] = dg2_ref[...] + _colsum(g2_term)
        dx2b = dx2_ref[...].astype(BF16)
        dx2b_ref[...] = dx2b
        yv = y_ref[...]
        av = attn_ref[...]
        z, t, sg = _glu(yv, wg_s[...], b_ref[...])
        s = z * sg
        d_attn, ga_term = _rms_bwd(_nt(dx2b, wo_s[0:ATTN_W, :]), av, _rms(av), ga_ref[...])
        d_s, gs_term = _rms_bwd(_nt(dx2b, wo_s[ATTN_W:, :]), s, _rms(s), gs_ref[...])
        dattn_ref[...] = d_attn
        dgp = d_s * z * sg * (1.0 - sg)
        dgpb = dgp.astype(BF16)
        dgp_ref[...] = dgpb
        dy_ref[...] = (d_s * sg + _nt(dgpb, wg_s[...])) * _gelu_grad(yv, t)
        dga_ref[...] = dga_ref[...] + _colsum(ga_term)
        dgs_ref[...] = dgs_ref[...] + _colsum(gs_term)
        db_ref[...] = db_ref[...] + _colsum(dgp)

    row = lambda i: (i, 0)
    const = lambda i: (0, 0)
    wide = lambda n: pl.BlockSpec((tm, n), row)
    vec = lambda n: pl.BlockSpec((1, n), const)
    any_spec = pl.BlockSpec(memory_space=pl.ANY)
    f32 = lambda n: jax.ShapeDtypeStruct((t_len, n), F32)
    b16 = lambda n: jax.ShapeDtypeStruct((t_len, n), BF16)
    acc = lambda n: jax.ShapeDtypeStruct((1, n), F32)
    return pl.pallas_call(
        body, name="mixer_mlp", grid=(t_len // tm,),
        in_specs=[wide(ATTN_W), wide(SSM_W), wide(D_MODEL), wide(D_MODEL), vec(SSM_W), vec(ATTN_W), vec(SSM_W), vec(D_MODEL),
                  pl.BlockSpec((SSM_W, SSM_W), const), pl.BlockSpec((D_MODEL, D_MODEL), const), any_spec, any_spec],
        out_specs=[wide(D_MODEL), wide(D_FF), wide(D_FF), wide(D_MODEL), wide(D_MODEL), wide(D_MODEL), wide(SSM_W),
                   wide(ATTN_W), wide(SSM_W), wide(D_MODEL), wide(SSM_W),
                   vec(D_MODEL), vec(LANES), vec(ATTN_W), vec(SSM_W), vec(SSM_W)],
        out_shape=[f32(D_MODEL), b16(D_FF), b16(D_FF), b16(D_MODEL), b16(D_MODEL), b16(D_MODEL), b16(SSM_W),
                   f32(ATTN_W), f32(SSM_W), b16(D_MODEL), b16(SSM_W),
                   acc(D_MODEL), acc(LANES), acc(ATTN_W), acc(SSM_W), acc(SSM_W)],
        scratch_shapes=[pltpu.VMEM((D_MODEL, D_FF), BF16), pltpu.VMEM((D_FF, D_MODEL), BF16),
                        pltpu.VMEM((tm, D_FF), F32), pltpu.SemaphoreType.DMA((2,))],
        compiler_params=_params("arbitrary"),
    )(attn, y, x, target, glu_b, ga, gs, g2, wg, wo, wu, wd)


def _inproj_bwd(dqs, dkn, dv, du, q_raw, k_raw, x, dx2, wi, g1, gq, gk, ones64):
    t_len = x.shape[0]
    tm = 512
    n_heads = ATTN_W // HEAD

    def body(dqs_ref, dkn_ref, dv_ref, du_ref, q_ref, k_ref, x_ref, dx2_ref, wi_ref, g1_ref, gq_ref, gk_ref, bd_ref,
             gx_ref, dproj_ref, dg1_ref, dgq_ref, dgk_ref, accq, acck):
        i = pl.program_id(0)

        @pl.when(i == 0)
        def _():
            dg1_ref[...] = jnp.zeros_like(dg1_ref)
            accq[...] = jnp.zeros_like(accq)
            acck[...] = jnp.zeros_like(acck)

        bd = bd_ref[...]

        def head_norm_bwd(dy, raw, gain, acc):
            r = lax.rsqrt(_group_mean(raw * raw, bd, HEAD) + EPS)
            xh = raw * r
            dxh = dy * gain
            acc[...] = acc[...] + _colsum(dy * xh)
            return r * (dxh - xh * _group_mean(dxh * xh, bd, HEAD))

        dq = head_norm_bwd(dqs_ref[...] * (HEAD ** -0.5), q_ref[...], gq_ref[...], accq)
        dk = head_norm_bwd(dkn_ref[...], k_ref[...], gk_ref[...], acck)
        dproj_ref[:, 0:ATTN_W] = dq.astype(BF16)
        dproj_ref[:, ATTN_W:2 * ATTN_W] = dk.astype(BF16)
        dproj_ref[:, 2 * ATTN_W:3 * ATTN_W] = dv_ref[...].astype(BF16)
        dproj_ref[:, 3 * ATTN_W:] = du_ref[...].astype(BF16)
        dxn = _nt(dproj_ref[...], wi_ref[...])
        xv = x_ref[...]
        g1v = g1_ref[...]
        dx, g1_term = _rms_bwd(dxn, xv, _rms(xv), g1v)
        gx_ref[...] = dx2_ref[...] + dx
        dg1_ref[...] = dg1_ref[...] + _colsum(g1_term)

        @pl.when(i == pl.num_programs(0) - 1)
        def _():
            for acc, out in ((accq, dgq_ref), (acck, dgk_ref)):
                tot = acc[:, 0:HEAD]
                for h in range(1, n_heads):
                    tot = tot + acc[:, h * HEAD:(h + 1) * HEAD]
                out[...] = tot

    row = lambda i: (i, 0)
    const = lambda i: (0, 0)
    aw = pl.BlockSpec((tm, ATTN_W), row)
    dm = pl.BlockSpec((tm, D_MODEL), row)
    return pl.pallas_call(
        body, name="inproj_bwd", grid=(t_len // tm,),
        in_specs=[aw, aw, aw, aw, aw, aw, dm, dm, pl.BlockSpec((D_MODEL, PROJ_W), const), pl.BlockSpec((1, D_MODEL), const),
                  pl.BlockSpec((1, ATTN_W), const), pl.BlockSpec((1, ATTN_W), const), pl.BlockSpec(ones64.shape, const)],
        out_specs=[dm, pl.BlockSpec((tm, PROJ_W), row), pl.BlockSpec((1, D_MODEL), const),
                   pl.BlockSpec((1, HEAD), const), pl.BlockSpec((1, HEAD), const)],
        out_shape=[jax.ShapeDtypeStruct((t_len, D_MODEL), F32), jax.ShapeDtypeStruct((t_len, PROJ_W), BF16),
                   jax.ShapeDtypeStruct((1, D_MODEL), F32), jax.ShapeDtypeStruct((1, HEAD), F32),
                   jax.ShapeDtypeStruct((1, HEAD), F32)],
        scratch_shapes=[pltpu.VMEM((1, ATTN_W), F32), pltpu.VMEM((1, ATTN_W), F32)],
        compiler_params=_params("arbitrary"),
    )(dqs, dkn, dv, du, q_raw, k_raw, x, dx2, wi, g1, gq, gk, ones64)


def _grad_matmul(a, b, name):
    t_len, m = a.shape
    n = b.shape[1]
    bm, bn, bt = min(m, 1024), min(n, 1024), min(t_len, 4096)

    def body(a_ref, b_ref, o_ref):
        @pl.when(pl.program_id(2) == 0)
        def _():
            o_ref[...] = jnp.zeros_like(o_ref)

        o_ref[...] = o_ref[...] + _tn(a_ref[...], b_ref[...])

    return pl.pallas_call(
        body, name=name, grid=(m // bm, n // bn, t_len // bt),
        in_specs=[pl.BlockSpec((bt, bm), lambda i, j, k: (k, i)), pl.BlockSpec((bt, bn), lambda i, j, k: (k, j))],
        out_specs=pl.BlockSpec((bm, bn), lambda i, j, k: (i, j)),
        out_shape=jax.ShapeDtypeStruct((m, n), F32),
        compiler_params=_params("arbitrary", "arbitrary", "arbitrary"),
    )(a, b)


def _adamw_update(w_ref, g_ref, m_ref, v_ref, d_o, m_o, v_o):
    gv = g_ref[...]
    mn = ADAM_B1 * m_ref[...] + (1.0 - ADAM_B1) * gv
    vn = ADAM_B2 * v_ref[...] + (1.0 - ADAM_B2) * jnp.square(gv)
    m_hat = mn / (1.0 - ADAM_B1 ** ADAM_STEP)
    v_hat = vn / (1.0 - ADAM_B2 ** ADAM_STEP)
    d_o[...] = -ADAM_LR * (m_hat / (jnp.sqrt(v_hat) + ADAM_EPS) + ADAM_WD * w_ref[...])
    m_o[...] = mn
    v_o[...] = vn


def _adamw_many(ws, gs, ms, vs):
    n = len(ws)

    def body(*refs):
        for i in range(n):
            _adamw_update(*[refs[k * n + i] for k in range(7)])

    shapes = [jax.ShapeDtypeStruct(w.shape, F32) for w in ws]
    outs = pl.pallas_call(body, name="adamw_small", out_shape=shapes * 3,
                          compiler_params=pltpu.CompilerParams(vmem_limit_bytes=VMEM_LIMIT))(*ws, *gs, *ms, *vs)
    return outs[0:n], outs[n:2 * n], outs[2 * n:]


def _adamw(w, g, m, v, name):
    rows, cols = w.shape
    br = _row_block(rows, 256)
    body = functools.partial(_adamw_update)

    spec = pl.BlockSpec((br, cols), lambda i: (i, 0))
    shape = jax.ShapeDtypeStruct((rows, cols), F32)
    return pl.pallas_call(
        body, name=name, grid=(rows // br,), in_specs=[spec] * 4, out_specs=[spec] * 3, out_shape=[shape] * 3,
        compiler_params=_params("arbitrary"),
    )(w, g, m, v)


def _sum_arrays(arrs, name, out_dtype=F32):
    rows, cols = arrs[0].shape
    br = _row_block(rows, 512)
    n = len(arrs)

    def body(*refs):
        tot = refs[0][...]
        for r in refs[1:n]:
            tot = tot + r[...]
        refs[n][...] = tot.astype(out_dtype)

    spec = pl.BlockSpec((br, cols), lambda i: (i, 0))
    return pl.pallas_call(
        body, name=name, grid=(rows // br,), in_specs=[spec] * n, out_specs=spec,
        out_shape=jax.ShapeDtypeStruct((rows, cols), out_dtype), compiler_params=_params("arbitrary"),
    )(*arrs)


GPL = N_GROUPS // N_LB
SW = GPL * N_STATE


def _eye_groups():
    return jnp.eye(GPL, dtype=F32)


def _s5_matrices(ab_r, ab_i, bb_r, bb_i, c_re, c_im, d_skip):
    eye = _eye_groups()
    a_cat = jnp.concatenate([ab_r.reshape(N_LB, HALF, LANES), ab_i.reshape(N_LB, HALF, LANES)], axis=1)

    def b_part(bb):
        b4 = jnp.transpose(bb.reshape(N_LB, GPL, N_STATE, GROUP_W), (0, 1, 3, 2))
        return (b4[:, :, :, None, :] * eye[None, :, None, :, None]).reshape(N_LB, LANES, SW)

    def c_part(cc):
        c4 = jnp.transpose(cc.reshape(N_LB, GPL, GROUP_W, N_STATE), (0, 1, 3, 2))
        return (c4[:, :, :, None, :] * eye[None, :, None, :, None]).reshape(N_LB, SW, LANES)

    b_mat = jnp.concatenate([b_part(bb_r), b_part(bb_i)], axis=2).astype(BF16)
    c_mat = jnp.concatenate([c_part(c_re), -c_part(c_im)], axis=1).astype(BF16)
    return a_cat, b_mat, c_mat, d_skip.reshape(N_LB, 1, LANES)


def _s5_unpack_grads(db, dc, da, dd):
    eye = _eye_groups()
    mask = eye[None, :, None, None, :, None]
    d6 = jnp.sum(db.reshape(N_LB, GPL, GROUP_W, 2, GPL, N_STATE) * mask, axis=4)
    dbb = jnp.transpose(d6, (3, 0, 1, 4, 2)).reshape(2, N_GROUPS * N_STATE, GROUP_W)
    c6 = jnp.sum(dc.reshape(N_LB, GPL, GROUP_W, 2, GPL, N_STATE) * mask, axis=4)
    dcc = jnp.transpose(c6, (3, 0, 1, 2, 4)).reshape(2, N_GROUPS, GROUP_W, N_STATE)
    dab_r = da[:, :HALF].reshape(N_GROUPS * N_STATE, 1)
    dab_i = da[:, HALF:].reshape(N_GROUPS * N_STATE, 1)
    return dab_r, dab_i, dbb[0], dbb[1], dcc[0], -dcc[1], dd.reshape(N_GROUPS, GROUP_W)


def _block_ones(n, width):
    i = lax.broadcasted_iota(jnp.int32, (n, n), 0) // width
    j = lax.broadcasted_iota(jnp.int32, (n, n), 1) // width
    return (i == j).astype(BF16)


def _tile_heads(g):
    return jnp.tile(g.reshape(1, HEAD), (1, ATTN_W // HEAD))


def _local_step(x, target, wi, rest, p, fwd_side=None, bwd_side=None):
    ones64 = _block_ones(MXU_WIDTH, HEAD)
    ones_hp = _block_ones(LANES, HEAD)
    g1 = p["norm1_g"].reshape(1, D_MODEL)
    g2 = p["norm2_g"].reshape(1, D_MODEL)
    gq = _tile_heads(p["q_norm_g"])
    gk = _tile_heads(p["k_norm_g"])
    ga = p["attn_out_norm_g"].reshape(1, ATTN_W)
    gs = p["ssm_out_norm_g"].reshape(1, SSM_W)
    glu_b = p["glu_b"].reshape(1, SSM_W)
    n_gp = N_GROUPS * N_STATE
    lr = p["ssm_a_re"].reshape(n_gp, 1)
    li = p["ssm_a_im"].reshape(n_gp, 1)
    ldt = jnp.repeat(p["ssm_log_dt"].reshape(N_GROUPS), N_STATE).reshape(n_gp, 1)
    br = p["ssm_b_re"].reshape(n_gp, GROUP_W)
    bi = p["ssm_b_im"].reshape(n_gp, GROUP_W)
    ab_r, ab_i, bb_r, bb_i = _disc_fwd(lr, li, ldt, br, bi)
    a_cat, b_mat, c_mat, d_mat = _s5_matrices(
        ab_r, ab_i, bb_r, bb_i, p["ssm_c_re"].reshape(N_GROUPS, GROUP_W, N_STATE),
        p["ssm_c_im"].reshape(N_GROUPS, GROUP_W, N_STATE), p["ssm_d"])

    xn, qn, kn, vv, u, q_raw, k_raw = _inproj_fwd(x, g1, wi, gq, gk, ones64)
    if fwd_side is None:
        attn, lse = _attn_fwd(qn, kn, vv)
    else:
        attn, lse, *rest = _attn_fwd(qn, kn, vv, *fwd_side)
    wg, wo, wu, wd = rest
    y, states = _s5_fwd(u, a_cat, b_mat, c_mat, d_mat)
    (dx2, hdn, dup, h, dyb, mix, z, d_attn, dy_ssm, dx2b, dgp, dg2, loss, dga, dgs, dglu_b) = _mixer_mlp(
        attn, y, x, target, wg, glu_b, ga, gs, wo, g2, wu, wd)
    big = {"w_mlp_up": _grad_matmul(h, dup, "grad_w_mlp_up"), "w_mlp_down": _grad_matmul(hdn, dyb, "grad_w_mlp_down"),
           "w_out": _grad_matmul(mix, dx2b, "grad_w_out"), "glu_w": _grad_matmul(z, dgp, "grad_glu_w")}
    rode = []
    if bwd_side is None:
        dqs, dkn, dvv = _attn_bwd(qn, kn, vv, attn, d_attn, lse, ones_hp)
    else:
        dqs, dkn, dvv, *rode = _attn_bwd(qn, kn, vv, attn, d_attn, lse, ones_hp, *bwd_side(big))
    du, db, dc, da, dd = _s5_bwd(u, dy_ssm, states, a_cat, b_mat, c_mat, d_mat)
    grad_x, dproj, dg1, dgq, dgk = _inproj_bwd(dqs, dkn, dvv, du, q_raw, k_raw, x, dx2, wi, g1, gq, gk, ones64)

    big["w_in"] = _grad_matmul(xn, dproj, "grad_w_in")
    dab_r, dab_i, dbb_r, dbb_i, dc_re, dc_im, dd_g = _s5_unpack_grads(db, dc, da, dd)
    cot = {"norm1_g": dg1, "q_norm_g": dgq, "k_norm_g": dgk, "ab_r": dab_r, "ab_i": dab_i, "bb_r": dbb_r, "bb_i": dbb_i,
           "ssm_c_re": dc_re, "ssm_c_im": dc_im, "ssm_d": dd_g, "glu_b": dglu_b, "attn_out_norm_g": dga,
           "ssm_out_norm_g": dgs, "norm2_g": dg2}
    return loss[0, 0], grad_x, big, cot, (lr, li, ldt, br, bi), rode


COT_NAMES = ("norm1_g", "q_norm_g", "k_norm_g", "ab_r", "ab_i", "bb_r", "bb_i", "ssm_c_re", "ssm_c_im", "ssm_d",
             "glu_b", "attn_out_norm_g", "ssm_out_norm_g", "norm2_g")
SMALL_NAMES = ("norm1_g", "q_norm_g", "k_norm_g", "ssm_a_re", "ssm_a_im", "ssm_log_dt", "ssm_b_re", "ssm_b_im",
               "ssm_c_re", "ssm_c_im", "ssm_d", "glu_b", "attn_out_norm_g", "ssm_out_norm_g", "norm2_g")
BIG_NAMES = ("w_in", "glu_w", "w_out", "w_mlp_up", "w_mlp_down")
PACK_ROWS = 1152


def _pack(arrs):
    flat = jnp.concatenate([a.reshape(-1) for a in arrs])
    return jnp.pad(flat, (0, PACK_ROWS * LANES - flat.shape[0])).reshape(PACK_ROWS, LANES)


def _unpack(packed, like):
    flat = packed.reshape(-1)
    out, pos = [], 0
    for a in like:
        out.append(flat[pos:pos + a.size].reshape(a.shape))
        pos += a.size
    return out


def _small_grads(cot, disc_in, p):
    lr, li, ldt, br, bi = disc_in
    group_sum = (lax.broadcasted_iota(jnp.int32, (N_GROUPS, N_GROUPS * N_STATE), 1) // N_STATE
                 == lax.broadcasted_iota(jnp.int32, (N_GROUPS, N_GROUPS * N_STATE), 0)).astype(F32)
    dlr, dli, dldt, dbr, dbi = _disc_bwd(lr, li, ldt, br, bi, cot["ab_r"], cot["ab_i"], cot["bb_r"], cot["bb_i"], group_sum)
    g = dict(cot)
    g.update(ssm_a_re=dlr, ssm_a_im=dli, ssm_log_dt=dldt[:, 0], ssm_b_re=dbr, ssm_b_im=dbi)
    return {n: g[n].reshape(p[n].shape) for n in SMALL_NAMES}


BIG = {
    "w_in": ((D_MODEL, PROJ_W), 1, PROJ_W // 4, 0, D_MODEL // 2),
    "glu_w": ((SSM_W, SSM_W), 0, SSM_W // 4, 1, SSM_W // 2),
    "w_out": ((D_MODEL, D_MODEL), 0, D_MODEL // 4, 1, D_MODEL // 2),
    "w_mlp_up": ((D_MODEL, D_FF), 1, D_FF // 4, 0, D_MODEL // 2),
    "w_mlp_down": ((D_FF, D_MODEL), 0, D_FF // 4, 1, D_MODEL // 2),
}
N_BIG = len(BIG_NAMES)
N_CHIPS = 4
ANY = pl.BlockSpec(memory_space=pl.ANY)


def _cut(name, shard=False, half=False):
    shape, s_ax, s_sz, h_ax, h_sz = BIG[name]
    shape = list(shape)
    if shard:
        shape[s_ax] = s_sz
    if half:
        shape[h_ax] = h_sz
    return tuple(shape)


def _window(name, base, shard=None, half=None):
    _, s_ax, s_sz, h_ax, h_sz = BIG[name]
    idx = [pl.ds(0, base[0]), pl.ds(0, base[1])]
    if shard is not None:
        idx[s_ax] = pl.ds(pl.multiple_of(shard * s_sz, s_sz), s_sz)
    if half is not None:
        idx[h_ax] = pl.ds(pl.multiple_of(half * h_sz, h_sz), h_sz)
    return tuple(idx)


def _mesh_pos():
    return lax.axis_index("x"), lax.axis_index("y"), lax.axis_index("c")


def _other_chips(x, y):
    return [(1 - x, y, 2 * (1 - x) + y), (x, 1 - y, 2 * x + 1 - y), (1 - x, 1 - y, 2 * (1 - x) + 1 - y)]


def _remote(src, dst, send_sem, recv_sem, dev):
    return pltpu.make_async_remote_copy(src_ref=src, dst_ref=dst, send_sem=send_sem, recv_sem=recv_sem,
                                        device_id=dev, device_id_type=MESH)


def _start_remote(src, dst, send_sem, recv_sem, dev):
    cp = _remote(src, dst, send_sem, recv_sem, dev)
    cp.start()
    return cp


class _Gather:
    def __init__(self, names):
        self.names = tuple(names)
        self.n = len(self.names)

    def in_specs(self):
        return [pl.BlockSpec(memory_space=pltpu.VMEM)] * self.n

    def out_specs(self):
        return [ANY] * self.n

    def out_shapes(self):
        return [jax.ShapeDtypeStruct(BIG[w][0], BF16) for w in self.names]

    def scratch_shapes(self):
        n_sem = (N_CHIPS - 1) * self.n
        return ([pltpu.VMEM(_cut(w, shard=True), BF16) for w in self.names]
                + [pltpu.SemaphoreType.DMA((n_sem,))] * 4 + [pltpu.SemaphoreType.DMA((self.n,))])

    def bind(self, ins, outs, scratch):
        self.ins, self.outs = ins, outs
        self.stage = scratch[:self.n]
        self.send, self.recv, self.fsend, self.frecv, self.lsem = scratch[self.n:]

    def _copies(self):
        x, y, c = _mesh_pos()
        me = 2 * x + y
        sib = (x, y, 1 - c)
        local, sends, lands, fwds, flands = [], [], [], [], []
        for w, n in enumerate(self.names):
            local.append(pltpu.make_async_copy(self.stage[w], self.outs[w].at[_window(n, BIG[n][0], shard=me)], self.lsem.at[w]))
        for k, (px, py, pj) in enumerate(_other_chips(x, y)):
            for w, n in enumerate(self.names):
                s = k * self.n + w
                sends.append(_remote(self.stage[w].at[_window(n, _cut(n, shard=True), half=c)],
                                     self.outs[w].at[_window(n, BIG[n][0], shard=me, half=c)],
                                     self.send.at[s], self.recv.at[s], (px, py, c)))
                got = self.outs[w].at[_window(n, BIG[n][0], shard=pj, half=c)]
                lands.append(_remote(got, got, self.send.at[s], self.recv.at[s], (px, py, c)))
                fwds.append(_remote(got, got, self.fsend.at[s], self.frecv.at[s], sib))
                theirs = self.outs[w].at[_window(n, BIG[n][0], shard=pj, half=1 - c)]
                flands.append(_remote(theirs, theirs, self.fsend.at[s], self.frecv.at[s], sib))
        return local, sends, lands, fwds, flands

    def start(self):
        for w in range(self.n):
            self.stage[w][...] = self.ins[w][...].astype(BF16)
        local, sends, _, _, _ = self._copies()
        for cp in local + sends:
            cp.start()

    def forward(self):
        _, _, lands, fwds, _ = self._copies()
        for land, fwd in zip(lands, fwds):
            land.wait_recv()
            fwd.start()

    def finish(self):
        local, sends, _, fwds, flands = self._copies()
        for cp in flands:
            cp.wait_recv()
        for cp in sends + fwds:
            cp.wait_send()
        for cp in local:
            cp.wait()


def _gather_weights(shards, names):
    g = _Gather(names)

    def body(*refs):
        g.bind(refs[0:g.n], refs[g.n:2 * g.n], refs[2 * g.n:])
        g.start()
        g.forward()
        g.finish()

    return pl.pallas_call(
        body, name="gather_" + "_".join(names), in_specs=g.in_specs(), out_specs=g.out_specs(), out_shape=g.out_shapes(),
        scratch_shapes=g.scratch_shapes(), compiler_params=pltpu.CompilerParams(vmem_limit_bytes=VMEM_LIMIT),
    )(*[shards[n] for n in names])


def _pair_exchange(grads, names, packed=None):
    n_big = len(names)
    n_all = n_big + (packed is not None)

    def body(*refs):
        ins, got = refs[0:n_all], refs[n_all:2 * n_all]
        send, recv = refs[2 * n_all:]
        x, y, c = _mesh_pos()
        sib = (x, y, 1 - c)
        copies = []
        for w, n in enumerate(names):
            copies.append(_start_remote(ins[w].at[_window(n, BIG[n][0], half=1 - c)], got[w], send.at[w], recv.at[w], sib))
        if packed is not None:
            copies.append(_start_remote(ins[n_big], got[n_big], send.at[n_big], recv.at[n_big], sib))
        for cp in copies:
            cp.wait()

    shapes = [jax.ShapeDtypeStruct(_cut(n, half=True), F32) for n in names]
    args = [grads[n] for n in names]
    if packed is not None:
        shapes.append(jax.ShapeDtypeStruct(packed.shape, F32))
        args.append(packed)
    return pl.pallas_call(
        body, name="grad_pair_exchange_" + "_".join(names), in_specs=[ANY] * n_all, out_specs=[ANY] * n_all, out_shape=shapes,
        scratch_shapes=[pltpu.SemaphoreType.DMA((n_all,)), pltpu.SemaphoreType.DMA((n_all,))],
    )(*args)


def _pair_sum(name, full, got, core):
    _, _, _, h_ax, _ = BIG[name]
    rows, cols = _cut(name, half=True)
    br = _row_block(rows, 512)
    nb = rows // br
    own_map = (lambda i, c: (i + c[0] * nb, 0)) if h_ax == 0 else (lambda i, c: (i, c[0]))

    def body(c_ref, own_ref, got_ref, o_ref):
        o_ref[...] = (own_ref[...] + got_ref[...]).astype(BF16)

    plain = pl.BlockSpec((br, cols), lambda i, c: (i, 0))
    return pl.pallas_call(
        body, name="pair_sum_" + name,
        grid_spec=pltpu.PrefetchScalarGridSpec(num_scalar_prefetch=1, grid=(nb,),
                                               in_specs=[pl.BlockSpec((br, cols), own_map), plain], out_specs=plain),
        out_shape=jax.ShapeDtypeStruct((rows, cols), BF16), compiler_params=_params("arbitrary"),
    )(core, full, got)


class _ChipExchange:
    def __init__(self, names, packed_shape=None):
        self.names = tuple(names)
        self.packed_shape = packed_shape
        self.n = len(self.names) + (packed_shape is not None)

    def in_specs(self):
        return [ANY] * self.n

    def out_specs(self):
        return [ANY] * self.n

    def out_shapes(self):
        shapes = [jax.ShapeDtypeStruct((N_CHIPS,) + _cut(w, shard=True, half=True), BF16) for w in self.names]
        if self.packed_shape is not None:
            shapes.append(jax.ShapeDtypeStruct((N_CHIPS,) + tuple(self.packed_shape), F32))
        return shapes

    def scratch_shapes(self):
        n_sem = (N_CHIPS - 1) * self.n
        return [pltpu.SemaphoreType.DMA((n_sem,)), pltpu.SemaphoreType.DMA((n_sem,))]

    def bind(self, ins, outs, scratch):
        self.ins, self.outs = ins, outs
        self.send, self.recv = scratch

    def _piece(self, w, shard):
        if w >= len(self.names):
            return self.ins[w]
        n = self.names[w]
        return self.ins[w].at[_window(n, _cut(n, half=True), shard=shard)]

    def _copies(self):
        x, y, c = _mesh_pos()
        me = 2 * x + y
        sends, lands = [], []
        for k, (px, py, pj) in enumerate(_other_chips(x, y)):
            for w in range(self.n):
                s = k * self.n + w
                sends.append(_remote(self._piece(w, pj), self.outs[w].at[me], self.send.at[s], self.recv.at[s], (px, py, c)))
                lands.append(_remote(self._piece(w, me), self.outs[w].at[pj], self.send.at[s], self.recv.at[s], (px, py, c)))
        return sends, lands

    def start(self):
        for cp in self._copies()[0]:
            cp.start()

    def finish(self):
        sends, lands = self._copies()
        for cp in lands:
            cp.wait_recv()
        for cp in sends:
            cp.wait_send()


def _chip_exchange(halves, packed, names):
    ex = _ChipExchange(names, packed.shape)

    def body(*refs):
        ex.bind(refs[0:ex.n], refs[ex.n:2 * ex.n], refs[2 * ex.n:])
        ex.start()
        ex.finish()

    return pl.pallas_call(
        body, name="grad_chip_exchange", in_specs=ex.in_specs(), out_specs=ex.out_specs(), out_shape=ex.out_shapes(),
        scratch_shapes=ex.scratch_shapes(),
    )(*halves, packed)


def _chip_sum(name, own, slots, chip):
    n_slot, rows, cols = slots.shape
    br = _row_block(rows, 512)
    nb = rows // br
    if name in BIG and BIG[name][1] == 1:
        own_map = lambda i, m: (i, m[0])
    elif name in BIG:
        own_map = lambda i, m: (i + m[0] * nb, 0)
    else:
        own_map = lambda i, m: (i, 0)

    def slot_map(j):
        return lambda i, m: (jnp.where(m[0] == j, (j + 1) % n_slot, j), i, 0)

    def body(m_ref, own_ref, *refs):
        own_blk = own_ref[...].astype(F32)
        tot = None
        for j in range(n_slot):
            term = jnp.where(m_ref[0] == j, own_blk, refs[j][...].astype(F32))
            tot = term if tot is None else tot + term
        refs[n_slot][...] = tot

    in_specs = [pl.BlockSpec((br, cols), own_map)] + [pl.BlockSpec((None, br, cols), slot_map(j)) for j in range(n_slot)]
    return pl.pallas_call(
        body, name="chip_sum_" + name,
        grid_spec=pltpu.PrefetchScalarGridSpec(num_scalar_prefetch=1, grid=(nb,), in_specs=in_specs,
                                               out_specs=pl.BlockSpec((br, cols), lambda i, m: (i, 0))),
        out_shape=jax.ShapeDtypeStruct((rows, cols), F32), compiler_params=_params("arbitrary"),
    )(chip, own, *([slots] * n_slot))


def _half_exchange(pieces):
    def body(*refs):
        ins, outs = refs[0:N_BIG], refs[N_BIG:2 * N_BIG]
        send, recv = refs[2 * N_BIG:]
        x, y, c = _mesh_pos()
        sib = (x, y, 1 - c)
        copies = []
        for w, n in enumerate(BIG_NAMES):
            copies.append(_start_remote(ins[w], outs[w], send.at[w], recv.at[w], sib))
        for cp in copies:
            cp.wait()

    return pl.pallas_call(
        body, name="grad_half_exchange", in_specs=[ANY] * N_BIG, out_specs=[ANY] * N_BIG,
        out_shape=[jax.ShapeDtypeStruct(_cut(n, shard=True, half=True), F32) for n in BIG_NAMES],
        scratch_shapes=[pltpu.SemaphoreType.DMA((N_BIG,)), pltpu.SemaphoreType.DMA((N_BIG,))],
    )(*pieces)


WEIGHT_NAMES = ("norm1_g", "w_in", "q_norm_g", "k_norm_g", "ssm_a_re", "ssm_a_im", "ssm_log_dt", "ssm_b_re", "ssm_b_im",
                "ssm_c_re", "ssm_c_im", "ssm_d", "glu_w", "glu_b", "attn_out_norm_g", "ssm_out_norm_g", "w_out", "norm2_g",
                "w_mlp_up", "w_mlp_down")


def _train_step(a):
    x = a["x"][0]
    target = a["loss_target"][0]
    shards = {n: a[n][0] for n in BIG_NAMES}
    p = {n: a[n][0] for n in SMALL_NAMES}
    core = lax.axis_index("c").astype(jnp.int32).reshape(1)
    chip_id = (2 * lax.axis_index("x") + lax.axis_index("y")).astype(jnp.int32).reshape(1)

    later = ("glu_w", "w_out", "w_mlp_up", "w_mlp_down")
    early = ("w_mlp_up", "w_mlp_down", "w_out", "glu_w")
    late = ("w_in",)
    (wi,) = _gather_weights(shards, ("w_in",))
    chip = {}

    def bwd_side(grads):
        got = _pair_exchange(grads, early)
        for n, g in zip(early, got):
            chip[n] = _pair_sum(n, grads[n], g, core)
        return _ChipExchange(early), [chip[n] for n in early]

    loss, grad_x, big, cot, disc_in, early_slots = _local_step(
        x, target, wi, None, p, fwd_side=(_Gather(later), [shards[n] for n in later]), bwd_side=bwd_side)
    slots = dict(zip(early, early_slots))

    cot_list = [cot[n] for n in COT_NAMES]
    packed = _pack(cot_list)
    *got, got_packed = _pair_exchange(big, late, packed)
    for n, g in zip(late, got):
        chip[n] = _pair_sum(n, big[n], g, core)
    chip_packed = _sum_arrays([packed, got_packed], "pair_sum_small")
    *late_slots, small_slots = _chip_exchange([chip[n] for n in late], chip_packed, late)
    slots.update(zip(late, late_slots))
    pieces = [_chip_sum(n, chip[n], slots[n], chip_id) for n in BIG_NAMES]
    small_sum = _chip_sum("small", chip_packed, small_slots, chip_id)
    shard_grads = {}
    for n, mine, theirs in zip(BIG_NAMES, pieces, _half_exchange(pieces)):
        h_ax = BIG[n][3]
        shard_grads[n] = jnp.where(core[0] == 0, jnp.concatenate([mine, theirs], axis=h_ax),
                                   jnp.concatenate([theirs, mine], axis=h_ax))
    small_grads = _small_grads(dict(zip(COT_NAMES, _unpack(small_sum, cot_list))), disc_in, p)

    grads, delta, new_m, new_v = {}, {}, {}, {}
    for n in BIG_NAMES:
        grads[n] = shard_grads[n]
        delta[n], new_m[n], new_v[n] = _adamw(a[n][0], grads[n], a["m_" + n][0], a["v_" + n][0], "adamw_" + n)
    swapped = ("ssm_b_re", "ssm_b_im", "ssm_d")

    def flat2(n, t):
        t = jnp.swapaxes(t, -1, -2) if n in swapped else t
        return t.reshape(-1, t.shape[-1])

    def unflat(n, t2):
        shape = p[n].shape
        if n in swapped:
            return jnp.swapaxes(t2.reshape(shape[:-2] + (shape[-1], shape[-2])), -1, -2)
        return t2.reshape(shape)

    res = _adamw_many([flat2(n, p[n]) for n in SMALL_NAMES], [flat2(n, small_grads[n]) for n in SMALL_NAMES],
                      [flat2(n, a["m_" + n][0]) for n in SMALL_NAMES], [flat2(n, a["v_" + n][0]) for n in SMALL_NAMES])
    for store, outs in zip((delta, new_m, new_v), res):
        store.update((n, unflat(n, t2)) for n, t2 in zip(SMALL_NAMES, outs))
    grads.update(small_grads)

    total = lax.psum(loss, ("x", "y", "c"))
    out = [total, grad_x[None]]
    for store in (grads, delta, new_m, new_v):
        out += [store[n].reshape(a[n].shape) for n in WEIGHT_NAMES]
    return tuple(out)


def kernel(x, norm1_g, w_in, q_norm_g, k_norm_g, ssm_a_re, ssm_a_im, ssm_log_dt, ssm_b_re, ssm_b_im, ssm_c_re, ssm_c_im, ssm_d, glu_w, glu_b, attn_out_norm_g, ssm_out_norm_g, w_out, norm2_g, w_mlp_up, w_mlp_down, loss_target, m_norm1_g, m_w_in, m_q_norm_g, m_k_norm_g, m_ssm_a_re, m_ssm_a_im, m_ssm_log_dt, m_ssm_b_re, m_ssm_b_im, m_ssm_c_re, m_ssm_c_im, m_ssm_d, m_glu_w, m_glu_b, m_attn_out_norm_g, m_ssm_out_norm_g, m_w_out, m_norm2_g, m_w_mlp_up, m_w_mlp_down, v_norm1_g, v_w_in, v_q_norm_g, v_k_norm_g, v_ssm_a_re, v_ssm_a_im, v_ssm_log_dt, v_ssm_b_re, v_ssm_b_im, v_ssm_c_re, v_ssm_c_im, v_ssm_d, v_glu_w, v_glu_b, v_attn_out_norm_g, v_ssm_out_norm_g, v_w_out, v_norm2_g, v_w_mlp_up, v_w_mlp_down):
    return _train_step(dict(locals()))
```

```python
import functools
import math

import jax
import jax.numpy as jnp
from jax import lax
from jax.experimental import pallas as pl
from jax.experimental.pallas import tpu as pltpu

F32 = jnp.float32
BF16 = jnp.bfloat16
MESH = pl.DeviceIdType.MESH

D_MODEL = 1024
ATTN_W = 512
SSM_W = 512
HEAD = 64
D_FF = 4096
PROJ_W = 2048
N_GROUPS = 32
N_STATE = 64
GROUP_W = 16
EPS = 1e-6
NEG = -1e30
DILATIONS = (1, 4, 16)
BLK = 128
TILE = 2048
LANES = 128
MXU_WIDTH = 256
N_LB = SSM_W // LANES
VMEM_LIMIT = 56 * 1024 * 1024

ADAM_LR, ADAM_B1, ADAM_B2, ADAM_EPS, ADAM_WD, ADAM_STEP = 0.001, 0.9, 0.999, 1e-08, 0.01, 10


def _params(*sem):
    return pltpu.CompilerParams(dimension_semantics=sem, vmem_limit_bytes=VMEM_LIMIT)


def _nt(a, b):
    return lax.dot_general(a, b, (((1,), (1,)), ((), ())), preferred_element_type=F32)


def _tn(a, b):
    return lax.dot_general(a, b, (((0,), (0,)), ((), ())), preferred_element_type=F32)


def _mm(a, b):
    return jnp.dot(a, b, preferred_element_type=F32)


def _group_mean(t, ones_bd, width):
    span = ones_bd.shape[0]
    hi = t.astype(BF16)
    lo = (t - hi.astype(F32)).astype(BF16)
    parts = [_mm(hi[:, k:k + span], ones_bd) + _mm(lo[:, k:k + span], ones_bd) for k in range(0, t.shape[1], span)]
    return (parts[0] if len(parts) == 1 else jnp.concatenate(parts, axis=1)) * (1.0 / width)


def _rms(x):
    return lax.rsqrt(jnp.mean(x * x, axis=-1, keepdims=True) + EPS)


def _rms_bwd(dy, x, r, g):
    xh = x * r
    dxh = dy * g
    dx = r * (dxh - xh * jnp.mean(dxh * xh, axis=-1, keepdims=True))
    return dx, dy * xh


def _colsum(x):
    return jnp.sum(x, axis=0, keepdims=True)


def _row_block(rows, cap):
    for b in range(min(rows, cap) // 8 * 8, 0, -8):
        if rows % b == 0:
            return b
    raise ValueError(f"no row block for {rows} rows")


def _inproj_fwd(x, g1, wi, gq, gk, ones64):
    t_len = x.shape[0]
    tm = 512
    n_hp = ATTN_W // LANES

    def body(x_ref, g1_ref, wi_ref, gq_ref, gk_ref, bd_ref, xn_ref, q_ref, k_ref, v_ref, u_ref, qr_ref, kr_ref):
        xv = x_ref[...]
        xn = (xv * _rms(xv) * g1_ref[...]).astype(BF16)
        xn_ref[...] = xn
        proj = _mm(xn, wi_ref[...])
        q = proj[:, 0:ATTN_W]
        k = proj[:, ATTN_W:2 * ATTN_W]
        v = proj[:, 2 * ATTN_W:3 * ATTN_W]
        u_ref[...] = proj[:, 3 * ATTN_W:]
        qr_ref[...] = q
        kr_ref[...] = k
        bd = bd_ref[...]
        qn = q * lax.rsqrt(_group_mean(q * q, bd, HEAD) + EPS) * gq_ref[...] * (HEAD ** -0.5)
        kn = k * lax.rsqrt(_group_mean(k * k, bd, HEAD) + EPS) * gk_ref[...]
        for hp in range(n_hp):
            sl = slice(hp * LANES, (hp + 1) * LANES)
            q_ref[hp] = qn[:, sl]
            k_ref[hp] = kn[:, sl]
            v_ref[hp] = v[:, sl]

    row = lambda i: (i, 0)
    const = lambda i: (0, 0)
    hp_spec = pl.BlockSpec((n_hp, tm, LANES), lambda i: (0, i, 0))
    hp_shape = jax.ShapeDtypeStruct((n_hp, t_len, LANES), F32)
    return pl.pallas_call(
        body, name="inproj_fwd", grid=(t_len // tm,),
        in_specs=[pl.BlockSpec((tm, D_MODEL), row), pl.BlockSpec((1, D_MODEL), const),
                  pl.BlockSpec((D_MODEL, PROJ_W), const), pl.BlockSpec((1, ATTN_W), const),
                  pl.BlockSpec((1, ATTN_W), const), pl.BlockSpec(ones64.shape, const)],
        out_specs=[pl.BlockSpec((tm, D_MODEL), row), hp_spec, hp_spec, hp_spec,
                   pl.BlockSpec((tm, SSM_W), row), pl.BlockSpec((tm, ATTN_W), row), pl.BlockSpec((tm, ATTN_W), row)],
        out_shape=[jax.ShapeDtypeStruct((t_len, D_MODEL), BF16), hp_shape, hp_shape, hp_shape,
                   jax.ShapeDtypeStruct((t_len, SSM_W), F32), jax.ShapeDtypeStruct((t_len, ATTN_W), F32),
                   jax.ShapeDtypeStruct((t_len, ATTN_W), F32)],
        compiler_params=_params("arbitrary"),
    )(x, g1, wi, gq, gk, ones64)


def _attn_masks():
    head0 = lax.broadcasted_iota(jnp.int32, (BLK, LANES), 1) < HEAD
    row = lax.broadcasted_iota(jnp.int32, (2 * BLK, 2 * BLK), 0) & (BLK - 1)
    col = lax.broadcasted_iota(jnp.int32, (2 * BLK, 2 * BLK), 1)
    return head0, (col < BLK) & (col >= row), (col >= BLK) & (col - BLK <= row)


def _stack_heads(x, head0):
    return jnp.concatenate([jnp.where(head0, x, 0.0), jnp.where(head0, 0.0, x)], axis=0).astype(BF16)


def _unit_rows(uidx, d):
    nb = TILE // (BLK * d)
    r = lax.div(uidx, nb)
    b = lax.rem(uidx, nb)
    start = r + d * BLK * b
    if d == 1:
        start = pl.multiple_of(start, BLK)
        mk = lambda s: pl.ds(pl.multiple_of(s, BLK), BLK)
    else:
        mk = lambda s: pl.ds(s, BLK, stride=d)
    return b, mk(start), mk(TILE + start), mk(TILE + start - d * BLK)


def _attn_fwd(q, k, v, side=None, side_args=()):
    n_hp, t_len, _ = q.shape
    nt = t_len // TILE
    ns = side.n if side is not None else 0
    n_steps = n_hp * nt

    def body(*refs):
        q_ref, kp_ref, kc_ref, vp_ref, vc_ref = refs[0:5]
        o_ref, lse_ref = refs[5 + ns:7 + ns]
        kk, vv, m_s, l_s, acc_s = refs[7 + 2 * ns:12 + 2 * ns]
        t = pl.program_id(1)
        step = pl.program_id(0) * nt + t
        if side is not None:
            side.bind(refs[5:5 + ns], refs[7 + ns:7 + 2 * ns], refs[12 + 2 * ns:])
            pl.when(step == 0)(side.start)
            pl.when(step == n_steps // 2)(side.forward)
        kk[0:TILE] = kp_ref[0]
        kk[TILE:] = kc_ref[0]
        vv[0:TILE] = vp_ref[0]
        vv[TILE:] = vc_ref[0]
        head0, band_prev, band_cur = _attn_masks()

        for pi, d in enumerate(DILATIONS):
            def unit(uidx, carry, d=d, pi=pi):
                b, rows_q, rows_c, rows_p = _unit_rows(uidx, d)
                mask = band_cur | (band_prev & ((t > 0) | (b > 0)))
                q2 = _stack_heads(q_ref.at[0][rows_q, :], head0)
                kcat = jnp.concatenate([kk[rows_p, :], kk[rows_c, :]], axis=0).astype(BF16)
                vcat = jnp.concatenate([vv[rows_p, :], vv[rows_c, :]], axis=0).astype(BF16)
                s = jnp.where(mask, _nt(q2, kcat), NEG)
                m = jnp.max(s, axis=1, keepdims=True)
                p = jnp.exp(s - m)
                ls = jnp.sum(p, axis=1, keepdims=True)
                pv = _mm(p.astype(BF16), vcat)
                m_s.at[pi][rows_q, :] = jnp.where(head0, m[0:BLK], m[BLK:])
                l_s.at[pi][rows_q, :] = jnp.where(head0, ls[0:BLK], ls[BLK:])
                acc_s.at[pi][rows_q, :] = jnp.where(head0, pv[0:BLK], pv[BLK:])
                return carry

            lax.fori_loop(0, TILE // BLK, unit, 0, unroll=16)

        m_all = jnp.maximum(jnp.maximum(m_s[0], m_s[1]), m_s[2])
        num = jnp.zeros((TILE, LANES), F32)
        den = jnp.zeros((TILE, LANES), F32)
        for pi in range(len(DILATIONS)):
            wgt = jnp.exp(m_s[pi] - m_all)
            num = num + acc_s[pi] * wgt
            den = den + l_s[pi] * wgt
        o_ref[...] = num / den
        lse_ref[0] = m_all + jnp.log(den)
        if side is not None:
            pl.when(step == n_steps - 1)(side.finish)

    cur = lambda hp, t: (hp, t, 0)
    prev = lambda hp, t: (hp, jnp.maximum(t - 1, 0), 0)
    blk = (1, TILE, LANES)
    per_pattern = pltpu.VMEM((len(DILATIONS), TILE, LANES), F32)
    extra = (side.in_specs(), side.out_specs(), side.out_shapes(), side.scratch_shapes()) if side is not None else ([], [], [], [])
    return pl.pallas_call(
        body, name="attn_fwd", grid=(n_hp, nt),
        in_specs=[pl.BlockSpec(blk, cur), pl.BlockSpec(blk, prev), pl.BlockSpec(blk, cur),
                  pl.BlockSpec(blk, prev), pl.BlockSpec(blk, cur)] + extra[0],
        out_specs=[pl.BlockSpec((TILE, LANES), lambda hp, t: (t, hp)), pl.BlockSpec(blk, cur)] + extra[1],
        out_shape=[jax.ShapeDtypeStruct((t_len, ATTN_W), F32), jax.ShapeDtypeStruct((n_hp, t_len, LANES), F32)] + extra[2],
        scratch_shapes=[pltpu.VMEM((2 * TILE, LANES), F32), pltpu.VMEM((2 * TILE, LANES), F32),
                        per_pattern, per_pattern, per_pattern] + extra[3],
        compiler_params=_params("arbitrary", "arbitrary"),
    )(q, k, k, v, v, *side_args)


def _attn_bwd(q, k, v, o, do, lse, ones_hp, side=None, side_args=()):
    n_hp, t_len, _ = q.shape
    nt = t_len // TILE
    ns = side.n if side is not None else 0
    n_pat = len(DILATIONS)

    def body(*refs):
        q_ref, kp_ref, kc_ref, vp_ref, vc_ref, o_ref, do_ref, lse_ref, bd_ref = refs[0:9]
        dq_ref, dk_ref, dv_ref = refs[9 + ns:12 + ns]
        kk, vv, dq_s, dkc, dkp, dvc, dvp, hold_k, hold_v, dl_s = refs[12 + 2 * ns:22 + 2 * ns]
        t = pl.program_id(1)
        if side is not None:
            side.bind(refs[9:9 + ns], refs[12 + ns:12 + 2 * ns], refs[22 + 2 * ns:])
            pl.when((pl.program_id(0) == 0) & (t == 0))(side.start)

        @pl.when(t < nt)
        def _():
            kk[0:TILE] = kp_ref[0]
            kk[TILE:] = kc_ref[0]
            vv[0:TILE] = vp_ref[0]
            vv[TILE:] = vc_ref[0]
            dl_s[...] = _group_mean(do_ref[...] * o_ref[...], bd_ref[...], 1.0)
            head0, band_prev, band_cur = _attn_masks()

            for pi, d in enumerate(DILATIONS):
                def unit(uidx, carry, d=d, pi=pi):
                    b, rows_q, rows_c, rows_p = _unit_rows(uidx, d)
                    mask = band_cur | (band_prev & ((t > 0) | (b > 0)))
                    q2 = _stack_heads(q_ref.at[0][rows_q, :], head0)
                    do2 = _stack_heads(do_ref[rows_q, :], head0)
                    lse_f = lse_ref.at[0][rows_q, :]
                    dl_f = dl_s[rows_q, :]
                    lse2 = jnp.concatenate([lse_f[:, 0:1], lse_f[:, HEAD:HEAD + 1]], axis=0)
                    dl2 = jnp.concatenate([dl_f[:, 0:1], dl_f[:, HEAD:HEAD + 1]], axis=0)
                    kcat = jnp.concatenate([kk[rows_p, :], kk[rows_c, :]], axis=0).astype(BF16)
                    vcat = jnp.concatenate([vv[rows_p, :], vv[rows_c, :]], axis=0).astype(BF16)
                    p = jnp.where(mask, jnp.exp(_nt(q2, kcat) - lse2), 0.0)
                    ds = (p * (_nt(do2, vcat) - dl2)).astype(BF16)
                    dq2 = _mm(ds, kcat)
                    dq_s.at[pi][rows_q, :] = jnp.where(head0, dq2[0:BLK], dq2[BLK:])
                    dk2 = _tn(ds, q2)
                    dv2 = _tn(p.astype(BF16), do2)
                    dkp.at[pi][rows_q, :] = dk2[0:BLK]
                    dkc.at[pi][rows_q, :] = dk2[BLK:]
                    dvp.at[pi][rows_q, :] = dv2[0:BLK]
                    dvc.at[pi][rows_q, :] = dv2[BLK:]
                    return carry

                lax.fori_loop(0, TILE // BLK, unit, 0, unroll=16)

            dq_ref[...] = dq_s[0] + dq_s[1] + dq_s[2]

        @pl.when(t > 0)
        def _():
            dk_ref[...] = hold_k[...]
            dv_ref[...] = hold_v[...]

        @pl.when((t > 0) & (t < nt))
        def _():
            for pi, d in enumerate(DILATIONS):
                back = d * BLK
                dk_ref[TILE - back:, :] = dk_ref[TILE - back:, :] + dkp[pi, 0:back, :]
                dv_ref[TILE - back:, :] = dv_ref[TILE - back:, :] + dvp[pi, 0:back, :]

        @pl.when(t < nt)
        def _():
            hold_k[...] = dkc[0] + dkc[1] + dkc[2]
            hold_v[...] = dvc[0] + dvc[1] + dvc[2]
            for pi, d in enumerate(DILATIONS):
                back = d * BLK
                if back < TILE:
                    hold_k[0:TILE - back, :] = hold_k[0:TILE - back, :] + dkp[pi, back:, :]
                    hold_v[0:TILE - back, :] = hold_v[0:TILE - back, :] + dvp[pi, back:, :]

        if side is not None:
            pl.when((pl.program_id(0) == n_hp - 1) & (t == nt))(side.finish)

    last = nt - 1
    extra = (side.in_specs(), side.out_specs(), side.out_shapes(), side.scratch_shapes()) if side is not None else ([], [], [], [])
    cur = lambda hp, t: (hp, jnp.minimum(t, last), 0)
    prev = lambda hp, t: (hp, jnp.clip(t - 1, 0, last), 0)
    cur2 = lambda hp, t: (jnp.minimum(t, last), hp)
    prev2 = lambda hp, t: (jnp.maximum(t - 1, 0), hp)
    blk = (1, TILE, LANES)
    blk2 = (TILE, LANES)
    out = jax.ShapeDtypeStruct((t_len, ATTN_W), F32)
    return pl.pallas_call(
        body, name="attn_bwd", grid=(n_hp, nt + 1),
        in_specs=[pl.BlockSpec(blk, cur), pl.BlockSpec(blk, prev), pl.BlockSpec(blk, cur),
                  pl.BlockSpec(blk, prev), pl.BlockSpec(blk, cur), pl.BlockSpec(blk2, cur2),
                  pl.BlockSpec(blk2, cur2), pl.BlockSpec(blk, cur), pl.BlockSpec((LANES, LANES), lambda hp, t: (0, 0))]
        + extra[0],
        out_specs=[pl.BlockSpec(blk2, cur2), pl.BlockSpec(blk2, prev2), pl.BlockSpec(blk2, prev2)] + extra[1],
        out_shape=[out, out, out] + extra[2],
        scratch_shapes=[pltpu.VMEM((2 * TILE, LANES), F32), pltpu.VMEM((2 * TILE, LANES), F32)]
        + [pltpu.VMEM((n_pat, TILE, LANES), F32)] * 5 + [pltpu.VMEM((TILE, LANES), F32)] * 3 + extra[3],
        compiler_params=_params("arbitrary", "arbitrary"),
    )(q, k, k, v, v, o, do, lse, ones_hp, *side_args)


def _discretise(lr, li, ldt, br, bi):
    dt = jnp.exp(ldt)
    mag = jnp.exp(lr * dt)
    ab_r, ab_i = mag * jnp.cos(li * dt), mag * jnp.sin(li * dt)
    den = lr * lr + li * li
    nr, ni = ab_r - 1.0, ab_i
    cr = (nr * lr + ni * li) / den
    ci = (ni * lr - nr * li) / den
    return ab_r, ab_i, cr * br - ci * bi, cr * bi + ci * br


def _disc_fwd(lr, li, ldt, br, bi):
    def body(lr_ref, li_ref, ldt_ref, br_ref, bi_ref, ar_o, ai_o, bbr_o, bbi_o):
        outs = _discretise(lr_ref[...], li_ref[...], ldt_ref[...], br_ref[...], bi_ref[...])
        for o_ref, val in zip((ar_o, ai_o, bbr_o, bbi_o), outs):
            o_ref[...] = val

    col = jax.ShapeDtypeStruct(lr.shape, F32)
    mat = jax.ShapeDtypeStruct(br.shape, F32)
    return pl.pallas_call(body, name="s5_disc_fwd", out_shape=[col, col, mat, mat])(lr, li, ldt, br, bi)


def _disc_bwd(lr, li, ldt, br, bi, d_ar, d_ai, d_bbr, d_bbi, group_sum):
    def body(lr_ref, li_ref, ldt_ref, br_ref, bi_ref, c1, c2, c3, c4, gs_ref, dlr_o, dli_o, dldt_o, dbr_o, dbi_o):
        _, vjp = jax.vjp(_discretise, lr_ref[...], li_ref[...], ldt_ref[...], br_ref[...], bi_ref[...])
        dlr, dli, dldt, dbr, dbi = vjp((c1[...], c2[...], c3[...], c4[...]))
        dlr_o[...] = dlr
        dli_o[...] = dli
        dbr_o[...] = dbr
        dbi_o[...] = dbi
        wide = jnp.broadcast_to(dldt, (dldt.shape[0], LANES))
        dldt_o[...] = jnp.dot(gs_ref[...], wide, precision=lax.Precision.HIGHEST, preferred_element_type=F32)

    col = jax.ShapeDtypeStruct(lr.shape, F32)
    mat = jax.ShapeDtypeStruct(br.shape, F32)
    return pl.pallas_call(
        body, name="s5_disc_bwd", out_shape=[col, col, jax.ShapeDtypeStruct((N_GROUPS, LANES), F32), mat, mat],
    )(lr, li, ldt, br, bi, d_ar, d_ai, d_bbr, d_bbi, group_sum)


N_CHUNK = TILE // BLK
HALF = 4


def _cmul(ar, ai, xr, xi):
    return ar * xr - ai * xi, ar * xi + ai * xr


def _power_table(a_ref, tab, sign, reverse):
    ar = [a_ref[0, j:j + 1, :] for j in range(HALF)]
    ai = [sign * a_ref[0, HALF + j:HALF + j + 1, :] for j in range(HALF)]

    def step(s, cur):
        row = pl.ds((BLK - 1 - s) if reverse else s, 1)
        nxt = []
        for j in range(HALF):
            tab.at[j][row, :] = cur[j]
            tab.at[HALF + j][row, :] = cur[HALF + j]
            nxt.append(_cmul(ar[j], ai[j], cur[j], cur[HALF + j]))
        return tuple(p[0] for p in nxt) + tuple(p[1] for p in nxt)

    lax.fori_loop(0, BLK, step, tuple(ar) + tuple(ai))


def _interleave(src, dst):
    for c in range(N_CHUNK):
        dst[pl.ds(c, BLK, stride=N_CHUNK), :] = src[c * BLK:(c + 1) * BLK, :]


def _deinterleave(src, dst):
    for c in range(N_CHUNK):
        dst[c * BLK:(c + 1) * BLK, :] = src[pl.ds(c, BLK, stride=N_CHUNK), :]


def _step_rows(s):
    return pl.ds(pl.multiple_of(s * N_CHUNK, N_CHUNK), N_CHUNK)


def _chunk_scan(buf, a_ref, sign, reverse):
    ar = [jnp.broadcast_to(a_ref[0, j:j + 1, :], (N_CHUNK, LANES)) for j in range(HALF)]
    ai = [sign * jnp.broadcast_to(a_ref[0, HALF + j:HALF + j + 1, :], (N_CHUNK, LANES)) for j in range(HALF)]

    def step(i, carry):
        s = (BLK - 1 - i) if reverse else i
        rows = _step_rows(s)
        out = []
        for j in range(HALF):
            pr, pi = _cmul(ar[j], ai[j], carry[j], carry[HALF + j])
            xr = buf.at[j][rows, :] + pr
            xi = buf.at[HALF + j][rows, :] + pi
            buf.at[j][rows, :] = xr
            buf.at[HALF + j][rows, :] = xi
            out.append((xr, xi))
        return tuple(p[0] for p in out) + tuple(p[1] for p in out)

    zero = jnp.zeros((N_CHUNK, LANES), F32)
    lax.fori_loop(0, BLK, step, (zero,) * (2 * HALF), unroll=2)


def _chunk_states(buf, carry_s, xin_s, tab, reverse):
    edge = 0 if reverse else BLK - 1
    top = 0 if reverse else BLK - 1
    pw = [tab[j, top:top + 1, :] for j in range(2 * HALF)]
    cur = [carry_s[j:j + 1, :] for j in range(2 * HALF)]
    summary = [buf[j, edge * N_CHUNK:(edge + 1) * N_CHUNK, :] for j in range(2 * HALF)]
    order = range(N_CHUNK - 1, -1, -1) if reverse else range(N_CHUNK)
    for c in order:
        for j in range(2 * HALF):
            xin_s[j, c:c + 1, :] = cur[j]
        nxt = []
        for j in range(HALF):
            pr, pi = _cmul(pw[j], pw[HALF + j], cur[j], cur[HALF + j])
            nxt.append((pr + summary[j][c:c + 1, :], pi + summary[HALF + j][c:c + 1, :]))
        cur = [p[0] for p in nxt] + [p[1] for p in nxt]
    for j in range(2 * HALF):
        carry_s[j:j + 1, :] = cur[j]


def _s5_fwd(u, a_cat, b_mat, c_mat, d_skip):
    t_len = u.shape[0]
    nt = t_len // TILE

    def body(u_ref, a_ref, b_ref, c_ref, d_ref, y_ref, x_ref, xs, us, tab, carry_s, xin_s):
        sb = pl.program_id(1)

        @pl.when(sb == 0)
        def _():
            _power_table(a_ref, tab, 1.0, False)
            carry_s[...] = jnp.zeros_like(carry_s)

        _interleave(u_ref, us)
        uv = us[...]
        bu = _mm(uv.astype(BF16), b_ref[0])
        for j in range(2 * HALF):
            xs[j] = bu[:, j * LANES:(j + 1) * LANES]
        _chunk_scan(xs, a_ref, 1.0, False)
        _chunk_states(xs, carry_s, xin_s, tab, False)
        xin = [xin_s[j] for j in range(2 * HALF)]

        def fix(s, acc):
            rows = _step_rows(s)
            for j in range(HALF):
                pr, pi = _cmul(tab.at[j][pl.ds(s, 1), :], tab.at[HALF + j][pl.ds(s, 1), :], xin[j], xin[HALF + j])
                xs.at[j][rows, :] = xs.at[j][rows, :] + pr
                xs.at[HALF + j][rows, :] = xs.at[HALF + j][rows, :] + pi
            return acc

        lax.fori_loop(0, BLK, fix, 0, unroll=2)
        xcat = jnp.concatenate([xs[j].astype(BF16) for j in range(2 * HALF)], axis=1)
        x_ref[0] = xcat
        us[...] = d_ref[0] * uv + _mm(xcat, c_ref[0])
        _deinterleave(us, y_ref)

    return pl.pallas_call(
        body, name="s5_fwd", grid=(N_LB, nt),
        in_specs=[pl.BlockSpec((TILE, LANES), lambda lb, sb: (sb, lb)),
                  pl.BlockSpec((1, 2 * HALF, LANES), lambda lb, sb: (lb, 0, 0)),
                  pl.BlockSpec((1, LANES, 2 * HALF * LANES), lambda lb, sb: (lb, 0, 0)),
                  pl.BlockSpec((1, 2 * HALF * LANES, LANES), lambda lb, sb: (lb, 0, 0)),
                  pl.BlockSpec((1, 1, LANES), lambda lb, sb: (lb, 0, 0))],
        out_specs=[pl.BlockSpec((TILE, LANES), lambda lb, sb: (sb, lb)),
                   pl.BlockSpec((1, TILE, 2 * HALF * LANES), lambda lb, sb: (lb, sb, 0))],
        out_shape=[jax.ShapeDtypeStruct((t_len, SSM_W), F32), jax.ShapeDtypeStruct((N_LB, t_len, 2 * HALF * LANES), BF16)],
        scratch_shapes=[pltpu.VMEM((2 * HALF, TILE, LANES), F32), pltpu.VMEM((TILE, LANES), F32),
                        pltpu.VMEM((2 * HALF, BLK, LANES), F32),
                        pltpu.VMEM((2 * HALF, LANES), F32), pltpu.VMEM((2 * HALF, N_CHUNK, LANES), F32)],
        compiler_params=_params("arbitrary", "arbitrary"),
    )(u, a_cat, b_mat, c_mat, d_skip)


def _s5_bwd(u, dy, states, a_cat, b_mat, c_mat, d_skip):
    t_len = u.shape[0]
    nt = t_len // TILE
    last = nt - 1

    def body(u_ref, dy_ref, x_ref, a_ref, b_ref, c_ref, d_ref, du_ref, db_ref, dc_ref, da_ref, dd_ref,
             gs, us, dys, tabc, lam_s, lin_s):
        sb = pl.program_id(1)

        @pl.when(sb == 0)
        def _():
            _power_table(a_ref, tabc, -1.0, True)
            lam_s[...] = jnp.zeros_like(lam_s)
            db_ref[...] = jnp.zeros_like(db_ref)
            dc_ref[...] = jnp.zeros_like(dc_ref)
            da_ref[...] = jnp.zeros_like(da_ref)
            dd_ref[...] = jnp.zeros_like(dd_ref)

        _interleave(u_ref, us)
        _interleave(dy_ref, dys)
        uv = us[...]
        dyv = dys[...]
        ub = uv.astype(BF16)
        dyb = dyv.astype(BF16)
        gy = _nt(dyb, c_ref[0])
        for j in range(2 * HALF):
            gs[j] = gy[:, j * LANES:(j + 1) * LANES]
        _chunk_scan(gs, a_ref, -1.0, True)
        _chunk_states(gs, lam_s, lin_s, tabc, True)
        zero = jnp.zeros((N_CHUNK, LANES), F32)
        x_tile = x_ref.at[0]
        for grp in range(0, HALF, 2):
            slabs = (grp, grp + 1)
            lin = [(lin_s[j], lin_s[HALF + j]) for j in slabs]

            def fix(i, carry, slabs=slabs, lin=lin):
                s = BLK - 1 - i
                rows = _step_rows(s)
                out = []
                for k, j in enumerate(slabs):
                    nr, ni, acc_r, acc_i = carry[4 * k:4 * k + 4]
                    xr = x_tile[rows, pl.ds(j * LANES, LANES)].astype(F32)
                    xi = x_tile[rows, pl.ds((HALF + j) * LANES, LANES)].astype(F32)
                    qr, qi = _cmul(tabc.at[j][pl.ds(s, 1), :], tabc.at[HALF + j][pl.ds(s, 1), :], lin[k][0], lin[k][1])
                    lr_ = gs.at[j][rows, :] + qr
                    li_ = gs.at[HALF + j][rows, :] + qi
                    gs.at[j][rows, :] = lr_
                    gs.at[HALF + j][rows, :] = li_
                    out += [lr_, li_, acc_r + (xr * nr + xi * ni), acc_i + (xr * ni - xi * nr)]
                return tuple(out)

            init = []
            for k in range(len(slabs)):
                init += [lin[k][0], lin[k][1], zero, zero]
            res = lax.fori_loop(0, BLK, fix, tuple(init), unroll=2)
            for k, j in enumerate(slabs):
                da_ref[0, j:j + 1, :] = da_ref[0, j:j + 1, :] + _colsum(res[4 * k + 2])
                da_ref[0, HALF + j:HALF + j + 1, :] = da_ref[0, HALF + j:HALF + j + 1, :] + _colsum(res[4 * k + 3])
        lam = jnp.concatenate([gs[j].astype(BF16) for j in range(2 * HALF)], axis=1)
        us[...] = _nt(lam, b_ref[0]) + d_ref[0] * dyv
        _deinterleave(us, du_ref)
        db_ref[0] = db_ref[0] + _tn(ub, lam)
        dc_ref[0] = dc_ref[0] + _tn(dyb, x_ref[0])
        dd_ref[0] = dd_ref[0] + _colsum(dyv * uv)

    rev = lambda lb, sb: (last - sb, lb)
    per_lb = lambda lb, sb: (lb, 0, 0)
    wide = 2 * HALF * LANES
    return pl.pallas_call(
        body, name="s5_bwd", grid=(N_LB, nt),
        in_specs=[pl.BlockSpec((TILE, LANES), rev), pl.BlockSpec((TILE, LANES), rev),
                  pl.BlockSpec((1, TILE, wide), lambda lb, sb: (lb, last - sb, 0)),
                  pl.BlockSpec((1, 2 * HALF, LANES), per_lb), pl.BlockSpec((1, LANES, wide), per_lb),
                  pl.BlockSpec((1, wide, LANES), per_lb), pl.BlockSpec((1, 1, LANES), per_lb)],
        out_specs=[pl.BlockSpec((TILE, LANES), rev), pl.BlockSpec((1, LANES, wide), per_lb),
                   pl.BlockSpec((1, LANES, wide), per_lb), pl.BlockSpec((1, 2 * HALF, LANES), per_lb),
                   pl.BlockSpec((1, 1, LANES), per_lb)],
        out_shape=[jax.ShapeDtypeStruct((t_len, SSM_W), F32), jax.ShapeDtypeStruct((N_LB, LANES, wide), F32),
                   jax.ShapeDtypeStruct((N_LB, LANES, wide), F32), jax.ShapeDtypeStruct((N_LB, 2 * HALF, LANES), F32),
                   jax.ShapeDtypeStruct((N_LB, 1, LANES), F32)],
        scratch_shapes=[pltpu.VMEM((2 * HALF, TILE, LANES), F32),
                        pltpu.VMEM((TILE, LANES), F32), pltpu.VMEM((TILE, LANES), F32),
                        pltpu.VMEM((2 * HALF, BLK, LANES), F32), pltpu.VMEM((2 * HALF, LANES), F32),
                        pltpu.VMEM((2 * HALF, N_CHUNK, LANES), F32)],
        compiler_params=_params("arbitrary", "arbitrary"),
    )(u, dy, states, a_cat, b_mat, c_mat, d_skip)


_GELU_C = math.sqrt(2.0 / math.pi)
_GELU_K = 0.044715


def _gelu(y):
    t = jnp.tanh(_GELU_C * (y + _GELU_K * (y * y * y)))
    return y * (0.5 * (1.0 + t)), t


def _gelu_grad(y, t):
    return 0.5 * (1.0 + t) + 0.5 * y * (1.0 - t * t) * (_GELU_C * (1.0 + 3.0 * _GELU_K * y * y))


def _glu(y, wg, bias):
    z, t = _gelu(y)
    sg = jax.nn.sigmoid(_mm(z.astype(BF16), wg) + bias)
    return z, t, sg


def _mixer_mlp(attn, y, x, target, wg, glu_b, ga, gs, wo, g2, wu, wd):
    t_len = x.shape[0]
    tm = 256
    fc = 1024
    n_fc = D_FF // fc

    def body(attn_ref, y_ref, x_ref, tg_ref, b_ref, ga_ref, gs_ref, g2_ref, wg_s, wo_s, wu_hbm, wd_hbm,
             dx2_ref, hdn_ref, dup_ref, h_ref, dyb_ref, mix_ref, z_ref, dattn_ref, dy_ref, dx2b_ref, dgp_ref,
             dg2_ref, loss_ref, dga_ref, dgs_ref, db_ref, wu_s, wd_s, relu_s, sem):
        @pl.when(pl.program_id(0) == 0)
        def _():
            copies = [pltpu.make_async_copy(src, dst, sem.at[k]) for k, (src, dst) in enumerate(((wu_hbm, wu_s), (wd_hbm, wd_s)))]
            for cp in copies:
                cp.start()
            for cp in copies:
                cp.wait()
            for acc in (dg2_ref, loss_ref, dga_ref, dgs_ref, db_ref):
                acc[...] = jnp.zeros_like(acc)

        av = attn_ref[...]
        z, _, sg = _glu(y_ref[...], wg_s[...], b_ref[...])
        z_ref[...] = z.astype(BF16)
        s = z * sg
        an = (av * _rms(av) * ga_ref[...]).astype(BF16)
        sn = (s * _rms(s) * gs_ref[...]).astype(BF16)
        mix_ref[:, 0:ATTN_W] = an
        mix_ref[:, ATTN_W:] = sn
        dx2_ref[...] = x_ref[...] + _mm(an, wo_s[0:ATTN_W, :]) + _mm(sn, wo_s[ATTN_W:, :])
        r = _rms(dx2_ref[...])
        g2v = g2_ref[...]
        h = (dx2_ref[...] * r * g2v).astype(BF16)
        h_ref[...] = h
        yout = dx2_ref[...]
        for c in range(n_fc):
            cols = slice(c * fc, (c + 1) * fc)
            ru = jnp.maximum(_mm(h, wu_s[:, cols]), 0.0)
            relu_s[:, cols] = ru
            hd = (ru * ru).astype(BF16)
            hdn_ref[:, cols] = hd
            yout = yout + _mm(hd, wd_s[cols, :])
        err = yout - tg_ref[...]
        loss_ref[...] = loss_ref[...] + 0.5 * jnp.sum(err * err) * (1.0 / D_MODEL)
        dy = err * (1.0 / D_MODEL)
        dyb = dy.astype(BF16)
        dyb_ref[...] = dyb
        dh = jnp.zeros((tm, D_MODEL), F32)
        for c in range(n_fc):
            cols = slice(c * fc, (c + 1) * fc)
            dup = (_nt(dyb, wd_s[cols, :]) * (2.0 * relu_s[:, cols])).astype(BF16)
            dup_ref[:, cols] = dup
            dh = dh + _nt(dup, wu_s[:, cols])
        dxn, g2_term = _rms_bwd(dh, dx2_ref[...], r, g2v)
        dx2_ref[...] = dy + dxn
        dg2_ref[...] = dg2_ref[...] + _colsum(g2_term)
        dx2b = dx2_ref[...].astype(BF16)
        dx2b_ref[...] = dx2b
        yv = y_ref[...]
        av = attn_ref[...]
        z, t, sg = _glu(yv, wg_s[...], b_ref[...])
        s = z * sg
        d_attn, ga_term = _rms_bwd(_nt(dx2b, wo_s[0:ATTN_W, :]), av, _rms(av), ga_ref[...])
        d_s, gs_term = _rms_bwd(_nt(dx2b, wo_s[ATTN_W:, :]), s, _rms(s), gs_ref[...])
        dattn_ref[...] = d_attn
        dgp = d_s * z * sg * (1.0 - sg)
        dgpb = dgp.astype(BF16)
        dgp_ref[...] = dgpb
        dy_ref[...] = (d_s * sg + _nt(dgpb, wg_s[...])) * _gelu_grad(yv, t)
        dga_ref[...] = dga_ref[...] + _colsum(ga_term)
        dgs_ref[...] = dgs_ref[...] + _colsum(gs_term)
        db_ref[...] = db_ref[...] + _colsum(dgp)

    row = lambda i: (i, 0)
    const = lambda i: (0, 0)
    wide = lambda n: pl.BlockSpec((tm, n), row)
    vec = lambda n: pl.BlockSpec((1, n), const)
    any_spec = pl.BlockSpec(memory_space=pl.ANY)
    f32 = lambda n: jax.ShapeDtypeStruct((t_len, n), F32)
    b16 = lambda n: jax.ShapeDtypeStruct((t_len, n), BF16)
    acc = lambda n: jax.ShapeDtypeStruct((1, n), F32)
    return pl.pallas_call(
        body, name="mixer_mlp", grid=(t_len // tm,),
        in_specs=[wide(ATTN_W), wide(SSM_W), wide(D_MODEL), wide(D_MODEL), vec(SSM_W), vec(ATTN_W), vec(SSM_W), vec(D_MODEL),
                  pl.BlockSpec((SSM_W, SSM_W), const), pl.BlockSpec((D_MODEL, D_MODEL), const), any_spec, any_spec],
        out_specs=[wide(D_MODEL), wide(D_FF), wide(D_FF), wide(D_MODEL), wide(D_MODEL), wide(D_MODEL), wide(SSM_W),
                   wide(ATTN_W), wide(SSM_W), wide(D_MODEL), wide(SSM_W),
                   vec(D_MODEL), vec(LANES), vec(ATTN_W), vec(SSM_W), vec(SSM_W)],
        out_shape=[f32(D_MODEL), b16(D_FF), b16(D_FF), b16(D_MODEL), b16(D_MODEL), b16(D_MODEL), b16(SSM_W),
                   f32(ATTN_W), f32(SSM_W), b16(D_MODEL), b16(SSM_W),
                   acc(D_MODEL), acc(LANES), acc(ATTN_W), acc(SSM_W), acc(SSM_W)],
        scratch_shapes=[pltpu.VMEM((D_MODEL, D_FF), BF16), pltpu.VMEM((D_FF, D_MODEL), BF16),
                        pltpu.VMEM((tm, D_FF), F32), pltpu.SemaphoreType.DMA((2,))],
        compiler_params=_params("arbitrary"),
    )(attn, y, x, target, glu_b, ga, gs, g2, wg, wo, wu, wd)


def _inproj_bwd(dqs, dkn, dv, du, q_raw, k_raw, x, dx2, wi, g1, gq, gk, ones64):
    t_len = x.shape[0]
    tm = 512
    n_heads = ATTN_W // HEAD

    def body(dqs_ref, dkn_ref, dv_ref, du_ref, q_ref, k_ref, x_ref, dx2_ref, wi_ref, g1_ref, gq_ref, gk_ref, bd_ref,
             gx_ref, dproj_ref, dg1_ref, dgq_ref, dgk_ref, accq, acck):
        i = pl.program_id(0)

        @pl.when(i == 0)
        def _():
            dg1_ref[...] = jnp.zeros_like(dg1_ref)
            accq[...] = jnp.zeros_like(accq)
            acck[...] = jnp.zeros_like(acck)

        bd = bd_ref[...]

        def head_norm_bwd(dy, raw, gain, acc):
            r = lax.rsqrt(_group_mean(raw * raw, bd, HEAD) + EPS)
            xh = raw * r
            dxh = dy * gain
            acc[...] = acc[...] + _colsum(dy * xh)
            return r * (dxh - xh * _group_mean(dxh * xh, bd, HEAD))

        dq = head_norm_bwd(dqs_ref[...] * (HEAD ** -0.5), q_ref[...], gq_ref[...], accq)
        dk = head_norm_bwd(dkn_ref[...], k_ref[...], gk_ref[...], acck)
        dproj_ref[:, 0:ATTN_W] = dq.astype(BF16)
        dproj_ref[:, ATTN_W:2 * ATTN_W] = dk.astype(BF16)
        dproj_ref[:, 2 * ATTN_W:3 * ATTN_W] = dv_ref[...].astype(BF16)
        dproj_ref[:, 3 * ATTN_W:] = du_ref[...].astype(BF16)
        dxn = _nt(dproj_ref[...], wi_ref[...])
        xv = x_ref[...]
        g1v = g1_ref[...]
        dx, g1_term = _rms_bwd(dxn, xv, _rms(xv), g1v)
        gx_ref[...] = dx2_ref[...] + dx
        dg1_ref[...] = dg1_ref[...] + _colsum(g1_term)

        @pl.when(i == pl.num_programs(0) - 1)
        def _():
            for acc, out in ((accq, dgq_ref), (acck, dgk_ref)):
                tot = acc[:, 0:HEAD]
                for h in range(1, n_heads):
                    tot = tot + acc[:, h * HEAD:(h + 1) * HEAD]
                out[...] = tot

    row = lambda i: (i, 0)
    const = lambda i: (0, 0)
    aw = pl.BlockSpec((tm, ATTN_W), row)
    dm = pl.BlockSpec((tm, D_MODEL), row)
    return pl.pallas_call(
        body, name="inproj_bwd", grid=(t_len // tm,),
        in_specs=[aw, aw, aw, aw, aw, aw, dm, dm, pl.BlockSpec((D_MODEL, PROJ_W), const), pl.BlockSpec((1, D_MODEL), const),
                  pl.BlockSpec((1, ATTN_W), const), pl.BlockSpec((1, ATTN_W), const), pl.BlockSpec(ones64.shape, const)],
        out_specs=[dm, pl.BlockSpec((tm, PROJ_W), row), pl.BlockSpec((1, D_MODEL), const),
                   pl.BlockSpec((1, HEAD), const), pl.BlockSpec((1, HEAD), const)],
        out_shape=[jax.ShapeDtypeStruct((t_len, D_MODEL), F32), jax.ShapeDtypeStruct((t_len, PROJ_W), BF16),
                   jax.ShapeDtypeStruct((1, D_MODEL), F32), jax.ShapeDtypeStruct((1, HEAD), F32),
                   jax.ShapeDtypeStruct((1, HEAD), F32)],
        scratch_shapes=[pltpu.VMEM((1, ATTN_W), F32), pltpu.VMEM((1, ATTN_W), F32)],
        compiler_params=_params("arbitrary"),
    )(dqs, dkn, dv, du, q_raw, k_raw, x, dx2, wi, g1, gq, gk, ones64)


def _grad_matmul(a, b, name, side=None, side_args=()):
    t_len, m = a.shape
    n = b.shape[1]
    bm, bn, bt = min(m, 1024), min(n, 1024), min(t_len, 4096)
    grid = (m // bm, n // bn, t_len // bt)
    ns = side.n if side is not None else 0

    def body(*refs):
        a_ref, b_ref, o_ref = refs[0], refs[1], refs[2 + ns]
        ids = [pl.program_id(k) for k in range(3)]
        if side is not None:
            side.bind(refs[2:2 + ns], refs[3 + ns:3 + 2 * ns], refs[3 + 2 * ns:])
            pl.when((ids[0] == 0) & (ids[1] == 0) & (ids[2] == 0))(side.start)

        @pl.when(ids[2] == 0)
        def _():
            o_ref[...] = jnp.zeros_like(o_ref)

        o_ref[...] = o_ref[...] + _tn(a_ref[...], b_ref[...])
        if side is not None:
            pl.when((ids[0] == grid[0] - 1) & (ids[1] == grid[1] - 1) & (ids[2] == grid[2] - 1))(side.finish)

    extra = (side.in_specs(), side.out_specs(), side.out_shapes(), side.scratch_shapes()) if side is not None else ([], [], [], [])
    out = pl.pallas_call(
        body, name=name, grid=grid,
        in_specs=[pl.BlockSpec((bt, bm), lambda i, j, k: (k, i)), pl.BlockSpec((bt, bn), lambda i, j, k: (k, j))] + extra[0],
        out_specs=[pl.BlockSpec((bm, bn), lambda i, j, k: (i, j))] + extra[1],
        out_shape=[jax.ShapeDtypeStruct((m, n), F32)] + extra[2],
        scratch_shapes=extra[3],
        compiler_params=_params("arbitrary", "arbitrary", "arbitrary"),
    )(a, b, *side_args)
    return out[0] if side is None else out


def _adamw_update(w_ref, g_ref, m_ref, v_ref, d_o, m_o, v_o):
    gv = g_ref[...]
    mn = ADAM_B1 * m_ref[...] + (1.0 - ADAM_B1) * gv
    vn = ADAM_B2 * v_ref[...] + (1.0 - ADAM_B2) * jnp.square(gv)
    m_hat = mn / (1.0 - ADAM_B1 ** ADAM_STEP)
    v_hat = vn / (1.0 - ADAM_B2 ** ADAM_STEP)
    d_o[...] = -ADAM_LR * (m_hat / (jnp.sqrt(v_hat) + ADAM_EPS) + ADAM_WD * w_ref[...])
    m_o[...] = mn
    v_o[...] = vn


def _adamw_many(ws, gs, ms, vs):
    n = len(ws)

    def body(*refs):
        for i in range(n):
            _adamw_update(*[refs[k * n + i] for k in range(7)])

    shapes = [jax.ShapeDtypeStruct(w.shape, F32) for w in ws]
    outs = pl.pallas_call(body, name="adamw_small", out_shape=shapes * 3,
                          compiler_params=pltpu.CompilerParams(vmem_limit_bytes=VMEM_LIMIT))(*ws, *gs, *ms, *vs)
    return outs[0:n], outs[n:2 * n], outs[2 * n:]


def _adamw(w, g, m, v, name):
    rows, cols = w.shape
    br = _row_block(rows, 256)
    body = functools.partial(_adamw_update)

    spec = pl.BlockSpec((br, cols), lambda i: (i, 0))
    shape = jax.ShapeDtypeStruct((rows, cols), F32)
    return pl.pallas_call(
        body, name=name, grid=(rows // br,), in_specs=[spec] * 4, out_specs=[spec] * 3, out_shape=[shape] * 3,
        compiler_params=_params("arbitrary"),
    )(w, g, m, v)


def _sum_arrays(arrs, name, out_dtype=F32):
    rows, cols = arrs[0].shape
    br = _row_block(rows, 512)
    n = len(arrs)

    def body(*refs):
        tot = refs[0][...]
        for r in refs[1:n]:
            tot = tot + r[...]
        refs[n][...] = tot.astype(out_dtype)

    spec = pl.BlockSpec((br, cols), lambda i: (i, 0))
    return pl.pallas_call(
        body, name=name, grid=(rows // br,), in_specs=[spec] * n, out_specs=spec,
        out_shape=jax.ShapeDtypeStruct((rows, cols), out_dtype), compiler_params=_params("arbitrary"),
    )(*arrs)


GPL = N_GROUPS // N_LB
SW = GPL * N_STATE


def _eye_groups():
    return jnp.eye(GPL, dtype=F32)


def _s5_matrices(ab_r, ab_i, bb_r, bb_i, c_re, c_im, d_skip):
    eye = _eye_groups()
    a_cat = jnp.concatenate([ab_r.reshape(N_LB, HALF, LANES), ab_i.reshape(N_LB, HALF, LANES)], axis=1)

    def b_part(bb):
        b4 = jnp.transpose(bb.reshape(N_LB, GPL, N_STATE, GROUP_W), (0, 1, 3, 2))
        return (b4[:, :, :, None, :] * eye[None, :, None, :, None]).reshape(N_LB, LANES, SW)

    def c_part(cc):
        c4 = jnp.transpose(cc.reshape(N_LB, GPL, GROUP_W, N_STATE), (0, 1, 3, 2))
        return (c4[:, :, :, None, :] * eye[None, :, None, :, None]).reshape(N_LB, SW, LANES)

    b_mat = jnp.concatenate([b_part(bb_r), b_part(bb_i)], axis=2).astype(BF16)
    c_mat = jnp.concatenate([c_part(c_re), -c_part(c_im)], axis=1).astype(BF16)
    return a_cat, b_mat, c_mat, d_skip.reshape(N_LB, 1, LANES)


def _s5_unpack_grads(db, dc, da, dd):
    eye = _eye_groups()
    mask = eye[None, :, None, None, :, None]
    d6 = jnp.sum(db.reshape(N_LB, GPL, GROUP_W, 2, GPL, N_STATE) * mask, axis=4)
    dbb = jnp.transpose(d6, (3, 0, 1, 4, 2)).reshape(2, N_GROUPS * N_STATE, GROUP_W)
    c6 = jnp.sum(dc.reshape(N_LB, GPL, GROUP_W, 2, GPL, N_STATE) * mask, axis=4)
    dcc = jnp.transpose(c6, (3, 0, 1, 2, 4)).reshape(2, N_GROUPS, GROUP_W, N_STATE)
    dab_r = da[:, :HALF].reshape(N_GROUPS * N_STATE, 1)
    dab_i = da[:, HALF:].reshape(N_GROUPS * N_STATE, 1)
    return dab_r, dab_i, dbb[0], dbb[1], dcc[0], -dcc[1], dd.reshape(N_GROUPS, GROUP_W)


def _block_ones(n, width):
    i = lax.broadcasted_iota(jnp.int32, (n, n), 0) // width
    j = lax.broadcasted_iota(jnp.int32, (n, n), 1) // width
    return (i == j).astype(BF16)


def _tile_heads(g):
    return jnp.tile(g.reshape(1, HEAD), (1, ATTN_W // HEAD))


def _local_step(x, target, wi, rest, p, fwd_side=None, bwd_side=None, last_side=None):
    ones64 = _block_ones(MXU_WIDTH, HEAD)
    ones_hp = _block_ones(LANES, HEAD)
    g1 = p["norm1_g"].reshape(1, D_MODEL)
    g2 = p["norm2_g"].reshape(1, D_MODEL)
    gq = _tile_heads(p["q_norm_g"])
    gk = _tile_heads(p["k_norm_g"])
    ga = p["attn_out_norm_g"].reshape(1, ATTN_W)
    gs = p["ssm_out_norm_g"].reshape(1, SSM_W)
    glu_b = p["glu_b"].reshape(1, SSM_W)
    n_gp = N_GROUPS * N_STATE
    lr = p["ssm_a_re"].reshape(n_gp, 1)
    li = p["ssm_a_im"].reshape(n_gp, 1)
    ldt = jnp.repeat(p["ssm_log_dt"].reshape(N_GROUPS), N_STATE).reshape(n_gp, 1)
    br = p["ssm_b_re"].reshape(n_gp, GROUP_W)
    bi = p["ssm_b_im"].reshape(n_gp, GROUP_W)
    ab_r, ab_i, bb_r, bb_i = _disc_fwd(lr, li, ldt, br, bi)
    a_cat, b_mat, c_mat, d_mat = _s5_matrices(
        ab_r, ab_i, bb_r, bb_i, p["ssm_c_re"].reshape(N_GROUPS, GROUP_W, N_STATE),
        p["ssm_c_im"].reshape(N_GROUPS, GROUP_W, N_STATE), p["ssm_d"])

    xn, qn, kn, vv, u, q_raw, k_raw = _inproj_fwd(x, g1, wi, gq, gk, ones64)
    if fwd_side is None:
        attn, lse = _attn_fwd(qn, kn, vv)
    else:
        attn, lse, *rest = _attn_fwd(qn, kn, vv, *fwd_side)
    wg, wo, wu, wd = rest
    y, states = _s5_fwd(u, a_cat, b_mat, c_mat, d_mat)
    (dx2, hdn, dup, h, dyb, mix, z, d_attn, dy_ssm, dx2b, dgp, dg2, loss, dga, dgs, dglu_b) = _mixer_mlp(
        attn, y, x, target, wg, glu_b, ga, gs, wo, g2, wu, wd)
    big = {"w_mlp_up": _grad_matmul(h, dup, "grad_w_mlp_up"), "w_mlp_down": _grad_matmul(hdn, dyb, "grad_w_mlp_down"),
           "w_out": _grad_matmul(mix, dx2b, "grad_w_out"), "glu_w": _grad_matmul(z, dgp, "grad_glu_w")}
    rode = []
    if bwd_side is None:
        dqs, dkn, dvv = _attn_bwd(qn, kn, vv, attn, d_attn, lse, ones_hp)
    else:
        dqs, dkn, dvv, *rode = _attn_bwd(qn, kn, vv, attn, d_attn, lse, ones_hp, *bwd_side(big))
    du, db, dc, da, dd = _s5_bwd(u, dy_ssm, states, a_cat, b_mat, c_mat, d_mat)
    grad_x, dproj, dg1, dgq, dgk = _inproj_bwd(dqs, dkn, dvv, du, q_raw, k_raw, x, dx2, wi, g1, gq, gk, ones64)

    dab_r, dab_i, dbb_r, dbb_i, dc_re, dc_im, dd_g = _s5_unpack_grads(db, dc, da, dd)
    cot = {"norm1_g": dg1, "q_norm_g": dgq, "k_norm_g": dgk, "ab_r": dab_r, "ab_i": dab_i, "bb_r": dbb_r, "bb_i": dbb_i,
           "ssm_c_re": dc_re, "ssm_c_im": dc_im, "ssm_d": dd_g, "glu_b": dglu_b, "attn_out_norm_g": dga,
           "ssm_out_norm_g": dgs, "norm2_g": dg2}
    rode_last = []
    if last_side is None:
        big["w_in"] = _grad_matmul(xn, dproj, "grad_w_in")
    else:
        big["w_in"], *rode_last = _grad_matmul(xn, dproj, "grad_w_in", *last_side(cot))
    return loss[0, 0], grad_x, big, cot, (lr, li, ldt, br, bi), rode, rode_last


COT_NAMES = ("norm1_g", "q_norm_g", "k_norm_g", "ab_r", "ab_i", "bb_r", "bb_i", "ssm_c_re", "ssm_c_im", "ssm_d",
             "glu_b", "attn_out_norm_g", "ssm_out_norm_g", "norm2_g")
SMALL_NAMES = ("norm1_g", "q_norm_g", "k_norm_g", "ssm_a_re", "ssm_a_im", "ssm_log_dt", "ssm_b_re", "ssm_b_im",
               "ssm_c_re", "ssm_c_im", "ssm_d", "glu_b", "attn_out_norm_g", "ssm_out_norm_g", "norm2_g")
BIG_NAMES = ("w_in", "glu_w", "w_out", "w_mlp_up", "w_mlp_down")
PACK_ROWS = 1152


def _pack(arrs):
    flat = jnp.concatenate([a.reshape(-1) for a in arrs])
    return jnp.pad(flat, (0, PACK_ROWS * LANES - flat.shape[0])).reshape(PACK_ROWS, LANES)


def _unpack(packed, like):
    flat = packed.reshape(-1)
    out, pos = [], 0
    for a in like:
        out.append(flat[pos:pos + a.size].reshape(a.shape))
        pos += a.size
    return out


def _small_grads(cot, disc_in, p):
    lr, li, ldt, br, bi = disc_in
    group_sum = (lax.broadcasted_iota(jnp.int32, (N_GROUPS, N_GROUPS * N_STATE), 1) // N_STATE
                 == lax.broadcasted_iota(jnp.int32, (N_GROUPS, N_GROUPS * N_STATE), 0)).astype(F32)
    dlr, dli, dldt, dbr, dbi = _disc_bwd(lr, li, ldt, br, bi, cot["ab_r"], cot["ab_i"], cot["bb_r"], cot["bb_i"], group_sum)
    g = dict(cot)
    g.update(ssm_a_re=dlr, ssm_a_im=dli, ssm_log_dt=dldt[:, 0], ssm_b_re=dbr, ssm_b_im=dbi)
    return {n: g[n].reshape(p[n].shape) for n in SMALL_NAMES}


BIG = {
    "w_in": ((D_MODEL, PROJ_W), 1, PROJ_W // 4, 0, D_MODEL // 2),
    "glu_w": ((SSM_W, SSM_W), 0, SSM_W // 4, 1, SSM_W // 2),
    "w_out": ((D_MODEL, D_MODEL), 0, D_MODEL // 4, 1, D_MODEL // 2),
    "w_mlp_up": ((D_MODEL, D_FF), 1, D_FF // 4, 0, D_MODEL // 2),
    "w_mlp_down": ((D_FF, D_MODEL), 0, D_FF // 4, 1, D_MODEL // 2),
}
N_BIG = len(BIG_NAMES)
N_CHIPS = 4
ANY = pl.BlockSpec(memory_space=pl.ANY)


def _cut(name, shard=False, half=False):
    shape, s_ax, s_sz, h_ax, h_sz = BIG[name]
    shape = list(shape)
    if shard:
        shape[s_ax] = s_sz
    if half:
        shape[h_ax] = h_sz
    return tuple(shape)


def _window(name, base, shard=None, half=None):
    _, s_ax, s_sz, h_ax, h_sz = BIG[name]
    idx = [pl.ds(0, base[0]), pl.ds(0, base[1])]
    if shard is not None:
        idx[s_ax] = pl.ds(pl.multiple_of(shard * s_sz, s_sz), s_sz)
    if half is not None:
        idx[h_ax] = pl.ds(pl.multiple_of(half * h_sz, h_sz), h_sz)
    return tuple(idx)


def _mesh_pos():
    return lax.axis_index("x"), lax.axis_index("y"), lax.axis_index("c")


def _other_chips(x, y):
    return [(1 - x, y, 2 * (1 - x) + y), (x, 1 - y, 2 * x + 1 - y), (1 - x, 1 - y, 2 * (1 - x) + 1 - y)]


def _remote(src, dst, send_sem, recv_sem, dev):
    return pltpu.make_async_remote_copy(src_ref=src, dst_ref=dst, send_sem=send_sem, recv_sem=recv_sem,
                                        device_id=dev, device_id_type=MESH)


def _start_remote(src, dst, send_sem, recv_sem, dev):
    cp = _remote(src, dst, send_sem, recv_sem, dev)
    cp.start()
    return cp


class _Gather:
    def __init__(self, names):
        self.names = tuple(names)
        self.n = len(self.names)

    def in_specs(self):
        return [pl.BlockSpec(memory_space=pltpu.VMEM)] * self.n

    def out_specs(self):
        return [ANY] * self.n

    def out_shapes(self):
        return [jax.ShapeDtypeStruct(BIG[w][0], BF16) for w in self.names]

    def scratch_shapes(self):
        n_sem = (N_CHIPS - 1) * self.n
        return ([pltpu.VMEM(_cut(w, shard=True), BF16) for w in self.names]
                + [pltpu.SemaphoreType.DMA((n_sem,))] * 4 + [pltpu.SemaphoreType.DMA((self.n,))])

    def bind(self, ins, outs, scratch):
        self.ins, self.outs = ins, outs
        self.stage = scratch[:self.n]
        self.send, self.recv, self.fsend, self.frecv, self.lsem = scratch[self.n:]

    def _copies(self):
        x, y, c = _mesh_pos()
        me = 2 * x + y
        sib = (x, y, 1 - c)
        local, sends, lands, fwds, flands = [], [], [], [], []
        for w, n in enumerate(self.names):
            local.append(pltpu.make_async_copy(self.stage[w], self.outs[w].at[_window(n, BIG[n][0], shard=me)], self.lsem.at[w]))
        for k, (px, py, pj) in enumerate(_other_chips(x, y)):
            for w, n in enumerate(self.names):
                s = k * self.n + w
                sends.append(_remote(self.stage[w].at[_window(n, _cut(n, shard=True), half=c)],
                                     self.outs[w].at[_window(n, BIG[n][0], shard=me, half=c)],
                                     self.send.at[s], self.recv.at[s], (px, py, c)))
                got = self.outs[w].at[_window(n, BIG[n][0], shard=pj, half=c)]
                lands.append(_remote(got, got, self.send.at[s], self.recv.at[s], (px, py, c)))
                fwds.append(_remote(got, got, self.fsend.at[s], self.frecv.at[s], sib))
                theirs = self.outs[w].at[_window(n, BIG[n][0], shard=pj, half=1 - c)]
                flands.append(_remote(theirs, theirs, self.fsend.at[s], self.frecv.at[s], sib))
        return local, sends, lands, fwds, flands

    def start(self):
        for w in range(self.n):
            self.stage[w][...] = self.ins[w][...].astype(BF16)
        local, sends, _, _, _ = self._copies()
        for cp in local + sends:
            cp.start()

    def forward(self):
        _, _, lands, fwds, _ = self._copies()
        for land, fwd in zip(lands, fwds):
            land.wait_recv()
            fwd.start()

    def finish(self):
        local, sends, _, fwds, flands = self._copies()
        for cp in flands:
            cp.wait_recv()
        for cp in sends + fwds:
            cp.wait_send()
        for cp in local:
            cp.wait()


def _gather_weights(shards, names):
    g = _Gather(names)

    def body(*refs):
        g.bind(refs[0:g.n], refs[g.n:2 * g.n], refs[2 * g.n:])
        g.start()
        g.forward()
        g.finish()

    return pl.pallas_call(
        body, name="gather_" + "_".join(names), in_specs=g.in_specs(), out_specs=g.out_specs(), out_shape=g.out_shapes(),
        scratch_shapes=g.scratch_shapes(), compiler_params=pltpu.CompilerParams(vmem_limit_bytes=VMEM_LIMIT),
    )(*[shards[n] for n in names])


def _pair_exchange(grads, names, packed=None):
    n_big = len(names)
    n_all = n_big + (packed is not None)

    def body(*refs):
        ins, got = refs[0:n_all], refs[n_all:2 * n_all]
        send, recv = refs[2 * n_all:]
        x, y, c = _mesh_pos()
        sib = (x, y, 1 - c)
        copies = []
        for w, n in enumerate(names):
            copies.append(_start_remote(ins[w].at[_window(n, BIG[n][0], half=1 - c)], got[w], send.at[w], recv.at[w], sib))
        if packed is not None:
            copies.append(_start_remote(ins[n_big], got[n_big], send.at[n_big], recv.at[n_big], sib))
        for cp in copies:
            cp.wait()

    shapes = [jax.ShapeDtypeStruct(_cut(n, half=True), F32) for n in names]
    args = [grads[n] for n in names]
    if packed is not None:
        shapes.append(jax.ShapeDtypeStruct(packed.shape, F32))
        args.append(packed)
    return pl.pallas_call(
        body, name="grad_pair_exchange_" + ("_".join(names) or "small"), in_specs=[ANY] * n_all, out_specs=[ANY] * n_all,
        out_shape=shapes,
        scratch_shapes=[pltpu.SemaphoreType.DMA((n_all,)), pltpu.SemaphoreType.DMA((n_all,))],
    )(*args)


def _pair_sum(name, full, got, core):
    _, _, _, h_ax, _ = BIG[name]
    rows, cols = _cut(name, half=True)
    br = _row_block(rows, 512)
    nb = rows // br
    own_map = (lambda i, c: (i + c[0] * nb, 0)) if h_ax == 0 else (lambda i, c: (i, c[0]))

    def body(c_ref, own_ref, got_ref, o_ref):
        o_ref[...] = (own_ref[...] + got_ref[...]).astype(BF16)

    plain = pl.BlockSpec((br, cols), lambda i, c: (i, 0))
    return pl.pallas_call(
        body, name="pair_sum_" + name,
        grid_spec=pltpu.PrefetchScalarGridSpec(num_scalar_prefetch=1, grid=(nb,),
                                               in_specs=[pl.BlockSpec((br, cols), own_map), plain], out_specs=plain),
        out_shape=jax.ShapeDtypeStruct((rows, cols), BF16), compiler_params=_params("arbitrary"),
    )(core, full, got)


class _ChipExchange:
    def __init__(self, names, packed_shape=None):
        self.names = tuple(names)
        self.packed_shape = packed_shape
        self.n = len(self.names) + (packed_shape is not None)

    def in_specs(self):
        return [ANY] * self.n

    def out_specs(self):
        return [ANY] * self.n

    def out_shapes(self):
        shapes = [jax.ShapeDtypeStruct((N_CHIPS,) + _cut(w, shard=True, half=True), BF16) for w in self.names]
        if self.packed_shape is not None:
            shapes.append(jax.ShapeDtypeStruct((N_CHIPS,) + tuple(self.packed_shape), F32))
        return shapes

    def scratch_shapes(self):
        n_sem = (N_CHIPS - 1) * self.n
        return [pltpu.SemaphoreType.DMA((n_sem,)), pltpu.SemaphoreType.DMA((n_sem,))]

    def bind(self, ins, outs, scratch):
        self.ins, self.outs = ins, outs
        self.send, self.recv = scratch

    def _piece(self, w, shard):
        if w >= len(self.names):
            return self.ins[w]
        n = self.names[w]
        return self.ins[w].at[_window(n, _cut(n, half=True), shard=shard)]

    def _copies(self):
        x, y, c = _mesh_pos()
        me = 2 * x + y
        sends, lands = [], []
        for k, (px, py, pj) in enumerate(_other_chips(x, y)):
            for w in range(self.n):
                s = k * self.n + w
                sends.append(_remote(self._piece(w, pj), self.outs[w].at[me], self.send.at[s], self.recv.at[s], (px, py, c)))
                lands.append(_remote(self._piece(w, me), self.outs[w].at[pj], self.send.at[s], self.recv.at[s], (px, py, c)))
        return sends, lands

    def start(self):
        for cp in self._copies()[0]:
            cp.start()

    def finish(self):
        sends, lands = self._copies()
        for cp in lands:
            cp.wait_recv()
        for cp in sends:
            cp.wait_send()


def _chip_exchange(halves, names):
    ex = _ChipExchange(names)

    def body(*refs):
        ex.bind(refs[0:ex.n], refs[ex.n:2 * ex.n], refs[2 * ex.n:])
        ex.start()
        ex.finish()

    return pl.pallas_call(
        body, name="grad_chip_exchange", in_specs=ex.in_specs(), out_specs=ex.out_specs(), out_shape=ex.out_shapes(),
        scratch_shapes=ex.scratch_shapes(),
    )(*halves)


def _chip_sum(name, own, slots, chip):
    n_slot, rows, cols = slots.shape
    br = _row_block(rows, 512)
    nb = rows // br
    if name in BIG and BIG[name][1] == 1:
        own_map = lambda i, m: (i, m[0])
    elif name in BIG:
        own_map = lambda i, m: (i + m[0] * nb, 0)
    else:
        own_map = lambda i, m: (i, 0)

    def slot_map(j):
        return lambda i, m: (jnp.where(m[0] == j, (j + 1) % n_slot, j), i, 0)

    def body(m_ref, own_ref, *refs):
        own_blk = own_ref[...].astype(F32)
        tot = None
        for j in range(n_slot):
            term = jnp.where(m_ref[0] == j, own_blk, refs[j][...].astype(F32))
            tot = term if tot is None else tot + term
        refs[n_slot][...] = tot

    in_specs = [pl.BlockSpec((br, cols), own_map)] + [pl.BlockSpec((None, br, cols), slot_map(j)) for j in range(n_slot)]
    return pl.pallas_call(
        body, name="chip_sum_" + name,
        grid_spec=pltpu.PrefetchScalarGridSpec(num_scalar_prefetch=1, grid=(nb,), in_specs=in_specs,
                                               out_specs=pl.BlockSpec((br, cols), lambda i, m: (i, 0))),
        out_shape=jax.ShapeDtypeStruct((rows, cols), F32), compiler_params=_params("arbitrary"),
    )(chip, own, *([slots] * n_slot))


def _half_exchange(pieces):
    def body(*refs):
        ins, outs = refs[0:N_BIG], refs[N_BIG:2 * N_BIG]
        send, recv = refs[2 * N_BIG:]
        x, y, c = _mesh_pos()
        sib = (x, y, 1 - c)
        copies = []
        for w, n in enumerate(BIG_NAMES):
            copies.append(_start_remote(ins[w], outs[w], send.at[w], recv.at[w], sib))
        for cp in copies:
            cp.wait()

    return pl.pallas_call(
        body, name="grad_half_exchange", in_specs=[ANY] * N_BIG, out_specs=[ANY] * N_BIG,
        out_shape=[jax.ShapeDtypeStruct(_cut(n, shard=True, half=True), F32) for n in BIG_NAMES],
        scratch_shapes=[pltpu.SemaphoreType.DMA((N_BIG,)), pltpu.SemaphoreType.DMA((N_BIG,))],
    )(*pieces)


WEIGHT_NAMES = ("norm1_g", "w_in", "q_norm_g", "k_norm_g", "ssm_a_re", "ssm_a_im", "ssm_log_dt", "ssm_b_re", "ssm_b_im",
                "ssm_c_re", "ssm_c_im", "ssm_d", "glu_w", "glu_b", "attn_out_norm_g", "ssm_out_norm_g", "w_out", "norm2_g",
                "w_mlp_up", "w_mlp_down")


def _train_step(a):
    x = a["x"][0]
    target = a["loss_target"][0]
    shards = {n: a[n][0] for n in BIG_NAMES}
    p = {n: a[n][0] for n in SMALL_NAMES}
    core = lax.axis_index("c").astype(jnp.int32).reshape(1)
    chip_id = (2 * lax.axis_index("x") + lax.axis_index("y")).astype(jnp.int32).reshape(1)

    later = ("glu_w", "w_out", "w_mlp_up", "w_mlp_down")
    early = ("w_mlp_up", "w_mlp_down", "w_out", "glu_w")
    late = ("w_in",)
    (wi,) = _gather_weights(shards, ("w_in",))
    chip = {}

    def bwd_side(grads):
        got = _pair_exchange(grads, early)
        for n, g in zip(early, got):
            chip[n] = _pair_sum(n, grads[n], g, core)
        return _ChipExchange(early), [chip[n] for n in early]

    small = {}

    def last_side(cot):
        small["list"] = [cot[n] for n in COT_NAMES]
        packed = _pack(small["list"])
        (got_packed,) = _pair_exchange({}, (), packed)
        small["chip"] = _sum_arrays([packed, got_packed], "pair_sum_small")
        return _ChipExchange((), packed.shape), [small["chip"]]

    loss, grad_x, big, cot, disc_in, early_slots, (small_slots,) = _local_step(
        x, target, wi, None, p, fwd_side=(_Gather(later), [shards[n] for n in later]), bwd_side=bwd_side,
        last_side=last_side)
    slots = dict(zip(early, early_slots))

    got = _pair_exchange(big, late)
    for n, g in zip(late, got):
        chip[n] = _pair_sum(n, big[n], g, core)
    slots.update(zip(late, _chip_exchange([chip[n] for n in late], late)))
    pieces = [_chip_sum(n, chip[n], slots[n], chip_id) for n in BIG_NAMES]
    small_sum = _chip_sum("small", small["chip"], small_slots, chip_id)
    cot_list = small["list"]
    shard_grads = {}
    for n, mine, theirs in zip(BIG_NAMES, pieces, _half_exchange(pieces)):
        h_ax = BIG[n][3]
        shard_grads[n] = jnp.where(core[0] == 0, jnp.concatenate([mine, theirs], axis=h_ax),
                                   jnp.concatenate([theirs, mine], axis=h_ax))
    small_grads = _small_grads(dict(zip(COT_NAMES, _unpack(small_sum, cot_list))), disc_in, p)

    grads, delta, new_m, new_v = {}, {}, {}, {}
    for n in BIG_NAMES:
        grads[n] = shard_grads[n]
        delta[n], new_m[n], new_v[n] = _adamw(a[n][0], grads[n], a["m_" + n][0], a["v_" + n][0], "adamw_" + n)
    swapped = ("ssm_b_re", "ssm_b_im", "ssm_d")

    def flat2(n, t):
        t = jnp.swapaxes(t, -1, -2) if n in swapped else t
        return t.reshape(-1, t.shape[-1])

    def unflat(n, t2):
        shape = p[n].shape
        if n in swapped:
            return jnp.swapaxes(t2.reshape(shape[:-2] + (shape[-1], shape[-2])), -1, -2)
        return t2.reshape(shape)

    res = _adamw_many([flat2(n, p[n]) for n in SMALL_NAMES], [flat2(n, small_grads[n]) for n in SMALL_NAMES],
                      [flat2(n, a["m_" + n][0]) for n in SMALL_NAMES], [flat2(n, a["v_" + n][0]) for n in SMALL_NAMES])
    for store, outs in zip((delta, new_m, new_v), res):
        store.update((n, unflat(n, t2)) for n, t2 in zip(SMALL_NAMES, outs))
    grads.update(small_grads)

    total = lax.psum(loss, ("x", "y", "c"))
    out = [total, grad_x[None]]
    for store in (grads, delta, new_m, new_v):
        out += [store[n].reshape(a[n].shape) for n in WEIGHT_NAMES]
    return tuple(out)


def kernel(x, norm1_g, w_in, q_norm_g, k_norm_g, ssm_a_re, ssm_a_im, ssm_log_dt, ssm_b_re, ssm_b_im, ssm_c_re, ssm_c_im, ssm_d, glu_w, glu_b, attn_out_norm_g, ssm_out_norm_g, w_out, norm2_g, w_mlp_up, w_mlp_down, loss_target, m_norm1_g, m_w_in, m_q_norm_g, m_k_norm_g, m_ssm_a_re, m_ssm_a_im, m_ssm_log_dt, m_ssm_b_re, m_ssm_b_im, m_ssm_c_re, m_ssm_c_im, m_ssm_d, m_glu_w, m_glu_b, m_attn_out_norm_g, m_ssm_out_norm_g, m_w_out, m_norm2_g, m_w_mlp_up, m_w_mlp_down, v_norm1_g, v_w_in, v_q_norm_g, v_k_norm_g, v_ssm_a_re, v_ssm_a_im, v_ssm_log_dt, v_ssm_b_re, v_ssm_b_im, v_ssm_c_re, v_ssm_c_im, v_ssm_d, v_glu_w, v_glu_b, v_attn_out_norm_g, v_ssm_out_norm_g, v_w_out, v_norm2_g, v_w_mlp_up, v_w_mlp_down):
    return _train_step(dict(locals()))
```

```python
import functools
import math

import jax
import jax.numpy as jnp
from jax import lax
from jax.experimental import pallas as pl
from jax.experimental.pallas import tpu as pltpu

F32 = jnp.float32
BF16 = jnp.bfloat16
MESH = pl.DeviceIdType.MESH

D_MODEL = 1024
ATTN_W = 512
SSM_W = 512
HEAD = 64
D_FF = 4096
PROJ_W = 2048
N_GROUPS = 32
N_STATE = 64
GROUP_W = 16
EPS = 1e-6
NEG = -1e30
DILATIONS = (1, 4, 16)
BLK = 128
TILE = 2048
LANES = 128
MXU_WIDTH = 256
N_LB = SSM_W // LANES
VMEM_LIMIT = 56 * 1024 * 1024

ADAM_LR, ADAM_B1, ADAM_B2, ADAM_EPS, ADAM_WD, ADAM_STEP = 0.001, 0.9, 0.999, 1e-08, 0.01, 10


def _params(*sem):
    return pltpu.CompilerParams(dimension_semantics=sem, vmem_limit_bytes=VMEM_LIMIT)


def _nt(a, b):
    return lax.dot_general(a, b, (((1,), (1,)), ((), ())), preferred_element_type=F32)


def _tn(a, b):
    return lax.dot_general(a, b, (((0,), (0,)), ((), ())), preferred_element_type=F32)


def _mm(a, b):
    return jnp.dot(a, b, preferred_element_type=F32)


def _group_mean(t, ones_bd, width):
    span = ones_bd.shape[0]
    hi = t.astype(BF16)
    lo = (t - hi.astype(F32)).astype(BF16)
    parts = [_mm(hi[:, k:k + span], ones_bd) + _mm(lo[:, k:k + span], ones_bd) for k in range(0, t.shape[1], span)]
    return (parts[0] if len(parts) == 1 else jnp.concatenate(parts, axis=1)) * (1.0 / width)


def _rms(x):
    return lax.rsqrt(jnp.mean(x * x, axis=-1, keepdims=True) + EPS)


def _rms_bwd(dy, x, r, g):
    xh = x * r
    dxh = dy * g
    dx = r * (dxh - xh * jnp.mean(dxh * xh, axis=-1, keepdims=True))
    return dx, dy * xh


def _colsum(x):
    return jnp.sum(x, axis=0, keepdims=True)


def _row_block(rows, cap):
    for b in range(min(rows, cap) // 8 * 8, 0, -8):
        if rows % b == 0:
            return b
    raise ValueError(f"no row block for {rows} rows")


def _inproj_fwd(x, g1, wi, gq, gk, ones64):
    t_len = x.shape[0]
    tm = 1024
    n_hp = ATTN_W // LANES

    def body(x_ref, g1_ref, wi_ref, gq_ref, gk_ref, bd_ref, xn_ref, q_ref, k_ref, v_ref, u_ref, qr_ref, kr_ref):
        xv = x_ref[...]
        xn = (xv * _rms(xv) * g1_ref[...]).astype(BF16)
        xn_ref[...] = xn
        proj = _mm(xn, wi_ref[...])
        q = proj[:, 0:ATTN_W]
        k = proj[:, ATTN_W:2 * ATTN_W]
        v = proj[:, 2 * ATTN_W:3 * ATTN_W]
        u_ref[...] = proj[:, 3 * ATTN_W:]
        qr_ref[...] = q
        kr_ref[...] = k
        bd = bd_ref[...]
        qn = q * lax.rsqrt(_group_mean(q * q, bd, HEAD) + EPS) * gq_ref[...] * (HEAD ** -0.5)
        kn = k * lax.rsqrt(_group_mean(k * k, bd, HEAD) + EPS) * gk_ref[...]
        for hp in range(n_hp):
            sl = slice(hp * LANES, (hp + 1) * LANES)
            q_ref[hp] = qn[:, sl]
            k_ref[hp] = kn[:, sl]
            v_ref[hp] = v[:, sl]

    row = lambda i: (i, 0)
    const = lambda i: (0, 0)
    hp_spec = pl.BlockSpec((n_hp, tm, LANES), lambda i: (0, i, 0))
    hp_shape = jax.ShapeDtypeStruct((n_hp, t_len, LANES), F32)
    return pl.pallas_call(
        body, name="inproj_fwd", grid=(t_len // tm,),
        in_specs=[pl.BlockSpec((tm, D_MODEL), row), pl.BlockSpec((1, D_MODEL), const),
                  pl.BlockSpec((D_MODEL, PROJ_W), const), pl.BlockSpec((1, ATTN_W), const),
                  pl.BlockSpec((1, ATTN_W), const), pl.BlockSpec(ones64.shape, const)],
        out_specs=[pl.BlockSpec((tm, D_MODEL), row), hp_spec, hp_spec, hp_spec,
                   pl.BlockSpec((tm, SSM_W), row), pl.BlockSpec((tm, ATTN_W), row), pl.BlockSpec((tm, ATTN_W), row)],
        out_shape=[jax.ShapeDtypeStruct((t_len, D_MODEL), BF16), hp_shape, hp_shape, hp_shape,
                   jax.ShapeDtypeStruct((t_len, SSM_W), F32), jax.ShapeDtypeStruct((t_len, ATTN_W), F32),
                   jax.ShapeDtypeStruct((t_len, ATTN_W), F32)],
        compiler_params=_params("arbitrary"),
    )(x, g1, wi, gq, gk, ones64)


def _attn_masks():
    head0 = lax.broadcasted_iota(jnp.int32, (BLK, LANES), 1) < HEAD
    row = lax.broadcasted_iota(jnp.int32, (2 * BLK, 2 * BLK), 0) & (BLK - 1)
    col = lax.broadcasted_iota(jnp.int32, (2 * BLK, 2 * BLK), 1)
    return head0, (col < BLK) & (col >= row), (col >= BLK) & (col - BLK <= row)


def _stack_heads(x, head0):
    return jnp.concatenate([jnp.where(head0, x, 0.0), jnp.where(head0, 0.0, x)], axis=0).astype(BF16)


def _unit_rows(uidx, d):
    nb = TILE // (BLK * d)
    r = lax.div(uidx, nb)
    b = lax.rem(uidx, nb)
    start = r + d * BLK * b
    if d == 1:
        start = pl.multiple_of(start, BLK)
        mk = lambda s: pl.ds(pl.multiple_of(s, BLK), BLK)
    else:
        mk = lambda s: pl.ds(s, BLK, stride=d)
    return b, mk(start), mk(TILE + start), mk(TILE + start - d * BLK)


def _attn_fwd(q, k, v, side=None, side_args=()):
    n_hp, t_len, _ = q.shape
    nt = t_len // TILE
    ns = side.n if side is not None else 0
    n_steps = n_hp * nt

    def body(*refs):
        q_ref, kp_ref, kc_ref, vp_ref, vc_ref = refs[0:5]
        o_ref, lse_ref = refs[5 + ns:7 + ns]
        kk, vv, m_s, l_s, acc_s = refs[7 + 2 * ns:12 + 2 * ns]
        t = pl.program_id(1)
        step = pl.program_id(0) * nt + t
        if side is not None:
            side.bind(refs[5:5 + ns], refs[7 + ns:7 + 2 * ns], refs[12 + 2 * ns:])
            pl.when(step == 0)(side.start)
            pl.when(step == n_steps // 2)(side.forward)
        kk[0:TILE] = kp_ref[0]
        kk[TILE:] = kc_ref[0]
        vv[0:TILE] = vp_ref[0]
        vv[TILE:] = vc_ref[0]
        head0, band_prev, band_cur = _attn_masks()

        for pi, d in enumerate(DILATIONS):
            def unit(uidx, carry, d=d, pi=pi):
                b, rows_q, rows_c, rows_p = _unit_rows(uidx, d)
                mask = band_cur | (band_prev & ((t > 0) | (b > 0)))
                q2 = _stack_heads(q_ref.at[0][rows_q, :], head0)
                kcat = jnp.concatenate([kk[rows_p, :], kk[rows_c, :]], axis=0).astype(BF16)
                vcat = jnp.concatenate([vv[rows_p, :], vv[rows_c, :]], axis=0).astype(BF16)
                s = jnp.where(mask, _nt(q2, kcat), NEG)
                m = jnp.max(s, axis=1, keepdims=True)
                p = jnp.exp(s - m)
                ls = jnp.sum(p, axis=1, keepdims=True)
                pv = _mm(p.astype(BF16), vcat)
                m_s.at[pi][rows_q, :] = jnp.where(head0, m[0:BLK], m[BLK:])
                l_s.at[pi][rows_q, :] = jnp.where(head0, ls[0:BLK], ls[BLK:])
                acc_s.at[pi][rows_q, :] = jnp.where(head0, pv[0:BLK], pv[BLK:])
                return carry

            lax.fori_loop(0, TILE // BLK, unit, 0, unroll=16)

        m_all = jnp.maximum(jnp.maximum(m_s[0], m_s[1]), m_s[2])
        num = jnp.zeros((TILE, LANES), F32)
        den = jnp.zeros((TILE, LANES), F32)
        for pi in range(len(DILATIONS)):
            wgt = jnp.exp(m_s[pi] - m_all)
            num = num + acc_s[pi] * wgt
            den = den + l_s[pi] * wgt
        o_ref[...] = num / den
        lse_ref[0] = m_all + jnp.log(den)
        if side is not None:
            pl.when(step == n_steps - 1)(side.finish)

    cur = lambda hp, t: (hp, t, 0)
    prev = lambda hp, t: (hp, jnp.maximum(t - 1, 0), 0)
    blk = (1, TILE, LANES)
    per_pattern = pltpu.VMEM((len(DILATIONS), TILE, LANES), F32)
    extra = (side.in_specs(), side.out_specs(), side.out_shapes(), side.scratch_shapes()) if side is not None else ([], [], [], [])
    return pl.pallas_call(
        body, name="attn_fwd", grid=(n_hp, nt),
        in_specs=[pl.BlockSpec(blk, cur), pl.BlockSpec(blk, prev), pl.BlockSpec(blk, cur),
                  pl.BlockSpec(blk, prev), pl.BlockSpec(blk, cur)] + extra[0],
        out_specs=[pl.BlockSpec((TILE, LANES), lambda hp, t: (t, hp)), pl.BlockSpec(blk, cur)] + extra[1],
        out_shape=[jax.ShapeDtypeStruct((t_len, ATTN_W), F32), jax.ShapeDtypeStruct((n_hp, t_len, LANES), F32)] + extra[2],
        scratch_shapes=[pltpu.VMEM((2 * TILE, LANES), F32), pltpu.VMEM((2 * TILE, LANES), F32),
                        per_pattern, per_pattern, per_pattern] + extra[3],
        compiler_params=_params("arbitrary", "arbitrary"),
    )(q, k, k, v, v, *side_args)


def _attn_bwd(q, k, v, o, do, lse, ones_hp, side=None, side_args=()):
    n_hp, t_len, _ = q.shape
    nt = t_len // TILE
    ns = side.n if side is not None else 0
    n_pat = len(DILATIONS)

    def body(*refs):
        q_ref, kp_ref, kc_ref, vp_ref, vc_ref, o_ref, do_ref, lse_ref, bd_ref = refs[0:9]
        dq_ref, dk_ref, dv_ref = refs[9 + ns:12 + ns]
        kk, vv, dq_s, dkc, dkp, dvc, dvp, hold_k, hold_v, dl_s = refs[12 + 2 * ns:22 + 2 * ns]
        t = pl.program_id(1)
        if side is not None:
            side.bind(refs[9:9 + ns], refs[12 + ns:12 + 2 * ns], refs[22 + 2 * ns:])
            pl.when((pl.program_id(0) == 0) & (t == 0))(side.start)

        @pl.when(t < nt)
        def _():
            kk[0:TILE] = kp_ref[0]
            kk[TILE:] = kc_ref[0]
            vv[0:TILE] = vp_ref[0]
            vv[TILE:] = vc_ref[0]
            dl_s[...] = _group_mean(do_ref[...] * o_ref[...], bd_ref[...], 1.0)
            head0, band_prev, band_cur = _attn_masks()

            for pi, d in enumerate(DILATIONS):
                def unit(uidx, carry, d=d, pi=pi):
                    b, rows_q, rows_c, rows_p = _unit_rows(uidx, d)
                    mask = band_cur | (band_prev & ((t > 0) | (b > 0)))
                    q2 = _stack_heads(q_ref.at[0][rows_q, :], head0)
                    do2 = _stack_heads(do_ref[rows_q, :], head0)
                    lse_f = lse_ref.at[0][rows_q, :]
                    dl_f = dl_s[rows_q, :]
                    lse2 = jnp.concatenate([lse_f[:, 0:1], lse_f[:, HEAD:HEAD + 1]], axis=0)
                    dl2 = jnp.concatenate([dl_f[:, 0:1], dl_f[:, HEAD:HEAD + 1]], axis=0)
                    kcat = jnp.concatenate([kk[rows_p, :], kk[rows_c, :]], axis=0).astype(BF16)
                    vcat = jnp.concatenate([vv[rows_p, :], vv[rows_c, :]], axis=0).astype(BF16)
                    p = jnp.where(mask, jnp.exp(_nt(q2, kcat) - lse2), 0.0)
                    ds = (p * (_nt(do2, vcat) - dl2)).astype(BF16)
                    dq2 = _mm(ds, kcat)
                    dq_s.at[pi][rows_q, :] = jnp.where(head0, dq2[0:BLK], dq2[BLK:])
                    dk2 = _tn(ds, q2)
                    dv2 = _tn(p.astype(BF16), do2)
                    dkp.at[pi][rows_q, :] = dk2[0:BLK]
                    dkc.at[pi][rows_q, :] = dk2[BLK:]
                    dvp.at[pi][rows_q, :] = dv2[0:BLK]
                    dvc.at[pi][rows_q, :] = dv2[BLK:]
                    return carry

                lax.fori_loop(0, TILE // BLK, unit, 0, unroll=16)

            dq_ref[...] = dq_s[0] + dq_s[1] + dq_s[2]

        @pl.when(t > 0)
        def _():
            dk_ref[...] = hold_k[...]
            dv_ref[...] = hold_v[...]

        @pl.when((t > 0) & (t < nt))
        def _():
            for pi, d in enumerate(DILATIONS):
                back = d * BLK
                dk_ref[TILE - back:, :] = dk_ref[TILE - back:, :] + dkp[pi, 0:back, :]
                dv_ref[TILE - back:, :] = dv_ref[TILE - back:, :] + dvp[pi, 0:back, :]

        @pl.when(t < nt)
        def _():
            hold_k[...] = dkc[0] + dkc[1] + dkc[2]
            hold_v[...] = dvc[0] + dvc[1] + dvc[2]
            for pi, d in enumerate(DILATIONS):
                back = d * BLK
                if back < TILE:
                    hold_k[0:TILE - back, :] = hold_k[0:TILE - back, :] + dkp[pi, back:, :]
                    hold_v[0:TILE - back, :] = hold_v[0:TILE - back, :] + dvp[pi, back:, :]

        if side is not None:
            pl.when((pl.program_id(0) == n_hp - 1) & (t == nt))(side.finish)

    last = nt - 1
    extra = (side.in_specs(), side.out_specs(), side.out_shapes(), side.scratch_shapes()) if side is not None else ([], [], [], [])
    cur = lambda hp, t: (hp, jnp.minimum(t, last), 0)
    prev = lambda hp, t: (hp, jnp.clip(t - 1, 0, last), 0)
    cur2 = lambda hp, t: (jnp.minimum(t, last), hp)
    prev2 = lambda hp, t: (jnp.maximum(t - 1, 0), hp)
    blk = (1, TILE, LANES)
    blk2 = (TILE, LANES)
    out = jax.ShapeDtypeStruct((t_len, ATTN_W), F32)
    return pl.pallas_call(
        body, name="attn_bwd", grid=(n_hp, nt + 1),
        in_specs=[pl.BlockSpec(blk, cur), pl.BlockSpec(blk, prev), pl.BlockSpec(blk, cur),
                  pl.BlockSpec(blk, prev), pl.BlockSpec(blk, cur), pl.BlockSpec(blk2, cur2),
                  pl.BlockSpec(blk2, cur2), pl.BlockSpec(blk, cur), pl.BlockSpec((LANES, LANES), lambda hp, t: (0, 0))]
        + extra[0],
        out_specs=[pl.BlockSpec(blk2, cur2), pl.BlockSpec(blk2, prev2), pl.BlockSpec(blk2, prev2)] + extra[1],
        out_shape=[out, out, out] + extra[2],
        scratch_shapes=[pltpu.VMEM((2 * TILE, LANES), F32), pltpu.VMEM((2 * TILE, LANES), F32)]
        + [pltpu.VMEM((n_pat, TILE, LANES), F32)] * 5 + [pltpu.VMEM((TILE, LANES), F32)] * 3 + extra[3],
        compiler_params=_params("arbitrary", "arbitrary"),
    )(q, k, k, v, v, o, do, lse, ones_hp, *side_args)


def _discretise(lr, li, ldt, br, bi):
    dt = jnp.exp(ldt)
    mag = jnp.exp(lr * dt)
    ab_r, ab_i = mag * jnp.cos(li * dt), mag * jnp.sin(li * dt)
    den = lr * lr + li * li
    nr, ni = ab_r - 1.0, ab_i
    cr = (nr * lr + ni * li) / den
    ci = (ni * lr - nr * li) / den
    return ab_r, ab_i, cr * br - ci * bi, cr * bi + ci * br


def _disc_fwd(lr, li, ldt, br, bi):
    def body(lr_ref, li_ref, ldt_ref, br_ref, bi_ref, ar_o, ai_o, bbr_o, bbi_o):
        outs = _discretise(lr_ref[...], li_ref[...], ldt_ref[...], br_ref[...], bi_ref[...])
        for o_ref, val in zip((ar_o, ai_o, bbr_o, bbi_o), outs):
            o_ref[...] = val

    col = jax.ShapeDtypeStruct(lr.shape, F32)
    mat = jax.ShapeDtypeStruct(br.shape, F32)
    return pl.pallas_call(body, name="s5_disc_fwd", out_shape=[col, col, mat, mat])(lr, li, ldt, br, bi)


def _disc_bwd(lr, li, ldt, br, bi, d_ar, d_ai, d_bbr, d_bbi, group_sum):
    def body(lr_ref, li_ref, ldt_ref, br_ref, bi_ref, c1, c2, c3, c4, gs_ref, dlr_o, dli_o, dldt_o, dbr_o, dbi_o):
        _, vjp = jax.vjp(_discretise, lr_ref[...], li_ref[...], ldt_ref[...], br_ref[...], bi_ref[...])
        dlr, dli, dldt, dbr, dbi = vjp((c1[...], c2[...], c3[...], c4[...]))
        dlr_o[...] = dlr
        dli_o[...] = dli
        dbr_o[...] = dbr
        dbi_o[...] = dbi
        wide = jnp.broadcast_to(dldt, (dldt.shape[0], LANES))
        dldt_o[...] = jnp.dot(gs_ref[...], wide, precision=lax.Precision.HIGHEST, preferred_element_type=F32)

    col = jax.ShapeDtypeStruct(lr.shape, F32)
    mat = jax.ShapeDtypeStruct(br.shape, F32)
    return pl.pallas_call(
        body, name="s5_disc_bwd", out_shape=[col, col, jax.ShapeDtypeStruct((N_GROUPS, LANES), F32), mat, mat],
    )(lr, li, ldt, br, bi, d_ar, d_ai, d_bbr, d_bbi, group_sum)


N_CHUNK = TILE // BLK
HALF = 4


def _cmul(ar, ai, xr, xi):
    return ar * xr - ai * xi, ar * xi + ai * xr


def _power_table(a_ref, tab, sign, reverse):
    ar = [a_ref[0, j:j + 1, :] for j in range(HALF)]
    ai = [sign * a_ref[0, HALF + j:HALF + j + 1, :] for j in range(HALF)]

    def step(s, cur):
        row = pl.ds((BLK - 1 - s) if reverse else s, 1)
        nxt = []
        for j in range(HALF):
            tab.at[j][row, :] = cur[j]
            tab.at[HALF + j][row, :] = cur[HALF + j]
            nxt.append(_cmul(ar[j], ai[j], cur[j], cur[HALF + j]))
        return tuple(p[0] for p in nxt) + tuple(p[1] for p in nxt)

    lax.fori_loop(0, BLK, step, tuple(ar) + tuple(ai))


def _interleave(src, dst):
    for c in range(N_CHUNK):
        dst[pl.ds(c, BLK, stride=N_CHUNK), :] = src[c * BLK:(c + 1) * BLK, :]


def _deinterleave(src, dst):
    for c in range(N_CHUNK):
        dst[c * BLK:(c + 1) * BLK, :] = src[pl.ds(c, BLK, stride=N_CHUNK), :]


def _step_rows(s):
    return pl.ds(pl.multiple_of(s * N_CHUNK, N_CHUNK), N_CHUNK)


def _chunk_scan(buf, a_ref, sign, reverse):
    ar = [jnp.broadcast_to(a_ref[0, j:j + 1, :], (N_CHUNK, LANES)) for j in range(HALF)]
    ai = [sign * jnp.broadcast_to(a_ref[0, HALF + j:HALF + j + 1, :], (N_CHUNK, LANES)) for j in range(HALF)]

    def step(i, carry):
        s = (BLK - 1 - i) if reverse else i
        rows = _step_rows(s)
        out = []
        for j in range(HALF):
            pr, pi = _cmul(ar[j], ai[j], carry[j], carry[HALF + j])
            xr = buf.at[j][rows, :] + pr
            xi = buf.at[HALF + j][rows, :] + pi
            buf.at[j][rows, :] = xr
            buf.at[HALF + j][rows, :] = xi
            out.append((xr, xi))
        return tuple(p[0] for p in out) + tuple(p[1] for p in out)

    zero = jnp.zeros((N_CHUNK, LANES), F32)
    lax.fori_loop(0, BLK, step, (zero,) * (2 * HALF), unroll=2)


def _chunk_states(buf, carry_s, xin_s, tab, reverse):
    edge = 0 if reverse else BLK - 1
    top = 0 if reverse else BLK - 1
    pw = [tab[j, top:top + 1, :] for j in range(2 * HALF)]
    cur = [carry_s[j:j + 1, :] for j in range(2 * HALF)]
    summary = [buf[j, edge * N_CHUNK:(edge + 1) * N_CHUNK, :] for j in range(2 * HALF)]
    order = range(N_CHUNK - 1, -1, -1) if reverse else range(N_CHUNK)
    for c in order:
        for j in range(2 * HALF):
            xin_s[j, c:c + 1, :] = cur[j]
        nxt = []
        for j in range(HALF):
            pr, pi = _cmul(pw[j], pw[HALF + j], cur[j], cur[HALF + j])
            nxt.append((pr + summary[j][c:c + 1, :], pi + summary[HALF + j][c:c + 1, :]))
        cur = [p[0] for p in nxt] + [p[1] for p in nxt]
    for j in range(2 * HALF):
        carry_s[j:j + 1, :] = cur[j]


def _s5_fwd(u, a_cat, b_mat, c_mat, d_skip):
    t_len = u.shape[0]
    nt = t_len // TILE

    def body(u_ref, a_ref, b_ref, c_ref, d_ref, y_ref, x_ref, xs, us, tab, carry_s, xin_s):
        sb = pl.program_id(1)

        @pl.when(sb == 0)
        def _():
            _power_table(a_ref, tab, 1.0, False)
            carry_s[...] = jnp.zeros_like(carry_s)

        _interleave(u_ref, us)
        uv = us[...]
        bu = _mm(uv.astype(BF16), b_ref[0])
        for j in range(2 * HALF):
            xs[j] = bu[:, j * LANES:(j + 1) * LANES]
        _chunk_scan(xs, a_ref, 1.0, False)
        _chunk_states(xs, carry_s, xin_s, tab, False)
        xin = [xin_s[j] for j in range(2 * HALF)]

        def fix(s, acc):
            rows = _step_rows(s)
            for j in range(HALF):
                pr, pi = _cmul(tab.at[j][pl.ds(s, 1), :], tab.at[HALF + j][pl.ds(s, 1), :], xin[j], xin[HALF + j])
                xs.at[j][rows, :] = xs.at[j][rows, :] + pr
                xs.at[HALF + j][rows, :] = xs.at[HALF + j][rows, :] + pi
            return acc

        lax.fori_loop(0, BLK, fix, 0, unroll=2)
        xcat = jnp.concatenate([xs[j].astype(BF16) for j in range(2 * HALF)], axis=1)
        x_ref[0] = xcat
        us[...] = d_ref[0] * uv + _mm(xcat, c_ref[0])
        _deinterleave(us, y_ref)

    return pl.pallas_call(
        body, name="s5_fwd", grid=(N_LB, nt),
        in_specs=[pl.BlockSpec((TILE, LANES), lambda lb, sb: (sb, lb)),
                  pl.BlockSpec((1, 2 * HALF, LANES), lambda lb, sb: (lb, 0, 0)),
                  pl.BlockSpec((1, LANES, 2 * HALF * LANES), lambda lb, sb: (lb, 0, 0)),
                  pl.BlockSpec((1, 2 * HALF * LANES, LANES), lambda lb, sb: (lb, 0, 0)),
                  pl.BlockSpec((1, 1, LANES), lambda lb, sb: (lb, 0, 0))],
        out_specs=[pl.BlockSpec((TILE, LANES), lambda lb, sb: (sb, lb)),
                   pl.BlockSpec((1, TILE, 2 * HALF * LANES), lambda lb, sb: (lb, sb, 0))],
        out_shape=[jax.ShapeDtypeStruct((t_len, SSM_W), F32), jax.ShapeDtypeStruct((N_LB, t_len, 2 * HALF * LANES), BF16)],
        scratch_shapes=[pltpu.VMEM((2 * HALF, TILE, LANES), F32), pltpu.VMEM((TILE, LANES), F32),
                        pltpu.VMEM((2 * HALF, BLK, LANES), F32),
                        pltpu.VMEM((2 * HALF, LANES), F32), pltpu.VMEM((2 * HALF, N_CHUNK, LANES), F32)],
        compiler_params=_params("arbitrary", "arbitrary"),
    )(u, a_cat, b_mat, c_mat, d_skip)


def _s5_bwd(u, dy, states, a_cat, b_mat, c_mat, d_skip):
    t_len = u.shape[0]
    nt = t_len // TILE
    last = nt - 1

    def body(u_ref, dy_ref, x_ref, a_ref, b_ref, c_ref, d_ref, du_ref, db_ref, dc_ref, da_ref, dd_ref,
             gs, us, dys, tabc, lam_s, lin_s):
        sb = pl.program_id(1)

        @pl.when(sb == 0)
        def _():
            _power_table(a_ref, tabc, -1.0, True)
            lam_s[...] = jnp.zeros_like(lam_s)
            db_ref[...] = jnp.zeros_like(db_ref)
            dc_ref[...] = jnp.zeros_like(dc_ref)
            da_ref[...] = jnp.zeros_like(da_ref)
            dd_ref[...] = jnp.zeros_like(dd_ref)

        _interleave(u_ref, us)
        _interleave(dy_ref, dys)
        uv = us[...]
        dyv = dys[...]
        ub = uv.astype(BF16)
        dyb = dyv.astype(BF16)
        gy = _nt(dyb, c_ref[0])
        for j in range(2 * HALF):
            gs[j] = gy[:, j * LANES:(j + 1) * LANES]
        _chunk_scan(gs, a_ref, -1.0, True)
        _chunk_states(gs, lam_s, lin_s, tabc, True)
        zero = jnp.zeros((N_CHUNK, LANES), F32)
        x_tile = x_ref.at[0]
        for grp in range(0, HALF, 2):
            slabs = (grp, grp + 1)
            lin = [(lin_s[j], lin_s[HALF + j]) for j in slabs]

            def fix(i, carry, slabs=slabs, lin=lin):
                s = BLK - 1 - i
                rows = _step_rows(s)
                out = []
                for k, j in enumerate(slabs):
                    nr, ni, acc_r, acc_i = carry[4 * k:4 * k + 4]
                    xr = x_tile[rows, pl.ds(j * LANES, LANES)].astype(F32)
                    xi = x_tile[rows, pl.ds((HALF + j) * LANES, LANES)].astype(F32)
                    qr, qi = _cmul(tabc.at[j][pl.ds(s, 1), :], tabc.at[HALF + j][pl.ds(s, 1), :], lin[k][0], lin[k][1])
                    lr_ = gs.at[j][rows, :] + qr
                    li_ = gs.at[HALF + j][rows, :] + qi
                    gs.at[j][rows, :] = lr_
                    gs.at[HALF + j][rows, :] = li_
                    out += [lr_, li_, acc_r + (xr * nr + xi * ni), acc_i + (xr * ni - xi * nr)]
                return tuple(out)

            init = []
            for k in range(len(slabs)):
                init += [lin[k][0], lin[k][1], zero, zero]
            res = lax.fori_loop(0, BLK, fix, tuple(init), unroll=2)
            for k, j in enumerate(slabs):
                da_ref[0, j:j + 1, :] = da_ref[0, j:j + 1, :] + _colsum(res[4 * k + 2])
                da_ref[0, HALF + j:HALF + j + 1, :] = da_ref[0, HALF + j:HALF + j + 1, :] + _colsum(res[4 * k + 3])
        lam = jnp.concatenate([gs[j].astype(BF16) for j in range(2 * HALF)], axis=1)
        us[...] = _nt(lam, b_ref[0]) + d_ref[0] * dyv
        _deinterleave(us, du_ref)
        db_ref[0] = db_ref[0] + _tn(ub, lam)
        dc_ref[0] = dc_ref[0] + _tn(dyb, x_ref[0])
        dd_ref[0] = dd_ref[0] + _colsum(dyv * uv)

    rev = lambda lb, sb: (last - sb, lb)
    per_lb = lambda lb, sb: (lb, 0, 0)
    wide = 2 * HALF * LANES
    return pl.pallas_call(
        body, name="s5_bwd", grid=(N_LB, nt),
        in_specs=[pl.BlockSpec((TILE, LANES), rev), pl.BlockSpec((TILE, LANES), rev),
                  pl.BlockSpec((1, TILE, wide), lambda lb, sb: (lb, last - sb, 0)),
                  pl.BlockSpec((1, 2 * HALF, LANES), per_lb), pl.BlockSpec((1, LANES, wide), per_lb),
                  pl.BlockSpec((1, wide, LANES), per_lb), pl.BlockSpec((1, 1, LANES), per_lb)],
        out_specs=[pl.BlockSpec((TILE, LANES), rev), pl.BlockSpec((1, LANES, wide), per_lb),
                   pl.BlockSpec((1, LANES, wide), per_lb), pl.BlockSpec((1, 2 * HALF, LANES), per_lb),
                   pl.BlockSpec((1, 1, LANES), per_lb)],
        out_shape=[jax.ShapeDtypeStruct((t_len, SSM_W), F32), jax.ShapeDtypeStruct((N_LB, LANES, wide), F32),
                   jax.ShapeDtypeStruct((N_LB, LANES, wide), F32), jax.ShapeDtypeStruct((N_LB, 2 * HALF, LANES), F32),
                   jax.ShapeDtypeStruct((N_LB, 1, LANES), F32)],
        scratch_shapes=[pltpu.VMEM((2 * HALF, TILE, LANES), F32),
                        pltpu.VMEM((TILE, LANES), F32), pltpu.VMEM((TILE, LANES), F32),
                        pltpu.VMEM((2 * HALF, BLK, LANES), F32), pltpu.VMEM((2 * HALF, LANES), F32),
                        pltpu.VMEM((2 * HALF, N_CHUNK, LANES), F32)],
        compiler_params=_params("arbitrary", "arbitrary"),
    )(u, dy, states, a_cat, b_mat, c_mat, d_skip)


_GELU_C = math.sqrt(2.0 / math.pi)
_GELU_K = 0.044715


def _gelu(y):
    t = jnp.tanh(_GELU_C * (y + _GELU_K * (y * y * y)))
    return y * (0.5 * (1.0 + t)), t


def _gelu_grad(y, t):
    return 0.5 * (1.0 + t) + 0.5 * y * (1.0 - t * t) * (_GELU_C * (1.0 + 3.0 * _GELU_K * y * y))


def _glu(y, wg, bias):
    z, t = _gelu(y)
    sg = jax.nn.sigmoid(_mm(z.astype(BF16), wg) + bias)
    return z, t, sg


def _mixer_mlp(attn, y, x, target, wg, glu_b, ga, gs, wo, g2, wu, wd):
    t_len = x.shape[0]
    tm = 256
    fc = 1024
    n_fc = D_FF // fc

    def body(attn_ref, y_ref, x_ref, tg_ref, b_ref, ga_ref, gs_ref, g2_ref, wg_s, wo_s, wu_hbm, wd_hbm,
             dx2_ref, hdn_ref, dup_ref, h_ref, dyb_ref, mix_ref, z_ref, dattn_ref, dy_ref, dx2b_ref, dgp_ref,
             dg2_ref, loss_ref, dga_ref, dgs_ref, db_ref, wu_s, wd_s, relu_s, glu_s, sem):
        @pl.when(pl.program_id(0) == 0)
        def _():
            copies = [pltpu.make_async_copy(src, dst, sem.at[k]) for k, (src, dst) in enumerate(((wu_hbm, wu_s), (wd_hbm, wd_s)))]
            for cp in copies:
                cp.start()
            for cp in copies:
                cp.wait()
            for acc in (dg2_ref, loss_ref, dga_ref, dgs_ref, db_ref):
                acc[...] = jnp.zeros_like(acc)

        av = attn_ref[...]
        z, t, sg = _glu(y_ref[...], wg_s[...], b_ref[...])
        glu_s[0] = z
        glu_s[1] = t
        glu_s[2] = sg
        z_ref[...] = z.astype(BF16)
        s = z * sg
        an = (av * _rms(av) * ga_ref[...]).astype(BF16)
        sn = (s * _rms(s) * gs_ref[...]).astype(BF16)
        mix_ref[:, 0:ATTN_W] = an
        mix_ref[:, ATTN_W:] = sn
        dx2_ref[...] = x_ref[...] + _mm(an, wo_s[0:ATTN_W, :]) + _mm(sn, wo_s[ATTN_W:, :])
        r = _rms(dx2_ref[...])
        g2v = g2_ref[...]
        h = (dx2_ref[...] * r * g2v).astype(BF16)
        h_ref[...] = h
        yout = dx2_ref[...]
        for c in range(n_fc):
            cols = slice(c * fc, (c + 1) * fc)
            ru = jnp.maximum(_mm(h, wu_s[:, cols]), 0.0)
            relu_s[:, cols] = ru
            hd = (ru * ru).astype(BF16)
            hdn_ref[:, cols] = hd
            yout = yout + _mm(hd, wd_s[cols, :])
        err = yout - tg_ref[...]
        loss_ref[...] = loss_ref[...] + 0.5 * jnp.sum(err * err) * (1.0 / D_MODEL)
        dy = err * (1.0 / D_MODEL)
        dyb = dy.astype(BF16)
        dyb_ref[...] = dyb
        dh = jnp.zeros((tm, D_MODEL), F32)
        for c in range(n_fc):
            cols = slice(c * fc, (c + 1) * fc)
            dup = (_nt(dyb, wd_s[cols, :]) * (2.0 * relu_s[:, cols])).astype(BF16)
            dup_ref[:, cols] = dup
            dh = dh + _nt(dup, wu_s[:, cols])
        dxn, g2_term = _rms_bwd(dh, dx2_ref[...], r, g2v)
        dx2_ref[...] = dy + dxn
        dg2_ref[...] = dg2_ref[...] + _colsum(g2_term)
        dx2b = dx2_ref[...].astype(BF16)
        dx2b_ref[...] = dx2b
        yv = y_ref[...]
        av = attn_ref[...]
        z, t, sg = glu_s[0], glu_s[1], glu_s[2]
        s = z * sg
        d_attn, ga_term = _rms_bwd(_nt(dx2b, wo_s[0:ATTN_W, :]), av, _rms(av), ga_ref[...])
        d_s, gs_term = _rms_bwd(_nt(dx2b, wo_s[ATTN_W:, :]), s, _rms(s), gs_ref[...])
        dattn_ref[...] = d_attn
        dgp = d_s * z * sg * (1.0 - sg)
        dgpb = dgp.astype(BF16)
        dgp_ref[...] = dgpb
        dy_ref[...] = (d_s * sg + _nt(dgpb, wg_s[...])) * _gelu_grad(yv, t)
        dga_ref[...] = dga_ref[...] + _colsum(ga_term)
        dgs_ref[...] = dgs_ref[...] + _colsum(gs_term)
        db_ref[...] = db_ref[...] + _colsum(dgp)

    row = lambda i: (i, 0)
    const = lambda i: (0, 0)
    wide = lambda n: pl.BlockSpec((tm, n), row)
    vec = lambda n: pl.BlockSpec((1, n), const)
    any_spec = pl.BlockSpec(memory_space=pl.ANY)
    f32 = lambda n: jax.ShapeDtypeStruct((t_len, n), F32)
    b16 = lambda n: jax.ShapeDtypeStruct((t_len, n), BF16)
    acc = lambda n: jax.ShapeDtypeStruct((1, n), F32)
    return pl.pallas_call(
        body, name="mixer_mlp", grid=(t_len // tm,),
        in_specs=[wide(ATTN_W), wide(SSM_W), wide(D_MODEL), wide(D_MODEL), vec(SSM_W), vec(ATTN_W), vec(SSM_W), vec(D_MODEL),
                  pl.BlockSpec((SSM_W, SSM_W), const), pl.BlockSpec((D_MODEL, D_MODEL), const), any_spec, any_spec],
        out_specs=[wide(D_MODEL), wide(D_FF), wide(D_FF), wide(D_MODEL), wide(D_MODEL), wide(D_MODEL), wide(SSM_W),
                   wide(ATTN_W), wide(SSM_W), wide(D_MODEL), wide(SSM_W),
                   vec(D_MODEL), vec(LANES), vec(ATTN_W), vec(SSM_W), vec(SSM_W)],
        out_shape=[f32(D_MODEL), b16(D_FF), b16(D_FF), b16(D_MODEL), b16(D_MODEL), b16(D_MODEL), b16(SSM_W),
                   f32(ATTN_W), f32(SSM_W), b16(D_MODEL), b16(SSM_W),
                   acc(D_MODEL), acc(LANES), acc(ATTN_W), acc(SSM_W), acc(SSM_W)],
        scratch_shapes=[pltpu.VMEM((D_MODEL, D_FF), BF16), pltpu.VMEM((D_FF, D_MODEL), BF16),
                        pltpu.VMEM((tm, D_FF), F32), pltpu.VMEM((3, tm, SSM_W), F32), pltpu.SemaphoreType.DMA((2,))],
        compiler_params=_params("arbitrary"),
    )(attn, y, x, target, glu_b, ga, gs, g2, wg, wo, wu, wd)


def _inproj_bwd(dqs, dkn, dv, du, q_raw, k_raw, x, dx2, wi, g1, gq, gk, ones64):
    t_len = x.shape[0]
    tm = 512
    n_heads = ATTN_W // HEAD

    def body(dqs_ref, dkn_ref, dv_ref, du_ref, q_ref, k_ref, x_ref, dx2_ref, wi_ref, g1_ref, gq_ref, gk_ref, bd_ref,
             gx_ref, dproj_ref, dg1_ref, dgq_ref, dgk_ref, accq, acck):
        i = pl.program_id(0)

        @pl.when(i == 0)
        def _():
            dg1_ref[...] = jnp.zeros_like(dg1_ref)
            accq[...] = jnp.zeros_like(accq)
            acck[...] = jnp.zeros_like(acck)

        bd = bd_ref[...]

        def head_norm_bwd(dy, raw, gain, acc):
            r = lax.rsqrt(_group_mean(raw * raw, bd, HEAD) + EPS)
            xh = raw * r
            dxh = dy * gain
            acc[...] = acc[...] + _colsum(dy * xh)
            return r * (dxh - xh * _group_mean(dxh * xh, bd, HEAD))

        dq = head_norm_bwd(dqs_ref[...] * (HEAD ** -0.5), q_ref[...], gq_ref[...], accq)
        dk = head_norm_bwd(dkn_ref[...], k_ref[...], gk_ref[...], acck)
        dproj_ref[:, 0:ATTN_W] = dq.astype(BF16)
        dproj_ref[:, ATTN_W:2 * ATTN_W] = dk.astype(BF16)
        dproj_ref[:, 2 * ATTN_W:3 * ATTN_W] = dv_ref[...].astype(BF16)
        dproj_ref[:, 3 * ATTN_W:] = du_ref[...].astype(BF16)
        dxn = _nt(dproj_ref[...], wi_ref[...])
        xv = x_ref[...]
        g1v = g1_ref[...]
        dx, g1_term = _rms_bwd(dxn, xv, _rms(xv), g1v)
        gx_ref[...] = dx2_ref[...] + dx
        dg1_ref[...] = dg1_ref[...] + _colsum(g1_term)

        @pl.when(i == pl.num_programs(0) - 1)
        def _():
            for acc, out in ((accq, dgq_ref), (acck, dgk_ref)):
                tot = acc[:, 0:HEAD]
                for h in range(1, n_heads):
                    tot = tot + acc[:, h * HEAD:(h + 1) * HEAD]
                out[...] = tot

    row = lambda i: (i, 0)
    const = lambda i: (0, 0)
    aw = pl.BlockSpec((tm, ATTN_W), row)
    dm = pl.BlockSpec((tm, D_MODEL), row)
    return pl.pallas_call(
        body, name="inproj_bwd", grid=(t_len // tm,),
        in_specs=[aw, aw, aw, aw, aw, aw, dm, dm, pl.BlockSpec((D_MODEL, PROJ_W), const), pl.BlockSpec((1, D_MODEL), const),
                  pl.BlockSpec((1, ATTN_W), const), pl.BlockSpec((1, ATTN_W), const), pl.BlockSpec(ones64.shape, const)],
        out_specs=[dm, pl.BlockSpec((tm, PROJ_W), row), pl.BlockSpec((1, D_MODEL), const),
                   pl.BlockSpec((1, HEAD), const), pl.BlockSpec((1, HEAD), const)],
        out_shape=[jax.ShapeDtypeStruct((t_len, D_MODEL), F32), jax.ShapeDtypeStruct((t_len, PROJ_W), BF16),
                   jax.ShapeDtypeStruct((1, D_MODEL), F32), jax.ShapeDtypeStruct((1, HEAD), F32),
                   jax.ShapeDtypeStruct((1, HEAD), F32)],
        scratch_shapes=[pltpu.VMEM((1, ATTN_W), F32), pltpu.VMEM((1, ATTN_W), F32)],
        compiler_params=_params("arbitrary"),
    )(dqs, dkn, dv, du, q_raw, k_raw, x, dx2, wi, g1, gq, gk, ones64)


def _grad_matmul(a, b, name, side=None, side_args=()):
    t_len, m = a.shape
    n = b.shape[1]
    bm, bn, bt = min(m, 1024), min(n, 1024), min(t_len, 4096)
    grid = (m // bm, n // bn, t_len // bt)
    ns = side.n if side is not None else 0

    def body(*refs):
        a_ref, b_ref, o_ref = refs[0], refs[1], refs[2 + ns]
        ids = [pl.program_id(k) for k in range(3)]
        if side is not None:
            side.bind(refs[2:2 + ns], refs[3 + ns:3 + 2 * ns], refs[3 + 2 * ns:])
            pl.when((ids[0] == 0) & (ids[1] == 0) & (ids[2] == 0))(side.start)

        @pl.when(ids[2] == 0)
        def _():
            o_ref[...] = jnp.zeros_like(o_ref)

        o_ref[...] = o_ref[...] + _tn(a_ref[...], b_ref[...])
        if side is not None:
            pl.when((ids[0] == grid[0] - 1) & (ids[1] == grid[1] - 1) & (ids[2] == grid[2] - 1))(side.finish)

    extra = (side.in_specs(), side.out_specs(), side.out_shapes(), side.scratch_shapes()) if side is not None else ([], [], [], [])
    out = pl.pallas_call(
        body, name=name, grid=grid,
        in_specs=[pl.BlockSpec((bt, bm), lambda i, j, k: (k, i)), pl.BlockSpec((bt, bn), lambda i, j, k: (k, j))] + extra[0],
        out_specs=[pl.BlockSpec((bm, bn), lambda i, j, k: (i, j))] + extra[1],
        out_shape=[jax.ShapeDtypeStruct((m, n), F32)] + extra[2],
        scratch_shapes=extra[3],
        compiler_params=_params("arbitrary", "arbitrary", "arbitrary"),
    )(a, b, *side_args)
    return out[0] if side is None else out


def _adamw_update(w_ref, g_ref, m_ref, v_ref, d_o, m_o, v_o):
    gv = g_ref[...]
    mn = ADAM_B1 * m_ref[...] + (1.0 - ADAM_B1) * gv
    vn = ADAM_B2 * v_ref[...] + (1.0 - ADAM_B2) * jnp.square(gv)
    m_hat = mn / (1.0 - ADAM_B1 ** ADAM_STEP)
    v_hat = vn / (1.0 - ADAM_B2 ** ADAM_STEP)
    d_o[...] = -ADAM_LR * (m_hat / (jnp.sqrt(v_hat) + ADAM_EPS) + ADAM_WD * w_ref[...])
    m_o[...] = mn
    v_o[...] = vn


def _adamw_many(ws, gs, ms, vs):
    n = len(ws)

    def body(*refs):
        for i in range(n):
            _adamw_update(*[refs[k * n + i] for k in range(7)])

    shapes = [jax.ShapeDtypeStruct(w.shape, F32) for w in ws]
    outs = pl.pallas_call(body, name="adamw_small", out_shape=shapes * 3,
                          compiler_params=pltpu.CompilerParams(vmem_limit_bytes=VMEM_LIMIT))(*ws, *gs, *ms, *vs)
    return outs[0:n], outs[n:2 * n], outs[2 * n:]


def _adamw(w, g, m, v, name):
    rows, cols = w.shape
    br = _row_block(rows, 256)
    body = functools.partial(_adamw_update)

    spec = pl.BlockSpec((br, cols), lambda i: (i, 0))
    shape = jax.ShapeDtypeStruct((rows, cols), F32)
    return pl.pallas_call(
        body, name=name, grid=(rows // br,), in_specs=[spec] * 4, out_specs=[spec] * 3, out_shape=[shape] * 3,
        compiler_params=_params("arbitrary"),
    )(w, g, m, v)


def _sum_arrays(arrs, name, out_dtype=F32):
    rows, cols = arrs[0].shape
    br = _row_block(rows, 512)
    n = len(arrs)

    def body(*refs):
        tot = refs[0][...]
        for r in refs[1:n]:
            tot = tot + r[...]
        refs[n][...] = tot.astype(out_dtype)

    spec = pl.BlockSpec((br, cols), lambda i: (i, 0))
    return pl.pallas_call(
        body, name=name, grid=(rows // br,), in_specs=[spec] * n, out_specs=spec,
        out_shape=jax.ShapeDtypeStruct((rows, cols), out_dtype), compiler_params=_params("arbitrary"),
    )(*arrs)


GPL = N_GROUPS // N_LB
SW = GPL * N_STATE


def _eye_groups():
    return jnp.eye(GPL, dtype=F32)


def _s5_matrices(ab_r, ab_i, bb_r, bb_i, c_re, c_im, d_skip):
    eye = _eye_groups()
    a_cat = jnp.concatenate([ab_r.reshape(N_LB, HALF, LANES), ab_i.reshape(N_LB, HALF, LANES)], axis=1)

    def b_part(bb):
        b4 = jnp.transpose(bb.reshape(N_LB, GPL, N_STATE, GROUP_W), (0, 1, 3, 2))
        return (b4[:, :, :, None, :] * eye[None, :, None, :, None]).reshape(N_LB, LANES, SW)

    def c_part(cc):
        c4 = jnp.transpose(cc.reshape(N_LB, GPL, GROUP_W, N_STATE), (0, 1, 3, 2))
        return (c4[:, :, :, None, :] * eye[None, :, None, :, None]).reshape(N_LB, SW, LANES)

    b_mat = jnp.concatenate([b_part(bb_r), b_part(bb_i)], axis=2).astype(BF16)
    c_mat = jnp.concatenate([c_part(c_re), -c_part(c_im)], axis=1).astype(BF16)
    return a_cat, b_mat, c_mat, d_skip.reshape(N_LB, 1, LANES)


def _s5_unpack_grads(db, dc, da, dd):
    eye = _eye_groups()
    mask = eye[None, :, None, None, :, None]
    d6 = jnp.sum(db.reshape(N_LB, GPL, GROUP_W, 2, GPL, N_STATE) * mask, axis=4)
    dbb = jnp.transpose(d6, (3, 0, 1, 4, 2)).reshape(2, N_GROUPS * N_STATE, GROUP_W)
    c6 = jnp.sum(dc.reshape(N_LB, GPL, GROUP_W, 2, GPL, N_STATE) * mask, axis=4)
    dcc = jnp.transpose(c6, (3, 0, 1, 2, 4)).reshape(2, N_GROUPS, GROUP_W, N_STATE)
    dab_r = da[:, :HALF].reshape(N_GROUPS * N_STATE, 1)
    dab_i = da[:, HALF:].reshape(N_GROUPS * N_STATE, 1)
    return dab_r, dab_i, dbb[0], dbb[1], dcc[0], -dcc[1], dd.reshape(N_GROUPS, GROUP_W)


def _block_ones(n, width):
    i = lax.broadcasted_iota(jnp.int32, (n, n), 0) // width
    j = lax.broadcasted_iota(jnp.int32, (n, n), 1) // width
    return (i == j).astype(BF16)


def _tile_heads(g):
    return jnp.tile(g.reshape(1, HEAD), (1, ATTN_W // HEAD))


def _local_step(x, target, wi, rest, p, fwd_side=None, bwd_side=None, last_side=None):
    ones64 = _block_ones(MXU_WIDTH, HEAD)
    ones_hp = _block_ones(LANES, HEAD)
    g1 = p["norm1_g"].reshape(1, D_MODEL)
    g2 = p["norm2_g"].reshape(1, D_MODEL)
    gq = _tile_heads(p["q_norm_g"])
    gk = _tile_heads(p["k_norm_g"])
    ga = p["attn_out_norm_g"].reshape(1, ATTN_W)
    gs = p["ssm_out_norm_g"].reshape(1, SSM_W)
    glu_b = p["glu_b"].reshape(1, SSM_W)
    n_gp = N_GROUPS * N_STATE
    lr = p["ssm_a_re"].reshape(n_gp, 1)
    li = p["ssm_a_im"].reshape(n_gp, 1)
    ldt = jnp.repeat(p["ssm_log_dt"].reshape(N_GROUPS), N_STATE).reshape(n_gp, 1)
    br = p["ssm_b_re"].reshape(n_gp, GROUP_W)
    bi = p["ssm_b_im"].reshape(n_gp, GROUP_W)
    ab_r, ab_i, bb_r, bb_i = _disc_fwd(lr, li, ldt, br, bi)
    a_cat, b_mat, c_mat, d_mat = _s5_matrices(
        ab_r, ab_i, bb_r, bb_i, p["ssm_c_re"].reshape(N_GROUPS, GROUP_W, N_STATE),
        p["ssm_c_im"].reshape(N_GROUPS, GROUP_W, N_STATE), p["ssm_d"])

    xn, qn, kn, vv, u, q_raw, k_raw = _inproj_fwd(x, g1, wi, gq, gk, ones64)
    if fwd_side is None:
        attn, lse = _attn_fwd(qn, kn, vv)
    else:
        attn, lse, *rest = _attn_fwd(qn, kn, vv, *fwd_side)
    wg, wo, wu, wd = rest
    y, states = _s5_fwd(u, a_cat, b_mat, c_mat, d_mat)
    (dx2, hdn, dup, h, dyb, mix, z, d_attn, dy_ssm, dx2b, dgp, dg2, loss, dga, dgs, dglu_b) = _mixer_mlp(
        attn, y, x, target, wg, glu_b, ga, gs, wo, g2, wu, wd)
    big = {"w_mlp_up": _grad_matmul(h, dup, "grad_w_mlp_up"), "w_mlp_down": _grad_matmul(hdn, dyb, "grad_w_mlp_down"),
           "w_out": _grad_matmul(mix, dx2b, "grad_w_out"), "glu_w": _grad_matmul(z, dgp, "grad_glu_w")}
    rode = []
    if bwd_side is None:
        dqs, dkn, dvv = _attn_bwd(qn, kn, vv, attn, d_attn, lse, ones_hp)
    else:
        dqs, dkn, dvv, *rode = _attn_bwd(qn, kn, vv, attn, d_attn, lse, ones_hp, *bwd_side(big))
    du, db, dc, da, dd = _s5_bwd(u, dy_ssm, states, a_cat, b_mat, c_mat, d_mat)
    grad_x, dproj, dg1, dgq, dgk = _inproj_bwd(dqs, dkn, dvv, du, q_raw, k_raw, x, dx2, wi, g1, gq, gk, ones64)

    dab_r, dab_i, dbb_r, dbb_i, dc_re, dc_im, dd_g = _s5_unpack_grads(db, dc, da, dd)
    cot = {"norm1_g": dg1, "q_norm_g": dgq, "k_norm_g": dgk, "ab_r": dab_r, "ab_i": dab_i, "bb_r": dbb_r, "bb_i": dbb_i,
           "ssm_c_re": dc_re, "ssm_c_im": dc_im, "ssm_d": dd_g, "glu_b": dglu_b, "attn_out_norm_g": dga,
           "ssm_out_norm_g": dgs, "norm2_g": dg2}
    rode_last = []
    if last_side is None:
        big["w_in"] = _grad_matmul(xn, dproj, "grad_w_in")
    else:
        big["w_in"], *rode_last = _grad_matmul(xn, dproj, "grad_w_in", *last_side(cot))
    return loss[0, 0], grad_x, big, cot, (lr, li, ldt, br, bi), rode, rode_last


COT_NAMES = ("norm1_g", "q_norm_g", "k_norm_g", "ab_r", "ab_i", "bb_r", "bb_i", "ssm_c_re", "ssm_c_im", "ssm_d",
             "glu_b", "attn_out_norm_g", "ssm_out_norm_g", "norm2_g")
SMALL_NAMES = ("norm1_g", "q_norm_g", "k_norm_g", "ssm_a_re", "ssm_a_im", "ssm_log_dt", "ssm_b_re", "ssm_b_im",
               "ssm_c_re", "ssm_c_im", "ssm_d", "glu_b", "attn_out_norm_g", "ssm_out_norm_g", "norm2_g")
BIG_NAMES = ("w_in", "glu_w", "w_out", "w_mlp_up", "w_mlp_down")
PACK_ROWS = 1152


def _pack(arrs):
    flat = jnp.concatenate([a.reshape(-1) for a in arrs])
    return jnp.pad(flat, (0, PACK_ROWS * LANES - flat.shape[0])).reshape(PACK_ROWS, LANES)


def _unpack(packed, like):
    flat = packed.reshape(-1)
    out, pos = [], 0
    for a in like:
        out.append(flat[pos:pos + a.size].reshape(a.shape))
        pos += a.size
    return out


def _small_grads(cot, disc_in, p):
    lr, li, ldt, br, bi = disc_in
    group_sum = (lax.broadcasted_iota(jnp.int32, (N_GROUPS, N_GROUPS * N_STATE), 1) // N_STATE
                 == lax.broadcasted_iota(jnp.int32, (N_GROUPS, N_GROUPS * N_STATE), 0)).astype(F32)
    dlr, dli, dldt, dbr, dbi = _disc_bwd(lr, li, ldt, br, bi, cot["ab_r"], cot["ab_i"], cot["bb_r"], cot["bb_i"], group_sum)
    g = dict(cot)
    g.update(ssm_a_re=dlr, ssm_a_im=dli, ssm_log_dt=dldt[:, 0], ssm_b_re=dbr, ssm_b_im=dbi)
    return {n: g[n].reshape(p[n].shape) for n in SMALL_NAMES}


BIG = {
    "w_in": ((D_MODEL, PROJ_W), 1, PROJ_W // 4, 0, D_MODEL // 2),
    "glu_w": ((SSM_W, SSM_W), 0, SSM_W // 4, 1, SSM_W // 2),
    "w_out": ((D_MODEL, D_MODEL), 0, D_MODEL // 4, 1, D_MODEL // 2),
    "w_mlp_up": ((D_MODEL, D_FF), 1, D_FF // 4, 0, D_MODEL // 2),
    "w_mlp_down": ((D_FF, D_MODEL), 0, D_FF // 4, 1, D_MODEL // 2),
}
N_BIG = len(BIG_NAMES)
N_CHIPS = 4
ANY = pl.BlockSpec(memory_space=pl.ANY)


def _cut(name, shard=False, half=False):
    shape, s_ax, s_sz, h_ax, h_sz = BIG[name]
    shape = list(shape)
    if shard:
        shape[s_ax] = s_sz
    if half:
        shape[h_ax] = h_sz
    return tuple(shape)


def _window(name, base, shard=None, half=None):
    _, s_ax, s_sz, h_ax, h_sz = BIG[name]
    idx = [pl.ds(0, base[0]), pl.ds(0, base[1])]
    if shard is not None:
        idx[s_ax] = pl.ds(pl.multiple_of(shard * s_sz, s_sz), s_sz)
    if half is not None:
        idx[h_ax] = pl.ds(pl.multiple_of(half * h_sz, h_sz), h_sz)
    return tuple(idx)


def _mesh_pos():
    return lax.axis_index("x"), lax.axis_index("y"), lax.axis_index("c")


def _other_chips(x, y):
    return [(1 - x, y, 2 * (1 - x) + y), (x, 1 - y, 2 * x + 1 - y), (1 - x, 1 - y, 2 * (1 - x) + 1 - y)]


def _remote(src, dst, send_sem, recv_sem, dev):
    return pltpu.make_async_remote_copy(src_ref=src, dst_ref=dst, send_sem=send_sem, recv_sem=recv_sem,
                                        device_id=dev, device_id_type=MESH)


def _start_remote(src, dst, send_sem, recv_sem, dev):
    cp = _remote(src, dst, send_sem, recv_sem, dev)
    cp.start()
    return cp


class _Gather:
    def __init__(self, names):
        self.names = tuple(names)
        self.n = len(self.names)

    def in_specs(self):
        return [pl.BlockSpec(memory_space=pltpu.VMEM)] * self.n

    def out_specs(self):
        return [ANY] * self.n

    def out_shapes(self):
        return [jax.ShapeDtypeStruct(BIG[w][0], BF16) for w in self.names]

    def scratch_shapes(self):
        n_sem = (N_CHIPS - 1) * self.n
        return ([pltpu.VMEM(_cut(w, shard=True), BF16) for w in self.names]
                + [pltpu.SemaphoreType.DMA((n_sem,))] * 4 + [pltpu.SemaphoreType.DMA((self.n,))])

    def bind(self, ins, outs, scratch):
        self.ins, self.outs = ins, outs
        self.stage = scratch[:self.n]
        self.send, self.recv, self.fsend, self.frecv, self.lsem = scratch[self.n:]

    def _copies(self):
        x, y, c = _mesh_pos()
        me = 2 * x + y
        sib = (x, y, 1 - c)
        local, sends, lands, fwds, flands = [], [], [], [], []
        for w, n in enumerate(self.names):
            local.append(pltpu.make_async_copy(self.stage[w], self.outs[w].at[_window(n, BIG[n][0], shard=me)], self.lsem.at[w]))
        for k, (px, py, pj) in enumerate(_other_chips(x, y)):
            for w, n in enumerate(self.names):
                s = k * self.n + w
                sends.append(_remote(self.stage[w].at[_window(n, _cut(n, shard=True), half=c)],
                                     self.outs[w].at[_window(n, BIG[n][0], shard=me, half=c)],
                                     self.send.at[s], self.recv.at[s], (px, py, c)))
                got = self.outs[w].at[_window(n, BIG[n][0], shard=pj, half=c)]
                lands.append(_remote(got, got, self.send.at[s], self.recv.at[s], (px, py, c)))
                fwds.append(_remote(got, got, self.fsend.at[s], self.frecv.at[s], sib))
                theirs = self.outs[w].at[_window(n, BIG[n][0], shard=pj, half=1 - c)]
                flands.append(_remote(theirs, theirs, self.fsend.at[s], self.frecv.at[s], sib))
        return local, sends, lands, fwds, flands

    def start(self):
        for w in range(self.n):
            self.stage[w][...] = self.ins[w][...].astype(BF16)
        local, sends, _, _, _ = self._copies()
        for cp in local + sends:
            cp.start()

    def forward(self):
        _, _, lands, fwds, _ = self._copies()
        for land, fwd in zip(lands, fwds):
            land.wait_recv()
            fwd.start()

    def finish(self):
        local, sends, _, fwds, flands = self._copies()
        for cp in flands:
            cp.wait_recv()
        for cp in sends + fwds:
            cp.wait_send()
        for cp in local:
            cp.wait()


def _gather_weights(shards, names):
    g = _Gather(names)

    def body(*refs):
        g.bind(refs[0:g.n], refs[g.n:2 * g.n], refs[2 * g.n:])
        g.start()
        g.forward()
        g.finish()

    return pl.pallas_call(
        body, name="gather_" + "_".join(names), in_specs=g.in_specs(), out_specs=g.out_specs(), out_shape=g.out_shapes(),
        scratch_shapes=g.scratch_shapes(), compiler_params=pltpu.CompilerParams(vmem_limit_bytes=VMEM_LIMIT),
    )(*[shards[n] for n in names])


def _pair_exchange(grads, names, packed=None):
    n_big = len(names)
    n_all = n_big + (packed is not None)

    def body(*refs):
        ins, got = refs[0:n_all], refs[n_all:2 * n_all]
        send, recv = refs[2 * n_all:]
        x, y, c = _mesh_pos()
        sib = (x, y, 1 - c)
        copies = []
        for w, n in enumerate(names):
            copies.append(_start_remote(ins[w].at[_window(n, BIG[n][0], half=1 - c)], got[w], send.at[w], recv.at[w], sib))
        if packed is not None:
            copies.append(_start_remote(ins[n_big], got[n_big], send.at[n_big], recv.at[n_big], sib))
        for cp in copies:
            cp.wait()

    shapes = [jax.ShapeDtypeStruct(_cut(n, half=True), F32) for n in names]
    args = [grads[n] for n in names]
    if packed is not None:
        shapes.append(jax.ShapeDtypeStruct(packed.shape, F32))
        args.append(packed)
    return pl.pallas_call(
        body, name="grad_pair_exchange_" + ("_".join(names) or "small"), in_specs=[ANY] * n_all, out_specs=[ANY] * n_all,
        out_shape=shapes,
        scratch_shapes=[pltpu.SemaphoreType.DMA((n_all,)), pltpu.SemaphoreType.DMA((n_all,))],
    )(*args)


def _pair_sum(name, full, got, core):
    _, _, _, h_ax, _ = BIG[name]
    rows, cols = _cut(name, half=True)
    br = _row_block(rows, 512)
    nb = rows // br
    own_map = (lambda i, c: (i + c[0] * nb, 0)) if h_ax == 0 else (lambda i, c: (i, c[0]))

    def body(c_ref, own_ref, got_ref, o_ref):
        o_ref[...] = (own_ref[...] + got_ref[...]).astype(BF16)

    plain = pl.BlockSpec((br, cols), lambda i, c: (i, 0))
    return pl.pallas_call(
        body, name="pair_sum_" + name,
        grid_spec=pltpu.PrefetchScalarGridSpec(num_scalar_prefetch=1, grid=(nb,),
                                               in_specs=[pl.BlockSpec((br, cols), own_map), plain], out_specs=plain),
        out_shape=jax.ShapeDtypeStruct((rows, cols), BF16), compiler_params=_params("arbitrary"),
    )(core, full, got)


class _ChipExchange:
    def __init__(self, names, packed_shape=None):
        self.names = tuple(names)
        self.packed_shape = packed_shape
        self.n = len(self.names) + (packed_shape is not None)

    def in_specs(self):
        return [ANY] * self.n

    def out_specs(self):
        return [ANY] * self.n

    def out_shapes(self):
        shapes = [jax.ShapeDtypeStruct((N_CHIPS,) + _cut(w, shard=True, half=True), BF16) for w in self.names]
        if self.packed_shape is not None:
            shapes.append(jax.ShapeDtypeStruct((N_CHIPS,) + tuple(self.packed_shape), F32))
        return shapes

    def scratch_shapes(self):
        n_sem = (N_CHIPS - 1) * self.n
        return [pltpu.SemaphoreType.DMA((n_sem,)), pltpu.SemaphoreType.DMA((n_sem,))]

    def bind(self, ins, outs, scratch):
        self.ins, self.outs = ins, outs
        self.send, self.recv = scratch

    def _piece(self, w, shard):
        if w >= len(self.names):
            return self.ins[w]
        n = self.names[w]
        return self.ins[w].at[_window(n, _cut(n, half=True), shard=shard)]

    def _copies(self):
        x, y, c = _mesh_pos()
        me = 2 * x + y
        sends, lands = [], []
        for k, (px, py, pj) in enumerate(_other_chips(x, y)):
            for w in range(self.n):
                s = k * self.n + w
                sends.append(_remote(self._piece(w, pj), self.outs[w].at[me], self.send.at[s], self.recv.at[s], (px, py, c)))
                lands.append(_remote(self._piece(w, me), self.outs[w].at[pj], self.send.at[s], self.recv.at[s], (px, py, c)))
        return sends, lands

    def start(self):
        for cp in self._copies()[0]:
            cp.start()

    def finish(self):
        sends, lands = self._copies()
        for cp in lands:
            cp.wait_recv()
        for cp in sends:
            cp.wait_send()


def _chip_exchange(halves, names):
    ex = _ChipExchange(names)

    def body(*refs):
        ex.bind(refs[0:ex.n], refs[ex.n:2 * ex.n], refs[2 * ex.n:])
        ex.start()
        ex.finish()

    return pl.pallas_call(
        body, name="grad_chip_exchange", in_specs=ex.in_specs(), out_specs=ex.out_specs(), out_shape=ex.out_shapes(),
        scratch_shapes=ex.scratch_shapes(),
    )(*halves)


def _chip_sum(name, own, slots, chip):
    n_slot, rows, cols = slots.shape
    br = _row_block(rows, 512)
    nb = rows // br
    if name in BIG and BIG[name][1] == 1:
        own_map = lambda i, m: (i, m[0])
    elif name in BIG:
        own_map = lambda i, m: (i + m[0] * nb, 0)
    else:
        own_map = lambda i, m: (i, 0)

    def slot_map(j):
        return lambda i, m: (jnp.where(m[0] == j, (j + 1) % n_slot, j), i, 0)

    def body(m_ref, own_ref, *refs):
        own_blk = own_ref[...].astype(F32)
        tot = None
        for j in range(n_slot):
            term = jnp.where(m_ref[0] == j, own_blk, refs[j][...].astype(F32))
            tot = term if tot is None else tot + term
        refs[n_slot][...] = tot

    in_specs = [pl.BlockSpec((br, cols), own_map)] + [pl.BlockSpec((None, br, cols), slot_map(j)) for j in range(n_slot)]
    return pl.pallas_call(
        body, name="chip_sum_" + name,
        grid_spec=pltpu.PrefetchScalarGridSpec(num_scalar_prefetch=1, grid=(nb,), in_specs=in_specs,
                                               out_specs=pl.BlockSpec((br, cols), lambda i, m: (i, 0))),
        out_shape=jax.ShapeDtypeStruct((rows, cols), F32), compiler_params=_params("arbitrary"),
    )(chip, own, *([slots] * n_slot))


def _half_exchange(pieces):
    def body(*refs):
        ins, outs = refs[0:N_BIG], refs[N_BIG:2 * N_BIG]
        send, recv = refs[2 * N_BIG:]
        x, y, c = _mesh_pos()
        sib = (x, y, 1 - c)
        copies = []
        for w, n in enumerate(BIG_NAMES):
            copies.append(_start_remote(ins[w], outs[w], send.at[w], recv.at[w], sib))
        for cp in copies:
            cp.wait()

    return pl.pallas_call(
        body, name="grad_half_exchange", in_specs=[ANY] * N_BIG, out_specs=[ANY] * N_BIG,
        out_shape=[jax.ShapeDtypeStruct(_cut(n, shard=True, half=True), F32) for n in BIG_NAMES],
        scratch_shapes=[pltpu.SemaphoreType.DMA((N_BIG,)), pltpu.SemaphoreType.DMA((N_BIG,))],
    )(*pieces)


WEIGHT_NAMES = ("norm1_g", "w_in", "q_norm_g", "k_norm_g", "ssm_a_re", "ssm_a_im", "ssm_log_dt", "ssm_b_re", "ssm_b_im",
                "ssm_c_re", "ssm_c_im", "ssm_d", "glu_w", "glu_b", "attn_out_norm_g", "ssm_out_norm_g", "w_out", "norm2_g",
                "w_mlp_up", "w_mlp_down")


def _train_step(a):
    x = a["x"][0]
    target = a["loss_target"][0]
    shards = {n: a[n][0] for n in BIG_NAMES}
    p = {n: a[n][0] for n in SMALL_NAMES}
    core = lax.axis_index("c").astype(jnp.int32).reshape(1)
    chip_id = (2 * lax.axis_index("x") + lax.axis_index("y")).astype(jnp.int32).reshape(1)

    later = ("glu_w", "w_out", "w_mlp_up", "w_mlp_down")
    early = ("w_mlp_up", "w_mlp_down", "w_out", "glu_w")
    late = ("w_in",)
    (wi,) = _gather_weights(shards, ("w_in",))
    chip = {}

    def bwd_side(grads):
        got = _pair_exchange(grads, early)
        for n, g in zip(early, got):
            chip[n] = _pair_sum(n, grads[n], g, core)
        return _ChipExchange(early), [chip[n] for n in early]

    small = {}

    def last_side(cot):
        small["list"] = [cot[n] for n in COT_NAMES]
        packed = _pack(small["list"])
        (got_packed,) = _pair_exchange({}, (), packed)
        small["chip"] = _sum_arrays([packed, got_packed], "pair_sum_small")
        return _ChipExchange((), packed.shape), [small["chip"]]

    loss, grad_x, big, cot, disc_in, early_slots, (small_slots,) = _local_step(
        x, target, wi, None, p, fwd_side=(_Gather(later), [shards[n] for n in later]), bwd_side=bwd_side,
        last_side=last_side)
    slots = dict(zip(early, early_slots))

    got = _pair_exchange(big, late)
    for n, g in zip(late, got):
        chip[n] = _pair_sum(n, big[n], g, core)
    slots.update(zip(late, _chip_exchange([chip[n] for n in late], late)))
    pieces = [_chip_sum(n, chip[n], slots[n], chip_id) for n in BIG_NAMES]
    small_sum = _chip_sum("small", small["chip"], small_slots, chip_id)
    cot_list = small["list"]
    shard_grads = {}
    for n, mine, theirs in zip(BIG_NAMES, pieces, _half_exchange(pieces)):
        h_ax = BIG[n][3]
        shard_grads[n] = jnp.where(core[0] == 0, jnp.concatenate([mine, theirs], axis=h_ax),
                                   jnp.concatenate([theirs, mine], axis=h_ax))
    small_grads = _small_grads(dict(zip(COT_NAMES, _unpack(small_sum, cot_list))), disc_in, p)

    grads, delta, new_m, new_v = {}, {}, {}, {}
    for n in BIG_NAMES:
        grads[n] = shard_grads[n]
        delta[n], new_m[n], new_v[n] = _adamw(a[n][0], grads[n], a["m_" + n][0], a["v_" + n][0], "adamw_" + n)
    swapped = ("ssm_b_re", "ssm_b_im", "ssm_d")

    def flat2(n, t):
        t = jnp.swapaxes(t, -1, -2) if n in swapped else t
        return t.reshape(-1, t.shape[-1])

    def unflat(n, t2):
        shape = p[n].shape
        if n in swapped:
            return jnp.swapaxes(t2.reshape(shape[:-2] + (shape[-1], shape[-2])), -1, -2)
        return t2.reshape(shape)

    res = _adamw_many([flat2(n, p[n]) for n in SMALL_NAMES], [flat2(n, small_grads[n]) for n in SMALL_NAMES],
                      [flat2(n, a["m_" + n][0]) for n in SMALL_NAMES], [flat2(n, a["v_" + n][0]) for n in SMALL_NAMES])
    for store, outs in zip((delta, new_m, new_v), res):
        store.update((n, unflat(n, t2)) for n, t2 in zip(SMALL_NAMES, outs))
    grads.update(small_grads)

    total = lax.psum(loss, ("x", "y", "c"))
    out = [total, grad_x[None]]
    for store in (grads, delta, new_m, new_v):
        out += [store[n].reshape(a[n].shape) for n in WEIGHT_NAMES]
    return tuple(out)


def kernel(x, norm1_g, w_in, q_norm_g, k_norm_g, ssm_a_re, ssm_a_im, ssm_log_dt, ssm_b_re, ssm_b_im, ssm_c_re, ssm_c_im, ssm_d, glu_w, glu_b, attn_out_norm_g, ssm_out_norm_g, w_out, norm2_g, w_mlp_up, w_mlp_down, loss_target, m_norm1_g, m_w_in, m_q_norm_g, m_k_norm_g, m_ssm_a_re, m_ssm_a_im, m_ssm_log_dt, m_ssm_b_re, m_ssm_b_im, m_ssm_c_re, m_ssm_c_im, m_ssm_d, m_glu_w, m_glu_b, m_attn_out_norm_g, m_ssm_out_norm_g, m_w_out, m_norm2_g, m_w_mlp_up, m_w_mlp_down, v_norm1_g, v_w_in, v_q_norm_g, v_k_norm_g, v_ssm_a_re, v_ssm_a_im, v_ssm_log_dt, v_ssm_b_re, v_ssm_b_im, v_ssm_c_re, v_ssm_c_im, v_ssm_d, v_glu_w, v_glu_b, v_attn_out_norm_g, v_ssm_out_norm_g, v_w_out, v_norm2_g, v_w_mlp_up, v_w_mlp_down):
    return _train_step(dict(locals()))
```

```python
import functools
import math

import jax
import jax.numpy as jnp
from jax import lax
from jax.experimental import pallas as pl
from jax.experimental.pallas import tpu as pltpu

F32 = jnp.float32
BF16 = jnp.bfloat16
MESH = pl.DeviceIdType.MESH

D_MODEL = 1024
ATTN_W = 512
SSM_W = 512
HEAD = 64
D_FF = 4096
PROJ_W = 2048
N_GROUPS = 32
N_STATE = 64
GROUP_W = 16
EPS = 1e-6
NEG = -1e30
DILATIONS = (1, 4, 16)
BLK = 128
TILE = 2048
LANES = 128
MXU_WIDTH = 256
N_LB = SSM_W // LANES
VMEM_LIMIT = 56 * 1024 * 1024

ADAM_LR, ADAM_B1, ADAM_B2, ADAM_EPS, ADAM_WD, ADAM_STEP = 0.001, 0.9, 0.999, 1e-08, 0.01, 10


def _params(*sem):
    return pltpu.CompilerParams(dimension_semantics=sem, vmem_limit_bytes=VMEM_LIMIT)


def _nt(a, b):
    return lax.dot_general(a, b, (((1,), (1,)), ((), ())), preferred_element_type=F32)


def _tn(a, b):
    return lax.dot_general(a, b, (((0,), (0,)), ((), ())), preferred_element_type=F32)


def _mm(a, b):
    return jnp.dot(a, b, preferred_element_type=F32)


def _group_mean(t, ones_bd, width):
    span = ones_bd.shape[0]
    hi = t.astype(BF16)
    lo = (t - hi.astype(F32)).astype(BF16)
    parts = [_mm(hi[:, k:k + span], ones_bd) + _mm(lo[:, k:k + span], ones_bd) for k in range(0, t.shape[1], span)]
    return (parts[0] if len(parts) == 1 else jnp.concatenate(parts, axis=1)) * (1.0 / width)


def _rms(x):
    return lax.rsqrt(jnp.mean(x * x, axis=-1, keepdims=True) + EPS)


def _rms_bwd(dy, x, r, g):
    xh = x * r
    dxh = dy * g
    dx = r * (dxh - xh * jnp.mean(dxh * xh, axis=-1, keepdims=True))
    return dx, dy * xh


def _colsum(x):
    return jnp.sum(x, axis=0, keepdims=True)


def _row_block(rows, cap):
    for b in range(min(rows, cap) // 8 * 8, 0, -8):
        if rows % b == 0:
            return b
    raise ValueError(f"no row block for {rows} rows")


def _inproj_fwd(x, g1, wi, gq, gk, ones64):
    t_len = x.shape[0]
    tm = 1024
    n_hp = ATTN_W // LANES

    def body(x_ref, g1_ref, wi_ref, gq_ref, gk_ref, bd_ref, xn_ref, q_ref, k_ref, v_ref, u_ref, qr_ref, kr_ref):
        xv = x_ref[...]
        xn = (xv * _rms(xv) * g1_ref[...]).astype(BF16)
        xn_ref[...] = xn
        proj = _mm(xn, wi_ref[...])
        q = proj[:, 0:ATTN_W]
        k = proj[:, ATTN_W:2 * ATTN_W]
        v = proj[:, 2 * ATTN_W:3 * ATTN_W]
        u_ref[...] = proj[:, 3 * ATTN_W:]
        qr_ref[...] = q
        kr_ref[...] = k
        bd = bd_ref[...]
        qn = q * lax.rsqrt(_group_mean(q * q, bd, HEAD) + EPS) * gq_ref[...] * (HEAD ** -0.5)
        kn = k * lax.rsqrt(_group_mean(k * k, bd, HEAD) + EPS) * gk_ref[...]
        for hp in range(n_hp):
            sl = slice(hp * LANES, (hp + 1) * LANES)
            q_ref[hp] = qn[:, sl]
            k_ref[hp] = kn[:, sl]
            v_ref[hp] = v[:, sl]

    row = lambda i: (i, 0)
    const = lambda i: (0, 0)
    hp_spec = pl.BlockSpec((n_hp, tm, LANES), lambda i: (0, i, 0))
    hp_shape = jax.ShapeDtypeStruct((n_hp, t_len, LANES), F32)
    return pl.pallas_call(
        body, name="inproj_fwd", grid=(t_len // tm,),
        in_specs=[pl.BlockSpec((tm, D_MODEL), row), pl.BlockSpec((1, D_MODEL), const),
                  pl.BlockSpec((D_MODEL, PROJ_W), const), pl.BlockSpec((1, ATTN_W), const),
                  pl.BlockSpec((1, ATTN_W), const), pl.BlockSpec(ones64.shape, const)],
        out_specs=[pl.BlockSpec((tm, D_MODEL), row), hp_spec, hp_spec, hp_spec,
                   pl.BlockSpec((tm, SSM_W), row), pl.BlockSpec((tm, ATTN_W), row), pl.BlockSpec((tm, ATTN_W), row)],
        out_shape=[jax.ShapeDtypeStruct((t_len, D_MODEL), BF16), hp_shape, hp_shape, hp_shape,
                   jax.ShapeDtypeStruct((t_len, SSM_W), F32), jax.ShapeDtypeStruct((t_len, ATTN_W), F32),
                   jax.ShapeDtypeStruct((t_len, ATTN_W), F32)],
        compiler_params=_params("arbitrary"),
    )(x, g1, wi, gq, gk, ones64)


def _attn_masks():
    head0 = lax.broadcasted_iota(jnp.int32, (BLK, LANES), 1) < HEAD
    row = lax.broadcasted_iota(jnp.int32, (2 * BLK, 2 * BLK), 0) & (BLK - 1)
    col = lax.broadcasted_iota(jnp.int32, (2 * BLK, 2 * BLK), 1)
    return head0, (col < BLK) & (col >= row), (col >= BLK) & (col - BLK <= row)


def _stack_heads(x, head0):
    return jnp.concatenate([jnp.where(head0, x, 0.0), jnp.where(head0, 0.0, x)], axis=0).astype(BF16)


def _unit_rows(uidx, d):
    nb = TILE // (BLK * d)
    r = lax.div(uidx, nb)
    b = lax.rem(uidx, nb)
    start = r + d * BLK * b
    if d == 1:
        start = pl.multiple_of(start, BLK)
        mk = lambda s: pl.ds(pl.multiple_of(s, BLK), BLK)
    else:
        mk = lambda s: pl.ds(s, BLK, stride=d)
    return b, mk(start), mk(TILE + start), mk(TILE + start - d * BLK)


def _attn_fwd(q, k, v, side=None, side_args=()):
    n_hp, t_len, _ = q.shape
    nt = t_len // TILE
    ns = side.n if side is not None else 0
    n_steps = n_hp * nt

    def body(*refs):
        q_ref, kp_ref, kc_ref, vp_ref, vc_ref = refs[0:5]
        o_ref, lse_ref = refs[5 + ns:7 + ns]
        kk, vv, m_s, l_s, acc_s = refs[7 + 2 * ns:12 + 2 * ns]
        t = pl.program_id(1)
        step = pl.program_id(0) * nt + t
        if side is not None:
            side.bind(refs[5:5 + ns], refs[7 + ns:7 + 2 * ns], refs[12 + 2 * ns:])
            pl.when(step == 0)(side.start)
            pl.when(step == n_steps // 2)(side.forward)
        kk[0:TILE] = kp_ref[0]
        kk[TILE:] = kc_ref[0]
        vv[0:TILE] = vp_ref[0]
        vv[TILE:] = vc_ref[0]
        head0, band_prev, band_cur = _attn_masks()

        for pi, d in enumerate(DILATIONS):
            def unit(uidx, carry, d=d, pi=pi):
                b, rows_q, rows_c, rows_p = _unit_rows(uidx, d)
                mask = band_cur | (band_prev & ((t > 0) | (b > 0)))
                q2 = _stack_heads(q_ref.at[0][rows_q, :], head0)
                kcat = jnp.concatenate([kk[rows_p, :], kk[rows_c, :]], axis=0).astype(BF16)
                vcat = jnp.concatenate([vv[rows_p, :], vv[rows_c, :]], axis=0).astype(BF16)
                s = jnp.where(mask, _nt(q2, kcat), NEG)
                m = jnp.max(s, axis=1, keepdims=True)
                p = jnp.exp(s - m)
                ls = jnp.sum(p, axis=1, keepdims=True)
                pv = _mm(p.astype(BF16), vcat)
                m_s.at[pi][rows_q, :] = jnp.where(head0, m[0:BLK], m[BLK:])
                l_s.at[pi][rows_q, :] = jnp.where(head0, ls[0:BLK], ls[BLK:])
                acc_s.at[pi][rows_q, :] = jnp.where(head0, pv[0:BLK], pv[BLK:])
                return carry

            lax.fori_loop(0, TILE // BLK, unit, 0, unroll=16)

        m_all = jnp.maximum(jnp.maximum(m_s[0], m_s[1]), m_s[2])
        num = jnp.zeros((TILE, LANES), F32)
        den = jnp.zeros((TILE, LANES), F32)
        for pi in range(len(DILATIONS)):
            wgt = jnp.exp(m_s[pi] - m_all)
            num = num + acc_s[pi] * wgt
            den = den + l_s[pi] * wgt
        o_ref[...] = num / den
        lse_ref[0] = m_all + jnp.log(den)
        if side is not None:
            pl.when(step == n_steps - 1)(side.finish)

    cur = lambda hp, t: (hp, t, 0)
    prev = lambda hp, t: (hp, jnp.maximum(t - 1, 0), 0)
    blk = (1, TILE, LANES)
    per_pattern = pltpu.VMEM((len(DILATIONS), TILE, LANES), F32)
    extra = (side.in_specs(), side.out_specs(), side.out_shapes(), side.scratch_shapes()) if side is not None else ([], [], [], [])
    return pl.pallas_call(
        body, name="attn_fwd", grid=(n_hp, nt),
        in_specs=[pl.BlockSpec(blk, cur), pl.BlockSpec(blk, prev), pl.BlockSpec(blk, cur),
                  pl.BlockSpec(blk, prev), pl.BlockSpec(blk, cur)] + extra[0],
        out_specs=[pl.BlockSpec((TILE, LANES), lambda hp, t: (t, hp)), pl.BlockSpec(blk, cur)] + extra[1],
        out_shape=[jax.ShapeDtypeStruct((t_len, ATTN_W), F32), jax.ShapeDtypeStruct((n_hp, t_len, LANES), F32)] + extra[2],
        scratch_shapes=[pltpu.VMEM((2 * TILE, LANES), F32), pltpu.VMEM((2 * TILE, LANES), F32),
                        per_pattern, per_pattern, per_pattern] + extra[3],
        compiler_params=_params("arbitrary", "arbitrary"),
    )(q, k, k, v, v, *side_args)


def _attn_bwd(q, k, v, o, do, lse, ones_hp, side=None, side_args=()):
    n_hp, t_len, _ = q.shape
    nt = t_len // TILE
    ns = side.n if side is not None else 0
    n_pat = len(DILATIONS)

    def body(*refs):
        q_ref, kp_ref, kc_ref, vp_ref, vc_ref, o_ref, do_ref, lse_ref, bd_ref = refs[0:9]
        dq_ref, dk_ref, dv_ref = refs[9 + ns:12 + ns]
        kk, vv, dq_s, dkc, dkp, dvc, dvp, hold_k, hold_v, dl_s = refs[12 + 2 * ns:22 + 2 * ns]
        t = pl.program_id(1)
        if side is not None:
            side.bind(refs[9:9 + ns], refs[12 + ns:12 + 2 * ns], refs[22 + 2 * ns:])
            pl.when((pl.program_id(0) == 0) & (t == 0))(side.start)

        @pl.when(t < nt)
        def _():
            kk[0:TILE] = kp_ref[0]
            kk[TILE:] = kc_ref[0]
            vv[0:TILE] = vp_ref[0]
            vv[TILE:] = vc_ref[0]
            dl_s[...] = _group_mean(do_ref[...] * o_ref[...], bd_ref[...], 1.0)
            head0, band_prev, band_cur = _attn_masks()

            for pi, d in enumerate(DILATIONS):
                def unit(uidx, carry, d=d, pi=pi):
                    b, rows_q, rows_c, rows_p = _unit_rows(uidx, d)
                    mask = band_cur | (band_prev & ((t > 0) | (b > 0)))
                    q2 = _stack_heads(q_ref.at[0][rows_q, :], head0)
                    do2 = _stack_heads(do_ref[rows_q, :], head0)
                    lse_f = lse_ref.at[0][rows_q, :]
                    dl_f = dl_s[rows_q, :]
                    lse2 = jnp.concatenate([lse_f[:, 0:1], lse_f[:, HEAD:HEAD + 1]], axis=0)
                    dl2 = jnp.concatenate([dl_f[:, 0:1], dl_f[:, HEAD:HEAD + 1]], axis=0)
                    kcat = jnp.concatenate([kk[rows_p, :], kk[rows_c, :]], axis=0).astype(BF16)
                    vcat = jnp.concatenate([vv[rows_p, :], vv[rows_c, :]], axis=0).astype(BF16)
                    p = jnp.where(mask, jnp.exp(_nt(q2, kcat) - lse2), 0.0)
                    ds = (p * (_nt(do2, vcat) - dl2)).astype(BF16)
                    dq2 = _mm(ds, kcat)
                    dq_s.at[pi][rows_q, :] = jnp.where(head0, dq2[0:BLK], dq2[BLK:])
                    dk2 = _tn(ds, q2)
                    dv2 = _tn(p.astype(BF16), do2)
                    dkp.at[pi][rows_q, :] = dk2[0:BLK]
                    dkc.at[pi][rows_q, :] = dk2[BLK:]
                    dvp.at[pi][rows_q, :] = dv2[0:BLK]
                    dvc.at[pi][rows_q, :] = dv2[BLK:]
                    return carry

                lax.fori_loop(0, TILE // BLK, unit, 0, unroll=16)

            dq_ref[...] = dq_s[0] + dq_s[1] + dq_s[2]

        @pl.when(t > 0)
        def _():
            dk_ref[...] = hold_k[...]
            dv_ref[...] = hold_v[...]

        @pl.when((t > 0) & (t < nt))
        def _():
            for pi, d in enumerate(DILATIONS):
                back = d * BLK
                dk_ref[TILE - back:, :] = dk_ref[TILE - back:, :] + dkp[pi, 0:back, :]
                dv_ref[TILE - back:, :] = dv_ref[TILE - back:, :] + dvp[pi, 0:back, :]

        @pl.when(t < nt)
        def _():
            hold_k[...] = dkc[0] + dkc[1] + dkc[2]
            hold_v[...] = dvc[0] + dvc[1] + dvc[2]
            for pi, d in enumerate(DILATIONS):
                back = d * BLK
                if back < TILE:
                    hold_k[0:TILE - back, :] = hold_k[0:TILE - back, :] + dkp[pi, back:, :]
                    hold_v[0:TILE - back, :] = hold_v[0:TILE - back, :] + dvp[pi, back:, :]

        if side is not None:
            pl.when((pl.program_id(0) == n_hp - 1) & (t == nt))(side.finish)

    last = nt - 1
    extra = (side.in_specs(), side.out_specs(), side.out_shapes(), side.scratch_shapes()) if side is not None else ([], [], [], [])
    cur = lambda hp, t: (hp, jnp.minimum(t, last), 0)
    prev = lambda hp, t: (hp, jnp.clip(t - 1, 0, last), 0)
    cur2 = lambda hp, t: (jnp.minimum(t, last), hp)
    prev2 = lambda hp, t: (jnp.maximum(t - 1, 0), hp)
    blk = (1, TILE, LANES)
    blk2 = (TILE, LANES)
    out = jax.ShapeDtypeStruct((t_len, ATTN_W), F32)
    return pl.pallas_call(
        body, name="attn_bwd", grid=(n_hp, nt + 1),
        in_specs=[pl.BlockSpec(blk, cur), pl.BlockSpec(blk, prev), pl.BlockSpec(blk, cur),
                  pl.BlockSpec(blk, prev), pl.BlockSpec(blk, cur), pl.BlockSpec(blk2, cur2),
                  pl.BlockSpec(blk2, cur2), pl.BlockSpec(blk, cur), pl.BlockSpec((LANES, LANES), lambda hp, t: (0, 0))]
        + extra[0],
        out_specs=[pl.BlockSpec(blk2, cur2), pl.BlockSpec(blk2, prev2), pl.BlockSpec(blk2, prev2)] + extra[1],
        out_shape=[out, out, out] + extra[2],
        scratch_shapes=[pltpu.VMEM((2 * TILE, LANES), F32), pltpu.VMEM((2 * TILE, LANES), F32)]
        + [pltpu.VMEM((n_pat, TILE, LANES), F32)] * 5 + [pltpu.VMEM((TILE, LANES), F32)] * 3 + extra[3],
        compiler_params=_params("arbitrary", "arbitrary"),
    )(q, k, k, v, v, o, do, lse, ones_hp, *side_args)


def _discretise(lr, li, ldt, br, bi):
    dt = jnp.exp(ldt)
    mag = jnp.exp(lr * dt)
    ab_r, ab_i = mag * jnp.cos(li * dt), mag * jnp.sin(li * dt)
    den = lr * lr + li * li
    nr, ni = ab_r - 1.0, ab_i
    cr = (nr * lr + ni * li) / den
    ci = (ni * lr - nr * li) / den
    return ab_r, ab_i, cr * br - ci * bi, cr * bi + ci * br


def _disc_fwd(lr, li, ldt, br, bi):
    def body(lr_ref, li_ref, ldt_ref, br_ref, bi_ref, ar_o, ai_o, bbr_o, bbi_o):
        outs = _discretise(lr_ref[...], li_ref[...], ldt_ref[...], br_ref[...], bi_ref[...])
        for o_ref, val in zip((ar_o, ai_o, bbr_o, bbi_o), outs):
            o_ref[...] = val

    col = jax.ShapeDtypeStruct(lr.shape, F32)
    mat = jax.ShapeDtypeStruct(br.shape, F32)
    return pl.pallas_call(body, name="s5_disc_fwd", out_shape=[col, col, mat, mat])(lr, li, ldt, br, bi)


def _disc_bwd(lr, li, ldt, br, bi, d_ar, d_ai, d_bbr, d_bbi, group_sum):
    def body(lr_ref, li_ref, ldt_ref, br_ref, bi_ref, c1, c2, c3, c4, gs_ref, dlr_o, dli_o, dldt_o, dbr_o, dbi_o):
        _, vjp = jax.vjp(_discretise, lr_ref[...], li_ref[...], ldt_ref[...], br_ref[...], bi_ref[...])
        dlr, dli, dldt, dbr, dbi = vjp((c1[...], c2[...], c3[...], c4[...]))
        dlr_o[...] = dlr
        dli_o[...] = dli
        dbr_o[...] = dbr
        dbi_o[...] = dbi
        wide = jnp.broadcast_to(dldt, (dldt.shape[0], LANES))
        dldt_o[...] = jnp.dot(gs_ref[...], wide, precision=lax.Precision.HIGHEST, preferred_element_type=F32)

    col = jax.ShapeDtypeStruct(lr.shape, F32)
    mat = jax.ShapeDtypeStruct(br.shape, F32)
    return pl.pallas_call(
        body, name="s5_disc_bwd", out_shape=[col, col, jax.ShapeDtypeStruct((N_GROUPS, LANES), F32), mat, mat],
    )(lr, li, ldt, br, bi, d_ar, d_ai, d_bbr, d_bbi, group_sum)


N_CHUNK = TILE // BLK
HALF = 4
ROW_PIECE = 256


def _cmul(ar, ai, xr, xi):
    return ar * xr - ai * xi, ar * xi + ai * xr


def _power_table(a_ref, tab, sign, reverse):
    ar = [a_ref[0, j:j + 1, :] for j in range(HALF)]
    ai = [sign * a_ref[0, HALF + j:HALF + j + 1, :] for j in range(HALF)]

    def step(s, cur):
        row = pl.ds((BLK - 1 - s) if reverse else s, 1)
        nxt = []
        for j in range(HALF):
            tab.at[j][row, :] = cur[j]
            tab.at[HALF + j][row, :] = cur[HALF + j]
            nxt.append(_cmul(ar[j], ai[j], cur[j], cur[HALF + j]))
        return tuple(p[0] for p in nxt) + tuple(p[1] for p in nxt)

    lax.fori_loop(0, BLK, step, tuple(ar) + tuple(ai))


def _interleave(src, dst):
    for c in range(N_CHUNK):
        dst[pl.ds(c, BLK, stride=N_CHUNK), :] = src[c * BLK:(c + 1) * BLK, :]


def _deinterleave(src, dst):
    for c in range(N_CHUNK):
        dst[c * BLK:(c + 1) * BLK, :] = src[pl.ds(c, BLK, stride=N_CHUNK), :]


def _step_rows(s):
    return pl.ds(pl.multiple_of(s * N_CHUNK, N_CHUNK), N_CHUNK)


def _chunk_scan(buf, a_ref, sign, reverse):
    ar = [jnp.broadcast_to(a_ref[0, j:j + 1, :], (N_CHUNK, LANES)) for j in range(HALF)]
    ai = [sign * jnp.broadcast_to(a_ref[0, HALF + j:HALF + j + 1, :], (N_CHUNK, LANES)) for j in range(HALF)]

    def step(i, carry):
        s = (BLK - 1 - i) if reverse else i
        rows = _step_rows(s)
        out = []
        for j in range(HALF):
            pr, pi = _cmul(ar[j], ai[j], carry[j], carry[HALF + j])
            xr = buf.at[j][rows, :] + pr
            xi = buf.at[HALF + j][rows, :] + pi
            buf.at[j][rows, :] = xr
            buf.at[HALF + j][rows, :] = xi
            out.append((xr, xi))
        return tuple(p[0] for p in out) + tuple(p[1] for p in out)

    zero = jnp.zeros((N_CHUNK, LANES), F32)
    lax.fori_loop(0, BLK, step, (zero,) * (2 * HALF), unroll=2)


def _chunk_states(buf, carry_s, xin_s, tab, reverse):
    edge = 0 if reverse else BLK - 1
    top = 0 if reverse else BLK - 1
    pw = [tab[j, top:top + 1, :] for j in range(2 * HALF)]
    cur = [carry_s[j:j + 1, :] for j in range(2 * HALF)]
    summary = [buf[j, edge * N_CHUNK:(edge + 1) * N_CHUNK, :] for j in range(2 * HALF)]
    order = range(N_CHUNK - 1, -1, -1) if reverse else range(N_CHUNK)
    for c in order:
        for j in range(2 * HALF):
            xin_s[j, c:c + 1, :] = cur[j]
        nxt = []
        for j in range(HALF):
            pr, pi = _cmul(pw[j], pw[HALF + j], cur[j], cur[HALF + j])
            nxt.append((pr + summary[j][c:c + 1, :], pi + summary[HALF + j][c:c + 1, :]))
        cur = [p[0] for p in nxt] + [p[1] for p in nxt]
    for j in range(2 * HALF):
        carry_s[j:j + 1, :] = cur[j]


def _s5_fwd(u, a_cat, b_mat, c_mat, d_skip):
    t_len = u.shape[0]
    nt = t_len // TILE

    def body(u_ref, a_ref, b_ref, c_ref, d_ref, y_ref, x_ref, xs, us, tab, carry_s, xin_s):
        sb = pl.program_id(1)

        @pl.when(sb == 0)
        def _():
            _power_table(a_ref, tab, 1.0, False)
            carry_s[...] = jnp.zeros_like(carry_s)

        _interleave(u_ref, us)
        for r in range(0, TILE, ROW_PIECE):
            bu = _mm(us[r:r + ROW_PIECE, :].astype(BF16), b_ref[0])
            for j in range(2 * HALF):
                xs[j, r:r + ROW_PIECE, :] = bu[:, j * LANES:(j + 1) * LANES]
        _chunk_scan(xs, a_ref, 1.0, False)
        _chunk_states(xs, carry_s, xin_s, tab, False)
        xin = [xin_s[j] for j in range(2 * HALF)]

        def fix(s, acc):
            rows = _step_rows(s)
            for j in range(HALF):
                pr, pi = _cmul(tab.at[j][pl.ds(s, 1), :], tab.at[HALF + j][pl.ds(s, 1), :], xin[j], xin[HALF + j])
                xs.at[j][rows, :] = xs.at[j][rows, :] + pr
                xs.at[HALF + j][rows, :] = xs.at[HALF + j][rows, :] + pi
            return acc

        lax.fori_loop(0, BLK, fix, 0, unroll=2)
        for r in range(0, TILE, ROW_PIECE):
            rows = slice(r, r + ROW_PIECE)
            xcat = jnp.concatenate([xs[j, rows, :].astype(BF16) for j in range(2 * HALF)], axis=1)
            x_ref[0, rows, :] = xcat
            us[rows, :] = d_ref[0] * us[rows, :] + _mm(xcat, c_ref[0])
        _deinterleave(us, y_ref)

    return pl.pallas_call(
        body, name="s5_fwd", grid=(N_LB, nt),
        in_specs=[pl.BlockSpec((TILE, LANES), lambda lb, sb: (sb, lb)),
                  pl.BlockSpec((1, 2 * HALF, LANES), lambda lb, sb: (lb, 0, 0)),
                  pl.BlockSpec((1, LANES, 2 * HALF * LANES), lambda lb, sb: (lb, 0, 0)),
                  pl.BlockSpec((1, 2 * HALF * LANES, LANES), lambda lb, sb: (lb, 0, 0)),
                  pl.BlockSpec((1, 1, LANES), lambda lb, sb: (lb, 0, 0))],
        out_specs=[pl.BlockSpec((TILE, LANES), lambda lb, sb: (sb, lb)),
                   pl.BlockSpec((1, TILE, 2 * HALF * LANES), lambda lb, sb: (lb, sb, 0))],
        out_shape=[jax.ShapeDtypeStruct((t_len, SSM_W), F32), jax.ShapeDtypeStruct((N_LB, t_len, 2 * HALF * LANES), BF16)],
        scratch_shapes=[pltpu.VMEM((2 * HALF, TILE, LANES), F32), pltpu.VMEM((TILE, LANES), F32),
                        pltpu.VMEM((2 * HALF, BLK, LANES), F32),
                        pltpu.VMEM((2 * HALF, LANES), F32), pltpu.VMEM((2 * HALF, N_CHUNK, LANES), F32)],
        compiler_params=_params("arbitrary", "arbitrary"),
    )(u, a_cat, b_mat, c_mat, d_skip)


def _s5_bwd(u, dy, states, a_cat, b_mat, c_mat, d_skip):
    t_len = u.shape[0]
    nt = t_len // TILE
    last = nt - 1

    def body(u_ref, dy_ref, x_ref, a_ref, b_ref, c_ref, d_ref, du_ref, db_ref, dc_ref, da_ref, dd_ref,
             gs, us, dys, tabc, lam_s, lin_s, lamb_s, dus):
        sb = pl.program_id(1)

        @pl.when(sb == 0)
        def _():
            _power_table(a_ref, tabc, -1.0, True)
            lam_s[...] = jnp.zeros_like(lam_s)
            db_ref[...] = jnp.zeros_like(db_ref)
            dc_ref[...] = jnp.zeros_like(dc_ref)
            da_ref[...] = jnp.zeros_like(da_ref)
            dd_ref[...] = jnp.zeros_like(dd_ref)

        _interleave(u_ref, us)
        _interleave(dy_ref, dys)
        for r in range(0, TILE, ROW_PIECE):
            gy = _nt(dys[r:r + ROW_PIECE, :].astype(BF16), c_ref[0])
            for j in range(2 * HALF):
                gs[j, r:r + ROW_PIECE, :] = gy[:, j * LANES:(j + 1) * LANES]
        _chunk_scan(gs, a_ref, -1.0, True)
        _chunk_states(gs, lam_s, lin_s, tabc, True)
        zero = jnp.zeros((N_CHUNK, LANES), F32)
        x_tile = x_ref.at[0]
        for grp in range(0, HALF, 2):
            slabs = (grp, grp + 1)
            lin = [(lin_s[j], lin_s[HALF + j]) for j in slabs]

            def fix(i, carry, slabs=slabs, lin=lin):
                s = BLK - 1 - i
                rows = _step_rows(s)
                out = []
                for k, j in enumerate(slabs):
                    nr, ni, acc_r, acc_i = carry[4 * k:4 * k + 4]
                    xr = x_tile[rows, pl.ds(j * LANES, LANES)].astype(F32)
                    xi = x_tile[rows, pl.ds((HALF + j) * LANES, LANES)].astype(F32)
                    qr, qi = _cmul(tabc.at[j][pl.ds(s, 1), :], tabc.at[HALF + j][pl.ds(s, 1), :], lin[k][0], lin[k][1])
                    lr_ = gs.at[j][rows, :] + qr
                    li_ = gs.at[HALF + j][rows, :] + qi
                    gs.at[j][rows, :] = lr_
                    gs.at[HALF + j][rows, :] = li_
                    out += [lr_, li_, acc_r + (xr * nr + xi * ni), acc_i + (xr * ni - xi * nr)]
                return tuple(out)

            init = []
            for k in range(len(slabs)):
                init += [lin[k][0], lin[k][1], zero, zero]
            res = lax.fori_loop(0, BLK, fix, tuple(init), unroll=2)
            for k, j in enumerate(slabs):
                da_ref[0, j:j + 1, :] = da_ref[0, j:j + 1, :] + _colsum(res[4 * k + 2])
                da_ref[0, HALF + j:HALF + j + 1, :] = da_ref[0, HALF + j:HALF + j + 1, :] + _colsum(res[4 * k + 3])
        dd = jnp.zeros((1, LANES), F32)
        for r in range(0, TILE, ROW_PIECE):
            rows = slice(r, r + ROW_PIECE)
            lam = jnp.concatenate([gs[j, rows, :].astype(BF16) for j in range(2 * HALF)], axis=1)
            lamb_s[rows, :] = lam
            dyv = dys[rows, :]
            dd = dd + _colsum(dyv * us[rows, :])
            dus[rows, :] = _nt(lam, b_ref[0]) + d_ref[0] * dyv
        _deinterleave(dus, du_ref)
        db_ref[0] = db_ref[0] + _tn(us[...].astype(BF16), lamb_s[...])
        dc_ref[0] = dc_ref[0] + _tn(dys[...].astype(BF16), x_ref[0])
        dd_ref[0] = dd_ref[0] + dd

    rev = lambda lb, sb: (last - sb, lb)
    per_lb = lambda lb, sb: (lb, 0, 0)
    wide = 2 * HALF * LANES
    return pl.pallas_call(
        body, name="s5_bwd", grid=(N_LB, nt),
        in_specs=[pl.BlockSpec((TILE, LANES), rev), pl.BlockSpec((TILE, LANES), rev),
                  pl.BlockSpec((1, TILE, wide), lambda lb, sb: (lb, last - sb, 0)),
                  pl.BlockSpec((1, 2 * HALF, LANES), per_lb), pl.BlockSpec((1, LANES, wide), per_lb),
                  pl.BlockSpec((1, wide, LANES), per_lb), pl.BlockSpec((1, 1, LANES), per_lb)],
        out_specs=[pl.BlockSpec((TILE, LANES), rev), pl.BlockSpec((1, LANES, wide), per_lb),
                   pl.BlockSpec((1, LANES, wide), per_lb), pl.BlockSpec((1, 2 * HALF, LANES), per_lb),
                   pl.BlockSpec((1, 1, LANES), per_lb)],
        out_shape=[jax.ShapeDtypeStruct((t_len, SSM_W), F32), jax.ShapeDtypeStruct((N_LB, LANES, wide), F32),
                   jax.ShapeDtypeStruct((N_LB, LANES, wide), F32), jax.ShapeDtypeStruct((N_LB, 2 * HALF, LANES), F32),
                   jax.ShapeDtypeStruct((N_LB, 1, LANES), F32)],
        scratch_shapes=[pltpu.VMEM((2 * HALF, TILE, LANES), F32),
                        pltpu.VMEM((TILE, LANES), F32), pltpu.VMEM((TILE, LANES), F32),
                        pltpu.VMEM((2 * HALF, BLK, LANES), F32), pltpu.VMEM((2 * HALF, LANES), F32),
                        pltpu.VMEM((2 * HALF, N_CHUNK, LANES), F32),
                        pltpu.VMEM((TILE, 2 * HALF * LANES), BF16), pltpu.VMEM((TILE, LANES), F32)],
        compiler_params=_params("arbitrary", "arbitrary"),
    )(u, dy, states, a_cat, b_mat, c_mat, d_skip)


_GELU_C = math.sqrt(2.0 / math.pi)
_GELU_K = 0.044715


def _gelu(y):
    t = jnp.tanh(_GELU_C * (y + _GELU_K * (y * y * y)))
    return y * (0.5 * (1.0 + t)), t


def _gelu_grad(y, t):
    return 0.5 * (1.0 + t) + 0.5 * y * (1.0 - t * t) * (_GELU_C * (1.0 + 3.0 * _GELU_K * y * y))


def _glu(y, wg, bias):
    z, t = _gelu(y)
    sg = jax.nn.sigmoid(_mm(z.astype(BF16), wg) + bias)
    return z, t, sg


def _mixer_mlp(attn, y, x, target, wg, glu_b, ga, gs, wo, g2, wu, wd):
    t_len = x.shape[0]
    tm = 256
    fc = 1024
    n_fc = D_FF // fc

    def body(attn_ref, y_ref, x_ref, tg_ref, b_ref, ga_ref, gs_ref, g2_ref, wg_s, wo_s, wu_hbm, wd_hbm,
             dx2_ref, hdn_ref, dup_ref, h_ref, dyb_ref, mix_ref, z_ref, dattn_ref, dy_ref, dx2b_ref, dgp_ref,
             dg2_ref, loss_ref, dga_ref, dgs_ref, db_ref, wu_s, wd_s, relu_s, glu_s, sem):
        @pl.when(pl.program_id(0) == 0)
        def _():
            copies = [pltpu.make_async_copy(src, dst, sem.at[k]) for k, (src, dst) in enumerate(((wu_hbm, wu_s), (wd_hbm, wd_s)))]
            for cp in copies:
                cp.start()
            for cp in copies:
                cp.wait()
            for acc in (dg2_ref, loss_ref, dga_ref, dgs_ref, db_ref):
                acc[...] = jnp.zeros_like(acc)

        av = attn_ref[...]
        z, t, sg = _glu(y_ref[...], wg_s[...], b_ref[...])
        glu_s[0] = z
        glu_s[1] = t
        glu_s[2] = sg
        z_ref[...] = z.astype(BF16)
        s = z * sg
        an = (av * _rms(av) * ga_ref[...]).astype(BF16)
        sn = (s * _rms(s) * gs_ref[...]).astype(BF16)
        mix_ref[:, 0:ATTN_W] = an
        mix_ref[:, ATTN_W:] = sn
        dx2_ref[...] = x_ref[...] + _mm(an, wo_s[0:ATTN_W, :]) + _mm(sn, wo_s[ATTN_W:, :])
        r = _rms(dx2_ref[...])
        g2v = g2_ref[...]
        h = (dx2_ref[...] * r * g2v).astype(BF16)
        h_ref[...] = h
        yout = dx2_ref[...]
        for c in range(n_fc):
            cols = slice(c * fc, (c + 1) * fc)
            ru = jnp.maximum(_mm(h, wu_s[:, cols]), 0.0)
            relu_s[:, cols] = ru
            hd = (ru * ru).astype(BF16)
            hdn_ref[:, cols] = hd
            yout = yout + _mm(hd, wd_s[cols, :])
        err = yout - tg_ref[...]
        loss_ref[...] = loss_ref[...] + 0.5 * jnp.sum(err * err) * (1.0 / D_MODEL)
        dy = err * (1.0 / D_MODEL)
        dyb = dy.astype(BF16)
        dyb_ref[...] = dyb
        dh = jnp.zeros((tm, D_MODEL), F32)
        for c in range(n_fc):
            cols = slice(c * fc, (c + 1) * fc)
            dup = (_nt(dyb, wd_s[cols, :]) * (2.0 * relu_s[:, cols])).astype(BF16)
            dup_ref[:, cols] = dup
            dh = dh + _nt(dup, wu_s[:, cols])
        dxn, g2_term = _rms_bwd(dh, dx2_ref[...], r, g2v)
        dx2_ref[...] = dy + dxn
        dg2_ref[...] = dg2_ref[...] + _colsum(g2_term)
        dx2b = dx2_ref[...].astype(BF16)
        dx2b_ref[...] = dx2b
        yv = y_ref[...]
        av = attn_ref[...]
        z, t, sg = glu_s[0], glu_s[1], glu_s[2]
        s = z * sg
        d_attn, ga_term = _rms_bwd(_nt(dx2b, wo_s[0:ATTN_W, :]), av, _rms(av), ga_ref[...])
        d_s, gs_term = _rms_bwd(_nt(dx2b, wo_s[ATTN_W:, :]), s, _rms(s), gs_ref[...])
        dattn_ref[...] = d_attn
        dgp = d_s * z * sg * (1.0 - sg)
        dgpb = dgp.astype(BF16)
        dgp_ref[...] = dgpb
        dy_ref[...] = (d_s * sg + _nt(dgpb, wg_s[...])) * _gelu_grad(yv, t)
        dga_ref[...] = dga_ref[...] + _colsum(ga_term)
        dgs_ref[...] = dgs_ref[...] + _colsum(gs_term)
        db_ref[...] = db_ref[...] + _colsum(dgp)

    row = lambda i: (i, 0)
    const = lambda i: (0, 0)
    wide = lambda n: pl.BlockSpec((tm, n), row)
    vec = lambda n: pl.BlockSpec((1, n), const)
    any_spec = pl.BlockSpec(memory_space=pl.ANY)
    f32 = lambda n: jax.ShapeDtypeStruct((t_len, n), F32)
    b16 = lambda n: jax.ShapeDtypeStruct((t_len, n), BF16)
    acc = lambda n: jax.ShapeDtypeStruct((1, n), F32)
    return pl.pallas_call(
        body, name="mixer_mlp", grid=(t_len // tm,),
        in_specs=[wide(ATTN_W), wide(SSM_W), wide(D_MODEL), wide(D_MODEL), vec(SSM_W), vec(ATTN_W), vec(SSM_W), vec(D_MODEL),
                  pl.BlockSpec((SSM_W, SSM_W), const), pl.BlockSpec((D_MODEL, D_MODEL), const), any_spec, any_spec],
        out_specs=[wide(D_MODEL), wide(D_FF), wide(D_FF), wide(D_MODEL), wide(D_MODEL), wide(D_MODEL), wide(SSM_W),
                   wide(ATTN_W), wide(SSM_W), wide(D_MODEL), wide(SSM_W),
                   vec(D_MODEL), vec(LANES), vec(ATTN_W), vec(SSM_W), vec(SSM_W)],
        out_shape=[f32(D_MODEL), b16(D_FF), b16(D_FF), b16(D_MODEL), b16(D_MODEL), b16(D_MODEL), b16(SSM_W),
                   f32(ATTN_W), f32(SSM_W), b16(D_MODEL), b16(SSM_W),
                   acc(D_MODEL), acc(LANES), acc(ATTN_W), acc(SSM_W), acc(SSM_W)],
        scratch_shapes=[pltpu.VMEM((D_MODEL, D_FF), BF16), pltpu.VMEM((D_FF, D_MODEL), BF16),
                        pltpu.VMEM((tm, D_FF), F32), pltpu.VMEM((3, tm, SSM_W), F32), pltpu.SemaphoreType.DMA((2,))],
        compiler_params=_params("arbitrary"),
    )(attn, y, x, target, glu_b, ga, gs, g2, wg, wo, wu, wd)


def _inproj_bwd(dqs, dkn, dv, du, q_raw, k_raw, x, dx2, wi, g1, gq, gk, ones64):
    t_len = x.shape[0]
    tm = 512
    n_heads = ATTN_W // HEAD

    def body(dqs_ref, dkn_ref, dv_ref, du_ref, q_ref, k_ref, x_ref, dx2_ref, wi_ref, g1_ref, gq_ref, gk_ref, bd_ref,
             gx_ref, dproj_ref, dg1_ref, dgq_ref, dgk_ref, accq, acck):
        i = pl.program_id(0)

        @pl.when(i == 0)
        def _():
            dg1_ref[...] = jnp.zeros_like(dg1_ref)
            accq[...] = jnp.zeros_like(accq)
            acck[...] = jnp.zeros_like(acck)

        bd = bd_ref[...]

        def head_norm_bwd(dy, raw, gain, acc):
            r = lax.rsqrt(_group_mean(raw * raw, bd, HEAD) + EPS)
            xh = raw * r
            dxh = dy * gain
            acc[...] = acc[...] + _colsum(dy * xh)
            return r * (dxh - xh * _group_mean(dxh * xh, bd, HEAD))

        dq = head_norm_bwd(dqs_ref[...] * (HEAD ** -0.5), q_ref[...], gq_ref[...], accq)
        dk = head_norm_bwd(dkn_ref[...], k_ref[...], gk_ref[...], acck)
        dproj_ref[:, 0:ATTN_W] = dq.astype(BF16)
        dproj_ref[:, ATTN_W:2 * ATTN_W] = dk.astype(BF16)
        dproj_ref[:, 2 * ATTN_W:3 * ATTN_W] = dv_ref[...].astype(BF16)
        dproj_ref[:, 3 * ATTN_W:] = du_ref[...].astype(BF16)
        dxn = _nt(dproj_ref[...], wi_ref[...])
        xv = x_ref[...]
        g1v = g1_ref[...]
        dx, g1_term = _rms_bwd(dxn, xv, _rms(xv), g1v)
        gx_ref[...] = dx2_ref[...] + dx
        dg1_ref[...] = dg1_ref[...] + _colsum(g1_term)

        @pl.when(i == pl.num_programs(0) - 1)
        def _():
            for acc, out in ((accq, dgq_ref), (acck, dgk_ref)):
                tot = acc[:, 0:HEAD]
                for h in range(1, n_heads):
                    tot = tot + acc[:, h * HEAD:(h + 1) * HEAD]
                out[...] = tot

    row = lambda i: (i, 0)
    const = lambda i: (0, 0)
    aw = pl.BlockSpec((tm, ATTN_W), row)
    dm = pl.BlockSpec((tm, D_MODEL), row)
    return pl.pallas_call(
        body, name="inproj_bwd", grid=(t_len // tm,),
        in_specs=[aw, aw, aw, aw, aw, aw, dm, dm, pl.BlockSpec((D_MODEL, PROJ_W), const), pl.BlockSpec((1, D_MODEL), const),
                  pl.BlockSpec((1, ATTN_W), const), pl.BlockSpec((1, ATTN_W), const), pl.BlockSpec(ones64.shape, const)],
        out_specs=[dm, pl.BlockSpec((tm, PROJ_W), row), pl.BlockSpec((1, D_MODEL), const),
                   pl.BlockSpec((1, HEAD), const), pl.BlockSpec((1, HEAD), const)],
        out_shape=[jax.ShapeDtypeStruct((t_len, D_MODEL), F32), jax.ShapeDtypeStruct((t_len, PROJ_W), BF16),
                   jax.ShapeDtypeStruct((1, D_MODEL), F32), jax.ShapeDtypeStruct((1, HEAD), F32),
                   jax.ShapeDtypeStruct((1, HEAD), F32)],
        scratch_shapes=[pltpu.VMEM((1, ATTN_W), F32), pltpu.VMEM((1, ATTN_W), F32)],
        compiler_params=_params("arbitrary"),
    )(dqs, dkn, dv, du, q_raw, k_raw, x, dx2, wi, g1, gq, gk, ones64)


def _grad_matmul(a, b, name, side=None, side_args=()):
    t_len, m = a.shape
    n = b.shape[1]
    bm, bn, bt = min(m, 1024), min(n, 1024), min(t_len, 4096)
    grid = (m // bm, n // bn, t_len // bt)
    ns = side.n if side is not None else 0

    def body(*refs):
        a_ref, b_ref, o_ref = refs[0], refs[1], refs[2 + ns]
        ids = [pl.program_id(k) for k in range(3)]
        if side is not None:
            side.bind(refs[2:2 + ns], refs[3 + ns:3 + 2 * ns], refs[3 + 2 * ns:])
            pl.when((ids[0] == 0) & (ids[1] == 0) & (ids[2] == 0))(side.start)

        @pl.when(ids[2] == 0)
        def _():
            o_ref[...] = jnp.zeros_like(o_ref)

        o_ref[...] = o_ref[...] + _tn(a_ref[...], b_ref[...])
        if side is not None:
            pl.when((ids[0] == grid[0] - 1) & (ids[1] == grid[1] - 1) & (ids[2] == grid[2] - 1))(side.finish)

    extra = (side.in_specs(), side.out_specs(), side.out_shapes(), side.scratch_shapes()) if side is not None else ([], [], [], [])
    out = pl.pallas_call(
        body, name=name, grid=grid,
        in_specs=[pl.BlockSpec((bt, bm), lambda i, j, k: (k, i)), pl.BlockSpec((bt, bn), lambda i, j, k: (k, j))] + extra[0],
        out_specs=[pl.BlockSpec((bm, bn), lambda i, j, k: (i, j))] + extra[1],
        out_shape=[jax.ShapeDtypeStruct((m, n), F32)] + extra[2],
        scratch_shapes=extra[3],
        compiler_params=_params("arbitrary", "arbitrary", "arbitrary"),
    )(a, b, *side_args)
    return out[0] if side is None else out


def _adamw_update(w_ref, g_ref, m_ref, v_ref, d_o, m_o, v_o):
    gv = g_ref[...]
    mn = ADAM_B1 * m_ref[...] + (1.0 - ADAM_B1) * gv
    vn = ADAM_B2 * v_ref[...] + (1.0 - ADAM_B2) * jnp.square(gv)
    m_hat = mn / (1.0 - ADAM_B1 ** ADAM_STEP)
    v_hat = vn / (1.0 - ADAM_B2 ** ADAM_STEP)
    d_o[...] = -ADAM_LR * (m_hat / (jnp.sqrt(v_hat) + ADAM_EPS) + ADAM_WD * w_ref[...])
    m_o[...] = mn
    v_o[...] = vn


def _adamw_many(ws, gs, ms, vs):
    n = len(ws)

    def body(*refs):
        for i in range(n):
            _adamw_update(*[refs[k * n + i] for k in range(7)])

    shapes = [jax.ShapeDtypeStruct(w.shape, F32) for w in ws]
    outs = pl.pallas_call(body, name="adamw_small", out_shape=shapes * 3,
                          compiler_params=pltpu.CompilerParams(vmem_limit_bytes=VMEM_LIMIT))(*ws, *gs, *ms, *vs)
    return outs[0:n], outs[n:2 * n], outs[2 * n:]


def _adamw(w, g, m, v, name):
    rows, cols = w.shape
    br = _row_block(rows, 256)
    body = functools.partial(_adamw_update)

    spec = pl.BlockSpec((br, cols), lambda i: (i, 0))
    shape = jax.ShapeDtypeStruct((rows, cols), F32)
    return pl.pallas_call(
        body, name=name, grid=(rows // br,), in_specs=[spec] * 4, out_specs=[spec] * 3, out_shape=[shape] * 3,
        compiler_params=_params("arbitrary"),
    )(w, g, m, v)


def _sum_arrays(arrs, name, out_dtype=F32):
    rows, cols = arrs[0].shape
    br = _row_block(rows, 512)
    n = len(arrs)

    def body(*refs):
        tot = refs[0][...]
        for r in refs[1:n]:
            tot = tot + r[...]
        refs[n][...] = tot.astype(out_dtype)

    spec = pl.BlockSpec((br, cols), lambda i: (i, 0))
    return pl.pallas_call(
        body, name=name, grid=(rows // br,), in_specs=[spec] * n, out_specs=spec,
        out_shape=jax.ShapeDtypeStruct((rows, cols), out_dtype), compiler_params=_params("arbitrary"),
    )(*arrs)


GPL = N_GROUPS // N_LB
SW = GPL * N_STATE


def _eye_groups():
    return jnp.eye(GPL, dtype=F32)


def _s5_matrices(ab_r, ab_i, bb_r, bb_i, c_re, c_im, d_skip):
    eye = _eye_groups()
    a_cat = jnp.concatenate([ab_r.reshape(N_LB, HALF, LANES), ab_i.reshape(N_LB, HALF, LANES)], axis=1)

    def b_part(bb):
        b4 = jnp.transpose(bb.reshape(N_LB, GPL, N_STATE, GROUP_W), (0, 1, 3, 2))
        return (b4[:, :, :, None, :] * eye[None, :, None, :, None]).reshape(N_LB, LANES, SW)

    def c_part(cc):
        c4 = jnp.transpose(cc.reshape(N_LB, GPL, GROUP_W, N_STATE), (0, 1, 3, 2))
        return (c4[:, :, :, None, :] * eye[None, :, None, :, None]).reshape(N_LB, SW, LANES)

    b_mat = jnp.concatenate([b_part(bb_r), b_part(bb_i)], axis=2).astype(BF16)
    c_mat = jnp.concatenate([c_part(c_re), -c_part(c_im)], axis=1).astype(BF16)
    return a_cat, b_mat, c_mat, d_skip.reshape(N_LB, 1, LANES)


def _s5_unpack_grads(db, dc, da, dd):
    eye = _eye_groups()
    mask = eye[None, :, None, None, :, None]
    d6 = jnp.sum(db.reshape(N_LB, GPL, GROUP_W, 2, GPL, N_STATE) * mask, axis=4)
    dbb = jnp.transpose(d6, (3, 0, 1, 4, 2)).reshape(2, N_GROUPS * N_STATE, GROUP_W)
    c6 = jnp.sum(dc.reshape(N_LB, GPL, GROUP_W, 2, GPL, N_STATE) * mask, axis=4)
    dcc = jnp.transpose(c6, (3, 0, 1, 2, 4)).reshape(2, N_GROUPS, GROUP_W, N_STATE)
    dab_r = da[:, :HALF].reshape(N_GROUPS * N_STATE, 1)
    dab_i = da[:, HALF:].reshape(N_GROUPS * N_STATE, 1)
    return dab_r, dab_i, dbb[0], dbb[1], dcc[0], -dcc[1], dd.reshape(N_GROUPS, GROUP_W)


def _block_ones(n, width):
    i = lax.broadcasted_iota(jnp.int32, (n, n), 0) // width
    j = lax.broadcasted_iota(jnp.int32, (n, n), 1) // width
    return (i == j).astype(BF16)


def _tile_heads(g):
    return jnp.tile(g.reshape(1, HEAD), (1, ATTN_W // HEAD))


def _local_step(x, target, wi, rest, p, fwd_side=None, bwd_side=None, last_side=None):
    ones64 = _block_ones(MXU_WIDTH, HEAD)
    ones_hp = _block_ones(LANES, HEAD)
    g1 = p["norm1_g"].reshape(1, D_MODEL)
    g2 = p["norm2_g"].reshape(1, D_MODEL)
    gq = _tile_heads(p["q_norm_g"])
    gk = _tile_heads(p["k_norm_g"])
    ga = p["attn_out_norm_g"].reshape(1, ATTN_W)
    gs = p["ssm_out_norm_g"].reshape(1, SSM_W)
    glu_b = p["glu_b"].reshape(1, SSM_W)
    n_gp = N_GROUPS * N_STATE
    lr = p["ssm_a_re"].reshape(n_gp, 1)
    li = p["ssm_a_im"].reshape(n_gp, 1)
    ldt = jnp.repeat(p["ssm_log_dt"].reshape(N_GROUPS), N_STATE).reshape(n_gp, 1)
    br = p["ssm_b_re"].reshape(n_gp, GROUP_W)
    bi = p["ssm_b_im"].reshape(n_gp, GROUP_W)
    ab_r, ab_i, bb_r, bb_i = _disc_fwd(lr, li, ldt, br, bi)
    a_cat, b_mat, c_mat, d_mat = _s5_matrices(
        ab_r, ab_i, bb_r, bb_i, p["ssm_c_re"].reshape(N_GROUPS, GROUP_W, N_STATE),
        p["ssm_c_im"].reshape(N_GROUPS, GROUP_W, N_STATE), p["ssm_d"])

    xn, qn, kn, vv, u, q_raw, k_raw = _inproj_fwd(x, g1, wi, gq, gk, ones64)
    if fwd_side is None:
        attn, lse = _attn_fwd(qn, kn, vv)
    else:
        attn, lse, *rest = _attn_fwd(qn, kn, vv, *fwd_side)
    wg, wo, wu, wd = rest
    y, states = _s5_fwd(u, a_cat, b_mat, c_mat, d_mat)
    (dx2, hdn, dup, h, dyb, mix, z, d_attn, dy_ssm, dx2b, dgp, dg2, loss, dga, dgs, dglu_b) = _mixer_mlp(
        attn, y, x, target, wg, glu_b, ga, gs, wo, g2, wu, wd)
    big = {"w_mlp_up": _grad_matmul(h, dup, "grad_w_mlp_up"), "w_mlp_down": _grad_matmul(hdn, dyb, "grad_w_mlp_down"),
           "w_out": _grad_matmul(mix, dx2b, "grad_w_out"), "glu_w": _grad_matmul(z, dgp, "grad_glu_w")}
    rode = []
    if bwd_side is None:
        dqs, dkn, dvv = _attn_bwd(qn, kn, vv, attn, d_attn, lse, ones_hp)
    else:
        dqs, dkn, dvv, *rode = _attn_bwd(qn, kn, vv, attn, d_attn, lse, ones_hp, *bwd_side(big))
    du, db, dc, da, dd = _s5_bwd(u, dy_ssm, states, a_cat, b_mat, c_mat, d_mat)
    grad_x, dproj, dg1, dgq, dgk = _inproj_bwd(dqs, dkn, dvv, du, q_raw, k_raw, x, dx2, wi, g1, gq, gk, ones64)

    dab_r, dab_i, dbb_r, dbb_i, dc_re, dc_im, dd_g = _s5_unpack_grads(db, dc, da, dd)
    cot = {"norm1_g": dg1, "q_norm_g": dgq, "k_norm_g": dgk, "ab_r": dab_r, "ab_i": dab_i, "bb_r": dbb_r, "bb_i": dbb_i,
           "ssm_c_re": dc_re, "ssm_c_im": dc_im, "ssm_d": dd_g, "glu_b": dglu_b, "attn_out_norm_g": dga,
           "ssm_out_norm_g": dgs, "norm2_g": dg2}
    rode_last = []
    if last_side is None:
        big["w_in"] = _grad_matmul(xn, dproj, "grad_w_in")
    else:
        big["w_in"], *rode_last = _grad_matmul(xn, dproj, "grad_w_in", *last_side(cot))
    return loss[0, 0], grad_x, big, cot, (lr, li, ldt, br, bi), rode, rode_last


COT_NAMES = ("norm1_g", "q_norm_g", "k_norm_g", "ab_r", "ab_i", "bb_r", "bb_i", "ssm_c_re", "ssm_c_im", "ssm_d",
             "glu_b", "attn_out_norm_g", "ssm_out_norm_g", "norm2_g")
SMALL_NAMES = ("norm1_g", "q_norm_g", "k_norm_g", "ssm_a_re", "ssm_a_im", "ssm_log_dt", "ssm_b_re", "ssm_b_im",
               "ssm_c_re", "ssm_c_im", "ssm_d", "glu_b", "attn_out_norm_g", "ssm_out_norm_g", "norm2_g")
BIG_NAMES = ("w_in", "glu_w", "w_out", "w_mlp_up", "w_mlp_down")
PACK_ROWS = 1152


def _pack(arrs):
    flat = jnp.concatenate([a.reshape(-1) for a in arrs])
    return jnp.pad(flat, (0, PACK_ROWS * LANES - flat.shape[0])).reshape(PACK_ROWS, LANES)


def _unpack(packed, like):
    flat = packed.reshape(-1)
    out, pos = [], 0
    for a in like:
        out.append(flat[pos:pos + a.size].reshape(a.shape))
        pos += a.size
    return out


def _small_grads(cot, disc_in, p):
    lr, li, ldt, br, bi = disc_in
    group_sum = (lax.broadcasted_iota(jnp.int32, (N_GROUPS, N_GROUPS * N_STATE), 1) // N_STATE
                 == lax.broadcasted_iota(jnp.int32, (N_GROUPS, N_GROUPS * N_STATE), 0)).astype(F32)
    dlr, dli, dldt, dbr, dbi = _disc_bwd(lr, li, ldt, br, bi, cot["ab_r"], cot["ab_i"], cot["bb_r"], cot["bb_i"], group_sum)
    g = dict(cot)
    g.update(ssm_a_re=dlr, ssm_a_im=dli, ssm_log_dt=dldt[:, 0], ssm_b_re=dbr, ssm_b_im=dbi)
    return {n: g[n].reshape(p[n].shape) for n in SMALL_NAMES}


BIG = {
    "w_in": ((D_MODEL, PROJ_W), 1, PROJ_W // 4, 0, D_MODEL // 2),
    "glu_w": ((SSM_W, SSM_W), 0, SSM_W // 4, 1, SSM_W // 2),
    "w_out": ((D_MODEL, D_MODEL), 0, D_MODEL // 4, 1, D_MODEL // 2),
    "w_mlp_up": ((D_MODEL, D_FF), 1, D_FF // 4, 0, D_MODEL // 2),
    "w_mlp_down": ((D_FF, D_MODEL), 0, D_FF // 4, 1, D_MODEL // 2),
}
N_BIG = len(BIG_NAMES)
N_CHIPS = 4
ANY = pl.BlockSpec(memory_space=pl.ANY)


def _cut(name, shard=False, half=False):
    shape, s_ax, s_sz, h_ax, h_sz = BIG[name]
    shape = list(shape)
    if shard:
        shape[s_ax] = s_sz
    if half:
        shape[h_ax] = h_sz
    return tuple(shape)


def _window(name, base, shard=None, half=None):
    _, s_ax, s_sz, h_ax, h_sz = BIG[name]
    idx = [pl.ds(0, base[0]), pl.ds(0, base[1])]
    if shard is not None:
        idx[s_ax] = pl.ds(pl.multiple_of(shard * s_sz, s_sz), s_sz)
    if half is not None:
        idx[h_ax] = pl.ds(pl.multiple_of(half * h_sz, h_sz), h_sz)
    return tuple(idx)


def _mesh_pos():
    return lax.axis_index("x"), lax.axis_index("y"), lax.axis_index("c")


def _other_chips(x, y):
    return [(1 - x, y, 2 * (1 - x) + y), (x, 1 - y, 2 * x + 1 - y), (1 - x, 1 - y, 2 * (1 - x) + 1 - y)]


def _remote(src, dst, send_sem, recv_sem, dev):
    return pltpu.make_async_remote_copy(src_ref=src, dst_ref=dst, send_sem=send_sem, recv_sem=recv_sem,
                                        device_id=dev, device_id_type=MESH)


def _start_remote(src, dst, send_sem, recv_sem, dev):
    cp = _remote(src, dst, send_sem, recv_sem, dev)
    cp.start()
    return cp


class _Gather:
    def __init__(self, names):
        self.names = tuple(names)
        self.n = len(self.names)

    def in_specs(self):
        return [pl.BlockSpec(memory_space=pltpu.VMEM)] * self.n

    def out_specs(self):
        return [ANY] * self.n

    def out_shapes(self):
        return [jax.ShapeDtypeStruct(BIG[w][0], BF16) for w in self.names]

    def scratch_shapes(self):
        n_sem = (N_CHIPS - 1) * self.n
        return ([pltpu.VMEM(_cut(w, shard=True), BF16) for w in self.names]
                + [pltpu.SemaphoreType.DMA((n_sem,))] * 4 + [pltpu.SemaphoreType.DMA((self.n,))])

    def bind(self, ins, outs, scratch):
        self.ins, self.outs = ins, outs
        self.stage = scratch[:self.n]
        self.send, self.recv, self.fsend, self.frecv, self.lsem = scratch[self.n:]

    def _copies(self):
        x, y, c = _mesh_pos()
        me = 2 * x + y
        sib = (x, y, 1 - c)
        local, sends, lands, fwds, flands = [], [], [], [], []
        for w, n in enumerate(self.names):
            local.append(pltpu.make_async_copy(self.stage[w], self.outs[w].at[_window(n, BIG[n][0], shard=me)], self.lsem.at[w]))
        for k, (px, py, pj) in enumerate(_other_chips(x, y)):
            for w, n in enumerate(self.names):
                s = k * self.n + w
                sends.append(_remote(self.stage[w].at[_window(n, _cut(n, shard=True), half=c)],
                                     self.outs[w].at[_window(n, BIG[n][0], shard=me, half=c)],
                                     self.send.at[s], self.recv.at[s], (px, py, c)))
                got = self.outs[w].at[_window(n, BIG[n][0], shard=pj, half=c)]
                lands.append(_remote(got, got, self.send.at[s], self.recv.at[s], (px, py, c)))
                fwds.append(_remote(got, got, self.fsend.at[s], self.frecv.at[s], sib))
                theirs = self.outs[w].at[_window(n, BIG[n][0], shard=pj, half=1 - c)]
                flands.append(_remote(theirs, theirs, self.fsend.at[s], self.frecv.at[s], sib))
        return local, sends, lands, fwds, flands

    def start(self):
        for w in range(self.n):
            self.stage[w][...] = self.ins[w][...].astype(BF16)
        local, sends, _, _, _ = self._copies()
        for cp in local + sends:
            cp.start()

    def forward(self):
        _, _, lands, fwds, _ = self._copies()
        for land, fwd in zip(lands, fwds):
            land.wait_recv()
            fwd.start()

    def finish(self):
        local, sends, _, fwds, flands = self._copies()
        for cp in flands:
            cp.wait_recv()
        for cp in sends + fwds:
            cp.wait_send()
        for cp in local:
            cp.wait()


def _gather_weights(shards, names):
    g = _Gather(names)

    def body(*refs):
        g.bind(refs[0:g.n], refs[g.n:2 * g.n], refs[2 * g.n:])
        g.start()
        g.forward()
        g.finish()

    return pl.pallas_call(
        body, name="gather_" + "_".join(names), in_specs=g.in_specs(), out_specs=g.out_specs(), out_shape=g.out_shapes(),
        scratch_shapes=g.scratch_shapes(), compiler_params=pltpu.CompilerParams(vmem_limit_bytes=VMEM_LIMIT),
    )(*[shards[n] for n in names])


def _pair_exchange(grads, names, packed=None):
    n_big = len(names)
    n_all = n_big + (packed is not None)

    def body(*refs):
        ins, got = refs[0:n_all], refs[n_all:2 * n_all]
        send, recv = refs[2 * n_all:]
        x, y, c = _mesh_pos()
        sib = (x, y, 1 - c)
        copies = []
        for w, n in enumerate(names):
            copies.append(_start_remote(ins[w].at[_window(n, BIG[n][0], half=1 - c)], got[w], send.at[w], recv.at[w], sib))
        if packed is not None:
            copies.append(_start_remote(ins[n_big], got[n_big], send.at[n_big], recv.at[n_big], sib))
        for cp in copies:
            cp.wait()

    shapes = [jax.ShapeDtypeStruct(_cut(n, half=True), F32) for n in names]
    args = [grads[n] for n in names]
    if packed is not None:
        shapes.append(jax.ShapeDtypeStruct(packed.shape, F32))
        args.append(packed)
    return pl.pallas_call(
        body, name="grad_pair_exchange_" + ("_".join(names) or "small"), in_specs=[ANY] * n_all, out_specs=[ANY] * n_all,
        out_shape=shapes,
        scratch_shapes=[pltpu.SemaphoreType.DMA((n_all,)), pltpu.SemaphoreType.DMA((n_all,))],
    )(*args)


def _pair_sum(name, full, got, core):
    _, _, _, h_ax, _ = BIG[name]
    rows, cols = _cut(name, half=True)
    br = _row_block(rows, 512)
    nb = rows // br
    own_map = (lambda i, c: (i + c[0] * nb, 0)) if h_ax == 0 else (lambda i, c: (i, c[0]))

    def body(c_ref, own_ref, got_ref, o_ref):
        o_ref[...] = (own_ref[...] + got_ref[...]).astype(BF16)

    plain = pl.BlockSpec((br, cols), lambda i, c: (i, 0))
    return pl.pallas_call(
        body, name="pair_sum_" + name,
        grid_spec=pltpu.PrefetchScalarGridSpec(num_scalar_prefetch=1, grid=(nb,),
                                               in_specs=[pl.BlockSpec((br, cols), own_map), plain], out_specs=plain),
        out_shape=jax.ShapeDtypeStruct((rows, cols), BF16), compiler_params=_params("arbitrary"),
    )(core, full, got)


class _ChipExchange:
    def __init__(self, names, packed_shape=None):
        self.names = tuple(names)
        self.packed_shape = packed_shape
        self.n = len(self.names) + (packed_shape is not None)

    def in_specs(self):
        return [ANY] * self.n

    def out_specs(self):
        return [ANY] * self.n

    def out_shapes(self):
        shapes = [jax.ShapeDtypeStruct((N_CHIPS,) + _cut(w, shard=True, half=True), BF16) for w in self.names]
        if self.packed_shape is not None:
            shapes.append(jax.ShapeDtypeStruct((N_CHIPS,) + tuple(self.packed_shape), F32))
        return shapes

    def scratch_shapes(self):
        n_sem = (N_CHIPS - 1) * self.n
        return [pltpu.SemaphoreType.DMA((n_sem,)), pltpu.SemaphoreType.DMA((n_sem,))]

    def bind(self, ins, outs, scratch):
        self.ins, self.outs = ins, outs
        self.send, self.recv = scratch

    def _piece(self, w, shard):
        if w >= len(self.names):
            return self.ins[w]
        n = self.names[w]
        return self.ins[w].at[_window(n, _cut(n, half=True), shard=shard)]

    def _copies(self):
        x, y, c = _mesh_pos()
        me = 2 * x + y
        sends, lands = [], []
        for k, (px, py, pj) in enumerate(_other_chips(x, y)):
            for w in range(self.n):
                s = k * self.n + w
                sends.append(_remote(self._piece(w, pj), self.outs[w].at[me], self.send.at[s], self.recv.at[s], (px, py, c)))
                lands.append(_remote(self._piece(w, me), self.outs[w].at[pj], self.send.at[s], self.recv.at[s], (px, py, c)))
        return sends, lands

    def start(self):
        for cp in self._copies()[0]:
            cp.start()

    def finish(self):
        sends, lands = self._copies()
        for cp in lands:
            cp.wait_recv()
        for cp in sends:
            cp.wait_send()


def _chip_exchange(halves, names):
    ex = _ChipExchange(names)

    def body(*refs):
        ex.bind(refs[0:ex.n], refs[ex.n:2 * ex.n], refs[2 * ex.n:])
        ex.start()
        ex.finish()

    return pl.pallas_call(
        body, name="grad_chip_exchange", in_specs=ex.in_specs(), out_specs=ex.out_specs(), out_shape=ex.out_shapes(),
        scratch_shapes=ex.scratch_shapes(),
    )(*halves)


def _chip_sum(name, own, slots, chip):
    n_slot, rows, cols = slots.shape
    br = _row_block(rows, 512)
    nb = rows // br
    if name in BIG and BIG[name][1] == 1:
        own_map = lambda i, m: (i, m[0])
    elif name in BIG:
        own_map = lambda i, m: (i + m[0] * nb, 0)
    else:
        own_map = lambda i, m: (i, 0)

    def slot_map(j):
        return lambda i, m: (jnp.where(m[0] == j, (j + 1) % n_slot, j), i, 0)

    def body(m_ref, own_ref, *refs):
        own_blk = own_ref[...].astype(F32)
        tot = None
        for j in range(n_slot):
            term = jnp.where(m_ref[0] == j, own_blk, refs[j][...].astype(F32))
            tot = term if tot is None else tot + term
        refs[n_slot][...] = tot

    in_specs = [pl.BlockSpec((br, cols), own_map)] + [pl.BlockSpec((None, br, cols), slot_map(j)) for j in range(n_slot)]
    return pl.pallas_call(
        body, name="chip_sum_" + name,
        grid_spec=pltpu.PrefetchScalarGridSpec(num_scalar_prefetch=1, grid=(nb,), in_specs=in_specs,
                                               out_specs=pl.BlockSpec((br, cols), lambda i, m: (i, 0))),
        out_shape=jax.ShapeDtypeStruct((rows, cols), F32), compiler_params=_params("arbitrary"),
    )(chip, own, *([slots] * n_slot))


def _half_exchange(pieces):
    def body(*refs):
        ins, outs = refs[0:N_BIG], refs[N_BIG:2 * N_BIG]
        send, recv = refs[2 * N_BIG:]
        x, y, c = _mesh_pos()
        sib = (x, y, 1 - c)
        copies = []
        for w, n in enumerate(BIG_NAMES):
            copies.append(_start_remote(ins[w], outs[w], send.at[w], recv.at[w], sib))
        for cp in copies:
            cp.wait()

    return pl.pallas_call(
        body, name="grad_half_exchange", in_specs=[ANY] * N_BIG, out_specs=[ANY] * N_BIG,
        out_shape=[jax.ShapeDtypeStruct(_cut(n, shard=True, half=True), F32) for n in BIG_NAMES],
        scratch_shapes=[pltpu.SemaphoreType.DMA((N_BIG,)), pltpu.SemaphoreType.DMA((N_BIG,))],
    )(*pieces)


WEIGHT_NAMES = ("norm1_g", "w_in", "q_norm_g", "k_norm_g", "ssm_a_re", "ssm_a_im", "ssm_log_dt", "ssm_b_re", "ssm_b_im",
                "ssm_c_re", "ssm_c_im", "ssm_d", "glu_w", "glu_b", "attn_out_norm_g", "ssm_out_norm_g", "w_out", "norm2_g",
                "w_mlp_up", "w_mlp_down")


def _train_step(a):
    x = a["x"][0]
    target = a["loss_target"][0]
    shards = {n: a[n][0] for n in BIG_NAMES}
    p = {n: a[n][0] for n in SMALL_NAMES}
    core = lax.axis_index("c").astype(jnp.int32).reshape(1)
    chip_id = (2 * lax.axis_index("x") + lax.axis_index("y")).astype(jnp.int32).reshape(1)

    later = ("glu_w", "w_out", "w_mlp_up", "w_mlp_down")
    early = ("w_mlp_up", "w_mlp_down", "w_out", "glu_w")
    late = ("w_in",)
    (wi,) = _gather_weights(shards, ("w_in",))
    chip = {}

    def bwd_side(grads):
        got = _pair_exchange(grads, early)
        for n, g in zip(early, got):
            chip[n] = _pair_sum(n, grads[n], g, core)
        return _ChipExchange(early), [chip[n] for n in early]

    small = {}

    def last_side(cot):
        small["list"] = [cot[n] for n in COT_NAMES]
        packed = _pack(small["list"])
        (got_packed,) = _pair_exchange({}, (), packed)
        small["chip"] = _sum_arrays([packed, got_packed], "pair_sum_small")
        return _ChipExchange((), packed.shape), [small["chip"]]

    loss, grad_x, big, cot, disc_in, early_slots, (small_slots,) = _local_step(
        x, target, wi, None, p, fwd_side=(_Gather(later), [shards[n] for n in later]), bwd_side=bwd_side,
        last_side=last_side)
    slots = dict(zip(early, early_slots))

    got = _pair_exchange(big, late)
    for n, g in zip(late, got):
        chip[n] = _pair_sum(n, big[n], g, core)
    slots.update(zip(late, _chip_exchange([chip[n] for n in late], late)))
    pieces = [_chip_sum(n, chip[n], slots[n], chip_id) for n in BIG_NAMES]
    small_sum = _chip_sum("small", small["chip"], small_slots, chip_id)
    cot_list = small["list"]
    shard_grads = {}
    for n, mine, theirs in zip(BIG_NAMES, pieces, _half_exchange(pieces)):
        h_ax = BIG[n][3]
        shard_grads[n] = jnp.where(core[0] == 0, jnp.concatenate([mine, theirs], axis=h_ax),
                                   jnp.concatenate([theirs, mine], axis=h_ax))
    small_grads = _small_grads(dict(zip(COT_NAMES, _unpack(small_sum, cot_list))), disc_in, p)

    grads, delta, new_m, new_v = {}, {}, {}, {}
    for n in BIG_NAMES:
        grads[n] = shard_grads[n]
        delta[n], new_m[n], new_v[n] = _adamw(a[n][0], grads[n], a["m_" + n][0], a["v_" + n][0], "adamw_" + n)
    swapped = ("ssm_b_re", "ssm_b_im", "ssm_d")

    def flat2(n, t):
        t = jnp.swapaxes(t, -1, -2) if n in swapped else t
        return t.reshape(-1, t.shape[-1])

    def unflat(n, t2):
        shape = p[n].shape
        if n in swapped:
            return jnp.swapaxes(t2.reshape(shape[:-2] + (shape[-1], shape[-2])), -1, -2)
        return t2.reshape(shape)

    res = _adamw_many([flat2(n, p[n]) for n in SMALL_NAMES], [flat2(n, small_grads[n]) for n in SMALL_NAMES],
                      [flat2(n, a["m_" + n][0]) for n in SMALL_NAMES], [flat2(n, a["v_" + n][0]) for n in SMALL_NAMES])
    for store, outs in zip((delta, new_m, new_v), res):
        store.update((n, unflat(n, t2)) for n, t2 in zip(SMALL_NAMES, outs))
    grads.update(small_grads)

    total = lax.psum(loss, ("x", "y", "c"))
    out = [total, grad_x[None]]
    for store in (grads, delta, new_m, new_v):
        out += [store[n].reshape(a[n].shape) for n in WEIGHT_NAMES]
    return tuple(out)


def kernel(x, norm1_g, w_in, q_norm_g, k_norm_g, ssm_a_re, ssm_a_im, ssm_log_dt, ssm_b_re, ssm_b_im, ssm_c_re, ssm_c_im, ssm_d, glu_w, glu_b, attn_out_norm_g, ssm_out_norm_g, w_out, norm2_g, w_mlp_up, w_mlp_down, loss_target, m_norm1_g, m_w_in, m_q_norm_g, m_k_norm_g, m_ssm_a_re, m_ssm_a_im, m_ssm_log_dt, m_ssm_b_re, m_ssm_b_im, m_ssm_c_re, m_ssm_c_im, m_ssm_d, m_glu_w, m_glu_b, m_attn_out_norm_g, m_ssm_out_norm_g, m_w_out, m_norm2_g, m_w_mlp_up, m_w_mlp_down, v_norm1_g, v_w_in, v_q_norm_g, v_k_norm_g, v_ssm_a_re, v_ssm_a_im, v_ssm_log_dt, v_ssm_b_re, v_ssm_b_im, v_ssm_c_re, v_ssm_c_im, v_ssm_d, v_glu_w, v_glu_b, v_attn_out_norm_g, v_ssm_out_norm_g, v_w_out, v_norm2_g, v_w_mlp_up, v_w_mlp_down):
    return _train_step(dict(locals()))
```
